```python
import math
import jax, jax.numpy as jnp
from jax import lax
import numpy as np

D_MODEL = 2048
BATCH = 8
SEQ = 4096
DEPTH = 2

N_META = 16
CHUNK = 128
SUB = 16
EPS = 1e-6

RET_HEADS = 8
RET_DK = 128
RET_DV = 256
RET_QK = RET_HEADS * RET_DK
RET_W = RET_HEADS * RET_DV
ROPE_BASE = 10000.0

S5_W = 1024
S5_GH = 16
S5_G = S5_W // S5_GH
S5_P = 64
DT_MIN = 1e-3
DT_MAX = 1e-1

GLA_HEADS = 4
GLA_DK = 256
GLA_DV = 512
GLA_QK = GLA_HEADS * GLA_DK
GLA_W = GLA_HEADS * GLA_DV
GLA_RANK = 16
GLA_TAU = 16.0

IN_AB = 2 * RET_QK + 2 * RET_W + 2 * S5_W
OUT_AB = RET_W + S5_W
IN_C = 2 * GLA_QK + 2 * GLA_W + GLA_RANK
N_EVEN = (DEPTH + 1) // 2
N_ODD = DEPTH // 2

kernel_name = "hybrid_retention_s5_gla_meta"


def _rmsnorm(x, w):
    xf = x.astype(jnp.float32)
    y = xf * lax.rsqrt(jnp.mean(xf * xf, axis=-1, keepdims=True) + EPS)
    return (y * w.astype(jnp.float32)).astype(x.dtype)


def _head_rmsnorm(o, w):
    y = o * lax.rsqrt(jnp.mean(o * o, axis=-1, keepdims=True) + EPS)
    b, l, hh, d = o.shape
    return y.reshape(b, l, hh * d) * w.astype(jnp.float32)


def _rope(t, cos, sin):
    half = t.shape[-1] // 2
    t1, t2 = t[..., :half], t[..., half:]
    c, s = cos[None, :, None, :], sin[None, :, None, :]
    return jnp.concatenate([t1 * c - t2 * s, t1 * s + t2 * c], axis=-1)


def _to_chunks(t):
    t = jnp.pad(t, ((0, 0), (CHUNK - N_META, 0), (0, 0), (0, 0)))
    b, lp, hh, d = t.shape
    return t.reshape(b, lp // CHUNK, CHUNK, hh, d).transpose(0, 3, 1, 2, 4)


def _from_chunks(t):
    b, hh, n, c, d = t.shape
    return t.transpose(0, 2, 3, 1, 4).reshape(b, n * c, hh, d)[:, CHUNK - N_META:]


def _retention(q, k, v):
    qc, kc, vc = _to_chunks(q), _to_chunks(k), _to_chunks(v)
    bsz = qc.shape[0]
    log_g = jnp.log1p(-jnp.exp2(-5.0 - jnp.arange(RET_HEADS, dtype=jnp.float32)))
    idx = jnp.arange(CHUNK, dtype=jnp.float32)
    diff = idx[:, None] - idx[None, :]
    causal = diff >= 0
    decay = jnp.where(causal, jnp.exp(log_g[:, None, None] * jnp.maximum(diff, 0.0)), 0.0)
    scores = jnp.einsum('bhnid,bhnjd->bhnij', qc, kc) * decay[None, :, None]
    o_intra = jnp.einsum('bhnij,bhnjv->bhniv', scores, vc)
    k_w = kc * jnp.exp(log_g[:, None] * (CHUNK - 1 - idx))[None, :, None, :, None]
    kv = jnp.einsum('bhnjd,bhnjv->nbhdv', k_w, vc)
    g_chunk = jnp.exp(log_g * CHUNK)[None, :, None, None]

    def step(s, kv_n):
        return s * g_chunk + kv_n, s

    s0 = jnp.zeros((bsz, RET_HEADS, RET_DK, RET_DV), jnp.float32)
    _, s_prev = lax.scan(step, s0, kv)
    q_w = qc * jnp.exp(log_g[:, None] * (idx + 1.0))[None, :, None, :, None]
    o_inter = jnp.einsum('bhnid,nbhdv->bhniv', q_w, s_prev)
    return _from_chunks(o_intra + o_inter)


def _s5(u, lam_re, lam_im, log_dt, b_re, b_im, c_re, c_im, d, w_glu):
    bsz, L, _ = u.shape
    lam_re = lam_re.astype(jnp.float32); lam_im = lam_im.astype(jnp.float32)
    dt = jnp.exp(log_dt.astype(jnp.float32))[:, None]
    mag = jnp.exp(lam_re * dt)
    ab_re, ab_im = mag * jnp.cos(lam_im * dt), mag * jnp.sin(lam_im * dt)
    den = lam_re * lam_re + lam_im * lam_im
    nr, ni = ab_re - 1.0, ab_im
    f_re = (nr * lam_re + ni * lam_im) / den
    f_im = (ni * lam_re - nr * lam_im) / den
    b_re = b_re.astype(jnp.float32); b_im = b_im.astype(jnp.float32)
    bb_re = f_re[..., None] * b_re - f_im[..., None] * b_im
    bb_im = f_re[..., None] * b_im + f_im[..., None] * b_re
    ug = u.reshape(bsz, L, S5_G, S5_GH)
    bu_re = jnp.einsum('blgh,gph->lbgp', ug, bb_re)
    bu_im = jnp.einsum('blgh,gph->lbgp', ug, bb_im)
    a_re = jnp.broadcast_to(ab_re, bu_re.shape)
    a_im = jnp.broadcast_to(ab_im, bu_im.shape)

    def combine(e1, e2):
        a1r, a1i, b1r, b1i = e1
        a2r, a2i, b2r, b2i = e2
        return (a2r * a1r - a2i * a1i,
                a2r * a1i + a2i * a1r,
                a2r * b1r - a2i * b1i + b2r,
                a2r * b1i + a2i * b1r + b2i)

    _, _, x_re, x_im = lax.associative_scan(combine, (a_re, a_im, bu_re, bu_im), axis=0)
    y = (jnp.einsum('lbgp,ghp->blgh', x_re, c_re.astype(jnp.float32))
         - jnp.einsum('lbgp,ghp->blgh', x_im, c_im.astype(jnp.float32)))
    y = y.reshape(bsz, L, S5_W) + d.astype(jnp.float32) * u
    y = jax.nn.gelu(y)
    return y * jax.nn.sigmoid(y @ w_glu.astype(jnp.float32))


def _gla(q, k, v, log_a):
    qc, kc, vc, gc = _to_chunks(q), _to_chunks(k), _to_chunks(v), _to_chunks(log_a)
    bsz, hh, n, c, dk = qc.shape
    dv = vc.shape[-1]
    nsub = CHUNK // SUB
    b = jnp.cumsum(gc, axis=3)
    b_last = b[:, :, :, -1]
    kv = jnp.einsum('bhnjd,bhnjv->nbhdv', kc * jnp.exp(b_last[:, :, :, None] - b), vc)
    dec = jnp.exp(b_last).transpose(2, 0, 1, 3)

    def step(s, inp):
        kv_n, dec_n = inp
        return s * dec_n[..., None] + kv_n, s

    s0 = jnp.zeros((bsz, hh, dk, dv), jnp.float32)
    _, s_prev = lax.scan(step, s0, (kv, dec))
    o_inter = jnp.einsum('bhnid,nbhdv->bhniv', qc * jnp.exp(b), s_prev)
    bs = b.reshape(bsz, hh, n, nsub, SUB, dk)
    qs = qc.reshape(bsz, hh, n, nsub, SUB, dk)
    ks = kc.reshape(bsz, hh, n, nsub, SUB, dk)
    vs = vc.reshape(bsz, hh, n, nsub, SUB, dv)
    b_ref = jnp.concatenate([jnp.zeros_like(bs[:, :, :, :1, 0]), bs[:, :, :, :-1, -1]], axis=3)
    q_hat = qs * jnp.exp(bs - b_ref[:, :, :, :, None])
    j_pos = jnp.arange(CHUNK)
    before = j_pos[None, :] < (jnp.arange(nsub) * SUB)[:, None]
    expo = jnp.where(before[:, :, None], b_ref[:, :, :, :, None] - b[:, :, :, None], -jnp.inf)
    k_hat = kc[:, :, :, None] * jnp.exp(expo)
    s_cross = jnp.einsum('bhnsid,bhnsjd->bhnsij', q_hat, k_hat)
    o_cross = jnp.einsum('bhnsij,bhnjv->bhnsiv', s_cross, vc)
    tri = jnp.arange(SUB)[:, None] >= jnp.arange(SUB)[None, :]
    expo_d = jnp.where(tri[:, :, None], bs[:, :, :, :, :, None] - bs[:, :, :, :, None], -jnp.inf)
    s_diag = jnp.einsum('bhnsid,bhnsjd,bhnsijd->bhnsij', qs, ks, jnp.exp(expo_d))
    o_diag = jnp.einsum('bhnsij,bhnsjv->bhnsiv', s_diag, vs)
    o_intra = (o_cross + o_diag).reshape(bsz, hh, n, c, dv)
    return _from_chunks(o_intra + o_inter)


def _mixer_ab(h, w_in, ret_norm_w, lam_re, lam_im, log_dt, b_re, b_im, c_re, c_im, d, w_glu, w_out, cos, sin):
    bsz, L, _ = h.shape
    proj = (h @ w_in).astype(jnp.float32)
    q, k, v, z_a, u, z_b = jnp.split(
        proj, [RET_QK, 2 * RET_QK, 2 * RET_QK + RET_W, 2 * RET_QK + 2 * RET_W,
               2 * RET_QK + 2 * RET_W + S5_W], axis=-1)
    q = _rope(q.reshape(bsz, L, RET_HEADS, RET_DK), cos, sin)
    k = _rope(k.reshape(bsz, L, RET_HEADS, RET_DK), cos, sin) * (RET_DK ** -0.5)
    v = v.reshape(bsz, L, RET_HEADS, RET_DV)
    o_a = _head_rmsnorm(_retention(q, k, v), ret_norm_w) * jax.nn.silu(z_a)
    o_b = _s5(u, lam_re, lam_im, log_dt, b_re, b_im, c_re, c_im, d, w_glu) * jax.nn.silu(z_b)
    return jnp.concatenate([o_a, o_b], axis=-1).astype(h.dtype) @ w_out


def _mixer_c(h, w_in, w_gate, b_gate, norm_w, w_out):
    bsz, L, _ = h.shape
    proj = (h @ w_in).astype(jnp.float32)
    q, k, v, z, g_low = jnp.split(
        proj, [GLA_QK, 2 * GLA_QK, 2 * GLA_QK + GLA_W, 2 * GLA_QK + 2 * GLA_W], axis=-1)
    log_a = jax.nn.log_sigmoid(g_low @ w_gate.astype(jnp.float32) + b_gate.astype(jnp.float32)) / GLA_TAU
    o = _gla(q.reshape(bsz, L, GLA_HEADS, GLA_DK) * (GLA_DK ** -0.5),
             k.reshape(bsz, L, GLA_HEADS, GLA_DK),
             v.reshape(bsz, L, GLA_HEADS, GLA_DV),
             log_a.reshape(bsz, L, GLA_HEADS, GLA_DK))
    o = _head_rmsnorm(o, norm_w) * jax.nn.silu(z)
    return o.astype(h.dtype) @ w_out


def _fwd_setup_inputs(seed: int = 0) -> dict:
    key = jax.random.key(seed)
    ks = jax.random.split(key, 24)
    f32 = jnp.float32

    def nrm(k, shape, scale):
        return jax.random.normal(k, shape, f32) * scale

    return {
        "x": nrm(ks[0], (BATCH, SEQ, D_MODEL), 1.0),
        "meta": nrm(ks[1], (N_META, D_MODEL), 1.0),
        "norm_ab_w": 1.0 + nrm(ks[2], (N_EVEN, D_MODEL), 0.02),
        "w_in_ab": nrm(ks[3], (N_EVEN, D_MODEL, IN_AB), D_MODEL ** -0.5),
        "ret_norm_w": 1.0 + nrm(ks[4], (N_EVEN, RET_W), 0.02),
        "s5_lam_re": -0.5 + nrm(ks[5], (N_EVEN, S5_G, S5_P), 0.01),
        "s5_lam_im": math.pi * jnp.arange(S5_P, dtype=f32) + nrm(ks[6], (N_EVEN, S5_G, S5_P), 0.01),
        "s5_log_dt": jax.random.uniform(ks[7], (N_EVEN, S5_G), f32, math.log(DT_MIN), math.log(DT_MAX)),
        "s5_b_re": nrm(ks[8], (N_EVEN, S5_G, S5_P, S5_GH), (2 * S5_GH) ** -0.5),
        "s5_b_im": nrm(ks[9], (N_EVEN, S5_G, S5_P, S5_GH), (2 * S5_GH) ** -0.5),
        "s5_c_re": nrm(ks[10], (N_EVEN, S5_G, S5_GH, S5_P), S5_P ** -0.5),
        "s5_c_im": nrm(ks[11], (N_EVEN, S5_G, S5_GH, S5_P), S5_P ** -0.5),
        "s5_d": nrm(ks[12], (N_EVEN, S5_W), 1.0),
        "s5_w_glu": nrm(ks[13], (N_EVEN, S5_W, S5_W), S5_W ** -0.5),
        "w_out_ab": nrm(ks[14], (N_EVEN, OUT_AB, D_MODEL), OUT_AB ** -0.5),
        "norm_c_w": 1.0 + nrm(ks[15], (N_ODD, D_MODEL), 0.02),
        "w_in_c": nrm(ks[16], (N_ODD, D_MODEL, IN_C), D_MODEL ** -0.5),
        "gla_w_gate": nrm(ks[17], (N_ODD, GLA_RANK, GLA_QK), GLA_RANK ** -0.5),
        "gla_b_gate": nrm(ks[18], (N_ODD, GLA_QK), 0.1),
        "gla_norm_w": 1.0 + nrm(ks[19], (N_ODD, GLA_W), 0.02),
        "w_out_c": nrm(ks[20], (N_ODD, GLA_W, D_MODEL), GLA_W ** -0.5),
        "final_norm_w": 1.0 + nrm(ks[21], (D_MODEL,), 0.02),
    }


def _fwd_reference(x, meta, norm_ab_w, w_in_ab, ret_norm_w, s5_lam_re, s5_lam_im, s5_log_dt,
              s5_b_re, s5_b_im, s5_c_re, s5_c_im, s5_d, s5_w_glu, w_out_ab,
              norm_c_w, w_in_c, gla_w_gate, gla_b_gate, gla_norm_w, w_out_c, final_norm_w):
    bsz = x.shape[0]
    h = jnp.concatenate(
        [jnp.broadcast_to(meta.astype(x.dtype)[None], (bsz, N_META, D_MODEL)), x], axis=1)
    L = h.shape[1]
    pos = jnp.arange(L, dtype=jnp.float32)
    inv_freq = jnp.power(ROPE_BASE, -jnp.arange(0, RET_DK, 2, dtype=jnp.float32) / RET_DK)
    ang = pos[:, None] * inv_freq[None, :]
    cos, sin = jnp.cos(ang), jnp.sin(ang)
    for layer in range(DEPTH):
        i = layer // 2
        if layer % 2 == 0:
            h = h + _mixer_ab(_rmsnorm(h, norm_ab_w[i]), w_in_ab[i], ret_norm_w[i],
                              s5_lam_re[i], s5_lam_im[i], s5_log_dt[i], s5_b_re[i], s5_b_im[i],
                              s5_c_re[i], s5_c_im[i], s5_d[i], s5_w_glu[i], w_out_ab[i], cos, sin)
        else:
            h = h + _mixer_c(_rmsnorm(h, norm_c_w[i]), w_in_c[i], gla_w_gate[i], gla_b_gate[i],
                             gla_norm_w[i], w_out_c[i])
    return _rmsnorm(h, final_norm_w)[:, N_META:]


import jax as _jax
import jax.numpy as _jnp

TWIN_FORMAT = 'train_step'
FWD_PARAMS = ['x', 'meta', 'norm_ab_w', 'w_in_ab', 'ret_norm_w', 's5_lam_re', 's5_lam_im', 's5_log_dt', 's5_b_re', 's5_b_im', 's5_c_re', 's5_c_im', 's5_d', 's5_w_glu', 'w_out_ab', 'norm_c_w', 'w_in_c', 'gla_w_gate', 'gla_b_gate', 'gla_norm_w', 'w_out_c', 'final_norm_w']
TWIN_WEIGHTS = ['meta', 'norm_ab_w', 'w_in_ab', 'ret_norm_w', 's5_lam_re', 's5_lam_im', 's5_log_dt', 's5_b_re', 's5_b_im', 's5_c_re', 's5_c_im', 's5_d', 's5_w_glu', 'w_out_ab', 'norm_c_w', 'w_in_c', 'gla_w_gate', 'gla_b_gate', 'gla_norm_w', 'w_out_c', 'final_norm_w']
TWIN_DIFF_INPUT = 'x'
TWIN_INPUTS = ['x', 'meta', 'norm_ab_w', 'w_in_ab', 'ret_norm_w', 's5_lam_re', 's5_lam_im', 's5_log_dt', 's5_b_re', 's5_b_im', 's5_c_re', 's5_c_im', 's5_d', 's5_w_glu', 'w_out_ab', 'norm_c_w', 'w_in_c', 'gla_w_gate', 'gla_b_gate', 'gla_norm_w', 'w_out_c', 'final_norm_w', 'loss_target', 'm_meta', 'm_norm_ab_w', 'm_w_in_ab', 'm_ret_norm_w', 'm_s5_lam_re', 'm_s5_lam_im', 'm_s5_log_dt', 'm_s5_b_re', 'm_s5_b_im', 'm_s5_c_re', 'm_s5_c_im', 'm_s5_d', 'm_s5_w_glu', 'm_w_out_ab', 'm_norm_c_w', 'm_w_in_c', 'm_gla_w_gate', 'm_gla_b_gate', 'm_gla_norm_w', 'm_w_out_c', 'm_final_norm_w', 'v_meta', 'v_norm_ab_w', 'v_w_in_ab', 'v_ret_norm_w', 'v_s5_lam_re', 'v_s5_lam_im', 'v_s5_log_dt', 'v_s5_b_re', 'v_s5_b_im', 'v_s5_c_re', 'v_s5_c_im', 'v_s5_d', 'v_s5_w_glu', 'v_w_out_ab', 'v_norm_c_w', 'v_w_in_c', 'v_gla_w_gate', 'v_gla_b_gate', 'v_gla_norm_w', 'v_w_out_c', 'v_final_norm_w']
TWIN_OUTPUTS = ['loss', 'grad_x', 'grad_meta', 'grad_norm_ab_w', 'grad_w_in_ab', 'grad_ret_norm_w', 'grad_s5_lam_re', 'grad_s5_lam_im', 'grad_s5_log_dt', 'grad_s5_b_re', 'grad_s5_b_im', 'grad_s5_c_re', 'grad_s5_c_im', 'grad_s5_d', 'grad_s5_w_glu', 'grad_w_out_ab', 'grad_norm_c_w', 'grad_w_in_c', 'grad_gla_w_gate', 'grad_gla_b_gate', 'grad_gla_norm_w', 'grad_w_out_c', 'grad_final_norm_w', 'delta_meta', 'delta_norm_ab_w', 'delta_w_in_ab', 'delta_ret_norm_w', 'delta_s5_lam_re', 'delta_s5_lam_im', 'delta_s5_log_dt', 'delta_s5_b_re', 'delta_s5_b_im', 'delta_s5_c_re', 'delta_s5_c_im', 'delta_s5_d', 'delta_s5_w_glu', 'delta_w_out_ab', 'delta_norm_c_w', 'delta_w_in_c', 'delta_gla_w_gate', 'delta_gla_b_gate', 'delta_gla_norm_w', 'delta_w_out_c', 'delta_final_norm_w', 'new_m_meta', 'new_m_norm_ab_w', 'new_m_w_in_ab', 'new_m_ret_norm_w', 'new_m_s5_lam_re', 'new_m_s5_lam_im', 'new_m_s5_log_dt', 'new_m_s5_b_re', 'new_m_s5_b_im', 'new_m_s5_c_re', 'new_m_s5_c_im', 'new_m_s5_d', 'new_m_s5_w_glu', 'new_m_w_out_ab', 'new_m_norm_c_w', 'new_m_w_in_c', 'new_m_gla_w_gate', 'new_m_gla_b_gate', 'new_m_gla_norm_w', 'new_m_w_out_c', 'new_m_final_norm_w', 'new_v_meta', 'new_v_norm_ab_w', 'new_v_w_in_ab', 'new_v_ret_norm_w', 'new_v_s5_lam_re', 'new_v_s5_lam_im', 'new_v_s5_log_dt', 'new_v_s5_b_re', 'new_v_s5_b_im', 'new_v_s5_c_re', 'new_v_s5_c_im', 'new_v_s5_d', 'new_v_s5_w_glu', 'new_v_w_out_ab', 'new_v_norm_c_w', 'new_v_w_in_c', 'new_v_gla_w_gate', 'new_v_gla_b_gate', 'new_v_gla_norm_w', 'new_v_w_out_c', 'new_v_final_norm_w']
TWIN_LEAF_KINDS = {'loss': 'loss', 'grad_x': 'grad_x', 'grad_meta': 'grad_w', 'grad_norm_ab_w': 'grad_w', 'grad_w_in_ab': 'grad_w', 'grad_ret_norm_w': 'grad_w', 'grad_s5_lam_re': 'grad_w', 'grad_s5_lam_im': 'grad_w', 'grad_s5_log_dt': 'grad_w', 'grad_s5_b_re': 'grad_w', 'grad_s5_b_im': 'grad_w', 'grad_s5_c_re': 'grad_w', 'grad_s5_c_im': 'grad_w', 'grad_s5_d': 'grad_w', 'grad_s5_w_glu': 'grad_w', 'grad_w_out_ab': 'grad_w', 'grad_norm_c_w': 'grad_w', 'grad_w_in_c': 'grad_w', 'grad_gla_w_gate': 'grad_w', 'grad_gla_b_gate': 'grad_w', 'grad_gla_norm_w': 'grad_w', 'grad_w_out_c': 'grad_w', 'grad_final_norm_w': 'grad_w', 'delta_meta': 'delta_w', 'delta_norm_ab_w': 'delta_w', 'delta_w_in_ab': 'delta_w', 'delta_ret_norm_w': 'delta_w', 'delta_s5_lam_re': 'delta_w', 'delta_s5_lam_im': 'delta_w', 'delta_s5_log_dt': 'delta_w', 'delta_s5_b_re': 'delta_w', 'delta_s5_b_im': 'delta_w', 'delta_s5_c_re': 'delta_w', 'delta_s5_c_im': 'delta_w', 'delta_s5_d': 'delta_w', 'delta_s5_w_glu': 'delta_w', 'delta_w_out_ab': 'delta_w', 'delta_norm_c_w': 'delta_w', 'delta_w_in_c': 'delta_w', 'delta_gla_w_gate': 'delta_w', 'delta_gla_b_gate': 'delta_w', 'delta_gla_norm_w': 'delta_w', 'delta_w_out_c': 'delta_w', 'delta_final_norm_w': 'delta_w', 'new_m_meta': 'new_m', 'new_m_norm_ab_w': 'new_m', 'new_m_w_in_ab': 'new_m', 'new_m_ret_norm_w': 'new_m', 'new_m_s5_lam_re': 'new_m', 'new_m_s5_lam_im': 'new_m', 'new_m_s5_log_dt': 'new_m', 'new_m_s5_b_re': 'new_m', 'new_m_s5_b_im': 'new_m', 'new_m_s5_c_re': 'new_m', 'new_m_s5_c_im': 'new_m', 'new_m_s5_d': 'new_m', 'new_m_s5_w_glu': 'new_m', 'new_m_w_out_ab': 'new_m', 'new_m_norm_c_w': 'new_m', 'new_m_w_in_c': 'new_m', 'new_m_gla_w_gate': 'new_m', 'new_m_gla_b_gate': 'new_m', 'new_m_gla_norm_w': 'new_m', 'new_m_w_out_c': 'new_m', 'new_m_final_norm_w': 'new_m', 'new_v_meta': 'new_v', 'new_v_norm_ab_w': 'new_v', 'new_v_w_in_ab': 'new_v', 'new_v_ret_norm_w': 'new_v', 'new_v_s5_lam_re': 'new_v', 'new_v_s5_lam_im': 'new_v', 'new_v_s5_log_dt': 'new_v', 'new_v_s5_b_re': 'new_v', 'new_v_s5_b_im': 'new_v', 'new_v_s5_c_re': 'new_v', 'new_v_s5_c_im': 'new_v', 'new_v_s5_d': 'new_v', 'new_v_s5_w_glu': 'new_v', 'new_v_w_out_ab': 'new_v', 'new_v_norm_c_w': 'new_v', 'new_v_w_in_c': 'new_v', 'new_v_gla_w_gate': 'new_v', 'new_v_gla_b_gate': 'new_v', 'new_v_gla_norm_w': 'new_v', 'new_v_w_out_c': 'new_v', 'new_v_final_norm_w': 'new_v'}


def _forward(args):
    return _fwd_reference(*[args[k] for k in FWD_PARAMS])


def _output_shape():
    def fwd():
        inp = _fwd_setup_inputs(0)
        return _fwd_reference(*[inp[k] for k in FWD_PARAMS])
    out = _jax.eval_shape(fwd)
    return out.shape, out.dtype

N_MICROBATCH = 1
ADAM_LR = 0.001
ADAM_B1 = 0.9
ADAM_B2 = 0.999
ADAM_EPS = 1e-08
ADAM_WD = 0.01
ADAM_STEP = 10
PER_EXAMPLE_BATCH_AXIS = {'x': 0, 'loss_target': 0}
SHARED_INPUTS = []
_WEIGHT_DTYPES = {'meta': _jnp.float32, 'norm_ab_w': _jnp.float32, 'w_in_ab': _jnp.float32, 'ret_norm_w': _jnp.float32, 's5_lam_re': _jnp.float32, 's5_lam_im': _jnp.float32, 's5_log_dt': _jnp.float32, 's5_b_re': _jnp.float32, 's5_b_im': _jnp.float32, 's5_c_re': _jnp.float32, 's5_c_im': _jnp.float32, 's5_d': _jnp.float32, 's5_w_glu': _jnp.float32, 'w_out_ab': _jnp.float32, 'norm_c_w': _jnp.float32, 'w_in_c': _jnp.float32, 'gla_w_gate': _jnp.float32, 'gla_b_gate': _jnp.float32, 'gla_norm_w': _jnp.float32, 'w_out_c': _jnp.float32, 'final_norm_w': _jnp.float32}
MOMENT_SCALE = {'meta': 7.192528e-03, 'norm_ab_w': 1.085847e-01, 'w_in_ab': 5.191774e-02, 'ret_norm_w': 5.001435e-02, 's5_lam_re': 1.165062e-03, 's5_lam_im': 1.265253e-03, 's5_log_dt': 9.641567e-01, 's5_b_re': 8.234971e-04, 's5_b_im': 8.379468e-04, 's5_c_re': 1.148221e-03, 's5_c_im': 1.148329e-03, 's5_d': 1.997156e-02, 's5_w_glu': 5.220381e-03, 'w_out_ab': 5.183367e-02, 'norm_c_w': 8.624825e-02, 'w_in_c': 4.923490e-02, 'gla_w_gate': 6.860659e-03, 'gla_b_gate': 2.987468e-02, 'gla_norm_w': 4.210013e-02, 'w_out_c': 4.175106e-02, 'final_norm_w': 1.600771e+01}


def _to_microbatches(a, axis):
    t = _jnp.moveaxis(a, axis, 0)
    t = t.reshape((N_MICROBATCH, t.shape[0] // N_MICROBATCH) + t.shape[1:])
    return _jnp.moveaxis(t, 1, axis + 1)


def setup_inputs(seed: int = 0) -> dict:
    inp = _fwd_setup_inputs(seed)
    key = _jax.random.fold_in(_jax.random.key(seed), 7919)
    shape, _ = _output_shape()
    out = dict(inp)
    out["loss_target"] = _jax.random.normal(_jax.random.fold_in(key, 0), shape, _jnp.float32)
    for i, name in enumerate(TWIN_WEIGHTS):
        w = inp[name].astype(_jnp.float32)
        if MOMENT_SCALE is None:
            s = _jnp.sqrt(_jnp.mean(_jnp.square(w)) + 1e-30)
        else:
            s = MOMENT_SCALE[name]
        km, kv = _jax.random.split(_jax.random.fold_in(key, i + 1))
        out[name] = w
        out["m_" + name] = s * _jax.random.normal(km, w.shape, _jnp.float32)
        out["v_" + name] = (s * s) * _jax.random.uniform(kv, w.shape, _jnp.float32, 0.5, 1.5)
    if N_MICROBATCH > 1:
        for name, axis in PER_EXAMPLE_BATCH_AXIS.items():
            out[name] = _to_microbatches(out[name], axis)
    return {'x': out['x'], 'meta': out['meta'], 'norm_ab_w': out['norm_ab_w'], 'w_in_ab': out['w_in_ab'], 'ret_norm_w': out['ret_norm_w'], 's5_lam_re': out['s5_lam_re'], 's5_lam_im': out['s5_lam_im'], 's5_log_dt': out['s5_log_dt'], 's5_b_re': out['s5_b_re'], 's5_b_im': out['s5_b_im'], 's5_c_re': out['s5_c_re'], 's5_c_im': out['s5_c_im'], 's5_d': out['s5_d'], 's5_w_glu': out['s5_w_glu'], 'w_out_ab': out['w_out_ab'], 'norm_c_w': out['norm_c_w'], 'w_in_c': out['w_in_c'], 'gla_w_gate': out['gla_w_gate'], 'gla_b_gate': out['gla_b_gate'], 'gla_norm_w': out['gla_norm_w'], 'w_out_c': out['w_out_c'], 'final_norm_w': out['final_norm_w'], 'loss_target': out['loss_target'], 'm_meta': out['m_meta'], 'm_norm_ab_w': out['m_norm_ab_w'], 'm_w_in_ab': out['m_w_in_ab'], 'm_ret_norm_w': out['m_ret_norm_w'], 'm_s5_lam_re': out['m_s5_lam_re'], 'm_s5_lam_im': out['m_s5_lam_im'], 'm_s5_log_dt': out['m_s5_log_dt'], 'm_s5_b_re': out['m_s5_b_re'], 'm_s5_b_im': out['m_s5_b_im'], 'm_s5_c_re': out['m_s5_c_re'], 'm_s5_c_im': out['m_s5_c_im'], 'm_s5_d': out['m_s5_d'], 'm_s5_w_glu': out['m_s5_w_glu'], 'm_w_out_ab': out['m_w_out_ab'], 'm_norm_c_w': out['m_norm_c_w'], 'm_w_in_c': out['m_w_in_c'], 'm_gla_w_gate': out['m_gla_w_gate'], 'm_gla_b_gate': out['m_gla_b_gate'], 'm_gla_norm_w': out['m_gla_norm_w'], 'm_w_out_c': out['m_w_out_c'], 'm_final_norm_w': out['m_final_norm_w'], 'v_meta': out['v_meta'], 'v_norm_ab_w': out['v_norm_ab_w'], 'v_w_in_ab': out['v_w_in_ab'], 'v_ret_norm_w': out['v_ret_norm_w'], 'v_s5_lam_re': out['v_s5_lam_re'], 'v_s5_lam_im': out['v_s5_lam_im'], 'v_s5_log_dt': out['v_s5_log_dt'], 'v_s5_b_re': out['v_s5_b_re'], 'v_s5_b_im': out['v_s5_b_im'], 'v_s5_c_re': out['v_s5_c_re'], 'v_s5_c_im': out['v_s5_c_im'], 'v_s5_d': out['v_s5_d'], 'v_s5_w_glu': out['v_s5_w_glu'], 'v_w_out_ab': out['v_w_out_ab'], 'v_norm_c_w': out['v_norm_c_w'], 'v_w_in_c': out['v_w_in_c'], 'v_gla_w_gate': out['v_gla_w_gate'], 'v_gla_b_gate': out['v_gla_b_gate'], 'v_gla_norm_w': out['v_gla_norm_w'], 'v_w_out_c': out['v_w_out_c'], 'v_final_norm_w': out['v_final_norm_w']}


def _loss(weights, diff, rest, loss_target):
    with _jax.named_scope("forward"):
        args = {**rest, TWIN_DIFF_INPUT: diff, **{k: w.astype(_WEIGHT_DTYPES[k]) for k, w in weights.items()}}
        y = _forward(args)
    with _jax.named_scope("loss_head"):
        err = _jnp.square(y.astype(_jnp.float32) - loss_target)
        return 0.5 * _jnp.sum(_jnp.mean(err, axis=-1)) if err.ndim else 0.5 * err


def _adamw(w, g, m, v):
    m = ADAM_B1 * m + (1.0 - ADAM_B1) * g
    v = ADAM_B2 * v + (1.0 - ADAM_B2) * _jnp.square(g)
    m_hat = m / (1.0 - ADAM_B1 ** ADAM_STEP)
    v_hat = v / (1.0 - ADAM_B2 ** ADAM_STEP)
    delta = -ADAM_LR * (m_hat / (_jnp.sqrt(v_hat) + ADAM_EPS) + ADAM_WD * w)
    return delta, m, v


def reference(x, meta, norm_ab_w, w_in_ab, ret_norm_w, s5_lam_re, s5_lam_im, s5_log_dt, s5_b_re, s5_b_im, s5_c_re, s5_c_im, s5_d, s5_w_glu, w_out_ab, norm_c_w, w_in_c, gla_w_gate, gla_b_gate, gla_norm_w, w_out_c, final_norm_w, loss_target, m_meta, m_norm_ab_w, m_w_in_ab, m_ret_norm_w, m_s5_lam_re, m_s5_lam_im, m_s5_log_dt, m_s5_b_re, m_s5_b_im, m_s5_c_re, m_s5_c_im, m_s5_d, m_s5_w_glu, m_w_out_ab, m_norm_c_w, m_w_in_c, m_gla_w_gate, m_gla_b_gate, m_gla_norm_w, m_w_out_c, m_final_norm_w, v_meta, v_norm_ab_w, v_w_in_ab, v_ret_norm_w, v_s5_lam_re, v_s5_lam_im, v_s5_log_dt, v_s5_b_re, v_s5_b_im, v_s5_c_re, v_s5_c_im, v_s5_d, v_s5_w_glu, v_w_out_ab, v_norm_c_w, v_w_in_c, v_gla_w_gate, v_gla_b_gate, v_gla_norm_w, v_w_out_c, v_final_norm_w):
    given = dict(x=x, meta=meta, norm_ab_w=norm_ab_w, w_in_ab=w_in_ab, ret_norm_w=ret_norm_w, s5_lam_re=s5_lam_re, s5_lam_im=s5_lam_im, s5_log_dt=s5_log_dt, s5_b_re=s5_b_re, s5_b_im=s5_b_im, s5_c_re=s5_c_re, s5_c_im=s5_c_im, s5_d=s5_d, s5_w_glu=s5_w_glu, w_out_ab=w_out_ab, norm_c_w=norm_c_w, w_in_c=w_in_c, gla_w_gate=gla_w_gate, gla_b_gate=gla_b_gate, gla_norm_w=gla_norm_w, w_out_c=w_out_c, final_norm_w=final_norm_w, loss_target=loss_target, m_meta=m_meta, m_norm_ab_w=m_norm_ab_w, m_w_in_ab=m_w_in_ab, m_ret_norm_w=m_ret_norm_w, m_s5_lam_re=m_s5_lam_re, m_s5_lam_im=m_s5_lam_im, m_s5_log_dt=m_s5_log_dt, m_s5_b_re=m_s5_b_re, m_s5_b_im=m_s5_b_im, m_s5_c_re=m_s5_c_re, m_s5_c_im=m_s5_c_im, m_s5_d=m_s5_d, m_s5_w_glu=m_s5_w_glu, m_w_out_ab=m_w_out_ab, m_norm_c_w=m_norm_c_w, m_w_in_c=m_w_in_c, m_gla_w_gate=m_gla_w_gate, m_gla_b_gate=m_gla_b_gate, m_gla_norm_w=m_gla_norm_w, m_w_out_c=m_w_out_c, m_final_norm_w=m_final_norm_w, v_meta=v_meta, v_norm_ab_w=v_norm_ab_w, v_w_in_ab=v_w_in_ab, v_ret_norm_w=v_ret_norm_w, v_s5_lam_re=v_s5_lam_re, v_s5_lam_im=v_s5_lam_im, v_s5_log_dt=v_s5_log_dt, v_s5_b_re=v_s5_b_re, v_s5_b_im=v_s5_b_im, v_s5_c_re=v_s5_c_re, v_s5_c_im=v_s5_c_im, v_s5_d=v_s5_d, v_s5_w_glu=v_s5_w_glu, v_w_out_ab=v_w_out_ab, v_norm_c_w=v_norm_c_w, v_w_in_c=v_w_in_c, v_gla_w_gate=v_gla_w_gate, v_gla_b_gate=v_gla_b_gate, v_gla_norm_w=v_gla_norm_w, v_w_out_c=v_w_out_c, v_final_norm_w=v_final_norm_w)
    weights = {n: given[n] for n in TWIN_WEIGHTS}
    shared = {n: given[n] for n in SHARED_INPUTS}
    per_example = {n: given[n] for n in ['x']}
    grad_fn = _jax.value_and_grad(_loss, argnums=(0, 1))

    def one_microbatch(ex, loss_target):
        ex = dict(ex)
        diff = ex.pop(TWIN_DIFF_INPUT)
        return grad_fn(weights, diff, {**shared, **ex}, loss_target)

    if N_MICROBATCH == 1:
        loss, (grad_w, grad_x) = one_microbatch(per_example, given["loss_target"])
    else:
        def body(carry, xs):
            loss_sum, grad_sum = carry
            l_k, (gw_k, gx_k) = one_microbatch(xs[0], xs[1])
            with _jax.named_scope("update"):
                return (loss_sum + l_k, _jax.tree.map(_jnp.add, grad_sum, gw_k)), gx_k

        init = (_jnp.zeros((), _jnp.float32), _jax.tree.map(_jnp.zeros_like, weights))
        (loss, grad_w), grad_x = _jax.lax.scan(body, init, (per_example, given["loss_target"]))
    with _jax.named_scope("update"):
        delta_w, new_m, new_v = {}, {}, {}
        for n in TWIN_WEIGHTS:
            delta_w[n], new_m[n], new_v[n] = _adamw(weights[n], grad_w[n], given["m_" + n], given["v_" + n])
    return (loss, grad_x, *[grad_w[n] for n in TWIN_WEIGHTS], *[delta_w[n] for n in TWIN_WEIGHTS],
            *[new_m[n] for n in TWIN_WEIGHTS], *[new_v[n] for n in TWIN_WEIGHTS])
```

```python
import functools
import math

import jax
import jax.numpy as jnp
from jax import lax
from jax.experimental import pallas as pl
from jax.experimental.pallas import tpu as pltpu

F32 = jnp.float32
BF16 = jnp.bfloat16
MESH = pl.DeviceIdType.MESH

D_MODEL = 2048
N_META = 16
CHUNK = 128
SUB = 16
NSUB = CHUNK // SUB
PAD = CHUNK - N_META
EPS = 1e-6

RET_HEADS = 8
RET_DK = 128
RET_DV = 256
RET_QK = RET_HEADS * RET_DK
RET_W = RET_HEADS * RET_DV
ROPE_BASE = 10000.0

S5_W = 1024
S5_GH = 16
S5_G = S5_W // S5_GH
S5_P = 64
S5_TG = 8
S5_NT = S5_G // S5_TG
S5_TU = S5_TG * S5_GH
S5_TS = S5_TG * S5_P

GLA_HEADS = 4
GLA_DK = 256
GLA_DV = 512
GLA_QK = GLA_HEADS * GLA_DK
GLA_W = GLA_HEADS * GLA_DV
GLA_RANK = 16
GLA_TAU = 16.0

IN_AB = 2 * RET_QK + 2 * RET_W + 2 * S5_W
OUT_AB = RET_W + S5_W
IN_C = 2 * GLA_QK + 2 * GLA_W + GLA_RANK
IN_C_PAD = 2 * GLA_QK + 2 * GLA_W + 128

ADAM_LR = 0.001
ADAM_B1 = 0.9
ADAM_B2 = 0.999
ADAM_EPS = 1e-08
ADAM_WD = 0.01
ADAM_STEP = 10

N_SHARD = 4
SMALL_COLS = 512

NN = (((1,), (0,)), ((), ()))
NT = (((1,), (1,)), ((), ()))
TN = (((0,), (0,)), ((), ()))


def _dot(a, b, dims=NN):
    return lax.dot_general(a.astype(BF16), b.astype(BF16), dims, preferred_element_type=F32)


def _mo(v, m):
    return v if isinstance(v, int) else pl.multiple_of(v, m)


def _sigmoid(x):
    return 1.0 / (1.0 + jnp.exp(-x))


def _row_tile(rows, cap):
    n = rows // CHUNK
    best = 1
    for d in range(1, n + 1):
        if n % d == 0 and d * CHUNK <= cap:
            best = d
    return best * CHUNK


def _col_tile(cols, cap):
    n = cols // 128
    best = 1
    for d in range(1, n + 1):
        if n % d == 0 and d * 128 <= cap:
            best = d
    return best * 128


def _matmul(name, a, b, dims, m, n, k, *, tm, tn, tk, out_dtype=F32, a_off=(0, 0), b_off=(0, 0),
            extras=(), epilogue=None, out_shape=None, out_spec=None):
    nk = k // tk
    assert m % tm == 0 and n % tn == 0 and k % tk == 0, (name, m, n, k, tm, tn, tk)
    ar, ac = a_off
    br, bc = b_off
    if dims == NN:
        a_spec = pl.BlockSpec((tm, tk), lambda i, j, kk: (i + ar, kk + ac))
        b_spec = pl.BlockSpec((tk, tn), lambda i, j, kk: (kk + br, j + bc))
    elif dims == NT:
        a_spec = pl.BlockSpec((tm, tk), lambda i, j, kk: (i + ar, kk + ac))
        b_spec = pl.BlockSpec((tn, tk), lambda i, j, kk: (j + br, kk + bc))
    else:
        a_spec = pl.BlockSpec((tk, tm), lambda i, j, kk: (kk + ar, i + ac))
        b_spec = pl.BlockSpec((tk, tn), lambda i, j, kk: (kk + br, j + bc))
    n_extra = len(extras)

    def body(*refs):
        a_ref, b_ref = refs[0], refs[1]
        e_refs = refs[2:2 + n_extra]
        o_ref = refs[2 + n_extra]
        acc_ref = refs[3 + n_extra]
        kk = pl.program_id(2)

        @pl.when(kk == 0)
        def _():
            acc_ref[...] = jnp.zeros_like(acc_ref)

        acc_ref[...] += _dot(a_ref[...], b_ref[...], dims)

        @pl.when(kk == nk - 1)
        def _():
            acc = acc_ref[...]
            if epilogue is not None:
                acc = epilogue(acc, *[e[...] for e in e_refs])
            o_ref[...] = acc.astype(o_ref.dtype)

    if out_shape is None:
        out_shape = jax.ShapeDtypeStruct((m, n), out_dtype)
    if out_spec is None:
        out_spec = pl.BlockSpec((tm, tn), lambda i, j, kk: (i, j))
    return pl.pallas_call(
        body, name=name, grid=(m // tm, n // tn, nk),
        in_specs=[a_spec, b_spec] + [pl.BlockSpec(bs, im) for (_, bs, im) in extras],
        out_specs=out_spec, out_shape=out_shape,
        scratch_shapes=[pltpu.VMEM((tm, tn), F32)],
        compiler_params=pltpu.CompilerParams(dimension_semantics=("parallel", "parallel", "arbitrary")),
    )(a, b, *[e for (e, _, _) in extras])


def _rms_fwd(name, h, w):
    rows, d = h.shape
    tm = _row_tile(rows, 512)

    def body(h_ref, w_ref, o_ref):
        x = h_ref[...]
        r = lax.rsqrt(jnp.mean(x * x, axis=-1, keepdims=True) + EPS)
        o_ref[...] = (x * r * w_ref[...]).astype(BF16)

    return pl.pallas_call(
        body, name=name, grid=(rows // tm,),
        in_specs=[pl.BlockSpec((tm, d), lambda i: (i, 0)), pl.BlockSpec((1, d), lambda i: (0, 0))],
        out_specs=pl.BlockSpec((tm, d), lambda i: (i, 0)),
        out_shape=jax.ShapeDtypeStruct((rows, d), BF16),
    )(h, w)


def _rms_bwd(name, dhn, h, w, dres):
    rows, d = h.shape
    tm = _row_tile(rows, 384)

    def body(g_ref, h_ref, w_ref, r_ref, dh_ref, dw_ref):
        i = pl.program_id(0)
        x = h_ref[...]
        r = lax.rsqrt(jnp.mean(x * x, axis=-1, keepdims=True) + EPS)
        xh = x * r
        g = g_ref[...]
        gw = g * w_ref[...]
        dh_ref[...] = r_ref[...] + r * (gw - xh * jnp.mean(gw * xh, axis=-1, keepdims=True))

        @pl.when(i == 0)
        def _():
            dw_ref[...] = jnp.zeros_like(dw_ref)

        dw_ref[...] += jnp.sum(g * xh, axis=0, keepdims=True)

    return pl.pallas_call(
        body, name=name, grid=(rows // tm,),
        in_specs=[pl.BlockSpec((tm, d), lambda i: (i, 0)), pl.BlockSpec((tm, d), lambda i: (i, 0)),
                  pl.BlockSpec((1, d), lambda i: (0, 0)), pl.BlockSpec((tm, d), lambda i: (i, 0))],
        out_specs=[pl.BlockSpec((tm, d), lambda i: (i, 0)), pl.BlockSpec((1, d), lambda i: (0, 0))],
        out_shape=[jax.ShapeDtypeStruct((rows, d), F32), jax.ShapeDtypeStruct((1, d), F32)],
    )(dhn, h, w, dres)


def _final_loss(h2, w, target):
    rows, d = h2.shape

    def body(h_ref, w_ref, t_ref, loss_ref, dh_ref, dw_ref):
        i = pl.program_id(0)

        @pl.when(i == 0)
        def _():
            loss_ref[...] = jnp.zeros_like(loss_ref)
            dw_ref[...] = jnp.zeros_like(dw_ref)
            dh_ref[...] = jnp.zeros_like(dh_ref)

        @pl.when(i > 0)
        def _():
            x = h_ref[...]
            r = lax.rsqrt(jnp.mean(x * x, axis=-1, keepdims=True) + EPS)
            xh = x * r
            wv = w_ref[...]
            err = xh * wv - t_ref[...]
            loss_ref[...] += 0.5 * jnp.sum(jnp.mean(err * err, axis=-1, keepdims=True), axis=0, keepdims=True)
            g = err * (1.0 / d)
            gw = g * wv
            dh_ref[...] = r * (gw - xh * jnp.mean(gw * xh, axis=-1, keepdims=True))
            dw_ref[...] += jnp.sum(g * xh, axis=0, keepdims=True)

    return pl.pallas_call(
        body, name="final_loss", grid=(rows // CHUNK,),
        in_specs=[pl.BlockSpec((CHUNK, d), lambda i: (i, 0)), pl.BlockSpec((1, d), lambda i: (0, 0)),
                  pl.BlockSpec((CHUNK, d), lambda i: (jnp.maximum(i - 1, 0), 0))],
        out_specs=[pl.BlockSpec((1, 1), lambda i: (0, 0)), pl.BlockSpec((CHUNK, d), lambda i: (i, 0)),
                   pl.BlockSpec((1, d), lambda i: (0, 0))],
        out_shape=[jax.ShapeDtypeStruct((1, 1), F32), jax.ShapeDtypeStruct((rows, d), F32),
                   jax.ShapeDtypeStruct((1, d), F32)],
    )(h2, w, target)


def _gate_fwd(o, z, w):
    rs = lax.rsqrt(jnp.mean(o * o, axis=-1, keepdims=True) + EPS)
    return o * rs * w * (z * _sigmoid(z))


def _gate_bwd(dout, o, z, w):
    rs = lax.rsqrt(jnp.mean(o * o, axis=-1, keepdims=True) + EPS)
    yn = o * rs
    sg = _sigmoid(z)
    sil = z * sg
    dsil = sg * (1.0 + z * (1.0 - sg))
    dz = dout * yn * w * dsil
    dyn = dout * w * sil
    dw = jnp.sum(dout * yn * sil, axis=0, keepdims=True)
    do = rs * (dyn - yn * jnp.mean(dyn * yn, axis=-1, keepdims=True))
    return do, dz, dw


def _rope(t, cosf, sinf):
    return t * cosf + pltpu.roll(t, RET_DK // 2, 1) * sinf


def _rope_t(d, cosf, sinf):
    return d * cosf + pltpu.roll(d * sinf, RET_DK // 2, 1)


def _ret_tables():
    log_g = jnp.log1p(-jnp.exp2(-5.0 - jnp.arange(RET_HEADS, dtype=F32)))
    idx = jnp.arange(CHUNK, dtype=F32)
    diff = idx[:, None] - idx[None, :]
    decay = jnp.where(diff >= 0, jnp.exp(log_g[:, None, None] * jnp.maximum(diff, 0.0)), 0.0)
    kw = jnp.exp(log_g[:, None] * (CHUNK - 1 - idx))
    qw = jnp.exp(log_g[:, None] * (idx + 1.0))
    gch = jnp.exp(log_g * CHUNK)
    kw = jnp.broadcast_to(kw[:, :, None], (RET_HEADS, CHUNK, RET_DK))
    qw = jnp.broadcast_to(qw[:, :, None], (RET_HEADS, CHUNK, RET_DK))
    gch = jnp.broadcast_to(gch[:, None, None], (RET_HEADS, 1, RET_DV))
    return decay, kw, qw, gch


def _rope_tables(rows):
    pos = jnp.arange(rows, dtype=F32) - float(PAD)
    inv_freq = jnp.power(ROPE_BASE, -jnp.arange(0, RET_DK, 2, dtype=F32) / RET_DK)
    ang = pos[:, None] * inv_freq[None, :]
    cos, sin = jnp.cos(ang), jnp.sin(ang)
    return jnp.concatenate([cos, cos], axis=1), jnp.concatenate([-sin, sin], axis=1)


def _ret_in_specs(rev, nc):
    def cn(n):
        return (nc - 1 - n) if rev else n
    kb = RET_QK // RET_DK
    vb = 2 * RET_QK // RET_DV
    zb = (2 * RET_QK + RET_W) // RET_DV
    return [
        pl.BlockSpec((CHUNK, RET_DK), lambda h, n: (cn(n), h)),
        pl.BlockSpec((CHUNK, RET_DK), lambda h, n: (cn(n), kb + h)),
        pl.BlockSpec((CHUNK, RET_DV), lambda h, n: (cn(n), vb + h)),
        pl.BlockSpec((CHUNK, RET_DV), lambda h, n: (cn(n), zb + h)),
        pl.BlockSpec((CHUNK, RET_DK), lambda h, n: (cn(n), 0)),
        pl.BlockSpec((CHUNK, RET_DK), lambda h, n: (cn(n), 0)),
        pl.BlockSpec((1, CHUNK, CHUNK), lambda h, n: (h, 0, 0)),
        pl.BlockSpec((1, CHUNK, RET_DK), lambda h, n: (h, 0, 0)),
        pl.BlockSpec((1, CHUNK, RET_DK), lambda h, n: (h, 0, 0)),
        pl.BlockSpec((1, 1, RET_DV), lambda h, n: (h, 0, 0)),
        pl.BlockSpec((1, RET_DV), lambda h, n: (0, h)),
    ]


def _ret_fwd(proj, cosf, sinf, tables, normw):
    rows = proj.shape[0]
    nc = rows // CHUNK
    decay, kw, qw, gch = tables

    def body(q_ref, k_ref, v_ref, z_ref, cos_ref, sin_ref, dm_ref, kw_ref, qw_ref, g_ref, w_ref,
             o_ref, oa_ref, st_ref, s_scr):
        n = pl.program_id(1)

        @pl.when(n == 0)
        def _():
            s_scr[...] = jnp.zeros_like(s_scr)

        cosv, sinv = cos_ref[...], sin_ref[...]
        q = _rope(q_ref[...], cosv, sinv)
        k = _rope(k_ref[...], cosv, sinv) * (RET_DK ** -0.5)
        v = v_ref[...]
        s = s_scr[...]
        st_ref[0, 0] = s.astype(BF16)
        a = _dot(q, k, NT) * dm_ref[0]
        o = _dot(a, v) + _dot(q * qw_ref[0], s)
        s_scr[...] = s * g_ref[0] + _dot(k * kw_ref[0], v, TN)
        o_ref[...] = o
        oa_ref[...] = _gate_fwd(o, z_ref[...], w_ref[...]).astype(BF16)

    return pl.pallas_call(
        body, name="ret_fwd", grid=(RET_HEADS, nc),
        in_specs=_ret_in_specs(False, nc),
        out_specs=[pl.BlockSpec((CHUNK, RET_DV), lambda h, n: (n, h)),
                   pl.BlockSpec((CHUNK, RET_DV), lambda h, n: (n, h)),
                   pl.BlockSpec((1, 1, RET_DK, RET_DV), lambda h, n: (h, n, 0, 0))],
        out_shape=[jax.ShapeDtypeStruct((rows, RET_W), F32), jax.ShapeDtypeStruct((rows, RET_W), BF16),
                   jax.ShapeDtypeStruct((RET_HEADS, nc, RET_DK, RET_DV), BF16)],
        scratch_shapes=[pltpu.VMEM((RET_DK, RET_DV), F32)],
        compiler_params=pltpu.CompilerParams(dimension_semantics=("parallel", "arbitrary")),
    )(proj, proj, proj, proj, cosf, sinf, decay, kw, qw, gch, normw)


def _ret_bwd(proj, cosf, sinf, tables, normw, o_ret, dmix, states):
    rows = proj.shape[0]
    nc = rows // CHUNK
    decay, kw, qw, gch = tables

    def rn(n):
        return nc - 1 - n

    def body(q_ref, k_ref, v_ref, z_ref, cos_ref, sin_ref, dm_ref, kw_ref, qw_ref, g_ref, w_ref,
             o_ref, do_ref, st_ref, dq_ref, dk_ref, dv_ref, dz_ref, dw_ref, ds_scr):
        n = pl.program_id(1)

        @pl.when(n == 0)
        def _():
            ds_scr[...] = jnp.zeros_like(ds_scr)
            dw_ref[...] = jnp.zeros_like(dw_ref)

        cosv, sinv = cos_ref[...], sin_ref[...]
        q = _rope(q_ref[...], cosv, sinv)
        k = _rope(k_ref[...], cosv, sinv) * (RET_DK ** -0.5)
        v = v_ref[...]
        do, dz, dw = _gate_bwd(do_ref[...], o_ref[...], z_ref[...], w_ref[...])
        dz_ref[...] = dz.astype(BF16)
        dw_ref[0] += dw
        dm = dm_ref[0]
        s = st_ref[0, 0]
        g1 = ds_scr[...]
        p = _dot(q, k, NT) * dm
        kwv = k * kw_ref[0]
        qwv = q * qw_ref[0]
        dp = _dot(do, v, NT)
        da = dp * dm
        dv = _dot(p, do, TN) + _dot(kwv, g1)
        dq = _dot(da, k) + _dot(do, s, NT) * qw_ref[0]
        dk = _dot(da, q, TN) + _dot(v, g1, NT) * kw_ref[0]
        ds_scr[...] = g1 * g_ref[0] + _dot(qwv, do, TN)
        dv_ref[...] = dv.astype(BF16)
        dq_ref[...] = _rope_t(dq, cosv, sinv).astype(BF16)
        dk_ref[...] = _rope_t(dk * (RET_DK ** -0.5), cosv, sinv).astype(BF16)

    in_specs = _ret_in_specs(True, nc) + [
        pl.BlockSpec((CHUNK, RET_DV), lambda h, n: (rn(n), h)),
        pl.BlockSpec((CHUNK, RET_DV), lambda h, n: (rn(n), h)),
        pl.BlockSpec((1, 1, RET_DK, RET_DV), lambda h, n: (h, rn(n), 0, 0)),
    ]
    return pl.pallas_call(
        body, name="ret_bwd", grid=(RET_HEADS, nc),
        in_specs=in_specs,
        out_specs=[pl.BlockSpec((CHUNK, RET_DK), lambda h, n: (rn(n), h)),
                   pl.BlockSpec((CHUNK, RET_DK), lambda h, n: (rn(n), h)),
                   pl.BlockSpec((CHUNK, RET_DV), lambda h, n: (rn(n), h)),
                   pl.BlockSpec((CHUNK, RET_DV), lambda h, n: (rn(n), h)),
                   pl.BlockSpec((1, 1, RET_DV), lambda h, n: (h, 0, 0))],
        out_shape=[jax.ShapeDtypeStruct((rows, RET_QK), BF16), jax.ShapeDtypeStruct((rows, RET_QK), BF16),
                   jax.ShapeDtypeStruct((rows, RET_W), BF16), jax.ShapeDtypeStruct((rows, RET_W), BF16),
                   jax.ShapeDtypeStruct((RET_HEADS, 1, RET_DV), F32)],
        scratch_shapes=[pltpu.VMEM((RET_DK, RET_DV), F32)],
        compiler_params=pltpu.CompilerParams(dimension_semantics=("parallel", "arbitrary")),
    )(proj, proj, proj, proj, cosf, sinf, decay, kw, qw, gch, normw, o_ret, dmix, states)


def _s5_discretize(lam_re, lam_im, log_dt, b_re, b_im):
    dt = jnp.exp(log_dt)[:, None]
    mag = jnp.exp(lam_re * dt)
    ab_re, ab_im = mag * jnp.cos(lam_im * dt), mag * jnp.sin(lam_im * dt)
    den = lam_re * lam_re + lam_im * lam_im
    nr, ni = ab_re - 1.0, ab_im
    f_re = (nr * lam_re + ni * lam_im) / den
    f_im = (ni * lam_re - nr * lam_im) / den
    bb_re = f_re[..., None] * b_re - f_im[..., None] * b_im
    bb_im = f_re[..., None] * b_im + f_im[..., None] * b_re
    return ab_re, ab_im, bb_re, bb_im


def _bdiag_in(bb):
    t = bb.reshape(S5_NT, S5_TG, S5_P, S5_GH).transpose(0, 1, 3, 2)
    eye = jnp.eye(S5_TG, dtype=bb.dtype)
    full = t[:, :, :, None, :] * eye[None, :, None, :, None]
    return full.reshape(S5_NT, S5_TU, S5_TS)


def _bdiag_in_extract(dense):
    t = dense.reshape(S5_NT, S5_TG, S5_GH, S5_TG, S5_P)
    diag = jnp.stack([t[:, g, :, g, :] for g in range(S5_TG)], axis=1)
    return diag.transpose(0, 1, 3, 2).reshape(S5_G, S5_P, S5_GH)


def _bdiag_out(c):
    t = c.reshape(S5_NT, S5_TG, S5_GH, S5_P).transpose(0, 1, 3, 2)
    eye = jnp.eye(S5_TG, dtype=c.dtype)
    full = t[:, :, :, None, :] * eye[None, :, None, :, None]
    return full.reshape(S5_NT, S5_TS, S5_TU)


def _bdiag_out_extract(dense):
    t = dense.reshape(S5_NT, S5_TG, S5_P, S5_TG, S5_GH)
    diag = jnp.stack([t[:, g, :, g, :] for g in range(S5_TG)], axis=1)
    return diag.transpose(0, 1, 3, 2).reshape(S5_G, S5_GH, S5_P)


def _cmul(ar, ai, br, bi):
    return ar * br - ai * bi, ar * bi + ai * br


def _pow_table(ar, ai, down):
    row = lax.broadcasted_iota(jnp.int32, (CHUNK, 1), 0)
    pr = jnp.broadcast_to(ar, (CHUNK, S5_TS))
    pi = jnp.broadcast_to(ai, (CHUNK, S5_TS))
    sh = 1
    while sh < CHUNK:
        if down:
            keep = row >= sh
            sr, si = pltpu.roll(pr, sh, 0), pltpu.roll(pi, sh, 0)
        else:
            keep = row < CHUNK - sh
            sr, si = pltpu.roll(pr, CHUNK - sh, 0), pltpu.roll(pi, CHUNK - sh, 0)
        sr = jnp.where(keep, sr, 1.0)
        si = jnp.where(keep, si, 0.0)
        pr, pi = _cmul(pr, pi, sr, si)
        sh *= 2
    return pr, pi


def _scan_block(xr, xi, ar, ai, down):
    row = lax.broadcasted_iota(jnp.int32, (CHUNK, 1), 0)
    sh = 1
    while sh < CHUNK:
        if down:
            keep = row >= sh
            sr, si = pltpu.roll(xr, sh, 0), pltpu.roll(xi, sh, 0)
        else:
            keep = row < CHUNK - sh
            sr, si = pltpu.roll(xr, CHUNK - sh, 0), pltpu.roll(xi, CHUNK - sh, 0)
        sr = jnp.where(keep, sr, 0.0)
        si = jnp.where(keep, si, 0.0)
        mr, mi = _cmul(ar, ai, sr, si)
        xr, xi = xr + mr, xi + mi
        ar, ai = _cmul(ar, ai, ar, ai)
        sh *= 2
    return xr, xi


def _gelu(y):
    c = math.sqrt(2.0 / math.pi)
    return 0.5 * y * (1.0 + jnp.tanh(c * (y + 0.044715 * y * y * y)))


def _gelu_grad(y):
    c = math.sqrt(2.0 / math.pi)
    th = jnp.tanh(c * (y + 0.044715 * y * y * y))
    return 0.5 * (1.0 + th) + 0.5 * y * (1.0 - th * th) * c * (1.0 + 3.0 * 0.044715 * y * y)


def _s5_fwd(proj, ab, bd_b, bd_c, dvec):
    rows = proj.shape[0]
    nc = rows // CHUNK
    ub = (2 * RET_QK + 2 * RET_W) // S5_TU
    ab_re, ab_im = ab
    bre, bim = bd_b
    cre, cim = bd_c

    def body(u_ref, ar_ref, ai_ref, bre_ref, bim_ref, cre_ref, cim_ref, d_ref,
             y_ref, g_ref, er_ref, ei_ref, pr_scr, pi_scr, xr_scr, xi_scr, cr_scr, ci_scr):
        n = pl.program_id(1)
        ar, ai = ar_ref[0], ai_ref[0]

        @pl.when(n == 0)
        def _():
            pr, pi = _pow_table(ar, ai, True)
            pr_scr[...] = pr
            pi_scr[...] = pi
            cr_scr[...] = jnp.zeros_like(cr_scr)
            ci_scr[...] = jnp.zeros_like(ci_scr)

        u = u_ref[...]
        er_ref[0, 0] = cr_scr[...]
        ei_ref[0, 0] = ci_scr[...]
        xr, xi = _scan_block(_dot(u, bre_ref[0]), _dot(u, bim_ref[0]), ar, ai, True)
        cr, ci = cr_scr[0:1, :], ci_scr[0:1, :]
        mr, mi = _cmul(pr_scr[...], pi_scr[...], cr, ci)
        xr, xi = xr + mr, xi + mi
        xr_scr[...] = xr
        xi_scr[...] = xi
        cr_scr[...] = jnp.broadcast_to(xr_scr[CHUNK - 1:CHUNK, :], cr_scr.shape)
        ci_scr[...] = jnp.broadcast_to(xi_scr[CHUNK - 1:CHUNK, :], ci_scr.shape)
        y = _dot(xr, cre_ref[0]) - _dot(xi, cim_ref[0]) + d_ref[...] * u
        y_ref[...] = y
        g_ref[...] = _gelu(y).astype(BF16)

    vec = pl.BlockSpec((1, 1, S5_TS), lambda t, n: (t, 0, 0))
    return pl.pallas_call(
        body, name="s5_fwd", grid=(S5_NT, nc),
        in_specs=[pl.BlockSpec((CHUNK, S5_TU), lambda t, n: (n, ub + t)), vec, vec,
                  pl.BlockSpec((1, S5_TU, S5_TS), lambda t, n: (t, 0, 0)),
                  pl.BlockSpec((1, S5_TU, S5_TS), lambda t, n: (t, 0, 0)),
                  pl.BlockSpec((1, S5_TS, S5_TU), lambda t, n: (t, 0, 0)),
                  pl.BlockSpec((1, S5_TS, S5_TU), lambda t, n: (t, 0, 0)),
                  pl.BlockSpec((1, S5_TU), lambda t, n: (0, t))],
        out_specs=[pl.BlockSpec((CHUNK, S5_TU), lambda t, n: (n, t)),
                   pl.BlockSpec((CHUNK, S5_TU), lambda t, n: (n, t)),
                   pl.BlockSpec((1, 1, 8, S5_TS), lambda t, n: (t, n, 0, 0)),
                   pl.BlockSpec((1, 1, 8, S5_TS), lambda t, n: (t, n, 0, 0))],
        out_shape=[jax.ShapeDtypeStruct((rows, S5_W), F32), jax.ShapeDtypeStruct((rows, S5_W), BF16),
                   jax.ShapeDtypeStruct((S5_NT, nc, 8, S5_TS), F32),
                   jax.ShapeDtypeStruct((S5_NT, nc, 8, S5_TS), F32)],
        scratch_shapes=[pltpu.VMEM((CHUNK, S5_TS), F32) for _ in range(4)]
        + [pltpu.VMEM((8, S5_TS), F32), pltpu.VMEM((8, S5_TS), F32)],
        compiler_params=pltpu.CompilerParams(dimension_semantics=("parallel", "arbitrary")),
    )(proj, ab_re.reshape(S5_NT, 1, S5_TS), ab_im.reshape(S5_NT, 1, S5_TS), bre, bim, cre, cim, dvec)


def _s5_bwd(proj, dy, ab, bd_b, bd_c, dvec, entry):
    rows = proj.shape[0]
    nc = rows // CHUNK
    ub = (2 * RET_QK + 2 * RET_W) // S5_TU
    ab_re, ab_im = ab
    bre, bim = bd_b
    cre, cim = bd_c
    er, ei = entry

    def rn(n):
        return nc - 1 - n

    def body(u_ref, dy_ref, ar_ref, ai_ref, bre_ref, bim_ref, cre_ref, cim_ref, d_ref, er_ref, ei_ref,
             du_ref, dbr_ref, dbi_ref, dcr_ref, dci_ref, dar_ref, dai_ref, dd_ref,
             pr_scr, pi_scr, qr_scr, qi_scr, tr_scr, ti_scr, gr_scr, gi_scr):
        n = pl.program_id(1)
        ar, ai = ar_ref[0], ai_ref[0]

        @pl.when(n == 0)
        def _():
            pr, pi = _pow_table(ar, ai, True)
            pr_scr[...] = pr
            pi_scr[...] = pi
            qr, qi = _pow_table(ar, -ai, False)
            qr_scr[...] = qr
            qi_scr[...] = qi
            gr_scr[...] = jnp.zeros_like(gr_scr)
            gi_scr[...] = jnp.zeros_like(gi_scr)
            for r in (dbr_ref, dbi_ref, dcr_ref, dci_ref, dar_ref, dai_ref, dd_ref):
                r[...] = jnp.zeros_like(r)

        u = u_ref[...]
        dy = dy_ref[...]
        row = lax.broadcasted_iota(jnp.int32, (CHUNK, 1), 0)
        xr, xi = _scan_block(_dot(u, bre_ref[0]), _dot(u, bim_ref[0]), ar, ai, True)
        cr, ci = er_ref[0, 0, 0:1, :], ei_ref[0, 0, 0:1, :]
        mr, mi = _cmul(pr_scr[...], pi_scr[...], cr, ci)
        xr, xi = xr + mr, xi + mi
        dcr_ref[0] += _dot(xr, dy, TN)
        dci_ref[0] -= _dot(xi, dy, TN)
        gr, gi = _scan_block(_dot(dy, cre_ref[0], NT), -_dot(dy, cim_ref[0], NT), ar, -ai, False)
        mr, mi = _cmul(qr_scr[...], qi_scr[...], gr_scr[0:1, :], gi_scr[0:1, :])
        gr, gi = gr + mr, gi + mi
        tr_scr[...] = gr
        ti_scr[...] = gi
        gr_scr[...] = jnp.broadcast_to(tr_scr[0:1, :], gr_scr.shape)
        gi_scr[...] = jnp.broadcast_to(ti_scr[0:1, :], gi_scr.shape)
        first = row == 0
        xpr = jnp.where(first, cr, pltpu.roll(xr, 1, 0))
        xpi = jnp.where(first, ci, pltpu.roll(xi, 1, 0))
        dar_ref[0] += jnp.sum((xpr * gr + xpi * gi).reshape(CHUNK // 8, 8, S5_TS), axis=0)
        dai_ref[0] += jnp.sum((xpr * gi - xpi * gr).reshape(CHUNK // 8, 8, S5_TS), axis=0)
        dbr_ref[0] += _dot(u, gr, TN)
        dbi_ref[0] += _dot(u, gi, TN)
        dv = d_ref[...]
        dd_ref[0] += jnp.sum((dy * u).reshape(CHUNK // 8, 8, S5_TU), axis=0)
        du = dy * dv + _dot(gr, bre_ref[0], NT) + _dot(gi, bim_ref[0], NT)
        du_ref[...] = du.astype(BF16)

    vec = pl.BlockSpec((1, 1, S5_TS), lambda t, n: (t, 0, 0))
    acc_b = pl.BlockSpec((1, S5_TU, S5_TS), lambda t, n: (t, 0, 0))
    acc_c = pl.BlockSpec((1, S5_TS, S5_TU), lambda t, n: (t, 0, 0))
    acc_a = pl.BlockSpec((1, 8, S5_TS), lambda t, n: (t, 0, 0))
    ent = pl.BlockSpec((1, 1, 8, S5_TS), lambda t, n: (t, rn(n), 0, 0))
    return pl.pallas_call(
        body, name="s5_bwd", grid=(S5_NT, nc),
        in_specs=[pl.BlockSpec((CHUNK, S5_TU), lambda t, n: (rn(n), ub + t)),
                  pl.BlockSpec((CHUNK, S5_TU), lambda t, n: (rn(n), t)), vec, vec,
                  acc_b, acc_b, acc_c, acc_c, pl.BlockSpec((1, S5_TU), lambda t, n: (0, t)), ent, ent],
        out_specs=[pl.BlockSpec((CHUNK, S5_TU), lambda t, n: (rn(n), t)), acc_b, acc_b, acc_c, acc_c, acc_a, acc_a,
                   pl.BlockSpec((1, 8, S5_TU), lambda t, n: (t, 0, 0))],
        out_shape=[jax.ShapeDtypeStruct((rows, S5_W), BF16),
                   jax.ShapeDtypeStruct((S5_NT, S5_TU, S5_TS), F32), jax.ShapeDtypeStruct((S5_NT, S5_TU, S5_TS), F32),
                   jax.ShapeDtypeStruct((S5_NT, S5_TS, S5_TU), F32), jax.ShapeDtypeStruct((S5_NT, S5_TS, S5_TU), F32),
                   jax.ShapeDtypeStruct((S5_NT, 8, S5_TS), F32), jax.ShapeDtypeStruct((S5_NT, 8, S5_TS), F32),
                   jax.ShapeDtypeStruct((S5_NT, 8, S5_TU), F32)],
        scratch_shapes=[pltpu.VMEM((CHUNK, S5_TS), F32) for _ in range(6)]
        + [pltpu.VMEM((8, S5_TS), F32), pltpu.VMEM((8, S5_TS), F32)],
        compiler_params=pltpu.CompilerParams(dimension_semantics=("parallel", "arbitrary")),
    )(proj, dy,ab_re.reshape(S5_NT, 1, S5_TS), ab_im.reshape(S5_NT, 1, S5_TS), bre, bim, cre, cim, dvec, er, ei)


def _s5_gate_bwd(dmix, g, t, proj):
    rows = g.shape[0]
    tm = _row_tile(rows, 384)
    ob = RET_W // S5_W
    zb = (2 * RET_QK + 2 * RET_W + S5_W) // S5_W

    def body(do_ref, g_ref, t_ref, z_ref, dz_ref, dt_ref, dg_ref):
        do = do_ref[...]
        gv = g_ref[...].astype(F32)
        z = z_ref[...]
        st = _sigmoid(t_ref[...])
        sg = _sigmoid(z)
        os5 = gv * st
        dz_ref[...] = (do * os5 * sg * (1.0 + z * (1.0 - sg))).astype(BF16)
        dos = do * z * sg
        dt_ref[...] = (dos * gv * st * (1.0 - st)).astype(BF16)
        dg_ref[...] = dos * st

    blk = pl.BlockSpec((tm, S5_W), lambda i: (i, 0))
    return pl.pallas_call(
        body, name="s5_gate_bwd", grid=(rows // tm,),
        in_specs=[pl.BlockSpec((tm, S5_W), lambda i: (i, ob)), blk, blk,
                  pl.BlockSpec((tm, S5_W), lambda i: (i, zb))],
        out_specs=[blk, blk, blk],
        out_shape=[jax.ShapeDtypeStruct((rows, S5_W), BF16), jax.ShapeDtypeStruct((rows, S5_W), BF16),
                   jax.ShapeDtypeStruct((rows, S5_W), F32)],
    )(dmix, g, t, proj)


def _split3(x):
    hi = x.astype(BF16)
    r = x - hi.astype(F32)
    mid = r.astype(BF16)
    lo = (r - mid.astype(F32)).astype(BF16)
    return hi, mid, lo


def _tri_sum(x, upper):
    i = lax.broadcasted_iota(jnp.int32, (CHUNK, CHUNK), 0)
    j = lax.broadcasted_iota(jnp.int32, (CHUNK, CHUNK), 1)
    tri = jnp.where((j >= i) if upper else (j <= i), 1.0, 0.0).astype(BF16)
    hi, mid, lo = _split3(x)
    return _dot(tri, lo) + _dot(tri, mid) + _dot(tri, hi)


def _gla_log_decay(gl, wg, bg, n):
    logit = _dot(gl, wg) + bg
    la = (jnp.minimum(logit, 0.0) - jnp.log(1.0 + jnp.exp(-jnp.abs(logit)))) * (1.0 / GLA_TAU)
    row = lax.broadcasted_iota(jnp.int32, (CHUNK, 1), 0)
    live = jnp.logical_or(n > 0, row >= PAD)
    return logit, jnp.where(live, la, 0.0), live


def _gla_in_specs(rev, nc):
    def cn(n):
        return (nc - 1 - n) if rev else n
    kb = GLA_QK // GLA_DK
    vb = 2 * GLA_QK // GLA_DV
    zb = (2 * GLA_QK + GLA_W) // GLA_DV
    gb = (2 * GLA_QK + 2 * GLA_W) // 128
    return [
        pl.BlockSpec((CHUNK, GLA_DK), lambda h, n: (cn(n), h)),
        pl.BlockSpec((CHUNK, GLA_DK), lambda h, n: (cn(n), kb + h)),
        pl.BlockSpec((CHUNK, GLA_DV), lambda h, n: (cn(n), vb + h)),
        pl.BlockSpec((CHUNK, GLA_DV), lambda h, n: (cn(n), zb + h)),
        pl.BlockSpec((CHUNK, 128), lambda h, n: (cn(n), gb)),
        pl.BlockSpec((128, GLA_DK), lambda h, n: (0, h)),
        pl.BlockSpec((1, GLA_DK), lambda h, n: (0, h)),
        pl.BlockSpec((1, GLA_DV), lambda h, n: (0, h)),
    ]


def _gla_fwd(proj, wgate, bgate, normw):
    rows = proj.shape[0]
    nc = rows // CHUNK

    def body(q_ref, k_ref, v_ref, z_ref, gl_ref, wg_ref, bg_ref, w_ref, o_ref, oc_ref, st_ref, s_scr, o_scr, b_scr):
        n = pl.program_id(1)

        @pl.when(n == 0)
        def _():
            s_scr[...] = jnp.zeros_like(s_scr)

        q = q_ref[...] * (GLA_DK ** -0.5)
        k = k_ref[...]
        v = v_ref[...]
        vb = v.astype(BF16)
        _, la, _ = _gla_log_decay(gl_ref[...], wg_ref[...], bg_ref[...], n)
        b = _tri_sum(la, False)
        b_scr[...] = b
        b_last = b_scr[CHUNK - 1:CHUNK, :]
        st = s_scr[...]
        st_ref[0, 0] = st
        s_scr[...] = st * jnp.exp(b_last) + _dot(v, k * jnp.exp(b_last - b), TN)
        rowc = lax.broadcasted_iota(jnp.int32, (CHUNK, 1), 0)
        rows16 = lax.broadcasted_iota(jnp.int32, (SUB, 1), 0)
        a_tot = jnp.zeros((CHUNK, CHUNK), F32)
        for s in range(1, NSUB):
            lo = s * SUB
            bref = b_scr[lo - 1:lo, :]
            in_s = jnp.logical_and(rowc >= lo, rowc < lo + SUB)
            qh = q * jnp.exp(jnp.where(in_s, b - bref, -1e30))
            kh = k * jnp.exp(jnp.where(rowc < lo, bref - b, -1e30))
            a_tot = a_tot + _dot(qh, kh, NT)
        o_scr[...] = _dot(q * jnp.exp(b), st, NT) + _dot(a_tot, vb)
        for s in range(NSUB):
            lo = s * SUB
            qs, bs = q[lo:lo + SUB], b[lo:lo + SUB]
            acc = jnp.zeros((SUB, GLA_DV), F32)
            for j in range(SUB):
                r = lo + j
                e = jnp.exp(jnp.where(rows16 >= j, bs - b_scr[r:r + 1, :], -1e30))
                col = jnp.sum(qs * k_ref[r:r + 1, :] * e, axis=1, keepdims=True)
                acc = acc + col * v_ref[r:r + 1, :]
            o_scr[lo:lo + SUB, :] += acc
        o = o_scr[...]
        o_ref[...] = o
        oc_ref[...] = _gate_fwd(o, z_ref[...], w_ref[...]).astype(BF16)

    return pl.pallas_call(
        body, name="gla_fwd", grid=(GLA_HEADS, nc),
        in_specs=_gla_in_specs(False, nc),
        out_specs=[pl.BlockSpec((CHUNK, GLA_DV), lambda h, n: (n, h)),
                   pl.BlockSpec((CHUNK, GLA_DV), lambda h, n: (n, h)),
                   pl.BlockSpec((1, 1, GLA_DV, GLA_DK), lambda h, n: (h, n, 0, 0))],
        out_shape=[jax.ShapeDtypeStruct((rows, GLA_W), F32), jax.ShapeDtypeStruct((rows, GLA_W), BF16),
                   jax.ShapeDtypeStruct((GLA_HEADS, nc, GLA_DV, GLA_DK), F32)],
        scratch_shapes=[pltpu.VMEM((GLA_DV, GLA_DK), F32), pltpu.VMEM((CHUNK, GLA_DV), F32),
                        pltpu.VMEM((CHUNK, GLA_DK), F32)],
        compiler_params=pltpu.CompilerParams(dimension_semantics=("parallel", "arbitrary")),
    )(proj, proj, proj, proj, proj, wgate, bgate, normw)


def _gla_bwd(proj, wgate, bgate, normw, o_gla, d_oc, states):
    rows = proj.shape[0]
    nc = rows // CHUNK

    def rn(n):
        return nc - 1 - n

    def body(q_ref, k_ref, v_ref, z_ref, gl_ref, wg_ref, bg_ref, w_ref, o_ref, do_ref, st_ref,
             dq_ref, dk_ref, dv_ref, dz_ref, dl_ref, dw_ref, dbg_ref,
             ds_scr, dq_scr, dk_scr, dv_scr, db_scr, b_scr):
        n = pl.program_id(1)
        cn = rn(n)

        @pl.when(n == 0)
        def _():
            ds_scr[...] = jnp.zeros_like(ds_scr)
            dw_ref[...] = jnp.zeros_like(dw_ref)
            dbg_ref[...] = jnp.zeros_like(dbg_ref)

        q = q_ref[...] * (GLA_DK ** -0.5)
        k = k_ref[...]
        v = v_ref[...]
        vb = v.astype(BF16)
        do, dz, dw = _gate_bwd(do_ref[...], o_ref[...], z_ref[...], w_ref[...])
        dz_ref[...] = dz.astype(BF16)
        dw_ref[0] += dw
        logit, la, live = _gla_log_decay(gl_ref[...], wg_ref[...], bg_ref[...], cn)
        b = _tri_sum(la, False)
        b_scr[...] = b
        b_last = b_scr[CHUNK - 1:CHUNK, :]
        e_last = jnp.exp(b_last)
        st = st_ref[0, 0]
        g1 = ds_scr[...]
        eb = jnp.exp(b)
        qe = q * eb
        dqe = _dot(do, st)
        dq_scr[...] = dqe * eb
        db_scr[...] = dqe * qe
        ekb = jnp.exp(b_last - b)
        kdec = k * ekb
        dkdec = _dot(v, g1)
        dv_scr[...] = _dot(kdec, g1, NT)
        dk_scr[...] = dkdec * ekb
        wk = dkdec * kdec
        db_scr[...] -= wk
        dbl = jnp.sum(wk, axis=0, keepdims=True) + jnp.sum(g1 * st, axis=0, keepdims=True) * e_last
        ds_scr[...] = g1 * e_last + _dot(do, qe, TN)
        rowc = lax.broadcasted_iota(jnp.int32, (CHUNK, 1), 0)
        rows16 = lax.broadcasted_iota(jnp.int32, (SUB, 1), 0)
        da_full = _dot(do, vb, NT)
        a_tot = jnp.zeros((CHUNK, CHUNK), F32)
        for s in range(1, NSUB):
            lo = s * SUB
            bref = b_scr[lo - 1:lo, :]
            in_s = jnp.logical_and(rowc >= lo, rowc < lo + SUB)
            eq = jnp.exp(jnp.where(in_s, b - bref, -1e30))
            ek = jnp.exp(jnp.where(rowc < lo, bref - b, -1e30))
            qh = q * eq
            kh = k * ek
            a_tot = a_tot + _dot(qh, kh, NT)
            da = jnp.where(in_s, da_full, 0.0)
            dqh = _dot(da, kh)
            dkh = _dot(da, qh, TN)
            tq = dqh * qh
            tk = dkh * kh
            dq_scr[...] += dqh * eq
            dk_scr[...] += dkh * ek
            db_scr[...] += tq - tk
            db_scr[lo - 1:lo, :] += jnp.sum(tk, axis=0, keepdims=True) - jnp.sum(tq, axis=0, keepdims=True)
        dv_scr[...] += _dot(a_tot, do, TN)
        for s in range(NSUB):
            lo = s * SUB
            qs, bs = q[lo:lo + SUB], b[lo:lo + SUB]
            dos = do[lo:lo + SUB]
            dqs = jnp.zeros((SUB, GLA_DK), F32)
            dks = jnp.zeros((SUB, GLA_DK), F32)
            dbs = jnp.zeros((SUB, GLA_DK), F32)
            dvs = jnp.zeros((SUB, GLA_DV), F32)
            for j in range(SUB):
                pick = rows16 == j
                r = lo + j
                kj, vj, bj = k_ref[r:r + 1, :], v_ref[r:r + 1, :], b_scr[r:r + 1, :]
                e = jnp.exp(jnp.where(rows16 >= j, bs - bj, -1e30))
                qe_j = qs * e
                col = jnp.sum(qe_j * kj, axis=1, keepdims=True)
                dcol = jnp.sum(dos * vj, axis=1, keepdims=True)
                dvs = dvs + jnp.where(pick, jnp.sum(col * dos, axis=0, keepdims=True), 0.0)
                m = dcol * e
                dqs = dqs + m * kj
                mq = m * qs
                dks = dks + jnp.where(pick, jnp.sum(mq, axis=0, keepdims=True), 0.0)
                t = mq * kj
                dbs = dbs + t - jnp.where(pick, jnp.sum(t, axis=0, keepdims=True), 0.0)
            dq_scr[lo:lo + SUB, :] += dqs
            dk_scr[lo:lo + SUB, :] += dks
            dv_scr[lo:lo + SUB, :] += dvs
            db_scr[lo:lo + SUB, :] += dbs
        db_scr[CHUNK - 1:CHUNK, :] += dbl
        dla = _tri_sum(db_scr[...], True)
        dlogit = jnp.where(live, dla * (1.0 / GLA_TAU) * _sigmoid(-logit), 0.0)
        dl_ref[...] = dlogit
        dbg_ref[0] += jnp.sum(dlogit, axis=0, keepdims=True)
        dq_ref[...] = (dq_scr[...] * (GLA_DK ** -0.5)).astype(BF16)
        dk_ref[...] = dk_scr[...].astype(BF16)
        dv_ref[...] = dv_scr[...].astype(BF16)

    in_specs = _gla_in_specs(True, nc) + [
        pl.BlockSpec((CHUNK, GLA_DV), lambda h, n: (rn(n), h)),
        pl.BlockSpec((CHUNK, GLA_DV), lambda h, n: (rn(n), h)),
        pl.BlockSpec((1, 1, GLA_DV, GLA_DK), lambda h, n: (h, rn(n), 0, 0)),
    ]
    return pl.pallas_call(
        body, name="gla_bwd", grid=(GLA_HEADS, nc),
        in_specs=in_specs,
        out_specs=[pl.BlockSpec((CHUNK, GLA_DK), lambda h, n: (rn(n), h)),
                   pl.BlockSpec((CHUNK, GLA_DK), lambda h, n: (rn(n), h)),
                   pl.BlockSpec((CHUNK, GLA_DV), lambda h, n: (rn(n), h)),
                   pl.BlockSpec((CHUNK, GLA_DV), lambda h, n: (rn(n), h)),
                   pl.BlockSpec((CHUNK, GLA_DK), lambda h, n: (rn(n), h)),
                   pl.BlockSpec((1, 1, GLA_DV), lambda h, n: (h, 0, 0)),
                   pl.BlockSpec((1, 1, GLA_DK), lambda h, n: (h, 0, 0))],
        out_shape=[jax.ShapeDtypeStruct((rows, GLA_QK), BF16), jax.ShapeDtypeStruct((rows, GLA_QK), BF16),
                   jax.ShapeDtypeStruct((rows, GLA_W), BF16), jax.ShapeDtypeStruct((rows, GLA_W), BF16),
                   jax.ShapeDtypeStruct((rows, GLA_QK), F32),
                   jax.ShapeDtypeStruct((GLA_HEADS, 1, GLA_DV), F32),
                   jax.ShapeDtypeStruct((GLA_HEADS, 1, GLA_DK), F32)],
        scratch_shapes=[pltpu.VMEM((GLA_DV, GLA_DK), F32), pltpu.VMEM((CHUNK, GLA_DK), F32),
                        pltpu.VMEM((CHUNK, GLA_DK), F32), pltpu.VMEM((CHUNK, GLA_DV), F32),
                        pltpu.VMEM((CHUNK, GLA_DK), F32), pltpu.VMEM((CHUNK, GLA_DK), F32)],
        compiler_params=pltpu.CompilerParams(dimension_semantics=("parallel", "arbitrary")),
    )(proj, proj, proj, proj, proj, wgate, bgate, normw, o_gla, d_oc, states)


def _adamw(name, w, g, m, v):
    rows, cols = w.shape
    tm = rows
    for cand in (256, 128, 64, 32, 16, 8):
        if rows % cand == 0:
            tm = cand
            break
    c1 = 1.0 - ADAM_B1 ** ADAM_STEP
    c2 = 1.0 - ADAM_B2 ** ADAM_STEP

    def body(w_ref, g_ref, m_ref, v_ref, d_ref, nm_ref, nv_ref):
        gv = g_ref[...]
        nm = ADAM_B1 * m_ref[...] + (1.0 - ADAM_B1) * gv
        nv = ADAM_B2 * v_ref[...] + (1.0 - ADAM_B2) * (gv * gv)
        nm_ref[...] = nm
        nv_ref[...] = nv
        d_ref[...] = -ADAM_LR * ((nm / c1) / (jnp.sqrt(nv / c2) + ADAM_EPS) + ADAM_WD * w_ref[...])

    blk = pl.BlockSpec((tm, cols), lambda i: (i, 0))
    return pl.pallas_call(
        body, name=name, grid=(rows // tm,),
        in_specs=[blk] * 4, out_specs=[blk] * 3,
        out_shape=[jax.ShapeDtypeStruct((rows, cols), F32)] * 3,
    )(w, g, m, v)


def _place():
    x, y, c = lax.axis_index("x"), lax.axis_index("y"), lax.axis_index("c")
    chips = [(1 - x, y), (x, 1 - y), (1 - x, 1 - y)]
    return x, y, c, chips


ANY = pl.BlockSpec(memory_space=pl.ANY)


def _gather_weights(shards, kinds):
    n_arr = len(shards)

    def out_struct(a, kind):
        r, cc = a.shape
        if kind == "row":
            return jax.ShapeDtypeStruct((N_SHARD * r, cc), a.dtype)
        if kind == "col":
            return jax.ShapeDtypeStruct((r, N_SHARD * cc), a.dtype)
        return jax.ShapeDtypeStruct((N_SHARD, r, cc), a.dtype)

    def body(*refs):
        ins = refs[:n_arr]
        outs = refs[n_arr:2 * n_arr]
        send_sems, recv_sems, local_sems = refs[2 * n_arr:]
        x, y, c, chips = _place()
        mine = 2 * x + y
        sibling = (x, y, 1 - c)

        def window(i, shard, half):
            r, cc = shards[i].shape
            hr = r // 2
            if kinds[i] == "row":
                return outs[i].at[pl.ds(_mo(shard * r + half * hr, 8), hr), :]
            if kinds[i] == "col":
                return outs[i].at[pl.ds(_mo(half * hr, 8), hr), pl.ds(_mo(shard * cc, 128), cc)]
            return outs[i].at[shard, pl.ds(_mo(half * hr, 8), hr), :]

        def src_half(i, half):
            hr = shards[i].shape[0] // 2
            return ins[i].at[pl.ds(_mo(half * hr, 8), hr), :]

        def copy(i, slot, src, dst, to):
            return pltpu.make_async_remote_copy(
                src_ref=src, dst_ref=dst, send_sem=send_sems.at[i, slot], recv_sem=recv_sems.at[i, slot],
                device_id=to, device_id_type=MESH)

        local = []
        for i in range(n_arr):
            for half in range(2):
                cp = pltpu.make_async_copy(src_half(i, half), window(i, mine, half), local_sems.at[i, half])
                cp.start()
                local.append(cp)
        first = []
        for i in range(n_arr):
            for j, chip in enumerate(chips):
                cp = copy(i, j, src_half(i, c), window(i, mine, c), (*chip, c))
                cp.start()
                first.append(cp)
        passed = []
        for i in range(n_arr):
            for j, chip in enumerate(chips):
                theirs = 2 * chip[0] + chip[1]
                copy(i, j, src_half(i, c), window(i, theirs, c), (*chip, c)).wait_recv()
                cp = copy(i, 3 + j, window(i, theirs, c), window(i, theirs, c), sibling)
                cp.start()
                passed.append(cp)
        for i in range(n_arr):
            for j, chip in enumerate(chips):
                theirs = 2 * chip[0] + chip[1]
                copy(i, 3 + j, window(i, theirs, 1 - c), window(i, theirs, 1 - c), sibling).wait_recv()
        for cp in first + passed:
            cp.wait_send()
        for cp in local:
            cp.wait()

    return pl.pallas_call(
        body, name="gather_weights",
        in_specs=[ANY] * n_arr, out_specs=[ANY] * n_arr,
        out_shape=[out_struct(a, kd) for a, kd in zip(shards, kinds)],
        scratch_shapes=[pltpu.SemaphoreType.DMA((n_arr, 6)), pltpu.SemaphoreType.DMA((n_arr, 6)),
                        pltpu.SemaphoreType.DMA((n_arr, 2))],
        compiler_params=pltpu.CompilerParams(has_side_effects=True),
    )(*shards)


def _allreduce_small(buf):
    rows, cols = buf.shape

    def body(in_ref, out_ref, sib_ref, pair_ref, far_ref, send_sems, recv_sems):
        x, y, c, chips = _place()
        sibling = (x, y, 1 - c)
        to_sib = pltpu.make_async_remote_copy(
            src_ref=in_ref, dst_ref=sib_ref, send_sem=send_sems.at[0], recv_sem=recv_sems.at[0],
            device_id=sibling, device_id_type=MESH)
        to_sib.start()
        to_sib.wait()
        pair_ref[...] = in_ref[...] + sib_ref[...]
        far = [pltpu.make_async_remote_copy(
            src_ref=pair_ref, dst_ref=far_ref.at[j], send_sem=send_sems.at[1 + j], recv_sem=recv_sems.at[1 + j],
            device_id=(*chip, c), device_id_type=MESH) for j, chip in enumerate(chips)]
        for cp in far:
            cp.start()
        for cp in far:
            cp.wait()
        out_ref[...] = (pair_ref[...] + far_ref[1]) + (far_ref[0] + far_ref[2])

    vm = pl.BlockSpec(memory_space=pltpu.VMEM)
    return pl.pallas_call(
        body, name="allreduce_small",
        in_specs=[vm], out_specs=vm,
        out_shape=jax.ShapeDtypeStruct((rows, cols), F32),
        scratch_shapes=[pltpu.VMEM((rows, cols), F32), pltpu.VMEM((rows, cols), F32),
                        pltpu.VMEM((3, rows, cols), F32),
                        pltpu.SemaphoreType.DMA((4,)), pltpu.SemaphoreType.DMA((4,))],
        compiler_params=pltpu.CompilerParams(has_side_effects=True),
    )(buf)


def _shard_window(ref, kind, shard_shape, shard, half):
    r, cc = shard_shape
    hr = r // 2
    if kind == "row":
        return ref.at[pl.ds(_mo(shard * r + half * hr, 8), hr), :]
    if kind == "col":
        return ref.at[pl.ds(_mo(half * hr, 8), hr), pl.ds(_mo(shard * cc, 128), cc)]
    return ref.at[shard, pl.ds(_mo(half * hr, 8), hr), :]


def _rs_pair_exchange(grads, kinds, shard_shapes):
    n_arr = len(grads)

    def body(*refs):
        ins = refs[:n_arr]
        outs = refs[n_arr:2 * n_arr]
        send_sems, recv_sems = refs[2 * n_arr:]
        x, y, c, _ = _place()
        sibling = (x, y, 1 - c)
        cps = []
        for i in range(n_arr):
            for s in range(N_SHARD):
                cp = pltpu.make_async_remote_copy(
                    src_ref=_shard_window(ins[i], kinds[i], shard_shapes[i], s, 1 - c), dst_ref=outs[i].at[s],
                    send_sem=send_sems.at[i, s], recv_sem=recv_sems.at[i, s],
                    device_id=sibling, device_id_type=MESH)
                cp.start()
                cps.append(cp)
        for cp in cps:
            cp.wait()

    return pl.pallas_call(
        body, name="rs_pair_exchange",
        in_specs=[ANY] * n_arr, out_specs=[ANY] * n_arr,
        out_shape=[jax.ShapeDtypeStruct((N_SHARD, r // 2, cc), F32) for (r, cc) in shard_shapes],
        scratch_shapes=[pltpu.SemaphoreType.DMA((n_arr, N_SHARD)), pltpu.SemaphoreType.DMA((n_arr, N_SHARD))],
        compiler_params=pltpu.CompilerParams(has_side_effects=True),
    )(*grads)


def _rs_pair_add(name, grad, got, kind, shard_shape, c):
    r, cc = shard_shape
    hr = r // 2
    tr = hr
    for cand in (256, 128, 64, 32, 16):
        if hr % cand == 0:
            tr = cand
            break
    nb = hr // tr

    if kind == "row":
        g_spec = pl.BlockSpec((tr, cc), lambda s, i, cr: (s * 2 * nb + cr[0] * nb + i, 0))
    elif kind == "col":
        g_spec = pl.BlockSpec((tr, cc), lambda s, i, cr: (cr[0] * nb + i, s))
    else:
        g_spec = pl.BlockSpec((None, tr, cc), lambda s, i, cr: (s, cr[0] * nb + i, 0))
    t_spec = pl.BlockSpec((None, tr, cc), lambda s, i, cr: (s, i, 0))

    def body(c_ref, g_ref, t_ref, p_ref, pb_ref):
        p = g_ref[...] + t_ref[...]
        p_ref[...] = p
        pb_ref[...] = p.astype(BF16)

    return pl.pallas_call(
        body, name=name,
        grid_spec=pltpu.PrefetchScalarGridSpec(
            num_scalar_prefetch=1, grid=(N_SHARD, nb),
            in_specs=[g_spec, t_spec], out_specs=[t_spec, t_spec]),
        out_shape=[jax.ShapeDtypeStruct((N_SHARD, hr, cc), F32), jax.ShapeDtypeStruct((N_SHARD, hr, cc), BF16)],
    )(c, grad, got)


def _rs_chip_exchange(pairs_bf16):
    n_arr = len(pairs_bf16)

    def body(*refs):
        ins = refs[:n_arr]
        outs = refs[n_arr:2 * n_arr]
        send_sems, recv_sems = refs[2 * n_arr:]
        x, y, c, chips = _place()
        cps = []
        for i in range(n_arr):
            for j, chip in enumerate(chips):
                cp = pltpu.make_async_remote_copy(
                    src_ref=ins[i].at[2 * chip[0] + chip[1]], dst_ref=outs[i].at[j],
                    send_sem=send_sems.at[i, j], recv_sem=recv_sems.at[i, j],
                    device_id=(*chip, c), device_id_type=MESH)
                cp.start()
                cps.append(cp)
        for cp in cps:
            cp.wait()

    return pl.pallas_call(
        body, name="rs_chip_exchange",
        in_specs=[ANY] * n_arr, out_specs=[ANY] * n_arr,
        out_shape=[jax.ShapeDtypeStruct((3,) + a.shape[1:], BF16) for a in pairs_bf16],
        scratch_shapes=[pltpu.SemaphoreType.DMA((n_arr, 3)), pltpu.SemaphoreType.DMA((n_arr, 3))],
        compiler_params=pltpu.CompilerParams(has_side_effects=True),
    )(*pairs_bf16)


def _rs_chip_add(name, pair_f32, got, shard_shape, mine_c):
    r, cc = shard_shape
    hr = r // 2
    tr = hr
    for cand in (256, 128, 64, 32, 16):
        if hr % cand == 0:
            tr = cand
            break
    nb = hr // tr

    def body(mc_ref, p_ref, t0_ref, t1_ref, t2_ref, o_ref):
        o_ref[...] = (p_ref[...] + t1_ref[...].astype(F32)) + (t0_ref[...].astype(F32) + t2_ref[...].astype(F32))

    def far(j):
        return pl.BlockSpec((None, tr, cc), lambda i, mc: (j, i, 0))

    return pl.pallas_call(
        body, name=name,
        grid_spec=pltpu.PrefetchScalarGridSpec(
            num_scalar_prefetch=1, grid=(nb,),
            in_specs=[pl.BlockSpec((None, tr, cc), lambda i, mc: (mc[0], i, 0)), far(0), far(1), far(2)],
            out_specs=pl.BlockSpec((tr, cc), lambda i, mc: (mc[1] * nb + i, 0))),
        out_shape=jax.ShapeDtypeStruct((r, cc), F32),
    )(mine_c, pair_f32, got, got, got)


def _rs_pair_share(halves, shard_shapes):
    n_arr = len(halves)

    def body(*refs):
        ins = refs[:n_arr]
        outs = refs[n_arr:2 * n_arr]
        send_sems, recv_sems = refs[2 * n_arr:]
        x, y, c, _ = _place()
        sibling = (x, y, 1 - c)
        cps = []
        for i in range(n_arr):
            hr = shard_shapes[i][0] // 2
            rows = pl.ds(_mo(c * hr, 8), hr)
            cp = pltpu.make_async_remote_copy(
                src_ref=outs[i].at[rows, :], dst_ref=outs[i].at[rows, :],
                send_sem=send_sems.at[i], recv_sem=recv_sems.at[i],
                device_id=sibling, device_id_type=MESH)
            cp.start()
            cps.append(cp)
        for cp in cps:
            cp.wait()

    return pl.pallas_call(
        body, name="rs_pair_share",
        in_specs=[ANY] * n_arr, out_specs=[ANY] * n_arr,
        out_shape=[jax.ShapeDtypeStruct(s, F32) for s in shard_shapes],
        input_output_aliases={i: i for i in range(n_arr)},
        scratch_shapes=[pltpu.SemaphoreType.DMA((n_arr,)), pltpu.SemaphoreType.DMA((n_arr,))],
        compiler_params=pltpu.CompilerParams(has_side_effects=True),
    )(*halves)


def _pack(arrays):
    flat = []
    for a in arrays:
        v = a.reshape(-1).astype(F32)
        flat.append(jnp.pad(v, (0, (-v.shape[0]) % SMALL_COLS)))
    buf = jnp.concatenate(flat).reshape(-1, SMALL_COLS)
    return jnp.pad(buf, ((0, (-buf.shape[0]) % 8), (0, 0)))


def _unpack(buf, shapes):
    out = []
    row = 0
    for s in shapes:
        size = math.prod(s)
        nrow = -(-size // SMALL_COLS)
        out.append(buf[row:row + nrow].reshape(-1)[:size].reshape(s))
        row += nrow
    return out


def kernel(x, meta, norm_ab_w, w_in_ab, ret_norm_w, s5_lam_re, s5_lam_im, s5_log_dt, s5_b_re, s5_b_im, s5_c_re, s5_c_im, s5_d, s5_w_glu, w_out_ab, norm_c_w, w_in_c, gla_w_gate, gla_b_gate, gla_norm_w, w_out_c, final_norm_w, loss_target, m_meta, m_norm_ab_w, m_w_in_ab, m_ret_norm_w, m_s5_lam_re, m_s5_lam_im, m_s5_log_dt, m_s5_b_re, m_s5_b_im, m_s5_c_re, m_s5_c_im, m_s5_d, m_s5_w_glu, m_w_out_ab, m_norm_c_w, m_w_in_c, m_gla_w_gate, m_gla_b_gate, m_gla_norm_w, m_w_out_c, m_final_norm_w, v_meta, v_norm_ab_w, v_w_in_ab, v_ret_norm_w, v_s5_lam_re, v_s5_lam_im, v_s5_log_dt, v_s5_b_re, v_s5_b_im, v_s5_c_re, v_s5_c_im, v_s5_d, v_s5_w_glu, v_w_out_ab, v_norm_c_w, v_w_in_c, v_gla_w_gate, v_gla_b_gate, v_gla_norm_w, v_w_out_c, v_final_norm_w):
    seq = x.shape[1]
    rows = seq + CHUNK
    xi, yi, ci = lax.axis_index("x"), lax.axis_index("y"), lax.axis_index("c")
    mine = 2 * xi + yi
    c_arr = jnp.reshape(ci, (1,)).astype(jnp.int32)
    mine_c = jnp.stack([mine, ci]).astype(jnp.int32)

    small_shard = _pack([meta, norm_c_w, gla_norm_w, gla_b_gate, gla_w_gate[0]])
    srows = small_shard.shape[0]
    big = [w_in_ab[0].astype(BF16), w_out_ab[0].astype(BF16), w_in_c[0].astype(BF16),
           w_out_c[0].astype(BF16), s5_w_glu[0].astype(BF16)]
    kinds = ["col", "row", "stack", "row", "row"]
    wab, wout_ab, wc_st, wout_c, wglu, small_all = _gather_weights(big + [small_shard], kinds + ["stack"])
    wc = jnp.concatenate([wc_st[j] for j in range(N_SHARD)] + [jnp.zeros((D_MODEL, IN_C_PAD - IN_C), BF16)], axis=1)
    q4 = D_MODEL // N_SHARD
    g4 = GLA_QK // N_SHARD
    parts = [_unpack(small_all[j], [(N_META, q4), (1, q4), (1, q4), (1, g4), (GLA_RANK, g4)]) for j in range(N_SHARD)]
    meta_f, norm_c_f, gla_norm_f, bgate_f, wgate_f = [jnp.concatenate([p[i] for p in parts], axis=1) for i in range(5)]
    wgate_pad = jnp.pad(wgate_f, ((0, 128 - GLA_RANK), (0, 0)))

    h0 = jnp.concatenate([jnp.zeros((PAD, D_MODEL), F32), meta_f, x[0]], axis=0)
    cosf, sinf = _rope_tables(rows)
    rtab = _ret_tables()
    ab_re, ab_im, bb_re, bb_im = _s5_discretize(s5_lam_re[0], s5_lam_im[0], s5_log_dt[0], s5_b_re[0], s5_b_im[0])
    ab = (ab_re, ab_im)
    bd_b = (_bdiag_in(bb_re), _bdiag_in(bb_im))
    bd_c = (_bdiag_out(s5_c_re[0]), _bdiag_out(s5_c_im[0]))

    tm = _row_tile(rows, 1408)
    tmk = _row_tile(rows, 1408)
    hn0 = _rms_fwd("norm_ab", h0, norm_ab_w)
    proj0 = _matmul("in_proj_ab", hn0, wab, NN, rows, IN_AB, D_MODEL, tm=tm, tn=512, tk=D_MODEL)
    o_ret, o_a, ret_states = _ret_fwd(proj0, cosf, sinf, rtab, ret_norm_w)
    y_s5, g_s5, s5_er, s5_ei = _s5_fwd(proj0, ab, bd_b, bd_c, s5_d)
    zb_blk = (2 * RET_QK + 2 * RET_W + S5_W) // 512

    def glu_out(acc, gv, z):
        return gv.astype(F32) * _sigmoid(acc) * (z * _sigmoid(z))

    t_glu = _matmul("glu", g_s5, wglu, NN, rows, S5_W, S5_W, tm=tm, tn=512, tk=S5_W)
    o_b = _matmul("glu_out", g_s5, wglu, NN, rows, S5_W, S5_W, tm=tm, tn=512, tk=S5_W, out_dtype=BF16,
                  extras=[(g_s5, (tm, 512), lambda i, j, kk: (i, j)),
                          (proj0, (tm, 512), lambda i, j, kk: (i, zb_blk + j))],
                  epilogue=glu_out)
    mix = jnp.concatenate([o_a, o_b], axis=1)
    h1 = _matmul("out_proj_ab", mix, wout_ab, NN, rows, D_MODEL, OUT_AB, tm=tm, tn=512, tk=1024,
                 extras=[(h0, (tm, 512), lambda i, j, kk: (i, j))], epilogue=lambda acc, r: acc + r)

    hn1 = _rms_fwd("norm_c", h1, norm_c_f)
    proj1 = _matmul("in_proj_c", hn1, wc, NN, rows, IN_C_PAD, D_MODEL, tm=tm, tn=896, tk=D_MODEL)
    o_gla, o_c, gla_states = _gla_fwd(proj1, wgate_pad, bgate_f, gla_norm_f)
    h2 = _matmul("out_proj_c", o_c, wout_c, NN, rows, D_MODEL, GLA_W, tm=tm, tn=512, tk=GLA_W,
                 extras=[(h1, (tm, 512), lambda i, j, kk: (i, j))], epilogue=lambda acc, r: acc + r)
    loss_dev, dh2, d_final = _final_loss(h2, final_norm_w.reshape(1, D_MODEL), loss_target[0])

    g_wout_c = _matmul("d_w_out_c", o_c, dh2, TN, GLA_W, D_MODEL, rows, tm=1024, tn=1024, tk=tmk)
    d_oc = _matmul("d_o_c", dh2, wout_c, NT, rows, GLA_W, D_MODEL, tm=tm, tn=512, tk=1024)
    dq1, dk1, dv1, dz1, dlogit, d_gla_norm, d_bgate = _gla_bwd(proj1, wgate_pad, bgate_f, gla_norm_f, o_gla, d_oc, gla_states)
    gl_blk = (2 * GLA_QK + 2 * GLA_W) // 128
    dgl = _matmul("d_g_low", dlogit, wgate_pad, NT, rows, 128, GLA_QK, tm=tm, tn=128, tk=GLA_QK, out_dtype=BF16)
    g_wgate = _matmul("d_w_gate", proj1, dlogit, TN, 128, GLA_QK, rows, tm=128, tn=GLA_QK, tk=tmk, a_off=(0, gl_blk))
    dproj1 = jnp.concatenate([dq1, dk1, dv1, dz1, dgl], axis=1)
    g_wc = _matmul("d_w_in_c", hn1, dproj1, TN, D_MODEL, IN_C_PAD, rows, tm=1024, tn=896, tk=tmk)
    dhn1 = _matmul("d_hn1", dproj1, wc, NT, rows, D_MODEL, IN_C_PAD, tm=tm, tn=512, tk=896)
    dh1, d_norm_c = _rms_bwd("norm_c_bwd", dhn1, h1, norm_c_f, dh2)

    g_wout_ab = _matmul("d_w_out_ab", mix, dh1, TN, OUT_AB, D_MODEL, rows, tm=1024, tn=1024, tk=tmk)
    dmix = _matmul("d_mix", dh1, wout_ab, NT, rows, OUT_AB, D_MODEL, tm=tm, tn=512, tk=1024)
    dq0, dk0, dv0, dza, d_ret_norm = _ret_bwd(proj0, cosf, sinf, rtab, ret_norm_w, o_ret, dmix, ret_states)
    dzb, dt_glu, dg_direct = _s5_gate_bwd(dmix, g_s5, t_glu, proj0)
    g_wglu = _matmul("d_w_glu", g_s5, dt_glu, TN, S5_W, S5_W, rows, tm=1024, tn=1024, tk=tmk)
    dy_s5 = _matmul("d_y_s5", dt_glu, wglu, NT, rows, S5_W, S5_W, tm=tm, tn=512, tk=S5_W,
                    extras=[(dg_direct, (tm, 512), lambda i, j, kk: (i, j)),
                            (y_s5, (tm, 512), lambda i, j, kk: (i, j))],
                    epilogue=lambda acc, dg, yv: (acc + dg) * _gelu_grad(yv))
    du, dbr_d, dbi_d, dcr_d, dci_d, dar_p, dai_p, dd_p = _s5_bwd(proj0, dy_s5, ab, bd_b, bd_c, s5_d, (s5_er, s5_ei))
    dproj0 = jnp.concatenate([dq0, dk0, dv0, dza, du, dzb], axis=1)
    g_wab = _matmul("d_w_in_ab", hn0, dproj0, TN, D_MODEL, IN_AB, rows, tm=1024, tn=1024, tk=tmk)
    dhn0 = _matmul("d_hn0", dproj0, wab, NT, rows, D_MODEL, IN_AB, tm=tm, tn=512, tk=2048)
    dh0, d_norm_ab = _rms_bwd("norm_ab_bwd", dhn0, h0, norm_ab_w, dh1)
    grad_x = dh0[CHUNK:][None]

    d_ab_re = jnp.sum(dar_p, axis=1).reshape(S5_G, S5_P)
    d_ab_im = jnp.sum(dai_p, axis=1).reshape(S5_G, S5_P)
    small_local = [loss_dev, dh0[PAD:CHUNK], d_norm_ab, d_ret_norm.reshape(1, RET_W), d_ab_re, d_ab_im,
                   _bdiag_in_extract(dbr_d), _bdiag_in_extract(dbi_d),
                   _bdiag_out_extract(dcr_d), _bdiag_out_extract(dci_d),
                   jnp.sum(dd_p, axis=1).reshape(1, S5_W), d_norm_c, g_wgate[:GLA_RANK],
                   d_bgate.reshape(1, GLA_QK), d_gla_norm.reshape(1, GLA_W), d_final]
    small_shapes = [a.shape for a in small_local]
    summed = _unpack(_allreduce_small(_pack(small_local)), small_shapes)
    (loss, g_meta_f, g_norm_ab, g_ret_norm, g_ab_re, g_ab_im, g_bb_re, g_bb_im, g_c_re, g_c_im, g_d,
     g_norm_c_f, g_wgate_f, g_bgate_f, g_gla_norm_f, g_final) = summed
    _, s5_vjp = jax.vjp(_s5_discretize, s5_lam_re[0], s5_lam_im[0], s5_log_dt[0], s5_b_re[0], s5_b_im[0])
    g_lam_re, g_lam_im, g_log_dt, g_b_re, g_b_im = s5_vjp((g_ab_re, g_ab_im, g_bb_re, g_bb_im))

    def take(a, width):
        return lax.dynamic_slice_in_dim(a, mine * width, width, axis=1)

    g_wc_st = jnp.stack([g_wc[:, j * (IN_C // N_SHARD):(j + 1) * (IN_C // N_SHARD)] for j in range(N_SHARD)])
    full = [g_wab, g_wout_ab, g_wc_st, g_wout_c, g_wglu]
    shard_shapes = [w_in_ab.shape[1:], w_out_ab.shape[1:], w_in_c.shape[1:], w_out_c.shape[1:], s5_w_glu.shape[1:]]
    names = ["w_in_ab", "w_out_ab", "w_in_c", "w_out_c", "w_glu"]
    got1 = _rs_pair_exchange(full, kinds, shard_shapes)
    pairs = [_rs_pair_add("rs_pair_add_" + nm, g, t, kd, ss, c_arr)
             for nm, g, t, kd, ss in zip(names, full, got1, kinds, shard_shapes)]
    got2 = _rs_chip_exchange([p[1] for p in pairs])
    halves = [_rs_chip_add("rs_chip_add_" + nm, p[0], t, ss, mine_c)
              for nm, p, t, ss in zip(names, pairs, got2, shard_shapes)]
    g_w_in_ab, g_w_out_ab, g_w_in_c, g_w_out_c, g_w_glu = _rs_pair_share(halves, shard_shapes)

    grads = {
        "meta": take(g_meta_f, q4), "norm_ab_w": g_norm_ab, "w_in_ab": g_w_in_ab[None], "ret_norm_w": g_ret_norm,
        "s5_lam_re": g_lam_re[None], "s5_lam_im": g_lam_im[None], "s5_log_dt": g_log_dt[None],
        "s5_b_re": g_b_re[None], "s5_b_im": g_b_im[None], "s5_c_re": g_c_re[None], "s5_c_im": g_c_im[None],
        "s5_d": g_d, "s5_w_glu": g_w_glu[None], "w_out_ab": g_w_out_ab[None], "norm_c_w": take(g_norm_c_f, q4),
        "w_in_c": g_w_in_c[None], "gla_w_gate": take(g_wgate_f, g4)[None], "gla_b_gate": take(g_bgate_f, g4),
        "gla_norm_w": take(g_gla_norm_f, q4), "w_out_c": g_w_out_c[None], "final_norm_w": g_final.reshape(D_MODEL),
    }
    weights = dict(meta=meta, norm_ab_w=norm_ab_w, w_in_ab=w_in_ab, ret_norm_w=ret_norm_w, s5_lam_re=s5_lam_re,
                   s5_lam_im=s5_lam_im, s5_log_dt=s5_log_dt, s5_b_re=s5_b_re, s5_b_im=s5_b_im, s5_c_re=s5_c_re,
                   s5_c_im=s5_c_im, s5_d=s5_d, s5_w_glu=s5_w_glu, w_out_ab=w_out_ab, norm_c_w=norm_c_w,
                   w_in_c=w_in_c, gla_w_gate=gla_w_gate, gla_b_gate=gla_b_gate, gla_norm_w=gla_norm_w,
                   w_out_c=w_out_c, final_norm_w=final_norm_w)
    m_in = dict(meta=m_meta, norm_ab_w=m_norm_ab_w, w_in_ab=m_w_in_ab, ret_norm_w=m_ret_norm_w,
                s5_lam_re=m_s5_lam_re, s5_lam_im=m_s5_lam_im, s5_log_dt=m_s5_log_dt, s5_b_re=m_s5_b_re,
                s5_b_im=m_s5_b_im, s5_c_re=m_s5_c_re, s5_c_im=m_s5_c_im, s5_d=m_s5_d, s5_w_glu=m_s5_w_glu,
                w_out_ab=m_w_out_ab, norm_c_w=m_norm_c_w, w_in_c=m_w_in_c, gla_w_gate=m_gla_w_gate,
                gla_b_gate=m_gla_b_gate, gla_norm_w=m_gla_norm_w, w_out_c=m_w_out_c, final_norm_w=m_final_norm_w)
    v_in = dict(meta=v_meta, norm_ab_w=v_norm_ab_w, w_in_ab=v_w_in_ab, ret_norm_w=v_ret_norm_w,
                s5_lam_re=v_s5_lam_re, s5_lam_im=v_s5_lam_im, s5_log_dt=v_s5_log_dt, s5_b_re=v_s5_b_re,
                s5_b_im=v_s5_b_im, s5_c_re=v_s5_c_re, s5_c_im=v_s5_c_im, s5_d=v_s5_d, s5_w_glu=v_s5_w_glu,
                w_out_ab=v_w_out_ab, norm_c_w=v_norm_c_w, w_in_c=v_w_in_c, gla_w_gate=v_gla_w_gate,
                gla_b_gate=v_gla_b_gate, gla_norm_w=v_gla_norm_w, w_out_c=v_w_out_c, final_norm_w=v_final_norm_w)
    order = list(weights)
    big_names = ["w_in_ab", "s5_w_glu", "w_out_ab", "w_in_c", "w_out_c"]
    small_names = [nm for nm in order if nm not in big_names]
    delta, new_m, new_v = {}, {}, {}
    for nm in big_names:
        shp = weights[nm].shape
        d2, m2, v2 = _adamw("adamw_" + nm, weights[nm][0], grads[nm][0], m_in[nm][0], v_in[nm][0])
        delta[nm], new_m[nm], new_v[nm] = d2.reshape(shp), m2.reshape(shp), v2.reshape(shp)
    sshapes = [weights[nm].shape for nm in small_names]
    d2, m2, v2 = _adamw("adamw_small", _pack([weights[nm] for nm in small_names]),
                        _pack([grads[nm] for nm in small_names]), _pack([m_in[nm] for nm in small_names]),
                        _pack([v_in[nm] for nm in small_names]))
    for nm, dd, mm, vv in zip(small_names, _unpack(d2, sshapes), _unpack(m2, sshapes), _unpack(v2, sshapes)):
        delta[nm], new_m[nm], new_v[nm] = dd, mm, vv
    grads = {nm: grads[nm].reshape(weights[nm].shape) for nm in order}
    return (loss.reshape(()), grad_x, *[grads[nm] for nm in order], *[delta[nm] for nm in order],
            *[new_m[nm] for nm in order], *[new_v[nm] for nm in order])
```

```python
import functools
import math

import jax
import jax.numpy as jnp
from jax import lax
from jax.experimental import pallas as pl
from jax.experimental.pallas import tpu as pltpu

F32 = jnp.float32
BF16 = jnp.bfloat16
MESH = pl.DeviceIdType.MESH

D_MODEL = 2048
N_META = 16
CHUNK = 128
SUB = 16
NSUB = CHUNK // SUB
PAD = CHUNK - N_META
EPS = 1e-6

RET_HEADS = 8
RET_DK = 128
RET_DV = 256
RET_QK = RET_HEADS * RET_DK
RET_W = RET_HEADS * RET_DV
ROPE_BASE = 10000.0

S5_W = 1024
S5_GH = 16
S5_G = S5_W // S5_GH
S5_P = 64
S5_TG = 8
S5_NT = S5_G // S5_TG
S5_TU = S5_TG * S5_GH
S5_TS = S5_TG * S5_P

GLA_HEADS = 4
GLA_DK = 256
GLA_DV = 512
GLA_QK = GLA_HEADS * GLA_DK
GLA_W = GLA_HEADS * GLA_DV
GLA_RANK = 16
GLA_TAU = 16.0

IN_AB = 2 * RET_QK + 2 * RET_W + 2 * S5_W
OUT_AB = RET_W + S5_W
IN_C = 2 * GLA_QK + 2 * GLA_W + GLA_RANK
IN_C_PAD = 2 * GLA_QK + 2 * GLA_W + 128

ADAM_LR = 0.001
ADAM_B1 = 0.9
ADAM_B2 = 0.999
ADAM_EPS = 1e-08
ADAM_WD = 0.01
ADAM_STEP = 10

N_SHARD = 4
SMALL_COLS = 512

NN = (((1,), (0,)), ((), ()))
NT = (((1,), (1,)), ((), ()))
TN = (((0,), (0,)), ((), ()))


def _dot(a, b, dims=NN):
    return lax.dot_general(a.astype(BF16), b.astype(BF16), dims, preferred_element_type=F32)


def _mo(v, m):
    return v if isinstance(v, int) else pl.multiple_of(v, m)


def _sigmoid(x):
    return 1.0 / (1.0 + jnp.exp(-x))


def _row_tile(rows, cap):
    n = rows // CHUNK
    best = 1
    for d in range(1, n + 1):
        if n % d == 0 and d * CHUNK <= cap:
            best = d
    return best * CHUNK


def _col_tile(cols, cap):
    n = cols // 128
    best = 1
    for d in range(1, n + 1):
        if n % d == 0 and d * 128 <= cap:
            best = d
    return best * 128


def _matmul(name, a, b, dims, m, n, k, *, tm, tn, tk, out_dtype=F32, a_off=(0, 0), b_off=(0, 0),
            extras=(), epilogue=None, out_shape=None, out_spec=None):
    nk = k // tk
    assert m % tm == 0 and n % tn == 0 and k % tk == 0, (name, m, n, k, tm, tn, tk)
    ar, ac = a_off
    br, bc = b_off
    if dims == NN:
        a_spec = pl.BlockSpec((tm, tk), lambda i, j, kk: (i + ar, kk + ac))
        b_spec = pl.BlockSpec((tk, tn), lambda i, j, kk: (kk + br, j + bc))
    elif dims == NT:
        a_spec = pl.BlockSpec((tm, tk), lambda i, j, kk: (i + ar, kk + ac))
        b_spec = pl.BlockSpec((tn, tk), lambda i, j, kk: (j + br, kk + bc))
    else:
        a_spec = pl.BlockSpec((tk, tm), lambda i, j, kk: (kk + ar, i + ac))
        b_spec = pl.BlockSpec((tk, tn), lambda i, j, kk: (kk + br, j + bc))
    n_extra = len(extras)

    def body(*refs):
        a_ref, b_ref = refs[0], refs[1]
        e_refs = refs[2:2 + n_extra]
        o_ref = refs[2 + n_extra]
        acc_ref = refs[3 + n_extra]
        kk = pl.program_id(2)

        @pl.when(kk == 0)
        def _():
            acc_ref[...] = jnp.zeros_like(acc_ref)

        acc_ref[...] += _dot(a_ref[...], b_ref[...], dims)

        @pl.when(kk == nk - 1)
        def _():
            acc = acc_ref[...]
            if epilogue is not None:
                acc = epilogue(acc, *[e[...] for e in e_refs])
            o_ref[...] = acc.astype(o_ref.dtype)

    if out_shape is None:
        out_shape = jax.ShapeDtypeStruct((m, n), out_dtype)
    if out_spec is None:
        out_spec = pl.BlockSpec((tm, tn), lambda i, j, kk: (i, j))
    return pl.pallas_call(
        body, name=name, grid=(m // tm, n // tn, nk),
        in_specs=[a_spec, b_spec] + [pl.BlockSpec(bs, im) for (_, bs, im) in extras],
        out_specs=out_spec, out_shape=out_shape,
        scratch_shapes=[pltpu.VMEM((tm, tn), F32)],
        compiler_params=pltpu.CompilerParams(dimension_semantics=("parallel", "parallel", "arbitrary")),
    )(a, b, *[e for (e, _, _) in extras])


def _rms_fwd(name, h, w):
    rows, d = h.shape
    tm = _row_tile(rows, 512)

    def body(h_ref, w_ref, o_ref):
        x = h_ref[...]
        r = lax.rsqrt(jnp.mean(x * x, axis=-1, keepdims=True) + EPS)
        o_ref[...] = (x * r * w_ref[...]).astype(BF16)

    return pl.pallas_call(
        body, name=name, grid=(rows // tm,),
        in_specs=[pl.BlockSpec((tm, d), lambda i: (i, 0)), pl.BlockSpec((1, d), lambda i: (0, 0))],
        out_specs=pl.BlockSpec((tm, d), lambda i: (i, 0)),
        out_shape=jax.ShapeDtypeStruct((rows, d), BF16),
    )(h, w)


def _rms_bwd(name, dhn, h, w, dres):
    rows, d = h.shape
    tm = _row_tile(rows, 384)

    def body(g_ref, h_ref, w_ref, r_ref, dh_ref, dw_ref):
        i = pl.program_id(0)
        x = h_ref[...]
        r = lax.rsqrt(jnp.mean(x * x, axis=-1, keepdims=True) + EPS)
        xh = x * r
        g = g_ref[...]
        gw = g * w_ref[...]
        dh_ref[...] = r_ref[...] + r * (gw - xh * jnp.mean(gw * xh, axis=-1, keepdims=True))

        @pl.when(i == 0)
        def _():
            dw_ref[...] = jnp.zeros_like(dw_ref)

        dw_ref[...] += jnp.sum(g * xh, axis=0, keepdims=True)

    return pl.pallas_call(
        body, name=name, grid=(rows // tm,),
        in_specs=[pl.BlockSpec((tm, d), lambda i: (i, 0)), pl.BlockSpec((tm, d), lambda i: (i, 0)),
                  pl.BlockSpec((1, d), lambda i: (0, 0)), pl.BlockSpec((tm, d), lambda i: (i, 0))],
        out_specs=[pl.BlockSpec((tm, d), lambda i: (i, 0)), pl.BlockSpec((1, d), lambda i: (0, 0))],
        out_shape=[jax.ShapeDtypeStruct((rows, d), F32), jax.ShapeDtypeStruct((1, d), F32)],
    )(dhn, h, w, dres)


def _final_loss(h2, w, target):
    rows, d = h2.shape

    def body(h_ref, w_ref, t_ref, loss_ref, dh_ref, dw_ref):
        i = pl.program_id(0)

        @pl.when(i == 0)
        def _():
            loss_ref[...] = jnp.zeros_like(loss_ref)
            dw_ref[...] = jnp.zeros_like(dw_ref)
            dh_ref[...] = jnp.zeros_like(dh_ref)

        @pl.when(i > 0)
        def _():
            x = h_ref[...]
            r = lax.rsqrt(jnp.mean(x * x, axis=-1, keepdims=True) + EPS)
            xh = x * r
            wv = w_ref[...]
            err = xh * wv - t_ref[...]
            loss_ref[...] += 0.5 * jnp.sum(jnp.mean(err * err, axis=-1, keepdims=True), axis=0, keepdims=True)
            g = err * (1.0 / d)
            gw = g * wv
            dh_ref[...] = r * (gw - xh * jnp.mean(gw * xh, axis=-1, keepdims=True))
            dw_ref[...] += jnp.sum(g * xh, axis=0, keepdims=True)

    return pl.pallas_call(
        body, name="final_loss", grid=(rows // CHUNK,),
        in_specs=[pl.BlockSpec((CHUNK, d), lambda i: (i, 0)), pl.BlockSpec((1, d), lambda i: (0, 0)),
                  pl.BlockSpec((CHUNK, d), lambda i: (jnp.maximum(i - 1, 0), 0))],
        out_specs=[pl.BlockSpec((1, 1), lambda i: (0, 0)), pl.BlockSpec((CHUNK, d), lambda i: (i, 0)),
                   pl.BlockSpec((1, d), lambda i: (0, 0))],
        out_shape=[jax.ShapeDtypeStruct((1, 1), F32), jax.ShapeDtypeStruct((rows, d), F32),
                   jax.ShapeDtypeStruct((1, d), F32)],
    )(h2, w, target)


def _gate_fwd(o, z, w):
    rs = lax.rsqrt(jnp.mean(o * o, axis=-1, keepdims=True) + EPS)
    return o * rs * w * (z * _sigmoid(z))


def _gate_bwd(dout, o, z, w):
    rs = lax.rsqrt(jnp.mean(o * o, axis=-1, keepdims=True) + EPS)
    yn = o * rs
    sg = _sigmoid(z)
    sil = z * sg
    dsil = sg * (1.0 + z * (1.0 - sg))
    dz = dout * yn * w * dsil
    dyn = dout * w * sil
    dw = jnp.sum(dout * yn * sil, axis=0, keepdims=True)
    do = rs * (dyn - yn * jnp.mean(dyn * yn, axis=-1, keepdims=True))
    return do, dz, dw


def _rope(t, cosf, sinf):
    return t * cosf + pltpu.roll(t, RET_DK // 2, 1) * sinf


def _rope_t(d, cosf, sinf):
    return d * cosf + pltpu.roll(d * sinf, RET_DK // 2, 1)


def _ret_tables():
    log_g = jnp.log1p(-jnp.exp2(-5.0 - jnp.arange(RET_HEADS, dtype=F32)))
    idx = jnp.arange(CHUNK, dtype=F32)
    diff = idx[:, None] - idx[None, :]
    decay = jnp.where(diff >= 0, jnp.exp(log_g[:, None, None] * jnp.maximum(diff, 0.0)), 0.0)
    kw = jnp.exp(log_g[:, None] * (CHUNK - 1 - idx))
    qw = jnp.exp(log_g[:, None] * (idx + 1.0))
    gch = jnp.exp(log_g * CHUNK)
    kw = jnp.broadcast_to(kw[:, :, None], (RET_HEADS, CHUNK, RET_DK))
    qw = jnp.broadcast_to(qw[:, :, None], (RET_HEADS, CHUNK, RET_DK))
    gch = jnp.broadcast_to(gch[:, None, None], (RET_HEADS, 1, RET_DV))
    return decay, kw, qw, gch


def _rope_tables(rows):
    pos = jnp.arange(rows, dtype=F32) - float(PAD)
    inv_freq = jnp.power(ROPE_BASE, -jnp.arange(0, RET_DK, 2, dtype=F32) / RET_DK)
    ang = pos[:, None] * inv_freq[None, :]
    cos, sin = jnp.cos(ang), jnp.sin(ang)
    return jnp.concatenate([cos, cos], axis=1), jnp.concatenate([-sin, sin], axis=1)


RET_HB = 4
RET_QB = RET_HB * RET_DK
RET_VB = RET_HB * RET_DV


def _ret_in_specs(rev, nc):
    def cn(n):
        return (nc - 1 - n) if rev else n
    kb = RET_QK // RET_QB
    vb = 2 * RET_QK // RET_VB
    zb = (2 * RET_QK + RET_W) // RET_VB
    return [
        pl.BlockSpec((CHUNK, RET_QB), lambda h, n: (cn(n), h)),
        pl.BlockSpec((CHUNK, RET_QB), lambda h, n: (cn(n), kb + h)),
        pl.BlockSpec((CHUNK, RET_VB), lambda h, n: (cn(n), vb + h)),
        pl.BlockSpec((CHUNK, RET_VB), lambda h, n: (cn(n), zb + h)),
        pl.BlockSpec((CHUNK, RET_DK), lambda h, n: (cn(n), 0)),
        pl.BlockSpec((CHUNK, RET_DK), lambda h, n: (cn(n), 0)),
        pl.BlockSpec((RET_HB, CHUNK, CHUNK), lambda h, n: (h, 0, 0)),
        pl.BlockSpec((RET_HB, CHUNK, RET_DK), lambda h, n: (h, 0, 0)),
        pl.BlockSpec((RET_HB, CHUNK, RET_DK), lambda h, n: (h, 0, 0)),
        pl.BlockSpec((RET_HB, 1, RET_DV), lambda h, n: (h, 0, 0)),
        pl.BlockSpec((1, RET_VB), lambda h, n: (0, h)),
    ]


def _ret_fwd(proj, cosf, sinf, tables, normw):
    rows = proj.shape[0]
    nc = rows // CHUNK
    decay, kw, qw, gch = tables

    def body(q_ref, k_ref, v_ref, z_ref, cos_ref, sin_ref, dm_ref, kw_ref, qw_ref, g_ref, w_ref,
             o_ref, oa_ref, st_ref, s_scr):
        n = pl.program_id(1)

        @pl.when(n == 0)
        def _():
            s_scr[...] = jnp.zeros_like(s_scr)

        cosv, sinv = cos_ref[...], sin_ref[...]
        for hh in range(RET_HB):
            qc = slice(hh * RET_DK, (hh + 1) * RET_DK)
            vc = slice(hh * RET_DV, (hh + 1) * RET_DV)
            q = _rope(q_ref[:, qc], cosv, sinv)
            k = _rope(k_ref[:, qc], cosv, sinv) * (RET_DK ** -0.5)
            v = v_ref[:, vc]
            s = s_scr[hh]
            st_ref[hh, 0] = s.astype(BF16)
            a = _dot(q, k, NT) * dm_ref[hh]
            o = _dot(a, v) + _dot(q * qw_ref[hh], s)
            s_scr[hh] = s * g_ref[hh] + _dot(k * kw_ref[hh], v, TN)
            o_ref[:, vc] = o
            oa_ref[:, vc] = _gate_fwd(o, z_ref[:, vc], w_ref[:, vc]).astype(BF16)

    return pl.pallas_call(
        body, name="ret_fwd", grid=(RET_HEADS // RET_HB, nc),
        in_specs=_ret_in_specs(False, nc),
        out_specs=[pl.BlockSpec((CHUNK, RET_VB), lambda h, n: (n, h)),
                   pl.BlockSpec((CHUNK, RET_VB), lambda h, n: (n, h)),
                   pl.BlockSpec((RET_HB, 1, RET_DK, RET_DV), lambda h, n: (h, n, 0, 0))],
        out_shape=[jax.ShapeDtypeStruct((rows, RET_W), F32), jax.ShapeDtypeStruct((rows, RET_W), BF16),
                   jax.ShapeDtypeStruct((RET_HEADS, nc, RET_DK, RET_DV), BF16)],
        scratch_shapes=[pltpu.VMEM((RET_HB, RET_DK, RET_DV), F32)],
        compiler_params=pltpu.CompilerParams(dimension_semantics=("parallel", "arbitrary")),
    )(proj, proj, proj, proj, cosf, sinf, decay, kw, qw, gch, normw)


def _ret_bwd(proj, cosf, sinf, tables, normw, o_ret, dmix, states):
    rows = proj.shape[0]
    nc = rows // CHUNK
    decay, kw, qw, gch = tables

    def rn(n):
        return nc - 1 - n

    def body(q_ref, k_ref, v_ref, z_ref, cos_ref, sin_ref, dm_ref, kw_ref, qw_ref, g_ref, w_ref,
             o_ref, do_ref, st_ref, dq_ref, dk_ref, dv_ref, dz_ref, dw_ref, ds_scr):
        n = pl.program_id(1)

        @pl.when(n == 0)
        def _():
            ds_scr[...] = jnp.zeros_like(ds_scr)
            dw_ref[...] = jnp.zeros_like(dw_ref)

        cosv, sinv = cos_ref[...], sin_ref[...]
        for hh in range(RET_HB):
            qc = slice(hh * RET_DK, (hh + 1) * RET_DK)
            vc = slice(hh * RET_DV, (hh + 1) * RET_DV)
            q = _rope(q_ref[:, qc], cosv, sinv)
            k = _rope(k_ref[:, qc], cosv, sinv) * (RET_DK ** -0.5)
            v = v_ref[:, vc]
            do, dz, dw = _gate_bwd(do_ref[:, vc], o_ref[:, vc], z_ref[:, vc], w_ref[:, vc])
            dz_ref[:, vc] = dz.astype(BF16)
            dw_ref[hh] += dw
            dm = dm_ref[hh]
            s = st_ref[hh, 0]
            g1 = ds_scr[hh]
            p = _dot(q, k, NT) * dm
            kwv = k * kw_ref[hh]
            qwv = q * qw_ref[hh]
            dp = _dot(do, v, NT)
            da = dp * dm
            dv = _dot(p, do, TN) + _dot(kwv, g1)
            dq = _dot(da, k) + _dot(do, s, NT) * qw_ref[hh]
            dk = _dot(da, q, TN) + _dot(v, g1, NT) * kw_ref[hh]
            ds_scr[hh] = g1 * g_ref[hh] + _dot(qwv, do, TN)
            dv_ref[:, vc] = dv.astype(BF16)
            dq_ref[:, qc] = _rope_t(dq, cosv, sinv).astype(BF16)
            dk_ref[:, qc] = _rope_t(dk * (RET_DK ** -0.5), cosv, sinv).astype(BF16)

    in_specs = _ret_in_specs(True, nc) + [
        pl.BlockSpec((CHUNK, RET_VB), lambda h, n: (rn(n), h)),
        pl.BlockSpec((CHUNK, RET_VB), lambda h, n: (rn(n), h)),
        pl.BlockSpec((RET_HB, 1, RET_DK, RET_DV), lambda h, n: (h, rn(n), 0, 0)),
    ]
    return pl.pallas_call(
        body, name="ret_bwd", grid=(RET_HEADS // RET_HB, nc),
        in_specs=in_specs,
        out_specs=[pl.BlockSpec((CHUNK, RET_QB), lambda h, n: (rn(n), h)),
                   pl.BlockSpec((CHUNK, RET_QB), lambda h, n: (rn(n), h)),
                   pl.BlockSpec((CHUNK, RET_VB), lambda h, n: (rn(n), h)),
                   pl.BlockSpec((CHUNK, RET_VB), lambda h, n: (rn(n), h)),
                   pl.BlockSpec((RET_HB, 1, RET_DV), lambda h, n: (h, 0, 0))],
        out_shape=[jax.ShapeDtypeStruct((rows, RET_QK), BF16), jax.ShapeDtypeStruct((rows, RET_QK), BF16),
                   jax.ShapeDtypeStruct((rows, RET_W), BF16), jax.ShapeDtypeStruct((rows, RET_W), BF16),
                   jax.ShapeDtypeStruct((RET_HEADS, 1, RET_DV), F32)],
        scratch_shapes=[pltpu.VMEM((RET_HB, RET_DK, RET_DV), F32)],
        compiler_params=pltpu.CompilerParams(dimension_semantics=("parallel", "arbitrary")),
    )(proj, proj, proj, proj, cosf, sinf, decay, kw, qw, gch, normw, o_ret, dmix, states)


def _s5_discretize(lam_re, lam_im, log_dt, b_re, b_im):
    dt = jnp.exp(log_dt)[:, None]
    mag = jnp.exp(lam_re * dt)
    ab_re, ab_im = mag * jnp.cos(lam_im * dt), mag * jnp.sin(lam_im * dt)
    den = lam_re * lam_re + lam_im * lam_im
    nr, ni = ab_re - 1.0, ab_im
    f_re = (nr * lam_re + ni * lam_im) / den
    f_im = (ni * lam_re - nr * lam_im) / den
    bb_re = f_re[..., None] * b_re - f_im[..., None] * b_im
    bb_im = f_re[..., None] * b_im + f_im[..., None] * b_re
    return ab_re, ab_im, bb_re, bb_im


def _bdiag_in(bb):
    t = bb.reshape(S5_NT, S5_TG, S5_P, S5_GH).transpose(0, 1, 3, 2)
    eye = jnp.eye(S5_TG, dtype=bb.dtype)
    full = t[:, :, :, None, :] * eye[None, :, None, :, None]
    return full.reshape(S5_NT, S5_TU, S5_TS)


def _bdiag_in_extract(dense):
    t = dense.reshape(S5_NT, S5_TG, S5_GH, S5_TG, S5_P)
    diag = jnp.stack([t[:, g, :, g, :] for g in range(S5_TG)], axis=1)
    return diag.transpose(0, 1, 3, 2).reshape(S5_G, S5_P, S5_GH)


def _bdiag_out(c):
    t = c.reshape(S5_NT, S5_TG, S5_GH, S5_P).transpose(0, 1, 3, 2)
    eye = jnp.eye(S5_TG, dtype=c.dtype)
    full = t[:, :, :, None, :] * eye[None, :, None, :, None]
    return full.reshape(S5_NT, S5_TS, S5_TU)


def _bdiag_out_extract(dense):
    t = dense.reshape(S5_NT, S5_TG, S5_P, S5_TG, S5_GH)
    diag = jnp.stack([t[:, g, :, g, :] for g in range(S5_TG)], axis=1)
    return diag.transpose(0, 1, 3, 2).reshape(S5_G, S5_GH, S5_P)


def _cmul(ar, ai, br, bi):
    return ar * br - ai * bi, ar * bi + ai * br


S5_SEG = 8
S5_STEPS = CHUNK // S5_SEG


def _seg_perm(x):
    c = x.shape[1]
    return jnp.swapaxes(x.reshape(S5_SEG, S5_STEPS, c), 0, 1).reshape(CHUNK, c)


def _seg_unperm(x):
    c = x.shape[1]
    return jnp.swapaxes(x.reshape(S5_STEPS, S5_SEG, c), 0, 1).reshape(CHUNK, c)


def _rows(x, p):
    return x[p * S5_SEG:(p + 1) * S5_SEG]


def _s5_tables(ar, ai, tr_scr, ti_scr, wfr_scr, wfi_scr, wbr_scr, wbi_scr):
    row = lax.broadcasted_iota(jnp.int32, (S5_SEG, 1), 0)
    a8r = jnp.broadcast_to(ar, (S5_SEG, S5_TS))
    a8i = jnp.broadcast_to(ai, (S5_SEG, S5_TS))
    pr, pi = a8r, a8i
    for p in range(S5_STEPS):
        tr_scr[p * S5_SEG:(p + 1) * S5_SEG, :] = pr
        ti_scr[p * S5_SEG:(p + 1) * S5_SEG, :] = pi
        if p < S5_STEPS - 1:
            pr, pi = _cmul(pr, pi, a8r, a8i)
    wr, wi = pr, pi
    sh = 1
    while sh < S5_SEG:
        keep = row >= sh
        sr = jnp.where(keep, pltpu.roll(wr, sh, 0), 1.0)
        si = jnp.where(keep, pltpu.roll(wi, sh, 0), 0.0)
        wr, wi = _cmul(wr, wi, sr, si)
        sh *= 2
    wfr_scr[...] = wr
    wfi_scr[...] = wi
    wr, wi = pr, -pi
    sh = 1
    while sh < S5_SEG:
        keep = row < S5_SEG - sh
        sr = jnp.where(keep, pltpu.roll(wr, S5_SEG - sh, 0), 1.0)
        si = jnp.where(keep, pltpu.roll(wi, S5_SEG - sh, 0), 0.0)
        wr, wi = _cmul(wr, wi, sr, si)
        sh *= 2
    wbr_scr[...] = wr
    wbi_scr[...] = wi


def _seg_scan(vr, vi, ar, ai, tr_scr, ti_scr, wr_scr, wi_scr, c0r, c0i, down):
    row = lax.broadcasted_iota(jnp.int32, (S5_SEG, 1), 0)
    sgn = 1.0 if down else -1.0
    order = list(range(S5_STEPS)) if down else list(range(S5_STEPS - 1, -1, -1))
    xr, xi = _rows(vr, order[0]), _rows(vi, order[0])
    loc = {order[0]: (xr, xi)}
    for p in order[1:]:
        mr, mi = _cmul(ar, sgn * ai, xr, xi)
        xr, xi = mr + _rows(vr, p), mi + _rows(vi, p)
        loc[p] = (xr, xi)
    last = S5_STEPS - 1
    mr, mi = tr_scr[last * S5_SEG:(last + 1) * S5_SEG, :], sgn * ti_scr[last * S5_SEG:(last + 1) * S5_SEG, :]
    er, ei = xr, xi
    sh = 1
    while sh < S5_SEG:
        if down:
            keep = row >= sh
            sr, si = pltpu.roll(er, sh, 0), pltpu.roll(ei, sh, 0)
        else:
            keep = row < S5_SEG - sh
            sr, si = pltpu.roll(er, S5_SEG - sh, 0), pltpu.roll(ei, S5_SEG - sh, 0)
        pr, pi = _cmul(mr, mi, jnp.where(keep, sr, 0.0), jnp.where(keep, si, 0.0))
        er, ei = er + pr, ei + pi
        mr, mi = _cmul(mr, mi, mr, mi)
        sh *= 2
    pr, pi = _cmul(wr_scr[...], wi_scr[...], c0r, c0i)
    er, ei = er + pr, ei + pi
    if down:
        nr = jnp.where(row == 0, c0r, pltpu.roll(er, 1, 0))
        ni = jnp.where(row == 0, c0i, pltpu.roll(ei, 1, 0))
    else:
        nr = jnp.where(row == S5_SEG - 1, c0r, pltpu.roll(er, S5_SEG - 1, 0))
        ni = jnp.where(row == S5_SEG - 1, c0i, pltpu.roll(ei, S5_SEG - 1, 0))
    out_r, out_i = [], []
    for p in range(S5_STEPS):
        q = p if down else S5_STEPS - 1 - p
        pr, pi = _cmul(tr_scr[q * S5_SEG:(q + 1) * S5_SEG, :], sgn * ti_scr[q * S5_SEG:(q + 1) * S5_SEG, :], nr, ni)
        out_r.append(loc[p][0] + pr)
        out_i.append(loc[p][1] + pi)
    return jnp.concatenate(out_r, axis=0), jnp.concatenate(out_i, axis=0), (nr, ni), (er, ei)


def _gelu(y):
    c = math.sqrt(2.0 / math.pi)
    return 0.5 * y * (1.0 + jnp.tanh(c * (y + 0.044715 * y * y * y)))


def _gelu_grad(y):
    c = math.sqrt(2.0 / math.pi)
    th = jnp.tanh(c * (y + 0.044715 * y * y * y))
    return 0.5 * (1.0 + th) + 0.5 * y * (1.0 - th * th) * c * (1.0 + 3.0 * 0.044715 * y * y)


def _s5_fwd(proj, ab, bd_b, bd_c, dvec):
    rows = proj.shape[0]
    nc = rows // CHUNK
    ub = (2 * RET_QK + 2 * RET_W) // S5_TU
    ab_re, ab_im = ab
    bre, bim = bd_b
    cre, cim = bd_c

    def body(u_ref, ar_ref, ai_ref, bre_ref, bim_ref, cre_ref, cim_ref, d_ref,
             y_ref, g_ref, er_ref, ei_ref, tr_scr, ti_scr, wfr_scr, wfi_scr, wbr_scr, wbi_scr,
             cr_scr, ci_scr, er_scr, ei_scr):
        n = pl.program_id(1)
        ar, ai = ar_ref[0], ai_ref[0]

        @pl.when(n == 0)
        def _():
            _s5_tables(ar, ai, tr_scr, ti_scr, wfr_scr, wfi_scr, wbr_scr, wbi_scr)
            cr_scr[...] = jnp.zeros_like(cr_scr)
            ci_scr[...] = jnp.zeros_like(ci_scr)

        u = _seg_perm(u_ref[...])
        c0r, c0i = cr_scr[...], ci_scr[...]
        er_ref[0, 0] = c0r
        ei_ref[0, 0] = c0i
        xr, xi, _, (er, ei) = _seg_scan(_dot(u, bre_ref[0]), _dot(u, bim_ref[0]), ar, ai, tr_scr, ti_scr,
                                        wfr_scr, wfi_scr, c0r, c0i, True)
        er_scr[...] = er
        ei_scr[...] = ei
        cr_scr[...] = jnp.broadcast_to(er_scr[S5_SEG - 1:S5_SEG, :], cr_scr.shape)
        ci_scr[...] = jnp.broadcast_to(ei_scr[S5_SEG - 1:S5_SEG, :], ci_scr.shape)
        y = _seg_unperm(_dot(xr, cre_ref[0]) - _dot(xi, cim_ref[0]) + d_ref[...] * u)
        y_ref[...] = y
        g_ref[...] = _gelu(y).astype(BF16)

    vec = pl.BlockSpec((1, 1, S5_TS), lambda t, n: (t, 0, 0))
    return pl.pallas_call(
        body, name="s5_fwd", grid=(S5_NT, nc),
        in_specs=[pl.BlockSpec((CHUNK, S5_TU), lambda t, n: (n, ub + t)), vec, vec,
                  pl.BlockSpec((1, S5_TU, S5_TS), lambda t, n: (t, 0, 0)),
                  pl.BlockSpec((1, S5_TU, S5_TS), lambda t, n: (t, 0, 0)),
                  pl.BlockSpec((1, S5_TS, S5_TU), lambda t, n: (t, 0, 0)),
                  pl.BlockSpec((1, S5_TS, S5_TU), lambda t, n: (t, 0, 0)),
                  pl.BlockSpec((1, S5_TU), lambda t, n: (0, t))],
        out_specs=[pl.BlockSpec((CHUNK, S5_TU), lambda t, n: (n, t)),
                   pl.BlockSpec((CHUNK, S5_TU), lambda t, n: (n, t)),
                   pl.BlockSpec((1, 1, 8, S5_TS), lambda t, n: (t, n, 0, 0)),
                   pl.BlockSpec((1, 1, 8, S5_TS), lambda t, n: (t, n, 0, 0))],
        out_shape=[jax.ShapeDtypeStruct((rows, S5_W), F32), jax.ShapeDtypeStruct((rows, S5_W), BF16),
                   jax.ShapeDtypeStruct((S5_NT, nc, 8, S5_TS), F32),
                   jax.ShapeDtypeStruct((S5_NT, nc, 8, S5_TS), F32)],
        scratch_shapes=[pltpu.VMEM((CHUNK, S5_TS), F32) for _ in range(2)]
        + [pltpu.VMEM((S5_SEG, S5_TS), F32) for _ in range(8)],
        compiler_params=pltpu.CompilerParams(dimension_semantics=("parallel", "arbitrary")),
    )(proj, ab_re.reshape(S5_NT, 1, S5_TS), ab_im.reshape(S5_NT, 1, S5_TS), bre, bim, cre, cim, dvec)


def _s5_bwd(proj, dy, ab, bd_b, bd_c, dvec, entry):
    rows = proj.shape[0]
    nc = rows // CHUNK
    ub = (2 * RET_QK + 2 * RET_W) // S5_TU
    ab_re, ab_im = ab
    bre, bim = bd_b
    cre, cim = bd_c
    er, ei = entry

    def rn(n):
        return nc - 1 - n

    def body(u_ref, dy_ref, ar_ref, ai_ref, bre_ref, bim_ref, cre_ref, cim_ref, d_ref, er_ref, ei_ref,
             du_ref, dbr_ref, dbi_ref, dcr_ref, dci_ref, dar_ref, dai_ref, dd_ref,
             tr_scr, ti_scr, wfr_scr, wfi_scr, wbr_scr, wbi_scr, gr_scr, gi_scr, er_scr, ei_scr):
        n = pl.program_id(1)
        ar, ai = ar_ref[0], ai_ref[0]

        @pl.when(n == 0)
        def _():
            _s5_tables(ar, ai, tr_scr, ti_scr, wfr_scr, wfi_scr, wbr_scr, wbi_scr)
            gr_scr[...] = jnp.zeros_like(gr_scr)
            gi_scr[...] = jnp.zeros_like(gi_scr)
            for r in (dbr_ref, dbi_ref, dcr_ref, dci_ref, dar_ref, dai_ref, dd_ref):
                r[...] = jnp.zeros_like(r)

        u = _seg_perm(u_ref[...])
        dy = _seg_perm(dy_ref[...])
        xr, xi, (pr, pi), _ = _seg_scan(_dot(u, bre_ref[0]), _dot(u, bim_ref[0]), ar, ai, tr_scr, ti_scr,
                                        wfr_scr, wfi_scr, er_ref[0, 0], ei_ref[0, 0], True)
        dcr_ref[0] += _dot(xr, dy, TN)
        dci_ref[0] -= _dot(xi, dy, TN)
        gr, gi, _, (er, ei) = _seg_scan(_dot(dy, cre_ref[0], NT), -_dot(dy, cim_ref[0], NT), ar, ai, tr_scr, ti_scr,
                                        wbr_scr, wbi_scr, gr_scr[...], gi_scr[...], False)
        er_scr[...] = er
        ei_scr[...] = ei
        gr_scr[...] = jnp.broadcast_to(er_scr[0:1, :], gr_scr.shape)
        gi_scr[...] = jnp.broadcast_to(ei_scr[0:1, :], gi_scr.shape)
        xpr = jnp.concatenate([pr, xr[:CHUNK - S5_SEG]], axis=0)
        xpi = jnp.concatenate([pi, xi[:CHUNK - S5_SEG]], axis=0)
        dar_ref[0] += jnp.sum((xpr * gr + xpi * gi).reshape(S5_STEPS, S5_SEG, S5_TS), axis=0)
        dai_ref[0] += jnp.sum((xpr * gi - xpi * gr).reshape(S5_STEPS, S5_SEG, S5_TS), axis=0)
        dbr_ref[0] += _dot(u, gr, TN)
        dbi_ref[0] += _dot(u, gi, TN)
        dd_ref[0] += jnp.sum((dy * u).reshape(S5_STEPS, S5_SEG, S5_TU), axis=0)
        du = dy * d_ref[...] + _dot(gr, bre_ref[0], NT) + _dot(gi, bim_ref[0], NT)
        du_ref[...] = _seg_unperm(du).astype(BF16)

    vec = pl.BlockSpec((1, 1, S5_TS), lambda t, n: (t, 0, 0))
    acc_b = pl.BlockSpec((1, S5_TU, S5_TS), lambda t, n: (t, 0, 0))
    acc_c = pl.BlockSpec((1, S5_TS, S5_TU), lambda t, n: (t, 0, 0))
    acc_a = pl.BlockSpec((1, 8, S5_TS), lambda t, n: (t, 0, 0))
    ent = pl.BlockSpec((1, 1, 8, S5_TS), lambda t, n: (t, rn(n), 0, 0))
    return pl.pallas_call(
        body, name="s5_bwd", grid=(S5_NT, nc),
        in_specs=[pl.BlockSpec((CHUNK, S5_TU), lambda t, n: (rn(n), ub + t)),
                  pl.BlockSpec((CHUNK, S5_TU), lambda t, n: (rn(n), t)), vec, vec,
                  acc_b, acc_b, acc_c, acc_c, pl.BlockSpec((1, S5_TU), lambda t, n: (0, t)), ent, ent],
        out_specs=[pl.BlockSpec((CHUNK, S5_TU), lambda t, n: (rn(n), t)), acc_b, acc_b, acc_c, acc_c, acc_a, acc_a,
                   pl.BlockSpec((1, 8, S5_TU), lambda t, n: (t, 0, 0))],
        out_shape=[jax.ShapeDtypeStruct((rows, S5_W), BF16),
                   jax.ShapeDtypeStruct((S5_NT, S5_TU, S5_TS), F32), jax.ShapeDtypeStruct((S5_NT, S5_TU, S5_TS), F32),
                   jax.ShapeDtypeStruct((S5_NT, S5_TS, S5_TU), F32), jax.ShapeDtypeStruct((S5_NT, S5_TS, S5_TU), F32),
                   jax.ShapeDtypeStruct((S5_NT, 8, S5_TS), F32), jax.ShapeDtypeStruct((S5_NT, 8, S5_TS), F32),
                   jax.ShapeDtypeStruct((S5_NT, 8, S5_TU), F32)],
        scratch_shapes=[pltpu.VMEM((CHUNK, S5_TS), F32) for _ in range(2)]
        + [pltpu.VMEM((S5_SEG, S5_TS), F32) for _ in range(8)],
        compiler_params=pltpu.CompilerParams(dimension_semantics=("parallel", "arbitrary")),
    )(proj, dy,ab_re.reshape(S5_NT, 1, S5_TS), ab_im.reshape(S5_NT, 1, S5_TS), bre, bim, cre, cim, dvec, er, ei)


def _s5_gate_bwd(dmix, g, t, proj):
    rows = g.shape[0]
    tm = _row_tile(rows, 384)
    ob = RET_W // S5_W
    zb = (2 * RET_QK + 2 * RET_W + S5_W) // S5_W

    def body(do_ref, g_ref, t_ref, z_ref, dz_ref, dt_ref, dg_ref):
        do = do_ref[...]
        gv = g_ref[...].astype(F32)
        z = z_ref[...]
        st = _sigmoid(t_ref[...])
        sg = _sigmoid(z)
        os5 = gv * st
        dz_ref[...] = (do * os5 * sg * (1.0 + z * (1.0 - sg))).astype(BF16)
        dos = do * z * sg
        dt_ref[...] = (dos * gv * st * (1.0 - st)).astype(BF16)
        dg_ref[...] = dos * st

    blk = pl.BlockSpec((tm, S5_W), lambda i: (i, 0))
    return pl.pallas_call(
        body, name="s5_gate_bwd", grid=(rows // tm,),
        in_specs=[pl.BlockSpec((tm, S5_W), lambda i: (i, ob)), blk, blk,
                  pl.BlockSpec((tm, S5_W), lambda i: (i, zb))],
        out_specs=[blk, blk, blk],
        out_shape=[jax.ShapeDtypeStruct((rows, S5_W), BF16), jax.ShapeDtypeStruct((rows, S5_W), BF16),
                   jax.ShapeDtypeStruct((rows, S5_W), F32)],
    )(dmix, g, t, proj)


def _split3(x):
    hi = x.astype(BF16)
    r = x - hi.astype(F32)
    mid = r.astype(BF16)
    lo = (r - mid.astype(F32)).astype(BF16)
    return hi, mid, lo


def _tri_sum(x, upper):
    i = lax.broadcasted_iota(jnp.int32, (CHUNK, CHUNK), 0)
    j = lax.broadcasted_iota(jnp.int32, (CHUNK, CHUNK), 1)
    tri = jnp.where((j >= i) if upper else (j <= i), 1.0, 0.0).astype(BF16)
    hi, mid, lo = _split3(x)
    return _dot(tri, lo) + _dot(tri, mid) + _dot(tri, hi)


def _gla_log_decay(gl, wg, bg, n):
    logit = _dot(gl, wg) + bg
    la = (jnp.minimum(logit, 0.0) - jnp.log(1.0 + jnp.exp(-jnp.abs(logit)))) * (1.0 / GLA_TAU)
    row = lax.broadcasted_iota(jnp.int32, (CHUNK, 1), 0)
    live = jnp.logical_or(n > 0, row >= PAD)
    return logit, jnp.where(live, la, 0.0), live


def _gla_in_specs(rev, nc):
    def cn(n):
        return (nc - 1 - n) if rev else n
    kb = GLA_QK // GLA_DK
    vb = 2 * GLA_QK // GLA_DV
    zb = (2 * GLA_QK + GLA_W) // GLA_DV
    gb = (2 * GLA_QK + 2 * GLA_W) // 128
    return [
        pl.BlockSpec((CHUNK, GLA_DK), lambda h, n: (cn(n), h)),
        pl.BlockSpec((CHUNK, GLA_DK), lambda h, n: (cn(n), kb + h)),
        pl.BlockSpec((CHUNK, GLA_DV), lambda h, n: (cn(n), vb + h)),
        pl.BlockSpec((CHUNK, GLA_DV), lambda h, n: (cn(n), zb + h)),
        pl.BlockSpec((CHUNK, 128), lambda h, n: (cn(n), gb)),
        pl.BlockSpec((128, GLA_DK), lambda h, n: (0, h)),
        pl.BlockSpec((1, GLA_DK), lambda h, n: (0, h)),
        pl.BlockSpec((1, GLA_DV), lambda h, n: (0, h)),
    ]


def _gla_fwd(proj, wgate, bgate, normw):
    rows = proj.shape[0]
    nc = rows // CHUNK

    def body(q_ref, k_ref, v_ref, z_ref, gl_ref, wg_ref, bg_ref, w_ref, o_ref, oc_ref, st_ref, s_scr, o_scr, b_scr):
        n = pl.program_id(1)

        @pl.when(n == 0)
        def _():
            s_scr[...] = jnp.zeros_like(s_scr)

        q = q_ref[...] * (GLA_DK ** -0.5)
        k = k_ref[...]
        v = v_ref[...]
        vb = v.astype(BF16)
        _, la, _ = _gla_log_decay(gl_ref[...], wg_ref[...], bg_ref[...], n)
        b = _tri_sum(la, False)
        b_scr[...] = b
        b_last = b_scr[CHUNK - 1:CHUNK, :]
        st = s_scr[...]
        st_ref[0, 0] = st
        s_scr[...] = st * jnp.exp(b_last) + _dot(v, k * jnp.exp(b_last - b), TN)
        rowc = lax.broadcasted_iota(jnp.int32, (CHUNK, 1), 0)
        rows16 = lax.broadcasted_iota(jnp.int32, (SUB, 1), 0)
        a_tot = jnp.zeros((CHUNK, CHUNK), F32)
        for s in range(1, NSUB):
            lo = s * SUB
            bref = b_scr[lo - 1:lo, :]
            in_s = jnp.logical_and(rowc >= lo, rowc < lo + SUB)
            qh = q * jnp.exp(jnp.where(in_s, b - bref, -1e30))
            kh = k * jnp.exp(jnp.where(rowc < lo, bref - b, -1e30))
            a_tot = a_tot + _dot(qh, kh, NT)
        o_scr[...] = _dot(q * jnp.exp(b), st, NT) + _dot(a_tot, vb)
        for s in range(NSUB):
            lo = s * SUB
            qs, bs = q[lo:lo + SUB], b[lo:lo + SUB]
            acc = jnp.zeros((SUB, GLA_DV), F32)
            for j in range(SUB):
                r = lo + j
                e = jnp.exp(jnp.where(rows16 >= j, bs - b_scr[r:r + 1, :], -1e30))
                col = jnp.sum(qs * k_ref[r:r + 1, :] * e, axis=1, keepdims=True)
                acc = acc + col * v_ref[r:r + 1, :]
            o_scr[lo:lo + SUB, :] += acc
        o = o_scr[...]
        o_ref[...] = o
        oc_ref[...] = _gate_fwd(o, z_ref[...], w_ref[...]).astype(BF16)

    return pl.pallas_call(
        body, name="gla_fwd", grid=(GLA_HEADS, nc),
        in_specs=_gla_in_specs(False, nc),
        out_specs=[pl.BlockSpec((CHUNK, GLA_DV), lambda h, n: (n, h)),
                   pl.BlockSpec((CHUNK, GLA_DV), lambda h, n: (n, h)),
                   pl.BlockSpec((1, 1, GLA_DV, GLA_DK), lambda h, n: (h, n, 0, 0))],
        out_shape=[jax.ShapeDtypeStruct((rows, GLA_W), F32), jax.ShapeDtypeStruct((rows, GLA_W), BF16),
                   jax.ShapeDtypeStruct((GLA_HEADS, nc, GLA_DV, GLA_DK), F32)],
        scratch_shapes=[pltpu.VMEM((GLA_DV, GLA_DK), F32), pltpu.VMEM((CHUNK, GLA_DV), F32),
                        pltpu.VMEM((CHUNK, GLA_DK), F32)],
        compiler_params=pltpu.CompilerParams(dimension_semantics=("parallel", "arbitrary")),
    )(proj, proj, proj, proj, proj, wgate, bgate, normw)


def _gla_bwd(proj, wgate, bgate, normw, o_gla, d_oc, states):
    rows = proj.shape[0]
    nc = rows // CHUNK

    def rn(n):
        return nc - 1 - n

    def body(q_ref, k_ref, v_ref, z_ref, gl_ref, wg_ref, bg_ref, w_ref, o_ref, do_ref, st_ref,
             dq_ref, dk_ref, dv_ref, dz_ref, dl_ref, dw_ref, dbg_ref,
             ds_scr, dq_scr, dk_scr, dv_scr, db_scr, b_scr):
        n = pl.program_id(1)
        cn = rn(n)

        @pl.when(n == 0)
        def _():
            ds_scr[...] = jnp.zeros_like(ds_scr)
            dw_ref[...] = jnp.zeros_like(dw_ref)
            dbg_ref[...] = jnp.zeros_like(dbg_ref)

        q = q_ref[...] * (GLA_DK ** -0.5)
        k = k_ref[...]
        v = v_ref[...]
        vb = v.astype(BF16)
        do, dz, dw = _gate_bwd(do_ref[...], o_ref[...], z_ref[...], w_ref[...])
        dz_ref[...] = dz.astype(BF16)
        dw_ref[0] += dw
        logit, la, live = _gla_log_decay(gl_ref[...], wg_ref[...], bg_ref[...], cn)
        b = _tri_sum(la, False)
        b_scr[...] = b
        b_last = b_scr[CHUNK - 1:CHUNK, :]
        e_last = jnp.exp(b_last)
        st = st_ref[0, 0]
        g1 = ds_scr[...]
        eb = jnp.exp(b)
        qe = q * eb
        dqe = _dot(do, st)
        dq_scr[...] = dqe * eb
        db_scr[...] = dqe * qe
        ekb = jnp.exp(b_last - b)
        kdec = k * ekb
        dkdec = _dot(v, g1)
        dv_scr[...] = _dot(kdec, g1, NT)
        dk_scr[...] = dkdec * ekb
        wk = dkdec * kdec
        db_scr[...] -= wk
        dbl = jnp.sum(wk, axis=0, keepdims=True) + jnp.sum(g1 * st, axis=0, keepdims=True) * e_last
        ds_scr[...] = g1 * e_last + _dot(do, qe, TN)
        rowc = lax.broadcasted_iota(jnp.int32, (CHUNK, 1), 0)
        rows16 = lax.broadcasted_iota(jnp.int32, (SUB, 1), 0)
        da_full = _dot(do, vb, NT)
        a_tot = jnp.zeros((CHUNK, CHUNK), F32)
        for s in range(1, NSUB):
            lo = s * SUB
            bref = b_scr[lo - 1:lo, :]
            in_s = jnp.logical_and(rowc >= lo, rowc < lo + SUB)
            eq = jnp.exp(jnp.where(in_s, b - bref, -1e30))
            ek = jnp.exp(jnp.where(rowc < lo, bref - b, -1e30))
            qh = q * eq
            kh = k * ek
            a_tot = a_tot + _dot(qh, kh, NT)
            da = jnp.where(in_s, da_full, 0.0)
            dqh = _dot(da, kh)
            dkh = _dot(da, qh, TN)
            tq = dqh * qh
            tk = dkh * kh
            dq_scr[...] += dqh * eq
            dk_scr[...] += dkh * ek
            db_scr[...] += tq - tk
            db_scr[lo - 1:lo, :] += jnp.sum(tk, axis=0, keepdims=True) - jnp.sum(tq, axis=0, keepdims=True)
        dv_scr[...] += _dot(a_tot, do, TN)
        for s in range(NSUB):
            lo = s * SUB
            qs, bs = q[lo:lo + SUB], b[lo:lo + SUB]
            dos = do[lo:lo + SUB]
            dqs = jnp.zeros((SUB, GLA_DK), F32)
            dks = jnp.zeros((SUB, GLA_DK), F32)
            dbs = jnp.zeros((SUB, GLA_DK), F32)
            dvs = jnp.zeros((SUB, GLA_DV), F32)
            for j in range(SUB):
                pick = rows16 == j
                r = lo + j
                kj, vj, bj = k_ref[r:r + 1, :], v_ref[r:r + 1, :], b_scr[r:r + 1, :]
                e = jnp.exp(jnp.where(rows16 >= j, bs - bj, -1e30))
                qe_j = qs * e
                col = jnp.sum(qe_j * kj, axis=1, keepdims=True)
                dcol = jnp.sum(dos * vj, axis=1, keepdims=True)
                dvs = dvs + jnp.where(pick, jnp.sum(col * dos, axis=0, keepdims=True), 0.0)
                m = dcol * e
                dqs = dqs + m * kj
                mq = m * qs
                dks = dks + jnp.where(pick, jnp.sum(mq, axis=0, keepdims=True), 0.0)
                t = mq * kj
                dbs = dbs + t - jnp.where(pick, jnp.sum(t, axis=0, keepdims=True), 0.0)
            dq_scr[lo:lo + SUB, :] += dqs
            dk_scr[lo:lo + SUB, :] += dks
            dv_scr[lo:lo + SUB, :] += dvs
            db_scr[lo:lo + SUB, :] += dbs
        db_scr[CHUNK - 1:CHUNK, :] += dbl
        dla = _tri_sum(db_scr[...], True)
        dlogit = jnp.where(live, dla * (1.0 / GLA_TAU) * _sigmoid(-logit), 0.0)
        dl_ref[...] = dlogit
        dbg_ref[0] += jnp.sum(dlogit, axis=0, keepdims=True)
        dq_ref[...] = (dq_scr[...] * (GLA_DK ** -0.5)).astype(BF16)
        dk_ref[...] = dk_scr[...].astype(BF16)
        dv_ref[...] = dv_scr[...].astype(BF16)

    in_specs = _gla_in_specs(True, nc) + [
        pl.BlockSpec((CHUNK, GLA_DV), lambda h, n: (rn(n), h)),
        pl.BlockSpec((CHUNK, GLA_DV), lambda h, n: (rn(n), h)),
        pl.BlockSpec((1, 1, GLA_DV, GLA_DK), lambda h, n: (h, rn(n), 0, 0)),
    ]
    return pl.pallas_call(
        body, name="gla_bwd", grid=(GLA_HEADS, nc),
        in_specs=in_specs,
        out_specs=[pl.BlockSpec((CHUNK, GLA_DK), lambda h, n: (rn(n), h)),
                   pl.BlockSpec((CHUNK, GLA_DK), lambda h, n: (rn(n), h)),
                   pl.BlockSpec((CHUNK, GLA_DV), lambda h, n: (rn(n), h)),
                   pl.BlockSpec((CHUNK, GLA_DV), lambda h, n: (rn(n), h)),
                   pl.BlockSpec((CHUNK, GLA_DK), lambda h, n: (rn(n), h)),
                   pl.BlockSpec((1, 1, GLA_DV), lambda h, n: (h, 0, 0)),
                   pl.BlockSpec((1, 1, GLA_DK), lambda h, n: (h, 0, 0))],
        out_shape=[jax.ShapeDtypeStruct((rows, GLA_QK), BF16), jax.ShapeDtypeStruct((rows, GLA_QK), BF16),
                   jax.ShapeDtypeStruct((rows, GLA_W), BF16), jax.ShapeDtypeStruct((rows, GLA_W), BF16),
                   jax.ShapeDtypeStruct((rows, GLA_QK), F32),
                   jax.ShapeDtypeStruct((GLA_HEADS, 1, GLA_DV), F32),
                   jax.ShapeDtypeStruct((GLA_HEADS, 1, GLA_DK), F32)],
        scratch_shapes=[pltpu.VMEM((GLA_DV, GLA_DK), F32), pltpu.VMEM((CHUNK, GLA_DK), F32),
                        pltpu.VMEM((CHUNK, GLA_DK), F32), pltpu.VMEM((CHUNK, GLA_DV), F32),
                        pltpu.VMEM((CHUNK, GLA_DK), F32), pltpu.VMEM((CHUNK, GLA_DK), F32)],
        compiler_params=pltpu.CompilerParams(dimension_semantics=("parallel", "arbitrary")),
    )(proj, proj, proj, proj, proj, wgate, bgate, normw, o_gla, d_oc, states)


def _adamw(name, w, g, m, v):
    rows, cols = w.shape
    tm = rows
    for cand in (256, 128, 64, 32, 16, 8):
        if rows % cand == 0:
            tm = cand
            break
    c1 = 1.0 - ADAM_B1 ** ADAM_STEP
    c2 = 1.0 - ADAM_B2 ** ADAM_STEP

    def body(w_ref, g_ref, m_ref, v_ref, d_ref, nm_ref, nv_ref):
        gv = g_ref[...]
        nm = ADAM_B1 * m_ref[...] + (1.0 - ADAM_B1) * gv
        nv = ADAM_B2 * v_ref[...] + (1.0 - ADAM_B2) * (gv * gv)
        nm_ref[...] = nm
        nv_ref[...] = nv
        d_ref[...] = -ADAM_LR * ((nm / c1) / (jnp.sqrt(nv / c2) + ADAM_EPS) + ADAM_WD * w_ref[...])

    blk = pl.BlockSpec((tm, cols), lambda i: (i, 0))
    return pl.pallas_call(
        body, name=name, grid=(rows // tm,),
        in_specs=[blk] * 4, out_specs=[blk] * 3,
        out_shape=[jax.ShapeDtypeStruct((rows, cols), F32)] * 3,
    )(w, g, m, v)


def _place():
    x, y, c = lax.axis_index("x"), lax.axis_index("y"), lax.axis_index("c")
    chips = [(1 - x, y), (x, 1 - y), (1 - x, 1 - y)]
    return x, y, c, chips


ANY = pl.BlockSpec(memory_space=pl.ANY)


def _gather_weights(shards, kinds):
    n_arr = len(shards)

    def out_struct(a, kind):
        r, cc = a.shape
        if kind == "row":
            return jax.ShapeDtypeStruct((N_SHARD * r, cc), a.dtype)
        if kind == "col":
            return jax.ShapeDtypeStruct((r, N_SHARD * cc), a.dtype)
        return jax.ShapeDtypeStruct((N_SHARD, r, cc), a.dtype)

    def body(*refs):
        ins = refs[:n_arr]
        outs = refs[n_arr:2 * n_arr]
        send_sems, recv_sems, local_sems = refs[2 * n_arr:]
        x, y, c, chips = _place()
        mine = 2 * x + y
        sibling = (x, y, 1 - c)

        def window(i, shard, half):
            r, cc = shards[i].shape
            hr = r // 2
            if kinds[i] == "row":
                return outs[i].at[pl.ds(_mo(shard * r + half * hr, 8), hr), :]
            if kinds[i] == "col":
                return outs[i].at[pl.ds(_mo(half * hr, 8), hr), pl.ds(_mo(shard * cc, 128), cc)]
            return outs[i].at[shard, pl.ds(_mo(half * hr, 8), hr), :]

        def src_half(i, half):
            hr = shards[i].shape[0] // 2
            return ins[i].at[pl.ds(_mo(half * hr, 8), hr), :]

        def copy(i, slot, src, dst, to):
            return pltpu.make_async_remote_copy(
                src_ref=src, dst_ref=dst, send_sem=send_sems.at[i, slot], recv_sem=recv_sems.at[i, slot],
                device_id=to, device_id_type=MESH)

        local = []
        for i in range(n_arr):
            for half in range(2):
                cp = pltpu.make_async_copy(src_half(i, half), window(i, mine, half), local_sems.at[i, half])
                cp.start()
                local.append(cp)
        first = []
        for i in range(n_arr):
            for j, chip in enumerate(chips):
                cp = copy(i, j, src_half(i, c), window(i, mine, c), (*chip, c))
                cp.start()
                first.append(cp)
        passed = []
        for i in range(n_arr):
            for j, chip in enumerate(chips):
                theirs = 2 * chip[0] + chip[1]
                copy(i, j, src_half(i, c), window(i, theirs, c), (*chip, c)).wait_recv()
                cp = copy(i, 3 + j, window(i, theirs, c), window(i, theirs, c), sibling)
                cp.start()
                passed.append(cp)
        for i in range(n_arr):
            for j, chip in enumerate(chips):
                theirs = 2 * chip[0] + chip[1]
                copy(i, 3 + j, window(i, theirs, 1 - c), window(i, theirs, 1 - c), sibling).wait_recv()
        for cp in first + passed:
            cp.wait_send()
        for cp in local:
            cp.wait()

    return pl.pallas_call(
        body, name="gather_weights",
        in_specs=[ANY] * n_arr, out_specs=[ANY] * n_arr,
        out_shape=[out_struct(a, kd) for a, kd in zip(shards, kinds)],
        scratch_shapes=[pltpu.SemaphoreType.DMA((n_arr, 6)), pltpu.SemaphoreType.DMA((n_arr, 6)),
                        pltpu.SemaphoreType.DMA((n_arr, 2))],
        compiler_params=pltpu.CompilerParams(has_side_effects=True),
    )(*shards)


def _allreduce_small(buf):
    rows, cols = buf.shape

    def body(in_ref, out_ref, sib_ref, pair_ref, far_ref, send_sems, recv_sems):
        x, y, c, chips = _place()
        sibling = (x, y, 1 - c)
        to_sib = pltpu.make_async_remote_copy(
            src_ref=in_ref, dst_ref=sib_ref, send_sem=send_sems.at[0], recv_sem=recv_sems.at[0],
            device_id=sibling, device_id_type=MESH)
        to_sib.start()
        to_sib.wait()
        pair_ref[...] = in_ref[...] + sib_ref[...]
        far = [pltpu.make_async_remote_copy(
            src_ref=pair_ref, dst_ref=far_ref.at[j], send_sem=send_sems.at[1 + j], recv_sem=recv_sems.at[1 + j],
            device_id=(*chip, c), device_id_type=MESH) for j, chip in enumerate(chips)]
        for cp in far:
            cp.start()
        for cp in far:
            cp.wait()
        out_ref[...] = (pair_ref[...] + far_ref[1]) + (far_ref[0] + far_ref[2])

    vm = pl.BlockSpec(memory_space=pltpu.VMEM)
    return pl.pallas_call(
        body, name="allreduce_small",
        in_specs=[vm], out_specs=vm,
        out_shape=jax.ShapeDtypeStruct((rows, cols), F32),
        scratch_shapes=[pltpu.VMEM((rows, cols), F32), pltpu.VMEM((rows, cols), F32),
                        pltpu.VMEM((3, rows, cols), F32),
                        pltpu.SemaphoreType.DMA((4,)), pltpu.SemaphoreType.DMA((4,))],
        compiler_params=pltpu.CompilerParams(has_side_effects=True),
    )(buf)


def _shard_window(ref, kind, shard_shape, shard, half):
    r, cc = shard_shape
    hr = r // 2
    if kind == "row":
        return ref.at[pl.ds(_mo(shard * r + half * hr, 8), hr), :]
    if kind == "col":
        return ref.at[pl.ds(_mo(half * hr, 8), hr), pl.ds(_mo(shard * cc, 128), cc)]
    return ref.at[shard, pl.ds(_mo(half * hr, 8), hr), :]


def _rs_pair_exchange(grads, kinds, shard_shapes):
    n_arr = len(grads)

    def body(*refs):
        ins = refs[:n_arr]
        outs = refs[n_arr:2 * n_arr]
        send_sems, recv_sems = refs[2 * n_arr:]
        x, y, c, _ = _place()
        sibling = (x, y, 1 - c)
        cps = []
        for i in range(n_arr):
            for s in range(N_SHARD):
                cp = pltpu.make_async_remote_copy(
                    src_ref=_shard_window(ins[i], kinds[i], shard_shapes[i], s, 1 - c), dst_ref=outs[i].at[s],
                    send_sem=send_sems.at[i, s], recv_sem=recv_sems.at[i, s],
                    device_id=sibling, device_id_type=MESH)
                cp.start()
                cps.append(cp)
        for cp in cps:
            cp.wait()

    return pl.pallas_call(
        body, name="rs_pair_exchange",
        in_specs=[ANY] * n_arr, out_specs=[ANY] * n_arr,
        out_shape=[jax.ShapeDtypeStruct((N_SHARD, r // 2, cc), F32) for (r, cc) in shard_shapes],
        scratch_shapes=[pltpu.SemaphoreType.DMA((n_arr, N_SHARD)), pltpu.SemaphoreType.DMA((n_arr, N_SHARD))],
        compiler_params=pltpu.CompilerParams(has_side_effects=True),
    )(*grads)


def _rs_pair_add(name, grad, got, kind, shard_shape, c):
    r, cc = shard_shape
    hr = r // 2
    tr = hr
    for cand in (256, 128, 64, 32, 16):
        if hr % cand == 0:
            tr = cand
            break
    nb = hr // tr

    if kind == "row":
        g_spec = pl.BlockSpec((tr, cc), lambda s, i, cr: (s * 2 * nb + cr[0] * nb + i, 0))
    elif kind == "col":
        g_spec = pl.BlockSpec((tr, cc), lambda s, i, cr: (cr[0] * nb + i, s))
    else:
        g_spec = pl.BlockSpec((None, tr, cc), lambda s, i, cr: (s, cr[0] * nb + i, 0))
    t_spec = pl.BlockSpec((None, tr, cc), lambda s, i, cr: (s, i, 0))

    def body(c_ref, g_ref, t_ref, p_ref, pb_ref):
        p = g_ref[...] + t_ref[...]
        p_ref[...] = p
        pb_ref[...] = p.astype(BF16)

    return pl.pallas_call(
        body, name=name,
        grid_spec=pltpu.PrefetchScalarGridSpec(
            num_scalar_prefetch=1, grid=(N_SHARD, nb),
            in_specs=[g_spec, t_spec], out_specs=[t_spec, t_spec]),
        out_shape=[jax.ShapeDtypeStruct((N_SHARD, hr, cc), F32), jax.ShapeDtypeStruct((N_SHARD, hr, cc), BF16)],
    )(c, grad, got)


def _rs_chip_exchange(pairs_bf16):
    n_arr = len(pairs_bf16)

    def body(*refs):
        ins = refs[:n_arr]
        outs = refs[n_arr:2 * n_arr]
        send_sems, recv_sems = refs[2 * n_arr:]
        x, y, c, chips = _place()
        cps = []
        for i in range(n_arr):
            for j, chip in enumerate(chips):
                cp = pltpu.make_async_remote_copy(
                    src_ref=ins[i].at[2 * chip[0] + chip[1]], dst_ref=outs[i].at[j],
                    send_sem=send_sems.at[i, j], recv_sem=recv_sems.at[i, j],
                    device_id=(*chip, c), device_id_type=MESH)
                cp.start()
                cps.append(cp)
        for cp in cps:
            cp.wait()

    return pl.pallas_call(
        body, name="rs_chip_exchange",
        in_specs=[ANY] * n_arr, out_specs=[ANY] * n_arr,
        out_shape=[jax.ShapeDtypeStruct((3,) + a.shape[1:], BF16) for a in pairs_bf16],
        scratch_shapes=[pltpu.SemaphoreType.DMA((n_arr, 3)), pltpu.SemaphoreType.DMA((n_arr, 3))],
        compiler_params=pltpu.CompilerParams(has_side_effects=True),
    )(*pairs_bf16)


def _rs_chip_add(name, pair_f32, got, shard_shape, mine_c):
    r, cc = shard_shape
    hr = r // 2
    tr = hr
    for cand in (256, 128, 64, 32, 16):
        if hr % cand == 0:
            tr = cand
            break
    nb = hr // tr

    def body(mc_ref, p_ref, t0_ref, t1_ref, t2_ref, o_ref):
        o_ref[...] = (p_ref[...] + t1_ref[...].astype(F32)) + (t0_ref[...].astype(F32) + t2_ref[...].astype(F32))

    def far(j):
        return pl.BlockSpec((None, tr, cc), lambda i, mc: (j, i, 0))

    return pl.pallas_call(
        body, name=name,
        grid_spec=pltpu.PrefetchScalarGridSpec(
            num_scalar_prefetch=1, grid=(nb,),
            in_specs=[pl.BlockSpec((None, tr, cc), lambda i, mc: (mc[0], i, 0)), far(0), far(1), far(2)],
            out_specs=pl.BlockSpec((tr, cc), lambda i, mc: (mc[1] * nb + i, 0))),
        out_shape=jax.ShapeDtypeStruct((r, cc), F32),
    )(mine_c, pair_f32, got, got, got)


def _rs_pair_share(halves, shard_shapes):
    n_arr = len(halves)

    def body(*refs):
        ins = refs[:n_arr]
        outs = refs[n_arr:2 * n_arr]
        send_sems, recv_sems = refs[2 * n_arr:]
        x, y, c, _ = _place()
        sibling = (x, y, 1 - c)
        cps = []
        for i in range(n_arr):
            hr = shard_shapes[i][0] // 2
            rows = pl.ds(_mo(c * hr, 8), hr)
            cp = pltpu.make_async_remote_copy(
                src_ref=outs[i].at[rows, :], dst_ref=outs[i].at[rows, :],
                send_sem=send_sems.at[i], recv_sem=recv_sems.at[i],
                device_id=sibling, device_id_type=MESH)
            cp.start()
            cps.append(cp)
        for cp in cps:
            cp.wait()

    return pl.pallas_call(
        body, name="rs_pair_share",
        in_specs=[ANY] * n_arr, out_specs=[ANY] * n_arr,
        out_shape=[jax.ShapeDtypeStruct(s, F32) for s in shard_shapes],
        input_output_aliases={i: i for i in range(n_arr)},
        scratch_shapes=[pltpu.SemaphoreType.DMA((n_arr,)), pltpu.SemaphoreType.DMA((n_arr,))],
        compiler_params=pltpu.CompilerParams(has_side_effects=True),
    )(*halves)


def _pack(arrays):
    flat = []
    for a in arrays:
        v = a.reshape(-1).astype(F32)
        flat.append(jnp.pad(v, (0, (-v.shape[0]) % SMALL_COLS)))
    buf = jnp.concatenate(flat).reshape(-1, SMALL_COLS)
    return jnp.pad(buf, ((0, (-buf.shape[0]) % 8), (0, 0)))


def _unpack(buf, shapes):
    out = []
    row = 0
    for s in shapes:
        size = math.prod(s)
        nrow = -(-size // SMALL_COLS)
        out.append(buf[row:row + nrow].reshape(-1)[:size].reshape(s))
        row += nrow
    return out


def kernel(x, meta, norm_ab_w, w_in_ab, ret_norm_w, s5_lam_re, s5_lam_im, s5_log_dt, s5_b_re, s5_b_im, s5_c_re, s5_c_im, s5_d, s5_w_glu, w_out_ab, norm_c_w, w_in_c, gla_w_gate, gla_b_gate, gla_norm_w, w_out_c, final_norm_w, loss_target, m_meta, m_norm_ab_w, m_w_in_ab, m_ret_norm_w, m_s5_lam_re, m_s5_lam_im, m_s5_log_dt, m_s5_b_re, m_s5_b_im, m_s5_c_re, m_s5_c_im, m_s5_d, m_s5_w_glu, m_w_out_ab, m_norm_c_w, m_w_in_c, m_gla_w_gate, m_gla_b_gate, m_gla_norm_w, m_w_out_c, m_final_norm_w, v_meta, v_norm_ab_w, v_w_in_ab, v_ret_norm_w, v_s5_lam_re, v_s5_lam_im, v_s5_log_dt, v_s5_b_re, v_s5_b_im, v_s5_c_re, v_s5_c_im, v_s5_d, v_s5_w_glu, v_w_out_ab, v_norm_c_w, v_w_in_c, v_gla_w_gate, v_gla_b_gate, v_gla_norm_w, v_w_out_c, v_final_norm_w):
    seq = x.shape[1]
    rows = seq + CHUNK
    xi, yi, ci = lax.axis_index("x"), lax.axis_index("y"), lax.axis_index("c")
    mine = 2 * xi + yi
    c_arr = jnp.reshape(ci, (1,)).astype(jnp.int32)
    mine_c = jnp.stack([mine, ci]).astype(jnp.int32)

    small_shard = _pack([meta, norm_c_w, gla_norm_w, gla_b_gate, gla_w_gate[0]])
    srows = small_shard.shape[0]
    big = [w_in_ab[0].astype(BF16), w_out_ab[0].astype(BF16), w_in_c[0].astype(BF16),
           w_out_c[0].astype(BF16), s5_w_glu[0].astype(BF16)]
    kinds = ["col", "row", "stack", "row", "row"]
    wab, wout_ab, wc_st, wout_c, wglu, small_all = _gather_weights(big + [small_shard], kinds + ["stack"])
    wc = jnp.concatenate([wc_st[j] for j in range(N_SHARD)] + [jnp.zeros((D_MODEL, IN_C_PAD - IN_C), BF16)], axis=1)
    q4 = D_MODEL // N_SHARD
    g4 = GLA_QK // N_SHARD
    parts = [_unpack(small_all[j], [(N_META, q4), (1, q4), (1, q4), (1, g4), (GLA_RANK, g4)]) for j in range(N_SHARD)]
    meta_f, norm_c_f, gla_norm_f, bgate_f, wgate_f = [jnp.concatenate([p[i] for p in parts], axis=1) for i in range(5)]
    wgate_pad = jnp.pad(wgate_f, ((0, 128 - GLA_RANK), (0, 0)))

    h0 = jnp.concatenate([jnp.zeros((PAD, D_MODEL), F32), meta_f, x[0]], axis=0)
    cosf, sinf = _rope_tables(rows)
    rtab = _ret_tables()
    ab_re, ab_im, bb_re, bb_im = _s5_discretize(s5_lam_re[0], s5_lam_im[0], s5_log_dt[0], s5_b_re[0], s5_b_im[0])
    ab = (ab_re, ab_im)
    bd_b = (_bdiag_in(bb_re), _bdiag_in(bb_im))
    bd_c = (_bdiag_out(s5_c_re[0]), _bdiag_out(s5_c_im[0]))

    tm = _row_tile(rows, 1408)
    tmk = _row_tile(rows, 1408)
    hn0 = _rms_fwd("norm_ab", h0, norm_ab_w)
    proj0 = _matmul("in_proj_ab", hn0, wab, NN, rows, IN_AB, D_MODEL, tm=tm, tn=512, tk=D_MODEL)
    o_ret, o_a, ret_states = _ret_fwd(proj0, cosf, sinf, rtab, ret_norm_w)
    y_s5, g_s5, s5_er, s5_ei = _s5_fwd(proj0, ab, bd_b, bd_c, s5_d)
    zb_blk = (2 * RET_QK + 2 * RET_W + S5_W) // 512

    def glu_out(acc, gv, z):
        return gv.astype(F32) * _sigmoid(acc) * (z * _sigmoid(z))

    t_glu = _matmul("glu", g_s5, wglu, NN, rows, S5_W, S5_W, tm=tm, tn=512, tk=S5_W)
    o_b = _matmul("glu_out", g_s5, wglu, NN, rows, S5_W, S5_W, tm=tm, tn=512, tk=S5_W, out_dtype=BF16,
                  extras=[(g_s5, (tm, 512), lambda i, j, kk: (i, j)),
                          (proj0, (tm, 512), lambda i, j, kk: (i, zb_blk + j))],
                  epilogue=glu_out)
    mix = jnp.concatenate([o_a, o_b], axis=1)
    h1 = _matmul("out_proj_ab", mix, wout_ab, NN, rows, D_MODEL, OUT_AB, tm=tm, tn=512, tk=1024,
                 extras=[(h0, (tm, 512), lambda i, j, kk: (i, j))], epilogue=lambda acc, r: acc + r)

    hn1 = _rms_fwd("norm_c", h1, norm_c_f)
    proj1 = _matmul("in_proj_c", hn1, wc, NN, rows, IN_C_PAD, D_MODEL, tm=tm, tn=896, tk=D_MODEL)
    o_gla, o_c, gla_states = _gla_fwd(proj1, wgate_pad, bgate_f, gla_norm_f)
    h2 = _matmul("out_proj_c", o_c, wout_c, NN, rows, D_MODEL, GLA_W, tm=tm, tn=512, tk=GLA_W,
                 extras=[(h1, (tm, 512), lambda i, j, kk: (i, j))], epilogue=lambda acc, r: acc + r)
    loss_dev, dh2, d_final = _final_loss(h2, final_norm_w.reshape(1, D_MODEL), loss_target[0])

    g_wout_c = _matmul("d_w_out_c", o_c, dh2, TN, GLA_W, D_MODEL, rows, tm=1024, tn=1024, tk=tmk)
    d_oc = _matmul("d_o_c", dh2, wout_c, NT, rows, GLA_W, D_MODEL, tm=tm, tn=512, tk=1024)
    dq1, dk1, dv1, dz1, dlogit, d_gla_norm, d_bgate = _gla_bwd(proj1, wgate_pad, bgate_f, gla_norm_f, o_gla, d_oc, gla_states)
    gl_blk = (2 * GLA_QK + 2 * GLA_W) // 128
    dgl = _matmul("d_g_low", dlogit, wgate_pad, NT, rows, 128, GLA_QK, tm=tm, tn=128, tk=GLA_QK, out_dtype=BF16)
    g_wgate = _matmul("d_w_gate", proj1, dlogit, TN, 128, GLA_QK, rows, tm=128, tn=GLA_QK, tk=tmk, a_off=(0, gl_blk))
    dproj1 = jnp.concatenate([dq1, dk1, dv1, dz1, dgl], axis=1)
    g_wc = _matmul("d_w_in_c", hn1, dproj1, TN, D_MODEL, IN_C_PAD, rows, tm=1024, tn=896, tk=tmk)
    dhn1 = _matmul("d_hn1", dproj1, wc, NT, rows, D_MODEL, IN_C_PAD, tm=tm, tn=512, tk=896)
    dh1, d_norm_c = _rms_bwd("norm_c_bwd", dhn1, h1, norm_c_f, dh2)

    g_wout_ab = _matmul("d_w_out_ab", mix, dh1, TN, OUT_AB, D_MODEL, rows, tm=1024, tn=1024, tk=tmk)
    dmix = _matmul("d_mix", dh1, wout_ab, NT, rows, OUT_AB, D_MODEL, tm=tm, tn=512, tk=1024)
    dq0, dk0, dv0, dza, d_ret_norm = _ret_bwd(proj0, cosf, sinf, rtab, ret_norm_w, o_ret, dmix, ret_states)
    dzb, dt_glu, dg_direct = _s5_gate_bwd(dmix, g_s5, t_glu, proj0)
    g_wglu = _matmul("d_w_glu", g_s5, dt_glu, TN, S5_W, S5_W, rows, tm=1024, tn=1024, tk=tmk)
    dy_s5 = _matmul("d_y_s5", dt_glu, wglu, NT, rows, S5_W, S5_W, tm=tm, tn=512, tk=S5_W,
                    extras=[(dg_direct, (tm, 512), lambda i, j, kk: (i, j)),
                            (y_s5, (tm, 512), lambda i, j, kk: (i, j))],
                    epilogue=lambda acc, dg, yv: (acc + dg) * _gelu_grad(yv))
    du, dbr_d, dbi_d, dcr_d, dci_d, dar_p, dai_p, dd_p = _s5_bwd(proj0, dy_s5, ab, bd_b, bd_c, s5_d, (s5_er, s5_ei))
    dproj0 = jnp.concatenate([dq0, dk0, dv0, dza, du, dzb], axis=1)
    g_wab = _matmul("d_w_in_ab", hn0, dproj0, TN, D_MODEL, IN_AB, rows, tm=1024, tn=1024, tk=tmk)
    dhn0 = _matmul("d_hn0", dproj0, wab, NT, rows, D_MODEL, IN_AB, tm=tm, tn=512, tk=2048)
    dh0, d_norm_ab = _rms_bwd("norm_ab_bwd", dhn0, h0, norm_ab_w, dh1)
    grad_x = dh0[CHUNK:][None]

    d_ab_re = jnp.sum(dar_p, axis=1).reshape(S5_G, S5_P)
    d_ab_im = jnp.sum(dai_p, axis=1).reshape(S5_G, S5_P)
    small_local = [loss_dev, dh0[PAD:CHUNK], d_norm_ab, d_ret_norm.reshape(1, RET_W), d_ab_re, d_ab_im,
                   _bdiag_in_extract(dbr_d), _bdiag_in_extract(dbi_d),
                   _bdiag_out_extract(dcr_d), _bdiag_out_extract(dci_d),
                   jnp.sum(dd_p, axis=1).reshape(1, S5_W), d_norm_c, g_wgate[:GLA_RANK],
                   d_bgate.reshape(1, GLA_QK), d_gla_norm.reshape(1, GLA_W), d_final]
    small_shapes = [a.shape for a in small_local]
    summed = _unpack(_allreduce_small(_pack(small_local)), small_shapes)
    (loss, g_meta_f, g_norm_ab, g_ret_norm, g_ab_re, g_ab_im, g_bb_re, g_bb_im, g_c_re, g_c_im, g_d,
     g_norm_c_f, g_wgate_f, g_bgate_f, g_gla_norm_f, g_final) = summed
    _, s5_vjp = jax.vjp(_s5_discretize, s5_lam_re[0], s5_lam_im[0], s5_log_dt[0], s5_b_re[0], s5_b_im[0])
    g_lam_re, g_lam_im, g_log_dt, g_b_re, g_b_im = s5_vjp((g_ab_re, g_ab_im, g_bb_re, g_bb_im))

    def take(a, width):
        return lax.dynamic_slice_in_dim(a, mine * width, width, axis=1)

    g_wc_st = jnp.stack([g_wc[:, j * (IN_C // N_SHARD):(j + 1) * (IN_C // N_SHARD)] for j in range(N_SHARD)])
    full = [g_wab, g_wout_ab, g_wc_st, g_wout_c, g_wglu]
    shard_shapes = [w_in_ab.shape[1:], w_out_ab.shape[1:], w_in_c.shape[1:], w_out_c.shape[1:], s5_w_glu.shape[1:]]
    names = ["w_in_ab", "w_out_ab", "w_in_c", "w_out_c", "w_glu"]
    got1 = _rs_pair_exchange(full, kinds, shard_shapes)
    pairs = [_rs_pair_add("rs_pair_add_" + nm, g, t, kd, ss, c_arr)
             for nm, g, t, kd, ss in zip(names, full, got1, kinds, shard_shapes)]
    got2 = _rs_chip_exchange([p[1] for p in pairs])
    halves = [_rs_chip_add("rs_chip_add_" + nm, p[0], t, ss, mine_c)
              for nm, p, t, ss in zip(names, pairs, got2, shard_shapes)]
    g_w_in_ab, g_w_out_ab, g_w_in_c, g_w_out_c, g_w_glu = _rs_pair_share(halves, shard_shapes)

    grads = {
        "meta": take(g_meta_f, q4), "norm_ab_w": g_norm_ab, "w_in_ab": g_w_in_ab[None], "ret_norm_w": g_ret_norm,
        "s5_lam_re": g_lam_re[None], "s5_lam_im": g_lam_im[None], "s5_log_dt": g_log_dt[None],
        "s5_b_re": g_b_re[None], "s5_b_im": g_b_im[None], "s5_c_re": g_c_re[None], "s5_c_im": g_c_im[None],
        "s5_d": g_d, "s5_w_glu": g_w_glu[None], "w_out_ab": g_w_out_ab[None], "norm_c_w": take(g_norm_c_f, q4),
        "w_in_c": g_w_in_c[None], "gla_w_gate": take(g_wgate_f, g4)[None], "gla_b_gate": take(g_bgate_f, g4),
        "gla_norm_w": take(g_gla_norm_f, q4), "w_out_c": g_w_out_c[None], "final_norm_w": g_final.reshape(D_MODEL),
    }
    weights = dict(meta=meta, norm_ab_w=norm_ab_w, w_in_ab=w_in_ab, ret_norm_w=ret_norm_w, s5_lam_re=s5_lam_re,
                   s5_lam_im=s5_lam_im, s5_log_dt=s5_log_dt, s5_b_re=s5_b_re, s5_b_im=s5_b_im, s5_c_re=s5_c_re,
                   s5_c_im=s5_c_im, s5_d=s5_d, s5_w_glu=s5_w_glu, w_out_ab=w_out_ab, norm_c_w=norm_c_w,
                   w_in_c=w_in_c, gla_w_gate=gla_w_gate, gla_b_gate=gla_b_gate, gla_norm_w=gla_norm_w,
                   w_out_c=w_out_c, final_norm_w=final_norm_w)
    m_in = dict(meta=m_meta, norm_ab_w=m_norm_ab_w, w_in_ab=m_w_in_ab, ret_norm_w=m_ret_norm_w,
                s5_lam_re=m_s5_lam_re, s5_lam_im=m_s5_lam_im, s5_log_dt=m_s5_log_dt, s5_b_re=m_s5_b_re,
                s5_b_im=m_s5_b_im, s5_c_re=m_s5_c_re, s5_c_im=m_s5_c_im, s5_d=m_s5_d, s5_w_glu=m_s5_w_glu,
                w_out_ab=m_w_out_ab, norm_c_w=m_norm_c_w, w_in_c=m_w_in_c, gla_w_gate=m_gla_w_gate,
                gla_b_gate=m_gla_b_gate, gla_norm_w=m_gla_norm_w, w_out_c=m_w_out_c, final_norm_w=m_final_norm_w)
    v_in = dict(meta=v_meta, norm_ab_w=v_norm_ab_w, w_in_ab=v_w_in_ab, ret_norm_w=v_ret_norm_w,
                s5_lam_re=v_s5_lam_re, s5_lam_im=v_s5_lam_im, s5_log_dt=v_s5_log_dt, s5_b_re=v_s5_b_re,
                s5_b_im=v_s5_b_im, s5_c_re=v_s5_c_re, s5_c_im=v_s5_c_im, s5_d=v_s5_d, s5_w_glu=v_s5_w_glu,
                w_out_ab=v_w_out_ab, norm_c_w=v_norm_c_w, w_in_c=v_w_in_c, gla_w_gate=v_gla_w_gate,
                gla_b_gate=v_gla_b_gate, gla_norm_w=v_gla_norm_w, w_out_c=v_w_out_c, final_norm_w=v_final_norm_w)
    order = list(weights)
    big_names = ["w_in_ab", "s5_w_glu", "w_out_ab", "w_in_c", "w_out_c"]
    small_names = [nm for nm in order if nm not in big_names]
    delta, new_m, new_v = {}, {}, {}
    for nm in big_names:
        shp = weights[nm].shape
        d2, m2, v2 = _adamw("adamw_" + nm, weights[nm][0], grads[nm][0], m_in[nm][0], v_in[nm][0])
        delta[nm], new_m[nm], new_v[nm] = d2.reshape(shp), m2.reshape(shp), v2.reshape(shp)
    sshapes = [weights[nm].shape for nm in small_names]
    d2, m2, v2 = _adamw("adamw_small", _pack([weights[nm] for nm in small_names]),
                        _pack([grads[nm] for nm in small_names]), _pack([m_in[nm] for nm in small_names]),
                        _pack([v_in[nm] for nm in small_names]))
    for nm, dd, mm, vv in zip(small_names, _unpack(d2, sshapes), _unpack(m2, sshapes), _unpack(v2, sshapes)):
        delta[nm], new_m[nm], new_v[nm] = dd, mm, vv
    grads = {nm: grads[nm].reshape(weights[nm].shape) for nm in order}
    return (loss.reshape(()), grad_x, *[grads[nm] for nm in order], *[delta[nm] for nm in order],
            *[new_m[nm] for nm in order], *[new_v[nm] for nm in order])
```

```python
import functools
import math

import jax
import jax.numpy as jnp
from jax import lax
from jax.experimental import pallas as pl
from jax.experimental.pallas import tpu as pltpu

F32 = jnp.float32
BF16 = jnp.bfloat16
MESH = pl.DeviceIdType.MESH

D_MODEL = 2048
N_META = 16
CHUNK = 128
SUB = 16
NSUB = CHUNK // SUB
PAD = CHUNK - N_META
EPS = 1e-6

RET_HEADS = 8
RET_DK = 128
RET_DV = 256
RET_QK = RET_HEADS * RET_DK
RET_W = RET_HEADS * RET_DV
ROPE_BASE = 10000.0

S5_W = 1024
S5_GH = 16
S5_G = S5_W // S5_GH
S5_P = 64
S5_TG = 8
S5_NT = S5_G // S5_TG
S5_TU = S5_TG * S5_GH
S5_TS = S5_TG * S5_P

GLA_HEADS = 4
GLA_DK = 256
GLA_DV = 512
GLA_QK = GLA_HEADS * GLA_DK
GLA_W = GLA_HEADS * GLA_DV
GLA_RANK = 16
GLA_TAU = 16.0

IN_AB = 2 * RET_QK + 2 * RET_W + 2 * S5_W
OUT_AB = RET_W + S5_W
IN_C = 2 * GLA_QK + 2 * GLA_W + GLA_RANK
IN_C_PAD = 2 * GLA_QK + 2 * GLA_W + 128

ADAM_LR = 0.001
ADAM_B1 = 0.9
ADAM_B2 = 0.999
ADAM_EPS = 1e-08
ADAM_WD = 0.01
ADAM_STEP = 10

N_SHARD = 4
SMALL_COLS = 512

NN = (((1,), (0,)), ((), ()))
NT = (((1,), (1,)), ((), ()))
TN = (((0,), (0,)), ((), ()))


def _dot(a, b, dims=NN):
    return lax.dot_general(a.astype(BF16), b.astype(BF16), dims, preferred_element_type=F32)


def _mo(v, m):
    return v if isinstance(v, int) else pl.multiple_of(v, m)


def _sigmoid(x):
    return 1.0 / (1.0 + jnp.exp(-x))


def _row_tile(rows, cap):
    n = rows // CHUNK
    best = 1
    for d in range(1, n + 1):
        if n % d == 0 and d * CHUNK <= cap:
            best = d
    return best * CHUNK


def _col_tile(cols, cap):
    n = cols // 128
    best = 1
    for d in range(1, n + 1):
        if n % d == 0 and d * 128 <= cap:
            best = d
    return best * 128


def _matmul(name, a, b, dims, m, n, k, *, tm, tn, tk, out_dtype=F32, a_off=(0, 0), b_off=(0, 0),
            extras=(), epilogue=None, out_shape=None, out_spec=None):
    nk = k // tk
    assert m % tm == 0 and n % tn == 0 and k % tk == 0, (name, m, n, k, tm, tn, tk)
    ar, ac = a_off
    br, bc = b_off
    if dims == NN:
        a_spec = pl.BlockSpec((tm, tk), lambda i, j, kk: (i + ar, kk + ac))
        b_spec = pl.BlockSpec((tk, tn), lambda i, j, kk: (kk + br, j + bc))
    elif dims == NT:
        a_spec = pl.BlockSpec((tm, tk), lambda i, j, kk: (i + ar, kk + ac))
        b_spec = pl.BlockSpec((tn, tk), lambda i, j, kk: (j + br, kk + bc))
    else:
        a_spec = pl.BlockSpec((tk, tm), lambda i, j, kk: (kk + ar, i + ac))
        b_spec = pl.BlockSpec((tk, tn), lambda i, j, kk: (kk + br, j + bc))
    n_extra = len(extras)

    def body(*refs):
        a_ref, b_ref = refs[0], refs[1]
        e_refs = refs[2:2 + n_extra]
        o_ref = refs[2 + n_extra]
        acc_ref = refs[3 + n_extra]
        kk = pl.program_id(2)

        @pl.when(kk == 0)
        def _():
            acc_ref[...] = jnp.zeros_like(acc_ref)

        acc_ref[...] += _dot(a_ref[...], b_ref[...], dims)

        @pl.when(kk == nk - 1)
        def _():
            acc = acc_ref[...]
            if epilogue is not None:
                acc = epilogue(acc, *[e[...] for e in e_refs])
            o_ref[...] = acc.astype(o_ref.dtype)

    if out_shape is None:
        out_shape = jax.ShapeDtypeStruct((m, n), out_dtype)
    if out_spec is None:
        out_spec = pl.BlockSpec((tm, tn), lambda i, j, kk: (i, j))
    return pl.pallas_call(
        body, name=name, grid=(m // tm, n // tn, nk),
        in_specs=[a_spec, b_spec] + [pl.BlockSpec(bs, im) for (_, bs, im) in extras],
        out_specs=out_spec, out_shape=out_shape,
        scratch_shapes=[pltpu.VMEM((tm, tn), F32)],
        compiler_params=pltpu.CompilerParams(dimension_semantics=("parallel", "parallel", "arbitrary")),
    )(a, b, *[e for (e, _, _) in extras])


def _rms_fwd(name, h, w):
    rows, d = h.shape
    tm = _row_tile(rows, 512)

    def body(h_ref, w_ref, o_ref):
        x = h_ref[...]
        r = lax.rsqrt(jnp.mean(x * x, axis=-1, keepdims=True) + EPS)
        o_ref[...] = (x * r * w_ref[...]).astype(BF16)

    return pl.pallas_call(
        body, name=name, grid=(rows // tm,),
        in_specs=[pl.BlockSpec((tm, d), lambda i: (i, 0)), pl.BlockSpec((1, d), lambda i: (0, 0))],
        out_specs=pl.BlockSpec((tm, d), lambda i: (i, 0)),
        out_shape=jax.ShapeDtypeStruct((rows, d), BF16),
    )(h, w)


def _rms_bwd(name, dhn, h, w, dres):
    rows, d = h.shape
    tm = _row_tile(rows, 384)

    def body(g_ref, h_ref, w_ref, r_ref, dh_ref, dw_ref):
        i = pl.program_id(0)
        x = h_ref[...]
        r = lax.rsqrt(jnp.mean(x * x, axis=-1, keepdims=True) + EPS)
        xh = x * r
        g = g_ref[...]
        gw = g * w_ref[...]
        dh_ref[...] = r_ref[...] + r * (gw - xh * jnp.mean(gw * xh, axis=-1, keepdims=True))

        @pl.when(i == 0)
        def _():
            dw_ref[...] = jnp.zeros_like(dw_ref)

        dw_ref[...] += jnp.sum(g * xh, axis=0, keepdims=True)

    return pl.pallas_call(
        body, name=name, grid=(rows // tm,),
        in_specs=[pl.BlockSpec((tm, d), lambda i: (i, 0)), pl.BlockSpec((tm, d), lambda i: (i, 0)),
                  pl.BlockSpec((1, d), lambda i: (0, 0)), pl.BlockSpec((tm, d), lambda i: (i, 0))],
        out_specs=[pl.BlockSpec((tm, d), lambda i: (i, 0)), pl.BlockSpec((1, d), lambda i: (0, 0))],
        out_shape=[jax.ShapeDtypeStruct((rows, d), F32), jax.ShapeDtypeStruct((1, d), F32)],
    )(dhn, h, w, dres)


def _final_loss(h2, w, target):
    rows, d = h2.shape

    def body(h_ref, w_ref, t_ref, loss_ref, dh_ref, dw_ref):
        i = pl.program_id(0)

        @pl.when(i == 0)
        def _():
            loss_ref[...] = jnp.zeros_like(loss_ref)
            dw_ref[...] = jnp.zeros_like(dw_ref)
            dh_ref[...] = jnp.zeros_like(dh_ref)

        @pl.when(i > 0)
        def _():
            x = h_ref[...]
            r = lax.rsqrt(jnp.mean(x * x, axis=-1, keepdims=True) + EPS)
            xh = x * r
            wv = w_ref[...]
            err = xh * wv - t_ref[...]
            loss_ref[...] += 0.5 * jnp.sum(jnp.mean(err * err, axis=-1, keepdims=True), axis=0, keepdims=True)
            g = err * (1.0 / d)
            gw = g * wv
            dh_ref[...] = r * (gw - xh * jnp.mean(gw * xh, axis=-1, keepdims=True))
            dw_ref[...] += jnp.sum(g * xh, axis=0, keepdims=True)

    return pl.pallas_call(
        body, name="final_loss", grid=(rows // CHUNK,),
        in_specs=[pl.BlockSpec((CHUNK, d), lambda i: (i, 0)), pl.BlockSpec((1, d), lambda i: (0, 0)),
                  pl.BlockSpec((CHUNK, d), lambda i: (jnp.maximum(i - 1, 0), 0))],
        out_specs=[pl.BlockSpec((1, 1), lambda i: (0, 0)), pl.BlockSpec((CHUNK, d), lambda i: (i, 0)),
                   pl.BlockSpec((1, d), lambda i: (0, 0))],
        out_shape=[jax.ShapeDtypeStruct((1, 1), F32), jax.ShapeDtypeStruct((rows, d), F32),
                   jax.ShapeDtypeStruct((1, d), F32)],
    )(h2, w, target)


def _gate_fwd(o, z, w):
    rs = lax.rsqrt(jnp.mean(o * o, axis=-1, keepdims=True) + EPS)
    return o * rs * w * (z * _sigmoid(z))


def _gate_bwd(dout, o, z, w):
    rs = lax.rsqrt(jnp.mean(o * o, axis=-1, keepdims=True) + EPS)
    yn = o * rs
    sg = _sigmoid(z)
    sil = z * sg
    dsil = sg * (1.0 + z * (1.0 - sg))
    dz = dout * yn * w * dsil
    dyn = dout * w * sil
    dw = jnp.sum(dout * yn * sil, axis=0, keepdims=True)
    do = rs * (dyn - yn * jnp.mean(dyn * yn, axis=-1, keepdims=True))
    return do, dz, dw


def _rope(t, cosf, sinf):
    return t * cosf + pltpu.roll(t, RET_DK // 2, 1) * sinf


def _rope_t(d, cosf, sinf):
    return d * cosf + pltpu.roll(d * sinf, RET_DK // 2, 1)


def _ret_tables():
    log_g = jnp.log1p(-jnp.exp2(-5.0 - jnp.arange(RET_HEADS, dtype=F32)))
    idx = jnp.arange(CHUNK, dtype=F32)
    diff = idx[:, None] - idx[None, :]
    decay = jnp.where(diff >= 0, jnp.exp(log_g[:, None, None] * jnp.maximum(diff, 0.0)), 0.0)
    kw = jnp.exp(log_g[:, None] * (CHUNK - 1 - idx))
    qw = jnp.exp(log_g[:, None] * (idx + 1.0))
    gch = jnp.exp(log_g * CHUNK)
    kw = jnp.broadcast_to(kw[:, :, None], (RET_HEADS, CHUNK, RET_DK))
    qw = jnp.broadcast_to(qw[:, :, None], (RET_HEADS, CHUNK, RET_DK))
    gch = jnp.broadcast_to(gch[:, None, None], (RET_HEADS, 1, RET_DV))
    return decay, kw, qw, gch


def _rope_tables(rows):
    pos = jnp.arange(rows, dtype=F32) - float(PAD)
    inv_freq = jnp.power(ROPE_BASE, -jnp.arange(0, RET_DK, 2, dtype=F32) / RET_DK)
    ang = pos[:, None] * inv_freq[None, :]
    cos, sin = jnp.cos(ang), jnp.sin(ang)
    return jnp.concatenate([cos, cos], axis=1), jnp.concatenate([-sin, sin], axis=1)


RET_HB = 4
RET_QB = RET_HB * RET_DK
RET_VB = RET_HB * RET_DV


def _ret_in_specs(rev, nc):
    def cn(n):
        return (nc - 1 - n) if rev else n
    kb = RET_QK // RET_QB
    vb = 2 * RET_QK // RET_VB
    zb = (2 * RET_QK + RET_W) // RET_VB
    return [
        pl.BlockSpec((CHUNK, RET_QB), lambda h, n: (cn(n), h)),
        pl.BlockSpec((CHUNK, RET_QB), lambda h, n: (cn(n), kb + h)),
        pl.BlockSpec((CHUNK, RET_VB), lambda h, n: (cn(n), vb + h)),
        pl.BlockSpec((CHUNK, RET_VB), lambda h, n: (cn(n), zb + h)),
        pl.BlockSpec((CHUNK, RET_DK), lambda h, n: (cn(n), 0)),
        pl.BlockSpec((CHUNK, RET_DK), lambda h, n: (cn(n), 0)),
        pl.BlockSpec((RET_HB, CHUNK, CHUNK), lambda h, n: (h, 0, 0)),
        pl.BlockSpec((RET_HB, CHUNK, RET_DK), lambda h, n: (h, 0, 0)),
        pl.BlockSpec((RET_HB, CHUNK, RET_DK), lambda h, n: (h, 0, 0)),
        pl.BlockSpec((RET_HB, 1, RET_DV), lambda h, n: (h, 0, 0)),
        pl.BlockSpec((1, RET_VB), lambda h, n: (0, h)),
    ]


def _ret_fwd(proj, cosf, sinf, tables, normw):
    rows = proj.shape[0]
    nc = rows // CHUNK
    decay, kw, qw, gch = tables

    def body(q_ref, k_ref, v_ref, z_ref, cos_ref, sin_ref, dm_ref, kw_ref, qw_ref, g_ref, w_ref,
             o_ref, oa_ref, st_ref, s_scr):
        n = pl.program_id(1)

        @pl.when(n == 0)
        def _():
            s_scr[...] = jnp.zeros_like(s_scr)

        cosv, sinv = cos_ref[...], sin_ref[...]
        for hh in range(RET_HB):
            qc = slice(hh * RET_DK, (hh + 1) * RET_DK)
            vc = slice(hh * RET_DV, (hh + 1) * RET_DV)
            q = _rope(q_ref[:, qc], cosv, sinv)
            k = _rope(k_ref[:, qc], cosv, sinv) * (RET_DK ** -0.5)
            v = v_ref[:, vc]
            s = s_scr[hh]
            st_ref[hh, 0] = s.astype(BF16)
            a = _dot(q, k, NT) * dm_ref[hh]
            o = _dot(a, v) + _dot(q * qw_ref[hh], s)
            s_scr[hh] = s * g_ref[hh] + _dot(k * kw_ref[hh], v, TN)
            o_ref[:, vc] = o
            oa_ref[:, vc] = _gate_fwd(o, z_ref[:, vc], w_ref[:, vc]).astype(BF16)

    return pl.pallas_call(
        body, name="ret_fwd", grid=(RET_HEADS // RET_HB, nc),
        in_specs=_ret_in_specs(False, nc),
        out_specs=[pl.BlockSpec((CHUNK, RET_VB), lambda h, n: (n, h)),
                   pl.BlockSpec((CHUNK, RET_VB), lambda h, n: (n, h)),
                   pl.BlockSpec((RET_HB, 1, RET_DK, RET_DV), lambda h, n: (h, n, 0, 0))],
        out_shape=[jax.ShapeDtypeStruct((rows, RET_W), F32), jax.ShapeDtypeStruct((rows, RET_W), BF16),
                   jax.ShapeDtypeStruct((RET_HEADS, nc, RET_DK, RET_DV), BF16)],
        scratch_shapes=[pltpu.VMEM((RET_HB, RET_DK, RET_DV), F32)],
        compiler_params=pltpu.CompilerParams(dimension_semantics=("parallel", "arbitrary")),
    )(proj, proj, proj, proj, cosf, sinf, decay, kw, qw, gch, normw)


def _ret_bwd(proj, cosf, sinf, tables, normw, o_ret, dmix, states):
    rows = proj.shape[0]
    nc = rows // CHUNK
    decay, kw, qw, gch = tables

    def rn(n):
        return nc - 1 - n

    def body(q_ref, k_ref, v_ref, z_ref, cos_ref, sin_ref, dm_ref, kw_ref, qw_ref, g_ref, w_ref,
             o_ref, do_ref, st_ref, dq_ref, dk_ref, dv_ref, dz_ref, dw_ref, ds_scr):
        n = pl.program_id(1)

        @pl.when(n == 0)
        def _():
            ds_scr[...] = jnp.zeros_like(ds_scr)
            dw_ref[...] = jnp.zeros_like(dw_ref)

        cosv, sinv = cos_ref[...], sin_ref[...]
        for hh in range(RET_HB):
            qc = slice(hh * RET_DK, (hh + 1) * RET_DK)
            vc = slice(hh * RET_DV, (hh + 1) * RET_DV)
            q = _rope(q_ref[:, qc], cosv, sinv)
            k = _rope(k_ref[:, qc], cosv, sinv) * (RET_DK ** -0.5)
            v = v_ref[:, vc]
            do, dz, dw = _gate_bwd(do_ref[:, vc], o_ref[:, vc], z_ref[:, vc], w_ref[:, vc])
            dz_ref[:, vc] = dz.astype(BF16)
            dw_ref[hh] += dw
            dm = dm_ref[hh]
            s = st_ref[hh, 0]
            g1 = ds_scr[hh]
            p = _dot(q, k, NT) * dm
            kwv = k * kw_ref[hh]
            qwv = q * qw_ref[hh]
            dp = _dot(do, v, NT)
            da = dp * dm
            dv = _dot(p, do, TN) + _dot(kwv, g1)
            dq = _dot(da, k) + _dot(do, s, NT) * qw_ref[hh]
            dk = _dot(da, q, TN) + _dot(v, g1, NT) * kw_ref[hh]
            ds_scr[hh] = g1 * g_ref[hh] + _dot(qwv, do, TN)
            dv_ref[:, vc] = dv.astype(BF16)
            dq_ref[:, qc] = _rope_t(dq, cosv, sinv).astype(BF16)
            dk_ref[:, qc] = _rope_t(dk * (RET_DK ** -0.5), cosv, sinv).astype(BF16)

    in_specs = _ret_in_specs(True, nc) + [
        pl.BlockSpec((CHUNK, RET_VB), lambda h, n: (rn(n), h)),
        pl.BlockSpec((CHUNK, RET_VB), lambda h, n: (rn(n), h)),
        pl.BlockSpec((RET_HB, 1, RET_DK, RET_DV), lambda h, n: (h, rn(n), 0, 0)),
    ]
    return pl.pallas_call(
        body, name="ret_bwd", grid=(RET_HEADS // RET_HB, nc),
        in_specs=in_specs,
        out_specs=[pl.BlockSpec((CHUNK, RET_QB), lambda h, n: (rn(n), h)),
                   pl.BlockSpec((CHUNK, RET_QB), lambda h, n: (rn(n), h)),
                   pl.BlockSpec((CHUNK, RET_VB), lambda h, n: (rn(n), h)),
                   pl.BlockSpec((CHUNK, RET_VB), lambda h, n: (rn(n), h)),
                   pl.BlockSpec((RET_HB, 1, RET_DV), lambda h, n: (h, 0, 0))],
        out_shape=[jax.ShapeDtypeStruct((rows, RET_QK), BF16), jax.ShapeDtypeStruct((rows, RET_QK), BF16),
                   jax.ShapeDtypeStruct((rows, RET_W), BF16), jax.ShapeDtypeStruct((rows, RET_W), BF16),
                   jax.ShapeDtypeStruct((RET_HEADS, 1, RET_DV), F32)],
        scratch_shapes=[pltpu.VMEM((RET_HB, RET_DK, RET_DV), F32)],
        compiler_params=pltpu.CompilerParams(dimension_semantics=("parallel", "arbitrary")),
    )(proj, proj, proj, proj, cosf, sinf, decay, kw, qw, gch, normw, o_ret, dmix, states)


def _s5_discretize(lam_re, lam_im, log_dt, b_re, b_im):
    dt = jnp.exp(log_dt)[:, None]
    mag = jnp.exp(lam_re * dt)
    ab_re, ab_im = mag * jnp.cos(lam_im * dt), mag * jnp.sin(lam_im * dt)
    den = lam_re * lam_re + lam_im * lam_im
    nr, ni = ab_re - 1.0, ab_im
    f_re = (nr * lam_re + ni * lam_im) / den
    f_im = (ni * lam_re - nr * lam_im) / den
    bb_re = f_re[..., None] * b_re - f_im[..., None] * b_im
    bb_im = f_re[..., None] * b_im + f_im[..., None] * b_re
    return ab_re, ab_im, bb_re, bb_im


def _bdiag_in(bb):
    t = bb.reshape(S5_NT, S5_TG, S5_P, S5_GH).transpose(0, 1, 3, 2)
    eye = jnp.eye(S5_TG, dtype=bb.dtype)
    full = t[:, :, :, None, :] * eye[None, :, None, :, None]
    return full.reshape(S5_NT, S5_TU, S5_TS)


def _bdiag_in_extract(dense):
    t = dense.reshape(S5_NT, S5_TG, S5_GH, S5_TG, S5_P)
    diag = jnp.stack([t[:, g, :, g, :] for g in range(S5_TG)], axis=1)
    return diag.transpose(0, 1, 3, 2).reshape(S5_G, S5_P, S5_GH)


def _bdiag_out(c):
    t = c.reshape(S5_NT, S5_TG, S5_GH, S5_P).transpose(0, 1, 3, 2)
    eye = jnp.eye(S5_TG, dtype=c.dtype)
    full = t[:, :, :, None, :] * eye[None, :, None, :, None]
    return full.reshape(S5_NT, S5_TS, S5_TU)


def _bdiag_out_extract(dense):
    t = dense.reshape(S5_NT, S5_TG, S5_P, S5_TG, S5_GH)
    diag = jnp.stack([t[:, g, :, g, :] for g in range(S5_TG)], axis=1)
    return diag.transpose(0, 1, 3, 2).reshape(S5_G, S5_GH, S5_P)


def _cmul(ar, ai, br, bi):
    return ar * br - ai * bi, ar * bi + ai * br


S5_SEG = 8
S5_STEPS = CHUNK // S5_SEG


def _seg_perm(x):
    c = x.shape[1]
    return jnp.swapaxes(x.reshape(S5_SEG, S5_STEPS, c), 0, 1).reshape(CHUNK, c)


def _seg_unperm(x):
    c = x.shape[1]
    return jnp.swapaxes(x.reshape(S5_STEPS, S5_SEG, c), 0, 1).reshape(CHUNK, c)


def _rows(x, p):
    return x[p * S5_SEG:(p + 1) * S5_SEG]


def _s5_tables(ar, ai, tr_scr, ti_scr, wfr_scr, wfi_scr, wbr_scr, wbi_scr):
    row = lax.broadcasted_iota(jnp.int32, (S5_SEG, 1), 0)
    a8r = jnp.broadcast_to(ar, (S5_SEG, S5_TS))
    a8i = jnp.broadcast_to(ai, (S5_SEG, S5_TS))
    pr, pi = a8r, a8i
    for p in range(S5_STEPS):
        tr_scr[p * S5_SEG:(p + 1) * S5_SEG, :] = pr
        ti_scr[p * S5_SEG:(p + 1) * S5_SEG, :] = pi
        if p < S5_STEPS - 1:
            pr, pi = _cmul(pr, pi, a8r, a8i)
    wr, wi = pr, pi
    sh = 1
    while sh < S5_SEG:
        keep = row >= sh
        sr = jnp.where(keep, pltpu.roll(wr, sh, 0), 1.0)
        si = jnp.where(keep, pltpu.roll(wi, sh, 0), 0.0)
        wr, wi = _cmul(wr, wi, sr, si)
        sh *= 2
    wfr_scr[...] = wr
    wfi_scr[...] = wi
    wr, wi = pr, -pi
    sh = 1
    while sh < S5_SEG:
        keep = row < S5_SEG - sh
        sr = jnp.where(keep, pltpu.roll(wr, S5_SEG - sh, 0), 1.0)
        si = jnp.where(keep, pltpu.roll(wi, S5_SEG - sh, 0), 0.0)
        wr, wi = _cmul(wr, wi, sr, si)
        sh *= 2
    wbr_scr[...] = wr
    wbi_scr[...] = wi


def _seg_scan(vr, vi, ar, ai, tr_scr, ti_scr, wr_scr, wi_scr, c0r, c0i, down):
    row = lax.broadcasted_iota(jnp.int32, (S5_SEG, 1), 0)
    sgn = 1.0 if down else -1.0
    order = list(range(S5_STEPS)) if down else list(range(S5_STEPS - 1, -1, -1))
    xr, xi = _rows(vr, order[0]), _rows(vi, order[0])
    loc = {order[0]: (xr, xi)}
    for p in order[1:]:
        mr, mi = _cmul(ar, sgn * ai, xr, xi)
        xr, xi = mr + _rows(vr, p), mi + _rows(vi, p)
        loc[p] = (xr, xi)
    last = S5_STEPS - 1
    mr, mi = tr_scr[last * S5_SEG:(last + 1) * S5_SEG, :], sgn * ti_scr[last * S5_SEG:(last + 1) * S5_SEG, :]
    er, ei = xr, xi
    sh = 1
    while sh < S5_SEG:
        if down:
            keep = row >= sh
            sr, si = pltpu.roll(er, sh, 0), pltpu.roll(ei, sh, 0)
        else:
            keep = row < S5_SEG - sh
            sr, si = pltpu.roll(er, S5_SEG - sh, 0), pltpu.roll(ei, S5_SEG - sh, 0)
        pr, pi = _cmul(mr, mi, jnp.where(keep, sr, 0.0), jnp.where(keep, si, 0.0))
        er, ei = er + pr, ei + pi
        mr, mi = _cmul(mr, mi, mr, mi)
        sh *= 2
    pr, pi = _cmul(wr_scr[...], wi_scr[...], c0r, c0i)
    er, ei = er + pr, ei + pi
    if down:
        nr = jnp.where(row == 0, c0r, pltpu.roll(er, 1, 0))
        ni = jnp.where(row == 0, c0i, pltpu.roll(ei, 1, 0))
    else:
        nr = jnp.where(row == S5_SEG - 1, c0r, pltpu.roll(er, S5_SEG - 1, 0))
        ni = jnp.where(row == S5_SEG - 1, c0i, pltpu.roll(ei, S5_SEG - 1, 0))
    out_r, out_i = [], []
    for p in range(S5_STEPS):
        q = p if down else S5_STEPS - 1 - p
        pr, pi = _cmul(tr_scr[q * S5_SEG:(q + 1) * S5_SEG, :], sgn * ti_scr[q * S5_SEG:(q + 1) * S5_SEG, :], nr, ni)
        out_r.append(loc[p][0] + pr)
        out_i.append(loc[p][1] + pi)
    return jnp.concatenate(out_r, axis=0), jnp.concatenate(out_i, axis=0), (nr, ni), (er, ei)


def _gelu(y):
    c = math.sqrt(2.0 / math.pi)
    return 0.5 * y * (1.0 + jnp.tanh(c * (y + 0.044715 * y * y * y)))


def _gelu_grad(y):
    c = math.sqrt(2.0 / math.pi)
    th = jnp.tanh(c * (y + 0.044715 * y * y * y))
    return 0.5 * (1.0 + th) + 0.5 * y * (1.0 - th * th) * c * (1.0 + 3.0 * 0.044715 * y * y)


def _s5_fwd(proj, ab, bd_b, bd_c, dvec):
    rows = proj.shape[0]
    nc = rows // CHUNK
    ub = (2 * RET_QK + 2 * RET_W) // S5_TU
    ab_re, ab_im = ab
    bre, bim = bd_b
    cre, cim = bd_c

    def body(u_ref, ar_ref, ai_ref, bre_ref, bim_ref, cre_ref, cim_ref, d_ref,
             y_ref, g_ref, er_ref, ei_ref, tr_scr, ti_scr, wfr_scr, wfi_scr, wbr_scr, wbi_scr,
             cr_scr, ci_scr, er_scr, ei_scr):
        n = pl.program_id(1)
        ar, ai = ar_ref[0], ai_ref[0]

        @pl.when(n == 0)
        def _():
            _s5_tables(ar, ai, tr_scr, ti_scr, wfr_scr, wfi_scr, wbr_scr, wbi_scr)
            cr_scr[...] = jnp.zeros_like(cr_scr)
            ci_scr[...] = jnp.zeros_like(ci_scr)

        u = _seg_perm(u_ref[...])
        c0r, c0i = cr_scr[...], ci_scr[...]
        er_ref[0, 0] = c0r
        ei_ref[0, 0] = c0i
        xr, xi, _, (er, ei) = _seg_scan(_dot(u, bre_ref[0]), _dot(u, bim_ref[0]), ar, ai, tr_scr, ti_scr,
                                        wfr_scr, wfi_scr, c0r, c0i, True)
        er_scr[...] = er
        ei_scr[...] = ei
        cr_scr[...] = jnp.broadcast_to(er_scr[S5_SEG - 1:S5_SEG, :], cr_scr.shape)
        ci_scr[...] = jnp.broadcast_to(ei_scr[S5_SEG - 1:S5_SEG, :], ci_scr.shape)
        y = _seg_unperm(_dot(xr, cre_ref[0]) - _dot(xi, cim_ref[0]) + d_ref[...] * u)
        y_ref[...] = y
        g_ref[...] = _gelu(y).astype(BF16)

    vec = pl.BlockSpec((1, 1, S5_TS), lambda t, n: (t, 0, 0))
    return pl.pallas_call(
        body, name="s5_fwd", grid=(S5_NT, nc),
        in_specs=[pl.BlockSpec((CHUNK, S5_TU), lambda t, n: (n, ub + t)), vec, vec,
                  pl.BlockSpec((1, S5_TU, S5_TS), lambda t, n: (t, 0, 0)),
                  pl.BlockSpec((1, S5_TU, S5_TS), lambda t, n: (t, 0, 0)),
                  pl.BlockSpec((1, S5_TS, S5_TU), lambda t, n: (t, 0, 0)),
                  pl.BlockSpec((1, S5_TS, S5_TU), lambda t, n: (t, 0, 0)),
                  pl.BlockSpec((1, S5_TU), lambda t, n: (0, t))],
        out_specs=[pl.BlockSpec((CHUNK, S5_TU), lambda t, n: (n, t)),
                   pl.BlockSpec((CHUNK, S5_TU), lambda t, n: (n, t)),
                   pl.BlockSpec((1, 1, 8, S5_TS), lambda t, n: (t, n, 0, 0)),
                   pl.BlockSpec((1, 1, 8, S5_TS), lambda t, n: (t, n, 0, 0))],
        out_shape=[jax.ShapeDtypeStruct((rows, S5_W), F32), jax.ShapeDtypeStruct((rows, S5_W), BF16),
                   jax.ShapeDtypeStruct((S5_NT, nc, 8, S5_TS), F32),
                   jax.ShapeDtypeStruct((S5_NT, nc, 8, S5_TS), F32)],
        scratch_shapes=[pltpu.VMEM((CHUNK, S5_TS), F32) for _ in range(2)]
        + [pltpu.VMEM((S5_SEG, S5_TS), F32) for _ in range(8)],
        compiler_params=pltpu.CompilerParams(dimension_semantics=("parallel", "arbitrary")),
    )(proj, ab_re.reshape(S5_NT, 1, S5_TS), ab_im.reshape(S5_NT, 1, S5_TS), bre, bim, cre, cim, dvec)


def _s5_bwd(proj, dy, ab, bd_b, bd_c, dvec, entry):
    rows = proj.shape[0]
    nc = rows // CHUNK
    ub = (2 * RET_QK + 2 * RET_W) // S5_TU
    ab_re, ab_im = ab
    bre, bim = bd_b
    cre, cim = bd_c
    er, ei = entry

    def rn(n):
        return nc - 1 - n

    def body(u_ref, dy_ref, ar_ref, ai_ref, bre_ref, bim_ref, cre_ref, cim_ref, d_ref, er_ref, ei_ref,
             du_ref, dbr_ref, dbi_ref, dcr_ref, dci_ref, dar_ref, dai_ref, dd_ref,
             tr_scr, ti_scr, wfr_scr, wfi_scr, wbr_scr, wbi_scr, gr_scr, gi_scr, er_scr, ei_scr):
        n = pl.program_id(1)
        ar, ai = ar_ref[0], ai_ref[0]

        @pl.when(n == 0)
        def _():
            _s5_tables(ar, ai, tr_scr, ti_scr, wfr_scr, wfi_scr, wbr_scr, wbi_scr)
            gr_scr[...] = jnp.zeros_like(gr_scr)
            gi_scr[...] = jnp.zeros_like(gi_scr)
            for r in (dbr_ref, dbi_ref, dcr_ref, dci_ref, dar_ref, dai_ref, dd_ref):
                r[...] = jnp.zeros_like(r)

        u = _seg_perm(u_ref[...])
        dy = _seg_perm(dy_ref[...])
        xr, xi, (pr, pi), _ = _seg_scan(_dot(u, bre_ref[0]), _dot(u, bim_ref[0]), ar, ai, tr_scr, ti_scr,
                                        wfr_scr, wfi_scr, er_ref[0, 0], ei_ref[0, 0], True)
        dcr_ref[0] += _dot(xr, dy, TN)
        dci_ref[0] -= _dot(xi, dy, TN)
        gr, gi, _, (er, ei) = _seg_scan(_dot(dy, cre_ref[0], NT), -_dot(dy, cim_ref[0], NT), ar, ai, tr_scr, ti_scr,
                                        wbr_scr, wbi_scr, gr_scr[...], gi_scr[...], False)
        er_scr[...] = er
        ei_scr[...] = ei
        gr_scr[...] = jnp.broadcast_to(er_scr[0:1, :], gr_scr.shape)
        gi_scr[...] = jnp.broadcast_to(ei_scr[0:1, :], gi_scr.shape)
        xpr = jnp.concatenate([pr, xr[:CHUNK - S5_SEG]], axis=0)
        xpi = jnp.concatenate([pi, xi[:CHUNK - S5_SEG]], axis=0)
        dar_ref[0] += jnp.sum((xpr * gr + xpi * gi).reshape(S5_STEPS, S5_SEG, S5_TS), axis=0)
        dai_ref[0] += jnp.sum((xpr * gi - xpi * gr).reshape(S5_STEPS, S5_SEG, S5_TS), axis=0)
        dbr_ref[0] += _dot(u, gr, TN)
        dbi_ref[0] += _dot(u, gi, TN)
        dd_ref[0] += jnp.sum((dy * u).reshape(S5_STEPS, S5_SEG, S5_TU), axis=0)
        du = dy * d_ref[...] + _dot(gr, bre_ref[0], NT) + _dot(gi, bim_ref[0], NT)
        du_ref[...] = _seg_unperm(du).astype(BF16)

    vec = pl.BlockSpec((1, 1, S5_TS), lambda t, n: (t, 0, 0))
    acc_b = pl.BlockSpec((1, S5_TU, S5_TS), lambda t, n: (t, 0, 0))
    acc_c = pl.BlockSpec((1, S5_TS, S5_TU), lambda t, n: (t, 0, 0))
    acc_a = pl.BlockSpec((1, 8, S5_TS), lambda t, n: (t, 0, 0))
    ent = pl.BlockSpec((1, 1, 8, S5_TS), lambda t, n: (t, rn(n), 0, 0))
    return pl.pallas_call(
        body, name="s5_bwd", grid=(S5_NT, nc),
        in_specs=[pl.BlockSpec((CHUNK, S5_TU), lambda t, n: (rn(n), ub + t)),
                  pl.BlockSpec((CHUNK, S5_TU), lambda t, n: (rn(n), t)), vec, vec,
                  acc_b, acc_b, acc_c, acc_c, pl.BlockSpec((1, S5_TU), lambda t, n: (0, t)), ent, ent],
        out_specs=[pl.BlockSpec((CHUNK, S5_TU), lambda t, n: (rn(n), t)), acc_b, acc_b, acc_c, acc_c, acc_a, acc_a,
                   pl.BlockSpec((1, 8, S5_TU), lambda t, n: (t, 0, 0))],
        out_shape=[jax.ShapeDtypeStruct((rows, S5_W), BF16),
                   jax.ShapeDtypeStruct((S5_NT, S5_TU, S5_TS), F32), jax.ShapeDtypeStruct((S5_NT, S5_TU, S5_TS), F32),
                   jax.ShapeDtypeStruct((S5_NT, S5_TS, S5_TU), F32), jax.ShapeDtypeStruct((S5_NT, S5_TS, S5_TU), F32),
                   jax.ShapeDtypeStruct((S5_NT, 8, S5_TS), F32), jax.ShapeDtypeStruct((S5_NT, 8, S5_TS), F32),
                   jax.ShapeDtypeStruct((S5_NT, 8, S5_TU), F32)],
        scratch_shapes=[pltpu.VMEM((CHUNK, S5_TS), F32) for _ in range(2)]
        + [pltpu.VMEM((S5_SEG, S5_TS), F32) for _ in range(8)],
        compiler_params=pltpu.CompilerParams(dimension_semantics=("parallel", "arbitrary")),
    )(proj, dy,ab_re.reshape(S5_NT, 1, S5_TS), ab_im.reshape(S5_NT, 1, S5_TS), bre, bim, cre, cim, dvec, er, ei)


def _s5_gate_bwd(dmix, g, t, proj):
    rows = g.shape[0]
    tm = _row_tile(rows, 384)
    ob = RET_W // S5_W
    zb = (2 * RET_QK + 2 * RET_W + S5_W) // S5_W

    def body(do_ref, g_ref, t_ref, z_ref, dz_ref, dt_ref, dg_ref):
        do = do_ref[...]
        gv = g_ref[...].astype(F32)
        z = z_ref[...]
        st = _sigmoid(t_ref[...])
        sg = _sigmoid(z)
        os5 = gv * st
        dz_ref[...] = (do * os5 * sg * (1.0 + z * (1.0 - sg))).astype(BF16)
        dos = do * z * sg
        dt_ref[...] = (dos * gv * st * (1.0 - st)).astype(BF16)
        dg_ref[...] = dos * st

    blk = pl.BlockSpec((tm, S5_W), lambda i: (i, 0))
    return pl.pallas_call(
        body, name="s5_gate_bwd", grid=(rows // tm,),
        in_specs=[pl.BlockSpec((tm, S5_W), lambda i: (i, ob)), blk, blk,
                  pl.BlockSpec((tm, S5_W), lambda i: (i, zb))],
        out_specs=[blk, blk, blk],
        out_shape=[jax.ShapeDtypeStruct((rows, S5_W), BF16), jax.ShapeDtypeStruct((rows, S5_W), BF16),
                   jax.ShapeDtypeStruct((rows, S5_W), F32)],
    )(dmix, g, t, proj)


def _split3(x):
    hi = x.astype(BF16)
    r = x - hi.astype(F32)
    mid = r.astype(BF16)
    lo = (r - mid.astype(F32)).astype(BF16)
    return hi, mid, lo


def _tri_sum(x, upper):
    i = lax.broadcasted_iota(jnp.int32, (CHUNK, CHUNK), 0)
    j = lax.broadcasted_iota(jnp.int32, (CHUNK, CHUNK), 1)
    tri = jnp.where((j >= i) if upper else (j <= i), 1.0, 0.0).astype(BF16)
    hi, mid, lo = _split3(x)
    return _dot(tri, lo) + _dot(tri, mid) + _dot(tri, hi)


def _gla_log_decay(gl, wg, bg, n):
    logit = _dot(gl, wg) + bg
    la = (jnp.minimum(logit, 0.0) - jnp.log(1.0 + jnp.exp(-jnp.abs(logit)))) * (1.0 / GLA_TAU)
    row = lax.broadcasted_iota(jnp.int32, (CHUNK, 1), 0)
    live = jnp.logical_or(n > 0, row >= PAD)
    return logit, jnp.where(live, la, 0.0), live


def _gla_in_specs(rev, nc):
    def cn(n):
        return (nc - 1 - n) if rev else n
    kb = GLA_QK // GLA_DK
    vb = 2 * GLA_QK // GLA_DV
    zb = (2 * GLA_QK + GLA_W) // GLA_DV
    gb = (2 * GLA_QK + 2 * GLA_W) // 128
    return [
        pl.BlockSpec((CHUNK, GLA_DK), lambda h, n: (cn(n), h)),
        pl.BlockSpec((CHUNK, GLA_DK), lambda h, n: (cn(n), kb + h)),
        pl.BlockSpec((CHUNK, GLA_DV), lambda h, n: (cn(n), vb + h)),
        pl.BlockSpec((CHUNK, GLA_DV), lambda h, n: (cn(n), zb + h)),
        pl.BlockSpec((CHUNK, 128), lambda h, n: (cn(n), gb)),
        pl.BlockSpec((128, GLA_DK), lambda h, n: (0, h)),
        pl.BlockSpec((1, GLA_DK), lambda h, n: (0, h)),
        pl.BlockSpec((1, GLA_DV), lambda h, n: (0, h)),
    ]


def _gla_fwd(proj, wgate, bgate, normw):
    rows = proj.shape[0]
    nc = rows // CHUNK

    def body(q_ref, k_ref, v_ref, z_ref, gl_ref, wg_ref, bg_ref, w_ref, o_ref, oc_ref, st_ref, s_scr, o_scr, b_scr):
        n = pl.program_id(1)

        @pl.when(n == 0)
        def _():
            s_scr[...] = jnp.zeros_like(s_scr)

        q = q_ref[...] * (GLA_DK ** -0.5)
        k = k_ref[...]
        v = v_ref[...]
        vb = v.astype(BF16)
        _, la, _ = _gla_log_decay(gl_ref[...], wg_ref[...], bg_ref[...], n)
        b = _tri_sum(la, False)
        b_scr[...] = b
        b_last = b_scr[CHUNK - 1:CHUNK, :]
        st = s_scr[...]
        st_ref[0, 0] = st
        s_scr[...] = st * jnp.exp(b_last) + _dot(v, k * jnp.exp(b_last - b), TN)
        rowc = lax.broadcasted_iota(jnp.int32, (CHUNK, 1), 0)
        rows16 = lax.broadcasted_iota(jnp.int32, (SUB, 1), 0)
        a_tot = jnp.zeros((CHUNK, CHUNK), F32)
        for s in range(1, NSUB):
            lo = s * SUB
            bref = b_scr[lo - 1:lo, :]
            in_s = jnp.logical_and(rowc >= lo, rowc < lo + SUB)
            qh = q * jnp.exp(jnp.where(in_s, b - bref, -1e30))
            kh = k * jnp.exp(jnp.where(rowc < lo, bref - b, -1e30))
            a_tot = a_tot + _dot(qh, kh, NT)
        o_scr[...] = _dot(q * jnp.exp(b), st, NT) + _dot(a_tot, vb)
        for s in range(NSUB):
            lo = s * SUB
            qs, bs = q[lo:lo + SUB], b[lo:lo + SUB]
            acc = jnp.zeros((SUB, GLA_DV), F32)
            for j in range(SUB):
                r = lo + j
                e = jnp.exp(jnp.where(rows16 >= j, bs - b_scr[r:r + 1, :], -1e30))
                col = jnp.sum(qs * k_ref[r:r + 1, :] * e, axis=1, keepdims=True)
                acc = acc + col * v_ref[r:r + 1, :]
            o_scr[lo:lo + SUB, :] += acc
        o = o_scr[...]
        o_ref[...] = o
        oc_ref[...] = _gate_fwd(o, z_ref[...], w_ref[...]).astype(BF16)

    return pl.pallas_call(
        body, name="gla_fwd", grid=(GLA_HEADS, nc),
        in_specs=_gla_in_specs(False, nc),
        out_specs=[pl.BlockSpec((CHUNK, GLA_DV), lambda h, n: (n, h)),
                   pl.BlockSpec((CHUNK, GLA_DV), lambda h, n: (n, h)),
                   pl.BlockSpec((1, 1, GLA_DV, GLA_DK), lambda h, n: (h, n, 0, 0))],
        out_shape=[jax.ShapeDtypeStruct((rows, GLA_W), F32), jax.ShapeDtypeStruct((rows, GLA_W), BF16),
                   jax.ShapeDtypeStruct((GLA_HEADS, nc, GLA_DV, GLA_DK), F32)],
        scratch_shapes=[pltpu.VMEM((GLA_DV, GLA_DK), F32), pltpu.VMEM((CHUNK, GLA_DV), F32),
                        pltpu.VMEM((CHUNK, GLA_DK), F32)],
        compiler_params=pltpu.CompilerParams(dimension_semantics=("parallel", "arbitrary")),
    )(proj, proj, proj, proj, proj, wgate, bgate, normw)


def _gla_bwd(proj, wgate, bgate, normw, o_gla, d_oc, states):
    rows = proj.shape[0]
    nc = rows // CHUNK

    def rn(n):
        return nc - 1 - n

    def body(q_ref, k_ref, v_ref, z_ref, gl_ref, wg_ref, bg_ref, w_ref, o_ref, do_ref, st_ref,
             dq_ref, dk_ref, dv_ref, dz_ref, dl_ref, dw_ref, dbg_ref,
             ds_scr, dq_scr, dk_scr, dv_scr, db_scr, b_scr):
        n = pl.program_id(1)
        cn = rn(n)

        @pl.when(n == 0)
        def _():
            ds_scr[...] = jnp.zeros_like(ds_scr)
            dw_ref[...] = jnp.zeros_like(dw_ref)
            dbg_ref[...] = jnp.zeros_like(dbg_ref)

        q = q_ref[...] * (GLA_DK ** -0.5)
        k = k_ref[...]
        v = v_ref[...]
        vb = v.astype(BF16)
        do, dz, dw = _gate_bwd(do_ref[...], o_ref[...], z_ref[...], w_ref[...])
        dz_ref[...] = dz.astype(BF16)
        dw_ref[0] += dw
        logit, la, live = _gla_log_decay(gl_ref[...], wg_ref[...], bg_ref[...], cn)
        b = _tri_sum(la, False)
        b_scr[...] = b
        b_last = b_scr[CHUNK - 1:CHUNK, :]
        e_last = jnp.exp(b_last)
        st = st_ref[0, 0]
        g1 = ds_scr[...]
        eb = jnp.exp(b)
        qe = q * eb
        dqe = _dot(do, st)
        dq_scr[...] = dqe * eb
        db_scr[...] = dqe * qe
        ekb = jnp.exp(b_last - b)
        kdec = k * ekb
        dkdec = _dot(v, g1)
        dv_scr[...] = _dot(kdec, g1, NT)
        dk_scr[...] = dkdec * ekb
        wk = dkdec * kdec
        db_scr[...] -= wk
        dbl = jnp.sum(wk, axis=0, keepdims=True) + jnp.sum(g1 * st, axis=0, keepdims=True) * e_last
        ds_scr[...] = g1 * e_last + _dot(do, qe, TN)
        rowc = lax.broadcasted_iota(jnp.int32, (CHUNK, 1), 0)
        rows16 = lax.broadcasted_iota(jnp.int32, (SUB, 1), 0)
        da_full = _dot(do, vb, NT)
        a_tot = jnp.zeros((CHUNK, CHUNK), F32)
        for s in range(1, NSUB):
            lo = s * SUB
            bref = b_scr[lo - 1:lo, :]
            in_s = jnp.logical_and(rowc >= lo, rowc < lo + SUB)
            eq = jnp.exp(jnp.where(in_s, b - bref, -1e30))
            ek = jnp.exp(jnp.where(rowc < lo, bref - b, -1e30))
            qh = q * eq
            kh = k * ek
            a_tot = a_tot + _dot(qh, kh, NT)
            da = jnp.where(in_s, da_full, 0.0)
            dqh = _dot(da, kh)
            dkh = _dot(da, qh, TN)
            tq = dqh * qh
            tk = dkh * kh
            dq_scr[...] += dqh * eq
            dk_scr[...] += dkh * ek
            db_scr[...] += tq - tk
            db_scr[lo - 1:lo, :] += jnp.sum(tk, axis=0, keepdims=True) - jnp.sum(tq, axis=0, keepdims=True)
        dv_scr[...] += _dot(a_tot, do, TN)
        for s in range(NSUB):
            lo = s * SUB
            qs, bs = q[lo:lo + SUB], b[lo:lo + SUB]
            dos = do[lo:lo + SUB]
            dqs = jnp.zeros((SUB, GLA_DK), F32)
            dks = jnp.zeros((SUB, GLA_DK), F32)
            dbs = jnp.zeros((SUB, GLA_DK), F32)
            dvs = jnp.zeros((SUB, GLA_DV), F32)
            for j in range(SUB):
                pick = rows16 == j
                r = lo + j
                kj, vj, bj = k_ref[r:r + 1, :], v_ref[r:r + 1, :], b_scr[r:r + 1, :]
                e = jnp.exp(jnp.where(rows16 >= j, bs - bj, -1e30))
                qe_j = qs * e
                col = jnp.sum(qe_j * kj, axis=1, keepdims=True)
                dcol = jnp.sum(dos * vj, axis=1, keepdims=True)
                dvs = dvs + jnp.where(pick, jnp.sum(col * dos, axis=0, keepdims=True), 0.0)
                m = dcol * e
                dqs = dqs + m * kj
                mq = m * qs
                dks = dks + jnp.where(pick, jnp.sum(mq, axis=0, keepdims=True), 0.0)
                t = mq * kj
                dbs = dbs + t - jnp.where(pick, jnp.sum(t, axis=0, keepdims=True), 0.0)
            dq_scr[lo:lo + SUB, :] += dqs
            dk_scr[lo:lo + SUB, :] += dks
            dv_scr[lo:lo + SUB, :] += dvs
            db_scr[lo:lo + SUB, :] += dbs
        db_scr[CHUNK - 1:CHUNK, :] += dbl
        dla = _tri_sum(db_scr[...], True)
        dlogit = jnp.where(live, dla * (1.0 / GLA_TAU) * _sigmoid(-logit), 0.0)
        dl_ref[...] = dlogit
        dbg_ref[0] += jnp.sum(dlogit, axis=0, keepdims=True)
        dq_ref[...] = (dq_scr[...] * (GLA_DK ** -0.5)).astype(BF16)
        dk_ref[...] = dk_scr[...].astype(BF16)
        dv_ref[...] = dv_scr[...].astype(BF16)

    in_specs = _gla_in_specs(True, nc) + [
        pl.BlockSpec((CHUNK, GLA_DV), lambda h, n: (rn(n), h)),
        pl.BlockSpec((CHUNK, GLA_DV), lambda h, n: (rn(n), h)),
        pl.BlockSpec((1, 1, GLA_DV, GLA_DK), lambda h, n: (h, rn(n), 0, 0)),
    ]
    return pl.pallas_call(
        body, name="gla_bwd", grid=(GLA_HEADS, nc),
        in_specs=in_specs,
        out_specs=[pl.BlockSpec((CHUNK, GLA_DK), lambda h, n: (rn(n), h)),
                   pl.BlockSpec((CHUNK, GLA_DK), lambda h, n: (rn(n), h)),
                   pl.BlockSpec((CHUNK, GLA_DV), lambda h, n: (rn(n), h)),
                   pl.BlockSpec((CHUNK, GLA_DV), lambda h, n: (rn(n), h)),
                   pl.BlockSpec((CHUNK, GLA_DK), lambda h, n: (rn(n), h)),
                   pl.BlockSpec((1, 1, GLA_DV), lambda h, n: (h, 0, 0)),
                   pl.BlockSpec((1, 1, GLA_DK), lambda h, n: (h, 0, 0))],
        out_shape=[jax.ShapeDtypeStruct((rows, GLA_QK), BF16), jax.ShapeDtypeStruct((rows, GLA_QK), BF16),
                   jax.ShapeDtypeStruct((rows, GLA_W), BF16), jax.ShapeDtypeStruct((rows, GLA_W), BF16),
                   jax.ShapeDtypeStruct((rows, GLA_QK), F32),
                   jax.ShapeDtypeStruct((GLA_HEADS, 1, GLA_DV), F32),
                   jax.ShapeDtypeStruct((GLA_HEADS, 1, GLA_DK), F32)],
        scratch_shapes=[pltpu.VMEM((GLA_DV, GLA_DK), F32), pltpu.VMEM((CHUNK, GLA_DK), F32),
                        pltpu.VMEM((CHUNK, GLA_DK), F32), pltpu.VMEM((CHUNK, GLA_DV), F32),
                        pltpu.VMEM((CHUNK, GLA_DK), F32), pltpu.VMEM((CHUNK, GLA_DK), F32)],
        compiler_params=pltpu.CompilerParams(dimension_semantics=("parallel", "arbitrary")),
    )(proj, proj, proj, proj, proj, wgate, bgate, normw, o_gla, d_oc, states)


def _adamw(name, w, g, m, v):
    rows, cols = w.shape
    tm = rows
    for cand in (256, 128, 64, 32, 16, 8):
        if rows % cand == 0:
            tm = cand
            break
    c1 = 1.0 - ADAM_B1 ** ADAM_STEP
    c2 = 1.0 - ADAM_B2 ** ADAM_STEP

    def body(w_ref, g_ref, m_ref, v_ref, d_ref, nm_ref, nv_ref):
        gv = g_ref[...]
        nm = ADAM_B1 * m_ref[...] + (1.0 - ADAM_B1) * gv
        nv = ADAM_B2 * v_ref[...] + (1.0 - ADAM_B2) * (gv * gv)
        nm_ref[...] = nm
        nv_ref[...] = nv
        d_ref[...] = -ADAM_LR * ((nm / c1) / (jnp.sqrt(nv / c2) + ADAM_EPS) + ADAM_WD * w_ref[...])

    blk = pl.BlockSpec((tm, cols), lambda i: (i, 0))
    return pl.pallas_call(
        body, name=name, grid=(rows // tm,),
        in_specs=[blk] * 4, out_specs=[blk] * 3,
        out_shape=[jax.ShapeDtypeStruct((rows, cols), F32)] * 3,
    )(w, g, m, v)


def _place():
    x, y, c = lax.axis_index("x"), lax.axis_index("y"), lax.axis_index("c")
    chips = [(1 - x, y), (x, 1 - y), (1 - x, 1 - y)]
    return x, y, c, chips


ANY = pl.BlockSpec(memory_space=pl.ANY)


def _gather_weights(shards, kinds):
    n_arr = len(shards)

    def out_struct(a, kind):
        r, cc = a.shape
        if kind == "row":
            return jax.ShapeDtypeStruct((N_SHARD * r, cc), a.dtype)
        if kind == "col":
            return jax.ShapeDtypeStruct((r, N_SHARD * cc), a.dtype)
        return jax.ShapeDtypeStruct((N_SHARD, r, cc), a.dtype)

    def body(*refs):
        ins = refs[:n_arr]
        outs = refs[n_arr:2 * n_arr]
        send_sems, recv_sems, local_sems = refs[2 * n_arr:]
        x, y, c, chips = _place()
        mine = 2 * x + y
        sibling = (x, y, 1 - c)

        def window(i, shard, half):
            r, cc = shards[i].shape
            hr = r // 2
            if kinds[i] == "row":
                return outs[i].at[pl.ds(_mo(shard * r + half * hr, 8), hr), :]
            if kinds[i] == "col":
                return outs[i].at[pl.ds(_mo(half * hr, 8), hr), pl.ds(_mo(shard * cc, 128), cc)]
            return outs[i].at[shard, pl.ds(_mo(half * hr, 8), hr), :]

        def src_half(i, half):
            hr = shards[i].shape[0] // 2
            return ins[i].at[pl.ds(_mo(half * hr, 8), hr), :]

        def copy(i, slot, src, dst, to):
            return pltpu.make_async_remote_copy(
                src_ref=src, dst_ref=dst, send_sem=send_sems.at[i, slot], recv_sem=recv_sems.at[i, slot],
                device_id=to, device_id_type=MESH)

        local = []
        for i in range(n_arr):
            for half in range(2):
                cp = pltpu.make_async_copy(src_half(i, half), window(i, mine, half), local_sems.at[i, half])
                cp.start()
                local.append(cp)
        first = []
        for i in range(n_arr):
            for j, chip in enumerate(chips):
                cp = copy(i, j, src_half(i, c), window(i, mine, c), (*chip, c))
                cp.start()
                first.append(cp)
        passed = []
        for i in range(n_arr):
            for j, chip in enumerate(chips):
                theirs = 2 * chip[0] + chip[1]
                copy(i, j, src_half(i, c), window(i, theirs, c), (*chip, c)).wait_recv()
                cp = copy(i, 3 + j, window(i, theirs, c), window(i, theirs, c), sibling)
                cp.start()
                passed.append(cp)
        for i in range(n_arr):
            for j, chip in enumerate(chips):
                theirs = 2 * chip[0] + chip[1]
                copy(i, 3 + j, window(i, theirs, 1 - c), window(i, theirs, 1 - c), sibling).wait_recv()
        for cp in first + passed:
            cp.wait_send()
        for cp in local:
            cp.wait()

    return pl.pallas_call(
        body, name="gather_weights",
        in_specs=[ANY] * n_arr, out_specs=[ANY] * n_arr,
        out_shape=[out_struct(a, kd) for a, kd in zip(shards, kinds)],
        scratch_shapes=[pltpu.SemaphoreType.DMA((n_arr, 6)), pltpu.SemaphoreType.DMA((n_arr, 6)),
                        pltpu.SemaphoreType.DMA((n_arr, 2))],
        compiler_params=pltpu.CompilerParams(has_side_effects=True),
    )(*shards)


def _allreduce_small(buf):
    rows, cols = buf.shape

    def body(in_ref, out_ref, sib_ref, pair_ref, far_ref, send_sems, recv_sems):
        x, y, c, chips = _place()
        sibling = (x, y, 1 - c)
        to_sib = pltpu.make_async_remote_copy(
            src_ref=in_ref, dst_ref=sib_ref, send_sem=send_sems.at[0], recv_sem=recv_sems.at[0],
            device_id=sibling, device_id_type=MESH)
        to_sib.start()
        to_sib.wait()
        pair_ref[...] = in_ref[...] + sib_ref[...]
        far = [pltpu.make_async_remote_copy(
            src_ref=pair_ref, dst_ref=far_ref.at[j], send_sem=send_sems.at[1 + j], recv_sem=recv_sems.at[1 + j],
            device_id=(*chip, c), device_id_type=MESH) for j, chip in enumerate(chips)]
        for cp in far:
            cp.start()
        for cp in far:
            cp.wait()
        out_ref[...] = (pair_ref[...] + far_ref[1]) + (far_ref[0] + far_ref[2])

    vm = pl.BlockSpec(memory_space=pltpu.VMEM)
    return pl.pallas_call(
        body, name="allreduce_small",
        in_specs=[vm], out_specs=vm,
        out_shape=jax.ShapeDtypeStruct((rows, cols), F32),
        scratch_shapes=[pltpu.VMEM((rows, cols), F32), pltpu.VMEM((rows, cols), F32),
                        pltpu.VMEM((3, rows, cols), F32),
                        pltpu.SemaphoreType.DMA((4,)), pltpu.SemaphoreType.DMA((4,))],
        compiler_params=pltpu.CompilerParams(has_side_effects=True),
    )(buf)


def _shard_window(ref, kind, shard_shape, shard, half):
    r, cc = shard_shape
    hr = r // 2
    if kind == "row":
        return ref.at[pl.ds(_mo(shard * r + half * hr, 8), hr), :]
    if kind == "col":
        return ref.at[pl.ds(_mo(half * hr, 8), hr), pl.ds(_mo(shard * cc, 128), cc)]
    return ref.at[shard, pl.ds(_mo(half * hr, 8), hr), :]


HBM = pl.BlockSpec(memory_space=pltpu.HBM)
SEM = pl.BlockSpec(memory_space=pltpu.SEMAPHORE)
DATAFLOW = pltpu.SideEffectType.DATAFLOW_SIDE_EFFECTING


def _in_hbm(a):
    return pltpu.with_memory_space_constraint(a, pltpu.HBM)


def _empty_hbm(shape, dtype):
    return _in_hbm(lax.empty(shape, dtype))


def _copies_start(name, bufs, n_copies, plan, carry):
    nb = len(bufs)

    def body(*refs):
        send_sems, recv_sems = refs[nb + 1], refs[nb + 2]
        for k, (src, dst, to) in enumerate(plan(refs[:nb])):
            pltpu.make_async_remote_copy(src_ref=src, dst_ref=dst, send_sem=send_sems.at[k], recv_sem=recv_sems.at[k],
                                         device_id=to, device_id_type=MESH).start()

    passed = list(bufs) + [carry]
    out = pl.pallas_call(
        body, name=name,
        in_specs=[HBM] * (nb + 1), out_specs=[SEM, SEM] + [HBM] * (nb + 1),
        out_shape=[pltpu.SemaphoreType.DMA((n_copies,)), pltpu.SemaphoreType.DMA((n_copies,))]
        + [pltpu.HBM(a.shape, a.dtype) for a in passed],
        input_output_aliases={i: 2 + i for i in range(nb + 1)},
        compiler_params=pltpu.CompilerParams(has_side_effects=DATAFLOW),
    )(*[_in_hbm(a) for a in passed])
    return out[0], out[1], list(out[2:2 + nb]), out[2 + nb]


def _copies_wait(name, send_sems, recv_sems, bufs, plan, after):
    nb = len(bufs)
    after = list(after) if isinstance(after, (list, tuple)) else [after]

    def body(*refs):
        send, recv = refs[nb], refs[nb + 1]
        for k, (src, dst, to) in enumerate(plan(refs[:nb])):
            cp = pltpu.make_async_remote_copy(src_ref=src, dst_ref=dst, send_sem=send.at[k], recv_sem=recv.at[k],
                                              device_id=to, device_id_type=MESH)
            cp.wait_send()
            cp.wait_recv()

    out = pl.pallas_call(
        body, name=name,
        in_specs=[HBM] * nb + [SEM, SEM] + [ANY] * len(after), out_specs=[HBM] * nb,
        out_shape=[pltpu.HBM(a.shape, a.dtype) for a in bufs],
        input_output_aliases={i: i for i in range(nb)},
        compiler_params=pltpu.CompilerParams(has_side_effects=DATAFLOW),
    )(*bufs, send_sems, recv_sems, *after)
    return list(out)


def _gathered_struct(a, kind):
    r, cc = a.shape
    if kind == "row":
        return jax.ShapeDtypeStruct((N_SHARD * r, cc), a.dtype)
    if kind == "col":
        return jax.ShapeDtypeStruct((r, N_SHARD * cc), a.dtype)
    return jax.ShapeDtypeStruct((N_SHARD, r, cc), a.dtype)


def _place_own(shards, kinds):
    n_arr = len(shards)

    def body(*refs):
        ins, outs, sems = refs[:n_arr], refs[n_arr:2 * n_arr], refs[2 * n_arr]
        mine = 2 * lax.axis_index("x") + lax.axis_index("y")
        cps = []
        for i in range(n_arr):
            hr = shards[i].shape[0] // 2
            for half in range(2):
                cp = pltpu.make_async_copy(ins[i].at[pl.ds(half * hr, hr), :],
                                           _shard_window(outs[i], kinds[i], shards[i].shape, mine, half), sems.at[i, half])
                cp.start()
                cps.append(cp)
        for cp in cps:
            cp.wait()

    return pl.pallas_call(
        body, name="place_own",
        in_specs=[ANY] * n_arr, out_specs=[ANY] * n_arr,
        out_shape=[_gathered_struct(a, kd) for a, kd in zip(shards, kinds)],
        scratch_shapes=[pltpu.SemaphoreType.DMA((n_arr, 2))],
    )(*shards)


def _gather_ici_plan(shard_shapes, kinds):
    n_arr = len(kinds)

    def plan(refs):
        x, y, c, chips = _place()
        out = []
        for i in range(n_arr):
            hr = shard_shapes[i][0] // 2
            src = refs[i].at[pl.ds(_mo(c * hr, 8), hr), :]
            dst = _shard_window(refs[n_arr + i], kinds[i], shard_shapes[i], 2 * x + y, c)
            out += [(src, dst, (*chip, c)) for chip in chips]
        return out

    return plan


def _gather_d2d_plan(shard_shapes, kinds):
    n_arr = len(kinds)

    def plan(refs):
        x, y, c, chips = _place()
        out = []
        for i in range(n_arr):
            for chip in chips:
                w = _shard_window(refs[i], kinds[i], shard_shapes[i], 2 * chip[0] + chip[1], c)
                out.append((w, w, (x, y, 1 - c)))
        return out

    return plan


def _rs_pair_plan(kinds, shard_shapes):
    n_arr = len(kinds)

    def plan(refs):
        x, y, c, _ = _place()
        out = []
        for i in range(n_arr):
            for s in range(N_SHARD):
                out.append((_shard_window(refs[i], kinds[i], shard_shapes[i], s, 1 - c), refs[n_arr + i].at[s],
                            (x, y, 1 - c)))
        return out

    return plan


def _rs_chip_plan(n_arr):
    def plan(refs):
        x, y, c, chips = _place()
        out = []
        for i in range(n_arr):
            for j, chip in enumerate(chips):
                out.append((refs[i].at[2 * chip[0] + chip[1]], refs[n_arr + i].at[j], (*chip, c)))
        return out

    return plan


def _rs_pair_add(name, grad, got, kind, shard_shape, c):
    r, cc = shard_shape
    hr = r // 2
    tr = hr
    for cand in (256, 128, 64, 32, 16):
        if hr % cand == 0:
            tr = cand
            break
    nb = hr // tr

    if kind == "row":
        g_spec = pl.BlockSpec((tr, cc), lambda s, i, cr: (s * 2 * nb + cr[0] * nb + i, 0))
    elif kind == "col":
        g_spec = pl.BlockSpec((tr, cc), lambda s, i, cr: (cr[0] * nb + i, s))
    else:
        g_spec = pl.BlockSpec((None, tr, cc), lambda s, i, cr: (s, cr[0] * nb + i, 0))
    t_spec = pl.BlockSpec((None, tr, cc), lambda s, i, cr: (s, i, 0))

    def body(c_ref, g_ref, t_ref, p_ref, pb_ref):
        p = g_ref[...] + t_ref[...]
        p_ref[...] = p
        pb_ref[...] = p.astype(BF16)

    return pl.pallas_call(
        body, name=name,
        grid_spec=pltpu.PrefetchScalarGridSpec(
            num_scalar_prefetch=1, grid=(N_SHARD, nb),
            in_specs=[g_spec, t_spec], out_specs=[t_spec, t_spec]),
        out_shape=[jax.ShapeDtypeStruct((N_SHARD, hr, cc), F32), jax.ShapeDtypeStruct((N_SHARD, hr, cc), BF16)],
    )(c, grad, got)


def _rs_chip_add(name, pair_f32, got, shard_shape, mine_c):
    r, cc = shard_shape
    hr = r // 2
    tr = hr
    for cand in (256, 128, 64, 32, 16):
        if hr % cand == 0:
            tr = cand
            break
    nb = hr // tr

    def body(mc_ref, p_ref, t0_ref, t1_ref, t2_ref, o_ref):
        o_ref[...] = (p_ref[...] + t1_ref[...].astype(F32)) + (t0_ref[...].astype(F32) + t2_ref[...].astype(F32))

    def far(j):
        return pl.BlockSpec((None, tr, cc), lambda i, mc: (j, i, 0))

    return pl.pallas_call(
        body, name=name,
        grid_spec=pltpu.PrefetchScalarGridSpec(
            num_scalar_prefetch=1, grid=(nb,),
            in_specs=[pl.BlockSpec((None, tr, cc), lambda i, mc: (mc[0], i, 0)), far(0), far(1), far(2)],
            out_specs=pl.BlockSpec((tr, cc), lambda i, mc: (mc[1] * nb + i, 0))),
        out_shape=jax.ShapeDtypeStruct((r, cc), F32),
    )(mine_c, pair_f32, got, got, got)


def _rs_pair_share(name, halves, shard_shapes):
    n_arr = len(halves)

    def body(*refs):
        ins = refs[:n_arr]
        outs = refs[n_arr:2 * n_arr]
        send_sems, recv_sems = refs[2 * n_arr:]
        x, y, c, _ = _place()
        sibling = (x, y, 1 - c)
        cps = []
        for i in range(n_arr):
            hr = shard_shapes[i][0] // 2
            rows = pl.ds(_mo(c * hr, 8), hr)
            cp = pltpu.make_async_remote_copy(
                src_ref=outs[i].at[rows, :], dst_ref=outs[i].at[rows, :],
                send_sem=send_sems.at[i], recv_sem=recv_sems.at[i],
                device_id=sibling, device_id_type=MESH)
            cp.start()
            cps.append(cp)
        for cp in cps:
            cp.wait()

    return pl.pallas_call(
        body, name=name,
        in_specs=[ANY] * n_arr, out_specs=[ANY] * n_arr,
        out_shape=[jax.ShapeDtypeStruct(s, F32) for s in shard_shapes],
        input_output_aliases={i: i for i in range(n_arr)},
        scratch_shapes=[pltpu.SemaphoreType.DMA((n_arr,)), pltpu.SemaphoreType.DMA((n_arr,))],
        compiler_params=pltpu.CompilerParams(has_side_effects=True),
    )(*halves)


def _pack(arrays):
    flat = []
    for a in arrays:
        v = a.reshape(-1).astype(F32)
        flat.append(jnp.pad(v, (0, (-v.shape[0]) % SMALL_COLS)))
    buf = jnp.concatenate(flat).reshape(-1, SMALL_COLS)
    return jnp.pad(buf, ((0, (-buf.shape[0]) % 8), (0, 0)))


def _unpack(buf, shapes):
    out = []
    row = 0
    for s in shapes:
        size = math.prod(s)
        nrow = -(-size // SMALL_COLS)
        out.append(buf[row:row + nrow].reshape(-1)[:size].reshape(s))
        row += nrow
    return out


def kernel(x, meta, norm_ab_w, w_in_ab, ret_norm_w, s5_lam_re, s5_lam_im, s5_log_dt, s5_b_re, s5_b_im, s5_c_re, s5_c_im, s5_d, s5_w_glu, w_out_ab, norm_c_w, w_in_c, gla_w_gate, gla_b_gate, gla_norm_w, w_out_c, final_norm_w, loss_target, m_meta, m_norm_ab_w, m_w_in_ab, m_ret_norm_w, m_s5_lam_re, m_s5_lam_im, m_s5_log_dt, m_s5_b_re, m_s5_b_im, m_s5_c_re, m_s5_c_im, m_s5_d, m_s5_w_glu, m_w_out_ab, m_norm_c_w, m_w_in_c, m_gla_w_gate, m_gla_b_gate, m_gla_norm_w, m_w_out_c, m_final_norm_w, v_meta, v_norm_ab_w, v_w_in_ab, v_ret_norm_w, v_s5_lam_re, v_s5_lam_im, v_s5_log_dt, v_s5_b_re, v_s5_b_im, v_s5_c_re, v_s5_c_im, v_s5_d, v_s5_w_glu, v_w_out_ab, v_norm_c_w, v_w_in_c, v_gla_w_gate, v_gla_b_gate, v_gla_norm_w, v_w_out_c, v_final_norm_w):
    seq = x.shape[1]
    rows = seq + CHUNK
    xi, yi, ci = lax.axis_index("x"), lax.axis_index("y"), lax.axis_index("c")
    mine = 2 * xi + yi
    c_arr = jnp.reshape(ci, (1,)).astype(jnp.int32)
    mine_c = jnp.stack([mine, ci]).astype(jnp.int32)

    small_shard = _pack([meta, norm_c_w, gla_norm_w, gla_b_gate, gla_w_gate[0]])
    srows = small_shard.shape[0]
    wab, small_all = _gather_weights([w_in_ab[0].astype(BF16), small_shard], ["col", "stack"])
    late = [w_out_ab[0].astype(BF16), w_in_c[0].astype(BF16), w_out_c[0].astype(BF16), s5_w_glu[0].astype(BF16)]
    late_kinds = ["row", "stack", "row", "row"]
    late_shapes = [a.shape for a in late]
    ici_plan = _gather_ici_plan(late_shapes, late_kinds)
    d2d_plan = _gather_d2d_plan(late_shapes, late_kinds)
    n_late = 3 * len(late)
    g_send, g_recv, g_bufs, wab = _copies_start("gather_late_ici_start", late + list(_place_own(late, late_kinds)),
                                                n_late, ici_plan, wab)
    q4 = D_MODEL // N_SHARD
    g4 = GLA_QK // N_SHARD
    parts = [_unpack(small_all[j], [(N_META, q4), (1, q4), (1, q4), (1, g4), (GLA_RANK, g4)]) for j in range(N_SHARD)]
    meta_f, norm_c_f, gla_norm_f, bgate_f, wgate_f = [jnp.concatenate([p[i] for p in parts], axis=1) for i in range(5)]
    wgate_pad = jnp.pad(wgate_f, ((0, 128 - GLA_RANK), (0, 0)))

    h0 = jnp.concatenate([jnp.zeros((PAD, D_MODEL), F32), meta_f, x[0]], axis=0)
    cosf, sinf = _rope_tables(rows)
    rtab = _ret_tables()
    ab_re, ab_im, bb_re, bb_im = _s5_discretize(s5_lam_re[0], s5_lam_im[0], s5_log_dt[0], s5_b_re[0], s5_b_im[0])
    ab = (ab_re, ab_im)
    bd_b = (_bdiag_in(bb_re), _bdiag_in(bb_im))
    bd_c = (_bdiag_out(s5_c_re[0]), _bdiag_out(s5_c_im[0]))

    tm = _row_tile(rows, 1408)
    tmk = _row_tile(rows, 1408)
    hn0 = _rms_fwd("norm_ab", h0, norm_ab_w)
    proj0 = _matmul("in_proj_ab", hn0, wab, NN, rows, IN_AB, D_MODEL, tm=tm, tn=512, tk=D_MODEL)
    o_ret, o_a, ret_states = _ret_fwd(proj0, cosf, sinf, rtab, ret_norm_w)
    g_bufs = _copies_wait("gather_late_ici_wait", g_send, g_recv, g_bufs, ici_plan, o_a)
    g_send, g_recv, g_bufs, proj0 = _copies_start("gather_late_d2d_start", g_bufs[len(late):], n_late, d2d_plan, proj0)
    y_s5, g_s5, s5_er, s5_ei = _s5_fwd(proj0, ab, bd_b, bd_c, s5_d)
    wout_ab, wc_st, wout_c, wglu = _copies_wait("gather_late_d2d_wait", g_send, g_recv, g_bufs, d2d_plan, g_s5)
    wc = jnp.concatenate([wc_st[j] for j in range(N_SHARD)] + [jnp.zeros((D_MODEL, IN_C_PAD - IN_C), BF16)], axis=1)
    zb_blk = (2 * RET_QK + 2 * RET_W + S5_W) // 512

    def glu_out(acc, gv, z):
        return gv.astype(F32) * _sigmoid(acc) * (z * _sigmoid(z))

    t_glu = _matmul("glu", g_s5, wglu, NN, rows, S5_W, S5_W, tm=tm, tn=512, tk=S5_W)
    o_b = _matmul("glu_out", g_s5, wglu, NN, rows, S5_W, S5_W, tm=tm, tn=512, tk=S5_W, out_dtype=BF16,
                  extras=[(g_s5, (tm, 512), lambda i, j, kk: (i, j)),
                          (proj0, (tm, 512), lambda i, j, kk: (i, zb_blk + j))],
                  epilogue=glu_out)
    mix = jnp.concatenate([o_a, o_b], axis=1)
    h1 = _matmul("out_proj_ab", mix, wout_ab, NN, rows, D_MODEL, OUT_AB, tm=tm, tn=512, tk=1024,
                 extras=[(h0, (tm, 512), lambda i, j, kk: (i, j))], epilogue=lambda acc, r: acc + r)

    hn1 = _rms_fwd("norm_c", h1, norm_c_f)
    proj1 = _matmul("in_proj_c", hn1, wc, NN, rows, IN_C_PAD, D_MODEL, tm=tm, tn=896, tk=D_MODEL)
    o_gla, o_c, gla_states = _gla_fwd(proj1, wgate_pad, bgate_f, gla_norm_f)
    h2 = _matmul("out_proj_c", o_c, wout_c, NN, rows, D_MODEL, GLA_W, tm=tm, tn=512, tk=GLA_W,
                 extras=[(h1, (tm, 512), lambda i, j, kk: (i, j))], epilogue=lambda acc, r: acc + r)
    loss_dev, dh2, d_final = _final_loss(h2, final_norm_w.reshape(1, D_MODEL), loss_target[0])

    g_wout_c = _matmul("d_w_out_c", o_c, dh2, TN, GLA_W, D_MODEL, rows, tm=1024, tn=1024, tk=tmk)
    d_oc = _matmul("d_o_c", dh2, wout_c, NT, rows, GLA_W, D_MODEL, tm=tm, tn=512, tk=1024)
    dq1, dk1, dv1, dz1, dlogit, d_gla_norm, d_bgate = _gla_bwd(proj1, wgate_pad, bgate_f, gla_norm_f, o_gla, d_oc, gla_states)
    gl_blk = (2 * GLA_QK + 2 * GLA_W) // 128
    dgl = _matmul("d_g_low", dlogit, wgate_pad, NT, rows, 128, GLA_QK, tm=tm, tn=128, tk=GLA_QK, out_dtype=BF16)
    g_wgate = _matmul("d_w_gate", proj1, dlogit, TN, 128, GLA_QK, rows, tm=128, tn=GLA_QK, tk=tmk, a_off=(0, gl_blk))
    dproj1 = jnp.concatenate([dq1, dk1, dv1, dz1, dgl], axis=1)
    g_wc = _matmul("d_w_in_c", hn1, dproj1, TN, D_MODEL, IN_C_PAD, rows, tm=1024, tn=896, tk=tmk)
    dhn1 = _matmul("d_hn1", dproj1, wc, NT, rows, D_MODEL, IN_C_PAD, tm=tm, tn=512, tk=896)
    dh1, d_norm_c = _rms_bwd("norm_c_bwd", dhn1, h1, norm_c_f, dh2)

    g_wout_ab = _matmul("d_w_out_ab", mix, dh1, TN, OUT_AB, D_MODEL, rows, tm=1024, tn=1024, tk=tmk)
    dmix = _matmul("d_mix", dh1, wout_ab, NT, rows, OUT_AB, D_MODEL, tm=tm, tn=512, tk=1024)
    dq0, dk0, dv0, dza, d_ret_norm = _ret_bwd(proj0, cosf, sinf, rtab, ret_norm_w, o_ret, dmix, ret_states)
    dzb, dt_glu, dg_direct = _s5_gate_bwd(dmix, g_s5, t_glu, proj0)
    g_wglu = _matmul("d_w_glu", g_s5, dt_glu, TN, S5_W, S5_W, rows, tm=1024, tn=1024, tk=tmk)
    dy_s5 = _matmul("d_y_s5", dt_glu, wglu, NT, rows, S5_W, S5_W, tm=tm, tn=512, tk=S5_W,
                    extras=[(dg_direct, (tm, 512), lambda i, j, kk: (i, j)),
                            (y_s5, (tm, 512), lambda i, j, kk: (i, j))],
                    epilogue=lambda acc, dg, yv: (acc + dg) * _gelu_grad(yv))
    g_wc_st = jnp.stack([g_wc[:, j * (IN_C // N_SHARD):(j + 1) * (IN_C // N_SHARD)] for j in range(N_SHARD)])
    rs1_names = ["w_out_ab", "w_in_c", "w_out_c", "w_glu"]
    rs1_shapes = [w_out_ab.shape[1:], w_in_c.shape[1:], w_out_c.shape[1:], s5_w_glu.shape[1:]]
    rs1_plan = _rs_pair_plan(late_kinds, rs1_shapes)
    rs1_land = [_empty_hbm((N_SHARD, r // 2, cc), F32) for (r, cc) in rs1_shapes]
    p_send, p_recv, p_bufs, dy_s5 = _copies_start("rs1_pair_start", [g_wout_ab, g_wc_st, g_wout_c, g_wglu] + rs1_land,
                                                  N_SHARD * 4, rs1_plan, dy_s5)
    du, dbr_d, dbi_d, dcr_d, dci_d, dar_p, dai_p, dd_p = _s5_bwd(proj0, dy_s5, ab, bd_b, bd_c, s5_d, (s5_er, s5_ei))
    p_bufs = _copies_wait("rs1_pair_wait", p_send, p_recv, p_bufs, rs1_plan, du)
    rs1_pairs = [_rs_pair_add("rs_pair_add_" + nm, g, t, kd, ss, c_arr)
                 for nm, g, t, kd, ss in zip(rs1_names, p_bufs[:4], p_bufs[4:], late_kinds, rs1_shapes)]
    dproj0 = jnp.concatenate([dq0, dk0, dv0, dza, du, dzb], axis=1)
    rs1_chip_plan = _rs_chip_plan(4)
    rs1_land2 = [_empty_hbm((3, r // 2, cc), BF16) for (r, cc) in rs1_shapes]
    c_send, c_recv, c_bufs, dproj0 = _copies_start("rs1_chip_start", [p[1] for p in rs1_pairs] + rs1_land2, 12,
                                                   rs1_chip_plan, dproj0)
    g_wab = _matmul("d_w_in_ab", hn0, dproj0, TN, D_MODEL, IN_AB, rows, tm=1024, tn=1024, tk=tmk)
    rs2_shapes = [w_in_ab.shape[1:]]
    rs2_plan = _rs_pair_plan(["col"], rs2_shapes)
    rs2_land = [_empty_hbm((N_SHARD, rs2_shapes[0][0] // 2, rs2_shapes[0][1]), F32)]
    q_send, q_recv, q_bufs, dproj0 = _copies_start("rs2_pair_start", [g_wab] + rs2_land, N_SHARD, rs2_plan, dproj0)
    dhn0 = _matmul("d_hn0", dproj0, wab, NT, rows, D_MODEL, IN_AB, tm=tm, tn=512, tk=2048)
    dh0, d_norm_ab = _rms_bwd("norm_ab_bwd", dhn0, h0, norm_ab_w, dh1)
    grad_x = dh0[CHUNK:][None]
    c_bufs = _copies_wait("rs1_chip_wait", c_send, c_recv, c_bufs, rs1_chip_plan, dh0)
    rs1_halves = [_rs_chip_add("rs_chip_add_" + nm, p[0], t, ss, mine_c)
                  for nm, p, t, ss in zip(rs1_names, rs1_pairs, c_bufs[4:], rs1_shapes)]
    g_w_out_ab, g_w_in_c, g_w_out_c, g_w_glu = _rs_pair_share("rs1_pair_share", rs1_halves, rs1_shapes)
    q_bufs = _copies_wait("rs2_pair_wait", q_send, q_recv, q_bufs, rs2_plan, g_w_glu)
    rs2_pair = _rs_pair_add("rs_pair_add_w_in_ab", q_bufs[0], q_bufs[1], "col", rs2_shapes[0], c_arr)
    rs2_chip_plan = _rs_chip_plan(1)
    rs2_land2 = [_empty_hbm((3, rs2_shapes[0][0] // 2, rs2_shapes[0][1]), BF16)]

    d_ab_re = jnp.sum(dar_p, axis=1).reshape(S5_G, S5_P)
    d_ab_im = jnp.sum(dai_p, axis=1).reshape(S5_G, S5_P)
    small_local = [loss_dev, dh0[PAD:CHUNK], d_norm_ab, d_ret_norm.reshape(1, RET_W), d_ab_re, d_ab_im,
                   _bdiag_in_extract(dbr_d), _bdiag_in_extract(dbi_d),
                   _bdiag_out_extract(dcr_d), _bdiag_out_extract(dci_d),
                   jnp.sum(dd_p, axis=1).reshape(1, S5_W), d_norm_c, g_wgate[:GLA_RANK],
                   d_bgate.reshape(1, GLA_QK), d_gla_norm.reshape(1, GLA_W), d_final]
    small_shapes = [a.shape for a in small_local]
    r_send, r_recv, r_bufs, small_buf = _copies_start("rs2_chip_start", [rs2_pair[1]] + rs2_land2, 3, rs2_chip_plan,
                                                      _pack(small_local))
    summed = _unpack(_allreduce_small(small_buf), small_shapes)
    (loss, g_meta_f, g_norm_ab, g_ret_norm, g_ab_re, g_ab_im, g_bb_re, g_bb_im, g_c_re, g_c_im, g_d,
     g_norm_c_f, g_wgate_f, g_bgate_f, g_gla_norm_f, g_final) = summed
    _, s5_vjp = jax.vjp(_s5_discretize, s5_lam_re[0], s5_lam_im[0], s5_log_dt[0], s5_b_re[0], s5_b_im[0])
    g_lam_re, g_lam_im, g_log_dt, g_b_re, g_b_im = s5_vjp((g_ab_re, g_ab_im, g_bb_re, g_bb_im))

    def take(a, width):
        return lax.dynamic_slice_in_dim(a, mine * width, width, axis=1)

    grads = {
        "meta": take(g_meta_f, q4), "norm_ab_w": g_norm_ab, "ret_norm_w": g_ret_norm,
        "s5_lam_re": g_lam_re[None], "s5_lam_im": g_lam_im[None], "s5_log_dt": g_log_dt[None],
        "s5_b_re": g_b_re[None], "s5_b_im": g_b_im[None], "s5_c_re": g_c_re[None], "s5_c_im": g_c_im[None],
        "s5_d": g_d, "s5_w_glu": g_w_glu[None], "w_out_ab": g_w_out_ab[None], "norm_c_w": take(g_norm_c_f, q4),
        "w_in_c": g_w_in_c[None], "gla_w_gate": take(g_wgate_f, g4)[None], "gla_b_gate": take(g_bgate_f, g4),
        "gla_norm_w": take(g_gla_norm_f, q4), "w_out_c": g_w_out_c[None], "final_norm_w": g_final.reshape(D_MODEL),
    }
    weights = dict(meta=meta, norm_ab_w=norm_ab_w, w_in_ab=w_in_ab, ret_norm_w=ret_norm_w, s5_lam_re=s5_lam_re,
                   s5_lam_im=s5_lam_im, s5_log_dt=s5_log_dt, s5_b_re=s5_b_re, s5_b_im=s5_b_im, s5_c_re=s5_c_re,
                   s5_c_im=s5_c_im, s5_d=s5_d, s5_w_glu=s5_w_glu, w_out_ab=w_out_ab, norm_c_w=norm_c_w,
                   w_in_c=w_in_c, gla_w_gate=gla_w_gate, gla_b_gate=gla_b_gate, gla_norm_w=gla_norm_w,
                   w_out_c=w_out_c, final_norm_w=final_norm_w)
    m_in = dict(meta=m_meta, norm_ab_w=m_norm_ab_w, w_in_ab=m_w_in_ab, ret_norm_w=m_ret_norm_w,
                s5_lam_re=m_s5_lam_re, s5_lam_im=m_s5_lam_im, s5_log_dt=m_s5_log_dt, s5_b_re=m_s5_b_re,
                s5_b_im=m_s5_b_im, s5_c_re=m_s5_c_re, s5_c_im=m_s5_c_im, s5_d=m_s5_d, s5_w_glu=m_s5_w_glu,
                w_out_ab=m_w_out_ab, norm_c_w=m_norm_c_w, w_in_c=m_w_in_c, gla_w_gate=m_gla_w_gate,
                gla_b_gate=m_gla_b_gate, gla_norm_w=m_gla_norm_w, w_out_c=m_w_out_c, final_norm_w=m_final_norm_w)
    v_in = dict(meta=v_meta, norm_ab_w=v_norm_ab_w, w_in_ab=v_w_in_ab, ret_norm_w=v_ret_norm_w,
                s5_lam_re=v_s5_lam_re, s5_lam_im=v_s5_lam_im, s5_log_dt=v_s5_log_dt, s5_b_re=v_s5_b_re,
                s5_b_im=v_s5_b_im, s5_c_re=v_s5_c_re, s5_c_im=v_s5_c_im, s5_d=v_s5_d, s5_w_glu=v_s5_w_glu,
                w_out_ab=v_w_out_ab, norm_c_w=v_norm_c_w, w_in_c=v_w_in_c, gla_w_gate=v_gla_w_gate,
                gla_b_gate=v_gla_b_gate, gla_norm_w=v_gla_norm_w, w_out_c=v_w_out_c, final_norm_w=v_final_norm_w)
    order = list(weights)
    big_names = ["s5_w_glu", "w_out_ab", "w_in_c", "w_out_c", "w_in_ab"]
    small_names = [nm for nm in order if nm not in big_names]
    delta, new_m, new_v = {}, {}, {}

    def big_update(nm):
        shp = weights[nm].shape
        d2, m2, v2 = _adamw("adamw_" + nm, weights[nm][0], grads[nm][0], m_in[nm][0], v_in[nm][0])
        delta[nm], new_m[nm], new_v[nm] = d2.reshape(shp), m2.reshape(shp), v2.reshape(shp)

    for nm in big_names[:-1]:
        big_update(nm)
    sshapes = [weights[nm].shape for nm in small_names]
    d2, m2, v2 = _adamw("adamw_small", _pack([weights[nm] for nm in small_names]),
                        _pack([grads[nm] for nm in small_names]), _pack([m_in[nm] for nm in small_names]),
                        _pack([v_in[nm] for nm in small_names]))
    for nm, dd, mm, vv in zip(small_names, _unpack(d2, sshapes), _unpack(m2, sshapes), _unpack(v2, sshapes)):
        delta[nm], new_m[nm], new_v[nm] = dd, mm, vv
    r_bufs = _copies_wait("rs2_chip_wait", r_send, r_recv, r_bufs, rs2_chip_plan,
                          [v2] + [new_v[nm] for nm in big_names[:-1]])
    rs2_half = _rs_chip_add("rs_chip_add_w_in_ab", rs2_pair[0], r_bufs[1], rs2_shapes[0], mine_c)
    grads["w_in_ab"] = _rs_pair_share("rs2_pair_share", [rs2_half], rs2_shapes)[0][None]
    big_update("w_in_ab")
    grads = {nm: grads[nm].reshape(weights[nm].shape) for nm in order}
    return (loss.reshape(()), grad_x, *[grads[nm] for nm in order], *[delta[nm] for nm in order],
            *[new_m[nm] for nm in order], *[new_v[nm] for nm in order])
```

```python
import functools
import math

import jax
import jax.numpy as jnp
from jax import lax
from jax.experimental import pallas as pl
from jax.experimental.pallas import tpu as pltpu

F32 = jnp.float32
BF16 = jnp.bfloat16
MESH = pl.DeviceIdType.MESH

D_MODEL = 2048
N_META = 16
CHUNK = 128
SUB = 16
NSUB = CHUNK // SUB
PAD = CHUNK - N_META
EPS = 1e-6

RET_HEADS = 8
RET_DK = 128
RET_DV = 256
RET_QK = RET_HEADS * RET_DK
RET_W = RET_HEADS * RET_DV
ROPE_BASE = 10000.0

S5_W = 1024
S5_GH = 16
S5_G = S5_W // S5_GH
S5_P = 64
S5_TG = 8
S5_NT = S5_G // S5_TG
S5_TU = S5_TG * S5_GH
S5_TS = S5_TG * S5_P

GLA_HEADS = 4
GLA_DK = 256
GLA_DV = 512
GLA_QK = GLA_HEADS * GLA_DK
GLA_W = GLA_HEADS * GLA_DV
GLA_RANK = 16
GLA_TAU = 16.0

IN_AB = 2 * RET_QK + 2 * RET_W + 2 * S5_W
OUT_AB = RET_W + S5_W
IN_C = 2 * GLA_QK + 2 * GLA_W + GLA_RANK
IN_C_PAD = 2 * GLA_QK + 2 * GLA_W + 128

ADAM_LR = 0.001
ADAM_B1 = 0.9
ADAM_B2 = 0.999
ADAM_EPS = 1e-08
ADAM_WD = 0.01
ADAM_STEP = 10

N_SHARD = 4
SMALL_COLS = 512

NN = (((1,), (0,)), ((), ()))
NT = (((1,), (1,)), ((), ()))
TN = (((0,), (0,)), ((), ()))


def _dot(a, b, dims=NN):
    return lax.dot_general(a.astype(BF16), b.astype(BF16), dims, preferred_element_type=F32)


def _mo(v, m):
    return v if isinstance(v, int) else pl.multiple_of(v, m)


def _sigmoid(x):
    return 1.0 / (1.0 + jnp.exp(-x))


def _row_tile(rows, cap):
    n = rows // CHUNK
    best = 1
    for d in range(1, n + 1):
        if n % d == 0 and d * CHUNK <= cap:
            best = d
    return best * CHUNK


def _col_tile(cols, cap):
    n = cols // 128
    best = 1
    for d in range(1, n + 1):
        if n % d == 0 and d * 128 <= cap:
            best = d
    return best * 128


def _matmul(name, a, b, dims, m, n, k, *, tm, tn, tk, out_dtype=F32, a_off=(0, 0), b_off=(0, 0),
            extras=(), epilogue=None, out_shape=None, out_spec=None):
    nk = k // tk
    assert m % tm == 0 and n % tn == 0 and k % tk == 0, (name, m, n, k, tm, tn, tk)
    ar, ac = a_off
    br, bc = b_off
    if dims == NN:
        a_spec = pl.BlockSpec((tm, tk), lambda i, j, kk: (i + ar, kk + ac))
        b_spec = pl.BlockSpec((tk, tn), lambda i, j, kk: (kk + br, j + bc))
    elif dims == NT:
        a_spec = pl.BlockSpec((tm, tk), lambda i, j, kk: (i + ar, kk + ac))
        b_spec = pl.BlockSpec((tn, tk), lambda i, j, kk: (j + br, kk + bc))
    else:
        a_spec = pl.BlockSpec((tk, tm), lambda i, j, kk: (kk + ar, i + ac))
        b_spec = pl.BlockSpec((tk, tn), lambda i, j, kk: (kk + br, j + bc))
    n_extra = len(extras)

    def body(*refs):
        a_ref, b_ref = refs[0], refs[1]
        e_refs = refs[2:2 + n_extra]
        o_ref = refs[2 + n_extra]
        acc_ref = refs[3 + n_extra]
        kk = pl.program_id(2)

        @pl.when(kk == 0)
        def _():
            acc_ref[...] = jnp.zeros_like(acc_ref)

        acc_ref[...] += _dot(a_ref[...], b_ref[...], dims)

        @pl.when(kk == nk - 1)
        def _():
            acc = acc_ref[...]
            if epilogue is not None:
                acc = epilogue(acc, *[e[...] for e in e_refs])
            o_ref[...] = acc.astype(o_ref.dtype)

    if out_shape is None:
        out_shape = jax.ShapeDtypeStruct((m, n), out_dtype)
    if out_spec is None:
        out_spec = pl.BlockSpec((tm, tn), lambda i, j, kk: (i, j))
    return pl.pallas_call(
        body, name=name, grid=(m // tm, n // tn, nk),
        in_specs=[a_spec, b_spec] + [pl.BlockSpec(bs, im) for (_, bs, im) in extras],
        out_specs=out_spec, out_shape=out_shape,
        scratch_shapes=[pltpu.VMEM((tm, tn), F32)],
        compiler_params=pltpu.CompilerParams(dimension_semantics=("parallel", "parallel", "arbitrary")),
    )(a, b, *[e for (e, _, _) in extras])


def _rms_fwd(name, h, w):
    rows, d = h.shape
    tm = _row_tile(rows, 512)

    def body(h_ref, w_ref, o_ref):
        x = h_ref[...]
        r = lax.rsqrt(jnp.mean(x * x, axis=-1, keepdims=True) + EPS)
        o_ref[...] = (x * r * w_ref[...]).astype(BF16)

    return pl.pallas_call(
        body, name=name, grid=(rows // tm,),
        in_specs=[pl.BlockSpec((tm, d), lambda i: (i, 0)), pl.BlockSpec((1, d), lambda i: (0, 0))],
        out_specs=pl.BlockSpec((tm, d), lambda i: (i, 0)),
        out_shape=jax.ShapeDtypeStruct((rows, d), BF16),
    )(h, w)


def _rms_bwd(name, dhn, h, w, dres):
    rows, d = h.shape
    tm = _row_tile(rows, 384)

    def body(g_ref, h_ref, w_ref, r_ref, dh_ref, dw_ref):
        i = pl.program_id(0)
        x = h_ref[...]
        r = lax.rsqrt(jnp.mean(x * x, axis=-1, keepdims=True) + EPS)
        xh = x * r
        g = g_ref[...]
        gw = g * w_ref[...]
        dh_ref[...] = r_ref[...] + r * (gw - xh * jnp.mean(gw * xh, axis=-1, keepdims=True))

        @pl.when(i == 0)
        def _():
            dw_ref[...] = jnp.zeros_like(dw_ref)

        dw_ref[...] += jnp.sum(g * xh, axis=0, keepdims=True)

    return pl.pallas_call(
        body, name=name, grid=(rows // tm,),
        in_specs=[pl.BlockSpec((tm, d), lambda i: (i, 0)), pl.BlockSpec((tm, d), lambda i: (i, 0)),
                  pl.BlockSpec((1, d), lambda i: (0, 0)), pl.BlockSpec((tm, d), lambda i: (i, 0))],
        out_specs=[pl.BlockSpec((tm, d), lambda i: (i, 0)), pl.BlockSpec((1, d), lambda i: (0, 0))],
        out_shape=[jax.ShapeDtypeStruct((rows, d), F32), jax.ShapeDtypeStruct((1, d), F32)],
    )(dhn, h, w, dres)


def _final_loss(h2, w, target):
    rows, d = h2.shape

    def body(h_ref, w_ref, t_ref, loss_ref, dh_ref, dw_ref):
        i = pl.program_id(0)

        @pl.when(i == 0)
        def _():
            loss_ref[...] = jnp.zeros_like(loss_ref)
            dw_ref[...] = jnp.zeros_like(dw_ref)
            dh_ref[...] = jnp.zeros_like(dh_ref)

        @pl.when(i > 0)
        def _():
            x = h_ref[...]
            r = lax.rsqrt(jnp.mean(x * x, axis=-1, keepdims=True) + EPS)
            xh = x * r
            wv = w_ref[...]
            err = xh * wv - t_ref[...]
            loss_ref[...] += 0.5 * jnp.sum(jnp.mean(err * err, axis=-1, keepdims=True), axis=0, keepdims=True)
            g = err * (1.0 / d)
            gw = g * wv
            dh_ref[...] = r * (gw - xh * jnp.mean(gw * xh, axis=-1, keepdims=True))
            dw_ref[...] += jnp.sum(g * xh, axis=0, keepdims=True)

    return pl.pallas_call(
        body, name="final_loss", grid=(rows // CHUNK,),
        in_specs=[pl.BlockSpec((CHUNK, d), lambda i: (i, 0)), pl.BlockSpec((1, d), lambda i: (0, 0)),
                  pl.BlockSpec((CHUNK, d), lambda i: (jnp.maximum(i - 1, 0), 0))],
        out_specs=[pl.BlockSpec((1, 1), lambda i: (0, 0)), pl.BlockSpec((CHUNK, d), lambda i: (i, 0)),
                   pl.BlockSpec((1, d), lambda i: (0, 0))],
        out_shape=[jax.ShapeDtypeStruct((1, 1), F32), jax.ShapeDtypeStruct((rows, d), F32),
                   jax.ShapeDtypeStruct((1, d), F32)],
    )(h2, w, target)


def _gate_fwd(o, z, w):
    rs = lax.rsqrt(jnp.mean(o * o, axis=-1, keepdims=True) + EPS)
    return o * rs * w * (z * _sigmoid(z))


def _gate_bwd(dout, o, z, w):
    rs = lax.rsqrt(jnp.mean(o * o, axis=-1, keepdims=True) + EPS)
    yn = o * rs
    sg = _sigmoid(z)
    sil = z * sg
    dsil = sg * (1.0 + z * (1.0 - sg))
    dz = dout * yn * w * dsil
    dyn = dout * w * sil
    dw = jnp.sum(dout * yn * sil, axis=0, keepdims=True)
    do = rs * (dyn - yn * jnp.mean(dyn * yn, axis=-1, keepdims=True))
    return do, dz, dw


def _rope(t, cosf, sinf):
    return t * cosf + pltpu.roll(t, RET_DK // 2, 1) * sinf


def _rope_t(d, cosf, sinf):
    return d * cosf + pltpu.roll(d * sinf, RET_DK // 2, 1)


def _ret_tables():
    log_g = jnp.log1p(-jnp.exp2(-5.0 - jnp.arange(RET_HEADS, dtype=F32)))
    idx = jnp.arange(CHUNK, dtype=F32)
    diff = idx[:, None] - idx[None, :]
    decay = jnp.where(diff >= 0, jnp.exp(log_g[:, None, None] * jnp.maximum(diff, 0.0)), 0.0)
    kw = jnp.exp(log_g[:, None] * (CHUNK - 1 - idx))
    qw = jnp.exp(log_g[:, None] * (idx + 1.0))
    gch = jnp.exp(log_g * CHUNK)
    kw = jnp.broadcast_to(kw[:, :, None], (RET_HEADS, CHUNK, RET_DK))
    qw = jnp.broadcast_to(qw[:, :, None], (RET_HEADS, CHUNK, RET_DK))
    gch = jnp.broadcast_to(gch[:, None, None], (RET_HEADS, 1, RET_DV))
    return decay, kw, qw, gch


def _rope_tables(rows):
    pos = jnp.arange(rows, dtype=F32) - float(PAD)
    inv_freq = jnp.power(ROPE_BASE, -jnp.arange(0, RET_DK, 2, dtype=F32) / RET_DK)
    ang = pos[:, None] * inv_freq[None, :]
    cos, sin = jnp.cos(ang), jnp.sin(ang)
    return jnp.concatenate([cos, cos], axis=1), jnp.concatenate([-sin, sin], axis=1)


RET_HB = 4
RET_QB = RET_HB * RET_DK
RET_VB = RET_HB * RET_DV


def _ret_in_specs(rev, nc):
    def cn(n):
        return (nc - 1 - n) if rev else n
    kb = RET_QK // RET_QB
    vb = 2 * RET_QK // RET_VB
    zb = (2 * RET_QK + RET_W) // RET_VB
    return [
        pl.BlockSpec((CHUNK, RET_QB), lambda h, n: (cn(n), h)),
        pl.BlockSpec((CHUNK, RET_QB), lambda h, n: (cn(n), kb + h)),
        pl.BlockSpec((CHUNK, RET_VB), lambda h, n: (cn(n), vb + h)),
        pl.BlockSpec((CHUNK, RET_VB), lambda h, n: (cn(n), zb + h)),
        pl.BlockSpec((CHUNK, RET_DK), lambda h, n: (cn(n), 0)),
        pl.BlockSpec((CHUNK, RET_DK), lambda h, n: (cn(n), 0)),
        pl.BlockSpec((RET_HB, CHUNK, CHUNK), lambda h, n: (h, 0, 0)),
        pl.BlockSpec((RET_HB, CHUNK, RET_DK), lambda h, n: (h, 0, 0)),
        pl.BlockSpec((RET_HB, CHUNK, RET_DK), lambda h, n: (h, 0, 0)),
        pl.BlockSpec((RET_HB, 1, RET_DV), lambda h, n: (h, 0, 0)),
        pl.BlockSpec((1, RET_VB), lambda h, n: (0, h)),
    ]


def _ret_fwd(proj, cosf, sinf, tables, normw):
    rows = proj.shape[0]
    nc = rows // CHUNK
    decay, kw, qw, gch = tables

    def body(q_ref, k_ref, v_ref, z_ref, cos_ref, sin_ref, dm_ref, kw_ref, qw_ref, g_ref, w_ref,
             o_ref, oa_ref, st_ref, s_scr):
        n = pl.program_id(1)

        @pl.when(n == 0)
        def _():
            s_scr[...] = jnp.zeros_like(s_scr)

        cosv, sinv = cos_ref[...], sin_ref[...]
        for hh in range(RET_HB):
            qc = slice(hh * RET_DK, (hh + 1) * RET_DK)
            vc = slice(hh * RET_DV, (hh + 1) * RET_DV)
            q = _rope(q_ref[:, qc], cosv, sinv)
            k = _rope(k_ref[:, qc], cosv, sinv) * (RET_DK ** -0.5)
            v = v_ref[:, vc]
            s = s_scr[hh]
            st_ref[hh, 0] = s.astype(BF16)
            a = _dot(q, k, NT) * dm_ref[hh]
            o = _dot(a, v) + _dot(q * qw_ref[hh], s)
            s_scr[hh] = s * g_ref[hh] + _dot(k * kw_ref[hh], v, TN)
            o_ref[:, vc] = o
            oa_ref[:, vc] = _gate_fwd(o, z_ref[:, vc], w_ref[:, vc]).astype(BF16)

    return pl.pallas_call(
        body, name="ret_fwd", grid=(RET_HEADS // RET_HB, nc),
        in_specs=_ret_in_specs(False, nc),
        out_specs=[pl.BlockSpec((CHUNK, RET_VB), lambda h, n: (n, h)),
                   pl.BlockSpec((CHUNK, RET_VB), lambda h, n: (n, h)),
                   pl.BlockSpec((RET_HB, 1, RET_DK, RET_DV), lambda h, n: (h, n, 0, 0))],
        out_shape=[jax.ShapeDtypeStruct((rows, RET_W), F32), jax.ShapeDtypeStruct((rows, RET_W), BF16),
                   jax.ShapeDtypeStruct((RET_HEADS, nc, RET_DK, RET_DV), BF16)],
        scratch_shapes=[pltpu.VMEM((RET_HB, RET_DK, RET_DV), F32)],
        compiler_params=pltpu.CompilerParams(dimension_semantics=("parallel", "arbitrary")),
    )(proj, proj, proj, proj, cosf, sinf, decay, kw, qw, gch, normw)


def _ret_bwd(proj, cosf, sinf, tables, normw, o_ret, dmix, states):
    rows = proj.shape[0]
    nc = rows // CHUNK
    decay, kw, qw, gch = tables

    def rn(n):
        return nc - 1 - n

    def body(q_ref, k_ref, v_ref, z_ref, cos_ref, sin_ref, dm_ref, kw_ref, qw_ref, g_ref, w_ref,
             o_ref, do_ref, st_ref, dq_ref, dk_ref, dv_ref, dz_ref, dw_ref, ds_scr):
        n = pl.program_id(1)

        @pl.when(n == 0)
        def _():
            ds_scr[...] = jnp.zeros_like(ds_scr)
            dw_ref[...] = jnp.zeros_like(dw_ref)

        cosv, sinv = cos_ref[...], sin_ref[...]
        for hh in range(RET_HB):
            qc = slice(hh * RET_DK, (hh + 1) * RET_DK)
            vc = slice(hh * RET_DV, (hh + 1) * RET_DV)
            q = _rope(q_ref[:, qc], cosv, sinv)
            k = _rope(k_ref[:, qc], cosv, sinv) * (RET_DK ** -0.5)
            v = v_ref[:, vc]
            do, dz, dw = _gate_bwd(do_ref[:, vc], o_ref[:, vc], z_ref[:, vc], w_ref[:, vc])
            dz_ref[:, vc] = dz.astype(BF16)
            dw_ref[hh] += dw
            dm = dm_ref[hh]
            s = st_ref[hh, 0]
            g1 = ds_scr[hh]
            p = _dot(q, k, NT) * dm
            kwv = k * kw_ref[hh]
            qwv = q * qw_ref[hh]
            dp = _dot(do, v, NT)
            da = dp * dm
            dv = _dot(p, do, TN) + _dot(kwv, g1)
            dq = _dot(da, k) + _dot(do, s, NT) * qw_ref[hh]
            dk = _dot(da, q, TN) + _dot(v, g1, NT) * kw_ref[hh]
            ds_scr[hh] = g1 * g_ref[hh] + _dot(qwv, do, TN)
            dv_ref[:, vc] = dv.astype(BF16)
            dq_ref[:, qc] = _rope_t(dq, cosv, sinv).astype(BF16)
            dk_ref[:, qc] = _rope_t(dk * (RET_DK ** -0.5), cosv, sinv).astype(BF16)

    in_specs = _ret_in_specs(True, nc) + [
        pl.BlockSpec((CHUNK, RET_VB), lambda h, n: (rn(n), h)),
        pl.BlockSpec((CHUNK, RET_VB), lambda h, n: (rn(n), h)),
        pl.BlockSpec((RET_HB, 1, RET_DK, RET_DV), lambda h, n: (h, rn(n), 0, 0)),
    ]
    return pl.pallas_call(
        body, name="ret_bwd", grid=(RET_HEADS // RET_HB, nc),
        in_specs=in_specs,
        out_specs=[pl.BlockSpec((CHUNK, RET_QB), lambda h, n: (rn(n), h)),
                   pl.BlockSpec((CHUNK, RET_QB), lambda h, n: (rn(n), h)),
                   pl.BlockSpec((CHUNK, RET_VB), lambda h, n: (rn(n), h)),
                   pl.BlockSpec((CHUNK, RET_VB), lambda h, n: (rn(n), h)),
                   pl.BlockSpec((RET_HB, 1, RET_DV), lambda h, n: (h, 0, 0))],
        out_shape=[jax.ShapeDtypeStruct((rows, RET_QK), BF16), jax.ShapeDtypeStruct((rows, RET_QK), BF16),
                   jax.ShapeDtypeStruct((rows, RET_W), BF16), jax.ShapeDtypeStruct((rows, RET_W), BF16),
                   jax.ShapeDtypeStruct((RET_HEADS, 1, RET_DV), F32)],
        scratch_shapes=[pltpu.VMEM((RET_HB, RET_DK, RET_DV), F32)],
        compiler_params=pltpu.CompilerParams(dimension_semantics=("parallel", "arbitrary")),
    )(proj, proj, proj, proj, cosf, sinf, decay, kw, qw, gch, normw, o_ret, dmix, states)


def _s5_discretize(lam_re, lam_im, log_dt, b_re, b_im):
    dt = jnp.exp(log_dt)[:, None]
    mag = jnp.exp(lam_re * dt)
    ab_re, ab_im = mag * jnp.cos(lam_im * dt), mag * jnp.sin(lam_im * dt)
    den = lam_re * lam_re + lam_im * lam_im
    nr, ni = ab_re - 1.0, ab_im
    f_re = (nr * lam_re + ni * lam_im) / den
    f_im = (ni * lam_re - nr * lam_im) / den
    bb_re = f_re[..., None] * b_re - f_im[..., None] * b_im
    bb_im = f_re[..., None] * b_im + f_im[..., None] * b_re
    return ab_re, ab_im, bb_re, bb_im


def _bdiag_in(bb):
    t = bb.reshape(S5_NT, S5_TG, S5_P, S5_GH).transpose(0, 1, 3, 2)
    eye = jnp.eye(S5_TG, dtype=bb.dtype)
    full = t[:, :, :, None, :] * eye[None, :, None, :, None]
    return full.reshape(S5_NT, S5_TU, S5_TS)


def _bdiag_in_extract(dense):
    t = dense.reshape(S5_NT, S5_TG, S5_GH, S5_TG, S5_P)
    diag = jnp.stack([t[:, g, :, g, :] for g in range(S5_TG)], axis=1)
    return diag.transpose(0, 1, 3, 2).reshape(S5_G, S5_P, S5_GH)


def _bdiag_out(c):
    t = c.reshape(S5_NT, S5_TG, S5_GH, S5_P).transpose(0, 1, 3, 2)
    eye = jnp.eye(S5_TG, dtype=c.dtype)
    full = t[:, :, :, None, :] * eye[None, :, None, :, None]
    return full.reshape(S5_NT, S5_TS, S5_TU)


def _bdiag_out_extract(dense):
    t = dense.reshape(S5_NT, S5_TG, S5_P, S5_TG, S5_GH)
    diag = jnp.stack([t[:, g, :, g, :] for g in range(S5_TG)], axis=1)
    return diag.transpose(0, 1, 3, 2).reshape(S5_G, S5_GH, S5_P)


def _cmul(ar, ai, br, bi):
    return ar * br - ai * bi, ar * bi + ai * br


S5_SEG = 8
S5_STEPS = CHUNK // S5_SEG


def _seg_perm(x):
    c = x.shape[1]
    return jnp.swapaxes(x.reshape(S5_SEG, S5_STEPS, c), 0, 1).reshape(CHUNK, c)


def _seg_unperm(x):
    c = x.shape[1]
    return jnp.swapaxes(x.reshape(S5_STEPS, S5_SEG, c), 0, 1).reshape(CHUNK, c)


def _rows(x, p):
    return x[p * S5_SEG:(p + 1) * S5_SEG]


def _s5_tables(ar, ai, tr_scr, ti_scr, wfr_scr, wfi_scr, wbr_scr, wbi_scr):
    row = lax.broadcasted_iota(jnp.int32, (S5_SEG, 1), 0)
    a8r = jnp.broadcast_to(ar, (S5_SEG, S5_TS))
    a8i = jnp.broadcast_to(ai, (S5_SEG, S5_TS))
    pr, pi = a8r, a8i
    for p in range(S5_STEPS):
        tr_scr[p * S5_SEG:(p + 1) * S5_SEG, :] = pr
        ti_scr[p * S5_SEG:(p + 1) * S5_SEG, :] = pi
        if p < S5_STEPS - 1:
            pr, pi = _cmul(pr, pi, a8r, a8i)
    wr, wi = pr, pi
    sh = 1
    while sh < S5_SEG:
        keep = row >= sh
        sr = jnp.where(keep, pltpu.roll(wr, sh, 0), 1.0)
        si = jnp.where(keep, pltpu.roll(wi, sh, 0), 0.0)
        wr, wi = _cmul(wr, wi, sr, si)
        sh *= 2
    wfr_scr[...] = wr
    wfi_scr[...] = wi
    wr, wi = pr, -pi
    sh = 1
    while sh < S5_SEG:
        keep = row < S5_SEG - sh
        sr = jnp.where(keep, pltpu.roll(wr, S5_SEG - sh, 0), 1.0)
        si = jnp.where(keep, pltpu.roll(wi, S5_SEG - sh, 0), 0.0)
        wr, wi = _cmul(wr, wi, sr, si)
        sh *= 2
    wbr_scr[...] = wr
    wbi_scr[...] = wi


def _seg_scan(vr, vi, ar, ai, tr_scr, ti_scr, wr_scr, wi_scr, c0r, c0i, down):
    row = lax.broadcasted_iota(jnp.int32, (S5_SEG, 1), 0)
    sgn = 1.0 if down else -1.0
    order = list(range(S5_STEPS)) if down else list(range(S5_STEPS - 1, -1, -1))
    xr, xi = _rows(vr, order[0]), _rows(vi, order[0])
    loc = {order[0]: (xr, xi)}
    for p in order[1:]:
        mr, mi = _cmul(ar, sgn * ai, xr, xi)
        xr, xi = mr + _rows(vr, p), mi + _rows(vi, p)
        loc[p] = (xr, xi)
    last = S5_STEPS - 1
    mr, mi = tr_scr[last * S5_SEG:(last + 1) * S5_SEG, :], sgn * ti_scr[last * S5_SEG:(last + 1) * S5_SEG, :]
    er, ei = xr, xi
    sh = 1
    while sh < S5_SEG:
        if down:
            keep = row >= sh
            sr, si = pltpu.roll(er, sh, 0), pltpu.roll(ei, sh, 0)
        else:
            keep = row < S5_SEG - sh
            sr, si = pltpu.roll(er, S5_SEG - sh, 0), pltpu.roll(ei, S5_SEG - sh, 0)
        pr, pi = _cmul(mr, mi, jnp.where(keep, sr, 0.0), jnp.where(keep, si, 0.0))
        er, ei = er + pr, ei + pi
        mr, mi = _cmul(mr, mi, mr, mi)
        sh *= 2
    pr, pi = _cmul(wr_scr[...], wi_scr[...], c0r, c0i)
    er, ei = er + pr, ei + pi
    if down:
        nr = jnp.where(row == 0, c0r, pltpu.roll(er, 1, 0))
        ni = jnp.where(row == 0, c0i, pltpu.roll(ei, 1, 0))
    else:
        nr = jnp.where(row == S5_SEG - 1, c0r, pltpu.roll(er, S5_SEG - 1, 0))
        ni = jnp.where(row == S5_SEG - 1, c0i, pltpu.roll(ei, S5_SEG - 1, 0))
    out_r, out_i = [], []
    for p in range(S5_STEPS):
        q = p if down else S5_STEPS - 1 - p
        pr, pi = _cmul(tr_scr[q * S5_SEG:(q + 1) * S5_SEG, :], sgn * ti_scr[q * S5_SEG:(q + 1) * S5_SEG, :], nr, ni)
        out_r.append(loc[p][0] + pr)
        out_i.append(loc[p][1] + pi)
    return jnp.concatenate(out_r, axis=0), jnp.concatenate(out_i, axis=0), (nr, ni), (er, ei)


def _gelu(y):
    c = math.sqrt(2.0 / math.pi)
    return 0.5 * y * (1.0 + jnp.tanh(c * (y + 0.044715 * y * y * y)))


def _gelu_grad(y):
    c = math.sqrt(2.0 / math.pi)
    th = jnp.tanh(c * (y + 0.044715 * y * y * y))
    return 0.5 * (1.0 + th) + 0.5 * y * (1.0 - th * th) * c * (1.0 + 3.0 * 0.044715 * y * y)


def _s5_fwd(proj, ab, bd_b, bd_c, dvec):
    rows = proj.shape[0]
    nc = rows // CHUNK
    ub = (2 * RET_QK + 2 * RET_W) // S5_TU
    ab_re, ab_im = ab
    bre, bim = bd_b
    cre, cim = bd_c

    def body(u_ref, ar_ref, ai_ref, bre_ref, bim_ref, cre_ref, cim_ref, d_ref,
             y_ref, g_ref, er_ref, ei_ref, tr_scr, ti_scr, wfr_scr, wfi_scr, wbr_scr, wbi_scr,
             cr_scr, ci_scr, er_scr, ei_scr):
        n = pl.program_id(1)
        ar, ai = ar_ref[0], ai_ref[0]

        @pl.when(n == 0)
        def _():
            _s5_tables(ar, ai, tr_scr, ti_scr, wfr_scr, wfi_scr, wbr_scr, wbi_scr)
            cr_scr[...] = jnp.zeros_like(cr_scr)
            ci_scr[...] = jnp.zeros_like(ci_scr)

        u = _seg_perm(u_ref[...])
        c0r, c0i = cr_scr[...], ci_scr[...]
        er_ref[0, 0] = c0r
        ei_ref[0, 0] = c0i
        xr, xi, _, (er, ei) = _seg_scan(_dot(u, bre_ref[0]), _dot(u, bim_ref[0]), ar, ai, tr_scr, ti_scr,
                                        wfr_scr, wfi_scr, c0r, c0i, True)
        er_scr[...] = er
        ei_scr[...] = ei
        cr_scr[...] = jnp.broadcast_to(er_scr[S5_SEG - 1:S5_SEG, :], cr_scr.shape)
        ci_scr[...] = jnp.broadcast_to(ei_scr[S5_SEG - 1:S5_SEG, :], ci_scr.shape)
        y = _seg_unperm(_dot(xr, cre_ref[0]) - _dot(xi, cim_ref[0]) + d_ref[...] * u)
        y_ref[...] = y
        g_ref[...] = _gelu(y).astype(BF16)

    vec = pl.BlockSpec((1, 1, S5_TS), lambda t, n: (t, 0, 0))
    return pl.pallas_call(
        body, name="s5_fwd", grid=(S5_NT, nc),
        in_specs=[pl.BlockSpec((CHUNK, S5_TU), lambda t, n: (n, ub + t)), vec, vec,
                  pl.BlockSpec((1, S5_TU, S5_TS), lambda t, n: (t, 0, 0)),
                  pl.BlockSpec((1, S5_TU, S5_TS), lambda t, n: (t, 0, 0)),
                  pl.BlockSpec((1, S5_TS, S5_TU), lambda t, n: (t, 0, 0)),
                  pl.BlockSpec((1, S5_TS, S5_TU), lambda t, n: (t, 0, 0)),
                  pl.BlockSpec((1, S5_TU), lambda t, n: (0, t))],
        out_specs=[pl.BlockSpec((CHUNK, S5_TU), lambda t, n: (n, t)),
                   pl.BlockSpec((CHUNK, S5_TU), lambda t, n: (n, t)),
                   pl.BlockSpec((1, 1, 8, S5_TS), lambda t, n: (t, n, 0, 0)),
                   pl.BlockSpec((1, 1, 8, S5_TS), lambda t, n: (t, n, 0, 0))],
        out_shape=[jax.ShapeDtypeStruct((rows, S5_W), F32), jax.ShapeDtypeStruct((rows, S5_W), BF16),
                   jax.ShapeDtypeStruct((S5_NT, nc, 8, S5_TS), F32),
                   jax.ShapeDtypeStruct((S5_NT, nc, 8, S5_TS), F32)],
        scratch_shapes=[pltpu.VMEM((CHUNK, S5_TS), F32) for _ in range(2)]
        + [pltpu.VMEM((S5_SEG, S5_TS), F32) for _ in range(8)],
        compiler_params=pltpu.CompilerParams(dimension_semantics=("parallel", "arbitrary")),
    )(proj, ab_re.reshape(S5_NT, 1, S5_TS), ab_im.reshape(S5_NT, 1, S5_TS), bre, bim, cre, cim, dvec)


def _s5_bwd(proj, dy, ab, bd_b, bd_c, dvec, entry):
    rows = proj.shape[0]
    nc = rows // CHUNK
    ub = (2 * RET_QK + 2 * RET_W) // S5_TU
    ab_re, ab_im = ab
    bre, bim = bd_b
    cre, cim = bd_c
    er, ei = entry

    def rn(n):
        return nc - 1 - n

    def body(u_ref, dy_ref, ar_ref, ai_ref, bre_ref, bim_ref, cre_ref, cim_ref, d_ref, er_ref, ei_ref,
             du_ref, dbr_ref, dbi_ref, dcr_ref, dci_ref, dar_ref, dai_ref, dd_ref,
             tr_scr, ti_scr, wfr_scr, wfi_scr, wbr_scr, wbi_scr, gr_scr, gi_scr, er_scr, ei_scr):
        n = pl.program_id(1)
        ar, ai = ar_ref[0], ai_ref[0]

        @pl.when(n == 0)
        def _():
            _s5_tables(ar, ai, tr_scr, ti_scr, wfr_scr, wfi_scr, wbr_scr, wbi_scr)
            gr_scr[...] = jnp.zeros_like(gr_scr)
            gi_scr[...] = jnp.zeros_like(gi_scr)
            for r in (dbr_ref, dbi_ref, dcr_ref, dci_ref, dar_ref, dai_ref, dd_ref):
                r[...] = jnp.zeros_like(r)

        u = _seg_perm(u_ref[...])
        dy = _seg_perm(dy_ref[...])
        xr, xi, (pr, pi), _ = _seg_scan(_dot(u, bre_ref[0]), _dot(u, bim_ref[0]), ar, ai, tr_scr, ti_scr,
                                        wfr_scr, wfi_scr, er_ref[0, 0], ei_ref[0, 0], True)
        dcr_ref[0] += _dot(xr, dy, TN)
        dci_ref[0] -= _dot(xi, dy, TN)
        gr, gi, _, (er, ei) = _seg_scan(_dot(dy, cre_ref[0], NT), -_dot(dy, cim_ref[0], NT), ar, ai, tr_scr, ti_scr,
                                        wbr_scr, wbi_scr, gr_scr[...], gi_scr[...], False)
        er_scr[...] = er
        ei_scr[...] = ei
        gr_scr[...] = jnp.broadcast_to(er_scr[0:1, :], gr_scr.shape)
        gi_scr[...] = jnp.broadcast_to(ei_scr[0:1, :], gi_scr.shape)
        xpr = jnp.concatenate([pr, xr[:CHUNK - S5_SEG]], axis=0)
        xpi = jnp.concatenate([pi, xi[:CHUNK - S5_SEG]], axis=0)
        dar_ref[0] += jnp.sum((xpr * gr + xpi * gi).reshape(S5_STEPS, S5_SEG, S5_TS), axis=0)
        dai_ref[0] += jnp.sum((xpr * gi - xpi * gr).reshape(S5_STEPS, S5_SEG, S5_TS), axis=0)
        dbr_ref[0] += _dot(u, gr, TN)
        dbi_ref[0] += _dot(u, gi, TN)
        dd_ref[0] += jnp.sum((dy * u).reshape(S5_STEPS, S5_SEG, S5_TU), axis=0)
        du = dy * d_ref[...] + _dot(gr, bre_ref[0], NT) + _dot(gi, bim_ref[0], NT)
        du_ref[...] = _seg_unperm(du).astype(BF16)

    vec = pl.BlockSpec((1, 1, S5_TS), lambda t, n: (t, 0, 0))
    acc_b = pl.BlockSpec((1, S5_TU, S5_TS), lambda t, n: (t, 0, 0))
    acc_c = pl.BlockSpec((1, S5_TS, S5_TU), lambda t, n: (t, 0, 0))
    acc_a = pl.BlockSpec((1, 8, S5_TS), lambda t, n: (t, 0, 0))
    ent = pl.BlockSpec((1, 1, 8, S5_TS), lambda t, n: (t, rn(n), 0, 0))
    return pl.pallas_call(
        body, name="s5_bwd", grid=(S5_NT, nc),
        in_specs=[pl.BlockSpec((CHUNK, S5_TU), lambda t, n: (rn(n), ub + t)),
                  pl.BlockSpec((CHUNK, S5_TU), lambda t, n: (rn(n), t)), vec, vec,
                  acc_b, acc_b, acc_c, acc_c, pl.BlockSpec((1, S5_TU), lambda t, n: (0, t)), ent, ent],
        out_specs=[pl.BlockSpec((CHUNK, S5_TU), lambda t, n: (rn(n), t)), acc_b, acc_b, acc_c, acc_c, acc_a, acc_a,
                   pl.BlockSpec((1, 8, S5_TU), lambda t, n: (t, 0, 0))],
        out_shape=[jax.ShapeDtypeStruct((rows, S5_W), BF16),
                   jax.ShapeDtypeStruct((S5_NT, S5_TU, S5_TS), F32), jax.ShapeDtypeStruct((S5_NT, S5_TU, S5_TS), F32),
                   jax.ShapeDtypeStruct((S5_NT, S5_TS, S5_TU), F32), jax.ShapeDtypeStruct((S5_NT, S5_TS, S5_TU), F32),
                   jax.ShapeDtypeStruct((S5_NT, 8, S5_TS), F32), jax.ShapeDtypeStruct((S5_NT, 8, S5_TS), F32),
                   jax.ShapeDtypeStruct((S5_NT, 8, S5_TU), F32)],
        scratch_shapes=[pltpu.VMEM((CHUNK, S5_TS), F32) for _ in range(2)]
        + [pltpu.VMEM((S5_SEG, S5_TS), F32) for _ in range(8)],
        compiler_params=pltpu.CompilerParams(dimension_semantics=("parallel", "arbitrary")),
    )(proj, dy,ab_re.reshape(S5_NT, 1, S5_TS), ab_im.reshape(S5_NT, 1, S5_TS), bre, bim, cre, cim, dvec, er, ei)


def _s5_gate_bwd(dmix, g, t, proj):
    rows = g.shape[0]
    tm = _row_tile(rows, 384)
    ob = RET_W // S5_W
    zb = (2 * RET_QK + 2 * RET_W + S5_W) // S5_W

    def body(do_ref, g_ref, t_ref, z_ref, dz_ref, dt_ref, dg_ref):
        do = do_ref[...]
        gv = g_ref[...].astype(F32)
        z = z_ref[...]
        st = _sigmoid(t_ref[...])
        sg = _sigmoid(z)
        os5 = gv * st
        dz_ref[...] = (do * os5 * sg * (1.0 + z * (1.0 - sg))).astype(BF16)
        dos = do * z * sg
        dt_ref[...] = (dos * gv * st * (1.0 - st)).astype(BF16)
        dg_ref[...] = dos * st

    blk = pl.BlockSpec((tm, S5_W), lambda i: (i, 0))
    return pl.pallas_call(
        body, name="s5_gate_bwd", grid=(rows // tm,),
        in_specs=[pl.BlockSpec((tm, S5_W), lambda i: (i, ob)), blk, blk,
                  pl.BlockSpec((tm, S5_W), lambda i: (i, zb))],
        out_specs=[blk, blk, blk],
        out_shape=[jax.ShapeDtypeStruct((rows, S5_W), BF16), jax.ShapeDtypeStruct((rows, S5_W), BF16),
                   jax.ShapeDtypeStruct((rows, S5_W), F32)],
    )(dmix, g, t, proj)


def _split3(x):
    hi = x.astype(BF16)
    r = x - hi.astype(F32)
    mid = r.astype(BF16)
    lo = (r - mid.astype(F32)).astype(BF16)
    return hi, mid, lo


def _tri_sum(x, upper):
    i = lax.broadcasted_iota(jnp.int32, (CHUNK, CHUNK), 0)
    j = lax.broadcasted_iota(jnp.int32, (CHUNK, CHUNK), 1)
    tri = jnp.where((j >= i) if upper else (j <= i), 1.0, 0.0).astype(BF16)
    hi, mid, lo = _split3(x)
    return _dot(tri, lo) + _dot(tri, mid) + _dot(tri, hi)


def _gla_log_decay(gl, wg, bg, n):
    logit = _dot(gl, wg) + bg
    la = (jnp.minimum(logit, 0.0) - jnp.log(1.0 + jnp.exp(-jnp.abs(logit)))) * (1.0 / GLA_TAU)
    row = lax.broadcasted_iota(jnp.int32, (CHUNK, 1), 0)
    live = jnp.logical_or(n > 0, row >= PAD)
    return logit, jnp.where(live, la, 0.0), live


def _gla_in_specs(rev, nc):
    def cn(n):
        return (nc - 1 - n) if rev else n
    kb = GLA_QK // GLA_DK
    vb = 2 * GLA_QK // GLA_DV
    zb = (2 * GLA_QK + GLA_W) // GLA_DV
    gb = (2 * GLA_QK + 2 * GLA_W) // 128
    return [
        pl.BlockSpec((CHUNK, GLA_DK), lambda h, n: (cn(n), h)),
        pl.BlockSpec((CHUNK, GLA_DK), lambda h, n: (cn(n), kb + h)),
        pl.BlockSpec((CHUNK, GLA_DV), lambda h, n: (cn(n), vb + h)),
        pl.BlockSpec((CHUNK, GLA_DV), lambda h, n: (cn(n), zb + h)),
        pl.BlockSpec((CHUNK, 128), lambda h, n: (cn(n), gb)),
        pl.BlockSpec((128, GLA_DK), lambda h, n: (0, h)),
        pl.BlockSpec((1, GLA_DK), lambda h, n: (0, h)),
        pl.BlockSpec((1, GLA_DV), lambda h, n: (0, h)),
    ]


def _gla_fwd(proj, wgate, bgate, normw):
    rows = proj.shape[0]
    nc = rows // CHUNK

    def body(q_ref, k_ref, v_ref, z_ref, gl_ref, wg_ref, bg_ref, w_ref, o_ref, oc_ref, st_ref, s_scr, o_scr, b_scr):
        n = pl.program_id(1)

        @pl.when(n == 0)
        def _():
            s_scr[...] = jnp.zeros_like(s_scr)

        q = q_ref[...] * (GLA_DK ** -0.5)
        k = k_ref[...]
        v = v_ref[...]
        vb = v.astype(BF16)
        _, la, _ = _gla_log_decay(gl_ref[...], wg_ref[...], bg_ref[...], n)
        b = _tri_sum(la, False)
        b_scr[...] = b
        b_last = b_scr[CHUNK - 1:CHUNK, :]
        st = s_scr[...]
        st_ref[0, 0] = st
        s_scr[...] = st * jnp.exp(b_last) + _dot(v, k * jnp.exp(b_last - b), TN)
        rowc = lax.broadcasted_iota(jnp.int32, (CHUNK, 1), 0)
        rows16 = lax.broadcasted_iota(jnp.int32, (SUB, 1), 0)
        a_tot = jnp.zeros((CHUNK, CHUNK), F32)
        for s in range(1, NSUB):
            lo = s * SUB
            bref = b_scr[lo - 1:lo, :]
            in_s = jnp.logical_and(rowc >= lo, rowc < lo + SUB)
            qh = q * jnp.exp(jnp.where(in_s, b - bref, -1e30))
            kh = k * jnp.exp(jnp.where(rowc < lo, bref - b, -1e30))
            a_tot = a_tot + _dot(qh, kh, NT)
        o_scr[...] = _dot(q * jnp.exp(b), st, NT) + _dot(a_tot, vb)
        for s in range(NSUB):
            lo = s * SUB
            qs, bs = q[lo:lo + SUB], b[lo:lo + SUB]
            acc = jnp.zeros((SUB, GLA_DV), F32)
            for j in range(SUB):
                r = lo + j
                e = jnp.exp(jnp.where(rows16 >= j, bs - b_scr[r:r + 1, :], -1e30))
                col = jnp.sum(qs * k_ref[r:r + 1, :] * e, axis=1, keepdims=True)
                acc = acc + col * v_ref[r:r + 1, :]
            o_scr[lo:lo + SUB, :] += acc
        o = o_scr[...]
        o_ref[...] = o
        oc_ref[...] = _gate_fwd(o, z_ref[...], w_ref[...]).astype(BF16)

    return pl.pallas_call(
        body, name="gla_fwd", grid=(GLA_HEADS, nc),
        in_specs=_gla_in_specs(False, nc),
        out_specs=[pl.BlockSpec((CHUNK, GLA_DV), lambda h, n: (n, h)),
                   pl.BlockSpec((CHUNK, GLA_DV), lambda h, n: (n, h)),
                   pl.BlockSpec((1, 1, GLA_DV, GLA_DK), lambda h, n: (h, n, 0, 0))],
        out_shape=[jax.ShapeDtypeStruct((rows, GLA_W), F32), jax.ShapeDtypeStruct((rows, GLA_W), BF16),
                   jax.ShapeDtypeStruct((GLA_HEADS, nc, GLA_DV, GLA_DK), F32)],
        scratch_shapes=[pltpu.VMEM((GLA_DV, GLA_DK), F32), pltpu.VMEM((CHUNK, GLA_DV), F32),
                        pltpu.VMEM((CHUNK, GLA_DK), F32)],
        compiler_params=pltpu.CompilerParams(dimension_semantics=("parallel", "arbitrary")),
    )(proj, proj, proj, proj, proj, wgate, bgate, normw)


def _gla_bwd(proj, wgate, bgate, normw, o_gla, d_oc, states):
    rows = proj.shape[0]
    nc = rows // CHUNK

    def rn(n):
        return nc - 1 - n

    def body(q_ref, k_ref, v_ref, z_ref, gl_ref, wg_ref, bg_ref, w_ref, o_ref, do_ref, st_ref,
             dq_ref, dk_ref, dv_ref, dz_ref, dl_ref, dw_ref, dbg_ref,
             ds_scr, dq_scr, dk_scr, dv_scr, db_scr, b_scr):
        n = pl.program_id(1)
        cn = rn(n)

        @pl.when(n == 0)
        def _():
            ds_scr[...] = jnp.zeros_like(ds_scr)
            dw_ref[...] = jnp.zeros_like(dw_ref)
            dbg_ref[...] = jnp.zeros_like(dbg_ref)

        q = q_ref[...] * (GLA_DK ** -0.5)
        k = k_ref[...]
        v = v_ref[...]
        vb = v.astype(BF16)
        do, dz, dw = _gate_bwd(do_ref[...], o_ref[...], z_ref[...], w_ref[...])
        dz_ref[...] = dz.astype(BF16)
        dw_ref[0] += dw
        logit, la, live = _gla_log_decay(gl_ref[...], wg_ref[...], bg_ref[...], cn)
        b = _tri_sum(la, False)
        b_scr[...] = b
        b_last = b_scr[CHUNK - 1:CHUNK, :]
        e_last = jnp.exp(b_last)
        st = st_ref[0, 0]
        g1 = ds_scr[...]
        eb = jnp.exp(b)
        qe = q * eb
        dqe = _dot(do, st)
        dq_scr[...] = dqe * eb
        db_scr[...] = dqe * qe
        ekb = jnp.exp(b_last - b)
        kdec = k * ekb
        dkdec = _dot(v, g1)
        dv_scr[...] = _dot(kdec, g1, NT)
        dk_scr[...] = dkdec * ekb
        wk = dkdec * kdec
        db_scr[...] -= wk
        dbl = jnp.sum(wk, axis=0, keepdims=True) + jnp.sum(g1 * st, axis=0, keepdims=True) * e_last
        ds_scr[...] = g1 * e_last + _dot(do, qe, TN)
        rowc = lax.broadcasted_iota(jnp.int32, (CHUNK, 1), 0)
        rows16 = lax.broadcasted_iota(jnp.int32, (SUB, 1), 0)
        da_full = _dot(do, vb, NT)
        a_tot = jnp.zeros((CHUNK, CHUNK), F32)
        for s in range(1, NSUB):
            lo = s * SUB
            bref = b_scr[lo - 1:lo, :]
            in_s = jnp.logical_and(rowc >= lo, rowc < lo + SUB)
            eq = jnp.exp(jnp.where(in_s, b - bref, -1e30))
            ek = jnp.exp(jnp.where(rowc < lo, bref - b, -1e30))
            qh = q * eq
            kh = k * ek
            a_tot = a_tot + _dot(qh, kh, NT)
            da = jnp.where(in_s, da_full, 0.0)
            dqh = _dot(da, kh)
            dkh = _dot(da, qh, TN)
            tq = dqh * qh
            tk = dkh * kh
            dq_scr[...] += dqh * eq
            dk_scr[...] += dkh * ek
            db_scr[...] += tq - tk
            db_scr[lo - 1:lo, :] += jnp.sum(tk, axis=0, keepdims=True) - jnp.sum(tq, axis=0, keepdims=True)
        dv_scr[...] += _dot(a_tot, do, TN)
        for s in range(NSUB):
            lo = s * SUB
            qs, bs = q[lo:lo + SUB], b[lo:lo + SUB]
            dos = do[lo:lo + SUB]
            dqs = jnp.zeros((SUB, GLA_DK), F32)
            dks = jnp.zeros((SUB, GLA_DK), F32)
            dbs = jnp.zeros((SUB, GLA_DK), F32)
            dvs = jnp.zeros((SUB, GLA_DV), F32)
            for j in range(SUB):
                pick = rows16 == j
                r = lo + j
                kj, vj, bj = k_ref[r:r + 1, :], v_ref[r:r + 1, :], b_scr[r:r + 1, :]
                e = jnp.exp(jnp.where(rows16 >= j, bs - bj, -1e30))
                qe_j = qs * e
                col = jnp.sum(qe_j * kj, axis=1, keepdims=True)
                dcol = jnp.sum(dos * vj, axis=1, keepdims=True)
                dvs = dvs + jnp.where(pick, jnp.sum(col * dos, axis=0, keepdims=True), 0.0)
                m = dcol * e
                dqs = dqs + m * kj
                mq = m * qs
                dks = dks + jnp.where(pick, jnp.sum(mq, axis=0, keepdims=True), 0.0)
                t = mq * kj
                dbs = dbs + t - jnp.where(pick, jnp.sum(t, axis=0, keepdims=True), 0.0)
            dq_scr[lo:lo + SUB, :] += dqs
            dk_scr[lo:lo + SUB, :] += dks
            dv_scr[lo:lo + SUB, :] += dvs
            db_scr[lo:lo + SUB, :] += dbs
        db_scr[CHUNK - 1:CHUNK, :] += dbl
        dla = _tri_sum(db_scr[...], True)
        dlogit = jnp.where(live, dla * (1.0 / GLA_TAU) * _sigmoid(-logit), 0.0)
        dl_ref[...] = dlogit
        dbg_ref[0] += jnp.sum(dlogit, axis=0, keepdims=True)
        dq_ref[...] = (dq_scr[...] * (GLA_DK ** -0.5)).astype(BF16)
        dk_ref[...] = dk_scr[...].astype(BF16)
        dv_ref[...] = dv_scr[...].astype(BF16)

    in_specs = _gla_in_specs(True, nc) + [
        pl.BlockSpec((CHUNK, GLA_DV), lambda h, n: (rn(n), h)),
        pl.BlockSpec((CHUNK, GLA_DV), lambda h, n: (rn(n), h)),
        pl.BlockSpec((1, 1, GLA_DV, GLA_DK), lambda h, n: (h, rn(n), 0, 0)),
    ]
    return pl.pallas_call(
        body, name="gla_bwd", grid=(GLA_HEADS, nc),
        in_specs=in_specs,
        out_specs=[pl.BlockSpec((CHUNK, GLA_DK), lambda h, n: (rn(n), h)),
                   pl.BlockSpec((CHUNK, GLA_DK), lambda h, n: (rn(n), h)),
                   pl.BlockSpec((CHUNK, GLA_DV), lambda h, n: (rn(n), h)),
                   pl.BlockSpec((CHUNK, GLA_DV), lambda h, n: (rn(n), h)),
                   pl.BlockSpec((CHUNK, GLA_DK), lambda h, n: (rn(n), h)),
                   pl.BlockSpec((1, 1, GLA_DV), lambda h, n: (h, 0, 0)),
                   pl.BlockSpec((1, 1, GLA_DK), lambda h, n: (h, 0, 0))],
        out_shape=[jax.ShapeDtypeStruct((rows, GLA_QK), BF16), jax.ShapeDtypeStruct((rows, GLA_QK), BF16),
                   jax.ShapeDtypeStruct((rows, GLA_W), BF16), jax.ShapeDtypeStruct((rows, GLA_W), BF16),
                   jax.ShapeDtypeStruct((rows, GLA_QK), F32),
                   jax.ShapeDtypeStruct((GLA_HEADS, 1, GLA_DV), F32),
                   jax.ShapeDtypeStruct((GLA_HEADS, 1, GLA_DK), F32)],
        scratch_shapes=[pltpu.VMEM((GLA_DV, GLA_DK), F32), pltpu.VMEM((CHUNK, GLA_DK), F32),
                        pltpu.VMEM((CHUNK, GLA_DK), F32), pltpu.VMEM((CHUNK, GLA_DV), F32),
                        pltpu.VMEM((CHUNK, GLA_DK), F32), pltpu.VMEM((CHUNK, GLA_DK), F32)],
        compiler_params=pltpu.CompilerParams(dimension_semantics=("parallel", "arbitrary")),
    )(proj, proj, proj, proj, proj, wgate, bgate, normw, o_gla, d_oc, states)


def _adamw(name, w, g, m, v):
    rows, cols = w.shape
    tm = rows
    for cand in (256, 128, 64, 32, 16, 8):
        if rows % cand == 0:
            tm = cand
            break
    c1 = 1.0 - ADAM_B1 ** ADAM_STEP
    c2 = 1.0 - ADAM_B2 ** ADAM_STEP

    def body(w_ref, g_ref, m_ref, v_ref, d_ref, nm_ref, nv_ref):
        gv = g_ref[...]
        nm = ADAM_B1 * m_ref[...] + (1.0 - ADAM_B1) * gv
        nv = ADAM_B2 * v_ref[...] + (1.0 - ADAM_B2) * (gv * gv)
        nm_ref[...] = nm
        nv_ref[...] = nv
        d_ref[...] = -ADAM_LR * ((nm / c1) / (jnp.sqrt(nv / c2) + ADAM_EPS) + ADAM_WD * w_ref[...])

    blk = pl.BlockSpec((tm, cols), lambda i: (i, 0))
    return pl.pallas_call(
        body, name=name, grid=(rows // tm,),
        in_specs=[blk] * 4, out_specs=[blk] * 3,
        out_shape=[jax.ShapeDtypeStruct((rows, cols), F32)] * 3,
    )(w, g, m, v)


def _place():
    x, y, c = lax.axis_index("x"), lax.axis_index("y"), lax.axis_index("c")
    chips = [(1 - x, y), (x, 1 - y), (1 - x, 1 - y)]
    return x, y, c, chips


ANY = pl.BlockSpec(memory_space=pl.ANY)


def _gathered_struct(shape, dtype, kind):
    r, cc = shape
    if kind == "row":
        return jax.ShapeDtypeStruct((N_SHARD * r, cc), dtype)
    if kind == "col":
        return jax.ShapeDtypeStruct((r, N_SHARD * cc), dtype)
    return jax.ShapeDtypeStruct((N_SHARD, r, cc), dtype)


def _cast_place(name, w, kind, mine_arr, dtype):
    r, cc = w.shape
    tr = r
    for cand in (256, 128, 64, 32, 16):
        if r % cand == 0:
            tr = cand
            break
    nb = r // tr
    if kind == "row":
        o_spec = pl.BlockSpec((tr, cc), lambda i, m: (m[0] * nb + i, 0))
    elif kind == "col":
        o_spec = pl.BlockSpec((tr, cc), lambda i, m: (i, m[0]))
    else:
        o_spec = pl.BlockSpec((None, tr, cc), lambda i, m: (m[0], i, 0))

    def body(m_ref, w_ref, o_ref):
        o_ref[...] = w_ref[...].astype(o_ref.dtype)

    return pl.pallas_call(
        body, name=name,
        grid_spec=pltpu.PrefetchScalarGridSpec(
            num_scalar_prefetch=1, grid=(nb,),
            in_specs=[pl.BlockSpec((tr, cc), lambda i, m: (i, 0))], out_specs=o_spec),
        out_shape=_gathered_struct(w.shape, dtype, kind),
    )(mine_arr, w)


def _gather_weights(bufs, kinds, shard_shapes):
    n_arr = len(bufs)

    def body(*refs):
        outs = refs[n_arr:2 * n_arr]
        send_sems, recv_sems = refs[2 * n_arr:]
        x, y, c, chips = _place()
        mine = 2 * x + y
        sibling = (x, y, 1 - c)

        def window(i, shard, half):
            return _shard_window(outs[i], kinds[i], shard_shapes[i], shard, half)

        def copy(i, slot, win, to):
            return pltpu.make_async_remote_copy(
                src_ref=win, dst_ref=win, send_sem=send_sems.at[i, slot], recv_sem=recv_sems.at[i, slot],
                device_id=to, device_id_type=MESH)

        first = []
        for i in range(n_arr):
            for j, chip in enumerate(chips):
                cp = copy(i, j, window(i, mine, c), (*chip, c))
                cp.start()
                first.append(cp)
        passed = []
        for i in range(n_arr):
            for j, chip in enumerate(chips):
                theirs = 2 * chip[0] + chip[1]
                copy(i, j, window(i, theirs, c), (*chip, c)).wait_recv()
                cp = copy(i, 3 + j, window(i, theirs, c), sibling)
                cp.start()
                passed.append(cp)
        for i in range(n_arr):
            for j, chip in enumerate(chips):
                theirs = 2 * chip[0] + chip[1]
                copy(i, 3 + j, window(i, theirs, 1 - c), sibling).wait_recv()
        for cp in first + passed:
            cp.wait_send()

    return pl.pallas_call(
        body, name="gather_weights",
        in_specs=[ANY] * n_arr, out_specs=[ANY] * n_arr,
        out_shape=[jax.ShapeDtypeStruct(a.shape, a.dtype) for a in bufs],
        input_output_aliases={i: i for i in range(n_arr)},
        scratch_shapes=[pltpu.SemaphoreType.DMA((n_arr, 6)), pltpu.SemaphoreType.DMA((n_arr, 6))],
        compiler_params=pltpu.CompilerParams(has_side_effects=True),
    )(*bufs)


def _allreduce_small(buf):
    rows, cols = buf.shape

    def body(in_ref, out_ref, sib_ref, pair_ref, far_ref, send_sems, recv_sems):
        x, y, c, chips = _place()
        sibling = (x, y, 1 - c)
        to_sib = pltpu.make_async_remote_copy(
            src_ref=in_ref, dst_ref=sib_ref, send_sem=send_sems.at[0], recv_sem=recv_sems.at[0],
            device_id=sibling, device_id_type=MESH)
        to_sib.start()
        to_sib.wait()
        pair_ref[...] = in_ref[...] + sib_ref[...]
        far = [pltpu.make_async_remote_copy(
            src_ref=pair_ref, dst_ref=far_ref.at[j], send_sem=send_sems.at[1 + j], recv_sem=recv_sems.at[1 + j],
            device_id=(*chip, c), device_id_type=MESH) for j, chip in enumerate(chips)]
        for cp in far:
            cp.start()
        for cp in far:
            cp.wait()
        out_ref[...] = (pair_ref[...] + far_ref[1]) + (far_ref[0] + far_ref[2])

    vm = pl.BlockSpec(memory_space=pltpu.VMEM)
    return pl.pallas_call(
        body, name="allreduce_small",
        in_specs=[vm], out_specs=vm,
        out_shape=jax.ShapeDtypeStruct((rows, cols), F32),
        scratch_shapes=[pltpu.VMEM((rows, cols), F32), pltpu.VMEM((rows, cols), F32),
                        pltpu.VMEM((3, rows, cols), F32),
                        pltpu.SemaphoreType.DMA((4,)), pltpu.SemaphoreType.DMA((4,))],
        compiler_params=pltpu.CompilerParams(has_side_effects=True),
    )(buf)


def _shard_window(ref, kind, shard_shape, shard, half):
    r, cc = shard_shape
    hr = r // 2
    if kind == "row":
        return ref.at[pl.ds(_mo(shard * r + half * hr, 8), hr), :]
    if kind == "col":
        return ref.at[pl.ds(_mo(half * hr, 8), hr), pl.ds(_mo(shard * cc, 128), cc)]
    return ref.at[shard, pl.ds(_mo(half * hr, 8), hr), :]


HBM = pl.BlockSpec(memory_space=pltpu.HBM)
SEM = pl.BlockSpec(memory_space=pltpu.SEMAPHORE)
DATAFLOW = pltpu.SideEffectType.DATAFLOW_SIDE_EFFECTING


def _in_hbm(a):
    return pltpu.with_memory_space_constraint(a, pltpu.HBM)


def _empty_hbm(shape, dtype):
    return _in_hbm(lax.empty(shape, dtype))


def _copies_start(name, bufs, n_copies, plan, carry):
    nb = len(bufs)

    def body(*refs):
        send_sems, recv_sems = refs[nb + 1], refs[nb + 2]
        for k, (src, dst, to) in enumerate(plan(refs[:nb])):
            pltpu.make_async_remote_copy(src_ref=src, dst_ref=dst, send_sem=send_sems.at[k], recv_sem=recv_sems.at[k],
                                         device_id=to, device_id_type=MESH).start()

    passed = list(bufs) + [carry]
    out = pl.pallas_call(
        body, name=name,
        in_specs=[HBM] * (nb + 1), out_specs=[SEM, SEM] + [HBM] * (nb + 1),
        out_shape=[pltpu.SemaphoreType.DMA((n_copies,)), pltpu.SemaphoreType.DMA((n_copies,))]
        + [pltpu.HBM(a.shape, a.dtype) for a in passed],
        input_output_aliases={i: 2 + i for i in range(nb + 1)},
        compiler_params=pltpu.CompilerParams(has_side_effects=DATAFLOW),
    )(*[_in_hbm(a) for a in passed])
    return out[0], out[1], list(out[2:2 + nb]), out[2 + nb]


def _copies_wait(name, send_sems, recv_sems, bufs, plan, after):
    nb = len(bufs)
    after = list(after) if isinstance(after, (list, tuple)) else [after]

    def body(*refs):
        send, recv = refs[nb], refs[nb + 1]
        for k, (src, dst, to) in enumerate(plan(refs[:nb])):
            cp = pltpu.make_async_remote_copy(src_ref=src, dst_ref=dst, send_sem=send.at[k], recv_sem=recv.at[k],
                                              device_id=to, device_id_type=MESH)
            cp.wait_send()
            cp.wait_recv()

    out = pl.pallas_call(
        body, name=name,
        in_specs=[HBM] * nb + [SEM, SEM] + [ANY] * len(after), out_specs=[HBM] * nb,
        out_shape=[pltpu.HBM(a.shape, a.dtype) for a in bufs],
        input_output_aliases={i: i for i in range(nb)},
        compiler_params=pltpu.CompilerParams(has_side_effects=DATAFLOW),
    )(*bufs, send_sems, recv_sems, *after)
    return list(out)


def _gather_ici_plan(shard_shapes, kinds):
    n_arr = len(kinds)

    def plan(refs):
        x, y, c, chips = _place()
        out = []
        for i in range(n_arr):
            w = _shard_window(refs[i], kinds[i], shard_shapes[i], 2 * x + y, c)
            out += [(w, w, (*chip, c)) for chip in chips]
        return out

    return plan


def _gather_d2d_plan(shard_shapes, kinds):
    n_arr = len(kinds)

    def plan(refs):
        x, y, c, chips = _place()
        out = []
        for i in range(n_arr):
            for chip in chips:
                w = _shard_window(refs[i], kinds[i], shard_shapes[i], 2 * chip[0] + chip[1], c)
                out.append((w, w, (x, y, 1 - c)))
        return out

    return plan


def _rs_pair_plan(kinds, shard_shapes):
    n_arr = len(kinds)

    def plan(refs):
        x, y, c, _ = _place()
        out = []
        for i in range(n_arr):
            for s in range(N_SHARD):
                out.append((_shard_window(refs[i], kinds[i], shard_shapes[i], s, 1 - c), refs[n_arr + i].at[s],
                            (x, y, 1 - c)))
        return out

    return plan


def _rs_chip_plan(n_arr):
    def plan(refs):
        x, y, c, chips = _place()
        out = []
        for i in range(n_arr):
            for j, chip in enumerate(chips):
                out.append((refs[i].at[2 * chip[0] + chip[1]], refs[n_arr + i].at[j], (*chip, c)))
        return out

    return plan


def _rs_pair_add(name, grad, got, kind, shard_shape, c):
    r, cc = shard_shape
    hr = r // 2
    tr = hr
    for cand in (256, 128, 64, 32, 16):
        if hr % cand == 0:
            tr = cand
            break
    nb = hr // tr

    if kind == "row":
        g_spec = pl.BlockSpec((tr, cc), lambda s, i, cr: (s * 2 * nb + cr[0] * nb + i, 0))
    elif kind == "col":
        g_spec = pl.BlockSpec((tr, cc), lambda s, i, cr: (cr[0] * nb + i, s))
    else:
        g_spec = pl.BlockSpec((None, tr, cc), lambda s, i, cr: (s, cr[0] * nb + i, 0))
    t_spec = pl.BlockSpec((None, tr, cc), lambda s, i, cr: (s, i, 0))

    def body(c_ref, g_ref, t_ref, p_ref, pb_ref):
        p = g_ref[...] + t_ref[...]
        p_ref[...] = p
        pb_ref[...] = p.astype(BF16)

    return pl.pallas_call(
        body, name=name,
        grid_spec=pltpu.PrefetchScalarGridSpec(
            num_scalar_prefetch=1, grid=(N_SHARD, nb),
            in_specs=[g_spec, t_spec], out_specs=[t_spec, t_spec]),
        out_shape=[jax.ShapeDtypeStruct((N_SHARD, hr, cc), F32), jax.ShapeDtypeStruct((N_SHARD, hr, cc), BF16)],
    )(c, grad, got)


def _rs_chip_add(name, pair_f32, got, shard_shape, mine_c):
    r, cc = shard_shape
    hr = r // 2
    tr = hr
    for cand in (256, 128, 64, 32, 16):
        if hr % cand == 0:
            tr = cand
            break
    nb = hr // tr

    def body(mc_ref, p_ref, t0_ref, t1_ref, t2_ref, o_ref):
        o_ref[...] = (p_ref[...] + t1_ref[...].astype(F32)) + (t0_ref[...].astype(F32) + t2_ref[...].astype(F32))

    def far(j):
        return pl.BlockSpec((None, tr, cc), lambda i, mc: (j, i, 0))

    return pl.pallas_call(
        body, name=name,
        grid_spec=pltpu.PrefetchScalarGridSpec(
            num_scalar_prefetch=1, grid=(nb,),
            in_specs=[pl.BlockSpec((None, tr, cc), lambda i, mc: (mc[0], i, 0)), far(0), far(1), far(2)],
            out_specs=pl.BlockSpec((tr, cc), lambda i, mc: (mc[1] * nb + i, 0))),
        out_shape=jax.ShapeDtypeStruct((r, cc), F32),
    )(mine_c, pair_f32, got, got, got)


def _rs_pair_share(name, halves, shard_shapes):
    n_arr = len(halves)

    def body(*refs):
        ins = refs[:n_arr]
        outs = refs[n_arr:2 * n_arr]
        send_sems, recv_sems = refs[2 * n_arr:]
        x, y, c, _ = _place()
        sibling = (x, y, 1 - c)
        cps = []
        for i in range(n_arr):
            hr = shard_shapes[i][0] // 2
            rows = pl.ds(_mo(c * hr, 8), hr)
            cp = pltpu.make_async_remote_copy(
                src_ref=outs[i].at[rows, :], dst_ref=outs[i].at[rows, :],
                send_sem=send_sems.at[i], recv_sem=recv_sems.at[i],
                device_id=sibling, device_id_type=MESH)
            cp.start()
            cps.append(cp)
        for cp in cps:
            cp.wait()

    return pl.pallas_call(
        body, name=name,
        in_specs=[ANY] * n_arr, out_specs=[ANY] * n_arr,
        out_shape=[jax.ShapeDtypeStruct(s, F32) for s in shard_shapes],
        input_output_aliases={i: i for i in range(n_arr)},
        scratch_shapes=[pltpu.SemaphoreType.DMA((n_arr,)), pltpu.SemaphoreType.DMA((n_arr,))],
        compiler_params=pltpu.CompilerParams(has_side_effects=True),
    )(*halves)


def _pack(arrays):
    flat = []
    for a in arrays:
        v = a.reshape(-1).astype(F32)
        flat.append(jnp.pad(v, (0, (-v.shape[0]) % SMALL_COLS)))
    buf = jnp.concatenate(flat).reshape(-1, SMALL_COLS)
    return jnp.pad(buf, ((0, (-buf.shape[0]) % 8), (0, 0)))


def _unpack(buf, shapes):
    out = []
    row = 0
    for s in shapes:
        size = math.prod(s)
        nrow = -(-size // SMALL_COLS)
        out.append(buf[row:row + nrow].reshape(-1)[:size].reshape(s))
        row += nrow
    return out


def kernel(x, meta, norm_ab_w, w_in_ab, ret_norm_w, s5_lam_re, s5_lam_im, s5_log_dt, s5_b_re, s5_b_im, s5_c_re, s5_c_im, s5_d, s5_w_glu, w_out_ab, norm_c_w, w_in_c, gla_w_gate, gla_b_gate, gla_norm_w, w_out_c, final_norm_w, loss_target, m_meta, m_norm_ab_w, m_w_in_ab, m_ret_norm_w, m_s5_lam_re, m_s5_lam_im, m_s5_log_dt, m_s5_b_re, m_s5_b_im, m_s5_c_re, m_s5_c_im, m_s5_d, m_s5_w_glu, m_w_out_ab, m_norm_c_w, m_w_in_c, m_gla_w_gate, m_gla_b_gate, m_gla_norm_w, m_w_out_c, m_final_norm_w, v_meta, v_norm_ab_w, v_w_in_ab, v_ret_norm_w, v_s5_lam_re, v_s5_lam_im, v_s5_log_dt, v_s5_b_re, v_s5_b_im, v_s5_c_re, v_s5_c_im, v_s5_d, v_s5_w_glu, v_w_out_ab, v_norm_c_w, v_w_in_c, v_gla_w_gate, v_gla_b_gate, v_gla_norm_w, v_w_out_c, v_final_norm_w):
    seq = x.shape[1]
    rows = seq + CHUNK
    xi, yi, ci = lax.axis_index("x"), lax.axis_index("y"), lax.axis_index("c")
    mine = 2 * xi + yi
    c_arr = jnp.reshape(ci, (1,)).astype(jnp.int32)
    mine_c = jnp.stack([mine, ci]).astype(jnp.int32)

    mine_arr = jnp.reshape(mine, (1,)).astype(jnp.int32)
    small_shard = _pack([meta, norm_c_w, gla_norm_w, gla_b_gate, gla_w_gate[0]])
    wab, small_all = _gather_weights(
        [_cast_place("place_w_in_ab", w_in_ab[0], "col", mine_arr, BF16),
         _cast_place("place_small", small_shard, "stack", mine_arr, F32)],
        ["col", "stack"], [w_in_ab.shape[1:], small_shard.shape])
    late = [("w_out_ab", w_out_ab[0]), ("w_in_c", w_in_c[0]), ("w_out_c", w_out_c[0]), ("w_glu", s5_w_glu[0])]
    late_kinds = ["row", "stack", "row", "row"]
    late_shapes = [a.shape for _, a in late]
    ici_plan = _gather_ici_plan(late_shapes, late_kinds)
    d2d_plan = _gather_d2d_plan(late_shapes, late_kinds)
    n_late = 3 * len(late)
    g_bufs = [_cast_place("place_" + nm, a, kd, mine_arr, BF16) for (nm, a), kd in zip(late, late_kinds)]
    g_send, g_recv, g_bufs, wab = _copies_start("gather_late_ici_start", g_bufs, n_late, ici_plan, wab)
    q4 = D_MODEL // N_SHARD
    g4 = GLA_QK // N_SHARD
    parts = [_unpack(small_all[j], [(N_META, q4), (1, q4), (1, q4), (1, g4), (GLA_RANK, g4)]) for j in range(N_SHARD)]
    meta_f, norm_c_f, gla_norm_f, bgate_f, wgate_f = [jnp.concatenate([p[i] for p in parts], axis=1) for i in range(5)]
    wgate_pad = jnp.pad(wgate_f, ((0, 128 - GLA_RANK), (0, 0)))

    h0 = jnp.concatenate([jnp.zeros((PAD, D_MODEL), F32), meta_f, x[0]], axis=0)
    cosf, sinf = _rope_tables(rows)
    rtab = _ret_tables()
    ab_re, ab_im, bb_re, bb_im = _s5_discretize(s5_lam_re[0], s5_lam_im[0], s5_log_dt[0], s5_b_re[0], s5_b_im[0])
    ab = (ab_re, ab_im)
    bd_b = (_bdiag_in(bb_re), _bdiag_in(bb_im))
    bd_c = (_bdiag_out(s5_c_re[0]), _bdiag_out(s5_c_im[0]))

    tm = _row_tile(rows, 1408)
    tmk = _row_tile(rows, 1408)
    hn0 = _rms_fwd("norm_ab", h0, norm_ab_w)
    proj0 = _matmul("in_proj_ab", hn0, wab, NN, rows, IN_AB, D_MODEL, tm=tm, tn=512, tk=D_MODEL)
    o_ret, o_a, ret_states = _ret_fwd(proj0, cosf, sinf, rtab, ret_norm_w)
    g_bufs = _copies_wait("gather_late_ici_wait", g_send, g_recv, g_bufs, ici_plan, o_a)
    g_send, g_recv, g_bufs, proj0 = _copies_start("gather_late_d2d_start", g_bufs, n_late, d2d_plan, proj0)
    y_s5, g_s5, s5_er, s5_ei = _s5_fwd(proj0, ab, bd_b, bd_c, s5_d)
    wout_ab, wc_st, wout_c, wglu = _copies_wait("gather_late_d2d_wait", g_send, g_recv, g_bufs, d2d_plan, g_s5)
    wc = jnp.concatenate([wc_st[j] for j in range(N_SHARD)] + [jnp.zeros((D_MODEL, IN_C_PAD - IN_C), BF16)], axis=1)
    zb_blk = (2 * RET_QK + 2 * RET_W + S5_W) // 512

    def glu_out(acc, gv, z):
        return gv.astype(F32) * _sigmoid(acc) * (z * _sigmoid(z))

    t_glu = _matmul("glu", g_s5, wglu, NN, rows, S5_W, S5_W, tm=tm, tn=512, tk=S5_W)
    o_b = _matmul("glu_out", g_s5, wglu, NN, rows, S5_W, S5_W, tm=tm, tn=512, tk=S5_W, out_dtype=BF16,
                  extras=[(g_s5, (tm, 512), lambda i, j, kk: (i, j)),
                          (proj0, (tm, 512), lambda i, j, kk: (i, zb_blk + j))],
                  epilogue=glu_out)
    mix = jnp.concatenate([o_a, o_b], axis=1)
    h1 = _matmul("out_proj_ab", mix, wout_ab, NN, rows, D_MODEL, OUT_AB, tm=tm, tn=512, tk=1024,
                 extras=[(h0, (tm, 512), lambda i, j, kk: (i, j))], epilogue=lambda acc, r: acc + r)

    hn1 = _rms_fwd("norm_c", h1, norm_c_f)
    proj1 = _matmul("in_proj_c", hn1, wc, NN, rows, IN_C_PAD, D_MODEL, tm=tm, tn=896, tk=D_MODEL)
    o_gla, o_c, gla_states = _gla_fwd(proj1, wgate_pad, bgate_f, gla_norm_f)
    h2 = _matmul("out_proj_c", o_c, wout_c, NN, rows, D_MODEL, GLA_W, tm=tm, tn=512, tk=GLA_W,
                 extras=[(h1, (tm, 512), lambda i, j, kk: (i, j))], epilogue=lambda acc, r: acc + r)
    loss_dev, dh2, d_final = _final_loss(h2, final_norm_w.reshape(1, D_MODEL), loss_target[0])

    g_wout_c = _matmul("d_w_out_c", o_c, dh2, TN, GLA_W, D_MODEL, rows, tm=1024, tn=1024, tk=tmk)
    d_oc = _matmul("d_o_c", dh2, wout_c, NT, rows, GLA_W, D_MODEL, tm=tm, tn=512, tk=1024)
    dq1, dk1, dv1, dz1, dlogit, d_gla_norm, d_bgate = _gla_bwd(proj1, wgate_pad, bgate_f, gla_norm_f, o_gla, d_oc, gla_states)
    gl_blk = (2 * GLA_QK + 2 * GLA_W) // 128
    dgl = _matmul("d_g_low", dlogit, wgate_pad, NT, rows, 128, GLA_QK, tm=tm, tn=128, tk=GLA_QK, out_dtype=BF16)
    g_wgate = _matmul("d_w_gate", proj1, dlogit, TN, 128, GLA_QK, rows, tm=128, tn=GLA_QK, tk=tmk, a_off=(0, gl_blk))
    dproj1 = jnp.concatenate([dq1, dk1, dv1, dz1, dgl], axis=1)
    g_wc = _matmul("d_w_in_c", hn1, dproj1, TN, D_MODEL, IN_C_PAD, rows, tm=1024, tn=896, tk=tmk)
    dhn1 = _matmul("d_hn1", dproj1, wc, NT, rows, D_MODEL, IN_C_PAD, tm=tm, tn=512, tk=896)
    dh1, d_norm_c = _rms_bwd("norm_c_bwd", dhn1, h1, norm_c_f, dh2)

    g_wout_ab = _matmul("d_w_out_ab", mix, dh1, TN, OUT_AB, D_MODEL, rows, tm=1024, tn=1024, tk=tmk)
    dmix = _matmul("d_mix", dh1, wout_ab, NT, rows, OUT_AB, D_MODEL, tm=tm, tn=512, tk=1024)
    dq0, dk0, dv0, dza, d_ret_norm = _ret_bwd(proj0, cosf, sinf, rtab, ret_norm_w, o_ret, dmix, ret_states)
    dzb, dt_glu, dg_direct = _s5_gate_bwd(dmix, g_s5, t_glu, proj0)
    g_wglu = _matmul("d_w_glu", g_s5, dt_glu, TN, S5_W, S5_W, rows, tm=1024, tn=1024, tk=tmk)
    dy_s5 = _matmul("d_y_s5", dt_glu, wglu, NT, rows, S5_W, S5_W, tm=tm, tn=512, tk=S5_W,
                    extras=[(dg_direct, (tm, 512), lambda i, j, kk: (i, j)),
                            (y_s5, (tm, 512), lambda i, j, kk: (i, j))],
                    epilogue=lambda acc, dg, yv: (acc + dg) * _gelu_grad(yv))
    g_wc_st = jnp.stack([g_wc[:, j * (IN_C // N_SHARD):(j + 1) * (IN_C // N_SHARD)] for j in range(N_SHARD)])
    rs1_names = ["w_out_ab", "w_in_c", "w_out_c", "w_glu"]
    rs1_shapes = [w_out_ab.shape[1:], w_in_c.shape[1:], w_out_c.shape[1:], s5_w_glu.shape[1:]]
    rs1_plan = _rs_pair_plan(late_kinds, rs1_shapes)
    rs1_land = [_empty_hbm((N_SHARD, r // 2, cc), F32) for (r, cc) in rs1_shapes]
    p_send, p_recv, p_bufs, dy_s5 = _copies_start("rs1_pair_start", [g_wout_ab, g_wc_st, g_wout_c, g_wglu] + rs1_land,
                                                  N_SHARD * 4, rs1_plan, dy_s5)
    du, dbr_d, dbi_d, dcr_d, dci_d, dar_p, dai_p, dd_p = _s5_bwd(proj0, dy_s5, ab, bd_b, bd_c, s5_d, (s5_er, s5_ei))
    p_bufs = _copies_wait("rs1_pair_wait", p_send, p_recv, p_bufs, rs1_plan, du)
    rs1_pairs = [_rs_pair_add("rs_pair_add_" + nm, g, t, kd, ss, c_arr)
                 for nm, g, t, kd, ss in zip(rs1_names, p_bufs[:4], p_bufs[4:], late_kinds, rs1_shapes)]
    dproj0 = jnp.concatenate([dq0, dk0, dv0, dza, du, dzb], axis=1)
    rs1_chip_plan = _rs_chip_plan(4)
    rs1_land2 = [_empty_hbm((3, r // 2, cc), BF16) for (r, cc) in rs1_shapes]
    c_send, c_recv, c_bufs, dproj0 = _copies_start("rs1_chip_start", [p[1] for p in rs1_pairs] + rs1_land2, 12,
                                                   rs1_chip_plan, dproj0)
    g_wab = _matmul("d_w_in_ab", hn0, dproj0, TN, D_MODEL, IN_AB, rows, tm=1024, tn=1024, tk=tmk)
    rs2_shapes = [w_in_ab.shape[1:]]
    rs2_plan = _rs_pair_plan(["col"], rs2_shapes)
    rs2_land = [_empty_hbm((N_SHARD, rs2_shapes[0][0] // 2, rs2_shapes[0][1]), F32)]
    q_send, q_recv, q_bufs, dproj0 = _copies_start("rs2_pair_start", [g_wab] + rs2_land, N_SHARD, rs2_plan, dproj0)
    dhn0 = _matmul("d_hn0", dproj0, wab, NT, rows, D_MODEL, IN_AB, tm=tm, tn=512, tk=2048)
    dh0, d_norm_ab = _rms_bwd("norm_ab_bwd", dhn0, h0, norm_ab_w, dh1)
    grad_x = dh0[CHUNK:][None]
    c_bufs = _copies_wait("rs1_chip_wait", c_send, c_recv, c_bufs, rs1_chip_plan, dh0)
    rs1_halves = [_rs_chip_add("rs_chip_add_" + nm, p[0], t, ss, mine_c)
                  for nm, p, t, ss in zip(rs1_names, rs1_pairs, c_bufs[4:], rs1_shapes)]
    g_w_out_ab, g_w_in_c, g_w_out_c, g_w_glu = _rs_pair_share("rs1_pair_share", rs1_halves, rs1_shapes)
    q_bufs = _copies_wait("rs2_pair_wait", q_send, q_recv, q_bufs, rs2_plan, g_w_glu)
    rs2_pair = _rs_pair_add("rs_pair_add_w_in_ab", q_bufs[0], q_bufs[1], "col", rs2_shapes[0], c_arr)
    rs2_chip_plan = _rs_chip_plan(1)
    rs2_land2 = [_empty_hbm((3, rs2_shapes[0][0] // 2, rs2_shapes[0][1]), BF16)]

    d_ab_re = jnp.sum(dar_p, axis=1).reshape(S5_G, S5_P)
    d_ab_im = jnp.sum(dai_p, axis=1).reshape(S5_G, S5_P)
    small_local = [loss_dev, dh0[PAD:CHUNK], d_norm_ab, d_ret_norm.reshape(1, RET_W), d_ab_re, d_ab_im,
                   _bdiag_in_extract(dbr_d), _bdiag_in_extract(dbi_d),
                   _bdiag_out_extract(dcr_d), _bdiag_out_extract(dci_d),
                   jnp.sum(dd_p, axis=1).reshape(1, S5_W), d_norm_c, g_wgate[:GLA_RANK],
                   d_bgate.reshape(1, GLA_QK), d_gla_norm.reshape(1, GLA_W), d_final]
    small_shapes = [a.shape for a in small_local]
    summed_buf = _allreduce_small(_pack(small_local))
    r_send, r_recv, r_bufs, summed_buf = _copies_start("rs2_chip_start", [rs2_pair[1]] + rs2_land2, 3, rs2_chip_plan,
                                                       summed_buf)
    summed = _unpack(summed_buf, small_shapes)
    (loss, g_meta_f, g_norm_ab, g_ret_norm, g_ab_re, g_ab_im, g_bb_re, g_bb_im, g_c_re, g_c_im, g_d,
     g_norm_c_f, g_wgate_f, g_bgate_f, g_gla_norm_f, g_final) = summed
    _, s5_vjp = jax.vjp(_s5_discretize, s5_lam_re[0], s5_lam_im[0], s5_log_dt[0], s5_b_re[0], s5_b_im[0])
    g_lam_re, g_lam_im, g_log_dt, g_b_re, g_b_im = s5_vjp((g_ab_re, g_ab_im, g_bb_re, g_bb_im))

    def take(a, width):
        return lax.dynamic_slice_in_dim(a, mine * width, width, axis=1)

    grads = {
        "meta": take(g_meta_f, q4), "norm_ab_w": g_norm_ab, "ret_norm_w": g_ret_norm,
        "s5_lam_re": g_lam_re[None], "s5_lam_im": g_lam_im[None], "s5_log_dt": g_log_dt[None],
        "s5_b_re": g_b_re[None], "s5_b_im": g_b_im[None], "s5_c_re": g_c_re[None], "s5_c_im": g_c_im[None],
        "s5_d": g_d, "s5_w_glu": g_w_glu[None], "w_out_ab": g_w_out_ab[None], "norm_c_w": take(g_norm_c_f, q4),
        "w_in_c": g_w_in_c[None], "gla_w_gate": take(g_wgate_f, g4)[None], "gla_b_gate": take(g_bgate_f, g4),
        "gla_norm_w": take(g_gla_norm_f, q4), "w_out_c": g_w_out_c[None], "final_norm_w": g_final.reshape(D_MODEL),
    }
    weights = dict(meta=meta, norm_ab_w=norm_ab_w, w_in_ab=w_in_ab, ret_norm_w=ret_norm_w, s5_lam_re=s5_lam_re,
                   s5_lam_im=s5_lam_im, s5_log_dt=s5_log_dt, s5_b_re=s5_b_re, s5_b_im=s5_b_im, s5_c_re=s5_c_re,
                   s5_c_im=s5_c_im, s5_d=s5_d, s5_w_glu=s5_w_glu, w_out_ab=w_out_ab, norm_c_w=norm_c_w,
                   w_in_c=w_in_c, gla_w_gate=gla_w_gate, gla_b_gate=gla_b_gate, gla_norm_w=gla_norm_w,
                   w_out_c=w_out_c, final_norm_w=final_norm_w)
    m_in = dict(meta=m_meta, norm_ab_w=m_norm_ab_w, w_in_ab=m_w_in_ab, ret_norm_w=m_ret_norm_w,
                s5_lam_re=m_s5_lam_re, s5_lam_im=m_s5_lam_im, s5_log_dt=m_s5_log_dt, s5_b_re=m_s5_b_re,
                s5_b_im=m_s5_b_im, s5_c_re=m_s5_c_re, s5_c_im=m_s5_c_im, s5_d=m_s5_d, s5_w_glu=m_s5_w_glu,
                w_out_ab=m_w_out_ab, norm_c_w=m_norm_c_w, w_in_c=m_w_in_c, gla_w_gate=m_gla_w_gate,
                gla_b_gate=m_gla_b_gate, gla_norm_w=m_gla_norm_w, w_out_c=m_w_out_c, final_norm_w=m_final_norm_w)
    v_in = dict(meta=v_meta, norm_ab_w=v_norm_ab_w, w_in_ab=v_w_in_ab, ret_norm_w=v_ret_norm_w,
                s5_lam_re=v_s5_lam_re, s5_lam_im=v_s5_lam_im, s5_log_dt=v_s5_log_dt, s5_b_re=v_s5_b_re,
                s5_b_im=v_s5_b_im, s5_c_re=v_s5_c_re, s5_c_im=v_s5_c_im, s5_d=v_s5_d, s5_w_glu=v_s5_w_glu,
                w_out_ab=v_w_out_ab, norm_c_w=v_norm_c_w, w_in_c=v_w_in_c, gla_w_gate=v_gla_w_gate,
                gla_b_gate=v_gla_b_gate, gla_norm_w=v_gla_norm_w, w_out_c=v_w_out_c, final_norm_w=v_final_norm_w)
    order = list(weights)
    big_names = ["s5_w_glu", "w_out_ab", "w_in_c", "w_out_c", "w_in_ab"]
    small_names = [nm for nm in order if nm not in big_names]
    delta, new_m, new_v = {}, {}, {}

    def big_update(nm):
        shp = weights[nm].shape
        d2, m2, v2 = _adamw("adamw_" + nm, weights[nm][0], grads[nm][0], m_in[nm][0], v_in[nm][0])
        delta[nm], new_m[nm], new_v[nm] = d2.reshape(shp), m2.reshape(shp), v2.reshape(shp)

    for nm in big_names[:-1]:
        big_update(nm)
    sshapes = [weights[nm].shape for nm in small_names]
    d2, m2, v2 = _adamw("adamw_small", _pack([weights[nm] for nm in small_names]),
                        _pack([grads[nm] for nm in small_names]), _pack([m_in[nm] for nm in small_names]),
                        _pack([v_in[nm] for nm in small_names]))
    for nm, dd, mm, vv in zip(small_names, _unpack(d2, sshapes), _unpack(m2, sshapes), _unpack(v2, sshapes)):
        delta[nm], new_m[nm], new_v[nm] = dd, mm, vv
    r_bufs = _copies_wait("rs2_chip_wait", r_send, r_recv, r_bufs, rs2_chip_plan,
                          [v2] + [new_v[nm] for nm in big_names[:-1]])
    rs2_half = _rs_chip_add("rs_chip_add_w_in_ab", rs2_pair[0], r_bufs[1], rs2_shapes[0], mine_c)
    grads["w_in_ab"] = _rs_pair_share("rs2_pair_share", [rs2_half], rs2_shapes)[0][None]
    big_update("w_in_ab")
    grads = {nm: grads[nm].reshape(weights[nm].shape) for nm in order}
    return (loss.reshape(()), grad_x, *[grads[nm] for nm in order], *[delta[nm] for nm in order],
            *[new_m[nm] for nm in order], *[new_v[nm] for nm in order])
```

```python
import functools
import math

import jax
import jax.numpy as jnp
from jax import lax
from jax.experimental import pallas as pl
from jax.experimental.pallas import tpu as pltpu

F32 = jnp.float32
BF16 = jnp.bfloat16
MESH = pl.DeviceIdType.MESH

D_MODEL = 2048
N_META = 16
CHUNK = 128
SUB = 16
NSUB = CHUNK // SUB
PAD = CHUNK - N_META
EPS = 1e-6

RET_HEADS = 8
RET_DK = 128
RET_DV = 256
RET_QK = RET_HEADS * RET_DK
RET_W = RET_HEADS * RET_DV
ROPE_BASE = 10000.0

S5_W = 1024
S5_GH = 16
S5_G = S5_W // S5_GH
S5_P = 64
S5_TG = 8
S5_NT = S5_G // S5_TG
S5_TU = S5_TG * S5_GH
S5_TS = S5_TG * S5_P
S5_TPS = 2
S5_UB = S5_TPS * S5_TU

GLA_HEADS = 4
GLA_DK = 256
GLA_DV = 512
GLA_QK = GLA_HEADS * GLA_DK
GLA_W = GLA_HEADS * GLA_DV
GLA_RANK = 16
GLA_TAU = 16.0

IN_AB = 2 * RET_QK + 2 * RET_W + 2 * S5_W
OUT_AB = RET_W + S5_W
IN_C = 2 * GLA_QK + 2 * GLA_W + GLA_RANK
IN_C_PAD = 2 * GLA_QK + 2 * GLA_W + 128

ADAM_LR = 0.001
ADAM_B1 = 0.9
ADAM_B2 = 0.999
ADAM_EPS = 1e-08
ADAM_WD = 0.01
ADAM_STEP = 10

N_SHARD = 4
SMALL_COLS = 512

NN = (((1,), (0,)), ((), ()))
NT = (((1,), (1,)), ((), ()))
TN = (((0,), (0,)), ((), ()))


def _dot(a, b, dims=NN):
    return lax.dot_general(a.astype(BF16), b.astype(BF16), dims, preferred_element_type=F32)


def _mo(v, m):
    return v if isinstance(v, int) else pl.multiple_of(v, m)


def _sigmoid(x):
    return 1.0 / (1.0 + jnp.exp(-x))


def _row_tile(rows, cap):
    n = rows // CHUNK
    best = 1
    for d in range(1, n + 1):
        if n % d == 0 and d * CHUNK <= cap:
            best = d
    return best * CHUNK


def _col_tile(cols, cap):
    n = cols // 128
    best = 1
    for d in range(1, n + 1):
        if n % d == 0 and d * 128 <= cap:
            best = d
    return best * 128


def _matmul(name, a, b, dims, m, n, k, *, tm, tn, tk, out_dtype=F32, a_off=(0, 0), b_off=(0, 0),
            extras=(), epilogue=None, out_shape=None, out_spec=None):
    nk = k // tk
    assert m % tm == 0 and n % tn == 0 and k % tk == 0, (name, m, n, k, tm, tn, tk)
    ar, ac = a_off
    br, bc = b_off
    if dims == NN:
        a_spec = pl.BlockSpec((tm, tk), lambda i, j, kk: (i + ar, kk + ac))
        b_spec = pl.BlockSpec((tk, tn), lambda i, j, kk: (kk + br, j + bc))
    elif dims == NT:
        a_spec = pl.BlockSpec((tm, tk), lambda i, j, kk: (i + ar, kk + ac))
        b_spec = pl.BlockSpec((tn, tk), lambda i, j, kk: (j + br, kk + bc))
    else:
        a_spec = pl.BlockSpec((tk, tm), lambda i, j, kk: (kk + ar, i + ac))
        b_spec = pl.BlockSpec((tk, tn), lambda i, j, kk: (kk + br, j + bc))
    n_extra = len(extras)
    if out_shape is None:
        out_shape = jax.ShapeDtypeStruct((m, n), out_dtype)
    in_place = nk > 1 and epilogue is None and out_shape.dtype == F32

    def body(*refs):
        a_ref, b_ref = refs[0], refs[1]
        e_refs = refs[2:2 + n_extra]
        o_ref = refs[2 + n_extra]
        part = _dot(a_ref[...], b_ref[...], dims)
        if nk == 1:
            if epilogue is not None:
                part = epilogue(part, *[e[...] for e in e_refs])
            o_ref[...] = part.astype(o_ref.dtype)
            return
        acc_ref = o_ref if in_place else refs[3 + n_extra]
        kk = pl.program_id(2)

        @pl.when(kk == 0)
        def _():
            acc_ref[...] = part

        @pl.when(kk > 0)
        def _():
            acc_ref[...] += part

        if not in_place:
            @pl.when(kk == nk - 1)
            def _():
                acc = acc_ref[...]
                if epilogue is not None:
                    acc = epilogue(acc, *[e[...] for e in e_refs])
                o_ref[...] = acc.astype(o_ref.dtype)

    if out_spec is None:
        out_spec = pl.BlockSpec((tm, tn), lambda i, j, kk: (i, j))
    return pl.pallas_call(
        body, name=name, grid=(m // tm, n // tn, nk),
        in_specs=[a_spec, b_spec] + [pl.BlockSpec(bs, im) for (_, bs, im) in extras],
        out_specs=out_spec, out_shape=out_shape,
        scratch_shapes=[] if (nk == 1 or in_place) else [pltpu.VMEM((tm, tn), F32)],
        compiler_params=pltpu.CompilerParams(dimension_semantics=("parallel", "parallel", "arbitrary")),
    )(a, b, *[e for (e, _, _) in extras])


def _rms_fwd(name, h, w):
    rows, d = h.shape
    tm = _row_tile(rows, 512)

    def body(h_ref, w_ref, o_ref):
        x = h_ref[...]
        r = lax.rsqrt(jnp.mean(x * x, axis=-1, keepdims=True) + EPS)
        o_ref[...] = (x * r * w_ref[...]).astype(BF16)

    return pl.pallas_call(
        body, name=name, grid=(rows // tm,),
        in_specs=[pl.BlockSpec((tm, d), lambda i: (i, 0)), pl.BlockSpec((1, d), lambda i: (0, 0))],
        out_specs=pl.BlockSpec((tm, d), lambda i: (i, 0)),
        out_shape=jax.ShapeDtypeStruct((rows, d), BF16),
    )(h, w)


def _rms_bwd(name, dhn, h, w, dres):
    rows, d = h.shape
    tm = _row_tile(rows, 384)

    def body(g_ref, h_ref, w_ref, r_ref, dh_ref, dw_ref):
        i = pl.program_id(0)
        x = h_ref[...]
        r = lax.rsqrt(jnp.mean(x * x, axis=-1, keepdims=True) + EPS)
        xh = x * r
        g = g_ref[...]
        gw = g * w_ref[...]
        dh_ref[...] = r_ref[...] + r * (gw - xh * jnp.mean(gw * xh, axis=-1, keepdims=True))

        @pl.when(i == 0)
        def _():
            dw_ref[...] = jnp.zeros_like(dw_ref)

        dw_ref[...] += jnp.sum(g * xh, axis=0, keepdims=True)

    return pl.pallas_call(
        body, name=name, grid=(rows // tm,),
        in_specs=[pl.BlockSpec((tm, d), lambda i: (i, 0)), pl.BlockSpec((tm, d), lambda i: (i, 0)),
                  pl.BlockSpec((1, d), lambda i: (0, 0)), pl.BlockSpec((tm, d), lambda i: (i, 0))],
        out_specs=[pl.BlockSpec((tm, d), lambda i: (i, 0)), pl.BlockSpec((1, d), lambda i: (0, 0))],
        out_shape=[jax.ShapeDtypeStruct((rows, d), F32), jax.ShapeDtypeStruct((1, d), F32)],
    )(dhn, h, w, dres)


def _final_loss(h2, w, target):
    rows, d = h2.shape

    def body(h_ref, w_ref, t_ref, loss_ref, dh_ref, dw_ref):
        i = pl.program_id(0)

        @pl.when(i == 0)
        def _():
            loss_ref[...] = jnp.zeros_like(loss_ref)
            dw_ref[...] = jnp.zeros_like(dw_ref)
            dh_ref[...] = jnp.zeros_like(dh_ref)

        @pl.when(i > 0)
        def _():
            x = h_ref[...]
            r = lax.rsqrt(jnp.mean(x * x, axis=-1, keepdims=True) + EPS)
            xh = x * r
            wv = w_ref[...]
            err = xh * wv - t_ref[...]
            loss_ref[...] += 0.5 * jnp.sum(jnp.mean(err * err, axis=-1, keepdims=True), axis=0, keepdims=True)
            g = err * (1.0 / d)
            gw = g * wv
            dh_ref[...] = r * (gw - xh * jnp.mean(gw * xh, axis=-1, keepdims=True))
            dw_ref[...] += jnp.sum(g * xh, axis=0, keepdims=True)

    return pl.pallas_call(
        body, name="final_loss", grid=(rows // CHUNK,),
        in_specs=[pl.BlockSpec((CHUNK, d), lambda i: (i, 0)), pl.BlockSpec((1, d), lambda i: (0, 0)),
                  pl.BlockSpec((CHUNK, d), lambda i: (jnp.maximum(i - 1, 0), 0))],
        out_specs=[pl.BlockSpec((1, 1), lambda i: (0, 0)), pl.BlockSpec((CHUNK, d), lambda i: (i, 0)),
                   pl.BlockSpec((1, d), lambda i: (0, 0))],
        out_shape=[jax.ShapeDtypeStruct((1, 1), F32), jax.ShapeDtypeStruct((rows, d), F32),
                   jax.ShapeDtypeStruct((1, d), F32)],
    )(h2, w, target)


def _gate_fwd(o, z, w):
    rs = lax.rsqrt(jnp.mean(o * o, axis=-1, keepdims=True) + EPS)
    return o * rs * w * (z * _sigmoid(z))


def _gate_bwd(dout, o, z, w):
    rs = lax.rsqrt(jnp.mean(o * o, axis=-1, keepdims=True) + EPS)
    yn = o * rs
    sg = _sigmoid(z)
    sil = z * sg
    dsil = sg * (1.0 + z * (1.0 - sg))
    dz = dout * yn * w * dsil
    dyn = dout * w * sil
    dw = jnp.sum(dout * yn * sil, axis=0, keepdims=True)
    do = rs * (dyn - yn * jnp.mean(dyn * yn, axis=-1, keepdims=True))
    return do, dz, dw


def _rope(t, cosf, sinf):
    return t * cosf + pltpu.roll(t, RET_DK // 2, 1) * sinf


def _rope_t(d, cosf, sinf):
    return d * cosf + pltpu.roll(d * sinf, RET_DK // 2, 1)


def _ret_tables():
    log_g = jnp.log1p(-jnp.exp2(-5.0 - jnp.arange(RET_HEADS, dtype=F32)))
    idx = jnp.arange(CHUNK, dtype=F32)
    diff = idx[:, None] - idx[None, :]
    decay = jnp.where(diff >= 0, jnp.exp(log_g[:, None, None] * jnp.maximum(diff, 0.0)), 0.0)
    kw = jnp.exp(log_g[:, None] * (CHUNK - 1 - idx))
    qw = jnp.exp(log_g[:, None] * (idx + 1.0))
    gch = jnp.exp(log_g * CHUNK)
    kw = jnp.broadcast_to(kw[:, :, None], (RET_HEADS, CHUNK, RET_DK))
    qw = jnp.broadcast_to(qw[:, :, None], (RET_HEADS, CHUNK, RET_DK))
    gch = jnp.broadcast_to(gch[:, None, None], (RET_HEADS, 1, RET_DV))
    return decay, kw, qw, gch


def _rope_tables(rows):
    pos = jnp.arange(rows, dtype=F32) - float(PAD)
    inv_freq = jnp.power(ROPE_BASE, -jnp.arange(0, RET_DK, 2, dtype=F32) / RET_DK)
    ang = pos[:, None] * inv_freq[None, :]
    cos, sin = jnp.cos(ang), jnp.sin(ang)
    return jnp.concatenate([cos, cos], axis=1), jnp.concatenate([-sin, sin], axis=1)


RET_HB = 4
RET_QB = RET_HB * RET_DK
RET_VB = RET_HB * RET_DV


def _ret_in_specs(rev, nc):
    def cn(n):
        return (nc - 1 - n) if rev else n
    kb = RET_QK // RET_QB
    vb = 2 * RET_QK // RET_VB
    zb = (2 * RET_QK + RET_W) // RET_VB
    return [
        pl.BlockSpec((CHUNK, RET_QB), lambda h, n: (cn(n), h)),
        pl.BlockSpec((CHUNK, RET_QB), lambda h, n: (cn(n), kb + h)),
        pl.BlockSpec((CHUNK, RET_VB), lambda h, n: (cn(n), vb + h)),
        pl.BlockSpec((CHUNK, RET_VB), lambda h, n: (cn(n), zb + h)),
        pl.BlockSpec((CHUNK, RET_DK), lambda h, n: (cn(n), 0)),
        pl.BlockSpec((CHUNK, RET_DK), lambda h, n: (cn(n), 0)),
        pl.BlockSpec((RET_HB, CHUNK, CHUNK), lambda h, n: (h, 0, 0)),
        pl.BlockSpec((RET_HB, CHUNK, RET_DK), lambda h, n: (h, 0, 0)),
        pl.BlockSpec((RET_HB, CHUNK, RET_DK), lambda h, n: (h, 0, 0)),
        pl.BlockSpec((RET_HB, 1, RET_DV), lambda h, n: (h, 0, 0)),
        pl.BlockSpec((1, RET_VB), lambda h, n: (0, h)),
    ]


def _ret_fwd(proj, cosf, sinf, tables, normw):
    rows = proj.shape[0]
    nc = rows // CHUNK
    decay, kw, qw, gch = tables

    def body(q_ref, k_ref, v_ref, z_ref, cos_ref, sin_ref, dm_ref, kw_ref, qw_ref, g_ref, w_ref,
             o_ref, oa_ref, st_ref, s_scr):
        n = pl.program_id(1)

        @pl.when(n == 0)
        def _():
            s_scr[...] = jnp.zeros_like(s_scr)

        cosv, sinv = cos_ref[...], sin_ref[...]
        for hh in range(RET_HB):
            qc = slice(hh * RET_DK, (hh + 1) * RET_DK)
            vc = slice(hh * RET_DV, (hh + 1) * RET_DV)
            q = _rope(q_ref[:, qc], cosv, sinv)
            k = _rope(k_ref[:, qc], cosv, sinv) * (RET_DK ** -0.5)
            v = v_ref[:, vc]
            s = s_scr[hh]
            st_ref[hh, 0] = s.astype(BF16)
            a = _dot(q, k, NT) * dm_ref[hh]
            o = _dot(a, v) + _dot(q * qw_ref[hh], s)
            s_scr[hh] = s * g_ref[hh] + _dot(k * kw_ref[hh], v, TN)
            o_ref[:, vc] = o
            oa_ref[:, vc] = _gate_fwd(o, z_ref[:, vc], w_ref[:, vc]).astype(BF16)

    return pl.pallas_call(
        body, name="ret_fwd", grid=(RET_HEADS // RET_HB, nc),
        in_specs=_ret_in_specs(False, nc),
        out_specs=[pl.BlockSpec((CHUNK, RET_VB), lambda h, n: (n, h)),
                   pl.BlockSpec((CHUNK, RET_VB), lambda h, n: (n, h)),
                   pl.BlockSpec((RET_HB, 1, RET_DK, RET_DV), lambda h, n: (h, n, 0, 0))],
        out_shape=[jax.ShapeDtypeStruct((rows, RET_W), F32), jax.ShapeDtypeStruct((rows, RET_W), BF16),
                   jax.ShapeDtypeStruct((RET_HEADS, nc, RET_DK, RET_DV), BF16)],
        scratch_shapes=[pltpu.VMEM((RET_HB, RET_DK, RET_DV), F32)],
        compiler_params=pltpu.CompilerParams(dimension_semantics=("parallel", "arbitrary")),
    )(proj, proj, proj, proj, cosf, sinf, decay, kw, qw, gch, normw)


def _ret_bwd(proj, cosf, sinf, tables, normw, o_ret, dmix, states):
    rows = proj.shape[0]
    nc = rows // CHUNK
    decay, kw, qw, gch = tables

    def rn(n):
        return nc - 1 - n

    def body(q_ref, k_ref, v_ref, z_ref, cos_ref, sin_ref, dm_ref, kw_ref, qw_ref, g_ref, w_ref,
             o_ref, do_ref, st_ref, dq_ref, dk_ref, dv_ref, dz_ref, dw_ref, ds_scr):
        n = pl.program_id(1)

        @pl.when(n == 0)
        def _():
            ds_scr[...] = jnp.zeros_like(ds_scr)
            dw_ref[...] = jnp.zeros_like(dw_ref)

        cosv, sinv = cos_ref[...], sin_ref[...]
        for hh in range(RET_HB):
            qc = slice(hh * RET_DK, (hh + 1) * RET_DK)
            vc = slice(hh * RET_DV, (hh + 1) * RET_DV)
            q = _rope(q_ref[:, qc], cosv, sinv)
            k = _rope(k_ref[:, qc], cosv, sinv) * (RET_DK ** -0.5)
            v = v_ref[:, vc]
            do, dz, dw = _gate_bwd(do_ref[:, vc], o_ref[:, vc], z_ref[:, vc], w_ref[:, vc])
            dz_ref[:, vc] = dz.astype(BF16)
            dw_ref[hh] += dw
            dm = dm_ref[hh]
            s = st_ref[hh, 0]
            g1 = ds_scr[hh]
            p = _dot(q, k, NT) * dm
            kwv = k * kw_ref[hh]
            qwv = q * qw_ref[hh]
            dp = _dot(do, v, NT)
            da = dp * dm
            dv = _dot(p, do, TN) + _dot(kwv, g1)
            dq = _dot(da, k) + _dot(do, s, NT) * qw_ref[hh]
            dk = _dot(da, q, TN) + _dot(v, g1, NT) * kw_ref[hh]
            ds_scr[hh] = g1 * g_ref[hh] + _dot(qwv, do, TN)
            dv_ref[:, vc] = dv.astype(BF16)
            dq_ref[:, qc] = _rope_t(dq, cosv, sinv).astype(BF16)
            dk_ref[:, qc] = _rope_t(dk * (RET_DK ** -0.5), cosv, sinv).astype(BF16)

    in_specs = _ret_in_specs(True, nc) + [
        pl.BlockSpec((CHUNK, RET_VB), lambda h, n: (rn(n), h)),
        pl.BlockSpec((CHUNK, RET_VB), lambda h, n: (rn(n), h)),
        pl.BlockSpec((RET_HB, 1, RET_DK, RET_DV), lambda h, n: (h, rn(n), 0, 0)),
    ]
    return pl.pallas_call(
        body, name="ret_bwd", grid=(RET_HEADS // RET_HB, nc),
        in_specs=in_specs,
        out_specs=[pl.BlockSpec((CHUNK, RET_QB), lambda h, n: (rn(n), h)),
                   pl.BlockSpec((CHUNK, RET_QB), lambda h, n: (rn(n), h)),
                   pl.BlockSpec((CHUNK, RET_VB), lambda h, n: (rn(n), h)),
                   pl.BlockSpec((CHUNK, RET_VB), lambda h, n: (rn(n), h)),
                   pl.BlockSpec((RET_HB, 1, RET_DV), lambda h, n: (h, 0, 0))],
        out_shape=[jax.ShapeDtypeStruct((rows, RET_QK), BF16), jax.ShapeDtypeStruct((rows, RET_QK), BF16),
                   jax.ShapeDtypeStruct((rows, RET_W), BF16), jax.ShapeDtypeStruct((rows, RET_W), BF16),
                   jax.ShapeDtypeStruct((RET_HEADS, 1, RET_DV), F32)],
        scratch_shapes=[pltpu.VMEM((RET_HB, RET_DK, RET_DV), F32)],
        compiler_params=pltpu.CompilerParams(dimension_semantics=("parallel", "arbitrary")),
    )(proj, proj, proj, proj, cosf, sinf, decay, kw, qw, gch, normw, o_ret, dmix, states)


def _s5_discretize(lam_re, lam_im, log_dt, b_re, b_im):
    dt = jnp.exp(log_dt)[:, None]
    mag = jnp.exp(lam_re * dt)
    ab_re, ab_im = mag * jnp.cos(lam_im * dt), mag * jnp.sin(lam_im * dt)
    den = lam_re * lam_re + lam_im * lam_im
    nr, ni = ab_re - 1.0, ab_im
    f_re = (nr * lam_re + ni * lam_im) / den
    f_im = (ni * lam_re - nr * lam_im) / den
    bb_re = f_re[..., None] * b_re - f_im[..., None] * b_im
    bb_im = f_re[..., None] * b_im + f_im[..., None] * b_re
    return ab_re, ab_im, bb_re, bb_im


def _bdiag_in(bb):
    t = bb.reshape(S5_NT, S5_TG, S5_P, S5_GH).transpose(0, 1, 3, 2)
    eye = jnp.eye(S5_TG, dtype=bb.dtype)
    full = t[:, :, :, None, :] * eye[None, :, None, :, None]
    return full.reshape(S5_NT, S5_TU, S5_TS)


def _bdiag_in_extract(dense):
    t = dense.reshape(S5_NT, S5_TG, S5_GH, S5_TG, S5_P)
    diag = jnp.stack([t[:, g, :, g, :] for g in range(S5_TG)], axis=1)
    return diag.transpose(0, 1, 3, 2).reshape(S5_G, S5_P, S5_GH)


def _bdiag_out(c):
    t = c.reshape(S5_NT, S5_TG, S5_GH, S5_P).transpose(0, 1, 3, 2)
    eye = jnp.eye(S5_TG, dtype=c.dtype)
    full = t[:, :, :, None, :] * eye[None, :, None, :, None]
    return full.reshape(S5_NT, S5_TS, S5_TU)


def _bdiag_out_extract(dense):
    t = dense.reshape(S5_NT, S5_TG, S5_P, S5_TG, S5_GH)
    diag = jnp.stack([t[:, g, :, g, :] for g in range(S5_TG)], axis=1)
    return diag.transpose(0, 1, 3, 2).reshape(S5_G, S5_GH, S5_P)


def _cmul(ar, ai, br, bi):
    return ar * br - ai * bi, ar * bi + ai * br


S5_SEG = 8
S5_STEPS = CHUNK // S5_SEG


def _seg_perm(x):
    c = x.shape[1]
    return jnp.swapaxes(x.reshape(S5_SEG, S5_STEPS, c), 0, 1).reshape(CHUNK, c)


def _seg_unperm(x):
    c = x.shape[1]
    return jnp.swapaxes(x.reshape(S5_STEPS, S5_SEG, c), 0, 1).reshape(CHUNK, c)


def _rows(x, p):
    return x[p * S5_SEG:(p + 1) * S5_SEG]


def _s5_tables(ar, ai, tr_scr, ti_scr, wfr_scr, wfi_scr, wbr_scr, wbi_scr):
    row = lax.broadcasted_iota(jnp.int32, (S5_SEG, 1), 0)
    a8r = jnp.broadcast_to(ar, (S5_SEG, S5_TS))
    a8i = jnp.broadcast_to(ai, (S5_SEG, S5_TS))
    pr, pi = a8r, a8i
    for p in range(S5_STEPS):
        tr_scr[p * S5_SEG:(p + 1) * S5_SEG, :] = pr
        ti_scr[p * S5_SEG:(p + 1) * S5_SEG, :] = pi
        if p < S5_STEPS - 1:
            pr, pi = _cmul(pr, pi, a8r, a8i)
    wr, wi = pr, pi
    sh = 1
    while sh < S5_SEG:
        keep = row >= sh
        sr = jnp.where(keep, pltpu.roll(wr, sh, 0), 1.0)
        si = jnp.where(keep, pltpu.roll(wi, sh, 0), 0.0)
        wr, wi = _cmul(wr, wi, sr, si)
        sh *= 2
    wfr_scr[...] = wr
    wfi_scr[...] = wi
    wr, wi = pr, -pi
    sh = 1
    while sh < S5_SEG:
        keep = row < S5_SEG - sh
        sr = jnp.where(keep, pltpu.roll(wr, S5_SEG - sh, 0), 1.0)
        si = jnp.where(keep, pltpu.roll(wi, S5_SEG - sh, 0), 0.0)
        wr, wi = _cmul(wr, wi, sr, si)
        sh *= 2
    wbr_scr[...] = wr
    wbi_scr[...] = wi


def _seg_scan(vr, vi, ar, ai, tr_scr, ti_scr, wr_scr, wi_scr, c0r, c0i, down):
    row = lax.broadcasted_iota(jnp.int32, (S5_SEG, 1), 0)
    sgn = 1.0 if down else -1.0
    order = list(range(S5_STEPS)) if down else list(range(S5_STEPS - 1, -1, -1))
    xr, xi = _rows(vr, order[0]), _rows(vi, order[0])
    loc = {order[0]: (xr, xi)}
    for p in order[1:]:
        mr, mi = _cmul(ar, sgn * ai, xr, xi)
        xr, xi = mr + _rows(vr, p), mi + _rows(vi, p)
        loc[p] = (xr, xi)
    last = S5_STEPS - 1
    mr, mi = tr_scr[last * S5_SEG:(last + 1) * S5_SEG, :], sgn * ti_scr[last * S5_SEG:(last + 1) * S5_SEG, :]
    er, ei = xr, xi
    sh = 1
    while sh < S5_SEG:
        if down:
            keep = row >= sh
            sr, si = pltpu.roll(er, sh, 0), pltpu.roll(ei, sh, 0)
        else:
            keep = row < S5_SEG - sh
            sr, si = pltpu.roll(er, S5_SEG - sh, 0), pltpu.roll(ei, S5_SEG - sh, 0)
        pr, pi = _cmul(mr, mi, jnp.where(keep, sr, 0.0), jnp.where(keep, si, 0.0))
        er, ei = er + pr, ei + pi
        mr, mi = _cmul(mr, mi, mr, mi)
        sh *= 2
    pr, pi = _cmul(wr_scr[...], wi_scr[...], c0r, c0i)
    er, ei = er + pr, ei + pi
    if down:
        nr = jnp.where(row == 0, c0r, pltpu.roll(er, 1, 0))
        ni = jnp.where(row == 0, c0i, pltpu.roll(ei, 1, 0))
    else:
        nr = jnp.where(row == S5_SEG - 1, c0r, pltpu.roll(er, S5_SEG - 1, 0))
        ni = jnp.where(row == S5_SEG - 1, c0i, pltpu.roll(ei, S5_SEG - 1, 0))
    out_r, out_i = [], []
    for p in range(S5_STEPS):
        q = p if down else S5_STEPS - 1 - p
        pr, pi = _cmul(tr_scr[q * S5_SEG:(q + 1) * S5_SEG, :], sgn * ti_scr[q * S5_SEG:(q + 1) * S5_SEG, :], nr, ni)
        out_r.append(loc[p][0] + pr)
        out_i.append(loc[p][1] + pi)
    return jnp.concatenate(out_r, axis=0), jnp.concatenate(out_i, axis=0), (nr, ni), (er, ei)


def _gelu(y):
    c = math.sqrt(2.0 / math.pi)
    return 0.5 * y * (1.0 + jnp.tanh(c * (y + 0.044715 * y * y * y)))


def _gelu_grad(y):
    c = math.sqrt(2.0 / math.pi)
    th = jnp.tanh(c * (y + 0.044715 * y * y * y))
    return 0.5 * (1.0 + th) + 0.5 * y * (1.0 - th * th) * c * (1.0 + 3.0 * 0.044715 * y * y)


def _s5_fwd(proj, ab, bd_b, bd_c, dvec):
    rows = proj.shape[0]
    nc = rows // CHUNK
    ub = (2 * RET_QK + 2 * RET_W) // S5_UB
    ab_re, ab_im = ab
    bre, bim = bd_b
    cre, cim = bd_c

    def body(u_ref, ar_ref, ai_ref, bre_ref, bim_ref, cre_ref, cim_ref, d_ref,
             y_ref, g_ref, er_ref, ei_ref, tr_scr, ti_scr, wfr_scr, wfi_scr, wbr_scr, wbi_scr,
             cr_scr, ci_scr, er_scr, ei_scr):
        n = pl.program_id(1)
        for tt in range(S5_TPS):
            cols = slice(tt * S5_TU, (tt + 1) * S5_TU)
            ar, ai = ar_ref[tt], ai_ref[tt]
            trs, tis, wfr, wfi = tr_scr.at[tt], ti_scr.at[tt], wfr_scr.at[tt], wfi_scr.at[tt]

            @pl.when(n == 0)
            def _(tt=tt, ar=ar, ai=ai, trs=trs, tis=tis, wfr=wfr, wfi=wfi):
                _s5_tables(ar, ai, trs, tis, wfr, wfi, wbr_scr.at[tt], wbi_scr.at[tt])
                cr_scr[tt] = jnp.zeros((S5_SEG, S5_TS), F32)
                ci_scr[tt] = jnp.zeros((S5_SEG, S5_TS), F32)

            u = _seg_perm(u_ref[:, cols])
            c0r, c0i = cr_scr[tt], ci_scr[tt]
            er_ref[tt, 0] = c0r
            ei_ref[tt, 0] = c0i
            xr, xi, _, (er, ei) = _seg_scan(_dot(u, bre_ref[tt]), _dot(u, bim_ref[tt]), ar, ai, trs, tis,
                                            wfr, wfi, c0r, c0i, True)
            er_scr[tt] = er
            ei_scr[tt] = ei
            cr_scr[tt] = jnp.broadcast_to(er_scr[tt, S5_SEG - 1:S5_SEG, :], (S5_SEG, S5_TS))
            ci_scr[tt] = jnp.broadcast_to(ei_scr[tt, S5_SEG - 1:S5_SEG, :], (S5_SEG, S5_TS))
            y = _seg_unperm(_dot(xr, cre_ref[tt]) - _dot(xi, cim_ref[tt]) + d_ref[:, cols] * u)
            y_ref[:, cols] = y
            g_ref[:, cols] = _gelu(y).astype(BF16)

    vec = pl.BlockSpec((S5_TPS, 1, S5_TS), lambda t, n: (t, 0, 0))
    return pl.pallas_call(
        body, name="s5_fwd", grid=(S5_NT // S5_TPS, nc),
        in_specs=[pl.BlockSpec((CHUNK, S5_UB), lambda t, n: (n, ub + t)), vec, vec,
                  pl.BlockSpec((S5_TPS, S5_TU, S5_TS), lambda t, n: (t, 0, 0)),
                  pl.BlockSpec((S5_TPS, S5_TU, S5_TS), lambda t, n: (t, 0, 0)),
                  pl.BlockSpec((S5_TPS, S5_TS, S5_TU), lambda t, n: (t, 0, 0)),
                  pl.BlockSpec((S5_TPS, S5_TS, S5_TU), lambda t, n: (t, 0, 0)),
                  pl.BlockSpec((1, S5_UB), lambda t, n: (0, t))],
        out_specs=[pl.BlockSpec((CHUNK, S5_UB), lambda t, n: (n, t)),
                   pl.BlockSpec((CHUNK, S5_UB), lambda t, n: (n, t)),
                   pl.BlockSpec((S5_TPS, 1, 8, S5_TS), lambda t, n: (t, n, 0, 0)),
                   pl.BlockSpec((S5_TPS, 1, 8, S5_TS), lambda t, n: (t, n, 0, 0))],
        out_shape=[jax.ShapeDtypeStruct((rows, S5_W), F32), jax.ShapeDtypeStruct((rows, S5_W), BF16),
                   jax.ShapeDtypeStruct((S5_NT, nc, 8, S5_TS), F32),
                   jax.ShapeDtypeStruct((S5_NT, nc, 8, S5_TS), F32)],
        scratch_shapes=[pltpu.VMEM((S5_TPS, CHUNK, S5_TS), F32) for _ in range(2)]
        + [pltpu.VMEM((S5_TPS, S5_SEG, S5_TS), F32) for _ in range(8)],
        compiler_params=pltpu.CompilerParams(dimension_semantics=("parallel", "arbitrary")),
    )(proj, ab_re.reshape(S5_NT, 1, S5_TS), ab_im.reshape(S5_NT, 1, S5_TS), bre, bim, cre, cim, dvec)


def _s5_bwd(proj, dy, ab, bd_b, bd_c, dvec, entry):
    rows = proj.shape[0]
    nc = rows // CHUNK
    ub = (2 * RET_QK + 2 * RET_W) // S5_UB
    ab_re, ab_im = ab
    bre, bim = bd_b
    cre, cim = bd_c
    er, ei = entry

    def rn(n):
        return nc - 1 - n

    def body(u_ref, dy_ref, ar_ref, ai_ref, bre_ref, bim_ref, cre_ref, cim_ref, d_ref, er_ref, ei_ref,
             du_ref, dbr_ref, dbi_ref, dcr_ref, dci_ref, dar_ref, dai_ref, dd_ref,
             tr_scr, ti_scr, wfr_scr, wfi_scr, wbr_scr, wbi_scr, gr_scr, gi_scr, er_scr, ei_scr):
        n = pl.program_id(1)

        @pl.when(n == 0)
        def _():
            gr_scr[...] = jnp.zeros_like(gr_scr)
            gi_scr[...] = jnp.zeros_like(gi_scr)
            for r in (dbr_ref, dbi_ref, dcr_ref, dci_ref, dar_ref, dai_ref, dd_ref):
                r[...] = jnp.zeros_like(r)

        for tt in range(S5_TPS):
            cols = slice(tt * S5_TU, (tt + 1) * S5_TU)
            ar, ai = ar_ref[tt], ai_ref[tt]
            trs, tis = tr_scr.at[tt], ti_scr.at[tt]

            @pl.when(n == 0)
            def _(tt=tt, ar=ar, ai=ai, trs=trs, tis=tis):
                _s5_tables(ar, ai, trs, tis, wfr_scr.at[tt], wfi_scr.at[tt], wbr_scr.at[tt], wbi_scr.at[tt])

            u = _seg_perm(u_ref[:, cols])
            dy = _seg_perm(dy_ref[:, cols])
            xr, xi, (pr, pi), _ = _seg_scan(_dot(u, bre_ref[tt]), _dot(u, bim_ref[tt]), ar, ai, trs, tis,
                                            wfr_scr.at[tt], wfi_scr.at[tt], er_ref[tt, 0], ei_ref[tt, 0], True)
            dcr_ref[tt] += _dot(xr, dy, TN)
            dci_ref[tt] -= _dot(xi, dy, TN)
            gr, gi, _, (er, ei) = _seg_scan(_dot(dy, cre_ref[tt], NT), -_dot(dy, cim_ref[tt], NT), ar, ai, trs, tis,
                                            wbr_scr.at[tt], wbi_scr.at[tt], gr_scr[tt], gi_scr[tt], False)
            er_scr[tt] = er
            ei_scr[tt] = ei
            gr_scr[tt] = jnp.broadcast_to(er_scr[tt, 0:1, :], (S5_SEG, S5_TS))
            gi_scr[tt] = jnp.broadcast_to(ei_scr[tt, 0:1, :], (S5_SEG, S5_TS))
            xpr = jnp.concatenate([pr, xr[:CHUNK - S5_SEG]], axis=0)
            xpi = jnp.concatenate([pi, xi[:CHUNK - S5_SEG]], axis=0)
            dar_ref[tt] += jnp.sum((xpr * gr + xpi * gi).reshape(S5_STEPS, S5_SEG, S5_TS), axis=0)
            dai_ref[tt] += jnp.sum((xpr * gi - xpi * gr).reshape(S5_STEPS, S5_SEG, S5_TS), axis=0)
            dbr_ref[tt] += _dot(u, gr, TN)
            dbi_ref[tt] += _dot(u, gi, TN)
            dd_ref[tt] += jnp.sum((dy * u).reshape(S5_STEPS, S5_SEG, S5_TU), axis=0)
            du = dy * d_ref[:, cols] + _dot(gr, bre_ref[tt], NT) + _dot(gi, bim_ref[tt], NT)
            du_ref[:, cols] = _seg_unperm(du).astype(BF16)

    vec = pl.BlockSpec((S5_TPS, 1, S5_TS), lambda t, n: (t, 0, 0))
    acc_b = pl.BlockSpec((S5_TPS, S5_TU, S5_TS), lambda t, n: (t, 0, 0))
    acc_c = pl.BlockSpec((S5_TPS, S5_TS, S5_TU), lambda t, n: (t, 0, 0))
    acc_a = pl.BlockSpec((S5_TPS, 8, S5_TS), lambda t, n: (t, 0, 0))
    ent = pl.BlockSpec((S5_TPS, 1, 8, S5_TS), lambda t, n: (t, rn(n), 0, 0))
    return pl.pallas_call(
        body, name="s5_bwd", grid=(S5_NT // S5_TPS, nc),
        in_specs=[pl.BlockSpec((CHUNK, S5_UB), lambda t, n: (rn(n), ub + t)),
                  pl.BlockSpec((CHUNK, S5_UB), lambda t, n: (rn(n), t)), vec, vec,
                  acc_b, acc_b, acc_c, acc_c, pl.BlockSpec((1, S5_UB), lambda t, n: (0, t)), ent, ent],
        out_specs=[pl.BlockSpec((CHUNK, S5_UB), lambda t, n: (rn(n), t)), acc_b, acc_b, acc_c, acc_c, acc_a, acc_a,
                   pl.BlockSpec((S5_TPS, 8, S5_TU), lambda t, n: (t, 0, 0))],
        out_shape=[jax.ShapeDtypeStruct((rows, S5_W), BF16),
                   jax.ShapeDtypeStruct((S5_NT, S5_TU, S5_TS), F32), jax.ShapeDtypeStruct((S5_NT, S5_TU, S5_TS), F32),
                   jax.ShapeDtypeStruct((S5_NT, S5_TS, S5_TU), F32), jax.ShapeDtypeStruct((S5_NT, S5_TS, S5_TU), F32),
                   jax.ShapeDtypeStruct((S5_NT, 8, S5_TS), F32), jax.ShapeDtypeStruct((S5_NT, 8, S5_TS), F32),
                   jax.ShapeDtypeStruct((S5_NT, 8, S5_TU), F32)],
        scratch_shapes=[pltpu.VMEM((S5_TPS, CHUNK, S5_TS), F32) for _ in range(2)]
        + [pltpu.VMEM((S5_TPS, S5_SEG, S5_TS), F32) for _ in range(8)],
        compiler_params=pltpu.CompilerParams(dimension_semantics=("parallel", "arbitrary")),
    )(proj, dy,ab_re.reshape(S5_NT, 1, S5_TS), ab_im.reshape(S5_NT, 1, S5_TS), bre, bim, cre, cim, dvec, er, ei)


def _s5_gate_bwd(dmix, g, t, proj):
    rows = g.shape[0]
    tm = _row_tile(rows, 384)
    ob = RET_W // S5_W
    zb = (2 * RET_QK + 2 * RET_W + S5_W) // S5_W

    def body(do_ref, g_ref, t_ref, z_ref, dz_ref, dt_ref, dg_ref):
        do = do_ref[...]
        gv = g_ref[...].astype(F32)
        z = z_ref[...]
        st = _sigmoid(t_ref[...])
        sg = _sigmoid(z)
        os5 = gv * st
        dz_ref[...] = (do * os5 * sg * (1.0 + z * (1.0 - sg))).astype(BF16)
        dos = do * z * sg
        dt_ref[...] = (dos * gv * st * (1.0 - st)).astype(BF16)
        dg_ref[...] = dos * st

    blk = pl.BlockSpec((tm, S5_W), lambda i: (i, 0))
    return pl.pallas_call(
        body, name="s5_gate_bwd", grid=(rows // tm,),
        in_specs=[pl.BlockSpec((tm, S5_W), lambda i: (i, ob)), blk, blk,
                  pl.BlockSpec((tm, S5_W), lambda i: (i, zb))],
        out_specs=[blk, blk, blk],
        out_shape=[jax.ShapeDtypeStruct((rows, S5_W), BF16), jax.ShapeDtypeStruct((rows, S5_W), BF16),
                   jax.ShapeDtypeStruct((rows, S5_W), F32)],
    )(dmix, g, t, proj)


def _split3(x):
    hi = x.astype(BF16)
    r = x - hi.astype(F32)
    mid = r.astype(BF16)
    lo = (r - mid.astype(F32)).astype(BF16)
    return hi, mid, lo


def _tri_sum(x, upper):
    i = lax.broadcasted_iota(jnp.int32, (CHUNK, CHUNK), 0)
    j = lax.broadcasted_iota(jnp.int32, (CHUNK, CHUNK), 1)
    tri = jnp.where((j >= i) if upper else (j <= i), 1.0, 0.0).astype(BF16)
    hi, mid, lo = _split3(x)
    return _dot(tri, lo) + _dot(tri, mid) + _dot(tri, hi)


def _gla_log_decay(gl, wg, bg, n):
    logit = _dot(gl, wg) + bg
    la = (jnp.minimum(logit, 0.0) - jnp.log(1.0 + jnp.exp(-jnp.abs(logit)))) * (1.0 / GLA_TAU)
    row = lax.broadcasted_iota(jnp.int32, (CHUNK, 1), 0)
    live = jnp.logical_or(n > 0, row >= PAD)
    return logit, jnp.where(live, la, 0.0), live


def _gla_in_specs(rev, nc):
    def cn(n):
        return (nc - 1 - n) if rev else n
    kb = GLA_QK // GLA_DK
    vb = 2 * GLA_QK // GLA_DV
    zb = (2 * GLA_QK + GLA_W) // GLA_DV
    gb = (2 * GLA_QK + 2 * GLA_W) // 128
    return [
        pl.BlockSpec((CHUNK, GLA_DK), lambda h, n: (cn(n), h)),
        pl.BlockSpec((CHUNK, GLA_DK), lambda h, n: (cn(n), kb + h)),
        pl.BlockSpec((CHUNK, GLA_DV), lambda h, n: (cn(n), vb + h)),
        pl.BlockSpec((CHUNK, GLA_DV), lambda h, n: (cn(n), zb + h)),
        pl.BlockSpec((CHUNK, 128), lambda h, n: (cn(n), gb)),
        pl.BlockSpec((128, GLA_DK), lambda h, n: (0, h)),
        pl.BlockSpec((1, GLA_DK), lambda h, n: (0, h)),
        pl.BlockSpec((1, GLA_DV), lambda h, n: (0, h)),
    ]


def _gla_fwd(proj, wgate, bgate, normw):
    rows = proj.shape[0]
    nc = rows // CHUNK

    def body(q_ref, k_ref, v_ref, z_ref, gl_ref, wg_ref, bg_ref, w_ref, o_ref, oc_ref, st_ref, s_scr, o_scr, b_scr):
        n = pl.program_id(1)

        @pl.when(n == 0)
        def _():
            s_scr[...] = jnp.zeros_like(s_scr)

        q = q_ref[...] * (GLA_DK ** -0.5)
        k = k_ref[...]
        v = v_ref[...]
        vb = v.astype(BF16)
        _, la, _ = _gla_log_decay(gl_ref[...], wg_ref[...], bg_ref[...], n)
        b = _tri_sum(la, False)
        b_scr[...] = b
        b_last = b_scr[CHUNK - 1:CHUNK, :]
        st = s_scr[...]
        st_ref[0, 0] = st
        s_scr[...] = st * jnp.exp(b_last) + _dot(v, k * jnp.exp(b_last - b), TN)
        rowc = lax.broadcasted_iota(jnp.int32, (CHUNK, 1), 0)
        rows16 = lax.broadcasted_iota(jnp.int32, (SUB, 1), 0)
        a_tot = jnp.zeros((CHUNK, CHUNK), F32)
        for s in range(1, NSUB):
            lo = s * SUB
            bref = b_scr[lo - 1:lo, :]
            in_s = jnp.logical_and(rowc >= lo, rowc < lo + SUB)
            qh = q * jnp.exp(jnp.where(in_s, b - bref, -1e30))
            kh = k * jnp.exp(jnp.where(rowc < lo, bref - b, -1e30))
            a_tot = a_tot + _dot(qh, kh, NT)
        o_scr[...] = _dot(q * jnp.exp(b), st, NT) + _dot(a_tot, vb)
        for s in range(NSUB):
            lo = s * SUB
            qs, bs = q[lo:lo + SUB], b[lo:lo + SUB]
            acc = jnp.zeros((SUB, GLA_DV), F32)
            for j in range(SUB):
                r = lo + j
                e = jnp.exp(jnp.where(rows16 >= j, bs - b_scr[r:r + 1, :], -1e30))
                col = jnp.sum(qs * k_ref[r:r + 1, :] * e, axis=1, keepdims=True)
                acc = acc + col * v_ref[r:r + 1, :]
            o_scr[lo:lo + SUB, :] += acc
        o = o_scr[...]
        o_ref[...] = o
        oc_ref[...] = _gate_fwd(o, z_ref[...], w_ref[...]).astype(BF16)

    return pl.pallas_call(
        body, name="gla_fwd", grid=(GLA_HEADS, nc),
        in_specs=_gla_in_specs(False, nc),
        out_specs=[pl.BlockSpec((CHUNK, GLA_DV), lambda h, n: (n, h)),
                   pl.BlockSpec((CHUNK, GLA_DV), lambda h, n: (n, h)),
                   pl.BlockSpec((1, 1, GLA_DV, GLA_DK), lambda h, n: (h, n, 0, 0))],
        out_shape=[jax.ShapeDtypeStruct((rows, GLA_W), F32), jax.ShapeDtypeStruct((rows, GLA_W), BF16),
                   jax.ShapeDtypeStruct((GLA_HEADS, nc, GLA_DV, GLA_DK), F32)],
        scratch_shapes=[pltpu.VMEM((GLA_DV, GLA_DK), F32), pltpu.VMEM((CHUNK, GLA_DV), F32),
                        pltpu.VMEM((CHUNK, GLA_DK), F32)],
        compiler_params=pltpu.CompilerParams(dimension_semantics=("parallel", "arbitrary")),
    )(proj, proj, proj, proj, proj, wgate, bgate, normw)


def _gla_bwd(proj, wgate, bgate, normw, o_gla, d_oc, states):
    rows = proj.shape[0]
    nc = rows // CHUNK

    def rn(n):
        return nc - 1 - n

    def body(q_ref, k_ref, v_ref, z_ref, gl_ref, wg_ref, bg_ref, w_ref, o_ref, do_ref, st_ref,
             dq_ref, dk_ref, dv_ref, dz_ref, dl_ref, dw_ref, dbg_ref,
             ds_scr, dq_scr, dk_scr, dv_scr, db_scr, b_scr):
        n = pl.program_id(1)
        cn = rn(n)

        @pl.when(n == 0)
        def _():
            ds_scr[...] = jnp.zeros_like(ds_scr)
            dw_ref[...] = jnp.zeros_like(dw_ref)
            dbg_ref[...] = jnp.zeros_like(dbg_ref)

        q = q_ref[...] * (GLA_DK ** -0.5)
        k = k_ref[...]
        v = v_ref[...]
        vb = v.astype(BF16)
        do, dz, dw = _gate_bwd(do_ref[...], o_ref[...], z_ref[...], w_ref[...])
        dz_ref[...] = dz.astype(BF16)
        dw_ref[0] += dw
        logit, la, live = _gla_log_decay(gl_ref[...], wg_ref[...], bg_ref[...], cn)
        b = _tri_sum(la, False)
        b_scr[...] = b
        b_last = b_scr[CHUNK - 1:CHUNK, :]
        e_last = jnp.exp(b_last)
        st = st_ref[0, 0]
        g1 = ds_scr[...]
        eb = jnp.exp(b)
        qe = q * eb
        dqe = _dot(do, st)
        dq_scr[...] = dqe * eb
        db_scr[...] = dqe * qe
        ekb = jnp.exp(b_last - b)
        kdec = k * ekb
        dkdec = _dot(v, g1)
        dv_scr[...] = _dot(kdec, g1, NT)
        dk_scr[...] = dkdec * ekb
        wk = dkdec * kdec
        db_scr[...] -= wk
        dbl = jnp.sum(wk, axis=0, keepdims=True) + jnp.sum(g1 * st, axis=0, keepdims=True) * e_last
        ds_scr[...] = g1 * e_last + _dot(do, qe, TN)
        rowc = lax.broadcasted_iota(jnp.int32, (CHUNK, 1), 0)
        rows16 = lax.broadcasted_iota(jnp.int32, (SUB, 1), 0)
        da_full = _dot(do, vb, NT)
        a_tot = jnp.zeros((CHUNK, CHUNK), F32)
        for s in range(1, NSUB):
            lo = s * SUB
            bref = b_scr[lo - 1:lo, :]
            in_s = jnp.logical_and(rowc >= lo, rowc < lo + SUB)
            eq = jnp.exp(jnp.where(in_s, b - bref, -1e30))
            ek = jnp.exp(jnp.where(rowc < lo, bref - b, -1e30))
            qh = q * eq
            kh = k * ek
            a_tot = a_tot + _dot(qh, kh, NT)
            da = jnp.where(in_s, da_full, 0.0)
            dqh = _dot(da, kh)
            dkh = _dot(da, qh, TN)
            tq = dqh * qh
            tk = dkh * kh
            dq_scr[...] += dqh * eq
            dk_scr[...] += dkh * ek
            db_scr[...] += tq - tk
            db_scr[lo - 1:lo, :] += jnp.sum(tk, axis=0, keepdims=True) - jnp.sum(tq, axis=0, keepdims=True)
        dv_scr[...] += _dot(a_tot, do, TN)
        for s in range(NSUB):
            lo = s * SUB
            qs, bs = q[lo:lo + SUB], b[lo:lo + SUB]
            dos = do[lo:lo + SUB]
            dqs = jnp.zeros((SUB, GLA_DK), F32)
            dks = jnp.zeros((SUB, GLA_DK), F32)
            dbs = jnp.zeros((SUB, GLA_DK), F32)
            dvs = jnp.zeros((SUB, GLA_DV), F32)
            for j in range(SUB):
                pick = rows16 == j
                r = lo + j
                kj, vj, bj = k_ref[r:r + 1, :], v_ref[r:r + 1, :], b_scr[r:r + 1, :]
                e = jnp.exp(jnp.where(rows16 >= j, bs - bj, -1e30))
                qe_j = qs * e
                col = jnp.sum(qe_j * kj, axis=1, keepdims=True)
                dcol = jnp.sum(dos * vj, axis=1, keepdims=True)
                dvs = dvs + jnp.where(pick, jnp.sum(col * dos, axis=0, keepdims=True), 0.0)
                m = dcol * e
                dqs = dqs + m * kj
                mq = m * qs
                dks = dks + jnp.where(pick, jnp.sum(mq, axis=0, keepdims=True), 0.0)
                t = mq * kj
                dbs = dbs + t - jnp.where(pick, jnp.sum(t, axis=0, keepdims=True), 0.0)
            dq_scr[lo:lo + SUB, :] += dqs
            dk_scr[lo:lo + SUB, :] += dks
            dv_scr[lo:lo + SUB, :] += dvs
            db_scr[lo:lo + SUB, :] += dbs
        db_scr[CHUNK - 1:CHUNK, :] += dbl
        dla = _tri_sum(db_scr[...], True)
        dlogit = jnp.where(live, dla * (1.0 / GLA_TAU) * _sigmoid(-logit), 0.0)
        dl_ref[...] = dlogit
        dbg_ref[0] += jnp.sum(dlogit, axis=0, keepdims=True)
        dq_ref[...] = (dq_scr[...] * (GLA_DK ** -0.5)).astype(BF16)
        dk_ref[...] = dk_scr[...].astype(BF16)
        dv_ref[...] = dv_scr[...].astype(BF16)

    in_specs = _gla_in_specs(True, nc) + [
        pl.BlockSpec((CHUNK, GLA_DV), lambda h, n: (rn(n), h)),
        pl.BlockSpec((CHUNK, GLA_DV), lambda h, n: (rn(n), h)),
        pl.BlockSpec((1, 1, GLA_DV, GLA_DK), lambda h, n: (h, rn(n), 0, 0)),
    ]
    return pl.pallas_call(
        body, name="gla_bwd", grid=(GLA_HEADS, nc),
        in_specs=in_specs,
        out_specs=[pl.BlockSpec((CHUNK, GLA_DK), lambda h, n: (rn(n), h)),
                   pl.BlockSpec((CHUNK, GLA_DK), lambda h, n: (rn(n), h)),
                   pl.BlockSpec((CHUNK, GLA_DV), lambda h, n: (rn(n), h)),
                   pl.BlockSpec((CHUNK, GLA_DV), lambda h, n: (rn(n), h)),
                   pl.BlockSpec((CHUNK, GLA_DK), lambda h, n: (rn(n), h)),
                   pl.BlockSpec((1, 1, GLA_DV), lambda h, n: (h, 0, 0)),
                   pl.BlockSpec((1, 1, GLA_DK), lambda h, n: (h, 0, 0))],
        out_shape=[jax.ShapeDtypeStruct((rows, GLA_QK), BF16), jax.ShapeDtypeStruct((rows, GLA_QK), BF16),
                   jax.ShapeDtypeStruct((rows, GLA_W), BF16), jax.ShapeDtypeStruct((rows, GLA_W), BF16),
                   jax.ShapeDtypeStruct((rows, GLA_QK), F32),
                   jax.ShapeDtypeStruct((GLA_HEADS, 1, GLA_DV), F32),
                   jax.ShapeDtypeStruct((GLA_HEADS, 1, GLA_DK), F32)],
        scratch_shapes=[pltpu.VMEM((GLA_DV, GLA_DK), F32), pltpu.VMEM((CHUNK, GLA_DK), F32),
                        pltpu.VMEM((CHUNK, GLA_DK), F32), pltpu.VMEM((CHUNK, GLA_DV), F32),
                        pltpu.VMEM((CHUNK, GLA_DK), F32), pltpu.VMEM((CHUNK, GLA_DK), F32)],
        compiler_params=pltpu.CompilerParams(dimension_semantics=("parallel", "arbitrary")),
    )(proj, proj, proj, proj, proj, wgate, bgate, normw, o_gla, d_oc, states)


def _adamw(name, w, g, m, v):
    rows, cols = w.shape[-2:]
    tm = rows
    for cand in (256, 128, 64, 32, 16, 8):
        if rows % cand == 0:
            tm = cand
            break
    c1 = 1.0 - ADAM_B1 ** ADAM_STEP
    c2 = 1.0 - ADAM_B2 ** ADAM_STEP

    def body(w_ref, g_ref, m_ref, v_ref, d_ref, nm_ref, nv_ref):
        gv = g_ref[...]
        nm = ADAM_B1 * m_ref[...] + (1.0 - ADAM_B1) * gv
        nv = ADAM_B2 * v_ref[...] + (1.0 - ADAM_B2) * (gv * gv)
        nm_ref[...] = nm
        nv_ref[...] = nv
        d_ref[...] = -ADAM_LR * ((nm / c1) / (jnp.sqrt(nv / c2) + ADAM_EPS) + ADAM_WD * w_ref[...])

    blk2 = pl.BlockSpec((tm, cols), lambda i: (i, 0))
    blk = pl.BlockSpec((None, tm, cols), lambda i: (0, i, 0)) if w.ndim == 3 else blk2
    return pl.pallas_call(
        body, name=name, grid=(rows // tm,),
        in_specs=[blk, blk2, blk, blk], out_specs=[blk] * 3,
        out_shape=[jax.ShapeDtypeStruct(w.shape, F32)] * 3,
    )(w, g, m, v)


def _place():
    x, y, c = lax.axis_index("x"), lax.axis_index("y"), lax.axis_index("c")
    chips = [(1 - x, y), (x, 1 - y), (1 - x, 1 - y)]
    return x, y, c, chips


ANY = pl.BlockSpec(memory_space=pl.ANY)


def _gathered_struct(shape, dtype, kind):
    r, cc = shape
    if kind == "row":
        return jax.ShapeDtypeStruct((N_SHARD * r, cc), dtype)
    if kind == "col":
        return jax.ShapeDtypeStruct((r, N_SHARD * cc), dtype)
    return jax.ShapeDtypeStruct((N_SHARD, r, cc), dtype)


def _cast_place(name, w, kind, mine_arr, dtype):
    r, cc = w.shape[-2:]
    tr = r
    for cand in (256, 128, 64, 32, 16):
        if r % cand == 0:
            tr = cand
            break
    nb = r // tr
    if kind == "row":
        o_spec = pl.BlockSpec((tr, cc), lambda i, m: (m[0] * nb + i, 0))
    elif kind == "col":
        o_spec = pl.BlockSpec((tr, cc), lambda i, m: (i, m[0]))
    else:
        o_spec = pl.BlockSpec((None, tr, cc), lambda i, m: (m[0], i, 0))
    if w.ndim == 3:
        w_spec = pl.BlockSpec((None, tr, cc), lambda i, m: (0, i, 0))
    else:
        w_spec = pl.BlockSpec((tr, cc), lambda i, m: (i, 0))

    def body(m_ref, w_ref, o_ref):
        o_ref[...] = w_ref[...].astype(o_ref.dtype)

    return pl.pallas_call(
        body, name=name,
        grid_spec=pltpu.PrefetchScalarGridSpec(
            num_scalar_prefetch=1, grid=(nb,), in_specs=[w_spec], out_specs=o_spec),
        out_shape=_gathered_struct((r, cc), dtype, kind),
    )(mine_arr, w)


def _allreduce_small(buf):
    rows, cols = buf.shape

    def body(in_ref, out_ref, sib_ref, pair_ref, far_ref, send_sems, recv_sems):
        x, y, c, chips = _place()
        sibling = (x, y, 1 - c)
        to_sib = pltpu.make_async_remote_copy(
            src_ref=in_ref, dst_ref=sib_ref, send_sem=send_sems.at[0], recv_sem=recv_sems.at[0],
            device_id=sibling, device_id_type=MESH)
        to_sib.start()
        to_sib.wait()
        pair_ref[...] = in_ref[...] + sib_ref[...]
        far = [pltpu.make_async_remote_copy(
            src_ref=pair_ref, dst_ref=far_ref.at[j], send_sem=send_sems.at[1 + j], recv_sem=recv_sems.at[1 + j],
            device_id=(*chip, c), device_id_type=MESH) for j, chip in enumerate(chips)]
        for cp in far:
            cp.start()
        for cp in far:
            cp.wait()
        out_ref[...] = (pair_ref[...] + far_ref[1]) + (far_ref[0] + far_ref[2])

    vm = pl.BlockSpec(memory_space=pltpu.VMEM)
    return pl.pallas_call(
        body, name="allreduce_small",
        in_specs=[vm], out_specs=vm,
        out_shape=jax.ShapeDtypeStruct((rows, cols), F32),
        scratch_shapes=[pltpu.VMEM((rows, cols), F32), pltpu.VMEM((rows, cols), F32),
                        pltpu.VMEM((3, rows, cols), F32),
                        pltpu.SemaphoreType.DMA((4,)), pltpu.SemaphoreType.DMA((4,))],
        compiler_params=pltpu.CompilerParams(has_side_effects=True),
    )(buf)


def _shard_window(ref, kind, shard_shape, shard, half):
    r, cc = shard_shape
    hr = r // 2
    if kind == "row":
        return ref.at[pl.ds(_mo(shard * r + half * hr, 8), hr), :]
    if kind == "col":
        return ref.at[pl.ds(_mo(half * hr, 8), hr), pl.ds(_mo(shard * cc, 128), cc)]
    return ref.at[shard, pl.ds(_mo(half * hr, 8), hr), :]


HBM = pl.BlockSpec(memory_space=pltpu.HBM)
SEM = pl.BlockSpec(memory_space=pltpu.SEMAPHORE)
DATAFLOW = pltpu.SideEffectType.DATAFLOW_SIDE_EFFECTING


def _in_hbm(a):
    return pltpu.with_memory_space_constraint(a, pltpu.HBM)


def _empty_hbm(shape, dtype):
    return _in_hbm(lax.empty(shape, dtype))


def _copies_start(name, bufs, n_copies, plan, carry):
    nb = len(bufs)

    def body(*refs):
        send_sems, recv_sems = refs[nb + 1], refs[nb + 2]
        for k, (src, dst, to) in enumerate(plan(refs[:nb])):
            pltpu.make_async_remote_copy(src_ref=src, dst_ref=dst, send_sem=send_sems.at[k], recv_sem=recv_sems.at[k],
                                         device_id=to, device_id_type=MESH).start()

    passed = list(bufs) + [carry]
    out = pl.pallas_call(
        body, name=name,
        in_specs=[HBM] * (nb + 1), out_specs=[SEM, SEM] + [HBM] * (nb + 1),
        out_shape=[pltpu.SemaphoreType.DMA((n_copies,)), pltpu.SemaphoreType.DMA((n_copies,))]
        + [pltpu.HBM(a.shape, a.dtype) for a in passed],
        input_output_aliases={i: 2 + i for i in range(nb + 1)},
        compiler_params=pltpu.CompilerParams(has_side_effects=DATAFLOW),
    )(*[_in_hbm(a) for a in passed])
    return out[0], out[1], list(out[2:2 + nb]), out[2 + nb]


def _copies_wait(name, send_sems, recv_sems, bufs, plan, after):
    nb = len(bufs)
    after = list(after) if isinstance(after, (list, tuple)) else [after]

    def body(*refs):
        send, recv = refs[nb], refs[nb + 1]
        for k, (src, dst, to) in enumerate(plan(refs[:nb])):
            cp = pltpu.make_async_remote_copy(src_ref=src, dst_ref=dst, send_sem=send.at[k], recv_sem=recv.at[k],
                                              device_id=to, device_id_type=MESH)
            cp.wait_send()
            cp.wait_recv()

    out = pl.pallas_call(
        body, name=name,
        in_specs=[HBM] * nb + [SEM, SEM] + [ANY] * len(after), out_specs=[HBM] * nb,
        out_shape=[pltpu.HBM(a.shape, a.dtype) for a in bufs],
        input_output_aliases={i: i for i in range(nb)},
        compiler_params=pltpu.CompilerParams(has_side_effects=DATAFLOW),
    )(*bufs, send_sems, recv_sems, *after)
    return list(out)


def _gather_ici_plan(shard_shapes, kinds):
    n_arr = len(kinds)

    def plan(refs):
        x, y, c, chips = _place()
        out = []
        for i in range(n_arr):
            w = _shard_window(refs[i], kinds[i], shard_shapes[i], 2 * x + y, c)
            out += [(w, w, (*chip, c)) for chip in chips]
        return out

    return plan


def _gather_d2d_plan(shard_shapes, kinds):
    n_arr = len(kinds)

    def plan(refs):
        x, y, c, chips = _place()
        out = []
        for i in range(n_arr):
            for chip in chips:
                w = _shard_window(refs[i], kinds[i], shard_shapes[i], 2 * chip[0] + chip[1], c)
                out.append((w, w, (x, y, 1 - c)))
        return out

    return plan


def _rs_pair_plan(kinds, shard_shapes):
    n_arr = len(kinds)

    def plan(refs):
        x, y, c, _ = _place()
        out = []
        for i in range(n_arr):
            for s in range(N_SHARD):
                out.append((_shard_window(refs[i], kinds[i], shard_shapes[i], s, 1 - c), refs[n_arr + i].at[s],
                            (x, y, 1 - c)))
        return out

    return plan


def _rs_chip_plan(n_arr):
    def plan(refs):
        x, y, c, chips = _place()
        out = []
        for i in range(n_arr):
            for j, chip in enumerate(chips):
                out.append((refs[i].at[2 * chip[0] + chip[1]], refs[n_arr + i].at[j], (*chip, c)))
        return out

    return plan


def _rs_pair_add(name, grad, got, kind, shard_shape, c):
    r, cc = shard_shape
    hr = r // 2
    tr = hr
    for cand in (256, 128, 64, 32, 16):
        if hr % cand == 0:
            tr = cand
            break
    nb = hr // tr

    if kind == "row":
        g_spec = pl.BlockSpec((tr, cc), lambda s, i, cr: (s * 2 * nb + cr[0] * nb + i, 0))
    elif kind == "col":
        g_spec = pl.BlockSpec((tr, cc), lambda s, i, cr: (cr[0] * nb + i, s))
    else:
        g_spec = pl.BlockSpec((None, tr, cc), lambda s, i, cr: (s, cr[0] * nb + i, 0))
    t_spec = pl.BlockSpec((None, tr, cc), lambda s, i, cr: (s, i, 0))

    def body(c_ref, g_ref, t_ref, p_ref, pb_ref):
        p = g_ref[...] + t_ref[...]
        p_ref[...] = p
        pb_ref[...] = p.astype(BF16)

    return pl.pallas_call(
        body, name=name,
        grid_spec=pltpu.PrefetchScalarGridSpec(
            num_scalar_prefetch=1, grid=(N_SHARD, nb),
            in_specs=[g_spec, t_spec], out_specs=[t_spec, t_spec]),
        out_shape=[jax.ShapeDtypeStruct((N_SHARD, hr, cc), F32), jax.ShapeDtypeStruct((N_SHARD, hr, cc), BF16)],
    )(c, grad, got)


def _rs_chip_add(name, pair_f32, got, shard_shape, mine_c):
    r, cc = shard_shape
    hr = r // 2
    tr = hr
    for cand in (256, 128, 64, 32, 16):
        if hr % cand == 0:
            tr = cand
            break
    nb = hr // tr

    def body(mc_ref, p_ref, t0_ref, t1_ref, t2_ref, o_ref):
        o_ref[...] = (p_ref[...] + t1_ref[...].astype(F32)) + (t0_ref[...].astype(F32) + t2_ref[...].astype(F32))

    def far(j):
        return pl.BlockSpec((None, tr, cc), lambda i, mc: (j, i, 0))

    return pl.pallas_call(
        body, name=name,
        grid_spec=pltpu.PrefetchScalarGridSpec(
            num_scalar_prefetch=1, grid=(nb,),
            in_specs=[pl.BlockSpec((None, tr, cc), lambda i, mc: (mc[0], i, 0)), far(0), far(1), far(2)],
            out_specs=pl.BlockSpec((tr, cc), lambda i, mc: (mc[1] * nb + i, 0))),
        out_shape=jax.ShapeDtypeStruct((r, cc), F32),
    )(mine_c, pair_f32, got, got, got)


def _rs_pair_share(name, halves, shard_shapes):
    n_arr = len(halves)

    def body(*refs):
        ins = refs[:n_arr]
        outs = refs[n_arr:2 * n_arr]
        send_sems, recv_sems = refs[2 * n_arr:]
        x, y, c, _ = _place()
        sibling = (x, y, 1 - c)
        cps = []
        for i in range(n_arr):
            hr = shard_shapes[i][0] // 2
            rows = pl.ds(_mo(c * hr, 8), hr)
            cp = pltpu.make_async_remote_copy(
                src_ref=outs[i].at[rows, :], dst_ref=outs[i].at[rows, :],
                send_sem=send_sems.at[i], recv_sem=recv_sems.at[i],
                device_id=sibling, device_id_type=MESH)
            cp.start()
            cps.append(cp)
        for cp in cps:
            cp.wait()

    return pl.pallas_call(
        body, name=name,
        in_specs=[ANY] * n_arr, out_specs=[ANY] * n_arr,
        out_shape=[jax.ShapeDtypeStruct(s, F32) for s in shard_shapes],
        input_output_aliases={i: i for i in range(n_arr)},
        scratch_shapes=[pltpu.SemaphoreType.DMA((n_arr,)), pltpu.SemaphoreType.DMA((n_arr,))],
        compiler_params=pltpu.CompilerParams(has_side_effects=True),
    )(*halves)


def _pack(arrays):
    flat = []
    for a in arrays:
        v = a.reshape(-1).astype(F32)
        flat.append(jnp.pad(v, (0, (-v.shape[0]) % SMALL_COLS)))
    buf = jnp.concatenate(flat).reshape(-1, SMALL_COLS)
    return jnp.pad(buf, ((0, (-buf.shape[0]) % 8), (0, 0)))


def _unpack(buf, shapes):
    out = []
    row = 0
    for s in shapes:
        size = math.prod(s)
        nrow = -(-size // SMALL_COLS)
        out.append(buf[row:row + nrow].reshape(-1)[:size].reshape(s))
        row += nrow
    return out


def kernel(x, meta, norm_ab_w, w_in_ab, ret_norm_w, s5_lam_re, s5_lam_im, s5_log_dt, s5_b_re, s5_b_im, s5_c_re, s5_c_im, s5_d, s5_w_glu, w_out_ab, norm_c_w, w_in_c, gla_w_gate, gla_b_gate, gla_norm_w, w_out_c, final_norm_w, loss_target, m_meta, m_norm_ab_w, m_w_in_ab, m_ret_norm_w, m_s5_lam_re, m_s5_lam_im, m_s5_log_dt, m_s5_b_re, m_s5_b_im, m_s5_c_re, m_s5_c_im, m_s5_d, m_s5_w_glu, m_w_out_ab, m_norm_c_w, m_w_in_c, m_gla_w_gate, m_gla_b_gate, m_gla_norm_w, m_w_out_c, m_final_norm_w, v_meta, v_norm_ab_w, v_w_in_ab, v_ret_norm_w, v_s5_lam_re, v_s5_lam_im, v_s5_log_dt, v_s5_b_re, v_s5_b_im, v_s5_c_re, v_s5_c_im, v_s5_d, v_s5_w_glu, v_w_out_ab, v_norm_c_w, v_w_in_c, v_gla_w_gate, v_gla_b_gate, v_gla_norm_w, v_w_out_c, v_final_norm_w):
    seq = x.shape[1]
    rows = seq + CHUNK
    xi, yi, ci = lax.axis_index("x"), lax.axis_index("y"), lax.axis_index("c")
    mine = 2 * xi + yi
    c_arr = jnp.reshape(ci, (1,)).astype(jnp.int32)
    mine_c = jnp.stack([mine, ci]).astype(jnp.int32)

    mine_arr = jnp.reshape(mine, (1,)).astype(jnp.int32)
    small_shard = _pack([meta, norm_c_w, gla_norm_w, gla_b_gate, gla_w_gate[0]])
    first_kinds = ["col", "stack"]
    first_shapes = [w_in_ab.shape[1:], small_shard.shape]
    first_ici = _gather_ici_plan(first_shapes, first_kinds)
    first_d2d = _gather_d2d_plan(first_shapes, first_kinds)
    f_send, f_recv, f_bufs, small_shard = _copies_start(
        "gather_first_ici_start",
        [_cast_place("place_w_in_ab", w_in_ab, "col", mine_arr, BF16),
         _cast_place("place_small", small_shard, "stack", mine_arr, F32)], 6, first_ici, small_shard)
    late = [("w_out_ab", w_out_ab), ("w_in_c", w_in_c), ("w_out_c", w_out_c), ("w_glu", s5_w_glu)]
    late_kinds = ["row", "stack", "row", "row"]
    late_shapes = [a.shape[1:] for _, a in late]
    ici_plan = _gather_ici_plan(late_shapes, late_kinds)
    d2d_plan = _gather_d2d_plan(late_shapes, late_kinds)
    n_late = 3 * len(late)
    g_bufs = [_cast_place("place_" + nm, a, kd, mine_arr, BF16) for (nm, a), kd in zip(late, late_kinds)]
    cosf, sinf = _rope_tables(rows)
    rtab = _ret_tables()
    ab_re, ab_im, bb_re, bb_im = _s5_discretize(s5_lam_re[0], s5_lam_im[0], s5_log_dt[0], s5_b_re[0], s5_b_im[0])
    ab = (ab_re, ab_im)
    bd_b = (_bdiag_in(bb_re), _bdiag_in(bb_im))
    bd_c = (_bdiag_out(s5_c_re[0]), _bdiag_out(s5_c_im[0]))
    f_bufs = _copies_wait("gather_first_ici_wait", f_send, f_recv, f_bufs, first_ici,
                          [cosf, sinf, bd_b[0], bd_b[1], bd_c[0], bd_c[1]] + g_bufs + list(rtab))
    f_send, f_recv, f_bufs, cosf = _copies_start("gather_first_d2d_start", f_bufs, 6, first_d2d, cosf)
    wab, small_all = _copies_wait("gather_first_d2d_wait", f_send, f_recv, f_bufs, first_d2d, cosf)
    g_send, g_recv, g_bufs, wab = _copies_start("gather_late_ici_start", g_bufs, n_late, ici_plan, wab)
    q4 = D_MODEL // N_SHARD
    g4 = GLA_QK // N_SHARD
    parts = [_unpack(small_all[j], [(N_META, q4), (1, q4), (1, q4), (1, g4), (GLA_RANK, g4)]) for j in range(N_SHARD)]
    meta_f, norm_c_f, gla_norm_f, bgate_f, wgate_f = [jnp.concatenate([p[i] for p in parts], axis=1) for i in range(5)]
    wgate_pad = jnp.pad(wgate_f, ((0, 128 - GLA_RANK), (0, 0)))

    h0 = jnp.concatenate([jnp.zeros((PAD, D_MODEL), F32), meta_f, x[0]], axis=0)

    tm = _row_tile(rows, 1408)
    tmk = _row_tile(rows, 1408)
    hn0 = _rms_fwd("norm_ab", h0, norm_ab_w)
    proj0 = _matmul("in_proj_ab", hn0, wab, NN, rows, IN_AB, D_MODEL, tm=tm, tn=512, tk=D_MODEL)
    o_ret, o_a, ret_states = _ret_fwd(proj0, cosf, sinf, rtab, ret_norm_w)
    g_bufs = _copies_wait("gather_late_ici_wait", g_send, g_recv, g_bufs, ici_plan, o_a)
    g_send, g_recv, g_bufs, proj0 = _copies_start("gather_late_d2d_start", g_bufs, n_late, d2d_plan, proj0)
    y_s5, g_s5, s5_er, s5_ei = _s5_fwd(proj0, ab, bd_b, bd_c, s5_d)
    wout_ab, wc_st, wout_c, wglu = _copies_wait("gather_late_d2d_wait", g_send, g_recv, g_bufs, d2d_plan, g_s5)
    wc = jnp.concatenate([wc_st[j] for j in range(N_SHARD)] + [jnp.zeros((D_MODEL, IN_C_PAD - IN_C), BF16)], axis=1)
    zb_blk = (2 * RET_QK + 2 * RET_W + S5_W) // 512

    def glu_out(acc, gv, z):
        return gv.astype(F32) * _sigmoid(acc) * (z * _sigmoid(z))

    t_glu = _matmul("glu", g_s5, wglu, NN, rows, S5_W, S5_W, tm=tm, tn=512, tk=S5_W)
    o_b = _matmul("glu_out", g_s5, wglu, NN, rows, S5_W, S5_W, tm=tm, tn=512, tk=S5_W, out_dtype=BF16,
                  extras=[(g_s5, (tm, 512), lambda i, j, kk: (i, j)),
                          (proj0, (tm, 512), lambda i, j, kk: (i, zb_blk + j))],
                  epilogue=glu_out)
    mix = jnp.concatenate([o_a, o_b], axis=1)
    h1 = _matmul("out_proj_ab", mix, wout_ab, NN, rows, D_MODEL, OUT_AB, tm=tm, tn=512, tk=1024,
                 extras=[(h0, (tm, 512), lambda i, j, kk: (i, j))], epilogue=lambda acc, r: acc + r)

    hn1 = _rms_fwd("norm_c", h1, norm_c_f)
    proj1 = _matmul("in_proj_c", hn1, wc, NN, rows, IN_C_PAD, D_MODEL, tm=tm, tn=896, tk=D_MODEL)
    o_gla, o_c, gla_states = _gla_fwd(proj1, wgate_pad, bgate_f, gla_norm_f)
    h2 = _matmul("out_proj_c", o_c, wout_c, NN, rows, D_MODEL, GLA_W, tm=tm, tn=512, tk=GLA_W,
                 extras=[(h1, (tm, 512), lambda i, j, kk: (i, j))], epilogue=lambda acc, r: acc + r)
    loss_dev, dh2, d_final = _final_loss(h2, final_norm_w.reshape(1, D_MODEL), loss_target[0])

    g_wout_c = _matmul("d_w_out_c", o_c, dh2, TN, GLA_W, D_MODEL, rows, tm=1024, tn=1024, tk=tmk)
    d_oc = _matmul("d_o_c", dh2, wout_c, NT, rows, GLA_W, D_MODEL, tm=tm, tn=512, tk=1024)
    dq1, dk1, dv1, dz1, dlogit, d_gla_norm, d_bgate = _gla_bwd(proj1, wgate_pad, bgate_f, gla_norm_f, o_gla, d_oc, gla_states)
    gl_blk = (2 * GLA_QK + 2 * GLA_W) // 128
    dgl = _matmul("d_g_low", dlogit, wgate_pad, NT, rows, 128, GLA_QK, tm=tm, tn=128, tk=GLA_QK, out_dtype=BF16)
    g_wgate = _matmul("d_w_gate", proj1, dlogit, TN, 128, GLA_QK, rows, tm=128, tn=GLA_QK, tk=tmk, a_off=(0, gl_blk))
    dproj1 = jnp.concatenate([dq1, dk1, dv1, dz1, dgl], axis=1)
    g_wc = _matmul("d_w_in_c", hn1, dproj1, TN, D_MODEL, IN_C_PAD, rows, tm=1024, tn=896, tk=tmk)
    dhn1 = _matmul("d_hn1", dproj1, wc, NT, rows, D_MODEL, IN_C_PAD, tm=tm, tn=512, tk=896)
    dh1, d_norm_c = _rms_bwd("norm_c_bwd", dhn1, h1, norm_c_f, dh2)

    g_wout_ab = _matmul("d_w_out_ab", mix, dh1, TN, OUT_AB, D_MODEL, rows, tm=1024, tn=1024, tk=tmk)
    dmix = _matmul("d_mix", dh1, wout_ab, NT, rows, OUT_AB, D_MODEL, tm=tm, tn=512, tk=1024)
    dq0, dk0, dv0, dza, d_ret_norm = _ret_bwd(proj0, cosf, sinf, rtab, ret_norm_w, o_ret, dmix, ret_states)
    dzb, dt_glu, dg_direct = _s5_gate_bwd(dmix, g_s5, t_glu, proj0)
    g_wglu = _matmul("d_w_glu", g_s5, dt_glu, TN, S5_W, S5_W, rows, tm=1024, tn=1024, tk=tmk)
    dy_s5 = _matmul("d_y_s5", dt_glu, wglu, NT, rows, S5_W, S5_W, tm=tm, tn=512, tk=S5_W,
                    extras=[(dg_direct, (tm, 512), lambda i, j, kk: (i, j)),
                            (y_s5, (tm, 512), lambda i, j, kk: (i, j))],
                    epilogue=lambda acc, dg, yv: (acc + dg) * _gelu_grad(yv))
    g_wc_st = jnp.stack([g_wc[:, j * (IN_C // N_SHARD):(j + 1) * (IN_C // N_SHARD)] for j in range(N_SHARD)])
    rs1_names = ["w_out_ab", "w_in_c", "w_out_c", "w_glu"]
    rs1_shapes = [w_out_ab.shape[1:], w_in_c.shape[1:], w_out_c.shape[1:], s5_w_glu.shape[1:]]
    rs1_plan = _rs_pair_plan(late_kinds, rs1_shapes)
    rs1_land = [_empty_hbm((N_SHARD, r // 2, cc), F32) for (r, cc) in rs1_shapes]
    p_send, p_recv, p_bufs, dy_s5 = _copies_start("rs1_pair_start", [g_wout_ab, g_wc_st, g_wout_c, g_wglu] + rs1_land,
                                                  N_SHARD * 4, rs1_plan, dy_s5)
    du, dbr_d, dbi_d, dcr_d, dci_d, dar_p, dai_p, dd_p = _s5_bwd(proj0, dy_s5, ab, bd_b, bd_c, s5_d, (s5_er, s5_ei))
    p_bufs = _copies_wait("rs1_pair_wait", p_send, p_recv, p_bufs, rs1_plan, du)
    rs1_pairs = [_rs_pair_add("rs_pair_add_" + nm, g, t, kd, ss, c_arr)
                 for nm, g, t, kd, ss in zip(rs1_names, p_bufs[:4], p_bufs[4:], late_kinds, rs1_shapes)]
    dproj0 = jnp.concatenate([dq0, dk0, dv0, dza, du, dzb], axis=1)
    rs1_chip_plan = _rs_chip_plan(4)
    rs1_land2 = [_empty_hbm((3, r // 2, cc), BF16) for (r, cc) in rs1_shapes]
    c_send, c_recv, c_bufs, dproj0 = _copies_start("rs1_chip_start", [p[1] for p in rs1_pairs] + rs1_land2, 12,
                                                   rs1_chip_plan, dproj0)
    g_wab = _matmul("d_w_in_ab", hn0, dproj0, TN, D_MODEL, IN_AB, rows, tm=1024, tn=1024, tk=tmk)
    rs2_shapes = [w_in_ab.shape[1:]]
    rs2_plan = _rs_pair_plan(["col"], rs2_shapes)
    rs2_land = [_empty_hbm((N_SHARD, rs2_shapes[0][0] // 2, rs2_shapes[0][1]), F32)]
    q_send, q_recv, q_bufs, dproj0 = _copies_start("rs2_pair_start", [g_wab] + rs2_land, N_SHARD, rs2_plan, dproj0)
    dhn0 = _matmul("d_hn0", dproj0, wab, NT, rows, D_MODEL, IN_AB, tm=tm, tn=512, tk=2048)
    dh0, d_norm_ab = _rms_bwd("norm_ab_bwd", dhn0, h0, norm_ab_w, dh1)
    grad_x = dh0[CHUNK:][None]
    c_bufs = _copies_wait("rs1_chip_wait", c_send, c_recv, c_bufs, rs1_chip_plan, dh0)
    rs1_halves = [_rs_chip_add("rs_chip_add_" + nm, p[0], t, ss, mine_c)
                  for nm, p, t, ss in zip(rs1_names, rs1_pairs, c_bufs[4:], rs1_shapes)]
    g_w_out_ab, g_w_in_c, g_w_out_c, g_w_glu = _rs_pair_share("rs1_pair_share", rs1_halves, rs1_shapes)
    q_bufs = _copies_wait("rs2_pair_wait", q_send, q_recv, q_bufs, rs2_plan, g_w_glu)
    rs2_pair = _rs_pair_add("rs_pair_add_w_in_ab", q_bufs[0], q_bufs[1], "col", rs2_shapes[0], c_arr)
    rs2_chip_plan = _rs_chip_plan(1)
    rs2_land2 = [_empty_hbm((3, rs2_shapes[0][0] // 2, rs2_shapes[0][1]), BF16)]

    d_ab_re = jnp.sum(dar_p, axis=1).reshape(S5_G, S5_P)
    d_ab_im = jnp.sum(dai_p, axis=1).reshape(S5_G, S5_P)
    small_local = [loss_dev, dh0[PAD:CHUNK], d_norm_ab, d_ret_norm.reshape(1, RET_W), d_ab_re, d_ab_im,
                   _bdiag_in_extract(dbr_d), _bdiag_in_extract(dbi_d),
                   _bdiag_out_extract(dcr_d), _bdiag_out_extract(dci_d),
                   jnp.sum(dd_p, axis=1).reshape(1, S5_W), d_norm_c, g_wgate[:GLA_RANK],
                   d_bgate.reshape(1, GLA_QK), d_gla_norm.reshape(1, GLA_W), d_final]
    small_shapes = [a.shape for a in small_local]
    summed_buf = _allreduce_small(_pack(small_local))
    r_send, r_recv, r_bufs, summed_buf = _copies_start("rs2_chip_start", [rs2_pair[1]] + rs2_land2, 3, rs2_chip_plan,
                                                       summed_buf)
    summed = _unpack(summed_buf, small_shapes)
    (loss, g_meta_f, g_norm_ab, g_ret_norm, g_ab_re, g_ab_im, g_bb_re, g_bb_im, g_c_re, g_c_im, g_d,
     g_norm_c_f, g_wgate_f, g_bgate_f, g_gla_norm_f, g_final) = summed
    _, s5_vjp = jax.vjp(_s5_discretize, s5_lam_re[0], s5_lam_im[0], s5_log_dt[0], s5_b_re[0], s5_b_im[0])
    g_lam_re, g_lam_im, g_log_dt, g_b_re, g_b_im = s5_vjp((g_ab_re, g_ab_im, g_bb_re, g_bb_im))

    def take(a, width):
        return lax.dynamic_slice_in_dim(a, mine * width, width, axis=1)

    grads = {
        "meta": take(g_meta_f, q4), "norm_ab_w": g_norm_ab, "ret_norm_w": g_ret_norm,
        "s5_lam_re": g_lam_re[None], "s5_lam_im": g_lam_im[None], "s5_log_dt": g_log_dt[None],
        "s5_b_re": g_b_re[None], "s5_b_im": g_b_im[None], "s5_c_re": g_c_re[None], "s5_c_im": g_c_im[None],
        "s5_d": g_d, "s5_w_glu": g_w_glu[None], "w_out_ab": g_w_out_ab[None], "norm_c_w": take(g_norm_c_f, q4),
        "w_in_c": g_w_in_c[None], "gla_w_gate": take(g_wgate_f, g4)[None], "gla_b_gate": take(g_bgate_f, g4),
        "gla_norm_w": take(g_gla_norm_f, q4), "w_out_c": g_w_out_c[None], "final_norm_w": g_final.reshape(D_MODEL),
    }
    weights = dict(meta=meta, norm_ab_w=norm_ab_w, w_in_ab=w_in_ab, ret_norm_w=ret_norm_w, s5_lam_re=s5_lam_re,
                   s5_lam_im=s5_lam_im, s5_log_dt=s5_log_dt, s5_b_re=s5_b_re, s5_b_im=s5_b_im, s5_c_re=s5_c_re,
                   s5_c_im=s5_c_im, s5_d=s5_d, s5_w_glu=s5_w_glu, w_out_ab=w_out_ab, norm_c_w=norm_c_w,
                   w_in_c=w_in_c, gla_w_gate=gla_w_gate, gla_b_gate=gla_b_gate, gla_norm_w=gla_norm_w,
                   w_out_c=w_out_c, final_norm_w=final_norm_w)
    m_in = dict(meta=m_meta, norm_ab_w=m_norm_ab_w, w_in_ab=m_w_in_ab, ret_norm_w=m_ret_norm_w,
                s5_lam_re=m_s5_lam_re, s5_lam_im=m_s5_lam_im, s5_log_dt=m_s5_log_dt, s5_b_re=m_s5_b_re,
                s5_b_im=m_s5_b_im, s5_c_re=m_s5_c_re, s5_c_im=m_s5_c_im, s5_d=m_s5_d, s5_w_glu=m_s5_w_glu,
                w_out_ab=m_w_out_ab, norm_c_w=m_norm_c_w, w_in_c=m_w_in_c, gla_w_gate=m_gla_w_gate,
                gla_b_gate=m_gla_b_gate, gla_norm_w=m_gla_norm_w, w_out_c=m_w_out_c, final_norm_w=m_final_norm_w)
    v_in = dict(meta=v_meta, norm_ab_w=v_norm_ab_w, w_in_ab=v_w_in_ab, ret_norm_w=v_ret_norm_w,
                s5_lam_re=v_s5_lam_re, s5_lam_im=v_s5_lam_im, s5_log_dt=v_s5_log_dt, s5_b_re=v_s5_b_re,
                s5_b_im=v_s5_b_im, s5_c_re=v_s5_c_re, s5_c_im=v_s5_c_im, s5_d=v_s5_d, s5_w_glu=v_s5_w_glu,
                w_out_ab=v_w_out_ab, norm_c_w=v_norm_c_w, w_in_c=v_w_in_c, gla_w_gate=v_gla_w_gate,
                gla_b_gate=v_gla_b_gate, gla_norm_w=v_gla_norm_w, w_out_c=v_w_out_c, final_norm_w=v_final_norm_w)
    order = list(weights)
    big_names = ["s5_w_glu", "w_out_ab", "w_in_c", "w_out_c", "w_in_ab"]
    small_names = [nm for nm in order if nm not in big_names]
    delta, new_m, new_v = {}, {}, {}

    def big_update(nm):
        delta[nm], new_m[nm], new_v[nm] = _adamw("adamw_" + nm, weights[nm], grads[nm][0], m_in[nm], v_in[nm])

    for nm in big_names[:-1]:
        big_update(nm)
    sshapes = [weights[nm].shape for nm in small_names]
    d2, m2, v2 = _adamw("adamw_small", _pack([weights[nm] for nm in small_names]),
                        _pack([grads[nm] for nm in small_names]), _pack([m_in[nm] for nm in small_names]),
                        _pack([v_in[nm] for nm in small_names]))
    for nm, dd, mm, vv in zip(small_names, _unpack(d2, sshapes), _unpack(m2, sshapes), _unpack(v2, sshapes)):
        delta[nm], new_m[nm], new_v[nm] = dd, mm, vv
    r_bufs = _copies_wait("rs2_chip_wait", r_send, r_recv, r_bufs, rs2_chip_plan,
                          [v2] + [new_v[nm] for nm in big_names[:-1]])
    rs2_half = _rs_chip_add("rs_chip_add_w_in_ab", rs2_pair[0], r_bufs[1], rs2_shapes[0], mine_c)
    grads["w_in_ab"] = _rs_pair_share("rs2_pair_share", [rs2_half], rs2_shapes)[0][None]
    big_update("w_in_ab")
    grads = {nm: grads[nm].reshape(weights[nm].shape) for nm in order}
    return (loss.reshape(()), grad_x, *[grads[nm] for nm in order], *[delta[nm] for nm in order],
            *[new_m[nm] for nm in order], *[new_v[nm] for nm in order])
```

```python
import functools
import math

import jax
import jax.numpy as jnp
from jax import lax
from jax.experimental import pallas as pl
from jax.experimental.pallas import tpu as pltpu

F32 = jnp.float32
BF16 = jnp.bfloat16
MESH = pl.DeviceIdType.MESH

D_MODEL = 2048
N_META = 16
CHUNK = 128
SUB = 16
NSUB = CHUNK // SUB
PAD = CHUNK - N_META
EPS = 1e-6

RET_HEADS = 8
RET_DK = 128
RET_DV = 256
RET_QK = RET_HEADS * RET_DK
RET_W = RET_HEADS * RET_DV
ROPE_BASE = 10000.0

S5_W = 1024
S5_GH = 16
S5_G = S5_W // S5_GH
S5_P = 64
S5_TG = 8
S5_NT = S5_G // S5_TG
S5_TU = S5_TG * S5_GH
S5_TS = S5_TG * S5_P
S5_FWD_TILES = 2
S5_BWD_TILES = 1

GLA_HEADS = 4
GLA_DK = 256
GLA_DV = 512
GLA_QK = GLA_HEADS * GLA_DK
GLA_W = GLA_HEADS * GLA_DV
GLA_RANK = 16
GLA_TAU = 16.0

IN_AB = 2 * RET_QK + 2 * RET_W + 2 * S5_W
OUT_AB = RET_W + S5_W
IN_C = 2 * GLA_QK + 2 * GLA_W + GLA_RANK
IN_C_PAD = 2 * GLA_QK + 2 * GLA_W + 128

ADAM_LR = 0.001
ADAM_B1 = 0.9
ADAM_B2 = 0.999
ADAM_EPS = 1e-08
ADAM_WD = 0.01
ADAM_STEP = 10

N_SHARD = 4
SMALL_COLS = 512

NN = (((1,), (0,)), ((), ()))
NT = (((1,), (1,)), ((), ()))
TN = (((0,), (0,)), ((), ()))


def _dot(a, b, dims=NN):
    return lax.dot_general(a.astype(BF16), b.astype(BF16), dims, preferred_element_type=F32)


def _mo(v, m):
    return v if isinstance(v, int) else pl.multiple_of(v, m)


def _sigmoid(x):
    return 1.0 / (1.0 + jnp.exp(-x))


def _row_tile(rows, cap):
    n = rows // CHUNK
    best = 1
    for d in range(1, n + 1):
        if n % d == 0 and d * CHUNK <= cap:
            best = d
    return best * CHUNK


def _col_tile(cols, cap):
    n = cols // 128
    best = 1
    for d in range(1, n + 1):
        if n % d == 0 and d * 128 <= cap:
            best = d
    return best * 128


def _matmul(name, a, b, dims, m, n, k, *, tm, tn, tk, out_dtype=F32, a_off=(0, 0), b_off=(0, 0),
            extras=(), epilogue=None, out_shape=None, out_spec=None):
    nk = k // tk
    assert m % tm == 0 and n % tn == 0 and k % tk == 0, (name, m, n, k, tm, tn, tk)
    ar, ac = a_off
    br, bc = b_off
    if dims == NN:
        a_spec = pl.BlockSpec((tm, tk), lambda i, j, kk: (i + ar, kk + ac))
        b_spec = pl.BlockSpec((tk, tn), lambda i, j, kk: (kk + br, j + bc))
    elif dims == NT:
        a_spec = pl.BlockSpec((tm, tk), lambda i, j, kk: (i + ar, kk + ac))
        b_spec = pl.BlockSpec((tn, tk), lambda i, j, kk: (j + br, kk + bc))
    else:
        a_spec = pl.BlockSpec((tk, tm), lambda i, j, kk: (kk + ar, i + ac))
        b_spec = pl.BlockSpec((tk, tn), lambda i, j, kk: (kk + br, j + bc))
    n_extra = len(extras)
    if out_shape is None:
        out_shape = jax.ShapeDtypeStruct((m, n), out_dtype)

    def body(*refs):
        a_ref, b_ref = refs[0], refs[1]
        e_refs = refs[2:2 + n_extra]
        o_ref = refs[2 + n_extra]
        part = _dot(a_ref[...], b_ref[...], dims)
        if nk == 1:
            if epilogue is not None:
                part = epilogue(part, *[e[...] for e in e_refs])
            o_ref[...] = part.astype(o_ref.dtype)
            return
        acc_ref = refs[3 + n_extra]
        kk = pl.program_id(2)

        @pl.when(kk == 0)
        def _():
            acc_ref[...] = jnp.zeros_like(acc_ref)

        acc_ref[...] += part

        @pl.when(kk == nk - 1)
        def _():
            acc = acc_ref[...]
            if epilogue is not None:
                acc = epilogue(acc, *[e[...] for e in e_refs])
            o_ref[...] = acc.astype(o_ref.dtype)

    if out_spec is None:
        out_spec = pl.BlockSpec((tm, tn), lambda i, j, kk: (i, j))
    return pl.pallas_call(
        body, name=name, grid=(m // tm, n // tn, nk),
        in_specs=[a_spec, b_spec] + [pl.BlockSpec(bs, im) for (_, bs, im) in extras],
        out_specs=out_spec, out_shape=out_shape,
        scratch_shapes=[] if nk == 1 else [pltpu.VMEM((tm, tn), F32)],
        compiler_params=pltpu.CompilerParams(dimension_semantics=("parallel", "parallel", "arbitrary")),
    )(a, b, *[e for (e, _, _) in extras])


def _rms_fwd(name, h, w):
    rows, d = h.shape
    tm = _row_tile(rows, 512)

    def body(h_ref, w_ref, o_ref):
        x = h_ref[...]
        r = lax.rsqrt(jnp.mean(x * x, axis=-1, keepdims=True) + EPS)
        o_ref[...] = (x * r * w_ref[...]).astype(BF16)

    return pl.pallas_call(
        body, name=name, grid=(rows // tm,),
        in_specs=[pl.BlockSpec((tm, d), lambda i: (i, 0)), pl.BlockSpec((1, d), lambda i: (0, 0))],
        out_specs=pl.BlockSpec((tm, d), lambda i: (i, 0)),
        out_shape=jax.ShapeDtypeStruct((rows, d), BF16),
    )(h, w)


def _rms_bwd(name, dhn, h, w, dres):
    rows, d = h.shape
    tm = _row_tile(rows, 384)

    def body(g_ref, h_ref, w_ref, r_ref, dh_ref, dw_ref):
        i = pl.program_id(0)
        x = h_ref[...]
        r = lax.rsqrt(jnp.mean(x * x, axis=-1, keepdims=True) + EPS)
        xh = x * r
        g = g_ref[...]
        gw = g * w_ref[...]
        dh_ref[...] = r_ref[...] + r * (gw - xh * jnp.mean(gw * xh, axis=-1, keepdims=True))

        @pl.when(i == 0)
        def _():
            dw_ref[...] = jnp.zeros_like(dw_ref)

        dw_ref[...] += jnp.sum(g * xh, axis=0, keepdims=True)

    return pl.pallas_call(
        body, name=name, grid=(rows // tm,),
        in_specs=[pl.BlockSpec((tm, d), lambda i: (i, 0)), pl.BlockSpec((tm, d), lambda i: (i, 0)),
                  pl.BlockSpec((1, d), lambda i: (0, 0)), pl.BlockSpec((tm, d), lambda i: (i, 0))],
        out_specs=[pl.BlockSpec((tm, d), lambda i: (i, 0)), pl.BlockSpec((1, d), lambda i: (0, 0))],
        out_shape=[jax.ShapeDtypeStruct((rows, d), F32), jax.ShapeDtypeStruct((1, d), F32)],
    )(dhn, h, w, dres)


def _final_loss(h2, w, target):
    rows, d = h2.shape

    def body(h_ref, w_ref, t_ref, loss_ref, dh_ref, dw_ref):
        i = pl.program_id(0)

        @pl.when(i == 0)
        def _():
            loss_ref[...] = jnp.zeros_like(loss_ref)
            dw_ref[...] = jnp.zeros_like(dw_ref)
            dh_ref[...] = jnp.zeros_like(dh_ref)

        @pl.when(i > 0)
        def _():
            x = h_ref[...]
            r = lax.rsqrt(jnp.mean(x * x, axis=-1, keepdims=True) + EPS)
            xh = x * r
            wv = w_ref[...]
            err = xh * wv - t_ref[...]
            loss_ref[...] += 0.5 * jnp.sum(jnp.mean(err * err, axis=-1, keepdims=True), axis=0, keepdims=True)
            g = err * (1.0 / d)
            gw = g * wv
            dh_ref[...] = r * (gw - xh * jnp.mean(gw * xh, axis=-1, keepdims=True))
            dw_ref[...] += jnp.sum(g * xh, axis=0, keepdims=True)

    return pl.pallas_call(
        body, name="final_loss", grid=(rows // CHUNK,),
        in_specs=[pl.BlockSpec((CHUNK, d), lambda i: (i, 0)), pl.BlockSpec((1, d), lambda i: (0, 0)),
                  pl.BlockSpec((CHUNK, d), lambda i: (jnp.maximum(i - 1, 0), 0))],
        out_specs=[pl.BlockSpec((1, 1), lambda i: (0, 0)), pl.BlockSpec((CHUNK, d), lambda i: (i, 0)),
                   pl.BlockSpec((1, d), lambda i: (0, 0))],
        out_shape=[jax.ShapeDtypeStruct((1, 1), F32), jax.ShapeDtypeStruct((rows, d), F32),
                   jax.ShapeDtypeStruct((1, d), F32)],
    )(h2, w, target)


def _gate_fwd(o, z, w):
    rs = lax.rsqrt(jnp.mean(o * o, axis=-1, keepdims=True) + EPS)
    return o * rs * w * (z * _sigmoid(z))


def _gate_bwd(dout, o, z, w):
    rs = lax.rsqrt(jnp.mean(o * o, axis=-1, keepdims=True) + EPS)
    yn = o * rs
    sg = _sigmoid(z)
    sil = z * sg
    dsil = sg * (1.0 + z * (1.0 - sg))
    dz = dout * yn * w * dsil
    dyn = dout * w * sil
    dw = jnp.sum(dout * yn * sil, axis=0, keepdims=True)
    do = rs * (dyn - yn * jnp.mean(dyn * yn, axis=-1, keepdims=True))
    return do, dz, dw


def _rope(t, cosf, sinf):
    return t * cosf + pltpu.roll(t, RET_DK // 2, 1) * sinf


def _rope_t(d, cosf, sinf):
    return d * cosf + pltpu.roll(d * sinf, RET_DK // 2, 1)


def _ret_tables():
    log_g = jnp.log1p(-jnp.exp2(-5.0 - jnp.arange(RET_HEADS, dtype=F32)))
    idx = jnp.arange(CHUNK, dtype=F32)
    diff = idx[:, None] - idx[None, :]
    decay = jnp.where(diff >= 0, jnp.exp(log_g[:, None, None] * jnp.maximum(diff, 0.0)), 0.0)
    kw = jnp.exp(log_g[:, None] * (CHUNK - 1 - idx))
    qw = jnp.exp(log_g[:, None] * (idx + 1.0))
    gch = jnp.exp(log_g * CHUNK)
    kw = jnp.broadcast_to(kw[:, :, None], (RET_HEADS, CHUNK, RET_DK))
    qw = jnp.broadcast_to(qw[:, :, None], (RET_HEADS, CHUNK, RET_DK))
    gch = jnp.broadcast_to(gch[:, None, None], (RET_HEADS, 1, RET_DV))
    return decay, kw, qw, gch


def _rope_tables(rows):
    pos = jnp.arange(rows, dtype=F32) - float(PAD)
    inv_freq = jnp.power(ROPE_BASE, -jnp.arange(0, RET_DK, 2, dtype=F32) / RET_DK)
    ang = pos[:, None] * inv_freq[None, :]
    cos, sin = jnp.cos(ang), jnp.sin(ang)
    return jnp.concatenate([cos, cos], axis=1), jnp.concatenate([-sin, sin], axis=1)


RET_HB = 4
RET_QB = RET_HB * RET_DK
RET_VB = RET_HB * RET_DV


def _ret_in_specs(rev, nc):
    def cn(n):
        return (nc - 1 - n) if rev else n
    kb = RET_QK // RET_QB
    vb = 2 * RET_QK // RET_VB
    zb = (2 * RET_QK + RET_W) // RET_VB
    return [
        pl.BlockSpec((CHUNK, RET_QB), lambda h, n: (cn(n), h)),
        pl.BlockSpec((CHUNK, RET_QB), lambda h, n: (cn(n), kb + h)),
        pl.BlockSpec((CHUNK, RET_VB), lambda h, n: (cn(n), vb + h)),
        pl.BlockSpec((CHUNK, RET_VB), lambda h, n: (cn(n), zb + h)),
        pl.BlockSpec((CHUNK, RET_DK), lambda h, n: (cn(n), 0)),
        pl.BlockSpec((CHUNK, RET_DK), lambda h, n: (cn(n), 0)),
        pl.BlockSpec((RET_HB, CHUNK, CHUNK), lambda h, n: (h, 0, 0)),
        pl.BlockSpec((RET_HB, CHUNK, RET_DK), lambda h, n: (h, 0, 0)),
        pl.BlockSpec((RET_HB, CHUNK, RET_DK), lambda h, n: (h, 0, 0)),
        pl.BlockSpec((RET_HB, 1, RET_DV), lambda h, n: (h, 0, 0)),
        pl.BlockSpec((1, RET_VB), lambda h, n: (0, h)),
    ]


def _ret_fwd(proj, cosf, sinf, tables, normw):
    rows = proj.shape[0]
    nc = rows // CHUNK
    decay, kw, qw, gch = tables

    def body(q_ref, k_ref, v_ref, z_ref, cos_ref, sin_ref, dm_ref, kw_ref, qw_ref, g_ref, w_ref,
             o_ref, oa_ref, st_ref, s_scr):
        n = pl.program_id(1)

        @pl.when(n == 0)
        def _():
            s_scr[...] = jnp.zeros_like(s_scr)

        cosv, sinv = cos_ref[...], sin_ref[...]
        for hh in range(RET_HB):
            qc = slice(hh * RET_DK, (hh + 1) * RET_DK)
            vc = slice(hh * RET_DV, (hh + 1) * RET_DV)
            q = _rope(q_ref[:, qc], cosv, sinv)
            k = _rope(k_ref[:, qc], cosv, sinv) * (RET_DK ** -0.5)
            v = v_ref[:, vc]
            s = s_scr[hh]
            st_ref[hh, 0] = s.astype(BF16)
            a = _dot(q, k, NT) * dm_ref[hh]
            o = _dot(a, v) + _dot(q * qw_ref[hh], s)
            s_scr[hh] = s * g_ref[hh] + _dot(k * kw_ref[hh], v, TN)
            o_ref[:, vc] = o
            oa_ref[:, vc] = _gate_fwd(o, z_ref[:, vc], w_ref[:, vc]).astype(BF16)

    return pl.pallas_call(
        body, name="ret_fwd", grid=(RET_HEADS // RET_HB, nc),
        in_specs=_ret_in_specs(False, nc),
        out_specs=[pl.BlockSpec((CHUNK, RET_VB), lambda h, n: (n, h)),
                   pl.BlockSpec((CHUNK, RET_VB), lambda h, n: (n, h)),
                   pl.BlockSpec((RET_HB, 1, RET_DK, RET_DV), lambda h, n: (h, n, 0, 0))],
        out_shape=[jax.ShapeDtypeStruct((rows, RET_W), F32), jax.ShapeDtypeStruct((rows, RET_W), BF16),
                   jax.ShapeDtypeStruct((RET_HEADS, nc, RET_DK, RET_DV), BF16)],
        scratch_shapes=[pltpu.VMEM((RET_HB, RET_DK, RET_DV), F32)],
        compiler_params=pltpu.CompilerParams(dimension_semantics=("parallel", "arbitrary")),
    )(proj, proj, proj, proj, cosf, sinf, decay, kw, qw, gch, normw)


def _ret_bwd(proj, cosf, sinf, tables, normw, o_ret, dmix, states):
    rows = proj.shape[0]
    nc = rows // CHUNK
    decay, kw, qw, gch = tables

    def rn(n):
        return nc - 1 - n

    def body(q_ref, k_ref, v_ref, z_ref, cos_ref, sin_ref, dm_ref, kw_ref, qw_ref, g_ref, w_ref,
             o_ref, do_ref, st_ref, dq_ref, dk_ref, dv_ref, dz_ref, dw_ref, ds_scr):
        n = pl.program_id(1)

        @pl.when(n == 0)
        def _():
            ds_scr[...] = jnp.zeros_like(ds_scr)
            dw_ref[...] = jnp.zeros_like(dw_ref)

        cosv, sinv = cos_ref[...], sin_ref[...]
        for hh in range(RET_HB):
            qc = slice(hh * RET_DK, (hh + 1) * RET_DK)
            vc = slice(hh * RET_DV, (hh + 1) * RET_DV)
            q = _rope(q_ref[:, qc], cosv, sinv)
            k = _rope(k_ref[:, qc], cosv, sinv) * (RET_DK ** -0.5)
            v = v_ref[:, vc]
            do, dz, dw = _gate_bwd(do_ref[:, vc], o_ref[:, vc], z_ref[:, vc], w_ref[:, vc])
            dz_ref[:, vc] = dz.astype(BF16)
            dw_ref[hh] += dw
            dm = dm_ref[hh]
            s = st_ref[hh, 0]
            g1 = ds_scr[hh]
            p = _dot(q, k, NT) * dm
            kwv = k * kw_ref[hh]
            qwv = q * qw_ref[hh]
            dp = _dot(do, v, NT)
            da = dp * dm
            dv = _dot(p, do, TN) + _dot(kwv, g1)
            dq = _dot(da, k) + _dot(do, s, NT) * qw_ref[hh]
            dk = _dot(da, q, TN) + _dot(v, g1, NT) * kw_ref[hh]
            ds_scr[hh] = g1 * g_ref[hh] + _dot(qwv, do, TN)
            dv_ref[:, vc] = dv.astype(BF16)
            dq_ref[:, qc] = _rope_t(dq, cosv, sinv).astype(BF16)
            dk_ref[:, qc] = _rope_t(dk * (RET_DK ** -0.5), cosv, sinv).astype(BF16)

    in_specs = _ret_in_specs(True, nc) + [
        pl.BlockSpec((CHUNK, RET_VB), lambda h, n: (rn(n), h)),
        pl.BlockSpec((CHUNK, RET_VB), lambda h, n: (rn(n), h)),
        pl.BlockSpec((RET_HB, 1, RET_DK, RET_DV), lambda h, n: (h, rn(n), 0, 0)),
    ]
    return pl.pallas_call(
        body, name="ret_bwd", grid=(RET_HEADS // RET_HB, nc),
        in_specs=in_specs,
        out_specs=[pl.BlockSpec((CHUNK, RET_QB), lambda h, n: (rn(n), h)),
                   pl.BlockSpec((CHUNK, RET_QB), lambda h, n: (rn(n), h)),
                   pl.BlockSpec((CHUNK, RET_VB), lambda h, n: (rn(n), h)),
                   pl.BlockSpec((CHUNK, RET_VB), lambda h, n: (rn(n), h)),
                   pl.BlockSpec((RET_HB, 1, RET_DV), lambda h, n: (h, 0, 0))],
        out_shape=[jax.ShapeDtypeStruct((rows, RET_QK), BF16), jax.ShapeDtypeStruct((rows, RET_QK), BF16),
                   jax.ShapeDtypeStruct((rows, RET_W), BF16), jax.ShapeDtypeStruct((rows, RET_W), BF16),
                   jax.ShapeDtypeStruct((RET_HEADS, 1, RET_DV), F32)],
        scratch_shapes=[pltpu.VMEM((RET_HB, RET_DK, RET_DV), F32)],
        compiler_params=pltpu.CompilerParams(dimension_semantics=("parallel", "arbitrary")),
    )(proj, proj, proj, proj, cosf, sinf, decay, kw, qw, gch, normw, o_ret, dmix, states)


def _s5_discretize(lam_re, lam_im, log_dt, b_re, b_im):
    dt = jnp.exp(log_dt)[:, None]
    mag = jnp.exp(lam_re * dt)
    ab_re, ab_im = mag * jnp.cos(lam_im * dt), mag * jnp.sin(lam_im * dt)
    den = lam_re * lam_re + lam_im * lam_im
    nr, ni = ab_re - 1.0, ab_im
    f_re = (nr * lam_re + ni * lam_im) / den
    f_im = (ni * lam_re - nr * lam_im) / den
    bb_re = f_re[..., None] * b_re - f_im[..., None] * b_im
    bb_im = f_re[..., None] * b_im + f_im[..., None] * b_re
    return ab_re, ab_im, bb_re, bb_im


def _bdiag_in(bb):
    t = bb.reshape(S5_NT, S5_TG, S5_P, S5_GH).transpose(0, 1, 3, 2)
    eye = jnp.eye(S5_TG, dtype=bb.dtype)
    full = t[:, :, :, None, :] * eye[None, :, None, :, None]
    return full.reshape(S5_NT, S5_TU, S5_TS)


def _bdiag_in_extract(dense):
    t = dense.reshape(S5_NT, S5_TG, S5_GH, S5_TG, S5_P)
    diag = jnp.stack([t[:, g, :, g, :] for g in range(S5_TG)], axis=1)
    return diag.transpose(0, 1, 3, 2).reshape(S5_G, S5_P, S5_GH)


def _bdiag_out(c):
    t = c.reshape(S5_NT, S5_TG, S5_GH, S5_P).transpose(0, 1, 3, 2)
    eye = jnp.eye(S5_TG, dtype=c.dtype)
    full = t[:, :, :, None, :] * eye[None, :, None, :, None]
    return full.reshape(S5_NT, S5_TS, S5_TU)


def _bdiag_out_extract(dense):
    t = dense.reshape(S5_NT, S5_TG, S5_P, S5_TG, S5_GH)
    diag = jnp.stack([t[:, g, :, g, :] for g in range(S5_TG)], axis=1)
    return diag.transpose(0, 1, 3, 2).reshape(S5_G, S5_GH, S5_P)


def _cmul(ar, ai, br, bi):
    return ar * br - ai * bi, ar * bi + ai * br


S5_SEG = 8
S5_STEPS = CHUNK // S5_SEG


def _seg_perm(x):
    c = x.shape[1]
    return jnp.swapaxes(x.reshape(S5_SEG, S5_STEPS, c), 0, 1).reshape(CHUNK, c)


def _seg_unperm(x):
    c = x.shape[1]
    return jnp.swapaxes(x.reshape(S5_STEPS, S5_SEG, c), 0, 1).reshape(CHUNK, c)


def _rows(x, p):
    return x[p * S5_SEG:(p + 1) * S5_SEG]


def _s5_tables(ar, ai, tr_scr, ti_scr, wfr_scr, wfi_scr, wbr_scr, wbi_scr):
    row = lax.broadcasted_iota(jnp.int32, (S5_SEG, 1), 0)
    a8r = jnp.broadcast_to(ar, (S5_SEG, S5_TS))
    a8i = jnp.broadcast_to(ai, (S5_SEG, S5_TS))
    pr, pi = a8r, a8i
    for p in range(S5_STEPS):
        tr_scr[p * S5_SEG:(p + 1) * S5_SEG, :] = pr
        ti_scr[p * S5_SEG:(p + 1) * S5_SEG, :] = pi
        if p < S5_STEPS - 1:
            pr, pi = _cmul(pr, pi, a8r, a8i)
    wr, wi = pr, pi
    sh = 1
    while sh < S5_SEG:
        keep = row >= sh
        sr = jnp.where(keep, pltpu.roll(wr, sh, 0), 1.0)
        si = jnp.where(keep, pltpu.roll(wi, sh, 0), 0.0)
        wr, wi = _cmul(wr, wi, sr, si)
        sh *= 2
    wfr_scr[...] = wr
    wfi_scr[...] = wi
    wr, wi = pr, -pi
    sh = 1
    while sh < S5_SEG:
        keep = row < S5_SEG - sh
        sr = jnp.where(keep, pltpu.roll(wr, S5_SEG - sh, 0), 1.0)
        si = jnp.where(keep, pltpu.roll(wi, S5_SEG - sh, 0), 0.0)
        wr, wi = _cmul(wr, wi, sr, si)
        sh *= 2
    wbr_scr[...] = wr
    wbi_scr[...] = wi


def _seg_scan(vr, vi, ar, ai, tr_scr, ti_scr, wr_scr, wi_scr, c0r, c0i, down):
    row = lax.broadcasted_iota(jnp.int32, (S5_SEG, 1), 0)
    sgn = 1.0 if down else -1.0
    order = list(range(S5_STEPS)) if down else list(range(S5_STEPS - 1, -1, -1))
    xr, xi = _rows(vr, order[0]), _rows(vi, order[0])
    loc = {order[0]: (xr, xi)}
    for p in order[1:]:
        mr, mi = _cmul(ar, sgn * ai, xr, xi)
        xr, xi = mr + _rows(vr, p), mi + _rows(vi, p)
        loc[p] = (xr, xi)
    last = S5_STEPS - 1
    mr, mi = tr_scr[last * S5_SEG:(last + 1) * S5_SEG, :], sgn * ti_scr[last * S5_SEG:(last + 1) * S5_SEG, :]
    er, ei = xr, xi
    sh = 1
    while sh < S5_SEG:
        if down:
            keep = row >= sh
            sr, si = pltpu.roll(er, sh, 0), pltpu.roll(ei, sh, 0)
        else:
            keep = row < S5_SEG - sh
            sr, si = pltpu.roll(er, S5_SEG - sh, 0), pltpu.roll(ei, S5_SEG - sh, 0)
        pr, pi = _cmul(mr, mi, jnp.where(keep, sr, 0.0), jnp.where(keep, si, 0.0))
        er, ei = er + pr, ei + pi
        mr, mi = _cmul(mr, mi, mr, mi)
        sh *= 2
    pr, pi = _cmul(wr_scr[...], wi_scr[...], c0r, c0i)
    er, ei = er + pr, ei + pi
    if down:
        nr = jnp.where(row == 0, c0r, pltpu.roll(er, 1, 0))
        ni = jnp.where(row == 0, c0i, pltpu.roll(ei, 1, 0))
    else:
        nr = jnp.where(row == S5_SEG - 1, c0r, pltpu.roll(er, S5_SEG - 1, 0))
        ni = jnp.where(row == S5_SEG - 1, c0i, pltpu.roll(ei, S5_SEG - 1, 0))
    out_r, out_i = [], []
    for p in range(S5_STEPS):
        q = p if down else S5_STEPS - 1 - p
        pr, pi = _cmul(tr_scr[q * S5_SEG:(q + 1) * S5_SEG, :], sgn * ti_scr[q * S5_SEG:(q + 1) * S5_SEG, :], nr, ni)
        out_r.append(loc[p][0] + pr)
        out_i.append(loc[p][1] + pi)
    return jnp.concatenate(out_r, axis=0), jnp.concatenate(out_i, axis=0), (nr, ni), (er, ei)


def _gelu(y):
    c = math.sqrt(2.0 / math.pi)
    return 0.5 * y * (1.0 + jnp.tanh(c * (y + 0.044715 * y * y * y)))


def _gelu_grad(y):
    c = math.sqrt(2.0 / math.pi)
    th = jnp.tanh(c * (y + 0.044715 * y * y * y))
    return 0.5 * (1.0 + th) + 0.5 * y * (1.0 - th * th) * c * (1.0 + 3.0 * 0.044715 * y * y)


def _s5_fwd(proj, ab, bd_b, bd_c, dvec):
    rows = proj.shape[0]
    nc = rows // CHUNK
    tps = S5_FWD_TILES
    ubw = tps * S5_TU
    ub = (2 * RET_QK + 2 * RET_W) // ubw
    ab_re, ab_im = ab
    bre, bim = bd_b
    cre, cim = bd_c

    def body(u_ref, ar_ref, ai_ref, bre_ref, bim_ref, cre_ref, cim_ref, d_ref,
             y_ref, g_ref, er_ref, ei_ref, tr_scr, ti_scr, wfr_scr, wfi_scr, wbr_scr, wbi_scr,
             cr_scr, ci_scr, er_scr, ei_scr):
        n = pl.program_id(1)
        for tt in range(tps):
            cols = slice(tt * S5_TU, (tt + 1) * S5_TU)
            ar, ai = ar_ref[tt], ai_ref[tt]
            trs, tis, wfr, wfi = tr_scr.at[tt], ti_scr.at[tt], wfr_scr.at[tt], wfi_scr.at[tt]

            @pl.when(n == 0)
            def _(tt=tt, ar=ar, ai=ai, trs=trs, tis=tis, wfr=wfr, wfi=wfi):
                _s5_tables(ar, ai, trs, tis, wfr, wfi, wbr_scr.at[tt], wbi_scr.at[tt])
                cr_scr[tt] = jnp.zeros((S5_SEG, S5_TS), F32)
                ci_scr[tt] = jnp.zeros((S5_SEG, S5_TS), F32)

            u = _seg_perm(u_ref[:, cols])
            c0r, c0i = cr_scr[tt], ci_scr[tt]
            er_ref[tt, 0] = c0r
            ei_ref[tt, 0] = c0i
            xr, xi, _, (er, ei) = _seg_scan(_dot(u, bre_ref[tt]), _dot(u, bim_ref[tt]), ar, ai, trs, tis,
                                            wfr, wfi, c0r, c0i, True)
            er_scr[tt] = er
            ei_scr[tt] = ei
            cr_scr[tt] = jnp.broadcast_to(er_scr[tt, S5_SEG - 1:S5_SEG, :], (S5_SEG, S5_TS))
            ci_scr[tt] = jnp.broadcast_to(ei_scr[tt, S5_SEG - 1:S5_SEG, :], (S5_SEG, S5_TS))
            y = _seg_unperm(_dot(xr, cre_ref[tt]) - _dot(xi, cim_ref[tt]) + d_ref[:, cols] * u)
            y_ref[:, cols] = y
            g_ref[:, cols] = _gelu(y).astype(BF16)

    vec = pl.BlockSpec((tps, 1, S5_TS), lambda t, n: (t, 0, 0))
    return pl.pallas_call(
        body, name="s5_fwd", grid=(S5_NT // tps, nc),
        in_specs=[pl.BlockSpec((CHUNK, ubw), lambda t, n: (n, ub + t)), vec, vec,
                  pl.BlockSpec((tps, S5_TU, S5_TS), lambda t, n: (t, 0, 0)),
                  pl.BlockSpec((tps, S5_TU, S5_TS), lambda t, n: (t, 0, 0)),
                  pl.BlockSpec((tps, S5_TS, S5_TU), lambda t, n: (t, 0, 0)),
                  pl.BlockSpec((tps, S5_TS, S5_TU), lambda t, n: (t, 0, 0)),
                  pl.BlockSpec((1, ubw), lambda t, n: (0, t))],
        out_specs=[pl.BlockSpec((CHUNK, ubw), lambda t, n: (n, t)),
                   pl.BlockSpec((CHUNK, ubw), lambda t, n: (n, t)),
                   pl.BlockSpec((tps, 1, 8, S5_TS), lambda t, n: (t, n, 0, 0)),
                   pl.BlockSpec((tps, 1, 8, S5_TS), lambda t, n: (t, n, 0, 0))],
        out_shape=[jax.ShapeDtypeStruct((rows, S5_W), F32), jax.ShapeDtypeStruct((rows, S5_W), BF16),
                   jax.ShapeDtypeStruct((S5_NT, nc, 8, S5_TS), F32),
                   jax.ShapeDtypeStruct((S5_NT, nc, 8, S5_TS), F32)],
        scratch_shapes=[pltpu.VMEM((tps, CHUNK, S5_TS), F32) for _ in range(2)]
        + [pltpu.VMEM((tps, S5_SEG, S5_TS), F32) for _ in range(8)],
        compiler_params=pltpu.CompilerParams(dimension_semantics=("parallel", "arbitrary")),
    )(proj, ab_re.reshape(S5_NT, 1, S5_TS), ab_im.reshape(S5_NT, 1, S5_TS), bre, bim, cre, cim, dvec)


def _s5_bwd(proj, dy, ab, bd_b, bd_c, dvec, entry):
    rows = proj.shape[0]
    nc = rows // CHUNK
    tps = S5_BWD_TILES
    ubw = tps * S5_TU
    ub = (2 * RET_QK + 2 * RET_W) // ubw
    ab_re, ab_im = ab
    bre, bim = bd_b
    cre, cim = bd_c
    er, ei = entry

    def rn(n):
        return nc - 1 - n

    def body(u_ref, dy_ref, ar_ref, ai_ref, bre_ref, bim_ref, cre_ref, cim_ref, d_ref, er_ref, ei_ref,
             du_ref, dbr_ref, dbi_ref, dcr_ref, dci_ref, dar_ref, dai_ref, dd_ref,
             tr_scr, ti_scr, wfr_scr, wfi_scr, wbr_scr, wbi_scr, gr_scr, gi_scr, er_scr, ei_scr):
        n = pl.program_id(1)

        @pl.when(n == 0)
        def _():
            gr_scr[...] = jnp.zeros_like(gr_scr)
            gi_scr[...] = jnp.zeros_like(gi_scr)
            for r in (dbr_ref, dbi_ref, dcr_ref, dci_ref, dar_ref, dai_ref, dd_ref):
                r[...] = jnp.zeros_like(r)

        for tt in range(tps):
            cols = slice(tt * S5_TU, (tt + 1) * S5_TU)
            ar, ai = ar_ref[tt], ai_ref[tt]
            trs, tis = tr_scr.at[tt], ti_scr.at[tt]

            @pl.when(n == 0)
            def _(tt=tt, ar=ar, ai=ai, trs=trs, tis=tis):
                _s5_tables(ar, ai, trs, tis, wfr_scr.at[tt], wfi_scr.at[tt], wbr_scr.at[tt], wbi_scr.at[tt])

            u = _seg_perm(u_ref[:, cols])
            dy = _seg_perm(dy_ref[:, cols])
            xr, xi, (pr, pi), _ = _seg_scan(_dot(u, bre_ref[tt]), _dot(u, bim_ref[tt]), ar, ai, trs, tis,
                                            wfr_scr.at[tt], wfi_scr.at[tt], er_ref[tt, 0], ei_ref[tt, 0], True)
            dcr_ref[tt] += _dot(xr, dy, TN)
            dci_ref[tt] -= _dot(xi, dy, TN)
            gr, gi, _, (er, ei) = _seg_scan(_dot(dy, cre_ref[tt], NT), -_dot(dy, cim_ref[tt], NT), ar, ai, trs, tis,
                                            wbr_scr.at[tt], wbi_scr.at[tt], gr_scr[tt], gi_scr[tt], False)
            er_scr[tt] = er
            ei_scr[tt] = ei
            gr_scr[tt] = jnp.broadcast_to(er_scr[tt, 0:1, :], (S5_SEG, S5_TS))
            gi_scr[tt] = jnp.broadcast_to(ei_scr[tt, 0:1, :], (S5_SEG, S5_TS))
            xpr = jnp.concatenate([pr, xr[:CHUNK - S5_SEG]], axis=0)
            xpi = jnp.concatenate([pi, xi[:CHUNK - S5_SEG]], axis=0)
            dar_ref[tt] += jnp.sum((xpr * gr + xpi * gi).reshape(S5_STEPS, S5_SEG, S5_TS), axis=0)
            dai_ref[tt] += jnp.sum((xpr * gi - xpi * gr).reshape(S5_STEPS, S5_SEG, S5_TS), axis=0)
            dbr_ref[tt] += _dot(u, gr, TN)
            dbi_ref[tt] += _dot(u, gi, TN)
            dd_ref[tt] += jnp.sum((dy * u).reshape(S5_STEPS, S5_SEG, S5_TU), axis=0)
            du = dy * d_ref[:, cols] + _dot(gr, bre_ref[tt], NT) + _dot(gi, bim_ref[tt], NT)
            du_ref[:, cols] = _seg_unperm(du).astype(BF16)

    vec = pl.BlockSpec((tps, 1, S5_TS), lambda t, n: (t, 0, 0))
    acc_b = pl.BlockSpec((tps, S5_TU, S5_TS), lambda t, n: (t, 0, 0))
    acc_c = pl.BlockSpec((tps, S5_TS, S5_TU), lambda t, n: (t, 0, 0))
    acc_a = pl.BlockSpec((tps, 8, S5_TS), lambda t, n: (t, 0, 0))
    ent = pl.BlockSpec((tps, 1, 8, S5_TS), lambda t, n: (t, rn(n), 0, 0))
    return pl.pallas_call(
        body, name="s5_bwd", grid=(S5_NT // tps, nc),
        in_specs=[pl.BlockSpec((CHUNK, ubw), lambda t, n: (rn(n), ub + t)),
                  pl.BlockSpec((CHUNK, ubw), lambda t, n: (rn(n), t)), vec, vec,
                  acc_b, acc_b, acc_c, acc_c, pl.BlockSpec((1, ubw), lambda t, n: (0, t)), ent, ent],
        out_specs=[pl.BlockSpec((CHUNK, ubw), lambda t, n: (rn(n), t)), acc_b, acc_b, acc_c, acc_c, acc_a, acc_a,
                   pl.BlockSpec((tps, 8, S5_TU), lambda t, n: (t, 0, 0))],
        out_shape=[jax.ShapeDtypeStruct((rows, S5_W), BF16),
                   jax.ShapeDtypeStruct((S5_NT, S5_TU, S5_TS), F32), jax.ShapeDtypeStruct((S5_NT, S5_TU, S5_TS), F32),
                   jax.ShapeDtypeStruct((S5_NT, S5_TS, S5_TU), F32), jax.ShapeDtypeStruct((S5_NT, S5_TS, S5_TU), F32),
                   jax.ShapeDtypeStruct((S5_NT, 8, S5_TS), F32), jax.ShapeDtypeStruct((S5_NT, 8, S5_TS), F32),
                   jax.ShapeDtypeStruct((S5_NT, 8, S5_TU), F32)],
        scratch_shapes=[pltpu.VMEM((tps, CHUNK, S5_TS), F32) for _ in range(2)]
        + [pltpu.VMEM((tps, S5_SEG, S5_TS), F32) for _ in range(8)],
        compiler_params=pltpu.CompilerParams(dimension_semantics=("parallel", "arbitrary")),
    )(proj, dy,ab_re.reshape(S5_NT, 1, S5_TS), ab_im.reshape(S5_NT, 1, S5_TS), bre, bim, cre, cim, dvec, er, ei)


def _s5_gate_bwd(dmix, g, t, proj):
    rows = g.shape[0]
    tm = _row_tile(rows, 384)
    ob = RET_W // S5_W
    zb = (2 * RET_QK + 2 * RET_W + S5_W) // S5_W

    def body(do_ref, g_ref, t_ref, z_ref, dz_ref, dt_ref, dg_ref):
        do = do_ref[...]
        gv = g_ref[...].astype(F32)
        z = z_ref[...]
        st = _sigmoid(t_ref[...])
        sg = _sigmoid(z)
        os5 = gv * st
        dz_ref[...] = (do * os5 * sg * (1.0 + z * (1.0 - sg))).astype(BF16)
        dos = do * z * sg
        dt_ref[...] = (dos * gv * st * (1.0 - st)).astype(BF16)
        dg_ref[...] = dos * st

    blk = pl.BlockSpec((tm, S5_W), lambda i: (i, 0))
    return pl.pallas_call(
        body, name="s5_gate_bwd", grid=(rows // tm,),
        in_specs=[pl.BlockSpec((tm, S5_W), lambda i: (i, ob)), blk, blk,
                  pl.BlockSpec((tm, S5_W), lambda i: (i, zb))],
        out_specs=[blk, blk, blk],
        out_shape=[jax.ShapeDtypeStruct((rows, S5_W), BF16), jax.ShapeDtypeStruct((rows, S5_W), BF16),
                   jax.ShapeDtypeStruct((rows, S5_W), F32)],
    )(dmix, g, t, proj)


def _split3(x):
    hi = x.astype(BF16)
    r = x - hi.astype(F32)
    mid = r.astype(BF16)
    lo = (r - mid.astype(F32)).astype(BF16)
    return hi, mid, lo


def _tri_sum(x, upper):
    i = lax.broadcasted_iota(jnp.int32, (CHUNK, CHUNK), 0)
    j = lax.broadcasted_iota(jnp.int32, (CHUNK, CHUNK), 1)
    tri = jnp.where((j >= i) if upper else (j <= i), 1.0, 0.0).astype(BF16)
    hi, mid, lo = _split3(x)
    return _dot(tri, lo) + _dot(tri, mid) + _dot(tri, hi)


def _gla_log_decay(gl, wg, bg, n):
    logit = _dot(gl, wg) + bg
    la = (jnp.minimum(logit, 0.0) - jnp.log(1.0 + jnp.exp(-jnp.abs(logit)))) * (1.0 / GLA_TAU)
    row = lax.broadcasted_iota(jnp.int32, (CHUNK, 1), 0)
    live = jnp.logical_or(n > 0, row >= PAD)
    return logit, jnp.where(live, la, 0.0), live


def _gla_in_specs(rev, nc):
    def cn(n):
        return (nc - 1 - n) if rev else n
    kb = GLA_QK // GLA_DK
    vb = 2 * GLA_QK // GLA_DV
    zb = (2 * GLA_QK + GLA_W) // GLA_DV
    gb = (2 * GLA_QK + 2 * GLA_W) // 128
    return [
        pl.BlockSpec((CHUNK, GLA_DK), lambda h, n: (cn(n), h)),
        pl.BlockSpec((CHUNK, GLA_DK), lambda h, n: (cn(n), kb + h)),
        pl.BlockSpec((CHUNK, GLA_DV), lambda h, n: (cn(n), vb + h)),
        pl.BlockSpec((CHUNK, GLA_DV), lambda h, n: (cn(n), zb + h)),
        pl.BlockSpec((CHUNK, 128), lambda h, n: (cn(n), gb)),
        pl.BlockSpec((128, GLA_DK), lambda h, n: (0, h)),
        pl.BlockSpec((1, GLA_DK), lambda h, n: (0, h)),
        pl.BlockSpec((1, GLA_DV), lambda h, n: (0, h)),
    ]


def _gla_fwd(proj, wgate, bgate, normw):
    rows = proj.shape[0]
    nc = rows // CHUNK

    def body(q_ref, k_ref, v_ref, z_ref, gl_ref, wg_ref, bg_ref, w_ref, o_ref, oc_ref, st_ref, s_scr, b_scr):
        n = pl.program_id(1)

        @pl.when(n == 0)
        def _():
            s_scr[...] = jnp.zeros_like(s_scr)

        q = q_ref[...] * (GLA_DK ** -0.5)
        k = k_ref[...]
        v = v_ref[...]
        vb = v.astype(BF16)
        _, la, _ = _gla_log_decay(gl_ref[...], wg_ref[...], bg_ref[...], n)
        b = _tri_sum(la, False)
        b_scr[...] = b
        b_last = b_scr[CHUNK - 1:CHUNK, :]
        st = s_scr[...]
        st_ref[0, 0] = st
        s_scr[...] = st * jnp.exp(b_last) + _dot(v, k * jnp.exp(b_last - b), TN)
        rowc = lax.broadcasted_iota(jnp.int32, (CHUNK, 1), 0)
        rows16 = lax.broadcasted_iota(jnp.int32, (SUB, 1), 0)
        a_tot = jnp.zeros((CHUNK, CHUNK), F32)
        for s in range(1, NSUB):
            lo = s * SUB
            bref = b_scr[lo - 1:lo, :]
            in_s = jnp.logical_and(rowc >= lo, rowc < lo + SUB)
            qh = q * jnp.exp(jnp.where(in_s, b - bref, -1e30))
            kh = k * jnp.exp(jnp.where(rowc < lo, bref - b, -1e30))
            a_tot = a_tot + _dot(qh, kh, NT)
        lane = lax.broadcasted_iota(jnp.int32, (SUB, CHUNK), 1)
        diag = []
        for s in range(NSUB):
            lo = s * SUB
            qs, bs = q[lo:lo + SUB], b[lo:lo + SUB]
            s_blk = jnp.zeros((SUB, CHUNK), F32)
            for j in range(SUB):
                r = lo + j
                e = jnp.exp(jnp.where(rows16 >= j, bs - b_scr[r:r + 1, :], -1e30))
                col = jnp.sum(qs * k_ref[r:r + 1, :] * e, axis=1, keepdims=True)
                s_blk = jnp.where(lane == r, col, s_blk)
            diag.append(s_blk)
        o = _dot(q * jnp.exp(b), st, NT) + _dot(a_tot + jnp.concatenate(diag, axis=0), vb)
        o_ref[...] = o
        oc_ref[...] = _gate_fwd(o, z_ref[...], w_ref[...]).astype(BF16)

    return pl.pallas_call(
        body, name="gla_fwd", grid=(GLA_HEADS, nc),
        in_specs=_gla_in_specs(False, nc),
        out_specs=[pl.BlockSpec((CHUNK, GLA_DV), lambda h, n: (n, h)),
                   pl.BlockSpec((CHUNK, GLA_DV), lambda h, n: (n, h)),
                   pl.BlockSpec((1, 1, GLA_DV, GLA_DK), lambda h, n: (h, n, 0, 0))],
        out_shape=[jax.ShapeDtypeStruct((rows, GLA_W), F32), jax.ShapeDtypeStruct((rows, GLA_W), BF16),
                   jax.ShapeDtypeStruct((GLA_HEADS, nc, GLA_DV, GLA_DK), F32)],
        scratch_shapes=[pltpu.VMEM((GLA_DV, GLA_DK), F32), pltpu.VMEM((CHUNK, GLA_DK), F32)],
        compiler_params=pltpu.CompilerParams(dimension_semantics=("parallel", "arbitrary")),
    )(proj, proj, proj, proj, proj, wgate, bgate, normw)


def _gla_bwd(proj, wgate, bgate, normw, o_gla, d_oc, states):
    rows = proj.shape[0]
    nc = rows // CHUNK

    def rn(n):
        return nc - 1 - n

    def body(q_ref, k_ref, v_ref, z_ref, gl_ref, wg_ref, bg_ref, w_ref, o_ref, do_ref, st_ref,
             dq_ref, dk_ref, dv_ref, dz_ref, dl_ref, dw_ref, dbg_ref,
             ds_scr, dq_scr, dk_scr, dv_scr, db_scr, b_scr, q_scr):
        n = pl.program_id(1)
        cn = rn(n)

        @pl.when(n == 0)
        def _():
            ds_scr[...] = jnp.zeros_like(ds_scr)
            dw_ref[...] = jnp.zeros_like(dw_ref)
            dbg_ref[...] = jnp.zeros_like(dbg_ref)

        q = q_ref[...] * (GLA_DK ** -0.5)
        k = k_ref[...]
        v = v_ref[...]
        vb = v.astype(BF16)
        do, dz, dw = _gate_bwd(do_ref[...], o_ref[...], z_ref[...], w_ref[...])
        dz_ref[...] = dz.astype(BF16)
        dw_ref[0] += dw
        logit, la, live = _gla_log_decay(gl_ref[...], wg_ref[...], bg_ref[...], cn)
        b = _tri_sum(la, False)
        b_scr[...] = b
        b_last = b_scr[CHUNK - 1:CHUNK, :]
        e_last = jnp.exp(b_last)
        st = st_ref[0, 0]
        g1 = ds_scr[...]
        eb = jnp.exp(b)
        qe = q * eb
        dqe = _dot(do, st)
        dq_scr[...] = dqe * eb
        db_scr[...] = dqe * qe
        ekb = jnp.exp(b_last - b)
        kdec = k * ekb
        dkdec = _dot(v, g1)
        dv_scr[...] = _dot(kdec, g1, NT)
        dk_scr[...] = dkdec * ekb
        wk = dkdec * kdec
        db_scr[...] -= wk
        dbl = jnp.sum(wk, axis=0, keepdims=True) + jnp.sum(g1 * st, axis=0, keepdims=True) * e_last
        ds_scr[...] = g1 * e_last + _dot(do, qe, TN)
        rowc = lax.broadcasted_iota(jnp.int32, (CHUNK, 1), 0)
        rows16 = lax.broadcasted_iota(jnp.int32, (SUB, 1), 0)
        da_full = _dot(do, vb, NT)
        a_tot = jnp.zeros((CHUNK, CHUNK), F32)
        for s in range(1, NSUB):
            lo = s * SUB
            bref = b_scr[lo - 1:lo, :]
            in_s = jnp.logical_and(rowc >= lo, rowc < lo + SUB)
            eq = jnp.exp(jnp.where(in_s, b - bref, -1e30))
            ek = jnp.exp(jnp.where(rowc < lo, bref - b, -1e30))
            qh = q * eq
            kh = k * ek
            a_tot = a_tot + _dot(qh, kh, NT)
            da = jnp.where(in_s, da_full, 0.0)
            dqh = _dot(da, kh)
            dkh = _dot(da, qh, TN)
            tq = dqh * qh
            tk = dkh * kh
            dq_scr[...] += dqh * eq
            dk_scr[...] += dkh * ek
            db_scr[...] += tq - tk
            db_scr[lo - 1:lo, :] += jnp.sum(tk, axis=0, keepdims=True) - jnp.sum(tq, axis=0, keepdims=True)
        dat_full = _dot(vb, do, NT)
        q_scr[...] = q
        lane = lax.broadcasted_iota(jnp.int32, (SUB, CHUNK), 1)
        diag = []
        for s in range(NSUB):
            lo = s * SUB
            qs, ks, bs = q[lo:lo + SUB], k[lo:lo + SUB], b[lo:lo + SUB]
            da_blk, dat_blk = da_full[lo:lo + SUB], dat_full[lo:lo + SUB]
            dqs = jnp.zeros((SUB, GLA_DK), F32)
            dks = jnp.zeros((SUB, GLA_DK), F32)
            dbs = jnp.zeros((SUB, GLA_DK), F32)
            s_blk = jnp.zeros((SUB, CHUNK), F32)
            for j in range(SUB):
                r = lo + j
                kj = k_ref[r:r + 1, :]
                e = jnp.exp(jnp.where(rows16 >= j, bs - b_scr[r:r + 1, :], -1e30))
                p = qs * e * kj
                s_blk = jnp.where(lane == r, jnp.sum(p, axis=1, keepdims=True), s_blk)
                dcol = jnp.sum(jnp.where(lane == r, da_blk, 0.0), axis=1, keepdims=True)
                dqs = dqs + (dcol * e) * kj
                dbs = dbs + dcol * p
            for i in range(SUB):
                r = lo + i
                e = jnp.exp(jnp.where(rows16 <= i, b_scr[r:r + 1, :] - bs, -1e30))
                drow = jnp.sum(jnp.where(lane == r, dat_blk, 0.0), axis=1, keepdims=True)
                nq = (drow * e) * q_scr[r:r + 1, :]
                dks = dks + nq
                dbs = dbs - nq * ks
            dq_scr[lo:lo + SUB, :] += dqs
            dk_scr[lo:lo + SUB, :] += dks
            db_scr[lo:lo + SUB, :] += dbs
            diag.append(s_blk)
        dv_scr[...] += _dot(a_tot + jnp.concatenate(diag, axis=0), do, TN)
        db_scr[CHUNK - 1:CHUNK, :] += dbl
        dla = _tri_sum(db_scr[...], True)
        dlogit = jnp.where(live, dla * (1.0 / GLA_TAU) * _sigmoid(-logit), 0.0)
        dl_ref[...] = dlogit
        dbg_ref[0] += jnp.sum(dlogit, axis=0, keepdims=True)
        dq_ref[...] = (dq_scr[...] * (GLA_DK ** -0.5)).astype(BF16)
        dk_ref[...] = dk_scr[...].astype(BF16)
        dv_ref[...] = dv_scr[...].astype(BF16)

    in_specs = _gla_in_specs(True, nc) + [
        pl.BlockSpec((CHUNK, GLA_DV), lambda h, n: (rn(n), h)),
        pl.BlockSpec((CHUNK, GLA_DV), lambda h, n: (rn(n), h)),
        pl.BlockSpec((1, 1, GLA_DV, GLA_DK), lambda h, n: (h, rn(n), 0, 0)),
    ]
    return pl.pallas_call(
        body, name="gla_bwd", grid=(GLA_HEADS, nc),
        in_specs=in_specs,
        out_specs=[pl.BlockSpec((CHUNK, GLA_DK), lambda h, n: (rn(n), h)),
                   pl.BlockSpec((CHUNK, GLA_DK), lambda h, n: (rn(n), h)),
                   pl.BlockSpec((CHUNK, GLA_DV), lambda h, n: (rn(n), h)),
                   pl.BlockSpec((CHUNK, GLA_DV), lambda h, n: (rn(n), h)),
                   pl.BlockSpec((CHUNK, GLA_DK), lambda h, n: (rn(n), h)),
                   pl.BlockSpec((1, 1, GLA_DV), lambda h, n: (h, 0, 0)),
                   pl.BlockSpec((1, 1, GLA_DK), lambda h, n: (h, 0, 0))],
        out_shape=[jax.ShapeDtypeStruct((rows, GLA_QK), BF16), jax.ShapeDtypeStruct((rows, GLA_QK), BF16),
                   jax.ShapeDtypeStruct((rows, GLA_W), BF16), jax.ShapeDtypeStruct((rows, GLA_W), BF16),
                   jax.ShapeDtypeStruct((rows, GLA_QK), F32),
                   jax.ShapeDtypeStruct((GLA_HEADS, 1, GLA_DV), F32),
                   jax.ShapeDtypeStruct((GLA_HEADS, 1, GLA_DK), F32)],
        scratch_shapes=[pltpu.VMEM((GLA_DV, GLA_DK), F32), pltpu.VMEM((CHUNK, GLA_DK), F32),
                        pltpu.VMEM((CHUNK, GLA_DK), F32), pltpu.VMEM((CHUNK, GLA_DV), F32),
                        pltpu.VMEM((CHUNK, GLA_DK), F32), pltpu.VMEM((CHUNK, GLA_DK), F32),
                        pltpu.VMEM((CHUNK, GLA_DK), F32)],
        compiler_params=pltpu.CompilerParams(dimension_semantics=("parallel", "arbitrary")),
    )(proj, proj, proj, proj, proj, wgate, bgate, normw, o_gla, d_oc, states)


def _adamw(name, w, g, m, v):
    rows, cols = w.shape
    tm = rows
    for cand in (256, 128, 64, 32, 16, 8):
        if rows % cand == 0:
            tm = cand
            break
    c1 = 1.0 - ADAM_B1 ** ADAM_STEP
    c2 = 1.0 - ADAM_B2 ** ADAM_STEP

    def body(w_ref, g_ref, m_ref, v_ref, d_ref, nm_ref, nv_ref):
        gv = g_ref[...]
        nm = ADAM_B1 * m_ref[...] + (1.0 - ADAM_B1) * gv
        nv = ADAM_B2 * v_ref[...] + (1.0 - ADAM_B2) * (gv * gv)
        nm_ref[...] = nm
        nv_ref[...] = nv
        d_ref[...] = -ADAM_LR * ((nm / c1) / (jnp.sqrt(nv / c2) + ADAM_EPS) + ADAM_WD * w_ref[...])

    blk = pl.BlockSpec((tm, cols), lambda i: (i, 0))
    return pl.pallas_call(
        body, name=name, grid=(rows // tm,),
        in_specs=[blk] * 4, out_specs=[blk] * 3,
        out_shape=[jax.ShapeDtypeStruct((rows, cols), F32)] * 3,
    )(w, g, m, v)


def _place():
    x, y, c = lax.axis_index("x"), lax.axis_index("y"), lax.axis_index("c")
    chips = [(1 - x, y), (x, 1 - y), (1 - x, 1 - y)]
    return x, y, c, chips


ANY = pl.BlockSpec(memory_space=pl.ANY)


def _gathered_struct(shape, dtype, kind):
    r, cc = shape
    if kind == "row":
        return jax.ShapeDtypeStruct((N_SHARD * r, cc), dtype)
    if kind == "col":
        return jax.ShapeDtypeStruct((r, N_SHARD * cc), dtype)
    return jax.ShapeDtypeStruct((N_SHARD, r, cc), dtype)


def _cast_place(name, w, kind, mine_arr, dtype):
    r, cc = w.shape
    tr = r
    for cand in (256, 128, 64, 32, 16):
        if r % cand == 0:
            tr = cand
            break
    nb = r // tr
    if kind == "row":
        o_spec = pl.BlockSpec((tr, cc), lambda i, m: (m[0] * nb + i, 0))
    elif kind == "col":
        o_spec = pl.BlockSpec((tr, cc), lambda i, m: (i, m[0]))
    else:
        o_spec = pl.BlockSpec((None, tr, cc), lambda i, m: (m[0], i, 0))
    w_spec = pl.BlockSpec((tr, cc), lambda i, m: (i, 0))

    def body(m_ref, w_ref, o_ref):
        o_ref[...] = w_ref[...].astype(o_ref.dtype)

    return pl.pallas_call(
        body, name=name,
        grid_spec=pltpu.PrefetchScalarGridSpec(
            num_scalar_prefetch=1, grid=(nb,), in_specs=[w_spec], out_specs=o_spec),
        out_shape=_gathered_struct((r, cc), dtype, kind),
    )(mine_arr, w)


def _allreduce_small(buf):
    rows, cols = buf.shape

    def body(in_ref, out_ref, sib_ref, pair_ref, far_ref, send_sems, recv_sems):
        x, y, c, chips = _place()
        sibling = (x, y, 1 - c)
        to_sib = pltpu.make_async_remote_copy(
            src_ref=in_ref, dst_ref=sib_ref, send_sem=send_sems.at[0], recv_sem=recv_sems.at[0],
            device_id=sibling, device_id_type=MESH)
        to_sib.start()
        to_sib.wait()
        pair_ref[...] = in_ref[...] + sib_ref[...]
        far = [pltpu.make_async_remote_copy(
            src_ref=pair_ref, dst_ref=far_ref.at[j], send_sem=send_sems.at[1 + j], recv_sem=recv_sems.at[1 + j],
            device_id=(*chip, c), device_id_type=MESH) for j, chip in enumerate(chips)]
        for cp in far:
            cp.start()
        for cp in far:
            cp.wait()
        out_ref[...] = (pair_ref[...] + far_ref[1]) + (far_ref[0] + far_ref[2])

    vm = pl.BlockSpec(memory_space=pltpu.VMEM)
    return pl.pallas_call(
        body, name="allreduce_small",
        in_specs=[vm], out_specs=vm,
        out_shape=jax.ShapeDtypeStruct((rows, cols), F32),
        scratch_shapes=[pltpu.VMEM((rows, cols), F32), pltpu.VMEM((rows, cols), F32),
                        pltpu.VMEM((3, rows, cols), F32),
                        pltpu.SemaphoreType.DMA((4,)), pltpu.SemaphoreType.DMA((4,))],
        compiler_params=pltpu.CompilerParams(has_side_effects=True),
    )(buf)


def _shard_window(ref, kind, shard_shape, shard, half):
    r, cc = shard_shape
    hr = r // 2
    if kind == "row":
        return ref.at[pl.ds(_mo(shard * r + half * hr, 8), hr), :]
    if kind == "col":
        return ref.at[pl.ds(_mo(half * hr, 8), hr), pl.ds(_mo(shard * cc, 128), cc)]
    return ref.at[shard, pl.ds(_mo(half * hr, 8), hr), :]


HBM = pl.BlockSpec(memory_space=pltpu.HBM)
SEM = pl.BlockSpec(memory_space=pltpu.SEMAPHORE)
DATAFLOW = pltpu.SideEffectType.DATAFLOW_SIDE_EFFECTING


def _in_hbm(a):
    return pltpu.with_memory_space_constraint(a, pltpu.HBM)


def _empty_hbm(shape, dtype):
    return _in_hbm(lax.empty(shape, dtype))


def _copies_start(name, bufs, n_copies, plan, carry):
    nb = len(bufs)

    def body(*refs):
        send_sems, recv_sems = refs[nb + 1], refs[nb + 2]
        for k, (src, dst, to) in enumerate(plan(refs[:nb])):
            pltpu.make_async_remote_copy(src_ref=src, dst_ref=dst, send_sem=send_sems.at[k], recv_sem=recv_sems.at[k],
                                         device_id=to, device_id_type=MESH).start()

    passed = list(bufs) + [carry]
    out = pl.pallas_call(
        body, name=name,
        in_specs=[HBM] * (nb + 1), out_specs=[SEM, SEM] + [HBM] * (nb + 1),
        out_shape=[pltpu.SemaphoreType.DMA((n_copies,)), pltpu.SemaphoreType.DMA((n_copies,))]
        + [pltpu.HBM(a.shape, a.dtype) for a in passed],
        input_output_aliases={i: 2 + i for i in range(nb + 1)},
        compiler_params=pltpu.CompilerParams(has_side_effects=DATAFLOW),
    )(*[_in_hbm(a) for a in passed])
    return out[0], out[1], list(out[2:2 + nb]), out[2 + nb]


def _copies_wait(name, send_sems, recv_sems, bufs, plan, after):
    nb = len(bufs)
    after = list(after) if isinstance(after, (list, tuple)) else [after]

    def body(*refs):
        send, recv = refs[nb], refs[nb + 1]
        for k, (src, dst, to) in enumerate(plan(refs[:nb])):
            cp = pltpu.make_async_remote_copy(src_ref=src, dst_ref=dst, send_sem=send.at[k], recv_sem=recv.at[k],
                                              device_id=to, device_id_type=MESH)
            cp.wait_send()
            cp.wait_recv()

    out = pl.pallas_call(
        body, name=name,
        in_specs=[HBM] * nb + [SEM, SEM] + [ANY] * len(after), out_specs=[HBM] * nb,
        out_shape=[pltpu.HBM(a.shape, a.dtype) for a in bufs],
        input_output_aliases={i: i for i in range(nb)},
        compiler_params=pltpu.CompilerParams(has_side_effects=DATAFLOW),
    )(*bufs, send_sems, recv_sems, *after)
    return list(out)


def _gather_ici_plan(shard_shapes, kinds):
    n_arr = len(kinds)

    def plan(refs):
        x, y, c, chips = _place()
        out = []
        for i in range(n_arr):
            w = _shard_window(refs[i], kinds[i], shard_shapes[i], 2 * x + y, c)
            out += [(w, w, (*chip, c)) for chip in chips]
        return out

    return plan


def _gather_d2d_plan(shard_shapes, kinds):
    n_arr = len(kinds)

    def plan(refs):
        x, y, c, chips = _place()
        out = []
        for i in range(n_arr):
            for chip in chips:
                w = _shard_window(refs[i], kinds[i], shard_shapes[i], 2 * chip[0] + chip[1], c)
                out.append((w, w, (x, y, 1 - c)))
        return out

    return plan


def _rs_pair_plan(kinds, shard_shapes):
    n_arr = len(kinds)

    def plan(refs):
        x, y, c, _ = _place()
        out = []
        for i in range(n_arr):
            for s in range(N_SHARD):
                out.append((_shard_window(refs[i], kinds[i], shard_shapes[i], s, 1 - c), refs[n_arr + i].at[s],
                            (x, y, 1 - c)))
        return out

    return plan


def _rs_chip_plan(n_arr):
    def plan(refs):
        x, y, c, chips = _place()
        out = []
        for i in range(n_arr):
            for j, chip in enumerate(chips):
                out.append((refs[i].at[2 * chip[0] + chip[1]], refs[n_arr + i].at[j], (*chip, c)))
        return out

    return plan


def _rs_pair_add(name, grad, got, kind, shard_shape, c):
    r, cc = shard_shape
    hr = r // 2
    tr = hr
    for cand in (256, 128, 64, 32, 16):
        if hr % cand == 0:
            tr = cand
            break
    nb = hr // tr

    if kind == "row":
        g_spec = pl.BlockSpec((tr, cc), lambda s, i, cr: (s * 2 * nb + cr[0] * nb + i, 0))
    elif kind == "col":
        g_spec = pl.BlockSpec((tr, cc), lambda s, i, cr: (cr[0] * nb + i, s))
    else:
        g_spec = pl.BlockSpec((None, tr, cc), lambda s, i, cr: (s, cr[0] * nb + i, 0))
    t_spec = pl.BlockSpec((None, tr, cc), lambda s, i, cr: (s, i, 0))

    def body(c_ref, g_ref, t_ref, p_ref, pb_ref):
        p = g_ref[...] + t_ref[...]
        p_ref[...] = p
        pb_ref[...] = p.astype(BF16)

    return pl.pallas_call(
        body, name=name,
        grid_spec=pltpu.PrefetchScalarGridSpec(
            num_scalar_prefetch=1, grid=(N_SHARD, nb),
            in_specs=[g_spec, t_spec], out_specs=[t_spec, t_spec]),
        out_shape=[jax.ShapeDtypeStruct((N_SHARD, hr, cc), F32), jax.ShapeDtypeStruct((N_SHARD, hr, cc), BF16)],
    )(c, grad, got)


def _rs_chip_add(name, pair_f32, got, shard_shape, mine_c):
    r, cc = shard_shape
    hr = r // 2
    tr = hr
    for cand in (256, 128, 64, 32, 16):
        if hr % cand == 0:
            tr = cand
            break
    nb = hr // tr

    def body(mc_ref, p_ref, t0_ref, t1_ref, t2_ref, o_ref):
        o_ref[...] = (p_ref[...] + t1_ref[...].astype(F32)) + (t0_ref[...].astype(F32) + t2_ref[...].astype(F32))

    def far(j):
        return pl.BlockSpec((None, tr, cc), lambda i, mc: (j, i, 0))

    return pl.pallas_call(
        body, name=name,
        grid_spec=pltpu.PrefetchScalarGridSpec(
            num_scalar_prefetch=1, grid=(nb,),
            in_specs=[pl.BlockSpec((None, tr, cc), lambda i, mc: (mc[0], i, 0)), far(0), far(1), far(2)],
            out_specs=pl.BlockSpec((tr, cc), lambda i, mc: (mc[1] * nb + i, 0))),
        out_shape=jax.ShapeDtypeStruct((r, cc), F32),
    )(mine_c, pair_f32, got, got, got)


def _rs_pair_share(name, halves, shard_shapes):
    n_arr = len(halves)

    def body(*refs):
        ins = refs[:n_arr]
        outs = refs[n_arr:2 * n_arr]
        send_sems, recv_sems = refs[2 * n_arr:]
        x, y, c, _ = _place()
        sibling = (x, y, 1 - c)
        cps = []
        for i in range(n_arr):
            hr = shard_shapes[i][0] // 2
            rows = pl.ds(_mo(c * hr, 8), hr)
            cp = pltpu.make_async_remote_copy(
                src_ref=outs[i].at[rows, :], dst_ref=outs[i].at[rows, :],
                send_sem=send_sems.at[i], recv_sem=recv_sems.at[i],
                device_id=sibling, device_id_type=MESH)
            cp.start()
            cps.append(cp)
        for cp in cps:
            cp.wait()

    return pl.pallas_call(
        body, name=name,
        in_specs=[ANY] * n_arr, out_specs=[ANY] * n_arr,
        out_shape=[jax.ShapeDtypeStruct(s, F32) for s in shard_shapes],
        input_output_aliases={i: i for i in range(n_arr)},
        scratch_shapes=[pltpu.SemaphoreType.DMA((n_arr,)), pltpu.SemaphoreType.DMA((n_arr,))],
        compiler_params=pltpu.CompilerParams(has_side_effects=True),
    )(*halves)


def _pack(arrays):
    flat = []
    for a in arrays:
        v = a.reshape(-1).astype(F32)
        flat.append(jnp.pad(v, (0, (-v.shape[0]) % SMALL_COLS)))
    buf = jnp.concatenate(flat).reshape(-1, SMALL_COLS)
    return jnp.pad(buf, ((0, (-buf.shape[0]) % 8), (0, 0)))


def _unpack(buf, shapes):
    out = []
    row = 0
    for s in shapes:
        size = math.prod(s)
        nrow = -(-size // SMALL_COLS)
        out.append(buf[row:row + nrow].reshape(-1)[:size].reshape(s))
        row += nrow
    return out


def kernel(x, meta, norm_ab_w, w_in_ab, ret_norm_w, s5_lam_re, s5_lam_im, s5_log_dt, s5_b_re, s5_b_im, s5_c_re, s5_c_im, s5_d, s5_w_glu, w_out_ab, norm_c_w, w_in_c, gla_w_gate, gla_b_gate, gla_norm_w, w_out_c, final_norm_w, loss_target, m_meta, m_norm_ab_w, m_w_in_ab, m_ret_norm_w, m_s5_lam_re, m_s5_lam_im, m_s5_log_dt, m_s5_b_re, m_s5_b_im, m_s5_c_re, m_s5_c_im, m_s5_d, m_s5_w_glu, m_w_out_ab, m_norm_c_w, m_w_in_c, m_gla_w_gate, m_gla_b_gate, m_gla_norm_w, m_w_out_c, m_final_norm_w, v_meta, v_norm_ab_w, v_w_in_ab, v_ret_norm_w, v_s5_lam_re, v_s5_lam_im, v_s5_log_dt, v_s5_b_re, v_s5_b_im, v_s5_c_re, v_s5_c_im, v_s5_d, v_s5_w_glu, v_w_out_ab, v_norm_c_w, v_w_in_c, v_gla_w_gate, v_gla_b_gate, v_gla_norm_w, v_w_out_c, v_final_norm_w):
    seq = x.shape[1]
    rows = seq + CHUNK
    xi, yi, ci = lax.axis_index("x"), lax.axis_index("y"), lax.axis_index("c")
    mine = 2 * xi + yi
    c_arr = jnp.reshape(ci, (1,)).astype(jnp.int32)
    mine_c = jnp.stack([mine, ci]).astype(jnp.int32)

    mine_arr = jnp.reshape(mine, (1,)).astype(jnp.int32)
    small_shard = _pack([meta, norm_c_w, gla_norm_w, gla_b_gate, gla_w_gate[0]])
    first_kinds = ["col", "stack"]
    first_shapes = [w_in_ab.shape[1:], small_shard.shape]
    first_ici = _gather_ici_plan(first_shapes, first_kinds)
    first_d2d = _gather_d2d_plan(first_shapes, first_kinds)
    f_send, f_recv, f_bufs, small_shard = _copies_start(
        "gather_first_ici_start",
        [_cast_place("place_w_in_ab", w_in_ab[0], "col", mine_arr, BF16),
         _cast_place("place_small", small_shard, "stack", mine_arr, F32)], 6, first_ici, small_shard)
    late = [("w_out_ab", w_out_ab[0]), ("w_in_c", w_in_c[0]), ("w_out_c", w_out_c[0]), ("w_glu", s5_w_glu[0])]
    late_kinds = ["row", "stack", "row", "row"]
    late_shapes = [a.shape for _, a in late]
    ici_plan = _gather_ici_plan(late_shapes, late_kinds)
    d2d_plan = _gather_d2d_plan(late_shapes, late_kinds)
    n_late = 3 * len(late)
    g_bufs = [_cast_place("place_" + nm, a, kd, mine_arr, BF16) for (nm, a), kd in zip(late, late_kinds)]
    cosf, sinf = _rope_tables(rows)
    rtab = _ret_tables()
    ab_re, ab_im, bb_re, bb_im = _s5_discretize(s5_lam_re[0], s5_lam_im[0], s5_log_dt[0], s5_b_re[0], s5_b_im[0])
    ab = (ab_re, ab_im)
    bd_b = (_bdiag_in(bb_re), _bdiag_in(bb_im))
    bd_c = (_bdiag_out(s5_c_re[0]), _bdiag_out(s5_c_im[0]))
    f_bufs = _copies_wait("gather_first_ici_wait", f_send, f_recv, f_bufs, first_ici,
                          [cosf, sinf, bd_b[0], bd_b[1], bd_c[0], bd_c[1]] + g_bufs + list(rtab))
    f_send, f_recv, f_bufs, cosf = _copies_start("gather_first_d2d_start", f_bufs, 6, first_d2d, cosf)
    wab, small_all = _copies_wait("gather_first_d2d_wait", f_send, f_recv, f_bufs, first_d2d, cosf)
    g_send, g_recv, g_bufs, wab = _copies_start("gather_late_ici_start", g_bufs, n_late, ici_plan, wab)
    q4 = D_MODEL // N_SHARD
    g4 = GLA_QK // N_SHARD
    parts = [_unpack(small_all[j], [(N_META, q4), (1, q4), (1, q4), (1, g4), (GLA_RANK, g4)]) for j in range(N_SHARD)]
    meta_f, norm_c_f, gla_norm_f, bgate_f, wgate_f = [jnp.concatenate([p[i] for p in parts], axis=1) for i in range(5)]
    wgate_pad = jnp.pad(wgate_f, ((0, 128 - GLA_RANK), (0, 0)))

    h0 = jnp.concatenate([jnp.zeros((PAD, D_MODEL), F32), meta_f, x[0]], axis=0)

    tm = _row_tile(rows, 1408)
    tmk = _row_tile(rows, 1408)
    hn0 = _rms_fwd("norm_ab", h0, norm_ab_w)
    proj0 = _matmul("in_proj_ab", hn0, wab, NN, rows, IN_AB, D_MODEL, tm=tm, tn=512, tk=D_MODEL)
    o_ret, o_a, ret_states = _ret_fwd(proj0, cosf, sinf, rtab, ret_norm_w)
    g_bufs = _copies_wait("gather_late_ici_wait", g_send, g_recv, g_bufs, ici_plan, o_a)
    g_send, g_recv, g_bufs, proj0 = _copies_start("gather_late_d2d_start", g_bufs, n_late, d2d_plan, proj0)
    y_s5, g_s5, s5_er, s5_ei = _s5_fwd(proj0, ab, bd_b, bd_c, s5_d)
    wout_ab, wc_st, wout_c, wglu = _copies_wait("gather_late_d2d_wait", g_send, g_recv, g_bufs, d2d_plan, g_s5)
    wc = jnp.concatenate([wc_st[j] for j in range(N_SHARD)] + [jnp.zeros((D_MODEL, IN_C_PAD - IN_C), BF16)], axis=1)
    zb_blk = (2 * RET_QK + 2 * RET_W + S5_W) // 512

    def glu_out(acc, gv, z):
        return gv.astype(F32) * _sigmoid(acc) * (z * _sigmoid(z))

    t_glu = _matmul("glu", g_s5, wglu, NN, rows, S5_W, S5_W, tm=tm, tn=512, tk=S5_W)
    o_b = _matmul("glu_out", g_s5, wglu, NN, rows, S5_W, S5_W, tm=tm, tn=512, tk=S5_W, out_dtype=BF16,
                  extras=[(g_s5, (tm, 512), lambda i, j, kk: (i, j)),
                          (proj0, (tm, 512), lambda i, j, kk: (i, zb_blk + j))],
                  epilogue=glu_out)
    mix = jnp.concatenate([o_a, o_b], axis=1)
    h1 = _matmul("out_proj_ab", mix, wout_ab, NN, rows, D_MODEL, OUT_AB, tm=tm, tn=512, tk=1024,
                 extras=[(h0, (tm, 512), lambda i, j, kk: (i, j))], epilogue=lambda acc, r: acc + r)

    hn1 = _rms_fwd("norm_c", h1, norm_c_f)
    proj1 = _matmul("in_proj_c", hn1, wc, NN, rows, IN_C_PAD, D_MODEL, tm=tm, tn=896, tk=D_MODEL)
    o_gla, o_c, gla_states = _gla_fwd(proj1, wgate_pad, bgate_f, gla_norm_f)
    h2 = _matmul("out_proj_c", o_c, wout_c, NN, rows, D_MODEL, GLA_W, tm=tm, tn=512, tk=GLA_W,
                 extras=[(h1, (tm, 512), lambda i, j, kk: (i, j))], epilogue=lambda acc, r: acc + r)
    loss_dev, dh2, d_final = _final_loss(h2, final_norm_w.reshape(1, D_MODEL), loss_target[0])

    g_wout_c = _matmul("d_w_out_c", o_c, dh2, TN, GLA_W, D_MODEL, rows, tm=1024, tn=1024, tk=tmk)
    d_oc = _matmul("d_o_c", dh2, wout_c, NT, rows, GLA_W, D_MODEL, tm=tm, tn=512, tk=1024)
    dq1, dk1, dv1, dz1, dlogit, d_gla_norm, d_bgate = _gla_bwd(proj1, wgate_pad, bgate_f, gla_norm_f, o_gla, d_oc, gla_states)
    gl_blk = (2 * GLA_QK + 2 * GLA_W) // 128
    dgl = _matmul("d_g_low", dlogit, wgate_pad, NT, rows, 128, GLA_QK, tm=tm, tn=128, tk=GLA_QK, out_dtype=BF16)
    g_wgate = _matmul("d_w_gate", proj1, dlogit, TN, 128, GLA_QK, rows, tm=128, tn=GLA_QK, tk=tmk, a_off=(0, gl_blk))
    dproj1 = jnp.concatenate([dq1, dk1, dv1, dz1, dgl], axis=1)
    g_wc = _matmul("d_w_in_c", hn1, dproj1, TN, D_MODEL, IN_C_PAD, rows, tm=1024, tn=896, tk=tmk)
    dhn1 = _matmul("d_hn1", dproj1, wc, NT, rows, D_MODEL, IN_C_PAD, tm=tm, tn=512, tk=896)
    dh1, d_norm_c = _rms_bwd("norm_c_bwd", dhn1, h1, norm_c_f, dh2)

    g_wout_ab = _matmul("d_w_out_ab", mix, dh1, TN, OUT_AB, D_MODEL, rows, tm=1024, tn=1024, tk=tmk)
    dmix = _matmul("d_mix", dh1, wout_ab, NT, rows, OUT_AB, D_MODEL, tm=tm, tn=512, tk=1024)
    dq0, dk0, dv0, dza, d_ret_norm = _ret_bwd(proj0, cosf, sinf, rtab, ret_norm_w, o_ret, dmix, ret_states)
    dzb, dt_glu, dg_direct = _s5_gate_bwd(dmix, g_s5, t_glu, proj0)
    g_wglu = _matmul("d_w_glu", g_s5, dt_glu, TN, S5_W, S5_W, rows, tm=1024, tn=1024, tk=tmk)
    dy_s5 = _matmul("d_y_s5", dt_glu, wglu, NT, rows, S5_W, S5_W, tm=tm, tn=512, tk=S5_W,
                    extras=[(dg_direct, (tm, 512), lambda i, j, kk: (i, j)),
                            (y_s5, (tm, 512), lambda i, j, kk: (i, j))],
                    epilogue=lambda acc, dg, yv: (acc + dg) * _gelu_grad(yv))
    g_wc_st = jnp.stack([g_wc[:, j * (IN_C // N_SHARD):(j + 1) * (IN_C // N_SHARD)] for j in range(N_SHARD)])
    rs1_names = ["w_out_ab", "w_in_c", "w_out_c", "w_glu"]
    rs1_shapes = [w_out_ab.shape[1:], w_in_c.shape[1:], w_out_c.shape[1:], s5_w_glu.shape[1:]]
    rs1_plan = _rs_pair_plan(late_kinds, rs1_shapes)
    rs1_land = [_empty_hbm((N_SHARD, r // 2, cc), F32) for (r, cc) in rs1_shapes]
    p_send, p_recv, p_bufs, dy_s5 = _copies_start("rs1_pair_start", [g_wout_ab, g_wc_st, g_wout_c, g_wglu] + rs1_land,
                                                  N_SHARD * 4, rs1_plan, dy_s5)
    du, dbr_d, dbi_d, dcr_d, dci_d, dar_p, dai_p, dd_p = _s5_bwd(proj0, dy_s5, ab, bd_b, bd_c, s5_d, (s5_er, s5_ei))
    p_bufs = _copies_wait("rs1_pair_wait", p_send, p_recv, p_bufs, rs1_plan, du)
    rs1_pairs = [_rs_pair_add("rs_pair_add_" + nm, g, t, kd, ss, c_arr)
                 for nm, g, t, kd, ss in zip(rs1_names, p_bufs[:4], p_bufs[4:], late_kinds, rs1_shapes)]
    dproj0 = jnp.concatenate([dq0, dk0, dv0, dza, du, dzb], axis=1)
    rs1_chip_plan = _rs_chip_plan(4)
    rs1_land2 = [_empty_hbm((3, r // 2, cc), BF16) for (r, cc) in rs1_shapes]
    c_send, c_recv, c_bufs, dproj0 = _copies_start("rs1_chip_start", [p[1] for p in rs1_pairs] + rs1_land2, 12,
                                                   rs1_chip_plan, dproj0)
    g_wab = _matmul("d_w_in_ab", hn0, dproj0, TN, D_MODEL, IN_AB, rows, tm=1024, tn=1024, tk=tmk)
    rs2_shapes = [w_in_ab.shape[1:]]
    rs2_plan = _rs_pair_plan(["col"], rs2_shapes)
    rs2_land = [_empty_hbm((N_SHARD, rs2_shapes[0][0] // 2, rs2_shapes[0][1]), F32)]
    q_send, q_recv, q_bufs, dproj0 = _copies_start("rs2_pair_start", [g_wab] + rs2_land, N_SHARD, rs2_plan, dproj0)
    dhn0 = _matmul("d_hn0", dproj0, wab, NT, rows, D_MODEL, IN_AB, tm=tm, tn=512, tk=2048)
    dh0, d_norm_ab = _rms_bwd("norm_ab_bwd", dhn0, h0, norm_ab_w, dh1)
    grad_x = dh0[CHUNK:][None]
    c_bufs = _copies_wait("rs1_chip_wait", c_send, c_recv, c_bufs, rs1_chip_plan, dh0)
    rs1_halves = [_rs_chip_add("rs_chip_add_" + nm, p[0], t, ss, mine_c)
                  for nm, p, t, ss in zip(rs1_names, rs1_pairs, c_bufs[4:], rs1_shapes)]
    g_w_out_ab, g_w_in_c, g_w_out_c, g_w_glu = _rs_pair_share("rs1_pair_share", rs1_halves, rs1_shapes)
    q_bufs = _copies_wait("rs2_pair_wait", q_send, q_recv, q_bufs, rs2_plan, g_w_glu)
    rs2_pair = _rs_pair_add("rs_pair_add_w_in_ab", q_bufs[0], q_bufs[1], "col", rs2_shapes[0], c_arr)
    rs2_chip_plan = _rs_chip_plan(1)
    rs2_land2 = [_empty_hbm((3, rs2_shapes[0][0] // 2, rs2_shapes[0][1]), BF16)]

    d_ab_re = jnp.sum(dar_p, axis=1).reshape(S5_G, S5_P)
    d_ab_im = jnp.sum(dai_p, axis=1).reshape(S5_G, S5_P)
    small_local = [loss_dev, dh0[PAD:CHUNK], d_norm_ab, d_ret_norm.reshape(1, RET_W), d_ab_re, d_ab_im,
                   _bdiag_in_extract(dbr_d), _bdiag_in_extract(dbi_d),
                   _bdiag_out_extract(dcr_d), _bdiag_out_extract(dci_d),
                   jnp.sum(dd_p, axis=1).reshape(1, S5_W), d_norm_c, g_wgate[:GLA_RANK],
                   d_bgate.reshape(1, GLA_QK), d_gla_norm.reshape(1, GLA_W), d_final]
    small_shapes = [a.shape for a in small_local]
    summed_buf = _allreduce_small(_pack(small_local))
    r_send, r_recv, r_bufs, summed_buf = _copies_start("rs2_chip_start", [rs2_pair[1]] + rs2_land2, 3, rs2_chip_plan,
                                                       summed_buf)
    summed = _unpack(summed_buf, small_shapes)
    (loss, g_meta_f, g_norm_ab, g_ret_norm, g_ab_re, g_ab_im, g_bb_re, g_bb_im, g_c_re, g_c_im, g_d,
     g_norm_c_f, g_wgate_f, g_bgate_f, g_gla_norm_f, g_final) = summed
    _, s5_vjp = jax.vjp(_s5_discretize, s5_lam_re[0], s5_lam_im[0], s5_log_dt[0], s5_b_re[0], s5_b_im[0])
    g_lam_re, g_lam_im, g_log_dt, g_b_re, g_b_im = s5_vjp((g_ab_re, g_ab_im, g_bb_re, g_bb_im))

    def take(a, width):
        return lax.dynamic_slice_in_dim(a, mine * width, width, axis=1)

    grads = {
        "meta": take(g_meta_f, q4), "norm_ab_w": g_norm_ab, "ret_norm_w": g_ret_norm,
        "s5_lam_re": g_lam_re[None], "s5_lam_im": g_lam_im[None], "s5_log_dt": g_log_dt[None],
        "s5_b_re": g_b_re[None], "s5_b_im": g_b_im[None], "s5_c_re": g_c_re[None], "s5_c_im": g_c_im[None],
        "s5_d": g_d, "s5_w_glu": g_w_glu[None], "w_out_ab": g_w_out_ab[None], "norm_c_w": take(g_norm_c_f, q4),
        "w_in_c": g_w_in_c[None], "gla_w_gate": take(g_wgate_f, g4)[None], "gla_b_gate": take(g_bgate_f, g4),
        "gla_norm_w": take(g_gla_norm_f, q4), "w_out_c": g_w_out_c[None], "final_norm_w": g_final.reshape(D_MODEL),
    }
    weights = dict(meta=meta, norm_ab_w=norm_ab_w, w_in_ab=w_in_ab, ret_norm_w=ret_norm_w, s5_lam_re=s5_lam_re,
                   s5_lam_im=s5_lam_im, s5_log_dt=s5_log_dt, s5_b_re=s5_b_re, s5_b_im=s5_b_im, s5_c_re=s5_c_re,
                   s5_c_im=s5_c_im, s5_d=s5_d, s5_w_glu=s5_w_glu, w_out_ab=w_out_ab, norm_c_w=norm_c_w,
                   w_in_c=w_in_c, gla_w_gate=gla_w_gate, gla_b_gate=gla_b_gate, gla_norm_w=gla_norm_w,
                   w_out_c=w_out_c, final_norm_w=final_norm_w)
    m_in = dict(meta=m_meta, norm_ab_w=m_norm_ab_w, w_in_ab=m_w_in_ab, ret_norm_w=m_ret_norm_w,
                s5_lam_re=m_s5_lam_re, s5_lam_im=m_s5_lam_im, s5_log_dt=m_s5_log_dt, s5_b_re=m_s5_b_re,
                s5_b_im=m_s5_b_im, s5_c_re=m_s5_c_re, s5_c_im=m_s5_c_im, s5_d=m_s5_d, s5_w_glu=m_s5_w_glu,
                w_out_ab=m_w_out_ab, norm_c_w=m_norm_c_w, w_in_c=m_w_in_c, gla_w_gate=m_gla_w_gate,
                gla_b_gate=m_gla_b_gate, gla_norm_w=m_gla_norm_w, w_out_c=m_w_out_c, final_norm_w=m_final_norm_w)
    v_in = dict(meta=v_meta, norm_ab_w=v_norm_ab_w, w_in_ab=v_w_in_ab, ret_norm_w=v_ret_norm_w,
                s5_lam_re=v_s5_lam_re, s5_lam_im=v_s5_lam_im, s5_log_dt=v_s5_log_dt, s5_b_re=v_s5_b_re,
                s5_b_im=v_s5_b_im, s5_c_re=v_s5_c_re, s5_c_im=v_s5_c_im, s5_d=v_s5_d, s5_w_glu=v_s5_w_glu,
                w_out_ab=v_w_out_ab, norm_c_w=v_norm_c_w, w_in_c=v_w_in_c, gla_w_gate=v_gla_w_gate,
                gla_b_gate=v_gla_b_gate, gla_norm_w=v_gla_norm_w, w_out_c=v_w_out_c, final_norm_w=v_final_norm_w)
    order = list(weights)
    big_names = ["s5_w_glu", "w_out_ab", "w_in_c", "w_out_c", "w_in_ab"]
    small_names = [nm for nm in order if nm not in big_names]
    delta, new_m, new_v = {}, {}, {}

    def big_update(nm):
        shp = weights[nm].shape
        d2, m2, v2 = _adamw("adamw_" + nm, weights[nm][0], grads[nm][0], m_in[nm][0], v_in[nm][0])
        delta[nm], new_m[nm], new_v[nm] = d2.reshape(shp), m2.reshape(shp), v2.reshape(shp)

    for nm in big_names[:-1]:
        big_update(nm)
    sshapes = [weights[nm].shape for nm in small_names]
    d2, m2, v2 = _adamw("adamw_small", _pack([weights[nm] for nm in small_names]),
                        _pack([grads[nm] for nm in small_names]), _pack([m_in[nm] for nm in small_names]),
                        _pack([v_in[nm] for nm in small_names]))
    for nm, dd, mm, vv in zip(small_names, _unpack(d2, sshapes), _unpack(m2, sshapes), _unpack(v2, sshapes)):
        delta[nm], new_m[nm], new_v[nm] = dd, mm, vv
    r_bufs = _copies_wait("rs2_chip_wait", r_send, r_recv, r_bufs, rs2_chip_plan,
                          [v2] + [new_v[nm] for nm in big_names[:-1]])
    rs2_half = _rs_chip_add("rs_chip_add_w_in_ab", rs2_pair[0], r_bufs[1], rs2_shapes[0], mine_c)
    grads["w_in_ab"] = _rs_pair_share("rs2_pair_share", [rs2_half], rs2_shapes)[0][None]
    big_update("w_in_ab")
    grads = {nm: grads[nm].reshape(weights[nm].shape) for nm in order}
    return (loss.reshape(()), grad_x, *[grads[nm] for nm in order], *[delta[nm] for nm in order],
            *[new_m[nm] for nm in order], *[new_v[nm] for nm in order])
```

```python
import functools
import math

import jax
import jax.numpy as jnp
from jax import lax
from jax.experimental import pallas as pl
from jax.experimental.pallas import tpu as pltpu

F32 = jnp.float32
BF16 = jnp.bfloat16
MESH = pl.DeviceIdType.MESH

D_MODEL = 2048
N_META = 16
CHUNK = 128
SUB = 16
NSUB = CHUNK // SUB
PAD = CHUNK - N_META
EPS = 1e-6

RET_HEADS = 8
RET_DK = 128
RET_DV = 256
RET_QK = RET_HEADS * RET_DK
RET_W = RET_HEADS * RET_DV
ROPE_BASE = 10000.0

S5_W = 1024
S5_GH = 16
S5_G = S5_W // S5_GH
S5_P = 64
S5_TG = 8
S5_NT = S5_G // S5_TG
S5_TU = S5_TG * S5_GH
S5_TS = S5_TG * S5_P
S5_FWD_TILES = 2
S5_BWD_TILES = 1

GLA_HEADS = 4
GLA_DK = 256
GLA_DV = 512
GLA_QK = GLA_HEADS * GLA_DK
GLA_W = GLA_HEADS * GLA_DV
GLA_RANK = 16
GLA_TAU = 16.0

IN_AB = 2 * RET_QK + 2 * RET_W + 2 * S5_W
OUT_AB = RET_W + S5_W
IN_C = 2 * GLA_QK + 2 * GLA_W + GLA_RANK
IN_C_PAD = 2 * GLA_QK + 2 * GLA_W + 128

ADAM_LR = 0.001
ADAM_B1 = 0.9
ADAM_B2 = 0.999
ADAM_EPS = 1e-08
ADAM_WD = 0.01
ADAM_STEP = 10

N_SHARD = 4
SMALL_COLS = 512

NN = (((1,), (0,)), ((), ()))
NT = (((1,), (1,)), ((), ()))
TN = (((0,), (0,)), ((), ()))


def _dot(a, b, dims=NN):
    return lax.dot_general(a.astype(BF16), b.astype(BF16), dims, preferred_element_type=F32)


def _mo(v, m):
    return v if isinstance(v, int) else pl.multiple_of(v, m)


def _sigmoid(x):
    return 1.0 / (1.0 + jnp.exp(-x))


def _row_tile(rows, cap):
    n = rows // CHUNK
    best = 1
    for d in range(1, n + 1):
        if n % d == 0 and d * CHUNK <= cap:
            best = d
    return best * CHUNK


def _col_tile(cols, cap):
    n = cols // 128
    best = 1
    for d in range(1, n + 1):
        if n % d == 0 and d * 128 <= cap:
            best = d
    return best * 128


def _matmul(name, a, b, dims, m, n, k, *, tm, tn, tk, out_dtype=F32, a_off=(0, 0), b_off=(0, 0),
            extras=(), epilogue=None, out_shape=None, out_spec=None):
    nk = k // tk
    assert m % tm == 0 and n % tn == 0 and k % tk == 0, (name, m, n, k, tm, tn, tk)
    ar, ac = a_off
    br, bc = b_off
    if dims == NN:
        a_spec = pl.BlockSpec((tm, tk), lambda i, j, kk: (i + ar, kk + ac))
        b_spec = pl.BlockSpec((tk, tn), lambda i, j, kk: (kk + br, j + bc))
    elif dims == NT:
        a_spec = pl.BlockSpec((tm, tk), lambda i, j, kk: (i + ar, kk + ac))
        b_spec = pl.BlockSpec((tn, tk), lambda i, j, kk: (j + br, kk + bc))
    else:
        a_spec = pl.BlockSpec((tk, tm), lambda i, j, kk: (kk + ar, i + ac))
        b_spec = pl.BlockSpec((tk, tn), lambda i, j, kk: (kk + br, j + bc))
    n_extra = len(extras)
    if out_shape is None:
        out_shape = jax.ShapeDtypeStruct((m, n), out_dtype)

    def body(*refs):
        a_ref, b_ref = refs[0], refs[1]
        e_refs = refs[2:2 + n_extra]
        o_ref = refs[2 + n_extra]
        part = _dot(a_ref[...], b_ref[...], dims)
        if nk == 1:
            if epilogue is not None:
                part = epilogue(part, *[e[...] for e in e_refs])
            o_ref[...] = part.astype(o_ref.dtype)
            return
        acc_ref = refs[3 + n_extra]
        kk = pl.program_id(2)

        @pl.when(kk == 0)
        def _():
            acc_ref[...] = jnp.zeros_like(acc_ref)

        acc_ref[...] += part

        @pl.when(kk == nk - 1)
        def _():
            acc = acc_ref[...]
            if epilogue is not None:
                acc = epilogue(acc, *[e[...] for e in e_refs])
            o_ref[...] = acc.astype(o_ref.dtype)

    if out_spec is None:
        out_spec = pl.BlockSpec((tm, tn), lambda i, j, kk: (i, j))
    return pl.pallas_call(
        body, name=name, grid=(m // tm, n // tn, nk),
        in_specs=[a_spec, b_spec] + [pl.BlockSpec(bs, im) for (_, bs, im) in extras],
        out_specs=out_spec, out_shape=out_shape,
        scratch_shapes=[] if nk == 1 else [pltpu.VMEM((tm, tn), F32)],
        compiler_params=pltpu.CompilerParams(dimension_semantics=("parallel", "parallel", "arbitrary")),
    )(a, b, *[e for (e, _, _) in extras])


def _rms_fwd(name, h, w):
    rows, d = h.shape
    tm = _row_tile(rows, 512)

    def body(h_ref, w_ref, o_ref):
        x = h_ref[...]
        r = lax.rsqrt(jnp.mean(x * x, axis=-1, keepdims=True) + EPS)
        o_ref[...] = (x * r * w_ref[...]).astype(BF16)

    return pl.pallas_call(
        body, name=name, grid=(rows // tm,),
        in_specs=[pl.BlockSpec((tm, d), lambda i: (i, 0)), pl.BlockSpec((1, d), lambda i: (0, 0))],
        out_specs=pl.BlockSpec((tm, d), lambda i: (i, 0)),
        out_shape=jax.ShapeDtypeStruct((rows, d), BF16),
    )(h, w)


def _rms_bwd(name, dhn, h, w, dres):
    rows, d = h.shape
    tm = _row_tile(rows, 384)

    def body(g_ref, h_ref, w_ref, r_ref, dh_ref, dw_ref):
        i = pl.program_id(0)
        x = h_ref[...]
        r = lax.rsqrt(jnp.mean(x * x, axis=-1, keepdims=True) + EPS)
        xh = x * r
        g = g_ref[...]
        gw = g * w_ref[...]
        dh_ref[...] = r_ref[...] + r * (gw - xh * jnp.mean(gw * xh, axis=-1, keepdims=True))

        @pl.when(i == 0)
        def _():
            dw_ref[...] = jnp.zeros_like(dw_ref)

        dw_ref[...] += jnp.sum(g * xh, axis=0, keepdims=True)

    return pl.pallas_call(
        body, name=name, grid=(rows // tm,),
        in_specs=[pl.BlockSpec((tm, d), lambda i: (i, 0)), pl.BlockSpec((tm, d), lambda i: (i, 0)),
                  pl.BlockSpec((1, d), lambda i: (0, 0)), pl.BlockSpec((tm, d), lambda i: (i, 0))],
        out_specs=[pl.BlockSpec((tm, d), lambda i: (i, 0)), pl.BlockSpec((1, d), lambda i: (0, 0))],
        out_shape=[jax.ShapeDtypeStruct((rows, d), F32), jax.ShapeDtypeStruct((1, d), F32)],
    )(dhn, h, w, dres)


def _embed_norm(x, meta, w):
    seq, d = x.shape
    rows = seq + CHUNK

    def body(x_ref, m_ref, w_ref, h_ref, o_ref):
        i = pl.program_id(0)

        def emit(h):
            h_ref[...] = h
            r = lax.rsqrt(jnp.mean(h * h, axis=-1, keepdims=True) + EPS)
            o_ref[...] = (h * r * w_ref[...]).astype(BF16)

        @pl.when(i == 0)
        def _():
            emit(jnp.concatenate([jnp.zeros((PAD, d), F32), m_ref[...]], axis=0))

        @pl.when(i > 0)
        def _():
            emit(x_ref[...])

    blk = pl.BlockSpec((CHUNK, d), lambda i: (i, 0))
    return pl.pallas_call(
        body, name="embed_norm_ab", grid=(rows // CHUNK,),
        in_specs=[pl.BlockSpec((CHUNK, d), lambda i: (jnp.maximum(i - 1, 0), 0)),
                  pl.BlockSpec((N_META, d), lambda i: (0, 0)), pl.BlockSpec((1, d), lambda i: (0, 0))],
        out_specs=[blk, blk],
        out_shape=[jax.ShapeDtypeStruct((rows, d), F32), jax.ShapeDtypeStruct((rows, d), BF16)],
    )(x, meta, w)


def _rms_bwd_embed(dhn, h, w, dres):
    rows, d = h.shape
    seq = rows - CHUNK

    def body(g_ref, h_ref, w_ref, r_ref, gx_ref, gm_ref, dw_ref):
        i = pl.program_id(0)
        x = h_ref[...]
        r = lax.rsqrt(jnp.mean(x * x, axis=-1, keepdims=True) + EPS)
        xh = x * r
        g = g_ref[...]
        gw = g * w_ref[...]
        dh = r_ref[...] + r * (gw - xh * jnp.mean(gw * xh, axis=-1, keepdims=True))

        @pl.when(i == 0)
        def _():
            dw_ref[...] = jnp.zeros_like(dw_ref)
            gm_ref[...] = dh[PAD:]

        @pl.when(i > 0)
        def _():
            gx_ref[...] = dh

        dw_ref[...] += jnp.sum(g * xh, axis=0, keepdims=True)

    blk = pl.BlockSpec((CHUNK, d), lambda i: (i, 0))
    return pl.pallas_call(
        body, name="norm_ab_bwd", grid=(rows // CHUNK,),
        in_specs=[blk, blk, pl.BlockSpec((1, d), lambda i: (0, 0)), blk],
        out_specs=[pl.BlockSpec((CHUNK, d), lambda i: (jnp.maximum(i - 1, 0), 0)),
                   pl.BlockSpec((N_META, d), lambda i: (0, 0)), pl.BlockSpec((1, d), lambda i: (0, 0))],
        out_shape=[jax.ShapeDtypeStruct((seq, d), F32), jax.ShapeDtypeStruct((N_META, d), F32),
                   jax.ShapeDtypeStruct((1, d), F32)],
    )(dhn, h, w, dres)


def _final_loss(h2, w, target):
    rows, d = h2.shape

    def body(h_ref, w_ref, t_ref, loss_ref, dh_ref, dw_ref):
        i = pl.program_id(0)

        @pl.when(i == 0)
        def _():
            loss_ref[...] = jnp.zeros_like(loss_ref)
            dw_ref[...] = jnp.zeros_like(dw_ref)
            dh_ref[...] = jnp.zeros_like(dh_ref)

        @pl.when(i > 0)
        def _():
            x = h_ref[...]
            r = lax.rsqrt(jnp.mean(x * x, axis=-1, keepdims=True) + EPS)
            xh = x * r
            wv = w_ref[...]
            err = xh * wv - t_ref[...]
            loss_ref[...] += 0.5 * jnp.sum(jnp.mean(err * err, axis=-1, keepdims=True), axis=0, keepdims=True)
            g = err * (1.0 / d)
            gw = g * wv
            dh_ref[...] = r * (gw - xh * jnp.mean(gw * xh, axis=-1, keepdims=True))
            dw_ref[...] += jnp.sum(g * xh, axis=0, keepdims=True)

    return pl.pallas_call(
        body, name="final_loss", grid=(rows // CHUNK,),
        in_specs=[pl.BlockSpec((CHUNK, d), lambda i: (i, 0)), pl.BlockSpec((1, d), lambda i: (0, 0)),
                  pl.BlockSpec((CHUNK, d), lambda i: (jnp.maximum(i - 1, 0), 0))],
        out_specs=[pl.BlockSpec((1, 1), lambda i: (0, 0)), pl.BlockSpec((CHUNK, d), lambda i: (i, 0)),
                   pl.BlockSpec((1, d), lambda i: (0, 0))],
        out_shape=[jax.ShapeDtypeStruct((1, 1), F32), jax.ShapeDtypeStruct((rows, d), F32),
                   jax.ShapeDtypeStruct((1, d), F32)],
    )(h2, w, target)


def _gate_fwd(o, z, w):
    rs = lax.rsqrt(jnp.mean(o * o, axis=-1, keepdims=True) + EPS)
    return o * rs * w * (z * _sigmoid(z))


def _gate_bwd(dout, o, z, w):
    rs = lax.rsqrt(jnp.mean(o * o, axis=-1, keepdims=True) + EPS)
    yn = o * rs
    sg = _sigmoid(z)
    sil = z * sg
    dsil = sg * (1.0 + z * (1.0 - sg))
    dz = dout * yn * w * dsil
    dyn = dout * w * sil
    dw = jnp.sum(dout * yn * sil, axis=0, keepdims=True)
    do = rs * (dyn - yn * jnp.mean(dyn * yn, axis=-1, keepdims=True))
    return do, dz, dw


def _rope(t, cosf, sinf):
    return t * cosf + pltpu.roll(t, RET_DK // 2, 1) * sinf


def _rope_t(d, cosf, sinf):
    return d * cosf + pltpu.roll(d * sinf, RET_DK // 2, 1)


def _ret_tables():
    log_g = jnp.log1p(-jnp.exp2(-5.0 - jnp.arange(RET_HEADS, dtype=F32)))
    idx = jnp.arange(CHUNK, dtype=F32)
    diff = idx[:, None] - idx[None, :]
    decay = jnp.where(diff >= 0, jnp.exp(log_g[:, None, None] * jnp.maximum(diff, 0.0)), 0.0)
    kw = jnp.exp(log_g[:, None] * (CHUNK - 1 - idx))
    qw = jnp.exp(log_g[:, None] * (idx + 1.0))
    gch = jnp.exp(log_g * CHUNK)
    kw = jnp.broadcast_to(kw[:, :, None], (RET_HEADS, CHUNK, RET_DK))
    qw = jnp.broadcast_to(qw[:, :, None], (RET_HEADS, CHUNK, RET_DK))
    gch = jnp.broadcast_to(gch[:, None, None], (RET_HEADS, 1, RET_DV))
    return decay, kw, qw, gch


def _rope_tables(rows):
    pos = jnp.arange(rows, dtype=F32) - float(PAD)
    inv_freq = jnp.power(ROPE_BASE, -jnp.arange(0, RET_DK, 2, dtype=F32) / RET_DK)
    ang = pos[:, None] * inv_freq[None, :]
    cos, sin = jnp.cos(ang), jnp.sin(ang)
    return jnp.concatenate([cos, cos], axis=1), jnp.concatenate([-sin, sin], axis=1)


RET_HB = 4
RET_QB = RET_HB * RET_DK
RET_VB = RET_HB * RET_DV


def _ret_in_specs(rev, nc):
    def cn(n):
        return (nc - 1 - n) if rev else n
    kb = RET_QK // RET_QB
    vb = 2 * RET_QK // RET_VB
    zb = (2 * RET_QK + RET_W) // RET_VB
    return [
        pl.BlockSpec((CHUNK, RET_QB), lambda h, n: (cn(n), h)),
        pl.BlockSpec((CHUNK, RET_QB), lambda h, n: (cn(n), kb + h)),
        pl.BlockSpec((CHUNK, RET_VB), lambda h, n: (cn(n), vb + h)),
        pl.BlockSpec((CHUNK, RET_VB), lambda h, n: (cn(n), zb + h)),
        pl.BlockSpec((CHUNK, RET_DK), lambda h, n: (cn(n), 0)),
        pl.BlockSpec((CHUNK, RET_DK), lambda h, n: (cn(n), 0)),
        pl.BlockSpec((RET_HB, CHUNK, CHUNK), lambda h, n: (h, 0, 0)),
        pl.BlockSpec((RET_HB, CHUNK, RET_DK), lambda h, n: (h, 0, 0)),
        pl.BlockSpec((RET_HB, CHUNK, RET_DK), lambda h, n: (h, 0, 0)),
        pl.BlockSpec((RET_HB, 1, RET_DV), lambda h, n: (h, 0, 0)),
        pl.BlockSpec((1, RET_VB), lambda h, n: (0, h)),
    ]


def _ret_fwd(proj, cosf, sinf, tables, normw):
    rows = proj.shape[0]
    nc = rows // CHUNK
    decay, kw, qw, gch = tables

    def body(q_ref, k_ref, v_ref, z_ref, cos_ref, sin_ref, dm_ref, kw_ref, qw_ref, g_ref, w_ref,
             o_ref, oa_ref, st_ref, s_scr):
        n = pl.program_id(1)

        @pl.when(n == 0)
        def _():
            s_scr[...] = jnp.zeros_like(s_scr)

        cosv, sinv = cos_ref[...], sin_ref[...]
        for hh in range(RET_HB):
            qc = slice(hh * RET_DK, (hh + 1) * RET_DK)
            vc = slice(hh * RET_DV, (hh + 1) * RET_DV)
            q = _rope(q_ref[:, qc], cosv, sinv)
            k = _rope(k_ref[:, qc], cosv, sinv) * (RET_DK ** -0.5)
            v = v_ref[:, vc]
            s = s_scr[hh]
            st_ref[hh, 0] = s.astype(BF16)
            a = _dot(q, k, NT) * dm_ref[hh]
            o = _dot(a, v) + _dot(q * qw_ref[hh], s)
            s_scr[hh] = s * g_ref[hh] + _dot(k * kw_ref[hh], v, TN)
            o_ref[:, vc] = o
            oa_ref[:, vc] = _gate_fwd(o, z_ref[:, vc], w_ref[:, vc]).astype(BF16)

    return pl.pallas_call(
        body, name="ret_fwd", grid=(RET_HEADS // RET_HB, nc),
        in_specs=_ret_in_specs(False, nc),
        out_specs=[pl.BlockSpec((CHUNK, RET_VB), lambda h, n: (n, h)),
                   pl.BlockSpec((CHUNK, RET_VB), lambda h, n: (n, h)),
                   pl.BlockSpec((RET_HB, 1, RET_DK, RET_DV), lambda h, n: (h, n, 0, 0))],
        out_shape=[jax.ShapeDtypeStruct((rows, RET_W), F32), jax.ShapeDtypeStruct((rows, RET_W), BF16),
                   jax.ShapeDtypeStruct((RET_HEADS, nc, RET_DK, RET_DV), BF16)],
        scratch_shapes=[pltpu.VMEM((RET_HB, RET_DK, RET_DV), F32)],
        compiler_params=pltpu.CompilerParams(dimension_semantics=("parallel", "arbitrary")),
    )(proj, proj, proj, proj, cosf, sinf, decay, kw, qw, gch, normw)


def _ret_bwd(proj, cosf, sinf, tables, normw, o_ret, dmix, states):
    rows = proj.shape[0]
    nc = rows // CHUNK
    decay, kw, qw, gch = tables

    def rn(n):
        return nc - 1 - n

    def body(q_ref, k_ref, v_ref, z_ref, cos_ref, sin_ref, dm_ref, kw_ref, qw_ref, g_ref, w_ref,
             o_ref, do_ref, st_ref, dq_ref, dk_ref, dv_ref, dz_ref, dw_ref, ds_scr):
        n = pl.program_id(1)

        @pl.when(n == 0)
        def _():
            ds_scr[...] = jnp.zeros_like(ds_scr)
            dw_ref[...] = jnp.zeros_like(dw_ref)

        cosv, sinv = cos_ref[...], sin_ref[...]
        for hh in range(RET_HB):
            qc = slice(hh * RET_DK, (hh + 1) * RET_DK)
            vc = slice(hh * RET_DV, (hh + 1) * RET_DV)
            q = _rope(q_ref[:, qc], cosv, sinv)
            k = _rope(k_ref[:, qc], cosv, sinv) * (RET_DK ** -0.5)
            v = v_ref[:, vc]
            do, dz, dw = _gate_bwd(do_ref[:, vc], o_ref[:, vc], z_ref[:, vc], w_ref[:, vc])
            dz_ref[:, vc] = dz.astype(BF16)
            dw_ref[hh] += dw
            dm = dm_ref[hh]
            s = st_ref[hh, 0]
            g1 = ds_scr[hh]
            p = _dot(q, k, NT) * dm
            kwv = k * kw_ref[hh]
            qwv = q * qw_ref[hh]
            dp = _dot(do, v, NT)
            da = dp * dm
            dv = _dot(p, do, TN) + _dot(kwv, g1)
            dq = _dot(da, k) + _dot(do, s, NT) * qw_ref[hh]
            dk = _dot(da, q, TN) + _dot(v, g1, NT) * kw_ref[hh]
            ds_scr[hh] = g1 * g_ref[hh] + _dot(qwv, do, TN)
            dv_ref[:, vc] = dv.astype(BF16)
            dq_ref[:, qc] = _rope_t(dq, cosv, sinv).astype(BF16)
            dk_ref[:, qc] = _rope_t(dk * (RET_DK ** -0.5), cosv, sinv).astype(BF16)

    in_specs = _ret_in_specs(True, nc) + [
        pl.BlockSpec((CHUNK, RET_VB), lambda h, n: (rn(n), h)),
        pl.BlockSpec((CHUNK, RET_VB), lambda h, n: (rn(n), h)),
        pl.BlockSpec((RET_HB, 1, RET_DK, RET_DV), lambda h, n: (h, rn(n), 0, 0)),
    ]
    return pl.pallas_call(
        body, name="ret_bwd", grid=(RET_HEADS // RET_HB, nc),
        in_specs=in_specs,
        out_specs=[pl.BlockSpec((CHUNK, RET_QB), lambda h, n: (rn(n), h)),
                   pl.BlockSpec((CHUNK, RET_QB), lambda h, n: (rn(n), h)),
                   pl.BlockSpec((CHUNK, RET_VB), lambda h, n: (rn(n), h)),
                   pl.BlockSpec((CHUNK, RET_VB), lambda h, n: (rn(n), h)),
                   pl.BlockSpec((RET_HB, 1, RET_DV), lambda h, n: (h, 0, 0))],
        out_shape=[jax.ShapeDtypeStruct((rows, RET_QK), BF16), jax.ShapeDtypeStruct((rows, RET_QK), BF16),
                   jax.ShapeDtypeStruct((rows, RET_W), BF16), jax.ShapeDtypeStruct((rows, RET_W), BF16),
                   jax.ShapeDtypeStruct((RET_HEADS, 1, RET_DV), F32)],
        scratch_shapes=[pltpu.VMEM((RET_HB, RET_DK, RET_DV), F32)],
        compiler_params=pltpu.CompilerParams(dimension_semantics=("parallel", "arbitrary")),
    )(proj, proj, proj, proj, cosf, sinf, decay, kw, qw, gch, normw, o_ret, dmix, states)


def _s5_discretize(lam_re, lam_im, log_dt, b_re, b_im):
    dt = jnp.exp(log_dt)[:, None]
    mag = jnp.exp(lam_re * dt)
    ab_re, ab_im = mag * jnp.cos(lam_im * dt), mag * jnp.sin(lam_im * dt)
    den = lam_re * lam_re + lam_im * lam_im
    nr, ni = ab_re - 1.0, ab_im
    f_re = (nr * lam_re + ni * lam_im) / den
    f_im = (ni * lam_re - nr * lam_im) / den
    bb_re = f_re[..., None] * b_re - f_im[..., None] * b_im
    bb_im = f_re[..., None] * b_im + f_im[..., None] * b_re
    return ab_re, ab_im, bb_re, bb_im


def _bdiag_in(bb):
    t = bb.reshape(S5_NT, S5_TG, S5_P, S5_GH).transpose(0, 1, 3, 2)
    eye = jnp.eye(S5_TG, dtype=bb.dtype)
    full = t[:, :, :, None, :] * eye[None, :, None, :, None]
    return full.reshape(S5_NT, S5_TU, S5_TS)


def _bdiag_in_extract(dense):
    t = dense.reshape(S5_NT, S5_TG, S5_GH, S5_TG, S5_P)
    diag = jnp.stack([t[:, g, :, g, :] for g in range(S5_TG)], axis=1)
    return diag.transpose(0, 1, 3, 2).reshape(S5_G, S5_P, S5_GH)


def _bdiag_out(c):
    t = c.reshape(S5_NT, S5_TG, S5_GH, S5_P).transpose(0, 1, 3, 2)
    eye = jnp.eye(S5_TG, dtype=c.dtype)
    full = t[:, :, :, None, :] * eye[None, :, None, :, None]
    return full.reshape(S5_NT, S5_TS, S5_TU)


def _bdiag_out_extract(dense):
    t = dense.reshape(S5_NT, S5_TG, S5_P, S5_TG, S5_GH)
    diag = jnp.stack([t[:, g, :, g, :] for g in range(S5_TG)], axis=1)
    return diag.transpose(0, 1, 3, 2).reshape(S5_G, S5_GH, S5_P)


def _cmul(ar, ai, br, bi):
    return ar * br - ai * bi, ar * bi + ai * br


S5_SEG = 8
S5_STEPS = CHUNK // S5_SEG


def _seg_perm(x):
    c = x.shape[1]
    return jnp.swapaxes(x.reshape(S5_SEG, S5_STEPS, c), 0, 1).reshape(CHUNK, c)


def _seg_unperm(x):
    c = x.shape[1]
    return jnp.swapaxes(x.reshape(S5_STEPS, S5_SEG, c), 0, 1).reshape(CHUNK, c)


def _rows(x, p):
    return x[p * S5_SEG:(p + 1) * S5_SEG]


def _s5_tables(ar, ai, tr_scr, ti_scr, wfr_scr, wfi_scr, wbr_scr, wbi_scr):
    row = lax.broadcasted_iota(jnp.int32, (S5_SEG, 1), 0)
    a8r = jnp.broadcast_to(ar, (S5_SEG, S5_TS))
    a8i = jnp.broadcast_to(ai, (S5_SEG, S5_TS))
    pr, pi = a8r, a8i
    for p in range(S5_STEPS):
        tr_scr[p * S5_SEG:(p + 1) * S5_SEG, :] = pr
        ti_scr[p * S5_SEG:(p + 1) * S5_SEG, :] = pi
        if p < S5_STEPS - 1:
            pr, pi = _cmul(pr, pi, a8r, a8i)
    wr, wi = pr, pi
    sh = 1
    while sh < S5_SEG:
        keep = row >= sh
        sr = jnp.where(keep, pltpu.roll(wr, sh, 0), 1.0)
        si = jnp.where(keep, pltpu.roll(wi, sh, 0), 0.0)
        wr, wi = _cmul(wr, wi, sr, si)
        sh *= 2
    wfr_scr[...] = wr
    wfi_scr[...] = wi
    wr, wi = pr, -pi
    sh = 1
    while sh < S5_SEG:
        keep = row < S5_SEG - sh
        sr = jnp.where(keep, pltpu.roll(wr, S5_SEG - sh, 0), 1.0)
        si = jnp.where(keep, pltpu.roll(wi, S5_SEG - sh, 0), 0.0)
        wr, wi = _cmul(wr, wi, sr, si)
        sh *= 2
    wbr_scr[...] = wr
    wbi_scr[...] = wi


def _seg_scan(vr, vi, ar, ai, tr_scr, ti_scr, wr_scr, wi_scr, c0r, c0i, down):
    row = lax.broadcasted_iota(jnp.int32, (S5_SEG, 1), 0)
    sgn = 1.0 if down else -1.0
    order = list(range(S5_STEPS)) if down else list(range(S5_STEPS - 1, -1, -1))
    xr, xi = _rows(vr, order[0]), _rows(vi, order[0])
    loc = {order[0]: (xr, xi)}
    for p in order[1:]:
        mr, mi = _cmul(ar, sgn * ai, xr, xi)
        xr, xi = mr + _rows(vr, p), mi + _rows(vi, p)
        loc[p] = (xr, xi)
    last = S5_STEPS - 1
    mr, mi = tr_scr[last * S5_SEG:(last + 1) * S5_SEG, :], sgn * ti_scr[last * S5_SEG:(last + 1) * S5_SEG, :]
    er, ei = xr, xi
    sh = 1
    while sh < S5_SEG:
        if down:
            keep = row >= sh
            sr, si = pltpu.roll(er, sh, 0), pltpu.roll(ei, sh, 0)
        else:
            keep = row < S5_SEG - sh
            sr, si = pltpu.roll(er, S5_SEG - sh, 0), pltpu.roll(ei, S5_SEG - sh, 0)
        pr, pi = _cmul(mr, mi, jnp.where(keep, sr, 0.0), jnp.where(keep, si, 0.0))
        er, ei = er + pr, ei + pi
        mr, mi = _cmul(mr, mi, mr, mi)
        sh *= 2
    pr, pi = _cmul(wr_scr[...], wi_scr[...], c0r, c0i)
    er, ei = er + pr, ei + pi
    if down:
        nr = jnp.where(row == 0, c0r, pltpu.roll(er, 1, 0))
        ni = jnp.where(row == 0, c0i, pltpu.roll(ei, 1, 0))
    else:
        nr = jnp.where(row == S5_SEG - 1, c0r, pltpu.roll(er, S5_SEG - 1, 0))
        ni = jnp.where(row == S5_SEG - 1, c0i, pltpu.roll(ei, S5_SEG - 1, 0))
    out_r, out_i = [], []
    for p in range(S5_STEPS):
        q = p if down else S5_STEPS - 1 - p
        pr, pi = _cmul(tr_scr[q * S5_SEG:(q + 1) * S5_SEG, :], sgn * ti_scr[q * S5_SEG:(q + 1) * S5_SEG, :], nr, ni)
        out_r.append(loc[p][0] + pr)
        out_i.append(loc[p][1] + pi)
    return jnp.concatenate(out_r, axis=0), jnp.concatenate(out_i, axis=0), (nr, ni), (er, ei)


def _gelu(y):
    c = math.sqrt(2.0 / math.pi)
    return 0.5 * y * (1.0 + jnp.tanh(c * (y + 0.044715 * y * y * y)))


def _gelu_grad(y):
    c = math.sqrt(2.0 / math.pi)
    th = jnp.tanh(c * (y + 0.044715 * y * y * y))
    return 0.5 * (1.0 + th) + 0.5 * y * (1.0 - th * th) * c * (1.0 + 3.0 * 0.044715 * y * y)


def _s5_fwd(proj, ab, bd_b, bd_c, dvec):
    rows = proj.shape[0]
    nc = rows // CHUNK
    tps = S5_FWD_TILES
    ubw = tps * S5_TU
    ub = (2 * RET_QK + 2 * RET_W) // ubw
    ab_re, ab_im = ab
    bre, bim = bd_b
    cre, cim = bd_c

    def body(u_ref, ar_ref, ai_ref, bre_ref, bim_ref, cre_ref, cim_ref, d_ref,
             y_ref, g_ref, er_ref, ei_ref, tr_scr, ti_scr, wfr_scr, wfi_scr, wbr_scr, wbi_scr,
             cr_scr, ci_scr, er_scr, ei_scr):
        n = pl.program_id(1)
        for tt in range(tps):
            cols = slice(tt * S5_TU, (tt + 1) * S5_TU)
            ar, ai = ar_ref[tt], ai_ref[tt]
            trs, tis, wfr, wfi = tr_scr.at[tt], ti_scr.at[tt], wfr_scr.at[tt], wfi_scr.at[tt]

            @pl.when(n == 0)
            def _(tt=tt, ar=ar, ai=ai, trs=trs, tis=tis, wfr=wfr, wfi=wfi):
                _s5_tables(ar, ai, trs, tis, wfr, wfi, wbr_scr.at[tt], wbi_scr.at[tt])
                cr_scr[tt] = jnp.zeros((S5_SEG, S5_TS), F32)
                ci_scr[tt] = jnp.zeros((S5_SEG, S5_TS), F32)

            u = _seg_perm(u_ref[:, cols])
            c0r, c0i = cr_scr[tt], ci_scr[tt]
            er_ref[tt, 0] = c0r
            ei_ref[tt, 0] = c0i
            xr, xi, _, (er, ei) = _seg_scan(_dot(u, bre_ref[tt]), _dot(u, bim_ref[tt]), ar, ai, trs, tis,
                                            wfr, wfi, c0r, c0i, True)
            er_scr[tt] = er
            ei_scr[tt] = ei
            cr_scr[tt] = jnp.broadcast_to(er_scr[tt, S5_SEG - 1:S5_SEG, :], (S5_SEG, S5_TS))
            ci_scr[tt] = jnp.broadcast_to(ei_scr[tt, S5_SEG - 1:S5_SEG, :], (S5_SEG, S5_TS))
            y = _seg_unperm(_dot(xr, cre_ref[tt]) - _dot(xi, cim_ref[tt]) + d_ref[:, cols] * u)
            y_ref[:, cols] = y
            g_ref[:, cols] = _gelu(y).astype(BF16)

    vec = pl.BlockSpec((tps, 1, S5_TS), lambda t, n: (t, 0, 0))
    return pl.pallas_call(
        body, name="s5_fwd", grid=(S5_NT // tps, nc),
        in_specs=[pl.BlockSpec((CHUNK, ubw), lambda t, n: (n, ub + t)), vec, vec,
                  pl.BlockSpec((tps, S5_TU, S5_TS), lambda t, n: (t, 0, 0)),
                  pl.BlockSpec((tps, S5_TU, S5_TS), lambda t, n: (t, 0, 0)),
                  pl.BlockSpec((tps, S5_TS, S5_TU), lambda t, n: (t, 0, 0)),
                  pl.BlockSpec((tps, S5_TS, S5_TU), lambda t, n: (t, 0, 0)),
                  pl.BlockSpec((1, ubw), lambda t, n: (0, t))],
        out_specs=[pl.BlockSpec((CHUNK, ubw), lambda t, n: (n, t)),
                   pl.BlockSpec((CHUNK, ubw), lambda t, n: (n, t)),
                   pl.BlockSpec((tps, 1, 8, S5_TS), lambda t, n: (t, n, 0, 0)),
                   pl.BlockSpec((tps, 1, 8, S5_TS), lambda t, n: (t, n, 0, 0))],
        out_shape=[jax.ShapeDtypeStruct((rows, S5_W), F32), jax.ShapeDtypeStruct((rows, S5_W), BF16),
                   jax.ShapeDtypeStruct((S5_NT, nc, 8, S5_TS), F32),
                   jax.ShapeDtypeStruct((S5_NT, nc, 8, S5_TS), F32)],
        scratch_shapes=[pltpu.VMEM((tps, CHUNK, S5_TS), F32) for _ in range(2)]
        + [pltpu.VMEM((tps, S5_SEG, S5_TS), F32) for _ in range(8)],
        compiler_params=pltpu.CompilerParams(dimension_semantics=("parallel", "arbitrary")),
    )(proj, ab_re.reshape(S5_NT, 1, S5_TS), ab_im.reshape(S5_NT, 1, S5_TS), bre, bim, cre, cim, dvec)


def _s5_bwd(proj, dy, ab, bd_b, bd_c, dvec, entry):
    rows = proj.shape[0]
    nc = rows // CHUNK
    tps = S5_BWD_TILES
    ubw = tps * S5_TU
    ub = (2 * RET_QK + 2 * RET_W) // ubw
    ab_re, ab_im = ab
    bre, bim = bd_b
    cre, cim = bd_c
    er, ei = entry

    def rn(n):
        return nc - 1 - n

    def body(u_ref, dy_ref, ar_ref, ai_ref, bre_ref, bim_ref, cre_ref, cim_ref, d_ref, er_ref, ei_ref,
             du_ref, dbr_ref, dbi_ref, dcr_ref, dci_ref, dar_ref, dai_ref, dd_ref,
             tr_scr, ti_scr, wfr_scr, wfi_scr, wbr_scr, wbi_scr, gr_scr, gi_scr, er_scr, ei_scr):
        n = pl.program_id(1)

        @pl.when(n == 0)
        def _():
            gr_scr[...] = jnp.zeros_like(gr_scr)
            gi_scr[...] = jnp.zeros_like(gi_scr)
            for r in (dbr_ref, dbi_ref, dcr_ref, dci_ref, dar_ref, dai_ref, dd_ref):
                r[...] = jnp.zeros_like(r)

        for tt in range(tps):
            cols = slice(tt * S5_TU, (tt + 1) * S5_TU)
            ar, ai = ar_ref[tt], ai_ref[tt]
            trs, tis = tr_scr.at[tt], ti_scr.at[tt]

            @pl.when(n == 0)
            def _(tt=tt, ar=ar, ai=ai, trs=trs, tis=tis):
                _s5_tables(ar, ai, trs, tis, wfr_scr.at[tt], wfi_scr.at[tt], wbr_scr.at[tt], wbi_scr.at[tt])

            u = _seg_perm(u_ref[:, cols])
            dy = _seg_perm(dy_ref[:, cols])
            xr, xi, (pr, pi), _ = _seg_scan(_dot(u, bre_ref[tt]), _dot(u, bim_ref[tt]), ar, ai, trs, tis,
                                            wfr_scr.at[tt], wfi_scr.at[tt], er_ref[tt, 0], ei_ref[tt, 0], True)
            dcr_ref[tt] += _dot(xr, dy, TN)
            dci_ref[tt] -= _dot(xi, dy, TN)
            gr, gi, _, (er, ei) = _seg_scan(_dot(dy, cre_ref[tt], NT), -_dot(dy, cim_ref[tt], NT), ar, ai, trs, tis,
                                            wbr_scr.at[tt], wbi_scr.at[tt], gr_scr[tt], gi_scr[tt], False)
            er_scr[tt] = er
            ei_scr[tt] = ei
            gr_scr[tt] = jnp.broadcast_to(er_scr[tt, 0:1, :], (S5_SEG, S5_TS))
            gi_scr[tt] = jnp.broadcast_to(ei_scr[tt, 0:1, :], (S5_SEG, S5_TS))
            xpr = jnp.concatenate([pr, xr[:CHUNK - S5_SEG]], axis=0)
            xpi = jnp.concatenate([pi, xi[:CHUNK - S5_SEG]], axis=0)
            dar_ref[tt] += jnp.sum((xpr * gr + xpi * gi).reshape(S5_STEPS, S5_SEG, S5_TS), axis=0)
            dai_ref[tt] += jnp.sum((xpr * gi - xpi * gr).reshape(S5_STEPS, S5_SEG, S5_TS), axis=0)
            dbr_ref[tt] += _dot(u, gr, TN)
            dbi_ref[tt] += _dot(u, gi, TN)
            dd_ref[tt] += jnp.sum((dy * u).reshape(S5_STEPS, S5_SEG, S5_TU), axis=0)
            du = dy * d_ref[:, cols] + _dot(gr, bre_ref[tt], NT) + _dot(gi, bim_ref[tt], NT)
            du_ref[:, cols] = _seg_unperm(du).astype(BF16)

    vec = pl.BlockSpec((tps, 1, S5_TS), lambda t, n: (t, 0, 0))
    acc_b = pl.BlockSpec((tps, S5_TU, S5_TS), lambda t, n: (t, 0, 0))
    acc_c = pl.BlockSpec((tps, S5_TS, S5_TU), lambda t, n: (t, 0, 0))
    acc_a = pl.BlockSpec((tps, 8, S5_TS), lambda t, n: (t, 0, 0))
    ent = pl.BlockSpec((tps, 1, 8, S5_TS), lambda t, n: (t, rn(n), 0, 0))
    return pl.pallas_call(
        body, name="s5_bwd", grid=(S5_NT // tps, nc),
        in_specs=[pl.BlockSpec((CHUNK, ubw), lambda t, n: (rn(n), ub + t)),
                  pl.BlockSpec((CHUNK, ubw), lambda t, n: (rn(n), t)), vec, vec,
                  acc_b, acc_b, acc_c, acc_c, pl.BlockSpec((1, ubw), lambda t, n: (0, t)), ent, ent],
        out_specs=[pl.BlockSpec((CHUNK, ubw), lambda t, n: (rn(n), t)), acc_b, acc_b, acc_c, acc_c, acc_a, acc_a,
                   pl.BlockSpec((tps, 8, S5_TU), lambda t, n: (t, 0, 0))],
        out_shape=[jax.ShapeDtypeStruct((rows, S5_W), BF16),
                   jax.ShapeDtypeStruct((S5_NT, S5_TU, S5_TS), F32), jax.ShapeDtypeStruct((S5_NT, S5_TU, S5_TS), F32),
                   jax.ShapeDtypeStruct((S5_NT, S5_TS, S5_TU), F32), jax.ShapeDtypeStruct((S5_NT, S5_TS, S5_TU), F32),
                   jax.ShapeDtypeStruct((S5_NT, 8, S5_TS), F32), jax.ShapeDtypeStruct((S5_NT, 8, S5_TS), F32),
                   jax.ShapeDtypeStruct((S5_NT, 8, S5_TU), F32)],
        scratch_shapes=[pltpu.VMEM((tps, CHUNK, S5_TS), F32) for _ in range(2)]
        + [pltpu.VMEM((tps, S5_SEG, S5_TS), F32) for _ in range(8)],
        compiler_params=pltpu.CompilerParams(dimension_semantics=("parallel", "arbitrary")),
    )(proj, dy,ab_re.reshape(S5_NT, 1, S5_TS), ab_im.reshape(S5_NT, 1, S5_TS), bre, bim, cre, cim, dvec, er, ei)


def _s5_gate_bwd(dmix, g, t, proj):
    rows = g.shape[0]
    tm = _row_tile(rows, 384)
    ob = RET_W // S5_W
    zb = (2 * RET_QK + 2 * RET_W + S5_W) // S5_W

    def body(do_ref, g_ref, t_ref, z_ref, dz_ref, dt_ref, dg_ref):
        do = do_ref[...]
        gv = g_ref[...].astype(F32)
        z = z_ref[...]
        st = _sigmoid(t_ref[...])
        sg = _sigmoid(z)
        os5 = gv * st
        dz_ref[...] = (do * os5 * sg * (1.0 + z * (1.0 - sg))).astype(BF16)
        dos = do * z * sg
        dt_ref[...] = (dos * gv * st * (1.0 - st)).astype(BF16)
        dg_ref[...] = dos * st

    blk = pl.BlockSpec((tm, S5_W), lambda i: (i, 0))
    return pl.pallas_call(
        body, name="s5_gate_bwd", grid=(rows // tm,),
        in_specs=[pl.BlockSpec((tm, S5_W), lambda i: (i, ob)), blk, blk,
                  pl.BlockSpec((tm, S5_W), lambda i: (i, zb))],
        out_specs=[blk, blk, blk],
        out_shape=[jax.ShapeDtypeStruct((rows, S5_W), BF16), jax.ShapeDtypeStruct((rows, S5_W), BF16),
                   jax.ShapeDtypeStruct((rows, S5_W), F32)],
    )(dmix, g, t, proj)


def _split3(x):
    hi = x.astype(BF16)
    r = x - hi.astype(F32)
    mid = r.astype(BF16)
    lo = (r - mid.astype(F32)).astype(BF16)
    return hi, mid, lo


def _tri_sum(x, upper):
    i = lax.broadcasted_iota(jnp.int32, (CHUNK, CHUNK), 0)
    j = lax.broadcasted_iota(jnp.int32, (CHUNK, CHUNK), 1)
    tri = jnp.where((j >= i) if upper else (j <= i), 1.0, 0.0).astype(BF16)
    hi, mid, lo = _split3(x)
    return _dot(tri, lo) + _dot(tri, mid) + _dot(tri, hi)


def _gla_log_decay(gl, wg, bg, n):
    logit = _dot(gl, wg) + bg
    la = (jnp.minimum(logit, 0.0) - jnp.log(1.0 + jnp.exp(-jnp.abs(logit)))) * (1.0 / GLA_TAU)
    row = lax.broadcasted_iota(jnp.int32, (CHUNK, 1), 0)
    live = jnp.logical_or(n > 0, row >= PAD)
    return logit, jnp.where(live, la, 0.0), live


def _gla_in_specs(rev, nc):
    def cn(n):
        return (nc - 1 - n) if rev else n
    kb = GLA_QK // GLA_DK
    vb = 2 * GLA_QK // GLA_DV
    zb = (2 * GLA_QK + GLA_W) // GLA_DV
    gb = (2 * GLA_QK + 2 * GLA_W) // 128
    return [
        pl.BlockSpec((CHUNK, GLA_DK), lambda h, n: (cn(n), h)),
        pl.BlockSpec((CHUNK, GLA_DK), lambda h, n: (cn(n), kb + h)),
        pl.BlockSpec((CHUNK, GLA_DV), lambda h, n: (cn(n), vb + h)),
        pl.BlockSpec((CHUNK, GLA_DV), lambda h, n: (cn(n), zb + h)),
        pl.BlockSpec((CHUNK, 128), lambda h, n: (cn(n), gb)),
        pl.BlockSpec((128, GLA_DK), lambda h, n: (0, h)),
        pl.BlockSpec((1, GLA_DK), lambda h, n: (0, h)),
        pl.BlockSpec((1, GLA_DV), lambda h, n: (0, h)),
    ]


def _gla_fwd(proj, wgate, bgate, normw):
    rows = proj.shape[0]
    nc = rows // CHUNK

    def body(q_ref, k_ref, v_ref, z_ref, gl_ref, wg_ref, bg_ref, w_ref, o_ref, oc_ref, st_ref, s_scr, b_scr):
        n = pl.program_id(1)

        @pl.when(n == 0)
        def _():
            s_scr[...] = jnp.zeros_like(s_scr)

        q = q_ref[...] * (GLA_DK ** -0.5)
        k = k_ref[...]
        v = v_ref[...]
        vb = v.astype(BF16)
        _, la, _ = _gla_log_decay(gl_ref[...], wg_ref[...], bg_ref[...], n)
        b = _tri_sum(la, False)
        b_scr[...] = b
        b_last = b_scr[CHUNK - 1:CHUNK, :]
        st = s_scr[...]
        st_ref[0, 0] = st
        s_scr[...] = st * jnp.exp(b_last) + _dot(v, k * jnp.exp(b_last - b), TN)
        rowc = lax.broadcasted_iota(jnp.int32, (CHUNK, 1), 0)
        rows16 = lax.broadcasted_iota(jnp.int32, (SUB, 1), 0)
        a_tot = jnp.zeros((CHUNK, CHUNK), F32)
        for s in range(1, NSUB):
            lo = s * SUB
            bref = b_scr[lo - 1:lo, :]
            in_s = jnp.logical_and(rowc >= lo, rowc < lo + SUB)
            qh = q * jnp.exp(jnp.where(in_s, b - bref, -1e30))
            kh = k * jnp.exp(jnp.where(rowc < lo, bref - b, -1e30))
            a_tot = a_tot + _dot(qh, kh, NT)
        lane = lax.broadcasted_iota(jnp.int32, (SUB, CHUNK), 1)
        diag = []
        for s in range(NSUB):
            lo = s * SUB
            qs, bs = q[lo:lo + SUB], b[lo:lo + SUB]
            s_blk = jnp.zeros((SUB, CHUNK), F32)
            for j in range(SUB):
                r = lo + j
                e = jnp.exp(jnp.where(rows16 >= j, bs - b_scr[r:r + 1, :], -1e30))
                col = jnp.sum(qs * k_ref[r:r + 1, :] * e, axis=1, keepdims=True)
                s_blk = jnp.where(lane == r, col, s_blk)
            diag.append(s_blk)
        o = _dot(q * jnp.exp(b), st, NT) + _dot(a_tot + jnp.concatenate(diag, axis=0), vb)
        o_ref[...] = o
        oc_ref[...] = _gate_fwd(o, z_ref[...], w_ref[...]).astype(BF16)

    return pl.pallas_call(
        body, name="gla_fwd", grid=(GLA_HEADS, nc),
        in_specs=_gla_in_specs(False, nc),
        out_specs=[pl.BlockSpec((CHUNK, GLA_DV), lambda h, n: (n, h)),
                   pl.BlockSpec((CHUNK, GLA_DV), lambda h, n: (n, h)),
                   pl.BlockSpec((1, 1, GLA_DV, GLA_DK), lambda h, n: (h, n, 0, 0))],
        out_shape=[jax.ShapeDtypeStruct((rows, GLA_W), F32), jax.ShapeDtypeStruct((rows, GLA_W), BF16),
                   jax.ShapeDtypeStruct((GLA_HEADS, nc, GLA_DV, GLA_DK), F32)],
        scratch_shapes=[pltpu.VMEM((GLA_DV, GLA_DK), F32), pltpu.VMEM((CHUNK, GLA_DK), F32)],
        compiler_params=pltpu.CompilerParams(dimension_semantics=("parallel", "arbitrary")),
    )(proj, proj, proj, proj, proj, wgate, bgate, normw)


def _gla_bwd(proj, wgate, bgate, normw, o_gla, d_oc, states):
    rows = proj.shape[0]
    nc = rows // CHUNK

    def rn(n):
        return nc - 1 - n

    def body(q_ref, k_ref, v_ref, z_ref, gl_ref, wg_ref, bg_ref, w_ref, o_ref, do_ref, st_ref,
             dq_ref, dk_ref, dv_ref, dz_ref, dl_ref, dw_ref, dbg_ref,
             ds_scr, dq_scr, dk_scr, dv_scr, db_scr, b_scr, q_scr):
        n = pl.program_id(1)
        cn = rn(n)

        @pl.when(n == 0)
        def _():
            ds_scr[...] = jnp.zeros_like(ds_scr)
            dw_ref[...] = jnp.zeros_like(dw_ref)
            dbg_ref[...] = jnp.zeros_like(dbg_ref)

        q = q_ref[...] * (GLA_DK ** -0.5)
        k = k_ref[...]
        v = v_ref[...]
        vb = v.astype(BF16)
        do, dz, dw = _gate_bwd(do_ref[...], o_ref[...], z_ref[...], w_ref[...])
        dz_ref[...] = dz.astype(BF16)
        dw_ref[0] += dw
        logit, la, live = _gla_log_decay(gl_ref[...], wg_ref[...], bg_ref[...], cn)
        b = _tri_sum(la, False)
        b_scr[...] = b
        b_last = b_scr[CHUNK - 1:CHUNK, :]
        e_last = jnp.exp(b_last)
        st = st_ref[0, 0]
        g1 = ds_scr[...]
        eb = jnp.exp(b)
        qe = q * eb
        dqe = _dot(do, st)
        dq_scr[...] = dqe * eb
        db_scr[...] = dqe * qe
        ekb = jnp.exp(b_last - b)
        kdec = k * ekb
        dkdec = _dot(v, g1)
        dv_scr[...] = _dot(kdec, g1, NT)
        dk_scr[...] = dkdec * ekb
        wk = dkdec * kdec
        db_scr[...] -= wk
        dbl = jnp.sum(wk, axis=0, keepdims=True) + jnp.sum(g1 * st, axis=0, keepdims=True) * e_last
        ds_scr[...] = g1 * e_last + _dot(do, qe, TN)
        rowc = lax.broadcasted_iota(jnp.int32, (CHUNK, 1), 0)
        rows16 = lax.broadcasted_iota(jnp.int32, (SUB, 1), 0)
        da_full = _dot(do, vb, NT)
        a_tot = jnp.zeros((CHUNK, CHUNK), F32)
        for s in range(1, NSUB):
            lo = s * SUB
            bref = b_scr[lo - 1:lo, :]
            in_s = jnp.logical_and(rowc >= lo, rowc < lo + SUB)
            eq = jnp.exp(jnp.where(in_s, b - bref, -1e30))
            ek = jnp.exp(jnp.where(rowc < lo, bref - b, -1e30))
            qh = q * eq
            kh = k * ek
            a_tot = a_tot + _dot(qh, kh, NT)
            da = jnp.where(in_s, da_full, 0.0)
            dqh = _dot(da, kh)
            dkh = _dot(da, qh, TN)
            tq = dqh * qh
            tk = dkh * kh
            dq_scr[...] += dqh * eq
            dk_scr[...] += dkh * ek
            db_scr[...] += tq - tk
            db_scr[lo - 1:lo, :] += jnp.sum(tk, axis=0, keepdims=True) - jnp.sum(tq, axis=0, keepdims=True)
        dat_full = _dot(vb, do, NT)
        q_scr[...] = q
        lane = lax.broadcasted_iota(jnp.int32, (SUB, CHUNK), 1)
        diag = []
        for s in range(NSUB):
            lo = s * SUB
            qs, ks, bs = q[lo:lo + SUB], k[lo:lo + SUB], b[lo:lo + SUB]
            da_blk, dat_blk = da_full[lo:lo + SUB], dat_full[lo:lo + SUB]
            dqs = jnp.zeros((SUB, GLA_DK), F32)
            dks = jnp.zeros((SUB, GLA_DK), F32)
            dbs = jnp.zeros((SUB, GLA_DK), F32)
            s_blk = jnp.zeros((SUB, CHUNK), F32)
            for j in range(SUB):
                r = lo + j
                kj = k_ref[r:r + 1, :]
                e = jnp.exp(jnp.where(rows16 >= j, bs - b_scr[r:r + 1, :], -1e30))
                p = qs * e * kj
                s_blk = jnp.where(lane == r, jnp.sum(p, axis=1, keepdims=True), s_blk)
                dcol = jnp.sum(jnp.where(lane == r, da_blk, 0.0), axis=1, keepdims=True)
                dqs = dqs + (dcol * e) * kj
                dbs = dbs + dcol * p
            for i in range(SUB):
                r = lo + i
                e = jnp.exp(jnp.where(rows16 <= i, b_scr[r:r + 1, :] - bs, -1e30))
                drow = jnp.sum(jnp.where(lane == r, dat_blk, 0.0), axis=1, keepdims=True)
                nq = (drow * e) * q_scr[r:r + 1, :]
                dks = dks + nq
                dbs = dbs - nq * ks
            dq_scr[lo:lo + SUB, :] += dqs
            dk_scr[lo:lo + SUB, :] += dks
            db_scr[lo:lo + SUB, :] += dbs
            diag.append(s_blk)
        dv_scr[...] += _dot(a_tot + jnp.concatenate(diag, axis=0), do, TN)
        db_scr[CHUNK - 1:CHUNK, :] += dbl
        dla = _tri_sum(db_scr[...], True)
        dlogit = jnp.where(live, dla * (1.0 / GLA_TAU) * _sigmoid(-logit), 0.0)
        dl_ref[...] = dlogit
        dbg_ref[0] += jnp.sum(dlogit, axis=0, keepdims=True)
        dq_ref[...] = (dq_scr[...] * (GLA_DK ** -0.5)).astype(BF16)
        dk_ref[...] = dk_scr[...].astype(BF16)
        dv_ref[...] = dv_scr[...].astype(BF16)

    in_specs = _gla_in_specs(True, nc) + [
        pl.BlockSpec((CHUNK, GLA_DV), lambda h, n: (rn(n), h)),
        pl.BlockSpec((CHUNK, GLA_DV), lambda h, n: (rn(n), h)),
        pl.BlockSpec((1, 1, GLA_DV, GLA_DK), lambda h, n: (h, rn(n), 0, 0)),
    ]
    return pl.pallas_call(
        body, name="gla_bwd", grid=(GLA_HEADS, nc),
        in_specs=in_specs,
        out_specs=[pl.BlockSpec((CHUNK, GLA_DK), lambda h, n: (rn(n), h)),
                   pl.BlockSpec((CHUNK, GLA_DK), lambda h, n: (rn(n), h)),
                   pl.BlockSpec((CHUNK, GLA_DV), lambda h, n: (rn(n), h)),
                   pl.BlockSpec((CHUNK, GLA_DV), lambda h, n: (rn(n), h)),
                   pl.BlockSpec((CHUNK, GLA_DK), lambda h, n: (rn(n), h)),
                   pl.BlockSpec((1, 1, GLA_DV), lambda h, n: (h, 0, 0)),
                   pl.BlockSpec((1, 1, GLA_DK), lambda h, n: (h, 0, 0))],
        out_shape=[jax.ShapeDtypeStruct((rows, GLA_QK), BF16), jax.ShapeDtypeStruct((rows, GLA_QK), BF16),
                   jax.ShapeDtypeStruct((rows, GLA_W), BF16), jax.ShapeDtypeStruct((rows, GLA_W), BF16),
                   jax.ShapeDtypeStruct((rows, GLA_QK), F32),
                   jax.ShapeDtypeStruct((GLA_HEADS, 1, GLA_DV), F32),
                   jax.ShapeDtypeStruct((GLA_HEADS, 1, GLA_DK), F32)],
        scratch_shapes=[pltpu.VMEM((GLA_DV, GLA_DK), F32), pltpu.VMEM((CHUNK, GLA_DK), F32),
                        pltpu.VMEM((CHUNK, GLA_DK), F32), pltpu.VMEM((CHUNK, GLA_DV), F32),
                        pltpu.VMEM((CHUNK, GLA_DK), F32), pltpu.VMEM((CHUNK, GLA_DK), F32),
                        pltpu.VMEM((CHUNK, GLA_DK), F32)],
        compiler_params=pltpu.CompilerParams(dimension_semantics=("parallel", "arbitrary")),
    )(proj, proj, proj, proj, proj, wgate, bgate, normw, o_gla, d_oc, states)


def _adamw(name, w, g, m, v):
    rows, cols = w.shape
    tm = 8
    for cand in range(8, rows + 1, 8):
        if rows % cand == 0 and cand * cols * 4 <= 2 ** 21:
            tm = cand
    c1 = 1.0 - ADAM_B1 ** ADAM_STEP
    c2 = 1.0 - ADAM_B2 ** ADAM_STEP

    def body(w_ref, g_ref, m_ref, v_ref, d_ref, nm_ref, nv_ref):
        gv = g_ref[...]
        nm = ADAM_B1 * m_ref[...] + (1.0 - ADAM_B1) * gv
        nv = ADAM_B2 * v_ref[...] + (1.0 - ADAM_B2) * (gv * gv)
        nm_ref[...] = nm
        nv_ref[...] = nv
        d_ref[...] = -ADAM_LR * ((nm / c1) / (jnp.sqrt(nv / c2) + ADAM_EPS) + ADAM_WD * w_ref[...])

    blk = pl.BlockSpec((tm, cols), lambda i: (i, 0))
    return pl.pallas_call(
        body, name=name, grid=(rows // tm,),
        in_specs=[blk] * 4, out_specs=[blk] * 3,
        out_shape=[jax.ShapeDtypeStruct((rows, cols), F32)] * 3,
    )(w, g, m, v)


def _place():
    x, y, c = lax.axis_index("x"), lax.axis_index("y"), lax.axis_index("c")
    chips = [(1 - x, y), (x, 1 - y), (1 - x, 1 - y)]
    return x, y, c, chips


ANY = pl.BlockSpec(memory_space=pl.ANY)


def _gathered_struct(shape, dtype, kind):
    r, cc = shape
    if kind == "row":
        return jax.ShapeDtypeStruct((N_SHARD * r, cc), dtype)
    if kind == "col":
        return jax.ShapeDtypeStruct((r, N_SHARD * cc), dtype)
    return jax.ShapeDtypeStruct((N_SHARD, r, cc), dtype)


def _cast_place(name, w, kind, mine_arr, dtype):
    r, cc = w.shape
    tr = r
    for cand in (256, 128, 64, 32, 16):
        if r % cand == 0:
            tr = cand
            break
    nb = r // tr
    if kind == "row":
        o_spec = pl.BlockSpec((tr, cc), lambda i, m: (m[0] * nb + i, 0))
    elif kind == "col":
        o_spec = pl.BlockSpec((tr, cc), lambda i, m: (i, m[0]))
    else:
        o_spec = pl.BlockSpec((None, tr, cc), lambda i, m: (m[0], i, 0))
    w_spec = pl.BlockSpec((tr, cc), lambda i, m: (i, 0))

    def body(m_ref, w_ref, o_ref):
        o_ref[...] = w_ref[...].astype(o_ref.dtype)

    return pl.pallas_call(
        body, name=name,
        grid_spec=pltpu.PrefetchScalarGridSpec(
            num_scalar_prefetch=1, grid=(nb,), in_specs=[w_spec], out_specs=o_spec),
        out_shape=_gathered_struct((r, cc), dtype, kind),
    )(mine_arr, w)


def _allreduce_small(buf):
    rows, cols = buf.shape

    def body(in_ref, out_ref, sib_ref, pair_ref, far_ref, send_sems, recv_sems):
        x, y, c, chips = _place()
        sibling = (x, y, 1 - c)
        to_sib = pltpu.make_async_remote_copy(
            src_ref=in_ref, dst_ref=sib_ref, send_sem=send_sems.at[0], recv_sem=recv_sems.at[0],
            device_id=sibling, device_id_type=MESH)
        to_sib.start()
        to_sib.wait()
        pair_ref[...] = in_ref[...] + sib_ref[...]
        far = [pltpu.make_async_remote_copy(
            src_ref=pair_ref, dst_ref=far_ref.at[j], send_sem=send_sems.at[1 + j], recv_sem=recv_sems.at[1 + j],
            device_id=(*chip, c), device_id_type=MESH) for j, chip in enumerate(chips)]
        for cp in far:
            cp.start()
        for cp in far:
            cp.wait()
        out_ref[...] = (pair_ref[...] + far_ref[1]) + (far_ref[0] + far_ref[2])

    vm = pl.BlockSpec(memory_space=pltpu.VMEM)
    return pl.pallas_call(
        body, name="allreduce_small",
        in_specs=[vm], out_specs=vm,
        out_shape=jax.ShapeDtypeStruct((rows, cols), F32),
        scratch_shapes=[pltpu.VMEM((rows, cols), F32), pltpu.VMEM((rows, cols), F32),
                        pltpu.VMEM((3, rows, cols), F32),
                        pltpu.SemaphoreType.DMA((4,)), pltpu.SemaphoreType.DMA((4,))],
        compiler_params=pltpu.CompilerParams(has_side_effects=True),
    )(buf)


def _shard_window(ref, kind, shard_shape, shard, half):
    r, cc = shard_shape
    hr = r // 2
    if kind == "row":
        return ref.at[pl.ds(_mo(shard * r + half * hr, 8), hr), :]
    if kind == "col":
        return ref.at[pl.ds(_mo(half * hr, 8), hr), pl.ds(_mo(shard * cc, 128), cc)]
    return ref.at[shard, pl.ds(_mo(half * hr, 8), hr), :]


HBM = pl.BlockSpec(memory_space=pltpu.HBM)
SEM = pl.BlockSpec(memory_space=pltpu.SEMAPHORE)
DATAFLOW = pltpu.SideEffectType.DATAFLOW_SIDE_EFFECTING


def _in_hbm(a):
    return pltpu.with_memory_space_constraint(a, pltpu.HBM)


def _empty_hbm(shape, dtype):
    return _in_hbm(lax.empty(shape, dtype))


def _copies_start(name, bufs, n_copies, plan, carry):
    nb = len(bufs)

    def body(*refs):
        send_sems, recv_sems = refs[nb + 1], refs[nb + 2]
        for k, (src, dst, to) in enumerate(plan(refs[:nb])):
            pltpu.make_async_remote_copy(src_ref=src, dst_ref=dst, send_sem=send_sems.at[k], recv_sem=recv_sems.at[k],
                                         device_id=to, device_id_type=MESH).start()

    passed = list(bufs) + [carry]
    out = pl.pallas_call(
        body, name=name,
        in_specs=[HBM] * (nb + 1), out_specs=[SEM, SEM] + [HBM] * (nb + 1),
        out_shape=[pltpu.SemaphoreType.DMA((n_copies,)), pltpu.SemaphoreType.DMA((n_copies,))]
        + [pltpu.HBM(a.shape, a.dtype) for a in passed],
        input_output_aliases={i: 2 + i for i in range(nb + 1)},
        compiler_params=pltpu.CompilerParams(has_side_effects=DATAFLOW),
    )(*[_in_hbm(a) for a in passed])
    return out[0], out[1], list(out[2:2 + nb]), out[2 + nb]


def _copies_wait(name, send_sems, recv_sems, bufs, plan, after):
    nb = len(bufs)
    after = list(after) if isinstance(after, (list, tuple)) else [after]

    def body(*refs):
        send, recv = refs[nb], refs[nb + 1]
        for k, (src, dst, to) in enumerate(plan(refs[:nb])):
            cp = pltpu.make_async_remote_copy(src_ref=src, dst_ref=dst, send_sem=send.at[k], recv_sem=recv.at[k],
                                              device_id=to, device_id_type=MESH)
            cp.wait_send()
            cp.wait_recv()

    out = pl.pallas_call(
        body, name=name,
        in_specs=[HBM] * nb + [SEM, SEM] + [ANY] * len(after), out_specs=[HBM] * nb,
        out_shape=[pltpu.HBM(a.shape, a.dtype) for a in bufs],
        input_output_aliases={i: i for i in range(nb)},
        compiler_params=pltpu.CompilerParams(has_side_effects=DATAFLOW),
    )(*bufs, send_sems, recv_sems, *after)
    return list(out)


def _gather_ici_plan(shard_shapes, kinds):
    n_arr = len(kinds)

    def plan(refs):
        x, y, c, chips = _place()
        out = []
        for i in range(n_arr):
            w = _shard_window(refs[i], kinds[i], shard_shapes[i], 2 * x + y, c)
            out += [(w, w, (*chip, c)) for chip in chips]
        return out

    return plan


def _gather_d2d_plan(shard_shapes, kinds):
    n_arr = len(kinds)

    def plan(refs):
        x, y, c, chips = _place()
        out = []
        for i in range(n_arr):
            for chip in chips:
                w = _shard_window(refs[i], kinds[i], shard_shapes[i], 2 * chip[0] + chip[1], c)
                out.append((w, w, (x, y, 1 - c)))
        return out

    return plan


def _rs_pair_plan(kinds, shard_shapes):
    n_arr = len(kinds)

    def plan(refs):
        x, y, c, _ = _place()
        out = []
        for i in range(n_arr):
            for s in range(N_SHARD):
                out.append((_shard_window(refs[i], kinds[i], shard_shapes[i], s, 1 - c), refs[n_arr + i].at[s],
                            (x, y, 1 - c)))
        return out

    return plan


def _rs_chip_plan(n_arr):
    def plan(refs):
        x, y, c, chips = _place()
        out = []
        for i in range(n_arr):
            for j, chip in enumerate(chips):
                out.append((refs[i].at[2 * chip[0] + chip[1]], refs[n_arr + i].at[j], (*chip, c)))
        return out

    return plan


def _rs_pair_add(name, grad, got, kind, shard_shape, c):
    r, cc = shard_shape
    hr = r // 2
    tr = hr
    for cand in (256, 128, 64, 32, 16):
        if hr % cand == 0:
            tr = cand
            break
    nb = hr // tr

    if kind == "row":
        g_spec = pl.BlockSpec((tr, cc), lambda s, i, cr: (s * 2 * nb + cr[0] * nb + i, 0))
    elif kind == "col":
        g_spec = pl.BlockSpec((tr, cc), lambda s, i, cr: (cr[0] * nb + i, s))
    else:
        g_spec = pl.BlockSpec((None, tr, cc), lambda s, i, cr: (s, cr[0] * nb + i, 0))
    t_spec = pl.BlockSpec((None, tr, cc), lambda s, i, cr: (s, i, 0))

    def body(c_ref, g_ref, t_ref, p_ref, pb_ref):
        p = g_ref[...] + t_ref[...]
        p_ref[...] = p
        pb_ref[...] = p.astype(BF16)

    return pl.pallas_call(
        body, name=name,
        grid_spec=pltpu.PrefetchScalarGridSpec(
            num_scalar_prefetch=1, grid=(N_SHARD, nb),
            in_specs=[g_spec, t_spec], out_specs=[t_spec, t_spec]),
        out_shape=[jax.ShapeDtypeStruct((N_SHARD, hr, cc), F32), jax.ShapeDtypeStruct((N_SHARD, hr, cc), BF16)],
    )(c, grad, got)


def _rs_chip_add(name, pair_f32, got, shard_shape, mine_c):
    r, cc = shard_shape
    hr = r // 2
    tr = hr
    for cand in (256, 128, 64, 32, 16):
        if hr % cand == 0:
            tr = cand
            break
    nb = hr // tr

    def body(mc_ref, p_ref, t0_ref, t1_ref, t2_ref, o_ref):
        o_ref[...] = (p_ref[...] + t1_ref[...].astype(F32)) + (t0_ref[...].astype(F32) + t2_ref[...].astype(F32))

    def far(j):
        return pl.BlockSpec((None, tr, cc), lambda i, mc: (j, i, 0))

    return pl.pallas_call(
        body, name=name,
        grid_spec=pltpu.PrefetchScalarGridSpec(
            num_scalar_prefetch=1, grid=(nb,),
            in_specs=[pl.BlockSpec((None, tr, cc), lambda i, mc: (mc[0], i, 0)), far(0), far(1), far(2)],
            out_specs=pl.BlockSpec((tr, cc), lambda i, mc: (mc[1] * nb + i, 0))),
        out_shape=jax.ShapeDtypeStruct((r, cc), F32),
    )(mine_c, pair_f32, got, got, got)


def _rs_pair_share(name, halves, shard_shapes):
    n_arr = len(halves)

    def body(*refs):
        ins = refs[:n_arr]
        outs = refs[n_arr:2 * n_arr]
        send_sems, recv_sems = refs[2 * n_arr:]
        x, y, c, _ = _place()
        sibling = (x, y, 1 - c)
        cps = []
        for i in range(n_arr):
            hr = shard_shapes[i][0] // 2
            rows = pl.ds(_mo(c * hr, 8), hr)
            cp = pltpu.make_async_remote_copy(
                src_ref=outs[i].at[rows, :], dst_ref=outs[i].at[rows, :],
                send_sem=send_sems.at[i], recv_sem=recv_sems.at[i],
                device_id=sibling, device_id_type=MESH)
            cp.start()
            cps.append(cp)
        for cp in cps:
            cp.wait()

    return pl.pallas_call(
        body, name=name,
        in_specs=[ANY] * n_arr, out_specs=[ANY] * n_arr,
        out_shape=[jax.ShapeDtypeStruct(s, F32) for s in shard_shapes],
        input_output_aliases={i: i for i in range(n_arr)},
        scratch_shapes=[pltpu.SemaphoreType.DMA((n_arr,)), pltpu.SemaphoreType.DMA((n_arr,))],
        compiler_params=pltpu.CompilerParams(has_side_effects=True),
    )(*halves)


def _pack(arrays):
    flat = jnp.concatenate([a.reshape(-1).astype(F32) for a in arrays])
    return jnp.pad(flat, (0, (-flat.shape[0]) % (16 * SMALL_COLS))).reshape(-1, SMALL_COLS)


def _unpack(buf, shapes):
    flat = buf.reshape(-1)
    out = []
    off = 0
    for s in shapes:
        size = math.prod(s)
        out.append(flat[off:off + size].reshape(s))
        off += size
    return out


def kernel(x, meta, norm_ab_w, w_in_ab, ret_norm_w, s5_lam_re, s5_lam_im, s5_log_dt, s5_b_re, s5_b_im, s5_c_re, s5_c_im, s5_d, s5_w_glu, w_out_ab, norm_c_w, w_in_c, gla_w_gate, gla_b_gate, gla_norm_w, w_out_c, final_norm_w, loss_target, m_meta, m_norm_ab_w, m_w_in_ab, m_ret_norm_w, m_s5_lam_re, m_s5_lam_im, m_s5_log_dt, m_s5_b_re, m_s5_b_im, m_s5_c_re, m_s5_c_im, m_s5_d, m_s5_w_glu, m_w_out_ab, m_norm_c_w, m_w_in_c, m_gla_w_gate, m_gla_b_gate, m_gla_norm_w, m_w_out_c, m_final_norm_w, v_meta, v_norm_ab_w, v_w_in_ab, v_ret_norm_w, v_s5_lam_re, v_s5_lam_im, v_s5_log_dt, v_s5_b_re, v_s5_b_im, v_s5_c_re, v_s5_c_im, v_s5_d, v_s5_w_glu, v_w_out_ab, v_norm_c_w, v_w_in_c, v_gla_w_gate, v_gla_b_gate, v_gla_norm_w, v_w_out_c, v_final_norm_w):
    seq = x.shape[1]
    rows = seq + CHUNK
    xi, yi, ci = lax.axis_index("x"), lax.axis_index("y"), lax.axis_index("c")
    mine = 2 * xi + yi
    c_arr = jnp.reshape(ci, (1,)).astype(jnp.int32)
    mine_c = jnp.stack([mine, ci]).astype(jnp.int32)

    mine_arr = jnp.reshape(mine, (1,)).astype(jnp.int32)
    small_shard = _pack([meta, norm_c_w, gla_norm_w, gla_b_gate, gla_w_gate[0]])
    first_kinds = ["col", "stack"]
    first_shapes = [w_in_ab.shape[1:], small_shard.shape]
    first_ici = _gather_ici_plan(first_shapes, first_kinds)
    first_d2d = _gather_d2d_plan(first_shapes, first_kinds)
    f_send, f_recv, f_bufs, small_shard = _copies_start(
        "gather_first_ici_start",
        [_cast_place("place_w_in_ab", w_in_ab[0], "col", mine_arr, BF16),
         _cast_place("place_small", small_shard, "stack", mine_arr, F32)], 6, first_ici, small_shard)
    late = [("w_out_ab", w_out_ab[0]), ("w_in_c", w_in_c[0]), ("w_out_c", w_out_c[0]), ("w_glu", s5_w_glu[0])]
    late_kinds = ["row", "stack", "row", "row"]
    late_shapes = [a.shape for _, a in late]
    ici_plan = _gather_ici_plan(late_shapes, late_kinds)
    d2d_plan = _gather_d2d_plan(late_shapes, late_kinds)
    n_late = 3 * len(late)
    g_bufs = [_cast_place("place_" + nm, a, kd, mine_arr, BF16) for (nm, a), kd in zip(late, late_kinds)]
    cosf, sinf = _rope_tables(rows)
    rtab = _ret_tables()
    ab_re, ab_im, bb_re, bb_im = _s5_discretize(s5_lam_re[0], s5_lam_im[0], s5_log_dt[0], s5_b_re[0], s5_b_im[0])
    ab = (ab_re, ab_im)
    bd_b = (_bdiag_in(bb_re), _bdiag_in(bb_im))
    bd_c = (_bdiag_out(s5_c_re[0]), _bdiag_out(s5_c_im[0]))
    f_bufs = _copies_wait("gather_first_ici_wait", f_send, f_recv, f_bufs, first_ici,
                          [cosf, sinf, bd_b[0], bd_b[1], bd_c[0], bd_c[1]] + g_bufs + list(rtab))
    f_send, f_recv, f_bufs, cosf = _copies_start("gather_first_d2d_start", f_bufs, 6, first_d2d, cosf)
    wab, small_all = _copies_wait("gather_first_d2d_wait", f_send, f_recv, f_bufs, first_d2d, cosf)
    g_send, g_recv, g_bufs, wab = _copies_start("gather_late_ici_start", g_bufs, n_late, ici_plan, wab)
    q4 = D_MODEL // N_SHARD
    g4 = GLA_QK // N_SHARD
    parts = [_unpack(small_all[j], [(N_META, q4), (1, q4), (1, q4), (1, g4), (GLA_RANK, g4)]) for j in range(N_SHARD)]
    meta_f, norm_c_f, gla_norm_f, bgate_f, wgate_f = [jnp.concatenate([p[i] for p in parts], axis=1) for i in range(5)]
    wgate_pad = jnp.pad(wgate_f, ((0, 128 - GLA_RANK), (0, 0)))

    h0, hn0 = _embed_norm(x[0], meta_f, norm_ab_w)

    tm = _row_tile(rows, 1408)
    tmk = _row_tile(rows, 1408)
    proj0 = _matmul("in_proj_ab", hn0, wab, NN, rows, IN_AB, D_MODEL, tm=tm, tn=512, tk=D_MODEL)
    o_ret, o_a, ret_states = _ret_fwd(proj0, cosf, sinf, rtab, ret_norm_w)
    g_bufs = _copies_wait("gather_late_ici_wait", g_send, g_recv, g_bufs, ici_plan, o_a)
    g_send, g_recv, g_bufs, proj0 = _copies_start("gather_late_d2d_start", g_bufs, n_late, d2d_plan, proj0)
    y_s5, g_s5, s5_er, s5_ei = _s5_fwd(proj0, ab, bd_b, bd_c, s5_d)
    wout_ab, wc_st, wout_c, wglu = _copies_wait("gather_late_d2d_wait", g_send, g_recv, g_bufs, d2d_plan, g_s5)
    wc = jnp.concatenate([wc_st[j] for j in range(N_SHARD)] + [jnp.zeros((D_MODEL, IN_C_PAD - IN_C), BF16)], axis=1)
    zb_blk = (2 * RET_QK + 2 * RET_W + S5_W) // 512

    def glu_out(acc, gv, z):
        return gv.astype(F32) * _sigmoid(acc) * (z * _sigmoid(z))

    t_glu = _matmul("glu", g_s5, wglu, NN, rows, S5_W, S5_W, tm=tm, tn=512, tk=S5_W)
    o_b = _matmul("glu_out", g_s5, wglu, NN, rows, S5_W, S5_W, tm=tm, tn=512, tk=S5_W, out_dtype=BF16,
                  extras=[(g_s5, (tm, 512), lambda i, j, kk: (i, j)),
                          (proj0, (tm, 512), lambda i, j, kk: (i, zb_blk + j))],
                  epilogue=glu_out)
    mix = jnp.concatenate([o_a, o_b], axis=1)
    h1 = _matmul("out_proj_ab", mix, wout_ab, NN, rows, D_MODEL, OUT_AB, tm=tm, tn=512, tk=1024,
                 extras=[(h0, (tm, 512), lambda i, j, kk: (i, j))], epilogue=lambda acc, r: acc + r)

    hn1 = _rms_fwd("norm_c", h1, norm_c_f)
    proj1 = _matmul("in_proj_c", hn1, wc, NN, rows, IN_C_PAD, D_MODEL, tm=tm, tn=896, tk=D_MODEL)
    o_gla, o_c, gla_states = _gla_fwd(proj1, wgate_pad, bgate_f, gla_norm_f)
    h2 = _matmul("out_proj_c", o_c, wout_c, NN, rows, D_MODEL, GLA_W, tm=tm, tn=512, tk=GLA_W,
                 extras=[(h1, (tm, 512), lambda i, j, kk: (i, j))], epilogue=lambda acc, r: acc + r)
    loss_dev, dh2, d_final = _final_loss(h2, final_norm_w.reshape(1, D_MODEL), loss_target[0])

    g_wout_c = _matmul("d_w_out_c", o_c, dh2, TN, GLA_W, D_MODEL, rows, tm=1024, tn=1024, tk=tmk)
    d_oc = _matmul("d_o_c", dh2, wout_c, NT, rows, GLA_W, D_MODEL, tm=tm, tn=512, tk=1024)
    dq1, dk1, dv1, dz1, dlogit, d_gla_norm, d_bgate = _gla_bwd(proj1, wgate_pad, bgate_f, gla_norm_f, o_gla, d_oc, gla_states)
    gl_blk = (2 * GLA_QK + 2 * GLA_W) // 128
    dgl = _matmul("d_g_low", dlogit, wgate_pad, NT, rows, 128, GLA_QK, tm=tm, tn=128, tk=GLA_QK, out_dtype=BF16)
    g_wgate = _matmul("d_w_gate", proj1, dlogit, TN, 128, GLA_QK, rows, tm=128, tn=GLA_QK, tk=tmk, a_off=(0, gl_blk))
    dproj1 = jnp.concatenate([dq1, dk1, dv1, dz1, dgl], axis=1)
    g_wc = _matmul("d_w_in_c", hn1, dproj1, TN, D_MODEL, IN_C_PAD, rows, tm=1024, tn=896, tk=tmk)
    dhn1 = _matmul("d_hn1", dproj1, wc, NT, rows, D_MODEL, IN_C_PAD, tm=tm, tn=512, tk=896)
    dh1, d_norm_c = _rms_bwd("norm_c_bwd", dhn1, h1, norm_c_f, dh2)

    g_wout_ab = _matmul("d_w_out_ab", mix, dh1, TN, OUT_AB, D_MODEL, rows, tm=1024, tn=1024, tk=tmk)
    dmix = _matmul("d_mix", dh1, wout_ab, NT, rows, OUT_AB, D_MODEL, tm=tm, tn=512, tk=1024)
    dq0, dk0, dv0, dza, d_ret_norm = _ret_bwd(proj0, cosf, sinf, rtab, ret_norm_w, o_ret, dmix, ret_states)
    dzb, dt_glu, dg_direct = _s5_gate_bwd(dmix, g_s5, t_glu, proj0)
    g_wglu = _matmul("d_w_glu", g_s5, dt_glu, TN, S5_W, S5_W, rows, tm=1024, tn=1024, tk=tmk)
    dy_s5 = _matmul("d_y_s5", dt_glu, wglu, NT, rows, S5_W, S5_W, tm=tm, tn=512, tk=S5_W,
                    extras=[(dg_direct, (tm, 512), lambda i, j, kk: (i, j)),
                            (y_s5, (tm, 512), lambda i, j, kk: (i, j))],
                    epilogue=lambda acc, dg, yv: (acc + dg) * _gelu_grad(yv))
    g_wc_st = jnp.stack([g_wc[:, j * (IN_C // N_SHARD):(j + 1) * (IN_C // N_SHARD)] for j in range(N_SHARD)])
    rs1_names = ["w_out_ab", "w_in_c", "w_out_c", "w_glu"]
    rs1_shapes = [w_out_ab.shape[1:], w_in_c.shape[1:], w_out_c.shape[1:], s5_w_glu.shape[1:]]
    rs1_plan = _rs_pair_plan(late_kinds, rs1_shapes)
    rs1_land = [_empty_hbm((N_SHARD, r // 2, cc), F32) for (r, cc) in rs1_shapes]
    p_send, p_recv, p_bufs, dy_s5 = _copies_start("rs1_pair_start", [g_wout_ab, g_wc_st, g_wout_c, g_wglu] + rs1_land,
                                                  N_SHARD * 4, rs1_plan, dy_s5)
    du, dbr_d, dbi_d, dcr_d, dci_d, dar_p, dai_p, dd_p = _s5_bwd(proj0, dy_s5, ab, bd_b, bd_c, s5_d, (s5_er, s5_ei))
    p_bufs = _copies_wait("rs1_pair_wait", p_send, p_recv, p_bufs, rs1_plan, du)
    rs1_pairs = [_rs_pair_add("rs_pair_add_" + nm, g, t, kd, ss, c_arr)
                 for nm, g, t, kd, ss in zip(rs1_names, p_bufs[:4], p_bufs[4:], late_kinds, rs1_shapes)]
    dproj0 = jnp.concatenate([dq0, dk0, dv0, dza, du, dzb], axis=1)
    rs1_chip_plan = _rs_chip_plan(4)
    rs1_land2 = [_empty_hbm((3, r // 2, cc), BF16) for (r, cc) in rs1_shapes]
    c_send, c_recv, c_bufs, dproj0 = _copies_start("rs1_chip_start", [p[1] for p in rs1_pairs] + rs1_land2, 12,
                                                   rs1_chip_plan, dproj0)
    g_wab = _matmul("d_w_in_ab", hn0, dproj0, TN, D_MODEL, IN_AB, rows, tm=1024, tn=1024, tk=tmk)
    rs2_shapes = [w_in_ab.shape[1:]]
    rs2_plan = _rs_pair_plan(["col"], rs2_shapes)
    rs2_land = [_empty_hbm((N_SHARD, rs2_shapes[0][0] // 2, rs2_shapes[0][1]), F32)]
    q_send, q_recv, q_bufs, dproj0 = _copies_start("rs2_pair_start", [g_wab] + rs2_land, N_SHARD, rs2_plan, dproj0)
    dhn0 = _matmul("d_hn0", dproj0, wab, NT, rows, D_MODEL, IN_AB, tm=tm, tn=512, tk=2048)
    grad_x, d_meta, d_norm_ab = _rms_bwd_embed(dhn0, h0, norm_ab_w, dh1)
    c_bufs = _copies_wait("rs1_chip_wait", c_send, c_recv, c_bufs, rs1_chip_plan, grad_x)
    grad_x = grad_x[None]
    rs1_halves = [_rs_chip_add("rs_chip_add_" + nm, p[0], t, ss, mine_c)
                  for nm, p, t, ss in zip(rs1_names, rs1_pairs, c_bufs[4:], rs1_shapes)]
    g_w_out_ab, g_w_in_c, g_w_out_c, g_w_glu = _rs_pair_share("rs1_pair_share", rs1_halves, rs1_shapes)
    q_bufs = _copies_wait("rs2_pair_wait", q_send, q_recv, q_bufs, rs2_plan, g_w_glu)
    rs2_pair = _rs_pair_add("rs_pair_add_w_in_ab", q_bufs[0], q_bufs[1], "col", rs2_shapes[0], c_arr)
    rs2_chip_plan = _rs_chip_plan(1)
    rs2_land2 = [_empty_hbm((3, rs2_shapes[0][0] // 2, rs2_shapes[0][1]), BF16)]

    d_ab_re = jnp.sum(dar_p, axis=1).reshape(S5_G, S5_P)
    d_ab_im = jnp.sum(dai_p, axis=1).reshape(S5_G, S5_P)
    small_local = [loss_dev, d_meta, d_norm_ab, d_ret_norm.reshape(1, RET_W), d_ab_re, d_ab_im,
                   _bdiag_in_extract(dbr_d), _bdiag_in_extract(dbi_d),
                   _bdiag_out_extract(dcr_d), _bdiag_out_extract(dci_d),
                   jnp.sum(dd_p, axis=1).reshape(1, S5_W), d_norm_c, g_wgate[:GLA_RANK],
                   d_bgate.reshape(1, GLA_QK), d_gla_norm.reshape(1, GLA_W), d_final]
    small_shapes = [a.shape for a in small_local]
    summed_buf = _allreduce_small(_pack(small_local))
    r_send, r_recv, r_bufs, summed_buf = _copies_start("rs2_chip_start", [rs2_pair[1]] + rs2_land2, 3, rs2_chip_plan,
                                                       summed_buf)
    summed = _unpack(summed_buf, small_shapes)
    (loss, g_meta_f, g_norm_ab, g_ret_norm, g_ab_re, g_ab_im, g_bb_re, g_bb_im, g_c_re, g_c_im, g_d,
     g_norm_c_f, g_wgate_f, g_bgate_f, g_gla_norm_f, g_final) = summed
    _, s5_vjp = jax.vjp(_s5_discretize, s5_lam_re[0], s5_lam_im[0], s5_log_dt[0], s5_b_re[0], s5_b_im[0])
    g_lam_re, g_lam_im, g_log_dt, g_b_re, g_b_im = s5_vjp((g_ab_re, g_ab_im, g_bb_re, g_bb_im))

    def take(a, width):
        return lax.dynamic_slice_in_dim(a, mine * width, width, axis=1)

    grads = {
        "meta": take(g_meta_f, q4), "norm_ab_w": g_norm_ab, "ret_norm_w": g_ret_norm,
        "s5_lam_re": g_lam_re[None], "s5_lam_im": g_lam_im[None], "s5_log_dt": g_log_dt[None],
        "s5_b_re": g_b_re[None], "s5_b_im": g_b_im[None], "s5_c_re": g_c_re[None], "s5_c_im": g_c_im[None],
        "s5_d": g_d, "s5_w_glu": g_w_glu[None], "w_out_ab": g_w_out_ab[None], "norm_c_w": take(g_norm_c_f, q4),
        "w_in_c": g_w_in_c[None], "gla_w_gate": take(g_wgate_f, g4)[None], "gla_b_gate": take(g_bgate_f, g4),
        "gla_norm_w": take(g_gla_norm_f, q4), "w_out_c": g_w_out_c[None], "final_norm_w": g_final.reshape(D_MODEL),
    }
    weights = dict(meta=meta, norm_ab_w=norm_ab_w, w_in_ab=w_in_ab, ret_norm_w=ret_norm_w, s5_lam_re=s5_lam_re,
                   s5_lam_im=s5_lam_im, s5_log_dt=s5_log_dt, s5_b_re=s5_b_re, s5_b_im=s5_b_im, s5_c_re=s5_c_re,
                   s5_c_im=s5_c_im, s5_d=s5_d, s5_w_glu=s5_w_glu, w_out_ab=w_out_ab, norm_c_w=norm_c_w,
                   w_in_c=w_in_c, gla_w_gate=gla_w_gate, gla_b_gate=gla_b_gate, gla_norm_w=gla_norm_w,
                   w_out_c=w_out_c, final_norm_w=final_norm_w)
    m_in = dict(meta=m_meta, norm_ab_w=m_norm_ab_w, w_in_ab=m_w_in_ab, ret_norm_w=m_ret_norm_w,
                s5_lam_re=m_s5_lam_re, s5_lam_im=m_s5_lam_im, s5_log_dt=m_s5_log_dt, s5_b_re=m_s5_b_re,
                s5_b_im=m_s5_b_im, s5_c_re=m_s5_c_re, s5_c_im=m_s5_c_im, s5_d=m_s5_d, s5_w_glu=m_s5_w_glu,
                w_out_ab=m_w_out_ab, norm_c_w=m_norm_c_w, w_in_c=m_w_in_c, gla_w_gate=m_gla_w_gate,
                gla_b_gate=m_gla_b_gate, gla_norm_w=m_gla_norm_w, w_out_c=m_w_out_c, final_norm_w=m_final_norm_w)
    v_in = dict(meta=v_meta, norm_ab_w=v_norm_ab_w, w_in_ab=v_w_in_ab, ret_norm_w=v_ret_norm_w,
                s5_lam_re=v_s5_lam_re, s5_lam_im=v_s5_lam_im, s5_log_dt=v_s5_log_dt, s5_b_re=v_s5_b_re,
                s5_b_im=v_s5_b_im, s5_c_re=v_s5_c_re, s5_c_im=v_s5_c_im, s5_d=v_s5_d, s5_w_glu=v_s5_w_glu,
                w_out_ab=v_w_out_ab, norm_c_w=v_norm_c_w, w_in_c=v_w_in_c, gla_w_gate=v_gla_w_gate,
                gla_b_gate=v_gla_b_gate, gla_norm_w=v_gla_norm_w, w_out_c=v_w_out_c, final_norm_w=v_final_norm_w)
    order = list(weights)
    big_names = ["s5_w_glu", "w_out_ab", "w_in_c", "w_out_c", "w_in_ab"]
    small_names = [nm for nm in order if nm not in big_names]
    delta, new_m, new_v = {}, {}, {}

    def big_update(nm):
        shp = weights[nm].shape
        if shp[-1] % 128 == 0:
            d2, m2, v2 = _adamw("adamw_" + nm, weights[nm][0], grads[nm][0], m_in[nm][0], v_in[nm][0])
            delta[nm], new_m[nm], new_v[nm] = d2.reshape(shp), m2.reshape(shp), v2.reshape(shp)
            return
        def lanes(a):
            return jnp.transpose(a).reshape(-1, 128)

        def back(a):
            return jnp.transpose(a.reshape(shp[2], shp[1])).reshape(shp)

        d2, m2, v2 = _adamw("adamw_" + nm, lanes(weights[nm][0]), lanes(grads[nm][0]), lanes(m_in[nm][0]),
                            lanes(v_in[nm][0]))
        delta[nm], new_m[nm], new_v[nm] = back(d2), back(m2), back(v2)

    for nm in big_names[:-1]:
        big_update(nm)
    sshapes = [weights[nm].shape for nm in small_names]
    d2, m2, v2 = _adamw("adamw_small", _pack([weights[nm] for nm in small_names]),
                        _pack([grads[nm] for nm in small_names]), _pack([m_in[nm] for nm in small_names]),
                        _pack([v_in[nm] for nm in small_names]))
    for nm, dd, mm, vv in zip(small_names, _unpack(d2, sshapes), _unpack(m2, sshapes), _unpack(v2, sshapes)):
        delta[nm], new_m[nm], new_v[nm] = dd, mm, vv
    r_bufs = _copies_wait("rs2_chip_wait", r_send, r_recv, r_bufs, rs2_chip_plan,
                          [v2] + [new_v[nm] for nm in big_names[:-1]])
    rs2_half = _rs_chip_add("rs_chip_add_w_in_ab", rs2_pair[0], r_bufs[1], rs2_shapes[0], mine_c)
    grads["w_in_ab"] = _rs_pair_share("rs2_pair_share", [rs2_half], rs2_shapes)[0][None]
    big_update("w_in_ab")
    grads = {nm: grads[nm].reshape(weights[nm].shape) for nm in order}
    return (loss.reshape(()), grad_x, *[grads[nm] for nm in order], *[delta[nm] for nm in order],
            *[new_m[nm] for nm in order], *[new_v[nm] for nm in order])
```

```python
import functools
import math

import jax
import jax.numpy as jnp
from jax import lax
from jax.experimental import pallas as pl
from jax.experimental.pallas import tpu as pltpu

F32 = jnp.float32
BF16 = jnp.bfloat16
MESH = pl.DeviceIdType.MESH

D_MODEL = 2048
N_META = 16
CHUNK = 128
SUB = 16
NSUB = CHUNK // SUB
PAD = CHUNK - N_META
EPS = 1e-6

RET_HEADS = 8
RET_DK = 128
RET_DV = 256
RET_QK = RET_HEADS * RET_DK
RET_W = RET_HEADS * RET_DV
ROPE_BASE = 10000.0

S5_W = 1024
S5_GH = 16
S5_G = S5_W // S5_GH
S5_P = 64
S5_TG = 8
S5_NT = S5_G // S5_TG
S5_TU = S5_TG * S5_GH
S5_TS = S5_TG * S5_P
S5_FWD_TILES = 2
S5_BWD_TILES = 1

GLA_HEADS = 4
GLA_DK = 256
GLA_DV = 512
GLA_QK = GLA_HEADS * GLA_DK
GLA_W = GLA_HEADS * GLA_DV
GLA_RANK = 16
GLA_TAU = 16.0

IN_AB = 2 * RET_QK + 2 * RET_W + 2 * S5_W
OUT_AB = RET_W + S5_W
IN_C = 2 * GLA_QK + 2 * GLA_W + GLA_RANK
IN_C_PAD = 2 * GLA_QK + 2 * GLA_W + 128

ADAM_LR = 0.001
ADAM_B1 = 0.9
ADAM_B2 = 0.999
ADAM_EPS = 1e-08
ADAM_WD = 0.01
ADAM_STEP = 10

N_SHARD = 4
SMALL_COLS = 512

NN = (((1,), (0,)), ((), ()))
NT = (((1,), (1,)), ((), ()))
TN = (((0,), (0,)), ((), ()))


def _dot(a, b, dims=NN):
    return lax.dot_general(a.astype(BF16), b.astype(BF16), dims, preferred_element_type=F32)


def _mo(v, m):
    return v if isinstance(v, int) else pl.multiple_of(v, m)


def _sigmoid(x):
    return 1.0 / (1.0 + jnp.exp(-x))


def _row_tile(rows, cap):
    n = rows // CHUNK
    best = 1
    for d in range(1, n + 1):
        if n % d == 0 and d * CHUNK <= cap:
            best = d
    return best * CHUNK


def _col_tile(cols, cap):
    n = cols // 128
    best = 1
    for d in range(1, n + 1):
        if n % d == 0 and d * 128 <= cap:
            best = d
    return best * 128


def _matmul(name, a, b, dims, m, n, k, *, tm, tn, tk, out_dtype=F32, a_off=(0, 0), b_off=(0, 0),
            extras=(), epilogue=None, out_shape=None, out_spec=None, segs=None, into=None):
    if segs is None:
        segs = [(a, a_off, b, b_off, k, tk)]
    assert m % tm == 0 and n % tn == 0, (name, m, n, tm, tn)
    starts, counts = [], []
    nk = 0
    for (_, _, _, _, ks, tks) in segs:
        assert ks % tks == 0, (name, ks, tks)
        starts.append(nk)
        counts.append(ks // tks)
        nk += ks // tks
    in_specs, operands = [], []
    for s, (sa, (ar, ac), sb, (br, bc), _, tks) in enumerate(segs):
        def kpos(kk, st=starts[s], cnt=counts[s]):
            return jnp.clip(kk - st, 0, cnt - 1) if len(segs) > 1 else kk

        if dims == NN:
            a_spec = pl.BlockSpec((tm, tks), lambda i, j, kk, p=kpos, r=ar, c=ac: (i + r, p(kk) + c))
            b_spec = pl.BlockSpec((tks, tn), lambda i, j, kk, p=kpos, r=br, c=bc: (p(kk) + r, j + c))
        elif dims == NT:
            a_spec = pl.BlockSpec((tm, tks), lambda i, j, kk, p=kpos, r=ar, c=ac: (i + r, p(kk) + c))
            b_spec = pl.BlockSpec((tn, tks), lambda i, j, kk, p=kpos, r=br, c=bc: (j + r, p(kk) + c))
        else:
            a_spec = pl.BlockSpec((tks, tm), lambda i, j, kk, p=kpos, r=ar, c=ac: (p(kk) + r, i + c))
            b_spec = pl.BlockSpec((tks, tn), lambda i, j, kk, p=kpos, r=br, c=bc: (p(kk) + r, j + c))
        in_specs += [a_spec, b_spec]
        operands += [sa, sb]
    n_seg = len(segs)
    n_extra = len(extras)
    if out_shape is None:
        out_shape = jax.ShapeDtypeStruct((m, n), out_dtype)

    def body(*refs):
        e_refs = refs[2 * n_seg:2 * n_seg + n_extra]
        n_in = 2 * n_seg + n_extra + (1 if into is not None else 0)
        o_ref = refs[n_in]
        if nk == 1:
            part = _dot(refs[0][...], refs[1][...], dims)
            if epilogue is not None:
                part = epilogue(part, *[e[...] for e in e_refs])
            o_ref[...] = part.astype(o_ref.dtype)
            return
        acc_ref = refs[n_in + 1]
        kk = pl.program_id(2)

        @pl.when(kk == 0)
        def _():
            acc_ref[...] = jnp.zeros_like(acc_ref)

        if n_seg == 1:
            acc_ref[...] += _dot(refs[0][...], refs[1][...], dims)
        else:
            for s in range(n_seg):
                @pl.when(jnp.logical_and(kk >= starts[s], kk < starts[s] + counts[s]))
                def _(s=s):
                    acc_ref[...] += _dot(refs[2 * s][...], refs[2 * s + 1][...], dims)

        @pl.when(kk == nk - 1)
        def _():
            acc = acc_ref[...]
            if epilogue is not None:
                acc = epilogue(acc, *[e[...] for e in e_refs])
            o_ref[...] = acc.astype(o_ref.dtype)

    if out_spec is None:
        out_spec = pl.BlockSpec((tm, tn), lambda i, j, kk: (i, j))
    in_specs += [pl.BlockSpec(bs, im) for (_, bs, im) in extras]
    operands += [e for (e, _, _) in extras]
    aliases = {}
    if into is not None:
        dest, ro, co = into
        out_shape = jax.ShapeDtypeStruct(dest.shape, dest.dtype)
        out_spec = pl.BlockSpec((tm, tn), lambda i, j, kk: (i + ro, j + co))
        aliases = {len(operands): 0}
        in_specs.append(ANY)
        operands.append(dest)
    return pl.pallas_call(
        body, name=name, grid=(m // tm, n // tn, nk),
        in_specs=in_specs, out_specs=out_spec, out_shape=out_shape, input_output_aliases=aliases,
        scratch_shapes=[] if nk == 1 else [pltpu.VMEM((tm, tn), F32)],
        compiler_params=pltpu.CompilerParams(dimension_semantics=("parallel", "parallel", "arbitrary")),
    )(*operands)


def _rms_fwd(name, h, w):
    rows, d = h.shape
    tm = _row_tile(rows, 512)

    def body(h_ref, w_ref, o_ref):
        x = h_ref[...]
        r = lax.rsqrt(jnp.mean(x * x, axis=-1, keepdims=True) + EPS)
        o_ref[...] = (x * r * w_ref[...]).astype(BF16)

    return pl.pallas_call(
        body, name=name, grid=(rows // tm,),
        in_specs=[pl.BlockSpec((tm, d), lambda i: (i, 0)), pl.BlockSpec((1, d), lambda i: (0, 0))],
        out_specs=pl.BlockSpec((tm, d), lambda i: (i, 0)),
        out_shape=jax.ShapeDtypeStruct((rows, d), BF16),
    )(h, w)


def _rms_bwd(name, dhn, h, w, dres):
    rows, d = h.shape
    tm = _row_tile(rows, 384)

    def body(g_ref, h_ref, w_ref, r_ref, dh_ref, dw_ref):
        i = pl.program_id(0)
        x = h_ref[...]
        r = lax.rsqrt(jnp.mean(x * x, axis=-1, keepdims=True) + EPS)
        xh = x * r
        g = g_ref[...]
        gw = g * w_ref[...]
        dh_ref[...] = r_ref[...] + r * (gw - xh * jnp.mean(gw * xh, axis=-1, keepdims=True))

        @pl.when(i == 0)
        def _():
            dw_ref[...] = jnp.zeros_like(dw_ref)

        dw_ref[...] += jnp.sum(g * xh, axis=0, keepdims=True)

    return pl.pallas_call(
        body, name=name, grid=(rows // tm,),
        in_specs=[pl.BlockSpec((tm, d), lambda i: (i, 0)), pl.BlockSpec((tm, d), lambda i: (i, 0)),
                  pl.BlockSpec((1, d), lambda i: (0, 0)), pl.BlockSpec((tm, d), lambda i: (i, 0))],
        out_specs=[pl.BlockSpec((tm, d), lambda i: (i, 0)), pl.BlockSpec((1, d), lambda i: (0, 0))],
        out_shape=[jax.ShapeDtypeStruct((rows, d), F32), jax.ShapeDtypeStruct((1, d), F32)],
    )(dhn, h, w, dres)


def _embed_norm(x, meta, w):
    seq, d = x.shape
    rows = seq + CHUNK

    def body(x_ref, m_ref, w_ref, h_ref, o_ref):
        i = pl.program_id(0)

        def emit(h):
            h_ref[...] = h
            r = lax.rsqrt(jnp.mean(h * h, axis=-1, keepdims=True) + EPS)
            o_ref[...] = (h * r * w_ref[...]).astype(BF16)

        @pl.when(i == 0)
        def _():
            emit(jnp.concatenate([jnp.zeros((PAD, d), F32), m_ref[...]], axis=0))

        @pl.when(i > 0)
        def _():
            emit(x_ref[...])

    blk = pl.BlockSpec((CHUNK, d), lambda i: (i, 0))
    return pl.pallas_call(
        body, name="embed_norm_ab", grid=(rows // CHUNK,),
        in_specs=[pl.BlockSpec((CHUNK, d), lambda i: (jnp.maximum(i - 1, 0), 0)),
                  pl.BlockSpec((N_META, d), lambda i: (0, 0)), pl.BlockSpec((1, d), lambda i: (0, 0))],
        out_specs=[blk, blk],
        out_shape=[jax.ShapeDtypeStruct((rows, d), F32), jax.ShapeDtypeStruct((rows, d), BF16)],
    )(x, meta, w)


def _rms_bwd_embed(dhn, h, w, dres):
    rows, d = h.shape
    seq = rows - CHUNK

    def body(g_ref, h_ref, w_ref, r_ref, gx_ref, gm_ref, dw_ref):
        i = pl.program_id(0)
        x = h_ref[...]
        r = lax.rsqrt(jnp.mean(x * x, axis=-1, keepdims=True) + EPS)
        xh = x * r
        g = g_ref[...]
        gw = g * w_ref[...]
        dh = r_ref[...] + r * (gw - xh * jnp.mean(gw * xh, axis=-1, keepdims=True))

        @pl.when(i == 0)
        def _():
            dw_ref[...] = jnp.zeros_like(dw_ref)
            gm_ref[...] = dh[PAD:]

        @pl.when(i > 0)
        def _():
            gx_ref[...] = dh

        dw_ref[...] += jnp.sum(g * xh, axis=0, keepdims=True)

    blk = pl.BlockSpec((CHUNK, d), lambda i: (i, 0))
    return pl.pallas_call(
        body, name="norm_ab_bwd", grid=(rows // CHUNK,),
        in_specs=[blk, blk, pl.BlockSpec((1, d), lambda i: (0, 0)), blk],
        out_specs=[pl.BlockSpec((CHUNK, d), lambda i: (jnp.maximum(i - 1, 0), 0)),
                   pl.BlockSpec((N_META, d), lambda i: (0, 0)), pl.BlockSpec((1, d), lambda i: (0, 0))],
        out_shape=[jax.ShapeDtypeStruct((seq, d), F32), jax.ShapeDtypeStruct((N_META, d), F32),
                   jax.ShapeDtypeStruct((1, d), F32)],
    )(dhn, h, w, dres)


def _final_loss(h2, w, target):
    rows, d = h2.shape

    def body(h_ref, w_ref, t_ref, loss_ref, dh_ref, dw_ref):
        i = pl.program_id(0)

        @pl.when(i == 0)
        def _():
            loss_ref[...] = jnp.zeros_like(loss_ref)
            dw_ref[...] = jnp.zeros_like(dw_ref)
            dh_ref[...] = jnp.zeros_like(dh_ref)

        @pl.when(i > 0)
        def _():
            x = h_ref[...]
            r = lax.rsqrt(jnp.mean(x * x, axis=-1, keepdims=True) + EPS)
            xh = x * r
            wv = w_ref[...]
            err = xh * wv - t_ref[...]
            loss_ref[...] += 0.5 * jnp.sum(jnp.mean(err * err, axis=-1, keepdims=True), axis=0, keepdims=True)
            g = err * (1.0 / d)
            gw = g * wv
            dh_ref[...] = r * (gw - xh * jnp.mean(gw * xh, axis=-1, keepdims=True))
            dw_ref[...] += jnp.sum(g * xh, axis=0, keepdims=True)

    return pl.pallas_call(
        body, name="final_loss", grid=(rows // CHUNK,),
        in_specs=[pl.BlockSpec((CHUNK, d), lambda i: (i, 0)), pl.BlockSpec((1, d), lambda i: (0, 0)),
                  pl.BlockSpec((CHUNK, d), lambda i: (jnp.maximum(i - 1, 0), 0))],
        out_specs=[pl.BlockSpec((1, 1), lambda i: (0, 0)), pl.BlockSpec((CHUNK, d), lambda i: (i, 0)),
                   pl.BlockSpec((1, d), lambda i: (0, 0))],
        out_shape=[jax.ShapeDtypeStruct((1, 1), F32), jax.ShapeDtypeStruct((rows, d), F32),
                   jax.ShapeDtypeStruct((1, d), F32)],
    )(h2, w, target)


def _gate_fwd(o, z, w):
    rs = lax.rsqrt(jnp.mean(o * o, axis=-1, keepdims=True) + EPS)
    return o * rs * w * (z * _sigmoid(z))


def _gate_bwd(dout, o, z, w):
    rs = lax.rsqrt(jnp.mean(o * o, axis=-1, keepdims=True) + EPS)
    yn = o * rs
    sg = _sigmoid(z)
    sil = z * sg
    dsil = sg * (1.0 + z * (1.0 - sg))
    dz = dout * yn * w * dsil
    dyn = dout * w * sil
    dw = jnp.sum(dout * yn * sil, axis=0, keepdims=True)
    do = rs * (dyn - yn * jnp.mean(dyn * yn, axis=-1, keepdims=True))
    return do, dz, dw


def _rope(t, cosf, sinf):
    return t * cosf + pltpu.roll(t, RET_DK // 2, 1) * sinf


def _rope_t(d, cosf, sinf):
    return d * cosf + pltpu.roll(d * sinf, RET_DK // 2, 1)


def _ret_tables():
    log_g = jnp.log1p(-jnp.exp2(-5.0 - jnp.arange(RET_HEADS, dtype=F32)))
    idx = jnp.arange(CHUNK, dtype=F32)
    diff = idx[:, None] - idx[None, :]
    decay = jnp.where(diff >= 0, jnp.exp(log_g[:, None, None] * jnp.maximum(diff, 0.0)), 0.0)
    kw = jnp.exp(log_g[:, None] * (CHUNK - 1 - idx))
    qw = jnp.exp(log_g[:, None] * (idx + 1.0))
    gch = jnp.exp(log_g * CHUNK)
    kw = jnp.broadcast_to(kw[:, :, None], (RET_HEADS, CHUNK, RET_DK))
    qw = jnp.broadcast_to(qw[:, :, None], (RET_HEADS, CHUNK, RET_DK))
    gch = jnp.broadcast_to(gch[:, None, None], (RET_HEADS, 1, RET_DV))
    return decay, kw, qw, gch


def _rope_tables(rows):
    pos = jnp.arange(rows, dtype=F32) - float(PAD)
    inv_freq = jnp.power(ROPE_BASE, -jnp.arange(0, RET_DK, 2, dtype=F32) / RET_DK)
    ang = pos[:, None] * inv_freq[None, :]
    cos, sin = jnp.cos(ang), jnp.sin(ang)
    return jnp.concatenate([cos, cos], axis=1), jnp.concatenate([-sin, sin], axis=1)


RET_HB = 4
RET_QB = RET_HB * RET_DK
RET_VB = RET_HB * RET_DV


def _ret_in_specs(rev, nc):
    def cn(n):
        return (nc - 1 - n) if rev else n
    kb = RET_QK // RET_QB
    vb = 2 * RET_QK // RET_VB
    zb = (2 * RET_QK + RET_W) // RET_VB
    return [
        pl.BlockSpec((CHUNK, RET_QB), lambda h, n: (cn(n), h)),
        pl.BlockSpec((CHUNK, RET_QB), lambda h, n: (cn(n), kb + h)),
        pl.BlockSpec((CHUNK, RET_VB), lambda h, n: (cn(n), vb + h)),
        pl.BlockSpec((CHUNK, RET_VB), lambda h, n: (cn(n), zb + h)),
        pl.BlockSpec((CHUNK, RET_DK), lambda h, n: (cn(n), 0)),
        pl.BlockSpec((CHUNK, RET_DK), lambda h, n: (cn(n), 0)),
        pl.BlockSpec((RET_HB, CHUNK, CHUNK), lambda h, n: (h, 0, 0)),
        pl.BlockSpec((RET_HB, CHUNK, RET_DK), lambda h, n: (h, 0, 0)),
        pl.BlockSpec((RET_HB, CHUNK, RET_DK), lambda h, n: (h, 0, 0)),
        pl.BlockSpec((RET_HB, 1, RET_DV), lambda h, n: (h, 0, 0)),
        pl.BlockSpec((1, RET_VB), lambda h, n: (0, h)),
    ]


def _ret_fwd(proj, cosf, sinf, tables, normw):
    rows = proj.shape[0]
    nc = rows // CHUNK
    decay, kw, qw, gch = tables

    def body(q_ref, k_ref, v_ref, z_ref, cos_ref, sin_ref, dm_ref, kw_ref, qw_ref, g_ref, w_ref,
             o_ref, oa_ref, st_ref, s_scr):
        n = pl.program_id(1)

        @pl.when(n == 0)
        def _():
            s_scr[...] = jnp.zeros_like(s_scr)

        cosv, sinv = cos_ref[...], sin_ref[...]
        for hh in range(RET_HB):
            qc = slice(hh * RET_DK, (hh + 1) * RET_DK)
            vc = slice(hh * RET_DV, (hh + 1) * RET_DV)
            q = _rope(q_ref[:, qc], cosv, sinv)
            k = _rope(k_ref[:, qc], cosv, sinv) * (RET_DK ** -0.5)
            v = v_ref[:, vc]
            s = s_scr[hh]
            st_ref[hh, 0] = s.astype(BF16)
            a = _dot(q, k, NT) * dm_ref[hh]
            o = _dot(a, v) + _dot(q * qw_ref[hh], s)
            s_scr[hh] = s * g_ref[hh] + _dot(k * kw_ref[hh], v, TN)
            o_ref[:, vc] = o
            oa_ref[:, vc] = _gate_fwd(o, z_ref[:, vc], w_ref[:, vc]).astype(BF16)

    return pl.pallas_call(
        body, name="ret_fwd", grid=(RET_HEADS // RET_HB, nc),
        in_specs=_ret_in_specs(False, nc),
        out_specs=[pl.BlockSpec((CHUNK, RET_VB), lambda h, n: (n, h)),
                   pl.BlockSpec((CHUNK, RET_VB), lambda h, n: (n, h)),
                   pl.BlockSpec((RET_HB, 1, RET_DK, RET_DV), lambda h, n: (h, n, 0, 0))],
        out_shape=[jax.ShapeDtypeStruct((rows, RET_W), F32), jax.ShapeDtypeStruct((rows, RET_W), BF16),
                   jax.ShapeDtypeStruct((RET_HEADS, nc, RET_DK, RET_DV), BF16)],
        scratch_shapes=[pltpu.VMEM((RET_HB, RET_DK, RET_DV), F32)],
        compiler_params=pltpu.CompilerParams(dimension_semantics=("parallel", "arbitrary")),
    )(proj, proj, proj, proj, cosf, sinf, decay, kw, qw, gch, normw)


def _ret_bwd(proj, cosf, sinf, tables, normw, o_ret, dmix, states):
    rows = proj.shape[0]
    nc = rows // CHUNK
    decay, kw, qw, gch = tables

    def rn(n):
        return nc - 1 - n

    def body(q_ref, k_ref, v_ref, z_ref, cos_ref, sin_ref, dm_ref, kw_ref, qw_ref, g_ref, w_ref,
             o_ref, do_ref, st_ref, dq_ref, dk_ref, dv_ref, dz_ref, dw_ref, ds_scr):
        n = pl.program_id(1)

        @pl.when(n == 0)
        def _():
            ds_scr[...] = jnp.zeros_like(ds_scr)
            dw_ref[...] = jnp.zeros_like(dw_ref)

        cosv, sinv = cos_ref[...], sin_ref[...]
        for hh in range(RET_HB):
            qc = slice(hh * RET_DK, (hh + 1) * RET_DK)
            vc = slice(hh * RET_DV, (hh + 1) * RET_DV)
            q = _rope(q_ref[:, qc], cosv, sinv)
            k = _rope(k_ref[:, qc], cosv, sinv) * (RET_DK ** -0.5)
            v = v_ref[:, vc]
            do, dz, dw = _gate_bwd(do_ref[:, vc], o_ref[:, vc], z_ref[:, vc], w_ref[:, vc])
            dz_ref[:, vc] = dz.astype(BF16)
            dw_ref[hh] += dw
            dm = dm_ref[hh]
            s = st_ref[hh, 0]
            g1 = ds_scr[hh]
            p = _dot(q, k, NT) * dm
            kwv = k * kw_ref[hh]
            qwv = q * qw_ref[hh]
            dp = _dot(do, v, NT)
            da = dp * dm
            dv = _dot(p, do, TN) + _dot(kwv, g1)
            dq = _dot(da, k) + _dot(do, s, NT) * qw_ref[hh]
            dk = _dot(da, q, TN) + _dot(v, g1, NT) * kw_ref[hh]
            ds_scr[hh] = g1 * g_ref[hh] + _dot(qwv, do, TN)
            dv_ref[:, vc] = dv.astype(BF16)
            dq_ref[:, qc] = _rope_t(dq, cosv, sinv).astype(BF16)
            dk_ref[:, qc] = _rope_t(dk * (RET_DK ** -0.5), cosv, sinv).astype(BF16)

    in_specs = _ret_in_specs(True, nc) + [
        pl.BlockSpec((CHUNK, RET_VB), lambda h, n: (rn(n), h)),
        pl.BlockSpec((CHUNK, RET_VB), lambda h, n: (rn(n), h)),
        pl.BlockSpec((RET_HB, 1, RET_DK, RET_DV), lambda h, n: (h, rn(n), 0, 0)),
    ]
    return pl.pallas_call(
        body, name="ret_bwd", grid=(RET_HEADS // RET_HB, nc),
        in_specs=in_specs,
        out_specs=[pl.BlockSpec((CHUNK, RET_QB), lambda h, n: (rn(n), h)),
                   pl.BlockSpec((CHUNK, RET_QB), lambda h, n: (rn(n), h)),
                   pl.BlockSpec((CHUNK, RET_VB), lambda h, n: (rn(n), h)),
                   pl.BlockSpec((CHUNK, RET_VB), lambda h, n: (rn(n), h)),
                   pl.BlockSpec((RET_HB, 1, RET_DV), lambda h, n: (h, 0, 0))],
        out_shape=[jax.ShapeDtypeStruct((rows, RET_QK), BF16), jax.ShapeDtypeStruct((rows, RET_QK), BF16),
                   jax.ShapeDtypeStruct((rows, RET_W), BF16), jax.ShapeDtypeStruct((rows, RET_W), BF16),
                   jax.ShapeDtypeStruct((RET_HEADS, 1, RET_DV), F32)],
        scratch_shapes=[pltpu.VMEM((RET_HB, RET_DK, RET_DV), F32)],
        compiler_params=pltpu.CompilerParams(dimension_semantics=("parallel", "arbitrary")),
    )(proj, proj, proj, proj, cosf, sinf, decay, kw, qw, gch, normw, o_ret, dmix, states)


def _s5_discretize(lam_re, lam_im, log_dt, b_re, b_im):
    dt = jnp.exp(log_dt)[:, None]
    mag = jnp.exp(lam_re * dt)
    ab_re, ab_im = mag * jnp.cos(lam_im * dt), mag * jnp.sin(lam_im * dt)
    den = lam_re * lam_re + lam_im * lam_im
    nr, ni = ab_re - 1.0, ab_im
    f_re = (nr * lam_re + ni * lam_im) / den
    f_im = (ni * lam_re - nr * lam_im) / den
    bb_re = f_re[..., None] * b_re - f_im[..., None] * b_im
    bb_im = f_re[..., None] * b_im + f_im[..., None] * b_re
    return ab_re, ab_im, bb_re, bb_im


def _bdiag_in(bb):
    t = bb.reshape(S5_NT, S5_TG, S5_P, S5_GH).transpose(0, 1, 3, 2)
    eye = jnp.eye(S5_TG, dtype=bb.dtype)
    full = t[:, :, :, None, :] * eye[None, :, None, :, None]
    return full.reshape(S5_NT, S5_TU, S5_TS)


def _bdiag_in_extract(dense):
    t = dense.reshape(S5_NT, S5_TG, S5_GH, S5_TG, S5_P)
    diag = jnp.stack([t[:, g, :, g, :] for g in range(S5_TG)], axis=1)
    return diag.transpose(0, 1, 3, 2).reshape(S5_G, S5_P, S5_GH)


def _bdiag_out(c):
    t = c.reshape(S5_NT, S5_TG, S5_GH, S5_P).transpose(0, 1, 3, 2)
    eye = jnp.eye(S5_TG, dtype=c.dtype)
    full = t[:, :, :, None, :] * eye[None, :, None, :, None]
    return full.reshape(S5_NT, S5_TS, S5_TU)


def _bdiag_out_extract(dense):
    t = dense.reshape(S5_NT, S5_TG, S5_P, S5_TG, S5_GH)
    diag = jnp.stack([t[:, g, :, g, :] for g in range(S5_TG)], axis=1)
    return diag.transpose(0, 1, 3, 2).reshape(S5_G, S5_GH, S5_P)


def _cmul(ar, ai, br, bi):
    return ar * br - ai * bi, ar * bi + ai * br


S5_SEG = 8
S5_STEPS = CHUNK // S5_SEG


def _seg_perm(x):
    c = x.shape[1]
    return jnp.swapaxes(x.reshape(S5_SEG, S5_STEPS, c), 0, 1).reshape(CHUNK, c)


def _seg_unperm(x):
    c = x.shape[1]
    return jnp.swapaxes(x.reshape(S5_STEPS, S5_SEG, c), 0, 1).reshape(CHUNK, c)


def _rows(x, p):
    return x[p * S5_SEG:(p + 1) * S5_SEG]


def _s5_tables(ar, ai, tr_scr, ti_scr, wfr_scr, wfi_scr, wbr_scr, wbi_scr):
    row = lax.broadcasted_iota(jnp.int32, (S5_SEG, 1), 0)
    a8r = jnp.broadcast_to(ar, (S5_SEG, S5_TS))
    a8i = jnp.broadcast_to(ai, (S5_SEG, S5_TS))
    pr, pi = a8r, a8i
    for p in range(S5_STEPS):
        tr_scr[p * S5_SEG:(p + 1) * S5_SEG, :] = pr
        ti_scr[p * S5_SEG:(p + 1) * S5_SEG, :] = pi
        if p < S5_STEPS - 1:
            pr, pi = _cmul(pr, pi, a8r, a8i)
    wr, wi = pr, pi
    sh = 1
    while sh < S5_SEG:
        keep = row >= sh
        sr = jnp.where(keep, pltpu.roll(wr, sh, 0), 1.0)
        si = jnp.where(keep, pltpu.roll(wi, sh, 0), 0.0)
        wr, wi = _cmul(wr, wi, sr, si)
        sh *= 2
    wfr_scr[...] = wr
    wfi_scr[...] = wi
    wr, wi = pr, -pi
    sh = 1
    while sh < S5_SEG:
        keep = row < S5_SEG - sh
        sr = jnp.where(keep, pltpu.roll(wr, S5_SEG - sh, 0), 1.0)
        si = jnp.where(keep, pltpu.roll(wi, S5_SEG - sh, 0), 0.0)
        wr, wi = _cmul(wr, wi, sr, si)
        sh *= 2
    wbr_scr[...] = wr
    wbi_scr[...] = wi


def _seg_scan(vr, vi, ar, ai, tr_scr, ti_scr, wr_scr, wi_scr, c0r, c0i, down):
    row = lax.broadcasted_iota(jnp.int32, (S5_SEG, 1), 0)
    sgn = 1.0 if down else -1.0
    order = list(range(S5_STEPS)) if down else list(range(S5_STEPS - 1, -1, -1))
    xr, xi = _rows(vr, order[0]), _rows(vi, order[0])
    loc = {order[0]: (xr, xi)}
    for p in order[1:]:
        mr, mi = _cmul(ar, sgn * ai, xr, xi)
        xr, xi = mr + _rows(vr, p), mi + _rows(vi, p)
        loc[p] = (xr, xi)
    last = S5_STEPS - 1
    mr, mi = tr_scr[last * S5_SEG:(last + 1) * S5_SEG, :], sgn * ti_scr[last * S5_SEG:(last + 1) * S5_SEG, :]
    er, ei = xr, xi
    sh = 1
    while sh < S5_SEG:
        if down:
            keep = row >= sh
            sr, si = pltpu.roll(er, sh, 0), pltpu.roll(ei, sh, 0)
        else:
            keep = row < S5_SEG - sh
            sr, si = pltpu.roll(er, S5_SEG - sh, 0), pltpu.roll(ei, S5_SEG - sh, 0)
        pr, pi = _cmul(mr, mi, jnp.where(keep, sr, 0.0), jnp.where(keep, si, 0.0))
        er, ei = er + pr, ei + pi
        mr, mi = _cmul(mr, mi, mr, mi)
        sh *= 2
    pr, pi = _cmul(wr_scr[...], wi_scr[...], c0r, c0i)
    er, ei = er + pr, ei + pi
    if down:
        nr = jnp.where(row == 0, c0r, pltpu.roll(er, 1, 0))
        ni = jnp.where(row == 0, c0i, pltpu.roll(ei, 1, 0))
    else:
        nr = jnp.where(row == S5_SEG - 1, c0r, pltpu.roll(er, S5_SEG - 1, 0))
        ni = jnp.where(row == S5_SEG - 1, c0i, pltpu.roll(ei, S5_SEG - 1, 0))
    out_r, out_i = [], []
    for p in range(S5_STEPS):
        q = p if down else S5_STEPS - 1 - p
        pr, pi = _cmul(tr_scr[q * S5_SEG:(q + 1) * S5_SEG, :], sgn * ti_scr[q * S5_SEG:(q + 1) * S5_SEG, :], nr, ni)
        out_r.append(loc[p][0] + pr)
        out_i.append(loc[p][1] + pi)
    return jnp.concatenate(out_r, axis=0), jnp.concatenate(out_i, axis=0), (nr, ni), (er, ei)


def _gelu(y):
    c = math.sqrt(2.0 / math.pi)
    return 0.5 * y * (1.0 + jnp.tanh(c * (y + 0.044715 * y * y * y)))


def _gelu_grad(y):
    c = math.sqrt(2.0 / math.pi)
    th = jnp.tanh(c * (y + 0.044715 * y * y * y))
    return 0.5 * (1.0 + th) + 0.5 * y * (1.0 - th * th) * c * (1.0 + 3.0 * 0.044715 * y * y)


def _s5_fwd(proj, ab, bd_b, bd_c, dvec):
    rows = proj.shape[0]
    nc = rows // CHUNK
    tps = S5_FWD_TILES
    ubw = tps * S5_TU
    ub = (2 * RET_QK + 2 * RET_W) // ubw
    ab_re, ab_im = ab
    bre, bim = bd_b
    cre, cim = bd_c

    def body(u_ref, ar_ref, ai_ref, bre_ref, bim_ref, cre_ref, cim_ref, d_ref,
             y_ref, g_ref, er_ref, ei_ref, tr_scr, ti_scr, wfr_scr, wfi_scr, wbr_scr, wbi_scr,
             cr_scr, ci_scr, er_scr, ei_scr):
        n = pl.program_id(1)
        for tt in range(tps):
            cols = slice(tt * S5_TU, (tt + 1) * S5_TU)
            ar, ai = ar_ref[tt], ai_ref[tt]
            trs, tis, wfr, wfi = tr_scr.at[tt], ti_scr.at[tt], wfr_scr.at[tt], wfi_scr.at[tt]

            @pl.when(n == 0)
            def _(tt=tt, ar=ar, ai=ai, trs=trs, tis=tis, wfr=wfr, wfi=wfi):
                _s5_tables(ar, ai, trs, tis, wfr, wfi, wbr_scr.at[tt], wbi_scr.at[tt])
                cr_scr[tt] = jnp.zeros((S5_SEG, S5_TS), F32)
                ci_scr[tt] = jnp.zeros((S5_SEG, S5_TS), F32)

            u = _seg_perm(u_ref[:, cols])
            c0r, c0i = cr_scr[tt], ci_scr[tt]
            er_ref[tt, 0] = c0r
            ei_ref[tt, 0] = c0i
            xr, xi, _, (er, ei) = _seg_scan(_dot(u, bre_ref[tt]), _dot(u, bim_ref[tt]), ar, ai, trs, tis,
                                            wfr, wfi, c0r, c0i, True)
            er_scr[tt] = er
            ei_scr[tt] = ei
            cr_scr[tt] = jnp.broadcast_to(er_scr[tt, S5_SEG - 1:S5_SEG, :], (S5_SEG, S5_TS))
            ci_scr[tt] = jnp.broadcast_to(ei_scr[tt, S5_SEG - 1:S5_SEG, :], (S5_SEG, S5_TS))
            y = _seg_unperm(_dot(xr, cre_ref[tt]) - _dot(xi, cim_ref[tt]) + d_ref[:, cols] * u)
            y_ref[:, cols] = y
            g_ref[:, cols] = _gelu(y).astype(BF16)

    vec = pl.BlockSpec((tps, 1, S5_TS), lambda t, n: (t, 0, 0))
    return pl.pallas_call(
        body, name="s5_fwd", grid=(S5_NT // tps, nc),
        in_specs=[pl.BlockSpec((CHUNK, ubw), lambda t, n: (n, ub + t)), vec, vec,
                  pl.BlockSpec((tps, S5_TU, S5_TS), lambda t, n: (t, 0, 0)),
                  pl.BlockSpec((tps, S5_TU, S5_TS), lambda t, n: (t, 0, 0)),
                  pl.BlockSpec((tps, S5_TS, S5_TU), lambda t, n: (t, 0, 0)),
                  pl.BlockSpec((tps, S5_TS, S5_TU), lambda t, n: (t, 0, 0)),
                  pl.BlockSpec((1, ubw), lambda t, n: (0, t))],
        out_specs=[pl.BlockSpec((CHUNK, ubw), lambda t, n: (n, t)),
                   pl.BlockSpec((CHUNK, ubw), lambda t, n: (n, t)),
                   pl.BlockSpec((tps, 1, 8, S5_TS), lambda t, n: (t, n, 0, 0)),
                   pl.BlockSpec((tps, 1, 8, S5_TS), lambda t, n: (t, n, 0, 0))],
        out_shape=[jax.ShapeDtypeStruct((rows, S5_W), F32), jax.ShapeDtypeStruct((rows, S5_W), BF16),
                   jax.ShapeDtypeStruct((S5_NT, nc, 8, S5_TS), F32),
                   jax.ShapeDtypeStruct((S5_NT, nc, 8, S5_TS), F32)],
        scratch_shapes=[pltpu.VMEM((tps, CHUNK, S5_TS), F32) for _ in range(2)]
        + [pltpu.VMEM((tps, S5_SEG, S5_TS), F32) for _ in range(8)],
        compiler_params=pltpu.CompilerParams(dimension_semantics=("parallel", "arbitrary")),
    )(proj, ab_re.reshape(S5_NT, 1, S5_TS), ab_im.reshape(S5_NT, 1, S5_TS), bre, bim, cre, cim, dvec)


def _s5_bwd(proj, dy, ab, bd_b, bd_c, dvec, entry):
    rows = proj.shape[0]
    nc = rows // CHUNK
    tps = S5_BWD_TILES
    ubw = tps * S5_TU
    ub = (2 * RET_QK + 2 * RET_W) // ubw
    ab_re, ab_im = ab
    bre, bim = bd_b
    cre, cim = bd_c
    er, ei = entry

    def rn(n):
        return nc - 1 - n

    def body(u_ref, dy_ref, ar_ref, ai_ref, bre_ref, bim_ref, cre_ref, cim_ref, d_ref, er_ref, ei_ref,
             du_ref, dbr_ref, dbi_ref, dcr_ref, dci_ref, dar_ref, dai_ref, dd_ref,
             tr_scr, ti_scr, wfr_scr, wfi_scr, wbr_scr, wbi_scr, gr_scr, gi_scr, er_scr, ei_scr):
        n = pl.program_id(1)

        @pl.when(n == 0)
        def _():
            gr_scr[...] = jnp.zeros_like(gr_scr)
            gi_scr[...] = jnp.zeros_like(gi_scr)
            for r in (dbr_ref, dbi_ref, dcr_ref, dci_ref, dar_ref, dai_ref, dd_ref):
                r[...] = jnp.zeros_like(r)

        for tt in range(tps):
            cols = slice(tt * S5_TU, (tt + 1) * S5_TU)
            ar, ai = ar_ref[tt], ai_ref[tt]
            trs, tis = tr_scr.at[tt], ti_scr.at[tt]

            @pl.when(n == 0)
            def _(tt=tt, ar=ar, ai=ai, trs=trs, tis=tis):
                _s5_tables(ar, ai, trs, tis, wfr_scr.at[tt], wfi_scr.at[tt], wbr_scr.at[tt], wbi_scr.at[tt])

            u = _seg_perm(u_ref[:, cols])
            dy = _seg_perm(dy_ref[:, cols])
            xr, xi, (pr, pi), _ = _seg_scan(_dot(u, bre_ref[tt]), _dot(u, bim_ref[tt]), ar, ai, trs, tis,
                                            wfr_scr.at[tt], wfi_scr.at[tt], er_ref[tt, 0], ei_ref[tt, 0], True)
            dcr_ref[tt] += _dot(xr, dy, TN)
            dci_ref[tt] -= _dot(xi, dy, TN)
            gr, gi, _, (er, ei) = _seg_scan(_dot(dy, cre_ref[tt], NT), -_dot(dy, cim_ref[tt], NT), ar, ai, trs, tis,
                                            wbr_scr.at[tt], wbi_scr.at[tt], gr_scr[tt], gi_scr[tt], False)
            er_scr[tt] = er
            ei_scr[tt] = ei
            gr_scr[tt] = jnp.broadcast_to(er_scr[tt, 0:1, :], (S5_SEG, S5_TS))
            gi_scr[tt] = jnp.broadcast_to(ei_scr[tt, 0:1, :], (S5_SEG, S5_TS))
            xpr = jnp.concatenate([pr, xr[:CHUNK - S5_SEG]], axis=0)
            xpi = jnp.concatenate([pi, xi[:CHUNK - S5_SEG]], axis=0)
            dar_ref[tt] += jnp.sum((xpr * gr + xpi * gi).reshape(S5_STEPS, S5_SEG, S5_TS), axis=0)
            dai_ref[tt] += jnp.sum((xpr * gi - xpi * gr).reshape(S5_STEPS, S5_SEG, S5_TS), axis=0)
            dbr_ref[tt] += _dot(u, gr, TN)
            dbi_ref[tt] += _dot(u, gi, TN)
            dd_ref[tt] += jnp.sum((dy * u).reshape(S5_STEPS, S5_SEG, S5_TU), axis=0)
            du = dy * d_ref[:, cols] + _dot(gr, bre_ref[tt], NT) + _dot(gi, bim_ref[tt], NT)
            du_ref[:, cols] = _seg_unperm(du).astype(BF16)

    vec = pl.BlockSpec((tps, 1, S5_TS), lambda t, n: (t, 0, 0))
    acc_b = pl.BlockSpec((tps, S5_TU, S5_TS), lambda t, n: (t, 0, 0))
    acc_c = pl.BlockSpec((tps, S5_TS, S5_TU), lambda t, n: (t, 0, 0))
    acc_a = pl.BlockSpec((tps, 8, S5_TS), lambda t, n: (t, 0, 0))
    ent = pl.BlockSpec((tps, 1, 8, S5_TS), lambda t, n: (t, rn(n), 0, 0))
    return pl.pallas_call(
        body, name="s5_bwd", grid=(S5_NT // tps, nc),
        in_specs=[pl.BlockSpec((CHUNK, ubw), lambda t, n: (rn(n), ub + t)),
                  pl.BlockSpec((CHUNK, ubw), lambda t, n: (rn(n), t)), vec, vec,
                  acc_b, acc_b, acc_c, acc_c, pl.BlockSpec((1, ubw), lambda t, n: (0, t)), ent, ent],
        out_specs=[pl.BlockSpec((CHUNK, ubw), lambda t, n: (rn(n), t)), acc_b, acc_b, acc_c, acc_c, acc_a, acc_a,
                   pl.BlockSpec((tps, 8, S5_TU), lambda t, n: (t, 0, 0))],
        out_shape=[jax.ShapeDtypeStruct((rows, S5_W), BF16),
                   jax.ShapeDtypeStruct((S5_NT, S5_TU, S5_TS), F32), jax.ShapeDtypeStruct((S5_NT, S5_TU, S5_TS), F32),
                   jax.ShapeDtypeStruct((S5_NT, S5_TS, S5_TU), F32), jax.ShapeDtypeStruct((S5_NT, S5_TS, S5_TU), F32),
                   jax.ShapeDtypeStruct((S5_NT, 8, S5_TS), F32), jax.ShapeDtypeStruct((S5_NT, 8, S5_TS), F32),
                   jax.ShapeDtypeStruct((S5_NT, 8, S5_TU), F32)],
        scratch_shapes=[pltpu.VMEM((tps, CHUNK, S5_TS), F32) for _ in range(2)]
        + [pltpu.VMEM((tps, S5_SEG, S5_TS), F32) for _ in range(8)],
        compiler_params=pltpu.CompilerParams(dimension_semantics=("parallel", "arbitrary")),
    )(proj, dy,ab_re.reshape(S5_NT, 1, S5_TS), ab_im.reshape(S5_NT, 1, S5_TS), bre, bim, cre, cim, dvec, er, ei)


def _s5_gate_bwd(dmix, g, t, proj):
    rows = g.shape[0]
    tm = _row_tile(rows, 384)
    ob = RET_W // S5_W
    zb = (2 * RET_QK + 2 * RET_W + S5_W) // S5_W

    def body(do_ref, g_ref, t_ref, z_ref, dz_ref, dt_ref, dg_ref):
        do = do_ref[...]
        gv = g_ref[...].astype(F32)
        z = z_ref[...]
        st = _sigmoid(t_ref[...])
        sg = _sigmoid(z)
        os5 = gv * st
        dz_ref[...] = (do * os5 * sg * (1.0 + z * (1.0 - sg))).astype(BF16)
        dos = do * z * sg
        dt_ref[...] = (dos * gv * st * (1.0 - st)).astype(BF16)
        dg_ref[...] = dos * st

    blk = pl.BlockSpec((tm, S5_W), lambda i: (i, 0))
    return pl.pallas_call(
        body, name="s5_gate_bwd", grid=(rows // tm,),
        in_specs=[pl.BlockSpec((tm, S5_W), lambda i: (i, ob)), blk, blk,
                  pl.BlockSpec((tm, S5_W), lambda i: (i, zb))],
        out_specs=[blk, blk, blk],
        out_shape=[jax.ShapeDtypeStruct((rows, S5_W), BF16), jax.ShapeDtypeStruct((rows, S5_W), BF16),
                   jax.ShapeDtypeStruct((rows, S5_W), F32)],
    )(dmix, g, t, proj)


def _split3(x):
    hi = x.astype(BF16)
    r = x - hi.astype(F32)
    mid = r.astype(BF16)
    lo = (r - mid.astype(F32)).astype(BF16)
    return hi, mid, lo


def _tri_sum(x, upper):
    i = lax.broadcasted_iota(jnp.int32, (CHUNK, CHUNK), 0)
    j = lax.broadcasted_iota(jnp.int32, (CHUNK, CHUNK), 1)
    tri = jnp.where((j >= i) if upper else (j <= i), 1.0, 0.0).astype(BF16)
    hi, mid, lo = _split3(x)
    return _dot(tri, lo) + _dot(tri, mid) + _dot(tri, hi)


def _gla_log_decay(gl, wg, bg, n):
    logit = _dot(gl, wg) + bg
    la = (jnp.minimum(logit, 0.0) - jnp.log(1.0 + jnp.exp(-jnp.abs(logit)))) * (1.0 / GLA_TAU)
    row = lax.broadcasted_iota(jnp.int32, (CHUNK, 1), 0)
    live = jnp.logical_or(n > 0, row >= PAD)
    return logit, jnp.where(live, la, 0.0), live


def _gla_in_specs(rev, nc):
    def cn(n):
        return (nc - 1 - n) if rev else n
    kb = GLA_QK // GLA_DK
    vb = 2 * GLA_QK // GLA_DV
    zb = (2 * GLA_QK + GLA_W) // GLA_DV
    gb = (2 * GLA_QK + 2 * GLA_W) // 128
    return [
        pl.BlockSpec((CHUNK, GLA_DK), lambda h, n: (cn(n), h)),
        pl.BlockSpec((CHUNK, GLA_DK), lambda h, n: (cn(n), kb + h)),
        pl.BlockSpec((CHUNK, GLA_DV), lambda h, n: (cn(n), vb + h)),
        pl.BlockSpec((CHUNK, GLA_DV), lambda h, n: (cn(n), zb + h)),
        pl.BlockSpec((CHUNK, 128), lambda h, n: (cn(n), gb)),
        pl.BlockSpec((128, GLA_DK), lambda h, n: (0, h)),
        pl.BlockSpec((1, GLA_DK), lambda h, n: (0, h)),
        pl.BlockSpec((1, GLA_DV), lambda h, n: (0, h)),
    ]


def _gla_fwd(proj, wgate, bgate, normw):
    rows = proj.shape[0]
    nc = rows // CHUNK

    def body(q_ref, k_ref, v_ref, z_ref, gl_ref, wg_ref, bg_ref, w_ref, o_ref, oc_ref, st_ref, s_scr, b_scr):
        n = pl.program_id(1)

        @pl.when(n == 0)
        def _():
            s_scr[...] = jnp.zeros_like(s_scr)

        q = q_ref[...] * (GLA_DK ** -0.5)
        k = k_ref[...]
        v = v_ref[...]
        vb = v.astype(BF16)
        _, la, _ = _gla_log_decay(gl_ref[...], wg_ref[...], bg_ref[...], n)
        b = _tri_sum(la, False)
        b_scr[...] = b
        b_last = b_scr[CHUNK - 1:CHUNK, :]
        st = s_scr[...]
        st_ref[0, 0] = st
        s_scr[...] = st * jnp.exp(b_last) + _dot(v, k * jnp.exp(b_last - b), TN)
        rowc = lax.broadcasted_iota(jnp.int32, (CHUNK, 1), 0)
        rows16 = lax.broadcasted_iota(jnp.int32, (SUB, 1), 0)
        a_tot = jnp.zeros((CHUNK, CHUNK), F32)
        for s in range(1, NSUB):
            lo = s * SUB
            bref = b_scr[lo - 1:lo, :]
            in_s = jnp.logical_and(rowc >= lo, rowc < lo + SUB)
            qh = q * jnp.exp(jnp.where(in_s, b - bref, -1e30))
            kh = k * jnp.exp(jnp.where(rowc < lo, bref - b, -1e30))
            a_tot = a_tot + _dot(qh, kh, NT)
        lane = lax.broadcasted_iota(jnp.int32, (SUB, CHUNK), 1)
        diag = []
        for s in range(NSUB):
            lo = s * SUB
            qs, bs = q[lo:lo + SUB], b[lo:lo + SUB]
            s_blk = jnp.zeros((SUB, CHUNK), F32)
            for j in range(SUB):
                r = lo + j
                e = jnp.exp(jnp.where(rows16 >= j, bs - b_scr[r:r + 1, :], -1e30))
                col = jnp.sum(qs * k_ref[r:r + 1, :] * e, axis=1, keepdims=True)
                s_blk = jnp.where(lane == r, col, s_blk)
            diag.append(s_blk)
        o = _dot(q * jnp.exp(b), st, NT) + _dot(a_tot + jnp.concatenate(diag, axis=0), vb)
        o_ref[...] = o
        oc_ref[...] = _gate_fwd(o, z_ref[...], w_ref[...]).astype(BF16)

    return pl.pallas_call(
        body, name="gla_fwd", grid=(GLA_HEADS, nc),
        in_specs=_gla_in_specs(False, nc),
        out_specs=[pl.BlockSpec((CHUNK, GLA_DV), lambda h, n: (n, h)),
                   pl.BlockSpec((CHUNK, GLA_DV), lambda h, n: (n, h)),
                   pl.BlockSpec((1, 1, GLA_DV, GLA_DK), lambda h, n: (h, n, 0, 0))],
        out_shape=[jax.ShapeDtypeStruct((rows, GLA_W), F32), jax.ShapeDtypeStruct((rows, GLA_W), BF16),
                   jax.ShapeDtypeStruct((GLA_HEADS, nc, GLA_DV, GLA_DK), F32)],
        scratch_shapes=[pltpu.VMEM((GLA_DV, GLA_DK), F32), pltpu.VMEM((CHUNK, GLA_DK), F32)],
        compiler_params=pltpu.CompilerParams(dimension_semantics=("parallel", "arbitrary")),
    )(proj, proj, proj, proj, proj, wgate, bgate, normw)


def _gla_bwd(proj, wgate, bgate, normw, o_gla, d_oc, states):
    rows = proj.shape[0]
    nc = rows // CHUNK

    def rn(n):
        return nc - 1 - n

    def body(q_ref, k_ref, v_ref, z_ref, gl_ref, wg_ref, bg_ref, w_ref, o_ref, do_ref, st_ref,
             dq_ref, dk_ref, dv_ref, dz_ref, dl_ref, dw_ref, dbg_ref,
             ds_scr, dq_scr, dk_scr, dv_scr, db_scr, b_scr, q_scr):
        n = pl.program_id(1)
        cn = rn(n)

        @pl.when(n == 0)
        def _():
            ds_scr[...] = jnp.zeros_like(ds_scr)
            dw_ref[...] = jnp.zeros_like(dw_ref)
            dbg_ref[...] = jnp.zeros_like(dbg_ref)

        q = q_ref[...] * (GLA_DK ** -0.5)
        k = k_ref[...]
        v = v_ref[...]
        vb = v.astype(BF16)
        do, dz, dw = _gate_bwd(do_ref[...], o_ref[...], z_ref[...], w_ref[...])
        dz_ref[...] = dz.astype(BF16)
        dw_ref[0] += dw
        logit, la, live = _gla_log_decay(gl_ref[...], wg_ref[...], bg_ref[...], cn)
        b = _tri_sum(la, False)
        b_scr[...] = b
        b_last = b_scr[CHUNK - 1:CHUNK, :]
        e_last = jnp.exp(b_last)
        st = st_ref[0, 0]
        g1 = ds_scr[...]
        eb = jnp.exp(b)
        qe = q * eb
        dqe = _dot(do, st)
        dq_scr[...] = dqe * eb
        db_scr[...] = dqe * qe
        ekb = jnp.exp(b_last - b)
        kdec = k * ekb
        dkdec = _dot(v, g1)
        dv_scr[...] = _dot(kdec, g1, NT)
        dk_scr[...] = dkdec * ekb
        wk = dkdec * kdec
        db_scr[...] -= wk
        dbl = jnp.sum(wk, axis=0, keepdims=True) + jnp.sum(g1 * st, axis=0, keepdims=True) * e_last
        ds_scr[...] = g1 * e_last + _dot(do, qe, TN)
        rowc = lax.broadcasted_iota(jnp.int32, (CHUNK, 1), 0)
        rows16 = lax.broadcasted_iota(jnp.int32, (SUB, 1), 0)
        da_full = _dot(do, vb, NT)
        a_tot = jnp.zeros((CHUNK, CHUNK), F32)
        for s in range(1, NSUB):
            lo = s * SUB
            bref = b_scr[lo - 1:lo, :]
            in_s = jnp.logical_and(rowc >= lo, rowc < lo + SUB)
            eq = jnp.exp(jnp.where(in_s, b - bref, -1e30))
            ek = jnp.exp(jnp.where(rowc < lo, bref - b, -1e30))
            qh = q * eq
            kh = k * ek
            a_tot = a_tot + _dot(qh, kh, NT)
            da = jnp.where(in_s, da_full, 0.0)
            dqh = _dot(da, kh)
            dkh = _dot(da, qh, TN)
            tq = dqh * qh
            tk = dkh * kh
            dq_scr[...] += dqh * eq
            dk_scr[...] += dkh * ek
            db_scr[...] += tq - tk
            db_scr[lo - 1:lo, :] += jnp.sum(tk, axis=0, keepdims=True) - jnp.sum(tq, axis=0, keepdims=True)
        dat_full = _dot(vb, do, NT)
        q_scr[...] = q
        lane = lax.broadcasted_iota(jnp.int32, (SUB, CHUNK), 1)
        diag = []
        for s in range(NSUB):
            lo = s * SUB
            qs, ks, bs = q[lo:lo + SUB], k[lo:lo + SUB], b[lo:lo + SUB]
            da_blk, dat_blk = da_full[lo:lo + SUB], dat_full[lo:lo + SUB]
            dqs = jnp.zeros((SUB, GLA_DK), F32)
            dks = jnp.zeros((SUB, GLA_DK), F32)
            dbs = jnp.zeros((SUB, GLA_DK), F32)
            s_blk = jnp.zeros((SUB, CHUNK), F32)
            for j in range(SUB):
                r = lo + j
                kj = k_ref[r:r + 1, :]
                e = jnp.exp(jnp.where(rows16 >= j, bs - b_scr[r:r + 1, :], -1e30))
                p = qs * e * kj
                s_blk = jnp.where(lane == r, jnp.sum(p, axis=1, keepdims=True), s_blk)
                dcol = jnp.sum(jnp.where(lane == r, da_blk, 0.0), axis=1, keepdims=True)
                dqs = dqs + (dcol * e) * kj
                dbs = dbs + dcol * p
            for i in range(SUB):
                r = lo + i
                e = jnp.exp(jnp.where(rows16 <= i, b_scr[r:r + 1, :] - bs, -1e30))
                drow = jnp.sum(jnp.where(lane == r, dat_blk, 0.0), axis=1, keepdims=True)
                nq = (drow * e) * q_scr[r:r + 1, :]
                dks = dks + nq
                dbs = dbs - nq * ks
            dq_scr[lo:lo + SUB, :] += dqs
            dk_scr[lo:lo + SUB, :] += dks
            db_scr[lo:lo + SUB, :] += dbs
            diag.append(s_blk)
        dv_scr[...] += _dot(a_tot + jnp.concatenate(diag, axis=0), do, TN)
        db_scr[CHUNK - 1:CHUNK, :] += dbl
        dla = _tri_sum(db_scr[...], True)
        dlogit = jnp.where(live, dla * (1.0 / GLA_TAU) * _sigmoid(-logit), 0.0)
        dl_ref[...] = dlogit
        dbg_ref[0] += jnp.sum(dlogit, axis=0, keepdims=True)
        dq_ref[...] = (dq_scr[...] * (GLA_DK ** -0.5)).astype(BF16)
        dk_ref[...] = dk_scr[...].astype(BF16)
        dv_ref[...] = dv_scr[...].astype(BF16)

    in_specs = _gla_in_specs(True, nc) + [
        pl.BlockSpec((CHUNK, GLA_DV), lambda h, n: (rn(n), h)),
        pl.BlockSpec((CHUNK, GLA_DV), lambda h, n: (rn(n), h)),
        pl.BlockSpec((1, 1, GLA_DV, GLA_DK), lambda h, n: (h, rn(n), 0, 0)),
    ]
    return pl.pallas_call(
        body, name="gla_bwd", grid=(GLA_HEADS, nc),
        in_specs=in_specs,
        out_specs=[pl.BlockSpec((CHUNK, GLA_DK), lambda h, n: (rn(n), h)),
                   pl.BlockSpec((CHUNK, GLA_DK), lambda h, n: (rn(n), h)),
                   pl.BlockSpec((CHUNK, GLA_DV), lambda h, n: (rn(n), h)),
                   pl.BlockSpec((CHUNK, GLA_DV), lambda h, n: (rn(n), h)),
                   pl.BlockSpec((CHUNK, GLA_DK), lambda h, n: (rn(n), h)),
                   pl.BlockSpec((1, 1, GLA_DV), lambda h, n: (h, 0, 0)),
                   pl.BlockSpec((1, 1, GLA_DK), lambda h, n: (h, 0, 0))],
        out_shape=[jax.ShapeDtypeStruct((rows, GLA_QK), BF16), jax.ShapeDtypeStruct((rows, GLA_QK), BF16),
                   jax.ShapeDtypeStruct((rows, GLA_W), BF16), jax.ShapeDtypeStruct((rows, GLA_W), BF16),
                   jax.ShapeDtypeStruct((rows, GLA_QK), F32),
                   jax.ShapeDtypeStruct((GLA_HEADS, 1, GLA_DV), F32),
                   jax.ShapeDtypeStruct((GLA_HEADS, 1, GLA_DK), F32)],
        scratch_shapes=[pltpu.VMEM((GLA_DV, GLA_DK), F32), pltpu.VMEM((CHUNK, GLA_DK), F32),
                        pltpu.VMEM((CHUNK, GLA_DK), F32), pltpu.VMEM((CHUNK, GLA_DV), F32),
                        pltpu.VMEM((CHUNK, GLA_DK), F32), pltpu.VMEM((CHUNK, GLA_DK), F32),
                        pltpu.VMEM((CHUNK, GLA_DK), F32)],
        compiler_params=pltpu.CompilerParams(dimension_semantics=("parallel", "arbitrary")),
    )(proj, proj, proj, proj, proj, wgate, bgate, normw, o_gla, d_oc, states)


def _adamw(name, w, g, m, v):
    rows, cols = w.shape
    tm = 8
    for cand in range(8, rows + 1, 8):
        if rows % cand == 0 and cand * cols * 4 <= 2 ** 21:
            tm = cand
    c1 = 1.0 - ADAM_B1 ** ADAM_STEP
    c2 = 1.0 - ADAM_B2 ** ADAM_STEP

    def body(w_ref, g_ref, m_ref, v_ref, d_ref, nm_ref, nv_ref):
        gv = g_ref[...]
        nm = ADAM_B1 * m_ref[...] + (1.0 - ADAM_B1) * gv
        nv = ADAM_B2 * v_ref[...] + (1.0 - ADAM_B2) * (gv * gv)
        nm_ref[...] = nm
        nv_ref[...] = nv
        d_ref[...] = -ADAM_LR * ((nm / c1) / (jnp.sqrt(nv / c2) + ADAM_EPS) + ADAM_WD * w_ref[...])

    blk = pl.BlockSpec((tm, cols), lambda i: (i, 0))
    return pl.pallas_call(
        body, name=name, grid=(rows // tm,),
        in_specs=[blk] * 4, out_specs=[blk] * 3,
        out_shape=[jax.ShapeDtypeStruct((rows, cols), F32)] * 3,
    )(w, g, m, v)


def _place():
    x, y, c = lax.axis_index("x"), lax.axis_index("y"), lax.axis_index("c")
    chips = [(1 - x, y), (x, 1 - y), (1 - x, 1 - y)]
    return x, y, c, chips


ANY = pl.BlockSpec(memory_space=pl.ANY)


def _gathered_struct(shape, dtype, kind):
    r, cc = shape
    if kind == "row":
        return jax.ShapeDtypeStruct((N_SHARD * r, cc), dtype)
    if kind == "col":
        return jax.ShapeDtypeStruct((r, N_SHARD * cc), dtype)
    return jax.ShapeDtypeStruct((N_SHARD, r, cc), dtype)


def _cast_place(name, w, kind, mine_arr, dtype):
    r, cc = w.shape
    tr = r
    for cand in (256, 128, 64, 32, 16):
        if r % cand == 0:
            tr = cand
            break
    nb = r // tr
    if kind == "row":
        o_spec = pl.BlockSpec((tr, cc), lambda i, m: (m[0] * nb + i, 0))
    elif kind == "col":
        o_spec = pl.BlockSpec((tr, cc), lambda i, m: (i, m[0]))
    else:
        o_spec = pl.BlockSpec((None, tr, cc), lambda i, m: (m[0], i, 0))
    w_spec = pl.BlockSpec((tr, cc), lambda i, m: (i, 0))

    def body(m_ref, w_ref, o_ref):
        o_ref[...] = w_ref[...].astype(o_ref.dtype)

    return pl.pallas_call(
        body, name=name,
        grid_spec=pltpu.PrefetchScalarGridSpec(
            num_scalar_prefetch=1, grid=(nb,), in_specs=[w_spec], out_specs=o_spec),
        out_shape=_gathered_struct((r, cc), dtype, kind),
    )(mine_arr, w)


def _allreduce_small(buf):
    rows, cols = buf.shape

    def body(in_ref, out_ref, sib_ref, pair_ref, far_ref, send_sems, recv_sems):
        x, y, c, chips = _place()
        sibling = (x, y, 1 - c)
        to_sib = pltpu.make_async_remote_copy(
            src_ref=in_ref, dst_ref=sib_ref, send_sem=send_sems.at[0], recv_sem=recv_sems.at[0],
            device_id=sibling, device_id_type=MESH)
        to_sib.start()
        to_sib.wait()
        pair_ref[...] = in_ref[...] + sib_ref[...]
        far = [pltpu.make_async_remote_copy(
            src_ref=pair_ref, dst_ref=far_ref.at[j], send_sem=send_sems.at[1 + j], recv_sem=recv_sems.at[1 + j],
            device_id=(*chip, c), device_id_type=MESH) for j, chip in enumerate(chips)]
        for cp in far:
            cp.start()
        for cp in far:
            cp.wait()
        out_ref[...] = (pair_ref[...] + far_ref[1]) + (far_ref[0] + far_ref[2])

    vm = pl.BlockSpec(memory_space=pltpu.VMEM)
    return pl.pallas_call(
        body, name="allreduce_small",
        in_specs=[vm], out_specs=vm,
        out_shape=jax.ShapeDtypeStruct((rows, cols), F32),
        scratch_shapes=[pltpu.VMEM((rows, cols), F32), pltpu.VMEM((rows, cols), F32),
                        pltpu.VMEM((3, rows, cols), F32),
                        pltpu.SemaphoreType.DMA((4,)), pltpu.SemaphoreType.DMA((4,))],
        compiler_params=pltpu.CompilerParams(has_side_effects=True),
    )(buf)


def _shard_window(ref, kind, shard_shape, shard, half):
    r, cc = shard_shape
    hr = r // 2
    if kind == "row":
        return ref.at[pl.ds(_mo(shard * r + half * hr, 8), hr), :]
    if kind == "col":
        return ref.at[pl.ds(_mo(half * hr, 8), hr), pl.ds(_mo(shard * cc, 128), cc)]
    return ref.at[shard, pl.ds(_mo(half * hr, 8), hr), :]


HBM = pl.BlockSpec(memory_space=pltpu.HBM)
SEM = pl.BlockSpec(memory_space=pltpu.SEMAPHORE)
DATAFLOW = pltpu.SideEffectType.DATAFLOW_SIDE_EFFECTING


def _in_hbm(a):
    return pltpu.with_memory_space_constraint(a, pltpu.HBM)


def _empty_hbm(shape, dtype):
    return _in_hbm(lax.empty(shape, dtype))


def _copies_start(name, bufs, n_copies, plan, carry):
    nb = len(bufs)

    def body(*refs):
        send_sems, recv_sems = refs[nb + 1], refs[nb + 2]
        for k, (src, dst, to) in enumerate(plan(refs[:nb])):
            pltpu.make_async_remote_copy(src_ref=src, dst_ref=dst, send_sem=send_sems.at[k], recv_sem=recv_sems.at[k],
                                         device_id=to, device_id_type=MESH).start()

    passed = list(bufs) + [carry]
    out = pl.pallas_call(
        body, name=name,
        in_specs=[HBM] * (nb + 1), out_specs=[SEM, SEM] + [HBM] * (nb + 1),
        out_shape=[pltpu.SemaphoreType.DMA((n_copies,)), pltpu.SemaphoreType.DMA((n_copies,))]
        + [pltpu.HBM(a.shape, a.dtype) for a in passed],
        input_output_aliases={i: 2 + i for i in range(nb + 1)},
        compiler_params=pltpu.CompilerParams(has_side_effects=DATAFLOW),
    )(*[_in_hbm(a) for a in passed])
    return out[0], out[1], list(out[2:2 + nb]), out[2 + nb]


def _copies_wait(name, send_sems, recv_sems, bufs, plan, after):
    nb = len(bufs)
    after = list(after) if isinstance(after, (list, tuple)) else [after]

    def body(*refs):
        send, recv = refs[nb], refs[nb + 1]
        for k, (src, dst, to) in enumerate(plan(refs[:nb])):
            cp = pltpu.make_async_remote_copy(src_ref=src, dst_ref=dst, send_sem=send.at[k], recv_sem=recv.at[k],
                                              device_id=to, device_id_type=MESH)
            cp.wait_send()
            cp.wait_recv()

    out = pl.pallas_call(
        body, name=name,
        in_specs=[HBM] * nb + [SEM, SEM] + [ANY] * len(after), out_specs=[HBM] * nb,
        out_shape=[pltpu.HBM(a.shape, a.dtype) for a in bufs],
        input_output_aliases={i: i for i in range(nb)},
        compiler_params=pltpu.CompilerParams(has_side_effects=DATAFLOW),
    )(*bufs, send_sems, recv_sems, *after)
    return list(out)


def _gather_ici_plan(shard_shapes, kinds):
    n_arr = len(kinds)

    def plan(refs):
        x, y, c, chips = _place()
        out = []
        for i in range(n_arr):
            w = _shard_window(refs[i], kinds[i], shard_shapes[i], 2 * x + y, c)
            out += [(w, w, (*chip, c)) for chip in chips]
        return out

    return plan


def _gather_d2d_plan(shard_shapes, kinds):
    n_arr = len(kinds)

    def plan(refs):
        x, y, c, chips = _place()
        out = []
        for i in range(n_arr):
            for chip in chips:
                w = _shard_window(refs[i], kinds[i], shard_shapes[i], 2 * chip[0] + chip[1], c)
                out.append((w, w, (x, y, 1 - c)))
        return out

    return plan


def _rs_pair_plan(kinds, shard_shapes):
    n_arr = len(kinds)

    def plan(refs):
        x, y, c, _ = _place()
        out = []
        for i in range(n_arr):
            for s in range(N_SHARD):
                out.append((_shard_window(refs[i], kinds[i], shard_shapes[i], s, 1 - c), refs[n_arr + i].at[s],
                            (x, y, 1 - c)))
        return out

    return plan


def _rs_chip_plan(n_arr):
    def plan(refs):
        x, y, c, chips = _place()
        out = []
        for i in range(n_arr):
            for j, chip in enumerate(chips):
                out.append((refs[i].at[2 * chip[0] + chip[1]], refs[n_arr + i].at[j], (*chip, c)))
        return out

    return plan


def _rs_pair_add(name, grad, got, kind, shard_shape, c):
    r, cc = shard_shape
    hr = r // 2
    tr = hr
    for cand in (256, 128, 64, 32, 16):
        if hr % cand == 0:
            tr = cand
            break
    nb = hr // tr

    if kind == "row":
        g_spec = pl.BlockSpec((tr, cc), lambda s, i, cr: (s * 2 * nb + cr[0] * nb + i, 0))
    elif kind == "col":
        g_spec = pl.BlockSpec((tr, cc), lambda s, i, cr: (cr[0] * nb + i, s))
    else:
        g_spec = pl.BlockSpec((None, tr, cc), lambda s, i, cr: (s, cr[0] * nb + i, 0))
    t_spec = pl.BlockSpec((None, tr, cc), lambda s, i, cr: (s, i, 0))

    def body(c_ref, g_ref, t_ref, p_ref, pb_ref):
        p = g_ref[...] + t_ref[...]
        p_ref[...] = p
        pb_ref[...] = p.astype(BF16)

    return pl.pallas_call(
        body, name=name,
        grid_spec=pltpu.PrefetchScalarGridSpec(
            num_scalar_prefetch=1, grid=(N_SHARD, nb),
            in_specs=[g_spec, t_spec], out_specs=[t_spec, t_spec]),
        out_shape=[jax.ShapeDtypeStruct((N_SHARD, hr, cc), F32), jax.ShapeDtypeStruct((N_SHARD, hr, cc), BF16)],
    )(c, grad, got)


def _rs_chip_add(name, pair_f32, got, shard_shape, mine_c):
    r, cc = shard_shape
    hr = r // 2
    tr = hr
    for cand in (256, 128, 64, 32, 16):
        if hr % cand == 0:
            tr = cand
            break
    nb = hr // tr

    def body(mc_ref, p_ref, t0_ref, t1_ref, t2_ref, o_ref):
        o_ref[...] = (p_ref[...] + t1_ref[...].astype(F32)) + (t0_ref[...].astype(F32) + t2_ref[...].astype(F32))

    def far(j):
        return pl.BlockSpec((None, tr, cc), lambda i, mc: (j, i, 0))

    return pl.pallas_call(
        body, name=name,
        grid_spec=pltpu.PrefetchScalarGridSpec(
            num_scalar_prefetch=1, grid=(nb,),
            in_specs=[pl.BlockSpec((None, tr, cc), lambda i, mc: (mc[0], i, 0)), far(0), far(1), far(2)],
            out_specs=pl.BlockSpec((tr, cc), lambda i, mc: (mc[1] * nb + i, 0))),
        out_shape=jax.ShapeDtypeStruct((r, cc), F32),
    )(mine_c, pair_f32, got, got, got)


def _rs_pair_share(name, halves, shard_shapes):
    n_arr = len(halves)

    def body(*refs):
        ins = refs[:n_arr]
        outs = refs[n_arr:2 * n_arr]
        send_sems, recv_sems = refs[2 * n_arr:]
        x, y, c, _ = _place()
        sibling = (x, y, 1 - c)
        cps = []
        for i in range(n_arr):
            hr = shard_shapes[i][0] // 2
            rows = pl.ds(_mo(c * hr, 8), hr)
            cp = pltpu.make_async_remote_copy(
                src_ref=outs[i].at[rows, :], dst_ref=outs[i].at[rows, :],
                send_sem=send_sems.at[i], recv_sem=recv_sems.at[i],
                device_id=sibling, device_id_type=MESH)
            cp.start()
            cps.append(cp)
        for cp in cps:
            cp.wait()

    return pl.pallas_call(
        body, name=name,
        in_specs=[ANY] * n_arr, out_specs=[ANY] * n_arr,
        out_shape=[jax.ShapeDtypeStruct(s, F32) for s in shard_shapes],
        input_output_aliases={i: i for i in range(n_arr)},
        scratch_shapes=[pltpu.SemaphoreType.DMA((n_arr,)), pltpu.SemaphoreType.DMA((n_arr,))],
        compiler_params=pltpu.CompilerParams(has_side_effects=True),
    )(*halves)


def _pack(arrays):
    flat = []
    for a in arrays:
        v = a.reshape(-1).astype(F32)
        flat.append(jnp.pad(v, (0, (-v.shape[0]) % SMALL_COLS)))
    buf = jnp.concatenate(flat).reshape(-1, SMALL_COLS)
    return jnp.pad(buf, ((0, (-buf.shape[0]) % 16), (0, 0)))


def _unpack(buf, shapes):
    out = []
    row = 0
    for s in shapes:
        size = math.prod(s)
        nrow = -(-size // SMALL_COLS)
        out.append(buf[row:row + nrow].reshape(-1)[:size].reshape(s))
        row += nrow
    return out


def kernel(x, meta, norm_ab_w, w_in_ab, ret_norm_w, s5_lam_re, s5_lam_im, s5_log_dt, s5_b_re, s5_b_im, s5_c_re, s5_c_im, s5_d, s5_w_glu, w_out_ab, norm_c_w, w_in_c, gla_w_gate, gla_b_gate, gla_norm_w, w_out_c, final_norm_w, loss_target, m_meta, m_norm_ab_w, m_w_in_ab, m_ret_norm_w, m_s5_lam_re, m_s5_lam_im, m_s5_log_dt, m_s5_b_re, m_s5_b_im, m_s5_c_re, m_s5_c_im, m_s5_d, m_s5_w_glu, m_w_out_ab, m_norm_c_w, m_w_in_c, m_gla_w_gate, m_gla_b_gate, m_gla_norm_w, m_w_out_c, m_final_norm_w, v_meta, v_norm_ab_w, v_w_in_ab, v_ret_norm_w, v_s5_lam_re, v_s5_lam_im, v_s5_log_dt, v_s5_b_re, v_s5_b_im, v_s5_c_re, v_s5_c_im, v_s5_d, v_s5_w_glu, v_w_out_ab, v_norm_c_w, v_w_in_c, v_gla_w_gate, v_gla_b_gate, v_gla_norm_w, v_w_out_c, v_final_norm_w):
    seq = x.shape[1]
    rows = seq + CHUNK
    xi, yi, ci = lax.axis_index("x"), lax.axis_index("y"), lax.axis_index("c")
    mine = 2 * xi + yi
    c_arr = jnp.reshape(ci, (1,)).astype(jnp.int32)
    mine_c = jnp.stack([mine, ci]).astype(jnp.int32)

    mine_arr = jnp.reshape(mine, (1,)).astype(jnp.int32)
    small_shard = _pack([meta, norm_c_w, gla_norm_w, gla_b_gate, gla_w_gate[0]])
    first_kinds = ["col", "stack"]
    first_shapes = [w_in_ab.shape[1:], small_shard.shape]
    first_ici = _gather_ici_plan(first_shapes, first_kinds)
    first_d2d = _gather_d2d_plan(first_shapes, first_kinds)
    f_send, f_recv, f_bufs, small_shard = _copies_start(
        "gather_first_ici_start",
        [_cast_place("place_w_in_ab", w_in_ab[0], "col", mine_arr, BF16),
         _cast_place("place_small", small_shard, "stack", mine_arr, F32)], 6, first_ici, small_shard)
    late = [("w_out_ab", w_out_ab[0]), ("w_in_c", w_in_c[0]), ("w_out_c", w_out_c[0]), ("w_glu", s5_w_glu[0])]
    late_kinds = ["row", "stack", "row", "row"]
    late_shapes = [a.shape for _, a in late]
    ici_plan = _gather_ici_plan(late_shapes, late_kinds)
    d2d_plan = _gather_d2d_plan(late_shapes, late_kinds)
    n_late = 3 * len(late)
    g_bufs = [_cast_place("place_" + nm, a, kd, mine_arr, BF16) for (nm, a), kd in zip(late, late_kinds)]
    cosf, sinf = _rope_tables(rows)
    rtab = _ret_tables()
    ab_re, ab_im, bb_re, bb_im = _s5_discretize(s5_lam_re[0], s5_lam_im[0], s5_log_dt[0], s5_b_re[0], s5_b_im[0])
    ab = (ab_re, ab_im)
    bd_b = (_bdiag_in(bb_re), _bdiag_in(bb_im))
    bd_c = (_bdiag_out(s5_c_re[0]), _bdiag_out(s5_c_im[0]))
    f_bufs = _copies_wait("gather_first_ici_wait", f_send, f_recv, f_bufs, first_ici,
                          [cosf, sinf, bd_b[0], bd_b[1], bd_c[0], bd_c[1]] + g_bufs + list(rtab))
    f_send, f_recv, f_bufs, cosf = _copies_start("gather_first_d2d_start", f_bufs, 6, first_d2d, cosf)
    wab, small_all = _copies_wait("gather_first_d2d_wait", f_send, f_recv, f_bufs, first_d2d, cosf)
    g_send, g_recv, g_bufs, wab = _copies_start("gather_late_ici_start", g_bufs, n_late, ici_plan, wab)
    q4 = D_MODEL // N_SHARD
    g4 = GLA_QK // N_SHARD
    parts = [_unpack(small_all[j], [(N_META, q4), (1, q4), (1, q4), (1, g4), (GLA_RANK, g4)]) for j in range(N_SHARD)]
    meta_f, norm_c_f, gla_norm_f, bgate_f, wgate_f = [jnp.concatenate([p[i] for p in parts], axis=1) for i in range(5)]
    wgate_pad = jnp.pad(wgate_f, ((0, 128 - GLA_RANK), (0, 0)))

    h0, hn0 = _embed_norm(x[0], meta_f, norm_ab_w)

    tm = _row_tile(rows, 1408)
    tmk = _row_tile(rows, 1408)
    proj0 = _matmul("in_proj_ab", hn0, wab, NN, rows, IN_AB, D_MODEL, tm=tm, tn=512, tk=D_MODEL)
    o_ret, o_a, ret_states = _ret_fwd(proj0, cosf, sinf, rtab, ret_norm_w)
    g_bufs = _copies_wait("gather_late_ici_wait", g_send, g_recv, g_bufs, ici_plan, o_a)
    g_send, g_recv, g_bufs, proj0 = _copies_start("gather_late_d2d_start", g_bufs, n_late, d2d_plan, proj0)
    y_s5, g_s5, s5_er, s5_ei = _s5_fwd(proj0, ab, bd_b, bd_c, s5_d)
    wout_ab, wc_st, wout_c, wglu = _copies_wait("gather_late_d2d_wait", g_send, g_recv, g_bufs, d2d_plan, g_s5)
    wc = jnp.concatenate([wc_st[j] for j in range(N_SHARD)] + [jnp.zeros((D_MODEL, IN_C_PAD - IN_C), BF16)], axis=1)
    zb_blk = (2 * RET_QK + 2 * RET_W + S5_W) // 512

    def glu_out(acc, gv, z):
        return gv.astype(F32) * _sigmoid(acc) * (z * _sigmoid(z))

    t_glu = _matmul("glu", g_s5, wglu, NN, rows, S5_W, S5_W, tm=tm, tn=512, tk=S5_W)
    o_b = _matmul("glu_out", g_s5, wglu, NN, rows, S5_W, S5_W, tm=tm, tn=512, tk=S5_W, out_dtype=BF16,
                  extras=[(g_s5, (tm, 512), lambda i, j, kk: (i, j)),
                          (proj0, (tm, 512), lambda i, j, kk: (i, zb_blk + j))],
                  epilogue=glu_out)
    h1 = _matmul("out_proj_ab", None, None, NN, rows, D_MODEL, OUT_AB, tm=tm, tn=512, tk=1024,
                 segs=[(o_a, (0, 0), wout_ab, (0, 0), RET_W, 1024),
                       (o_b, (0, 0), wout_ab, (RET_W // 1024, 0), S5_W, 1024)],
                 extras=[(h0, (tm, 512), lambda i, j, kk: (i, j))], epilogue=lambda acc, r: acc + r)

    hn1 = _rms_fwd("norm_c", h1, norm_c_f)
    proj1 = _matmul("in_proj_c", hn1, wc, NN, rows, IN_C_PAD, D_MODEL, tm=tm, tn=896, tk=D_MODEL)
    o_gla, o_c, gla_states = _gla_fwd(proj1, wgate_pad, bgate_f, gla_norm_f)
    h2 = _matmul("out_proj_c", o_c, wout_c, NN, rows, D_MODEL, GLA_W, tm=tm, tn=512, tk=GLA_W,
                 extras=[(h1, (tm, 512), lambda i, j, kk: (i, j))], epilogue=lambda acc, r: acc + r)
    loss_dev, dh2, d_final = _final_loss(h2, final_norm_w.reshape(1, D_MODEL), loss_target[0])

    g_wout_c = _matmul("d_w_out_c", o_c, dh2, TN, GLA_W, D_MODEL, rows, tm=1024, tn=1024, tk=tmk)
    d_oc = _matmul("d_o_c", dh2, wout_c, NT, rows, GLA_W, D_MODEL, tm=tm, tn=512, tk=1024)
    dq1, dk1, dv1, dz1, dlogit, d_gla_norm, d_bgate = _gla_bwd(proj1, wgate_pad, bgate_f, gla_norm_f, o_gla, d_oc, gla_states)
    gl_blk = (2 * GLA_QK + 2 * GLA_W) // 128
    dgl = _matmul("d_g_low", dlogit, wgate_pad, NT, rows, 128, GLA_QK, tm=tm, tn=128, tk=GLA_QK, out_dtype=BF16)
    g_wgate = _matmul("d_w_gate", proj1, dlogit, TN, 128, GLA_QK, rows, tm=128, tn=GLA_QK, tk=tmk, a_off=(0, gl_blk))
    tms = _row_tile(rows, 704)
    slabs1 = [("q", dq1, GLA_QK), ("k", dk1, GLA_QK), ("v", dv1, GLA_W), ("z", dz1, GLA_W)]
    g_wc = None
    col = 0
    for nm, slab, width in slabs1:
        if g_wc is None:
            g_wc = _matmul("d_w_in_c_" + nm, hn1, slab, TN, D_MODEL, width, rows, tm=1024, tn=1024, tk=tmk,
                           out_shape=jax.ShapeDtypeStruct((D_MODEL, IN_C_PAD), F32))
        else:
            g_wc = _matmul("d_w_in_c_" + nm, hn1, slab, TN, D_MODEL, width, rows, tm=1024, tn=1024, tk=tmk,
                           into=(g_wc, 0, col // 1024))
        col += width
    g_wc = _matmul("d_w_in_c_g", hn1, dgl, TN, D_MODEL, 128, rows, tm=1024, tn=128, tk=tmk, into=(g_wc, 0, col // 128))
    segs1, col = [], 0
    for nm, slab, width in slabs1:
        segs1.append((slab, (0, 0), wc, (0, col // 1024), width, 1024))
        col += width
    segs1.append((dgl, (0, 0), wc, (0, col // 128), 128, 128))
    dhn1 = _matmul("d_hn1", None, None, NT, rows, D_MODEL, IN_C_PAD, tm=tms, tn=512, tk=1024, segs=segs1)
    dh1, d_norm_c = _rms_bwd("norm_c_bwd", dhn1, h1, norm_c_f, dh2)

    g_wout_ab = _matmul("d_w_out_ab_a", o_a, dh1, TN, RET_W, D_MODEL, rows, tm=1024, tn=1024, tk=tmk,
                        out_shape=jax.ShapeDtypeStruct((OUT_AB, D_MODEL), F32))
    g_wout_ab = _matmul("d_w_out_ab_b", o_b, dh1, TN, S5_W, D_MODEL, rows, tm=1024, tn=1024, tk=tmk,
                        into=(g_wout_ab, RET_W // 1024, 0))
    dmix = _matmul("d_mix", dh1, wout_ab, NT, rows, OUT_AB, D_MODEL, tm=tm, tn=512, tk=1024)
    dq0, dk0, dv0, dza, d_ret_norm = _ret_bwd(proj0, cosf, sinf, rtab, ret_norm_w, o_ret, dmix, ret_states)
    dzb, dt_glu, dg_direct = _s5_gate_bwd(dmix, g_s5, t_glu, proj0)
    g_wglu = _matmul("d_w_glu", g_s5, dt_glu, TN, S5_W, S5_W, rows, tm=1024, tn=1024, tk=tmk)
    dy_s5 = _matmul("d_y_s5", dt_glu, wglu, NT, rows, S5_W, S5_W, tm=tm, tn=512, tk=S5_W,
                    extras=[(dg_direct, (tm, 512), lambda i, j, kk: (i, j)),
                            (y_s5, (tm, 512), lambda i, j, kk: (i, j))],
                    epilogue=lambda acc, dg, yv: (acc + dg) * _gelu_grad(yv))
    g_wc_st = jnp.stack([g_wc[:, j * (IN_C // N_SHARD):(j + 1) * (IN_C // N_SHARD)] for j in range(N_SHARD)])
    rs1_names = ["w_out_ab", "w_in_c", "w_out_c", "w_glu"]
    rs1_shapes = [w_out_ab.shape[1:], w_in_c.shape[1:], w_out_c.shape[1:], s5_w_glu.shape[1:]]
    rs1_plan = _rs_pair_plan(late_kinds, rs1_shapes)
    rs1_land = [_empty_hbm((N_SHARD, r // 2, cc), F32) for (r, cc) in rs1_shapes]
    p_send, p_recv, p_bufs, dy_s5 = _copies_start("rs1_pair_start", [g_wout_ab, g_wc_st, g_wout_c, g_wglu] + rs1_land,
                                                  N_SHARD * 4, rs1_plan, dy_s5)
    du, dbr_d, dbi_d, dcr_d, dci_d, dar_p, dai_p, dd_p = _s5_bwd(proj0, dy_s5, ab, bd_b, bd_c, s5_d, (s5_er, s5_ei))
    p_bufs = _copies_wait("rs1_pair_wait", p_send, p_recv, p_bufs, rs1_plan, du)
    rs1_pairs = [_rs_pair_add("rs_pair_add_" + nm, g, t, kd, ss, c_arr)
                 for nm, g, t, kd, ss in zip(rs1_names, p_bufs[:4], p_bufs[4:], late_kinds, rs1_shapes)]
    rs1_chip_plan = _rs_chip_plan(4)
    rs1_land2 = [_empty_hbm((3, r // 2, cc), BF16) for (r, cc) in rs1_shapes]
    c_send, c_recv, c_bufs, du = _copies_start("rs1_chip_start", [p[1] for p in rs1_pairs] + rs1_land2, 12,
                                               rs1_chip_plan, du)
    slabs0 = [("q", dq0, RET_QK), ("k", dk0, RET_QK), ("v", dv0, RET_W), ("za", dza, RET_W), ("u", du, S5_W),
              ("zb", dzb, S5_W)]
    g_wab = None
    col = 0
    for nm, slab, width in slabs0:
        if g_wab is None:
            g_wab = _matmul("d_w_in_ab_" + nm, hn0, slab, TN, D_MODEL, width, rows, tm=1024, tn=1024, tk=tmk,
                            out_shape=jax.ShapeDtypeStruct((D_MODEL, IN_AB), F32))
        else:
            g_wab = _matmul("d_w_in_ab_" + nm, hn0, slab, TN, D_MODEL, width, rows, tm=1024, tn=1024, tk=tmk,
                            into=(g_wab, 0, col // 1024))
        col += width
    rs2_shapes = [w_in_ab.shape[1:]]
    rs2_plan = _rs_pair_plan(["col"], rs2_shapes)
    rs2_land = [_empty_hbm((N_SHARD, rs2_shapes[0][0] // 2, rs2_shapes[0][1]), F32)]
    q_send, q_recv, q_bufs, dzb = _copies_start("rs2_pair_start", [g_wab] + rs2_land, N_SHARD, rs2_plan, dzb)
    segs0, col = [], 0
    for nm, slab, width in slabs0[:-1] + [("zb", dzb, S5_W)]:
        segs0.append((slab, (0, 0), wab, (0, col // 1024), width, 1024))
        col += width
    dhn0 = _matmul("d_hn0", None, None, NT, rows, D_MODEL, IN_AB, tm=tms, tn=512, tk=1024, segs=segs0)
    grad_x, d_meta, d_norm_ab = _rms_bwd_embed(dhn0, h0, norm_ab_w, dh1)
    c_bufs = _copies_wait("rs1_chip_wait", c_send, c_recv, c_bufs, rs1_chip_plan, grad_x)
    grad_x = grad_x[None]
    rs1_halves = [_rs_chip_add("rs_chip_add_" + nm, p[0], t, ss, mine_c)
                  for nm, p, t, ss in zip(rs1_names, rs1_pairs, c_bufs[4:], rs1_shapes)]
    g_w_out_ab, g_w_in_c, g_w_out_c, g_w_glu = _rs_pair_share("rs1_pair_share", rs1_halves, rs1_shapes)
    q_bufs = _copies_wait("rs2_pair_wait", q_send, q_recv, q_bufs, rs2_plan, g_w_glu)
    rs2_pair = _rs_pair_add("rs_pair_add_w_in_ab", q_bufs[0], q_bufs[1], "col", rs2_shapes[0], c_arr)
    rs2_chip_plan = _rs_chip_plan(1)
    rs2_land2 = [_empty_hbm((3, rs2_shapes[0][0] // 2, rs2_shapes[0][1]), BF16)]

    d_ab_re = jnp.sum(dar_p, axis=1).reshape(S5_G, S5_P)
    d_ab_im = jnp.sum(dai_p, axis=1).reshape(S5_G, S5_P)
    small_local = [loss_dev, d_meta, d_norm_ab, d_ret_norm.reshape(1, RET_W), d_ab_re, d_ab_im,
                   _bdiag_in_extract(dbr_d), _bdiag_in_extract(dbi_d),
                   _bdiag_out_extract(dcr_d), _bdiag_out_extract(dci_d),
                   jnp.sum(dd_p, axis=1).reshape(1, S5_W), d_norm_c, g_wgate[:GLA_RANK],
                   d_bgate.reshape(1, GLA_QK), d_gla_norm.reshape(1, GLA_W), d_final]
    small_shapes = [a.shape for a in small_local]
    summed_buf = _allreduce_small(_pack(small_local))
    r_send, r_recv, r_bufs, summed_buf = _copies_start("rs2_chip_start", [rs2_pair[1]] + rs2_land2, 3, rs2_chip_plan,
                                                       summed_buf)
    summed = _unpack(summed_buf, small_shapes)
    (loss, g_meta_f, g_norm_ab, g_ret_norm, g_ab_re, g_ab_im, g_bb_re, g_bb_im, g_c_re, g_c_im, g_d,
     g_norm_c_f, g_wgate_f, g_bgate_f, g_gla_norm_f, g_final) = summed
    _, s5_vjp = jax.vjp(_s5_discretize, s5_lam_re[0], s5_lam_im[0], s5_log_dt[0], s5_b_re[0], s5_b_im[0])
    g_lam_re, g_lam_im, g_log_dt, g_b_re, g_b_im = s5_vjp((g_ab_re, g_ab_im, g_bb_re, g_bb_im))

    def take(a, width):
        return lax.dynamic_slice_in_dim(a, mine * width, width, axis=1)

    grads = {
        "meta": take(g_meta_f, q4), "norm_ab_w": g_norm_ab, "ret_norm_w": g_ret_norm,
        "s5_lam_re": g_lam_re[None], "s5_lam_im": g_lam_im[None], "s5_log_dt": g_log_dt[None],
        "s5_b_re": g_b_re[None], "s5_b_im": g_b_im[None], "s5_c_re": g_c_re[None], "s5_c_im": g_c_im[None],
        "s5_d": g_d, "s5_w_glu": g_w_glu[None], "w_out_ab": g_w_out_ab[None], "norm_c_w": take(g_norm_c_f, q4),
        "w_in_c": g_w_in_c[None], "gla_w_gate": take(g_wgate_f, g4)[None], "gla_b_gate": take(g_bgate_f, g4),
        "gla_norm_w": take(g_gla_norm_f, q4), "w_out_c": g_w_out_c[None], "final_norm_w": g_final.reshape(D_MODEL),
    }
    weights = dict(meta=meta, norm_ab_w=norm_ab_w, w_in_ab=w_in_ab, ret_norm_w=ret_norm_w, s5_lam_re=s5_lam_re,
                   s5_lam_im=s5_lam_im, s5_log_dt=s5_log_dt, s5_b_re=s5_b_re, s5_b_im=s5_b_im, s5_c_re=s5_c_re,
                   s5_c_im=s5_c_im, s5_d=s5_d, s5_w_glu=s5_w_glu, w_out_ab=w_out_ab, norm_c_w=norm_c_w,
                   w_in_c=w_in_c, gla_w_gate=gla_w_gate, gla_b_gate=gla_b_gate, gla_norm_w=gla_norm_w,
                   w_out_c=w_out_c, final_norm_w=final_norm_w)
    m_in = dict(meta=m_meta, norm_ab_w=m_norm_ab_w, w_in_ab=m_w_in_ab, ret_norm_w=m_ret_norm_w,
                s5_lam_re=m_s5_lam_re, s5_lam_im=m_s5_lam_im, s5_log_dt=m_s5_log_dt, s5_b_re=m_s5_b_re,
                s5_b_im=m_s5_b_im, s5_c_re=m_s5_c_re, s5_c_im=m_s5_c_im, s5_d=m_s5_d, s5_w_glu=m_s5_w_glu,
                w_out_ab=m_w_out_ab, norm_c_w=m_norm_c_w, w_in_c=m_w_in_c, gla_w_gate=m_gla_w_gate,
                gla_b_gate=m_gla_b_gate, gla_norm_w=m_gla_norm_w, w_out_c=m_w_out_c, final_norm_w=m_final_norm_w)
    v_in = dict(meta=v_meta, norm_ab_w=v_norm_ab_w, w_in_ab=v_w_in_ab, ret_norm_w=v_ret_norm_w,
                s5_lam_re=v_s5_lam_re, s5_lam_im=v_s5_lam_im, s5_log_dt=v_s5_log_dt, s5_b_re=v_s5_b_re,
                s5_b_im=v_s5_b_im, s5_c_re=v_s5_c_re, s5_c_im=v_s5_c_im, s5_d=v_s5_d, s5_w_glu=v_s5_w_glu,
                w_out_ab=v_w_out_ab, norm_c_w=v_norm_c_w, w_in_c=v_w_in_c, gla_w_gate=v_gla_w_gate,
                gla_b_gate=v_gla_b_gate, gla_norm_w=v_gla_norm_w, w_out_c=v_w_out_c, final_norm_w=v_final_norm_w)
    order = list(weights)
    big_names = ["s5_w_glu", "w_out_ab", "w_in_c", "w_out_c", "w_in_ab"]
    small_names = [nm for nm in order if nm not in big_names]
    delta, new_m, new_v = {}, {}, {}

    def big_update(nm):
        shp = weights[nm].shape
        d2, m2, v2 = _adamw("adamw_" + nm, weights[nm][0], grads[nm][0], m_in[nm][0], v_in[nm][0])
        delta[nm], new_m[nm], new_v[nm] = d2.reshape(shp), m2.reshape(shp), v2.reshape(shp)

    for nm in big_names[:-1]:
        big_update(nm)
    sshapes = [weights[nm].shape for nm in small_names]
    d2, m2, v2 = _adamw("adamw_small", _pack([weights[nm] for nm in small_names]),
                        _pack([grads[nm] for nm in small_names]), _pack([m_in[nm] for nm in small_names]),
                        _pack([v_in[nm] for nm in small_names]))
    for nm, dd, mm, vv in zip(small_names, _unpack(d2, sshapes), _unpack(m2, sshapes), _unpack(v2, sshapes)):
        delta[nm], new_m[nm], new_v[nm] = dd, mm, vv
    r_bufs = _copies_wait("rs2_chip_wait", r_send, r_recv, r_bufs, rs2_chip_plan,
                          [v2] + [new_v[nm] for nm in big_names[:-1]])
    rs2_half = _rs_chip_add("rs_chip_add_w_in_ab", rs2_pair[0], r_bufs[1], rs2_shapes[0], mine_c)
    grads["w_in_ab"] = _rs_pair_share("rs2_pair_share", [rs2_half], rs2_shapes)[0][None]
    big_update("w_in_ab")
    grads = {nm: grads[nm].reshape(weights[nm].shape) for nm in order}
    return (loss.reshape(()), grad_x, *[grads[nm] for nm in order], *[delta[nm] for nm in order],
            *[new_m[nm] for nm in order], *[new_v[nm] for nm in order])
```

```python
import functools
import math

import jax
import jax.numpy as jnp
from jax import lax
from jax.experimental import pallas as pl
from jax.experimental.pallas import tpu as pltpu

F32 = jnp.float32
BF16 = jnp.bfloat16
MESH = pl.DeviceIdType.MESH

D_MODEL = 2048
N_META = 16
CHUNK = 128
SUB = 16
NSUB = CHUNK // SUB
PAD = CHUNK - N_META
EPS = 1e-6

RET_HEADS = 8
RET_DK = 128
RET_DV = 256
RET_QK = RET_HEADS * RET_DK
RET_W = RET_HEADS * RET_DV
ROPE_BASE = 10000.0

S5_W = 1024
S5_GH = 16
S5_G = S5_W // S5_GH
S5_P = 64
S5_TG = 8
S5_NT = S5_G // S5_TG
S5_TU = S5_TG * S5_GH
S5_TS = S5_TG * S5_P
S5_FWD_TILES = 2
S5_BWD_TILES = 1

GLA_HEADS = 4
GLA_DK = 256
GLA_DV = 512
GLA_QK = GLA_HEADS * GLA_DK
GLA_W = GLA_HEADS * GLA_DV
GLA_RANK = 16
GLA_TAU = 16.0

IN_AB = 2 * RET_QK + 2 * RET_W + 2 * S5_W
OUT_AB = RET_W + S5_W
IN_C = 2 * GLA_QK + 2 * GLA_W + GLA_RANK
IN_C_PAD = 2 * GLA_QK + 2 * GLA_W + 128

ADAM_LR = 0.001
ADAM_B1 = 0.9
ADAM_B2 = 0.999
ADAM_EPS = 1e-08
ADAM_WD = 0.01
ADAM_STEP = 10

N_SHARD = 4
SMALL_COLS = 512

NN = (((1,), (0,)), ((), ()))
NT = (((1,), (1,)), ((), ()))
TN = (((0,), (0,)), ((), ()))


def _dot(a, b, dims=NN):
    return lax.dot_general(a.astype(BF16), b.astype(BF16), dims, preferred_element_type=F32)


def _mo(v, m):
    return v if isinstance(v, int) else pl.multiple_of(v, m)


def _sigmoid(x):
    return 1.0 / (1.0 + jnp.exp(-x))


def _row_tile(rows, cap):
    n = rows // CHUNK
    best = 1
    for d in range(1, n + 1):
        if n % d == 0 and d * CHUNK <= cap:
            best = d
    return best * CHUNK


def _col_tile(cols, cap):
    n = cols // 128
    best = 1
    for d in range(1, n + 1):
        if n % d == 0 and d * 128 <= cap:
            best = d
    return best * 128


def _matmul(name, a, b, dims, m, n, k, *, tm, tn, tk, out_dtype=F32, a_off=(0, 0), b_off=(0, 0),
            extras=(), epilogue=None, out_shape=None, out_spec=None, segs=None, into=None):
    if segs is None:
        segs = [(a, a_off, b, b_off, k, tk)]
    assert m % tm == 0 and n % tn == 0, (name, m, n, tm, tn)
    starts, counts = [], []
    nk = 0
    for (_, _, _, _, ks, tks) in segs:
        assert ks % tks == 0, (name, ks, tks)
        starts.append(nk)
        counts.append(ks // tks)
        nk += ks // tks
    in_specs, operands = [], []
    for s, (sa, (ar, ac), sb, (br, bc), _, tks) in enumerate(segs):
        def kpos(kk, st=starts[s], cnt=counts[s]):
            return jnp.clip(kk - st, 0, cnt - 1) if len(segs) > 1 else kk

        if dims == NN:
            a_spec = pl.BlockSpec((tm, tks), lambda i, j, kk, p=kpos, r=ar, c=ac: (i + r, p(kk) + c))
            b_spec = pl.BlockSpec((tks, tn), lambda i, j, kk, p=kpos, r=br, c=bc: (p(kk) + r, j + c))
        elif dims == NT:
            a_spec = pl.BlockSpec((tm, tks), lambda i, j, kk, p=kpos, r=ar, c=ac: (i + r, p(kk) + c))
            b_spec = pl.BlockSpec((tn, tks), lambda i, j, kk, p=kpos, r=br, c=bc: (j + r, p(kk) + c))
        else:
            a_spec = pl.BlockSpec((tks, tm), lambda i, j, kk, p=kpos, r=ar, c=ac: (p(kk) + r, i + c))
            b_spec = pl.BlockSpec((tks, tn), lambda i, j, kk, p=kpos, r=br, c=bc: (p(kk) + r, j + c))
        in_specs += [a_spec, b_spec]
        operands += [sa, sb]
    n_seg = len(segs)
    n_extra = len(extras)
    if out_shape is None:
        out_shape = jax.ShapeDtypeStruct((m, n), out_dtype)

    def body(*refs):
        e_refs = refs[2 * n_seg:2 * n_seg + n_extra]
        n_in = 2 * n_seg + n_extra + (1 if into is not None else 0)
        o_ref = refs[n_in]
        if nk == 1:
            part = _dot(refs[0][...], refs[1][...], dims)
            if epilogue is not None:
                part = epilogue(part, *[e[...] for e in e_refs])
            o_ref[...] = part.astype(o_ref.dtype)
            return
        acc_ref = refs[n_in + 1]
        kk = pl.program_id(2)

        @pl.when(kk == 0)
        def _():
            acc_ref[...] = jnp.zeros_like(acc_ref)

        if n_seg == 1:
            acc_ref[...] += _dot(refs[0][...], refs[1][...], dims)
        else:
            for s in range(n_seg):
                @pl.when(jnp.logical_and(kk >= starts[s], kk < starts[s] + counts[s]))
                def _(s=s):
                    acc_ref[...] += _dot(refs[2 * s][...], refs[2 * s + 1][...], dims)

        @pl.when(kk == nk - 1)
        def _():
            acc = acc_ref[...]
            if epilogue is not None:
                acc = epilogue(acc, *[e[...] for e in e_refs])
            o_ref[...] = acc.astype(o_ref.dtype)

    if out_spec is None:
        out_spec = pl.BlockSpec((tm, tn), lambda i, j, kk: (i, j))
    in_specs += [pl.BlockSpec(bs, im) for (_, bs, im) in extras]
    operands += [e for (e, _, _) in extras]
    aliases = {}
    if into is not None:
        dest, ro, co = into
        out_shape = jax.ShapeDtypeStruct(dest.shape, dest.dtype)
        out_spec = pl.BlockSpec((tm, tn), lambda i, j, kk: (i + ro, j + co))
        aliases = {len(operands): 0}
        in_specs.append(ANY)
        operands.append(dest)
    return pl.pallas_call(
        body, name=name, grid=(m // tm, n // tn, nk),
        in_specs=in_specs, out_specs=out_spec, out_shape=out_shape, input_output_aliases=aliases,
        scratch_shapes=[] if nk == 1 else [pltpu.VMEM((tm, tn), F32)],
        compiler_params=pltpu.CompilerParams(dimension_semantics=("parallel", "parallel", "arbitrary")),
    )(*operands)


def _rms_fwd(name, h, w):
    rows, d = h.shape
    tm = _row_tile(rows, 512)

    def body(h_ref, w_ref, o_ref):
        x = h_ref[...]
        r = lax.rsqrt(jnp.mean(x * x, axis=-1, keepdims=True) + EPS)
        o_ref[...] = (x * r * w_ref[...]).astype(BF16)

    return pl.pallas_call(
        body, name=name, grid=(rows // tm,),
        in_specs=[pl.BlockSpec((tm, d), lambda i: (i, 0)), pl.BlockSpec((1, d), lambda i: (0, 0))],
        out_specs=pl.BlockSpec((tm, d), lambda i: (i, 0)),
        out_shape=jax.ShapeDtypeStruct((rows, d), BF16),
    )(h, w)


def _rms_bwd(name, dhn, h, w, dres):
    rows, d = h.shape
    tm = _row_tile(rows, 384)

    def body(g_ref, h_ref, w_ref, r_ref, dh_ref, dw_ref):
        i = pl.program_id(0)
        x = h_ref[...]
        r = lax.rsqrt(jnp.mean(x * x, axis=-1, keepdims=True) + EPS)
        xh = x * r
        g = g_ref[...]
        gw = g * w_ref[...]
        dh_ref[...] = r_ref[...] + r * (gw - xh * jnp.mean(gw * xh, axis=-1, keepdims=True))

        @pl.when(i == 0)
        def _():
            dw_ref[...] = jnp.zeros_like(dw_ref)

        dw_ref[...] += jnp.sum(g * xh, axis=0, keepdims=True)

    return pl.pallas_call(
        body, name=name, grid=(rows // tm,),
        in_specs=[pl.BlockSpec((tm, d), lambda i: (i, 0)), pl.BlockSpec((tm, d), lambda i: (i, 0)),
                  pl.BlockSpec((1, d), lambda i: (0, 0)), pl.BlockSpec((tm, d), lambda i: (i, 0))],
        out_specs=[pl.BlockSpec((tm, d), lambda i: (i, 0)), pl.BlockSpec((1, d), lambda i: (0, 0))],
        out_shape=[jax.ShapeDtypeStruct((rows, d), F32), jax.ShapeDtypeStruct((1, d), F32)],
    )(dhn, h, w, dres)


def _embed_norm(x, meta, w):
    seq, d = x.shape
    rows = seq + CHUNK

    def body(x_ref, m_ref, w_ref, h_ref, o_ref):
        i = pl.program_id(0)

        def emit(h):
            h_ref[...] = h
            r = lax.rsqrt(jnp.mean(h * h, axis=-1, keepdims=True) + EPS)
            o_ref[...] = (h * r * w_ref[...]).astype(BF16)

        @pl.when(i == 0)
        def _():
            emit(jnp.concatenate([jnp.zeros((PAD, d), F32), m_ref[...]], axis=0))

        @pl.when(i > 0)
        def _():
            emit(x_ref[...])

    blk = pl.BlockSpec((CHUNK, d), lambda i: (i, 0))
    return pl.pallas_call(
        body, name="embed_norm_ab", grid=(rows // CHUNK,),
        in_specs=[pl.BlockSpec((CHUNK, d), lambda i: (jnp.maximum(i - 1, 0), 0)),
                  pl.BlockSpec((N_META, d), lambda i: (0, 0)), pl.BlockSpec((1, d), lambda i: (0, 0))],
        out_specs=[blk, blk],
        out_shape=[jax.ShapeDtypeStruct((rows, d), F32), jax.ShapeDtypeStruct((rows, d), BF16)],
    )(x, meta, w)


def _rms_bwd_embed(dhn, h, w, dres):
    rows, d = h.shape
    seq = rows - CHUNK

    def body(g_ref, h_ref, w_ref, r_ref, gx_ref, gm_ref, dw_ref):
        i = pl.program_id(0)
        x = h_ref[...]
        r = lax.rsqrt(jnp.mean(x * x, axis=-1, keepdims=True) + EPS)
        xh = x * r
        g = g_ref[...]
        gw = g * w_ref[...]
        dh = r_ref[...] + r * (gw - xh * jnp.mean(gw * xh, axis=-1, keepdims=True))

        @pl.when(i == 0)
        def _():
            dw_ref[...] = jnp.zeros_like(dw_ref)
            gm_ref[...] = dh[PAD:]

        @pl.when(i > 0)
        def _():
            gx_ref[...] = dh

        dw_ref[...] += jnp.sum(g * xh, axis=0, keepdims=True)

    blk = pl.BlockSpec((CHUNK, d), lambda i: (i, 0))
    return pl.pallas_call(
        body, name="norm_ab_bwd", grid=(rows // CHUNK,),
        in_specs=[blk, blk, pl.BlockSpec((1, d), lambda i: (0, 0)), blk],
        out_specs=[pl.BlockSpec((CHUNK, d), lambda i: (jnp.maximum(i - 1, 0), 0)),
                   pl.BlockSpec((N_META, d), lambda i: (0, 0)), pl.BlockSpec((1, d), lambda i: (0, 0))],
        out_shape=[jax.ShapeDtypeStruct((seq, d), F32), jax.ShapeDtypeStruct((N_META, d), F32),
                   jax.ShapeDtypeStruct((1, d), F32)],
    )(dhn, h, w, dres)


def _final_loss(h2, w, target):
    rows, d = h2.shape

    def body(h_ref, w_ref, t_ref, loss_ref, dh_ref, dw_ref):
        i = pl.program_id(0)

        @pl.when(i == 0)
        def _():
            loss_ref[...] = jnp.zeros_like(loss_ref)
            dw_ref[...] = jnp.zeros_like(dw_ref)
            dh_ref[...] = jnp.zeros_like(dh_ref)

        @pl.when(i > 0)
        def _():
            x = h_ref[...]
            r = lax.rsqrt(jnp.mean(x * x, axis=-1, keepdims=True) + EPS)
            xh = x * r
            wv = w_ref[...]
            err = xh * wv - t_ref[...]
            loss_ref[...] += 0.5 * jnp.sum(jnp.mean(err * err, axis=-1, keepdims=True), axis=0, keepdims=True)
            g = err * (1.0 / d)
            gw = g * wv
            dh_ref[...] = r * (gw - xh * jnp.mean(gw * xh, axis=-1, keepdims=True))
            dw_ref[...] += jnp.sum(g * xh, axis=0, keepdims=True)

    return pl.pallas_call(
        body, name="final_loss", grid=(rows // CHUNK,),
        in_specs=[pl.BlockSpec((CHUNK, d), lambda i: (i, 0)), pl.BlockSpec((1, d), lambda i: (0, 0)),
                  pl.BlockSpec((CHUNK, d), lambda i: (jnp.maximum(i - 1, 0), 0))],
        out_specs=[pl.BlockSpec((1, 1), lambda i: (0, 0)), pl.BlockSpec((CHUNK, d), lambda i: (i, 0)),
                   pl.BlockSpec((1, d), lambda i: (0, 0))],
        out_shape=[jax.ShapeDtypeStruct((1, 1), F32), jax.ShapeDtypeStruct((rows, d), F32),
                   jax.ShapeDtypeStruct((1, d), F32)],
    )(h2, w, target)


def _gate_fwd(o, z, w):
    rs = lax.rsqrt(jnp.mean(o * o, axis=-1, keepdims=True) + EPS)
    return o * rs * w * (z * _sigmoid(z))


def _gate_bwd(dout, o, z, w):
    rs = lax.rsqrt(jnp.mean(o * o, axis=-1, keepdims=True) + EPS)
    yn = o * rs
    sg = _sigmoid(z)
    sil = z * sg
    dsil = sg * (1.0 + z * (1.0 - sg))
    dz = dout * yn * w * dsil
    dyn = dout * w * sil
    dw = jnp.sum(dout * yn * sil, axis=0, keepdims=True)
    do = rs * (dyn - yn * jnp.mean(dyn * yn, axis=-1, keepdims=True))
    return do, dz, dw


def _rope(t, cosf, sinf):
    return t * cosf + pltpu.roll(t, RET_DK // 2, 1) * sinf


def _rope_t(d, cosf, sinf):
    return d * cosf + pltpu.roll(d * sinf, RET_DK // 2, 1)


def _ret_tables():
    log_g = jnp.log1p(-jnp.exp2(-5.0 - jnp.arange(RET_HEADS, dtype=F32)))
    idx = jnp.arange(CHUNK, dtype=F32)
    diff = idx[:, None] - idx[None, :]
    decay = jnp.where(diff >= 0, jnp.exp(log_g[:, None, None] * jnp.maximum(diff, 0.0)), 0.0)
    kw = jnp.exp(log_g[:, None] * (CHUNK - 1 - idx))
    qw = jnp.exp(log_g[:, None] * (idx + 1.0))
    gch = jnp.exp(log_g * CHUNK)
    kw = jnp.broadcast_to(kw[:, :, None], (RET_HEADS, CHUNK, RET_DK))
    qw = jnp.broadcast_to(qw[:, :, None], (RET_HEADS, CHUNK, RET_DK))
    gch = jnp.broadcast_to(gch[:, None, None], (RET_HEADS, 1, RET_DV))
    return decay, kw, qw, gch


def _rope_tables(rows):
    pos = jnp.arange(rows, dtype=F32) - float(PAD)
    inv_freq = jnp.power(ROPE_BASE, -jnp.arange(0, RET_DK, 2, dtype=F32) / RET_DK)
    ang = pos[:, None] * inv_freq[None, :]
    cos, sin = jnp.cos(ang), jnp.sin(ang)
    return jnp.concatenate([cos, cos], axis=1), jnp.concatenate([-sin, sin], axis=1)


RET_HB = 8
RET_QB = RET_HB * RET_DK
RET_VB = RET_HB * RET_DV


def _ret_in_specs(rev, nc):
    def cn(n):
        return (nc - 1 - n) if rev else n
    kb = RET_QK // RET_QB
    vb = 2 * RET_QK // RET_VB
    zb = (2 * RET_QK + RET_W) // RET_VB
    return [
        pl.BlockSpec((CHUNK, RET_QB), lambda h, n: (cn(n), h)),
        pl.BlockSpec((CHUNK, RET_QB), lambda h, n: (cn(n), kb + h)),
        pl.BlockSpec((CHUNK, RET_VB), lambda h, n: (cn(n), vb + h)),
        pl.BlockSpec((CHUNK, RET_VB), lambda h, n: (cn(n), zb + h)),
        pl.BlockSpec((CHUNK, RET_DK), lambda h, n: (cn(n), 0)),
        pl.BlockSpec((CHUNK, RET_DK), lambda h, n: (cn(n), 0)),
        pl.BlockSpec((RET_HB, CHUNK, CHUNK), lambda h, n: (h, 0, 0)),
        pl.BlockSpec((RET_HB, CHUNK, RET_DK), lambda h, n: (h, 0, 0)),
        pl.BlockSpec((RET_HB, CHUNK, RET_DK), lambda h, n: (h, 0, 0)),
        pl.BlockSpec((RET_HB, 1, RET_DV), lambda h, n: (h, 0, 0)),
        pl.BlockSpec((1, RET_VB), lambda h, n: (0, h)),
    ]


def _ret_fwd(proj, cosf, sinf, tables, normw):
    rows = proj.shape[0]
    nc = rows // CHUNK
    decay, kw, qw, gch = tables

    def body(q_ref, k_ref, v_ref, z_ref, cos_ref, sin_ref, dm_ref, kw_ref, qw_ref, g_ref, w_ref,
             o_ref, oa_ref, st_ref, s_scr):
        n = pl.program_id(1)

        @pl.when(n == 0)
        def _():
            s_scr[...] = jnp.zeros_like(s_scr)

        cosv, sinv = cos_ref[...], sin_ref[...]
        for hh in range(RET_HB):
            qc = slice(hh * RET_DK, (hh + 1) * RET_DK)
            vc = slice(hh * RET_DV, (hh + 1) * RET_DV)
            q = _rope(q_ref[:, qc], cosv, sinv)
            k = _rope(k_ref[:, qc], cosv, sinv) * (RET_DK ** -0.5)
            v = v_ref[:, vc]
            s = s_scr[hh]
            st_ref[hh, 0] = s.astype(BF16)
            a = _dot(q, k, NT) * dm_ref[hh]
            o = _dot(a, v) + _dot(q * qw_ref[hh], s)
            s_scr[hh] = s * g_ref[hh] + _dot(k * kw_ref[hh], v, TN)
            o_ref[:, vc] = o
            oa_ref[:, vc] = _gate_fwd(o, z_ref[:, vc], w_ref[:, vc]).astype(BF16)

    return pl.pallas_call(
        body, name="ret_fwd", grid=(RET_HEADS // RET_HB, nc),
        in_specs=_ret_in_specs(False, nc),
        out_specs=[pl.BlockSpec((CHUNK, RET_VB), lambda h, n: (n, h)),
                   pl.BlockSpec((CHUNK, RET_VB), lambda h, n: (n, h)),
                   pl.BlockSpec((RET_HB, 1, RET_DK, RET_DV), lambda h, n: (h, n, 0, 0))],
        out_shape=[jax.ShapeDtypeStruct((rows, RET_W), F32), jax.ShapeDtypeStruct((rows, RET_W), BF16),
                   jax.ShapeDtypeStruct((RET_HEADS, nc, RET_DK, RET_DV), BF16)],
        scratch_shapes=[pltpu.VMEM((RET_HB, RET_DK, RET_DV), F32)],
        compiler_params=pltpu.CompilerParams(dimension_semantics=("parallel", "arbitrary")),
    )(proj, proj, proj, proj, cosf, sinf, decay, kw, qw, gch, normw)


def _ret_bwd(proj, cosf, sinf, tables, normw, o_ret, dmix, states):
    rows = proj.shape[0]
    nc = rows // CHUNK
    decay, kw, qw, gch = tables

    def rn(n):
        return nc - 1 - n

    def body(q_ref, k_ref, v_ref, z_ref, cos_ref, sin_ref, dm_ref, kw_ref, qw_ref, g_ref, w_ref,
             o_ref, do_ref, st_ref, dq_ref, dk_ref, dv_ref, dz_ref, dw_ref, ds_scr):
        n = pl.program_id(1)

        @pl.when(n == 0)
        def _():
            ds_scr[...] = jnp.zeros_like(ds_scr)
            dw_ref[...] = jnp.zeros_like(dw_ref)

        cosv, sinv = cos_ref[...], sin_ref[...]
        for hh in range(RET_HB):
            qc = slice(hh * RET_DK, (hh + 1) * RET_DK)
            vc = slice(hh * RET_DV, (hh + 1) * RET_DV)
            q = _rope(q_ref[:, qc], cosv, sinv)
            k = _rope(k_ref[:, qc], cosv, sinv) * (RET_DK ** -0.5)
            v = v_ref[:, vc]
            do, dz, dw = _gate_bwd(do_ref[:, vc], o_ref[:, vc], z_ref[:, vc], w_ref[:, vc])
            dz_ref[:, vc] = dz.astype(BF16)
            dw_ref[hh] += dw
            dm = dm_ref[hh]
            s = st_ref[hh, 0]
            g1 = ds_scr[hh]
            p = _dot(q, k, NT) * dm
            kwv = k * kw_ref[hh]
            qwv = q * qw_ref[hh]
            dp = _dot(do, v, NT)
            da = dp * dm
            dv = _dot(p, do, TN) + _dot(kwv, g1)
            dq = _dot(da, k) + _dot(do, s, NT) * qw_ref[hh]
            dk = _dot(da, q, TN) + _dot(v, g1, NT) * kw_ref[hh]
            ds_scr[hh] = g1 * g_ref[hh] + _dot(qwv, do, TN)
            dv_ref[:, vc] = dv.astype(BF16)
            dq_ref[:, qc] = _rope_t(dq, cosv, sinv).astype(BF16)
            dk_ref[:, qc] = _rope_t(dk * (RET_DK ** -0.5), cosv, sinv).astype(BF16)

    in_specs = _ret_in_specs(True, nc) + [
        pl.BlockSpec((CHUNK, RET_VB), lambda h, n: (rn(n), h)),
        pl.BlockSpec((CHUNK, RET_VB), lambda h, n: (rn(n), h)),
        pl.BlockSpec((RET_HB, 1, RET_DK, RET_DV), lambda h, n: (h, rn(n), 0, 0)),
    ]
    return pl.pallas_call(
        body, name="ret_bwd", grid=(RET_HEADS // RET_HB, nc),
        in_specs=in_specs,
        out_specs=[pl.BlockSpec((CHUNK, RET_QB), lambda h, n: (rn(n), h)),
                   pl.BlockSpec((CHUNK, RET_QB), lambda h, n: (rn(n), h)),
                   pl.BlockSpec((CHUNK, RET_VB), lambda h, n: (rn(n), h)),
                   pl.BlockSpec((CHUNK, RET_VB), lambda h, n: (rn(n), h)),
                   pl.BlockSpec((RET_HB, 1, RET_DV), lambda h, n: (h, 0, 0))],
        out_shape=[jax.ShapeDtypeStruct((rows, RET_QK), BF16), jax.ShapeDtypeStruct((rows, RET_QK), BF16),
                   jax.ShapeDtypeStruct((rows, RET_W), BF16), jax.ShapeDtypeStruct((rows, RET_W), BF16),
                   jax.ShapeDtypeStruct((RET_HEADS, 1, RET_DV), F32)],
        scratch_shapes=[pltpu.VMEM((RET_HB, RET_DK, RET_DV), F32)],
        compiler_params=pltpu.CompilerParams(dimension_semantics=("parallel", "arbitrary")),
    )(proj, proj, proj, proj, cosf, sinf, decay, kw, qw, gch, normw, o_ret, dmix, states)


def _s5_discretize(lam_re, lam_im, log_dt, b_re, b_im):
    dt = jnp.exp(log_dt)[:, None]
    mag = jnp.exp(lam_re * dt)
    ab_re, ab_im = mag * jnp.cos(lam_im * dt), mag * jnp.sin(lam_im * dt)
    den = lam_re * lam_re + lam_im * lam_im
    nr, ni = ab_re - 1.0, ab_im
    f_re = (nr * lam_re + ni * lam_im) / den
    f_im = (ni * lam_re - nr * lam_im) / den
    bb_re = f_re[..., None] * b_re - f_im[..., None] * b_im
    bb_im = f_re[..., None] * b_im + f_im[..., None] * b_re
    return ab_re, ab_im, bb_re, bb_im


def _bdiag_in(bb):
    t = bb.reshape(S5_NT, S5_TG, S5_P, S5_GH).transpose(0, 1, 3, 2)
    eye = jnp.eye(S5_TG, dtype=bb.dtype)
    full = t[:, :, :, None, :] * eye[None, :, None, :, None]
    return full.reshape(S5_NT, S5_TU, S5_TS)


def _bdiag_in_extract(dense):
    t = dense.reshape(S5_NT, S5_TG, S5_GH, S5_TG, S5_P)
    diag = jnp.stack([t[:, g, :, g, :] for g in range(S5_TG)], axis=1)
    return diag.transpose(0, 1, 3, 2).reshape(S5_G, S5_P, S5_GH)


def _bdiag_out(c):
    t = c.reshape(S5_NT, S5_TG, S5_GH, S5_P).transpose(0, 1, 3, 2)
    eye = jnp.eye(S5_TG, dtype=c.dtype)
    full = t[:, :, :, None, :] * eye[None, :, None, :, None]
    return full.reshape(S5_NT, S5_TS, S5_TU)


def _bdiag_out_extract(dense):
    t = dense.reshape(S5_NT, S5_TG, S5_P, S5_TG, S5_GH)
    diag = jnp.stack([t[:, g, :, g, :] for g in range(S5_TG)], axis=1)
    return diag.transpose(0, 1, 3, 2).reshape(S5_G, S5_GH, S5_P)


def _cmul(ar, ai, br, bi):
    return ar * br - ai * bi, ar * bi + ai * br


S5_SEG = 8
S5_STEPS = CHUNK // S5_SEG


def _seg_perm(x):
    c = x.shape[1]
    return jnp.swapaxes(x.reshape(S5_SEG, S5_STEPS, c), 0, 1).reshape(CHUNK, c)


def _seg_unperm(x):
    c = x.shape[1]
    return jnp.swapaxes(x.reshape(S5_STEPS, S5_SEG, c), 0, 1).reshape(CHUNK, c)


def _rows(x, p):
    return x[p * S5_SEG:(p + 1) * S5_SEG]


def _s5_tables(ar, ai, tr_scr, ti_scr, wfr_scr, wfi_scr, wbr_scr, wbi_scr):
    row = lax.broadcasted_iota(jnp.int32, (S5_SEG, 1), 0)
    a8r = jnp.broadcast_to(ar, (S5_SEG, S5_TS))
    a8i = jnp.broadcast_to(ai, (S5_SEG, S5_TS))
    pr, pi = a8r, a8i
    for p in range(S5_STEPS):
        tr_scr[p * S5_SEG:(p + 1) * S5_SEG, :] = pr
        ti_scr[p * S5_SEG:(p + 1) * S5_SEG, :] = pi
        if p < S5_STEPS - 1:
            pr, pi = _cmul(pr, pi, a8r, a8i)
    wr, wi = pr, pi
    sh = 1
    while sh < S5_SEG:
        keep = row >= sh
        sr = jnp.where(keep, pltpu.roll(wr, sh, 0), 1.0)
        si = jnp.where(keep, pltpu.roll(wi, sh, 0), 0.0)
        wr, wi = _cmul(wr, wi, sr, si)
        sh *= 2
    wfr_scr[...] = wr
    wfi_scr[...] = wi
    wr, wi = pr, -pi
    sh = 1
    while sh < S5_SEG:
        keep = row < S5_SEG - sh
        sr = jnp.where(keep, pltpu.roll(wr, S5_SEG - sh, 0), 1.0)
        si = jnp.where(keep, pltpu.roll(wi, S5_SEG - sh, 0), 0.0)
        wr, wi = _cmul(wr, wi, sr, si)
        sh *= 2
    wbr_scr[...] = wr
    wbi_scr[...] = wi


def _seg_scan(vr, vi, ar, ai, tr_scr, ti_scr, wr_scr, wi_scr, c0r, c0i, down):
    row = lax.broadcasted_iota(jnp.int32, (S5_SEG, 1), 0)
    sgn = 1.0 if down else -1.0
    order = list(range(S5_STEPS)) if down else list(range(S5_STEPS - 1, -1, -1))
    xr, xi = _rows(vr, order[0]), _rows(vi, order[0])
    loc = {order[0]: (xr, xi)}
    for p in order[1:]:
        mr, mi = _cmul(ar, sgn * ai, xr, xi)
        xr, xi = mr + _rows(vr, p), mi + _rows(vi, p)
        loc[p] = (xr, xi)
    last = S5_STEPS - 1
    mr, mi = tr_scr[last * S5_SEG:(last + 1) * S5_SEG, :], sgn * ti_scr[last * S5_SEG:(last + 1) * S5_SEG, :]
    er, ei = xr, xi
    sh = 1
    while sh < S5_SEG:
        if down:
            keep = row >= sh
            sr, si = pltpu.roll(er, sh, 0), pltpu.roll(ei, sh, 0)
        else:
            keep = row < S5_SEG - sh
            sr, si = pltpu.roll(er, S5_SEG - sh, 0), pltpu.roll(ei, S5_SEG - sh, 0)
        pr, pi = _cmul(mr, mi, jnp.where(keep, sr, 0.0), jnp.where(keep, si, 0.0))
        er, ei = er + pr, ei + pi
        mr, mi = _cmul(mr, mi, mr, mi)
        sh *= 2
    pr, pi = _cmul(wr_scr[...], wi_scr[...], c0r, c0i)
    er, ei = er + pr, ei + pi
    if down:
        nr = jnp.where(row == 0, c0r, pltpu.roll(er, 1, 0))
        ni = jnp.where(row == 0, c0i, pltpu.roll(ei, 1, 0))
    else:
        nr = jnp.where(row == S5_SEG - 1, c0r, pltpu.roll(er, S5_SEG - 1, 0))
        ni = jnp.where(row == S5_SEG - 1, c0i, pltpu.roll(ei, S5_SEG - 1, 0))
    out_r, out_i = [], []
    for p in range(S5_STEPS):
        q = p if down else S5_STEPS - 1 - p
        pr, pi = _cmul(tr_scr[q * S5_SEG:(q + 1) * S5_SEG, :], sgn * ti_scr[q * S5_SEG:(q + 1) * S5_SEG, :], nr, ni)
        out_r.append(loc[p][0] + pr)
        out_i.append(loc[p][1] + pi)
    return jnp.concatenate(out_r, axis=0), jnp.concatenate(out_i, axis=0), (nr, ni), (er, ei)


def _gelu(y):
    c = math.sqrt(2.0 / math.pi)
    return 0.5 * y * (1.0 + jnp.tanh(c * (y + 0.044715 * y * y * y)))


def _gelu_grad(y):
    c = math.sqrt(2.0 / math.pi)
    th = jnp.tanh(c * (y + 0.044715 * y * y * y))
    return 0.5 * (1.0 + th) + 0.5 * y * (1.0 - th * th) * c * (1.0 + 3.0 * 0.044715 * y * y)


def _s5_fwd(proj, ab, bd_b, bd_c, dvec):
    rows = proj.shape[0]
    nc = rows // CHUNK
    tps = S5_FWD_TILES
    ubw = tps * S5_TU
    ub = (2 * RET_QK + 2 * RET_W) // ubw
    ab_re, ab_im = ab
    bre, bim = bd_b
    cre, cim = bd_c

    def body(u_ref, ar_ref, ai_ref, bre_ref, bim_ref, cre_ref, cim_ref, d_ref,
             y_ref, g_ref, er_ref, ei_ref, tr_scr, ti_scr, wfr_scr, wfi_scr, wbr_scr, wbi_scr,
             cr_scr, ci_scr, er_scr, ei_scr):
        n = pl.program_id(1)
        for tt in range(tps):
            cols = slice(tt * S5_TU, (tt + 1) * S5_TU)
            ar, ai = ar_ref[tt], ai_ref[tt]
            trs, tis, wfr, wfi = tr_scr.at[tt], ti_scr.at[tt], wfr_scr.at[tt], wfi_scr.at[tt]

            @pl.when(n == 0)
            def _(tt=tt, ar=ar, ai=ai, trs=trs, tis=tis, wfr=wfr, wfi=wfi):
                _s5_tables(ar, ai, trs, tis, wfr, wfi, wbr_scr.at[tt], wbi_scr.at[tt])
                cr_scr[tt] = jnp.zeros((S5_SEG, S5_TS), F32)
                ci_scr[tt] = jnp.zeros((S5_SEG, S5_TS), F32)

            u = _seg_perm(u_ref[:, cols])
            c0r, c0i = cr_scr[tt], ci_scr[tt]
            er_ref[tt, 0] = c0r
            ei_ref[tt, 0] = c0i
            xr, xi, _, (er, ei) = _seg_scan(_dot(u, bre_ref[tt]), _dot(u, bim_ref[tt]), ar, ai, trs, tis,
                                            wfr, wfi, c0r, c0i, True)
            er_scr[tt] = er
            ei_scr[tt] = ei
            cr_scr[tt] = jnp.broadcast_to(er_scr[tt, S5_SEG - 1:S5_SEG, :], (S5_SEG, S5_TS))
            ci_scr[tt] = jnp.broadcast_to(ei_scr[tt, S5_SEG - 1:S5_SEG, :], (S5_SEG, S5_TS))
            y = _seg_unperm(_dot(xr, cre_ref[tt]) - _dot(xi, cim_ref[tt]) + d_ref[:, cols] * u)
            y_ref[:, cols] = y
            g_ref[:, cols] = _gelu(y).astype(BF16)

    vec = pl.BlockSpec((tps, 1, S5_TS), lambda t, n: (t, 0, 0))
    return pl.pallas_call(
        body, name="s5_fwd", grid=(S5_NT // tps, nc),
        in_specs=[pl.BlockSpec((CHUNK, ubw), lambda t, n: (n, ub + t)), vec, vec,
                  pl.BlockSpec((tps, S5_TU, S5_TS), lambda t, n: (t, 0, 0)),
                  pl.BlockSpec((tps, S5_TU, S5_TS), lambda t, n: (t, 0, 0)),
                  pl.BlockSpec((tps, S5_TS, S5_TU), lambda t, n: (t, 0, 0)),
                  pl.BlockSpec((tps, S5_TS, S5_TU), lambda t, n: (t, 0, 0)),
                  pl.BlockSpec((1, ubw), lambda t, n: (0, t))],
        out_specs=[pl.BlockSpec((CHUNK, ubw), lambda t, n: (n, t)),
                   pl.BlockSpec((CHUNK, ubw), lambda t, n: (n, t)),
                   pl.BlockSpec((tps, 1, 8, S5_TS), lambda t, n: (t, n, 0, 0)),
                   pl.BlockSpec((tps, 1, 8, S5_TS), lambda t, n: (t, n, 0, 0))],
        out_shape=[jax.ShapeDtypeStruct((rows, S5_W), F32), jax.ShapeDtypeStruct((rows, S5_W), BF16),
                   jax.ShapeDtypeStruct((S5_NT, nc, 8, S5_TS), F32),
                   jax.ShapeDtypeStruct((S5_NT, nc, 8, S5_TS), F32)],
        scratch_shapes=[pltpu.VMEM((tps, CHUNK, S5_TS), F32) for _ in range(2)]
        + [pltpu.VMEM((tps, S5_SEG, S5_TS), F32) for _ in range(8)],
        compiler_params=pltpu.CompilerParams(dimension_semantics=("parallel", "arbitrary")),
    )(proj, ab_re.reshape(S5_NT, 1, S5_TS), ab_im.reshape(S5_NT, 1, S5_TS), bre, bim, cre, cim, dvec)


def _s5_bwd(proj, dy, ab, bd_b, bd_c, dvec, entry):
    rows = proj.shape[0]
    nc = rows // CHUNK
    tps = S5_BWD_TILES
    ubw = tps * S5_TU
    ub = (2 * RET_QK + 2 * RET_W) // ubw
    ab_re, ab_im = ab
    bre, bim = bd_b
    cre, cim = bd_c
    er, ei = entry

    def rn(n):
        return nc - 1 - n

    def body(u_ref, dy_ref, ar_ref, ai_ref, bre_ref, bim_ref, cre_ref, cim_ref, d_ref, er_ref, ei_ref,
             du_ref, dbr_ref, dbi_ref, dcr_ref, dci_ref, dar_ref, dai_ref, dd_ref,
             tr_scr, ti_scr, wfr_scr, wfi_scr, wbr_scr, wbi_scr, gr_scr, gi_scr, er_scr, ei_scr):
        n = pl.program_id(1)

        @pl.when(n == 0)
        def _():
            gr_scr[...] = jnp.zeros_like(gr_scr)
            gi_scr[...] = jnp.zeros_like(gi_scr)
            for r in (dbr_ref, dbi_ref, dcr_ref, dci_ref, dar_ref, dai_ref, dd_ref):
                r[...] = jnp.zeros_like(r)

        for tt in range(tps):
            cols = slice(tt * S5_TU, (tt + 1) * S5_TU)
            ar, ai = ar_ref[tt], ai_ref[tt]
            trs, tis = tr_scr.at[tt], ti_scr.at[tt]

            @pl.when(n == 0)
            def _(tt=tt, ar=ar, ai=ai, trs=trs, tis=tis):
                _s5_tables(ar, ai, trs, tis, wfr_scr.at[tt], wfi_scr.at[tt], wbr_scr.at[tt], wbi_scr.at[tt])

            u = _seg_perm(u_ref[:, cols])
            dy = _seg_perm(dy_ref[:, cols])
            xr, xi, (pr, pi), _ = _seg_scan(_dot(u, bre_ref[tt]), _dot(u, bim_ref[tt]), ar, ai, trs, tis,
                                            wfr_scr.at[tt], wfi_scr.at[tt], er_ref[tt, 0], ei_ref[tt, 0], True)
            dcr_ref[tt] += _dot(xr, dy, TN)
            dci_ref[tt] -= _dot(xi, dy, TN)
            gr, gi, _, (er, ei) = _seg_scan(_dot(dy, cre_ref[tt], NT), -_dot(dy, cim_ref[tt], NT), ar, ai, trs, tis,
                                            wbr_scr.at[tt], wbi_scr.at[tt], gr_scr[tt], gi_scr[tt], False)
            er_scr[tt] = er
            ei_scr[tt] = ei
            gr_scr[tt] = jnp.broadcast_to(er_scr[tt, 0:1, :], (S5_SEG, S5_TS))
            gi_scr[tt] = jnp.broadcast_to(ei_scr[tt, 0:1, :], (S5_SEG, S5_TS))
            xpr = jnp.concatenate([pr, xr[:CHUNK - S5_SEG]], axis=0)
            xpi = jnp.concatenate([pi, xi[:CHUNK - S5_SEG]], axis=0)
            dar_ref[tt] += jnp.sum((xpr * gr + xpi * gi).reshape(S5_STEPS, S5_SEG, S5_TS), axis=0)
            dai_ref[tt] += jnp.sum((xpr * gi - xpi * gr).reshape(S5_STEPS, S5_SEG, S5_TS), axis=0)
            dbr_ref[tt] += _dot(u, gr, TN)
            dbi_ref[tt] += _dot(u, gi, TN)
            dd_ref[tt] += jnp.sum((dy * u).reshape(S5_STEPS, S5_SEG, S5_TU), axis=0)
            du = dy * d_ref[:, cols] + _dot(gr, bre_ref[tt], NT) + _dot(gi, bim_ref[tt], NT)
            du_ref[:, cols] = _seg_unperm(du).astype(BF16)

    vec = pl.BlockSpec((tps, 1, S5_TS), lambda t, n: (t, 0, 0))
    acc_b = pl.BlockSpec((tps, S5_TU, S5_TS), lambda t, n: (t, 0, 0))
    acc_c = pl.BlockSpec((tps, S5_TS, S5_TU), lambda t, n: (t, 0, 0))
    acc_a = pl.BlockSpec((tps, 8, S5_TS), lambda t, n: (t, 0, 0))
    ent = pl.BlockSpec((tps, 1, 8, S5_TS), lambda t, n: (t, rn(n), 0, 0))
    return pl.pallas_call(
        body, name="s5_bwd", grid=(S5_NT // tps, nc),
        in_specs=[pl.BlockSpec((CHUNK, ubw), lambda t, n: (rn(n), ub + t)),
                  pl.BlockSpec((CHUNK, ubw), lambda t, n: (rn(n), t)), vec, vec,
                  acc_b, acc_b, acc_c, acc_c, pl.BlockSpec((1, ubw), lambda t, n: (0, t)), ent, ent],
        out_specs=[pl.BlockSpec((CHUNK, ubw), lambda t, n: (rn(n), t)), acc_b, acc_b, acc_c, acc_c, acc_a, acc_a,
                   pl.BlockSpec((tps, 8, S5_TU), lambda t, n: (t, 0, 0))],
        out_shape=[jax.ShapeDtypeStruct((rows, S5_W), BF16),
                   jax.ShapeDtypeStruct((S5_NT, S5_TU, S5_TS), F32), jax.ShapeDtypeStruct((S5_NT, S5_TU, S5_TS), F32),
                   jax.ShapeDtypeStruct((S5_NT, S5_TS, S5_TU), F32), jax.ShapeDtypeStruct((S5_NT, S5_TS, S5_TU), F32),
                   jax.ShapeDtypeStruct((S5_NT, 8, S5_TS), F32), jax.ShapeDtypeStruct((S5_NT, 8, S5_TS), F32),
                   jax.ShapeDtypeStruct((S5_NT, 8, S5_TU), F32)],
        scratch_shapes=[pltpu.VMEM((tps, CHUNK, S5_TS), F32) for _ in range(2)]
        + [pltpu.VMEM((tps, S5_SEG, S5_TS), F32) for _ in range(8)],
        compiler_params=pltpu.CompilerParams(dimension_semantics=("parallel", "arbitrary")),
    )(proj, dy,ab_re.reshape(S5_NT, 1, S5_TS), ab_im.reshape(S5_NT, 1, S5_TS), bre, bim, cre, cim, dvec, er, ei)


def _s5_gate_bwd(dmix, g, t, proj):
    rows = g.shape[0]
    tm = _row_tile(rows, 384)
    ob = RET_W // S5_W
    zb = (2 * RET_QK + 2 * RET_W + S5_W) // S5_W

    def body(do_ref, g_ref, t_ref, z_ref, dz_ref, dt_ref, dg_ref):
        do = do_ref[...]
        gv = g_ref[...].astype(F32)
        z = z_ref[...]
        st = _sigmoid(t_ref[...])
        sg = _sigmoid(z)
        os5 = gv * st
        dz_ref[...] = (do * os5 * sg * (1.0 + z * (1.0 - sg))).astype(BF16)
        dos = do * z * sg
        dt_ref[...] = (dos * gv * st * (1.0 - st)).astype(BF16)
        dg_ref[...] = dos * st

    blk = pl.BlockSpec((tm, S5_W), lambda i: (i, 0))
    return pl.pallas_call(
        body, name="s5_gate_bwd", grid=(rows // tm,),
        in_specs=[pl.BlockSpec((tm, S5_W), lambda i: (i, ob)), blk, blk,
                  pl.BlockSpec((tm, S5_W), lambda i: (i, zb))],
        out_specs=[blk, blk, blk],
        out_shape=[jax.ShapeDtypeStruct((rows, S5_W), BF16), jax.ShapeDtypeStruct((rows, S5_W), BF16),
                   jax.ShapeDtypeStruct((rows, S5_W), F32)],
    )(dmix, g, t, proj)


def _split3(x):
    hi = x.astype(BF16)
    r = x - hi.astype(F32)
    mid = r.astype(BF16)
    lo = (r - mid.astype(F32)).astype(BF16)
    return hi, mid, lo


def _tri_sum(x, upper):
    i = lax.broadcasted_iota(jnp.int32, (CHUNK, CHUNK), 0)
    j = lax.broadcasted_iota(jnp.int32, (CHUNK, CHUNK), 1)
    tri = jnp.where((j >= i) if upper else (j <= i), 1.0, 0.0).astype(BF16)
    hi, mid, lo = _split3(x)
    return _dot(tri, lo) + _dot(tri, mid) + _dot(tri, hi)


def _gla_log_decay(gl, wg, bg, n):
    logit = _dot(gl, wg) + bg
    la = (jnp.minimum(logit, 0.0) - jnp.log(1.0 + jnp.exp(-jnp.abs(logit)))) * (1.0 / GLA_TAU)
    row = lax.broadcasted_iota(jnp.int32, (CHUNK, 1), 0)
    live = jnp.logical_or(n > 0, row >= PAD)
    return logit, jnp.where(live, la, 0.0), live


def _gla_in_specs(rev, nc):
    def cn(n):
        return (nc - 1 - n) if rev else n
    kb = GLA_QK // GLA_DK
    vb = 2 * GLA_QK // GLA_DV
    zb = (2 * GLA_QK + GLA_W) // GLA_DV
    gb = (2 * GLA_QK + 2 * GLA_W) // 128
    return [
        pl.BlockSpec((CHUNK, GLA_DK), lambda h, n: (cn(n), h)),
        pl.BlockSpec((CHUNK, GLA_DK), lambda h, n: (cn(n), kb + h)),
        pl.BlockSpec((CHUNK, GLA_DV), lambda h, n: (cn(n), vb + h)),
        pl.BlockSpec((CHUNK, GLA_DV), lambda h, n: (cn(n), zb + h)),
        pl.BlockSpec((CHUNK, 128), lambda h, n: (cn(n), gb)),
        pl.BlockSpec((128, GLA_DK), lambda h, n: (0, h)),
        pl.BlockSpec((1, GLA_DK), lambda h, n: (0, h)),
        pl.BlockSpec((1, GLA_DV), lambda h, n: (0, h)),
    ]


def _gla_fwd(proj, wgate, bgate, normw):
    rows = proj.shape[0]
    nc = rows // CHUNK

    def body(q_ref, k_ref, v_ref, z_ref, gl_ref, wg_ref, bg_ref, w_ref, o_ref, oc_ref, st_ref, s_scr, b_scr):
        n = pl.program_id(1)

        @pl.when(n == 0)
        def _():
            s_scr[...] = jnp.zeros_like(s_scr)

        q = q_ref[...] * (GLA_DK ** -0.5)
        k = k_ref[...]
        v = v_ref[...]
        vb = v.astype(BF16)
        _, la, _ = _gla_log_decay(gl_ref[...], wg_ref[...], bg_ref[...], n)
        b = _tri_sum(la, False)
        b_scr[...] = b
        b_last = b_scr[CHUNK - 1:CHUNK, :]
        st = s_scr[...]
        st_ref[0, 0] = st
        s_scr[...] = st * jnp.exp(b_last) + _dot(v, k * jnp.exp(b_last - b), TN)
        rowc = lax.broadcasted_iota(jnp.int32, (CHUNK, 1), 0)
        rows16 = lax.broadcasted_iota(jnp.int32, (SUB, 1), 0)
        a_tot = jnp.zeros((CHUNK, CHUNK), F32)
        for s in range(1, NSUB):
            lo = s * SUB
            bref = b_scr[lo - 1:lo, :]
            in_s = jnp.logical_and(rowc >= lo, rowc < lo + SUB)
            qh = q * jnp.exp(jnp.where(in_s, b - bref, -1e30))
            kh = k * jnp.exp(jnp.where(rowc < lo, bref - b, -1e30))
            a_tot = a_tot + _dot(qh, kh, NT)
        lane = lax.broadcasted_iota(jnp.int32, (SUB, CHUNK), 1)
        diag = []
        for s in range(NSUB):
            lo = s * SUB
            qs, bs = q[lo:lo + SUB], b[lo:lo + SUB]
            s_blk = jnp.zeros((SUB, CHUNK), F32)
            for j in range(SUB):
                r = lo + j
                e = jnp.exp(jnp.where(rows16 >= j, bs - b_scr[r:r + 1, :], -1e30))
                col = jnp.sum(qs * k_ref[r:r + 1, :] * e, axis=1, keepdims=True)
                s_blk = jnp.where(lane == r, col, s_blk)
            diag.append(s_blk)
        o = _dot(q * jnp.exp(b), st, NT) + _dot(a_tot + jnp.concatenate(diag, axis=0), vb)
        o_ref[...] = o
        oc_ref[...] = _gate_fwd(o, z_ref[...], w_ref[...]).astype(BF16)

    return pl.pallas_call(
        body, name="gla_fwd", grid=(GLA_HEADS, nc),
        in_specs=_gla_in_specs(False, nc),
        out_specs=[pl.BlockSpec((CHUNK, GLA_DV), lambda h, n: (n, h)),
                   pl.BlockSpec((CHUNK, GLA_DV), lambda h, n: (n, h)),
                   pl.BlockSpec((1, 1, GLA_DV, GLA_DK), lambda h, n: (h, n, 0, 0))],
        out_shape=[jax.ShapeDtypeStruct((rows, GLA_W), F32), jax.ShapeDtypeStruct((rows, GLA_W), BF16),
                   jax.ShapeDtypeStruct((GLA_HEADS, nc, GLA_DV, GLA_DK), F32)],
        scratch_shapes=[pltpu.VMEM((GLA_DV, GLA_DK), F32), pltpu.VMEM((CHUNK, GLA_DK), F32)],
        compiler_params=pltpu.CompilerParams(dimension_semantics=("parallel", "arbitrary")),
    )(proj, proj, proj, proj, proj, wgate, bgate, normw)


def _gla_bwd(proj, wgate, bgate, normw, o_gla, d_oc, states):
    rows = proj.shape[0]
    nc = rows // CHUNK

    def rn(n):
        return nc - 1 - n

    def body(q_ref, k_ref, v_ref, z_ref, gl_ref, wg_ref, bg_ref, w_ref, o_ref, do_ref, st_ref,
             dq_ref, dk_ref, dv_ref, dz_ref, dl_ref, dw_ref, dbg_ref,
             ds_scr, dq_scr, dk_scr, dv_scr, db_scr, b_scr, q_scr):
        n = pl.program_id(1)
        cn = rn(n)

        @pl.when(n == 0)
        def _():
            ds_scr[...] = jnp.zeros_like(ds_scr)
            dw_ref[...] = jnp.zeros_like(dw_ref)
            dbg_ref[...] = jnp.zeros_like(dbg_ref)

        q = q_ref[...] * (GLA_DK ** -0.5)
        k = k_ref[...]
        v = v_ref[...]
        vb = v.astype(BF16)
        do, dz, dw = _gate_bwd(do_ref[...], o_ref[...], z_ref[...], w_ref[...])
        dz_ref[...] = dz.astype(BF16)
        dw_ref[0] += dw
        logit, la, live = _gla_log_decay(gl_ref[...], wg_ref[...], bg_ref[...], cn)
        b = _tri_sum(la, False)
        b_scr[...] = b
        b_last = b_scr[CHUNK - 1:CHUNK, :]
        e_last = jnp.exp(b_last)
        st = st_ref[0, 0]
        g1 = ds_scr[...]
        eb = jnp.exp(b)
        qe = q * eb
        dqe = _dot(do, st)
        dq_scr[...] = dqe * eb
        db_scr[...] = dqe * qe
        ekb = jnp.exp(b_last - b)
        kdec = k * ekb
        dkdec = _dot(v, g1)
        dv_scr[...] = _dot(kdec, g1, NT)
        dk_scr[...] = dkdec * ekb
        wk = dkdec * kdec
        db_scr[...] -= wk
        dbl = jnp.sum(wk, axis=0, keepdims=True) + jnp.sum(g1 * st, axis=0, keepdims=True) * e_last
        ds_scr[...] = g1 * e_last + _dot(do, qe, TN)
        rowc = lax.broadcasted_iota(jnp.int32, (CHUNK, 1), 0)
        rows16 = lax.broadcasted_iota(jnp.int32, (SUB, 1), 0)
        da_full = _dot(do, vb, NT)
        a_tot = jnp.zeros((CHUNK, CHUNK), F32)
        for s in range(1, NSUB):
            lo = s * SUB
            bref = b_scr[lo - 1:lo, :]
            in_s = jnp.logical_and(rowc >= lo, rowc < lo + SUB)
            eq = jnp.exp(jnp.where(in_s, b - bref, -1e30))
            ek = jnp.exp(jnp.where(rowc < lo, bref - b, -1e30))
            qh = q * eq
            kh = k * ek
            a_tot = a_tot + _dot(qh, kh, NT)
            da = jnp.where(in_s, da_full, 0.0)
            dqh = _dot(da, kh)
            dkh = _dot(da, qh, TN)
            tq = dqh * qh
            tk = dkh * kh
            dq_scr[...] += dqh * eq
            dk_scr[...] += dkh * ek
            db_scr[...] += tq - tk
            db_scr[lo - 1:lo, :] += jnp.sum(tk, axis=0, keepdims=True) - jnp.sum(tq, axis=0, keepdims=True)
        dat_full = _dot(vb, do, NT)
        q_scr[...] = q
        lane = lax.broadcasted_iota(jnp.int32, (SUB, CHUNK), 1)
        diag = []
        for s in range(NSUB):
            lo = s * SUB
            qs, ks, bs = q[lo:lo + SUB], k[lo:lo + SUB], b[lo:lo + SUB]
            da_blk, dat_blk = da_full[lo:lo + SUB], dat_full[lo:lo + SUB]
            dqs = jnp.zeros((SUB, GLA_DK), F32)
            dks = jnp.zeros((SUB, GLA_DK), F32)
            dbs = jnp.zeros((SUB, GLA_DK), F32)
            s_blk = jnp.zeros((SUB, CHUNK), F32)
            for j in range(SUB):
                r = lo + j
                kj = k_ref[r:r + 1, :]
                e = jnp.exp(jnp.where(rows16 >= j, bs - b_scr[r:r + 1, :], -1e30))
                p = qs * e * kj
                s_blk = jnp.where(lane == r, jnp.sum(p, axis=1, keepdims=True), s_blk)
                dcol = jnp.sum(jnp.where(lane == r, da_blk, 0.0), axis=1, keepdims=True)
                dqs = dqs + (dcol * e) * kj
                dbs = dbs + dcol * p
            for i in range(SUB):
                r = lo + i
                e = jnp.exp(jnp.where(rows16 <= i, b_scr[r:r + 1, :] - bs, -1e30))
                drow = jnp.sum(jnp.where(lane == r, dat_blk, 0.0), axis=1, keepdims=True)
                nq = (drow * e) * q_scr[r:r + 1, :]
                dks = dks + nq
                dbs = dbs - nq * ks
            dq_scr[lo:lo + SUB, :] += dqs
            dk_scr[lo:lo + SUB, :] += dks
            db_scr[lo:lo + SUB, :] += dbs
            diag.append(s_blk)
        dv_scr[...] += _dot(a_tot + jnp.concatenate(diag, axis=0), do, TN)
        db_scr[CHUNK - 1:CHUNK, :] += dbl
        dla = _tri_sum(db_scr[...], True)
        dlogit = jnp.where(live, dla * (1.0 / GLA_TAU) * _sigmoid(-logit), 0.0)
        dl_ref[...] = dlogit
        dbg_ref[0] += jnp.sum(dlogit, axis=0, keepdims=True)
        dq_ref[...] = (dq_scr[...] * (GLA_DK ** -0.5)).astype(BF16)
        dk_ref[...] = dk_scr[...].astype(BF16)
        dv_ref[...] = dv_scr[...].astype(BF16)

    in_specs = _gla_in_specs(True, nc) + [
        pl.BlockSpec((CHUNK, GLA_DV), lambda h, n: (rn(n), h)),
        pl.BlockSpec((CHUNK, GLA_DV), lambda h, n: (rn(n), h)),
        pl.BlockSpec((1, 1, GLA_DV, GLA_DK), lambda h, n: (h, rn(n), 0, 0)),
    ]
    return pl.pallas_call(
        body, name="gla_bwd", grid=(GLA_HEADS, nc),
        in_specs=in_specs,
        out_specs=[pl.BlockSpec((CHUNK, GLA_DK), lambda h, n: (rn(n), h)),
                   pl.BlockSpec((CHUNK, GLA_DK), lambda h, n: (rn(n), h)),
                   pl.BlockSpec((CHUNK, GLA_DV), lambda h, n: (rn(n), h)),
                   pl.BlockSpec((CHUNK, GLA_DV), lambda h, n: (rn(n), h)),
                   pl.BlockSpec((CHUNK, GLA_DK), lambda h, n: (rn(n), h)),
                   pl.BlockSpec((1, 1, GLA_DV), lambda h, n: (h, 0, 0)),
                   pl.BlockSpec((1, 1, GLA_DK), lambda h, n: (h, 0, 0))],
        out_shape=[jax.ShapeDtypeStruct((rows, GLA_QK), BF16), jax.ShapeDtypeStruct((rows, GLA_QK), BF16),
                   jax.ShapeDtypeStruct((rows, GLA_W), BF16), jax.ShapeDtypeStruct((rows, GLA_W), BF16),
                   jax.ShapeDtypeStruct((rows, GLA_QK), F32),
                   jax.ShapeDtypeStruct((GLA_HEADS, 1, GLA_DV), F32),
                   jax.ShapeDtypeStruct((GLA_HEADS, 1, GLA_DK), F32)],
        scratch_shapes=[pltpu.VMEM((GLA_DV, GLA_DK), F32), pltpu.VMEM((CHUNK, GLA_DK), F32),
                        pltpu.VMEM((CHUNK, GLA_DK), F32), pltpu.VMEM((CHUNK, GLA_DV), F32),
                        pltpu.VMEM((CHUNK, GLA_DK), F32), pltpu.VMEM((CHUNK, GLA_DK), F32),
                        pltpu.VMEM((CHUNK, GLA_DK), F32)],
        compiler_params=pltpu.CompilerParams(dimension_semantics=("parallel", "arbitrary")),
    )(proj, proj, proj, proj, proj, wgate, bgate, normw, o_gla, d_oc, states)


def _adamw(name, w, g, m, v):
    rows, cols = w.shape
    tm = 8
    for cand in range(8, rows + 1, 8):
        if rows % cand == 0 and cand * cols * 4 <= 2 ** 21:
            tm = cand
    c1 = 1.0 - ADAM_B1 ** ADAM_STEP
    c2 = 1.0 - ADAM_B2 ** ADAM_STEP

    def body(w_ref, g_ref, m_ref, v_ref, d_ref, nm_ref, nv_ref):
        gv = g_ref[...]
        nm = ADAM_B1 * m_ref[...] + (1.0 - ADAM_B1) * gv
        nv = ADAM_B2 * v_ref[...] + (1.0 - ADAM_B2) * (gv * gv)
        nm_ref[...] = nm
        nv_ref[...] = nv
        d_ref[...] = -ADAM_LR * ((nm / c1) / (jnp.sqrt(nv / c2) + ADAM_EPS) + ADAM_WD * w_ref[...])

    blk = pl.BlockSpec((tm, cols), lambda i: (i, 0))
    return pl.pallas_call(
        body, name=name, grid=(rows // tm,),
        in_specs=[blk] * 4, out_specs=[blk] * 3,
        out_shape=[jax.ShapeDtypeStruct((rows, cols), F32)] * 3,
    )(w, g, m, v)


def _place():
    x, y, c = lax.axis_index("x"), lax.axis_index("y"), lax.axis_index("c")
    chips = [(1 - x, y), (x, 1 - y), (1 - x, 1 - y)]
    return x, y, c, chips


ANY = pl.BlockSpec(memory_space=pl.ANY)


def _gathered_struct(shape, dtype, kind):
    r, cc = shape
    if kind == "row":
        return jax.ShapeDtypeStruct((N_SHARD * r, cc), dtype)
    if kind == "col":
        return jax.ShapeDtypeStruct((r, N_SHARD * cc), dtype)
    return jax.ShapeDtypeStruct((N_SHARD, r, cc), dtype)


def _cast_place(name, w, kind, mine_arr, dtype):
    r, cc = w.shape
    tr = r
    for cand in (256, 128, 64, 32, 16):
        if r % cand == 0:
            tr = cand
            break
    nb = r // tr
    if kind == "row":
        o_spec = pl.BlockSpec((tr, cc), lambda i, m: (m[0] * nb + i, 0))
    elif kind == "col":
        o_spec = pl.BlockSpec((tr, cc), lambda i, m: (i, m[0]))
    else:
        o_spec = pl.BlockSpec((None, tr, cc), lambda i, m: (m[0], i, 0))
    w_spec = pl.BlockSpec((tr, cc), lambda i, m: (i, 0))

    def body(m_ref, w_ref, o_ref):
        o_ref[...] = w_ref[...].astype(o_ref.dtype)

    return pl.pallas_call(
        body, name=name,
        grid_spec=pltpu.PrefetchScalarGridSpec(
            num_scalar_prefetch=1, grid=(nb,), in_specs=[w_spec], out_specs=o_spec),
        out_shape=_gathered_struct((r, cc), dtype, kind),
    )(mine_arr, w)


def _allreduce_small(buf):
    rows, cols = buf.shape

    def body(in_ref, out_ref, sib_ref, pair_ref, far_ref, send_sems, recv_sems):
        x, y, c, chips = _place()
        sibling = (x, y, 1 - c)
        to_sib = pltpu.make_async_remote_copy(
            src_ref=in_ref, dst_ref=sib_ref, send_sem=send_sems.at[0], recv_sem=recv_sems.at[0],
            device_id=sibling, device_id_type=MESH)
        to_sib.start()
        to_sib.wait()
        pair_ref[...] = in_ref[...] + sib_ref[...]
        far = [pltpu.make_async_remote_copy(
            src_ref=pair_ref, dst_ref=far_ref.at[j], send_sem=send_sems.at[1 + j], recv_sem=recv_sems.at[1 + j],
            device_id=(*chip, c), device_id_type=MESH) for j, chip in enumerate(chips)]
        for cp in far:
            cp.start()
        for cp in far:
            cp.wait()
        out_ref[...] = (pair_ref[...] + far_ref[1]) + (far_ref[0] + far_ref[2])

    vm = pl.BlockSpec(memory_space=pltpu.VMEM)
    return pl.pallas_call(
        body, name="allreduce_small",
        in_specs=[vm], out_specs=vm,
        out_shape=jax.ShapeDtypeStruct((rows, cols), F32),
        scratch_shapes=[pltpu.VMEM((rows, cols), F32), pltpu.VMEM((rows, cols), F32),
                        pltpu.VMEM((3, rows, cols), F32),
                        pltpu.SemaphoreType.DMA((4,)), pltpu.SemaphoreType.DMA((4,))],
        compiler_params=pltpu.CompilerParams(has_side_effects=True),
    )(buf)


def _shard_window(ref, kind, shard_shape, shard, half):
    r, cc = shard_shape
    hr = r // 2
    if kind == "row":
        return ref.at[pl.ds(_mo(shard * r + half * hr, 8), hr), :]
    if kind == "col":
        return ref.at[pl.ds(_mo(half * hr, 8), hr), pl.ds(_mo(shard * cc, 128), cc)]
    if kind == "colw":
        return ref.at[pl.ds(_mo(half * hr, 8), hr), pl.ds(_mo(shard * (cc - 128), 128), cc)]
    return ref.at[shard, pl.ds(_mo(half * hr, 8), hr), :]


HBM = pl.BlockSpec(memory_space=pltpu.HBM)
SEM = pl.BlockSpec(memory_space=pltpu.SEMAPHORE)
DATAFLOW = pltpu.SideEffectType.DATAFLOW_SIDE_EFFECTING


def _in_hbm(a):
    return pltpu.with_memory_space_constraint(a, pltpu.HBM)


def _empty_hbm(shape, dtype):
    return _in_hbm(lax.empty(shape, dtype))


def _copies_start(name, bufs, n_copies, plan, carry):
    nb = len(bufs)

    def body(*refs):
        send_sems, recv_sems = refs[nb + 1], refs[nb + 2]
        for k, (src, dst, to) in enumerate(plan(refs[:nb])):
            pltpu.make_async_remote_copy(src_ref=src, dst_ref=dst, send_sem=send_sems.at[k], recv_sem=recv_sems.at[k],
                                         device_id=to, device_id_type=MESH).start()

    passed = list(bufs) + [carry]
    out = pl.pallas_call(
        body, name=name,
        in_specs=[HBM] * (nb + 1), out_specs=[SEM, SEM] + [HBM] * (nb + 1),
        out_shape=[pltpu.SemaphoreType.DMA((n_copies,)), pltpu.SemaphoreType.DMA((n_copies,))]
        + [pltpu.HBM(a.shape, a.dtype) for a in passed],
        input_output_aliases={i: 2 + i for i in range(nb + 1)},
        compiler_params=pltpu.CompilerParams(has_side_effects=DATAFLOW),
    )(*[_in_hbm(a) for a in passed])
    return out[0], out[1], list(out[2:2 + nb]), out[2 + nb]


def _copies_wait(name, send_sems, recv_sems, bufs, plan, after):
    nb = len(bufs)
    after = list(after) if isinstance(after, (list, tuple)) else [after]

    def body(*refs):
        send, recv = refs[nb], refs[nb + 1]
        for k, (src, dst, to) in enumerate(plan(refs[:nb])):
            cp = pltpu.make_async_remote_copy(src_ref=src, dst_ref=dst, send_sem=send.at[k], recv_sem=recv.at[k],
                                              device_id=to, device_id_type=MESH)
            cp.wait_send()
            cp.wait_recv()

    out = pl.pallas_call(
        body, name=name,
        in_specs=[HBM] * nb + [SEM, SEM] + [ANY] * len(after), out_specs=[HBM] * nb,
        out_shape=[pltpu.HBM(a.shape, a.dtype) for a in bufs],
        input_output_aliases={i: i for i in range(nb)},
        compiler_params=pltpu.CompilerParams(has_side_effects=DATAFLOW),
    )(*bufs, send_sems, recv_sems, *after)
    return list(out)


def _gather_ici_plan(shard_shapes, kinds):
    n_arr = len(kinds)

    def plan(refs):
        x, y, c, chips = _place()
        out = []
        for i in range(n_arr):
            w = _shard_window(refs[i], kinds[i], shard_shapes[i], 2 * x + y, c)
            out += [(w, w, (*chip, c)) for chip in chips]
        return out

    return plan


def _gather_d2d_plan(shard_shapes, kinds):
    n_arr = len(kinds)

    def plan(refs):
        x, y, c, chips = _place()
        out = []
        for i in range(n_arr):
            for chip in chips:
                w = _shard_window(refs[i], kinds[i], shard_shapes[i], 2 * chip[0] + chip[1], c)
                out.append((w, w, (x, y, 1 - c)))
        return out

    return plan


def _rs_pair_plan(kinds, shard_shapes):
    n_arr = len(kinds)

    def plan(refs):
        x, y, c, _ = _place()
        out = []
        for i in range(n_arr):
            for s in range(N_SHARD):
                out.append((_shard_window(refs[i], kinds[i], shard_shapes[i], s, 1 - c), refs[n_arr + i].at[s],
                            (x, y, 1 - c)))
        return out

    return plan


def _rs_chip_plan(n_arr):
    def plan(refs):
        x, y, c, chips = _place()
        out = []
        for i in range(n_arr):
            for j, chip in enumerate(chips):
                out.append((refs[i].at[2 * chip[0] + chip[1]], refs[n_arr + i].at[j], (*chip, c)))
        return out

    return plan


def _rs_pair_add(name, grad, got, kind, shard_shape, c):
    r, cc = shard_shape
    hr = r // 2
    tr = hr
    for cand in (256, 128, 64, 32, 16):
        if hr % cand == 0:
            tr = cand
            break
    nb = hr // tr

    def body(c_ref, g_ref, t_ref, p_ref, pb_ref):
        p = g_ref[...] + t_ref[...]
        p_ref[...] = p
        pb_ref[...] = p.astype(BF16)

    out_shape = [jax.ShapeDtypeStruct((N_SHARD, hr, cc), F32), jax.ShapeDtypeStruct((N_SHARD, hr, cc), BF16)]
    if kind == "colw":
        tiles = cc // 128
        tr = hr
        g_spec = pl.BlockSpec((tr, 128), lambda s, t, cr: (cr[0], s * (tiles - 1) + t))
        t_spec = pl.BlockSpec((None, tr, 128), lambda s, t, cr: (s, 0, t))
        return pl.pallas_call(
            body, name=name,
            grid_spec=pltpu.PrefetchScalarGridSpec(
                num_scalar_prefetch=1, grid=(N_SHARD, tiles), in_specs=[g_spec, t_spec], out_specs=[t_spec, t_spec]),
            out_shape=out_shape,
        )(c, grad, got)
    if kind == "row":
        g_spec = pl.BlockSpec((tr, cc), lambda s, i, cr: (s * 2 * nb + cr[0] * nb + i, 0))
    elif kind == "col":
        g_spec = pl.BlockSpec((tr, cc), lambda s, i, cr: (cr[0] * nb + i, s))
    else:
        g_spec = pl.BlockSpec((None, tr, cc), lambda s, i, cr: (s, cr[0] * nb + i, 0))
    t_spec = pl.BlockSpec((None, tr, cc), lambda s, i, cr: (s, i, 0))
    return pl.pallas_call(
        body, name=name,
        grid_spec=pltpu.PrefetchScalarGridSpec(
            num_scalar_prefetch=1, grid=(N_SHARD, nb),
            in_specs=[g_spec, t_spec], out_specs=[t_spec, t_spec]),
        out_shape=out_shape,
    )(c, grad, got)


def _rs_chip_add(name, pair_f32, got, shard_shape, mine_c):
    r, cc = shard_shape
    hr = r // 2
    tr = hr
    for cand in (256, 128, 64, 32, 16):
        if hr % cand == 0:
            tr = cand
            break
    nb = hr // tr

    def body(mc_ref, p_ref, t0_ref, t1_ref, t2_ref, o_ref):
        o_ref[...] = (p_ref[...] + t1_ref[...].astype(F32)) + (t0_ref[...].astype(F32) + t2_ref[...].astype(F32))

    def far(j):
        return pl.BlockSpec((None, tr, cc), lambda i, mc: (j, i, 0))

    return pl.pallas_call(
        body, name=name,
        grid_spec=pltpu.PrefetchScalarGridSpec(
            num_scalar_prefetch=1, grid=(nb,),
            in_specs=[pl.BlockSpec((None, tr, cc), lambda i, mc: (mc[0], i, 0)), far(0), far(1), far(2)],
            out_specs=pl.BlockSpec((tr, cc), lambda i, mc: (mc[1] * nb + i, 0))),
        out_shape=jax.ShapeDtypeStruct((r, cc), F32),
    )(mine_c, pair_f32, got, got, got)


def _rs_pair_share(name, halves, shard_shapes):
    n_arr = len(halves)

    def body(*refs):
        ins = refs[:n_arr]
        outs = refs[n_arr:2 * n_arr]
        send_sems, recv_sems = refs[2 * n_arr:]
        x, y, c, _ = _place()
        sibling = (x, y, 1 - c)
        cps = []
        for i in range(n_arr):
            hr = shard_shapes[i][0] // 2
            rows = pl.ds(_mo(c * hr, 8), hr)
            cp = pltpu.make_async_remote_copy(
                src_ref=outs[i].at[rows, :], dst_ref=outs[i].at[rows, :],
                send_sem=send_sems.at[i], recv_sem=recv_sems.at[i],
                device_id=sibling, device_id_type=MESH)
            cp.start()
            cps.append(cp)
        for cp in cps:
            cp.wait()

    return pl.pallas_call(
        body, name=name,
        in_specs=[ANY] * n_arr, out_specs=[ANY] * n_arr,
        out_shape=[jax.ShapeDtypeStruct(s, F32) for s in shard_shapes],
        input_output_aliases={i: i for i in range(n_arr)},
        scratch_shapes=[pltpu.SemaphoreType.DMA((n_arr,)), pltpu.SemaphoreType.DMA((n_arr,))],
        compiler_params=pltpu.CompilerParams(has_side_effects=True),
    )(*halves)


def _pack(arrays):
    flat = []
    for a in arrays:
        v = a.reshape(-1).astype(F32)
        flat.append(jnp.pad(v, (0, (-v.shape[0]) % SMALL_COLS)))
    buf = jnp.concatenate(flat).reshape(-1, SMALL_COLS)
    return jnp.pad(buf, ((0, (-buf.shape[0]) % 16), (0, 0)))


def _unpack(buf, shapes):
    out = []
    row = 0
    for s in shapes:
        size = math.prod(s)
        nrow = -(-size // SMALL_COLS)
        out.append(buf[row:row + nrow].reshape(-1)[:size].reshape(s))
        row += nrow
    return out


def kernel(x, meta, norm_ab_w, w_in_ab, ret_norm_w, s5_lam_re, s5_lam_im, s5_log_dt, s5_b_re, s5_b_im, s5_c_re, s5_c_im, s5_d, s5_w_glu, w_out_ab, norm_c_w, w_in_c, gla_w_gate, gla_b_gate, gla_norm_w, w_out_c, final_norm_w, loss_target, m_meta, m_norm_ab_w, m_w_in_ab, m_ret_norm_w, m_s5_lam_re, m_s5_lam_im, m_s5_log_dt, m_s5_b_re, m_s5_b_im, m_s5_c_re, m_s5_c_im, m_s5_d, m_s5_w_glu, m_w_out_ab, m_norm_c_w, m_w_in_c, m_gla_w_gate, m_gla_b_gate, m_gla_norm_w, m_w_out_c, m_final_norm_w, v_meta, v_norm_ab_w, v_w_in_ab, v_ret_norm_w, v_s5_lam_re, v_s5_lam_im, v_s5_log_dt, v_s5_b_re, v_s5_b_im, v_s5_c_re, v_s5_c_im, v_s5_d, v_s5_w_glu, v_w_out_ab, v_norm_c_w, v_w_in_c, v_gla_w_gate, v_gla_b_gate, v_gla_norm_w, v_w_out_c, v_final_norm_w):
    seq = x.shape[1]
    rows = seq + CHUNK
    xi, yi, ci = lax.axis_index("x"), lax.axis_index("y"), lax.axis_index("c")
    mine = 2 * xi + yi
    c_arr = jnp.reshape(ci, (1,)).astype(jnp.int32)
    mine_c = jnp.stack([mine, ci]).astype(jnp.int32)

    mine_arr = jnp.reshape(mine, (1,)).astype(jnp.int32)
    small_shard = _pack([meta, norm_c_w, gla_norm_w, gla_b_gate, gla_w_gate[0]])
    first_kinds = ["col", "stack"]
    first_shapes = [w_in_ab.shape[1:], small_shard.shape]
    first_ici = _gather_ici_plan(first_shapes, first_kinds)
    first_d2d = _gather_d2d_plan(first_shapes, first_kinds)
    f_send, f_recv, f_bufs, small_shard = _copies_start(
        "gather_first_ici_start",
        [_cast_place("place_w_in_ab", w_in_ab[0], "col", mine_arr, BF16),
         _cast_place("place_small", small_shard, "stack", mine_arr, F32)], 6, first_ici, small_shard)
    late = [("w_out_ab", w_out_ab[0]), ("w_in_c", w_in_c[0]), ("w_out_c", w_out_c[0]), ("w_glu", s5_w_glu[0])]
    late_kinds = ["row", "stack", "row", "row"]
    late_shapes = [a.shape for _, a in late]
    ici_plan = _gather_ici_plan(late_shapes, late_kinds)
    d2d_plan = _gather_d2d_plan(late_shapes, late_kinds)
    n_late = 3 * len(late)
    g_bufs = [_cast_place("place_" + nm, a, kd, mine_arr, BF16) for (nm, a), kd in zip(late, late_kinds)]
    cosf, sinf = _rope_tables(rows)
    rtab = _ret_tables()
    ab_re, ab_im, bb_re, bb_im = _s5_discretize(s5_lam_re[0], s5_lam_im[0], s5_log_dt[0], s5_b_re[0], s5_b_im[0])
    ab = (ab_re, ab_im)
    bd_b = (_bdiag_in(bb_re), _bdiag_in(bb_im))
    bd_c = (_bdiag_out(s5_c_re[0]), _bdiag_out(s5_c_im[0]))
    f_bufs = _copies_wait("gather_first_ici_wait", f_send, f_recv, f_bufs, first_ici,
                          [cosf, sinf, bd_b[0], bd_b[1], bd_c[0], bd_c[1]] + g_bufs + list(rtab))
    f_send, f_recv, f_bufs, cosf = _copies_start("gather_first_d2d_start", f_bufs, 6, first_d2d, cosf)
    wab, small_all = _copies_wait("gather_first_d2d_wait", f_send, f_recv, f_bufs, first_d2d, cosf)
    g_send, g_recv, g_bufs, wab = _copies_start("gather_late_ici_start", g_bufs, n_late, ici_plan, wab)
    q4 = D_MODEL // N_SHARD
    g4 = GLA_QK // N_SHARD
    parts = [_unpack(small_all[j], [(N_META, q4), (1, q4), (1, q4), (1, g4), (GLA_RANK, g4)]) for j in range(N_SHARD)]
    meta_f, norm_c_f, gla_norm_f, bgate_f, wgate_f = [jnp.concatenate([p[i] for p in parts], axis=1) for i in range(5)]
    wgate_pad = jnp.pad(wgate_f, ((0, 128 - GLA_RANK), (0, 0)))

    h0, hn0 = _embed_norm(x[0], meta_f, norm_ab_w)

    tm = _row_tile(rows, 1408)
    tmk = _row_tile(rows, 1408)
    proj0 = _matmul("in_proj_ab", hn0, wab, NN, rows, IN_AB, D_MODEL, tm=tm, tn=512, tk=D_MODEL)
    o_ret, o_a, ret_states = _ret_fwd(proj0, cosf, sinf, rtab, ret_norm_w)
    g_bufs = _copies_wait("gather_late_ici_wait", g_send, g_recv, g_bufs, ici_plan, o_a)
    g_send, g_recv, g_bufs, proj0 = _copies_start("gather_late_d2d_start", g_bufs, n_late, d2d_plan, proj0)
    y_s5, g_s5, s5_er, s5_ei = _s5_fwd(proj0, ab, bd_b, bd_c, s5_d)
    wout_ab, wc_st, wout_c, wglu = _copies_wait("gather_late_d2d_wait", g_send, g_recv, g_bufs, d2d_plan, g_s5)
    wc = jnp.concatenate([wc_st[j] for j in range(N_SHARD)] + [jnp.zeros((D_MODEL, IN_C_PAD - IN_C), BF16)], axis=1)
    zb_blk = (2 * RET_QK + 2 * RET_W + S5_W) // 512

    def glu_out(acc, gv, z):
        return gv.astype(F32) * _sigmoid(acc) * (z * _sigmoid(z))

    t_glu = _matmul("glu", g_s5, wglu, NN, rows, S5_W, S5_W, tm=tm, tn=512, tk=S5_W)
    o_b = _matmul("glu_out", g_s5, wglu, NN, rows, S5_W, S5_W, tm=tm, tn=512, tk=S5_W, out_dtype=BF16,
                  extras=[(g_s5, (tm, 512), lambda i, j, kk: (i, j)),
                          (proj0, (tm, 512), lambda i, j, kk: (i, zb_blk + j))],
                  epilogue=glu_out)
    h1 = _matmul("out_proj_ab", None, None, NN, rows, D_MODEL, OUT_AB, tm=tm, tn=512, tk=1024,
                 segs=[(o_a, (0, 0), wout_ab, (0, 0), RET_W, 1024),
                       (o_b, (0, 0), wout_ab, (RET_W // 1024, 0), S5_W, 1024)],
                 extras=[(h0, (tm, 512), lambda i, j, kk: (i, j))], epilogue=lambda acc, r: acc + r)

    hn1 = _rms_fwd("norm_c", h1, norm_c_f)
    proj1 = _matmul("in_proj_c", hn1, wc, NN, rows, IN_C_PAD, D_MODEL, tm=tm, tn=896, tk=D_MODEL)
    o_gla, o_c, gla_states = _gla_fwd(proj1, wgate_pad, bgate_f, gla_norm_f)
    h2 = _matmul("out_proj_c", o_c, wout_c, NN, rows, D_MODEL, GLA_W, tm=tm, tn=512, tk=GLA_W,
                 extras=[(h1, (tm, 512), lambda i, j, kk: (i, j))], epilogue=lambda acc, r: acc + r)
    loss_dev, dh2, d_final = _final_loss(h2, final_norm_w.reshape(1, D_MODEL), loss_target[0])

    g_wout_c = _matmul("d_w_out_c", o_c, dh2, TN, GLA_W, D_MODEL, rows, tm=1024, tn=1024, tk=tmk)
    d_oc = _matmul("d_o_c", dh2, wout_c, NT, rows, GLA_W, D_MODEL, tm=tm, tn=512, tk=1024)
    dq1, dk1, dv1, dz1, dlogit, d_gla_norm, d_bgate = _gla_bwd(proj1, wgate_pad, bgate_f, gla_norm_f, o_gla, d_oc, gla_states)
    gl_blk = (2 * GLA_QK + 2 * GLA_W) // 128
    dgl = _matmul("d_g_low", dlogit, wgate_pad, NT, rows, 128, GLA_QK, tm=tm, tn=128, tk=GLA_QK, out_dtype=BF16)
    g_wgate = _matmul("d_w_gate", proj1, dlogit, TN, 128, GLA_QK, rows, tm=128, tn=GLA_QK, tk=tmk, a_off=(0, gl_blk))
    dproj1 = jnp.concatenate([dq1, dk1, dv1, dz1, dgl], axis=1)
    g_wc = _matmul("d_w_in_c", hn1, dproj1, TN, D_MODEL, IN_C_PAD, rows, tm=1024, tn=896, tk=tmk)
    dhn1 = _matmul("d_hn1", dproj1, wc, NT, rows, D_MODEL, IN_C_PAD, tm=tm, tn=512, tk=896)
    dh1, d_norm_c = _rms_bwd("norm_c_bwd", dhn1, h1, norm_c_f, dh2)

    g_wout_ab = _matmul("d_w_out_ab_a", o_a, dh1, TN, RET_W, D_MODEL, rows, tm=1024, tn=1024, tk=tmk,
                        out_shape=jax.ShapeDtypeStruct((OUT_AB, D_MODEL), F32))
    g_wout_ab = _matmul("d_w_out_ab_b", o_b, dh1, TN, S5_W, D_MODEL, rows, tm=1024, tn=1024, tk=tmk,
                        into=(g_wout_ab, RET_W // 1024, 0))
    dmix = _matmul("d_mix", dh1, wout_ab, NT, rows, OUT_AB, D_MODEL, tm=tm, tn=512, tk=1024)
    dq0, dk0, dv0, dza, d_ret_norm = _ret_bwd(proj0, cosf, sinf, rtab, ret_norm_w, o_ret, dmix, ret_states)
    dzb, dt_glu, dg_direct = _s5_gate_bwd(dmix, g_s5, t_glu, proj0)
    g_wglu = _matmul("d_w_glu", g_s5, dt_glu, TN, S5_W, S5_W, rows, tm=1024, tn=1024, tk=tmk)
    dy_s5 = _matmul("d_y_s5", dt_glu, wglu, NT, rows, S5_W, S5_W, tm=tm, tn=512, tk=S5_W,
                    extras=[(dg_direct, (tm, 512), lambda i, j, kk: (i, j)),
                            (y_s5, (tm, 512), lambda i, j, kk: (i, j))],
                    epilogue=lambda acc, dg, yv: (acc + dg) * _gelu_grad(yv))
    wc_cols = IN_C // N_SHARD
    wc_win = (wc_cols // 128 + 1) * 128
    rs1_names = ["w_out_ab", "w_in_c", "w_out_c", "w_glu"]
    rs1_kinds = ["row", "colw", "row", "row"]
    rs1_shapes = [w_out_ab.shape[1:], (D_MODEL, wc_win), w_out_c.shape[1:], s5_w_glu.shape[1:]]
    rs1_plan = _rs_pair_plan(rs1_kinds, rs1_shapes)
    rs1_land = [_empty_hbm((N_SHARD, r // 2, cc), F32) for (r, cc) in rs1_shapes]
    p_send, p_recv, p_bufs, dy_s5 = _copies_start("rs1_pair_start", [g_wout_ab, g_wc, g_wout_c, g_wglu] + rs1_land,
                                                  N_SHARD * 4, rs1_plan, dy_s5)
    du, dbr_d, dbi_d, dcr_d, dci_d, dar_p, dai_p, dd_p = _s5_bwd(proj0, dy_s5, ab, bd_b, bd_c, s5_d, (s5_er, s5_ei))
    p_bufs = _copies_wait("rs1_pair_wait", p_send, p_recv, p_bufs, rs1_plan, du)
    rs1_pairs = [_rs_pair_add("rs_pair_add_" + nm, g, t, kd, ss, c_arr)
                 for nm, g, t, kd, ss in zip(rs1_names, p_bufs[:4], p_bufs[4:], rs1_kinds, rs1_shapes)]
    rs1_chip_plan = _rs_chip_plan(4)
    rs1_land2 = [_empty_hbm((3, r // 2, cc), BF16) for (r, cc) in rs1_shapes]
    dproj0 = jnp.concatenate([dq0, dk0, dv0, dza, du, dzb], axis=1)
    c_send, c_recv, c_bufs, dproj0 = _copies_start("rs1_chip_start", [p[1] for p in rs1_pairs] + rs1_land2, 12,
                                                   rs1_chip_plan, dproj0)
    g_wab = _matmul("d_w_in_ab", hn0, dproj0, TN, D_MODEL, IN_AB, rows, tm=1024, tn=1024, tk=tmk)
    rs2_shapes = [w_in_ab.shape[1:]]
    rs2_plan = _rs_pair_plan(["col"], rs2_shapes)
    rs2_land = [_empty_hbm((N_SHARD, rs2_shapes[0][0] // 2, rs2_shapes[0][1]), F32)]
    q_send, q_recv, q_bufs, dproj0 = _copies_start("rs2_pair_start", [g_wab] + rs2_land, N_SHARD, rs2_plan, dproj0)
    dhn0 = _matmul("d_hn0", dproj0, wab, NT, rows, D_MODEL, IN_AB, tm=tm, tn=512, tk=2048)
    grad_x, d_meta, d_norm_ab = _rms_bwd_embed(dhn0, h0, norm_ab_w, dh1)
    c_bufs = _copies_wait("rs1_chip_wait", c_send, c_recv, c_bufs, rs1_chip_plan, grad_x)
    grad_x = grad_x[None]
    rs1_halves = [_rs_chip_add("rs_chip_add_" + nm, p[0], t, ss, mine_c)
                  for nm, p, t, ss in zip(rs1_names, rs1_pairs, c_bufs[4:], rs1_shapes)]
    g_w_out_ab, g_w_in_c, g_w_out_c, g_w_glu = _rs_pair_share("rs1_pair_share", rs1_halves, rs1_shapes)
    g_w_in_c = lax.dynamic_slice(g_w_in_c, (0, (wc_cols % 128) * mine), (D_MODEL, wc_cols))
    q_bufs = _copies_wait("rs2_pair_wait", q_send, q_recv, q_bufs, rs2_plan, g_w_glu)
    rs2_pair = _rs_pair_add("rs_pair_add_w_in_ab", q_bufs[0], q_bufs[1], "col", rs2_shapes[0], c_arr)
    rs2_chip_plan = _rs_chip_plan(1)
    rs2_land2 = [_empty_hbm((3, rs2_shapes[0][0] // 2, rs2_shapes[0][1]), BF16)]

    d_ab_re = jnp.sum(dar_p, axis=1).reshape(S5_G, S5_P)
    d_ab_im = jnp.sum(dai_p, axis=1).reshape(S5_G, S5_P)
    small_local = [loss_dev, d_meta, d_norm_ab, d_ret_norm.reshape(1, RET_W), d_ab_re, d_ab_im,
                   _bdiag_in_extract(dbr_d), _bdiag_in_extract(dbi_d),
                   _bdiag_out_extract(dcr_d), _bdiag_out_extract(dci_d),
                   jnp.sum(dd_p, axis=1).reshape(1, S5_W), d_norm_c, g_wgate[:GLA_RANK],
                   d_bgate.reshape(1, GLA_QK), d_gla_norm.reshape(1, GLA_W), d_final]
    small_shapes = [a.shape for a in small_local]
    summed_buf = _allreduce_small(_pack(small_local))
    r_send, r_recv, r_bufs, summed_buf = _copies_start("rs2_chip_start", [rs2_pair[1]] + rs2_land2, 3, rs2_chip_plan,
                                                       summed_buf)
    summed = _unpack(summed_buf, small_shapes)
    (loss, g_meta_f, g_norm_ab, g_ret_norm, g_ab_re, g_ab_im, g_bb_re, g_bb_im, g_c_re, g_c_im, g_d,
     g_norm_c_f, g_wgate_f, g_bgate_f, g_gla_norm_f, g_final) = summed
    _, s5_vjp = jax.vjp(_s5_discretize, s5_lam_re[0], s5_lam_im[0], s5_log_dt[0], s5_b_re[0], s5_b_im[0])
    g_lam_re, g_lam_im, g_log_dt, g_b_re, g_b_im = s5_vjp((g_ab_re, g_ab_im, g_bb_re, g_bb_im))

    def take(a, width):
        return lax.dynamic_slice_in_dim(a, mine * width, width, axis=1)

    grads = {
        "meta": take(g_meta_f, q4), "norm_ab_w": g_norm_ab, "ret_norm_w": g_ret_norm,
        "s5_lam_re": g_lam_re[None], "s5_lam_im": g_lam_im[None], "s5_log_dt": g_log_dt[None],
        "s5_b_re": g_b_re[None], "s5_b_im": g_b_im[None], "s5_c_re": g_c_re[None], "s5_c_im": g_c_im[None],
        "s5_d": g_d, "s5_w_glu": g_w_glu[None], "w_out_ab": g_w_out_ab[None], "norm_c_w": take(g_norm_c_f, q4),
        "w_in_c": g_w_in_c[None], "gla_w_gate": take(g_wgate_f, g4)[None], "gla_b_gate": take(g_bgate_f, g4),
        "gla_norm_w": take(g_gla_norm_f, q4), "w_out_c": g_w_out_c[None], "final_norm_w": g_final.reshape(D_MODEL),
    }
    weights = dict(meta=meta, norm_ab_w=norm_ab_w, w_in_ab=w_in_ab, ret_norm_w=ret_norm_w, s5_lam_re=s5_lam_re,
                   s5_lam_im=s5_lam_im, s5_log_dt=s5_log_dt, s5_b_re=s5_b_re, s5_b_im=s5_b_im, s5_c_re=s5_c_re,
                   s5_c_im=s5_c_im, s5_d=s5_d, s5_w_glu=s5_w_glu, w_out_ab=w_out_ab, norm_c_w=norm_c_w,
                   w_in_c=w_in_c, gla_w_gate=gla_w_gate, gla_b_gate=gla_b_gate, gla_norm_w=gla_norm_w,
                   w_out_c=w_out_c, final_norm_w=final_norm_w)
    m_in = dict(meta=m_meta, norm_ab_w=m_norm_ab_w, w_in_ab=m_w_in_ab, ret_norm_w=m_ret_norm_w,
                s5_lam_re=m_s5_lam_re, s5_lam_im=m_s5_lam_im, s5_log_dt=m_s5_log_dt, s5_b_re=m_s5_b_re,
                s5_b_im=m_s5_b_im, s5_c_re=m_s5_c_re, s5_c_im=m_s5_c_im, s5_d=m_s5_d, s5_w_glu=m_s5_w_glu,
                w_out_ab=m_w_out_ab, norm_c_w=m_norm_c_w, w_in_c=m_w_in_c, gla_w_gate=m_gla_w_gate,
                gla_b_gate=m_gla_b_gate, gla_norm_w=m_gla_norm_w, w_out_c=m_w_out_c, final_norm_w=m_final_norm_w)
    v_in = dict(meta=v_meta, norm_ab_w=v_norm_ab_w, w_in_ab=v_w_in_ab, ret_norm_w=v_ret_norm_w,
                s5_lam_re=v_s5_lam_re, s5_lam_im=v_s5_lam_im, s5_log_dt=v_s5_log_dt, s5_b_re=v_s5_b_re,
                s5_b_im=v_s5_b_im, s5_c_re=v_s5_c_re, s5_c_im=v_s5_c_im, s5_d=v_s5_d, s5_w_glu=v_s5_w_glu,
                w_out_ab=v_w_out_ab, norm_c_w=v_norm_c_w, w_in_c=v_w_in_c, gla_w_gate=v_gla_w_gate,
                gla_b_gate=v_gla_b_gate, gla_norm_w=v_gla_norm_w, w_out_c=v_w_out_c, final_norm_w=v_final_norm_w)
    order = list(weights)
    big_names = ["s5_w_glu", "w_out_ab", "w_in_c", "w_out_c", "w_in_ab"]
    small_names = [nm for nm in order if nm not in big_names]
    delta, new_m, new_v = {}, {}, {}

    def big_update(nm):
        shp = weights[nm].shape
        d2, m2, v2 = _adamw("adamw_" + nm, weights[nm][0], grads[nm][0], m_in[nm][0], v_in[nm][0])
        delta[nm], new_m[nm], new_v[nm] = d2.reshape(shp), m2.reshape(shp), v2.reshape(shp)

    for nm in big_names[:-1]:
        big_update(nm)
    sshapes = [weights[nm].shape for nm in small_names]
    d2, m2, v2 = _adamw("adamw_small", _pack([weights[nm] for nm in small_names]),
                        _pack([grads[nm] for nm in small_names]), _pack([m_in[nm] for nm in small_names]),
                        _pack([v_in[nm] for nm in small_names]))
    for nm, dd, mm, vv in zip(small_names, _unpack(d2, sshapes), _unpack(m2, sshapes), _unpack(v2, sshapes)):
        delta[nm], new_m[nm], new_v[nm] = dd, mm, vv
    r_bufs = _copies_wait("rs2_chip_wait", r_send, r_recv, r_bufs, rs2_chip_plan,
                          [v2] + [new_v[nm] for nm in big_names[:-1]])
    rs2_half = _rs_chip_add("rs_chip_add_w_in_ab", rs2_pair[0], r_bufs[1], rs2_shapes[0], mine_c)
    grads["w_in_ab"] = _rs_pair_share("rs2_pair_share", [rs2_half], rs2_shapes)[0][None]
    big_update("w_in_ab")
    grads = {nm: grads[nm].reshape(weights[nm].shape) for nm in order}
    return (loss.reshape(()), grad_x, *[grads[nm] for nm in order], *[delta[nm] for nm in order],
            *[new_m[nm] for nm in order], *[new_v[nm] for nm in order])
```

```python
import functools
import math

import jax
import jax.numpy as jnp
from jax import lax
from jax.experimental import pallas as pl
from jax.experimental.pallas import tpu as pltpu

F32 = jnp.float32
BF16 = jnp.bfloat16
MESH = pl.DeviceIdType.MESH

D_MODEL = 2048
N_META = 16
CHUNK = 128
SUB = 16
NSUB = CHUNK // SUB
PAD = CHUNK - N_META
EPS = 1e-6

RET_HEADS = 8
RET_DK = 128
RET_DV = 256
RET_QK = RET_HEADS * RET_DK
RET_W = RET_HEADS * RET_DV
ROPE_BASE = 10000.0

S5_W = 1024
S5_GH = 16
S5_G = S5_W // S5_GH
S5_P = 64
S5_TG = 8
S5_NT = S5_G // S5_TG
S5_TU = S5_TG * S5_GH
S5_TS = S5_TG * S5_P
S5_FWD_TILES = 2
S5_BWD_TILES = 1

GLA_HEADS = 4
GLA_DK = 256
GLA_DV = 512
GLA_QK = GLA_HEADS * GLA_DK
GLA_W = GLA_HEADS * GLA_DV
GLA_RANK = 16
GLA_TAU = 16.0

IN_AB = 2 * RET_QK + 2 * RET_W + 2 * S5_W
OUT_AB = RET_W + S5_W
IN_C = 2 * GLA_QK + 2 * GLA_W + GLA_RANK
IN_C_PAD = 2 * GLA_QK + 2 * GLA_W + 128

ADAM_LR = 0.001
ADAM_B1 = 0.9
ADAM_B2 = 0.999
ADAM_EPS = 1e-08
ADAM_WD = 0.01
ADAM_STEP = 10

N_SHARD = 4
SMALL_COLS = 512

NN = (((1,), (0,)), ((), ()))
NT = (((1,), (1,)), ((), ()))
TN = (((0,), (0,)), ((), ()))


def _dot(a, b, dims=NN):
    return lax.dot_general(a.astype(BF16), b.astype(BF16), dims, preferred_element_type=F32)


def _mo(v, m):
    return v if isinstance(v, int) else pl.multiple_of(v, m)


def _sigmoid(x):
    return 1.0 / (1.0 + jnp.exp(-x))


def _row_tile(rows, cap):
    n = rows // CHUNK
    best = 1
    for d in range(1, n + 1):
        if n % d == 0 and d * CHUNK <= cap:
            best = d
    return best * CHUNK


def _col_tile(cols, cap):
    n = cols // 128
    best = 1
    for d in range(1, n + 1):
        if n % d == 0 and d * 128 <= cap:
            best = d
    return best * 128


def _matmul(name, a, b, dims, m, n, k, *, tm, tn, tk, out_dtype=F32, a_off=(0, 0), b_off=(0, 0),
            extras=(), epilogue=None, out_shape=None, out_spec=None, segs=None, into=None):
    if segs is None:
        segs = [(a, a_off, b, b_off, k, tk)]
    assert m % tm == 0 and n % tn == 0, (name, m, n, tm, tn)
    starts, counts = [], []
    nk = 0
    for (_, _, _, _, ks, tks) in segs:
        assert ks % tks == 0, (name, ks, tks)
        starts.append(nk)
        counts.append(ks // tks)
        nk += ks // tks
    in_specs, operands = [], []
    for s, (sa, (ar, ac), sb, (br, bc), _, tks) in enumerate(segs):
        def kpos(kk, st=starts[s], cnt=counts[s]):
            return jnp.clip(kk - st, 0, cnt - 1) if len(segs) > 1 else kk

        if dims == NN:
            a_spec = pl.BlockSpec((tm, tks), lambda i, j, kk, p=kpos, r=ar, c=ac: (i + r, p(kk) + c))
            b_spec = pl.BlockSpec((tks, tn), lambda i, j, kk, p=kpos, r=br, c=bc: (p(kk) + r, j + c))
        elif dims == NT:
            a_spec = pl.BlockSpec((tm, tks), lambda i, j, kk, p=kpos, r=ar, c=ac: (i + r, p(kk) + c))
            b_spec = pl.BlockSpec((tn, tks), lambda i, j, kk, p=kpos, r=br, c=bc: (j + r, p(kk) + c))
        else:
            a_spec = pl.BlockSpec((tks, tm), lambda i, j, kk, p=kpos, r=ar, c=ac: (p(kk) + r, i + c))
            b_spec = pl.BlockSpec((tks, tn), lambda i, j, kk, p=kpos, r=br, c=bc: (p(kk) + r, j + c))
        in_specs += [a_spec, b_spec]
        operands += [sa, sb]
    n_seg = len(segs)
    n_extra = len(extras)
    if out_shape is None:
        out_shape = jax.ShapeDtypeStruct((m, n), out_dtype)

    def body(*refs):
        e_refs = refs[2 * n_seg:2 * n_seg + n_extra]
        n_in = 2 * n_seg + n_extra + (1 if into is not None else 0)
        o_ref = refs[n_in]
        if nk == 1:
            part = _dot(refs[0][...], refs[1][...], dims)
            if epilogue is not None:
                part = epilogue(part, *[e[...] for e in e_refs])
            o_ref[...] = part.astype(o_ref.dtype)
            return
        acc_ref = refs[n_in + 1]
        kk = pl.program_id(2)

        @pl.when(kk == 0)
        def _():
            acc_ref[...] = jnp.zeros_like(acc_ref)

        if n_seg == 1:
            acc_ref[...] += _dot(refs[0][...], refs[1][...], dims)
        else:
            for s in range(n_seg):
                @pl.when(jnp.logical_and(kk >= starts[s], kk < starts[s] + counts[s]))
                def _(s=s):
                    acc_ref[...] += _dot(refs[2 * s][...], refs[2 * s + 1][...], dims)

        @pl.when(kk == nk - 1)
        def _():
            acc = acc_ref[...]
            if epilogue is not None:
                acc = epilogue(acc, *[e[...] for e in e_refs])
            o_ref[...] = acc.astype(o_ref.dtype)

    if out_spec is None:
        out_spec = pl.BlockSpec((tm, tn), lambda i, j, kk: (i, j))
    in_specs += [pl.BlockSpec(bs, im) for (_, bs, im) in extras]
    operands += [e for (e, _, _) in extras]
    aliases = {}
    if into is not None:
        dest, ro, co = into
        out_shape = jax.ShapeDtypeStruct(dest.shape, dest.dtype)
        out_spec = pl.BlockSpec((tm, tn), lambda i, j, kk: (i + ro, j + co))
        aliases = {len(operands): 0}
        in_specs.append(ANY)
        operands.append(dest)
    return pl.pallas_call(
        body, name=name, grid=(m // tm, n // tn, nk),
        in_specs=in_specs, out_specs=out_spec, out_shape=out_shape, input_output_aliases=aliases,
        scratch_shapes=[] if nk == 1 else [pltpu.VMEM((tm, tn), F32)],
        compiler_params=pltpu.CompilerParams(dimension_semantics=("parallel", "parallel", "arbitrary")),
    )(*operands)


def _rms_fwd(name, h, w):
    rows, d = h.shape
    tm = _row_tile(rows, 512)

    def body(h_ref, w_ref, o_ref):
        x = h_ref[...]
        r = lax.rsqrt(jnp.mean(x * x, axis=-1, keepdims=True) + EPS)
        o_ref[...] = (x * r * w_ref[...]).astype(BF16)

    return pl.pallas_call(
        body, name=name, grid=(rows // tm,),
        in_specs=[pl.BlockSpec((tm, d), lambda i: (i, 0)), pl.BlockSpec((1, d), lambda i: (0, 0))],
        out_specs=pl.BlockSpec((tm, d), lambda i: (i, 0)),
        out_shape=jax.ShapeDtypeStruct((rows, d), BF16),
    )(h, w)


def _rms_bwd(name, dhn, h, w, dres):
    rows, d = h.shape
    tm = _row_tile(rows, 384)

    def body(g_ref, h_ref, w_ref, r_ref, dh_ref, dw_ref):
        i = pl.program_id(0)
        x = h_ref[...]
        r = lax.rsqrt(jnp.mean(x * x, axis=-1, keepdims=True) + EPS)
        xh = x * r
        g = g_ref[...]
        gw = g * w_ref[...]
        dh_ref[...] = r_ref[...] + r * (gw - xh * jnp.mean(gw * xh, axis=-1, keepdims=True))

        @pl.when(i == 0)
        def _():
            dw_ref[...] = jnp.zeros_like(dw_ref)

        dw_ref[...] += jnp.sum(g * xh, axis=0, keepdims=True)

    return pl.pallas_call(
        body, name=name, grid=(rows // tm,),
        in_specs=[pl.BlockSpec((tm, d), lambda i: (i, 0)), pl.BlockSpec((tm, d), lambda i: (i, 0)),
                  pl.BlockSpec((1, d), lambda i: (0, 0)), pl.BlockSpec((tm, d), lambda i: (i, 0))],
        out_specs=[pl.BlockSpec((tm, d), lambda i: (i, 0)), pl.BlockSpec((1, d), lambda i: (0, 0))],
        out_shape=[jax.ShapeDtypeStruct((rows, d), F32), jax.ShapeDtypeStruct((1, d), F32)],
    )(dhn, h, w, dres)


def _embed_norm(x, meta, w):
    seq, d = x.shape
    rows = seq + CHUNK

    def body(x_ref, m_ref, w_ref, h_ref, o_ref):
        i = pl.program_id(0)

        def emit(h):
            h_ref[...] = h
            r = lax.rsqrt(jnp.mean(h * h, axis=-1, keepdims=True) + EPS)
            o_ref[...] = (h * r * w_ref[...]).astype(BF16)

        @pl.when(i == 0)
        def _():
            emit(jnp.concatenate([jnp.zeros((PAD, d), F32), m_ref[...]], axis=0))

        @pl.when(i > 0)
        def _():
            emit(x_ref[...])

    blk = pl.BlockSpec((CHUNK, d), lambda i: (i, 0))
    return pl.pallas_call(
        body, name="embed_norm_ab", grid=(rows // CHUNK,),
        in_specs=[pl.BlockSpec((CHUNK, d), lambda i: (jnp.maximum(i - 1, 0), 0)),
                  pl.BlockSpec((N_META, d), lambda i: (0, 0)), pl.BlockSpec((1, d), lambda i: (0, 0))],
        out_specs=[blk, blk],
        out_shape=[jax.ShapeDtypeStruct((rows, d), F32), jax.ShapeDtypeStruct((rows, d), BF16)],
    )(x, meta, w)


def _rms_bwd_embed(dhn, h, w, dres):
    rows, d = h.shape
    seq = rows - CHUNK

    def body(g_ref, h_ref, w_ref, r_ref, gx_ref, gm_ref, dw_ref):
        i = pl.program_id(0)
        x = h_ref[...]
        r = lax.rsqrt(jnp.mean(x * x, axis=-1, keepdims=True) + EPS)
        xh = x * r
        g = g_ref[...]
        gw = g * w_ref[...]
        dh = r_ref[...] + r * (gw - xh * jnp.mean(gw * xh, axis=-1, keepdims=True))

        @pl.when(i == 0)
        def _():
            dw_ref[...] = jnp.zeros_like(dw_ref)
            gm_ref[...] = dh[PAD:]

        @pl.when(i > 0)
        def _():
            gx_ref[...] = dh

        dw_ref[...] += jnp.sum(g * xh, axis=0, keepdims=True)

    blk = pl.BlockSpec((CHUNK, d), lambda i: (i, 0))
    return pl.pallas_call(
        body, name="norm_ab_bwd", grid=(rows // CHUNK,),
        in_specs=[blk, blk, pl.BlockSpec((1, d), lambda i: (0, 0)), blk],
        out_specs=[pl.BlockSpec((CHUNK, d), lambda i: (jnp.maximum(i - 1, 0), 0)),
                   pl.BlockSpec((N_META, d), lambda i: (0, 0)), pl.BlockSpec((1, d), lambda i: (0, 0))],
        out_shape=[jax.ShapeDtypeStruct((seq, d), F32), jax.ShapeDtypeStruct((N_META, d), F32),
                   jax.ShapeDtypeStruct((1, d), F32)],
    )(dhn, h, w, dres)


def _final_loss(h2, w, target):
    rows, d = h2.shape

    def body(h_ref, w_ref, t_ref, loss_ref, dh_ref, dw_ref):
        i = pl.program_id(0)

        @pl.when(i == 0)
        def _():
            loss_ref[...] = jnp.zeros_like(loss_ref)
            dw_ref[...] = jnp.zeros_like(dw_ref)
            dh_ref[...] = jnp.zeros_like(dh_ref)

        @pl.when(i > 0)
        def _():
            x = h_ref[...]
            r = lax.rsqrt(jnp.mean(x * x, axis=-1, keepdims=True) + EPS)
            xh = x * r
            wv = w_ref[...]
            err = xh * wv - t_ref[...]
            loss_ref[...] += 0.5 * jnp.sum(jnp.mean(err * err, axis=-1, keepdims=True), axis=0, keepdims=True)
            g = err * (1.0 / d)
            gw = g * wv
            dh_ref[...] = r * (gw - xh * jnp.mean(gw * xh, axis=-1, keepdims=True))
            dw_ref[...] += jnp.sum(g * xh, axis=0, keepdims=True)

    return pl.pallas_call(
        body, name="final_loss", grid=(rows // CHUNK,),
        in_specs=[pl.BlockSpec((CHUNK, d), lambda i: (i, 0)), pl.BlockSpec((1, d), lambda i: (0, 0)),
                  pl.BlockSpec((CHUNK, d), lambda i: (jnp.maximum(i - 1, 0), 0))],
        out_specs=[pl.BlockSpec((1, 1), lambda i: (0, 0)), pl.BlockSpec((CHUNK, d), lambda i: (i, 0)),
                   pl.BlockSpec((1, d), lambda i: (0, 0))],
        out_shape=[jax.ShapeDtypeStruct((1, 1), F32), jax.ShapeDtypeStruct((rows, d), F32),
                   jax.ShapeDtypeStruct((1, d), F32)],
    )(h2, w, target)


def _gate_fwd(o, z, w):
    rs = lax.rsqrt(jnp.mean(o * o, axis=-1, keepdims=True) + EPS)
    return o * rs * w * (z * _sigmoid(z))


def _gate_bwd(dout, o, z, w):
    rs = lax.rsqrt(jnp.mean(o * o, axis=-1, keepdims=True) + EPS)
    yn = o * rs
    sg = _sigmoid(z)
    sil = z * sg
    dsil = sg * (1.0 + z * (1.0 - sg))
    dz = dout * yn * w * dsil
    dyn = dout * w * sil
    dw = jnp.sum(dout * yn * sil, axis=0, keepdims=True)
    do = rs * (dyn - yn * jnp.mean(dyn * yn, axis=-1, keepdims=True))
    return do, dz, dw


def _rope(t, cosf, sinf):
    return t * cosf + pltpu.roll(t, RET_DK // 2, 1) * sinf


def _rope_t(d, cosf, sinf):
    return d * cosf + pltpu.roll(d * sinf, RET_DK // 2, 1)


def _ret_tables():
    log_g = jnp.log1p(-jnp.exp2(-5.0 - jnp.arange(RET_HEADS, dtype=F32)))
    idx = jnp.arange(CHUNK, dtype=F32)
    diff = idx[:, None] - idx[None, :]
    decay = jnp.where(diff >= 0, jnp.exp(log_g[:, None, None] * jnp.maximum(diff, 0.0)), 0.0)
    kw = jnp.exp(log_g[:, None] * (CHUNK - 1 - idx))
    qw = jnp.exp(log_g[:, None] * (idx + 1.0))
    gch = jnp.exp(log_g * CHUNK)
    kw = jnp.broadcast_to(kw[:, :, None], (RET_HEADS, CHUNK, RET_DK))
    qw = jnp.broadcast_to(qw[:, :, None], (RET_HEADS, CHUNK, RET_DK))
    gch = jnp.broadcast_to(gch[:, None, None], (RET_HEADS, 1, RET_DV))
    return decay, kw, qw, gch


def _rope_tables(rows):
    pos = jnp.arange(rows, dtype=F32) - float(PAD)
    inv_freq = jnp.power(ROPE_BASE, -jnp.arange(0, RET_DK, 2, dtype=F32) / RET_DK)
    ang = pos[:, None] * inv_freq[None, :]
    cos, sin = jnp.cos(ang), jnp.sin(ang)
    return jnp.concatenate([cos, cos], axis=1), jnp.concatenate([-sin, sin], axis=1)


RET_HB = 8
RET_QB = RET_HB * RET_DK
RET_VB = RET_HB * RET_DV


def _ret_in_specs(rev, nc):
    def cn(n):
        return (nc - 1 - n) if rev else n
    kb = RET_QK // RET_QB
    vb = 2 * RET_QK // RET_VB
    zb = (2 * RET_QK + RET_W) // RET_VB
    return [
        pl.BlockSpec((CHUNK, RET_QB), lambda h, n: (cn(n), h)),
        pl.BlockSpec((CHUNK, RET_QB), lambda h, n: (cn(n), kb + h)),
        pl.BlockSpec((CHUNK, RET_VB), lambda h, n: (cn(n), vb + h)),
        pl.BlockSpec((CHUNK, RET_VB), lambda h, n: (cn(n), zb + h)),
        pl.BlockSpec((CHUNK, RET_DK), lambda h, n: (cn(n), 0)),
        pl.BlockSpec((CHUNK, RET_DK), lambda h, n: (cn(n), 0)),
        pl.BlockSpec((RET_HB, CHUNK, CHUNK), lambda h, n: (h, 0, 0)),
        pl.BlockSpec((RET_HB, CHUNK, RET_DK), lambda h, n: (h, 0, 0)),
        pl.BlockSpec((RET_HB, CHUNK, RET_DK), lambda h, n: (h, 0, 0)),
        pl.BlockSpec((RET_HB, 1, RET_DV), lambda h, n: (h, 0, 0)),
        pl.BlockSpec((1, RET_VB), lambda h, n: (0, h)),
    ]


def _ret_fwd(proj, cosf, sinf, tables, normw):
    rows = proj.shape[0]
    nc = rows // CHUNK
    decay, kw, qw, gch = tables

    def body(q_ref, k_ref, v_ref, z_ref, cos_ref, sin_ref, dm_ref, kw_ref, qw_ref, g_ref, w_ref,
             o_ref, oa_ref, st_ref, s_scr):
        n = pl.program_id(1)

        @pl.when(n == 0)
        def _():
            s_scr[...] = jnp.zeros_like(s_scr)

        cosv, sinv = cos_ref[...], sin_ref[...]
        for hh in range(RET_HB):
            qc = slice(hh * RET_DK, (hh + 1) * RET_DK)
            vc = slice(hh * RET_DV, (hh + 1) * RET_DV)
            q = _rope(q_ref[:, qc], cosv, sinv)
            k = _rope(k_ref[:, qc], cosv, sinv) * (RET_DK ** -0.5)
            v = v_ref[:, vc]
            s = s_scr[hh]
            st_ref[hh, 0] = s.astype(BF16)
            a = _dot(q, k, NT) * dm_ref[hh]
            o = _dot(a, v) + _dot(q * qw_ref[hh], s)
            s_scr[hh] = s * g_ref[hh] + _dot(k * kw_ref[hh], v, TN)
            o_ref[:, vc] = o
            oa_ref[:, vc] = _gate_fwd(o, z_ref[:, vc], w_ref[:, vc]).astype(BF16)

    return pl.pallas_call(
        body, name="ret_fwd", grid=(RET_HEADS // RET_HB, nc),
        in_specs=_ret_in_specs(False, nc),
        out_specs=[pl.BlockSpec((CHUNK, RET_VB), lambda h, n: (n, h)),
                   pl.BlockSpec((CHUNK, RET_VB), lambda h, n: (n, h)),
                   pl.BlockSpec((RET_HB, 1, RET_DK, RET_DV), lambda h, n: (h, n, 0, 0))],
        out_shape=[jax.ShapeDtypeStruct((rows, RET_W), F32), jax.ShapeDtypeStruct((rows, RET_W), BF16),
                   jax.ShapeDtypeStruct((RET_HEADS, nc, RET_DK, RET_DV), BF16)],
        scratch_shapes=[pltpu.VMEM((RET_HB, RET_DK, RET_DV), F32)],
        compiler_params=pltpu.CompilerParams(dimension_semantics=("parallel", "arbitrary")),
    )(proj, proj, proj, proj, cosf, sinf, decay, kw, qw, gch, normw)


def _ret_bwd(proj, cosf, sinf, tables, normw, o_ret, dmix, states):
    assert RET_HB == RET_HEADS
    rows = proj.shape[0]
    nc = rows // CHUNK
    decay, kw, qw, gch = tables
    ret_cols = 2 * RET_QK + 2 * RET_W

    def rn(n):
        return nc - 1 - n

    def body(q_ref, k_ref, v_ref, z_ref, cos_ref, sin_ref, dm_ref, kw_ref, qw_ref, g_ref, w_ref,
             o_ref, do_ref, st_ref, dp_ref, dw_ref, ds_scr):
        n = pl.program_id(1)
        dq_ref = dp_ref.at[:, 0:RET_QK]
        dk_ref = dp_ref.at[:, RET_QK:2 * RET_QK]
        dv_ref = dp_ref.at[:, 2 * RET_QK:2 * RET_QK + RET_W]
        dz_ref = dp_ref.at[:, 2 * RET_QK + RET_W:ret_cols]

        @pl.when(n == 0)
        def _():
            ds_scr[...] = jnp.zeros_like(ds_scr)
            dw_ref[...] = jnp.zeros_like(dw_ref)

        cosv, sinv = cos_ref[...], sin_ref[...]
        for hh in range(RET_HB):
            qc = slice(hh * RET_DK, (hh + 1) * RET_DK)
            vc = slice(hh * RET_DV, (hh + 1) * RET_DV)
            q = _rope(q_ref[:, qc], cosv, sinv)
            k = _rope(k_ref[:, qc], cosv, sinv) * (RET_DK ** -0.5)
            v = v_ref[:, vc]
            do, dz, dw = _gate_bwd(do_ref[:, vc], o_ref[:, vc], z_ref[:, vc], w_ref[:, vc])
            dz_ref[:, vc] = dz.astype(BF16)
            dw_ref[hh] += dw
            dm = dm_ref[hh]
            s = st_ref[hh, 0]
            g1 = ds_scr[hh]
            p = _dot(q, k, NT) * dm
            kwv = k * kw_ref[hh]
            qwv = q * qw_ref[hh]
            dp = _dot(do, v, NT)
            da = dp * dm
            dv = _dot(p, do, TN) + _dot(kwv, g1)
            dq = _dot(da, k) + _dot(do, s, NT) * qw_ref[hh]
            dk = _dot(da, q, TN) + _dot(v, g1, NT) * kw_ref[hh]
            ds_scr[hh] = g1 * g_ref[hh] + _dot(qwv, do, TN)
            dv_ref[:, vc] = dv.astype(BF16)
            dq_ref[:, qc] = _rope_t(dq, cosv, sinv).astype(BF16)
            dk_ref[:, qc] = _rope_t(dk * (RET_DK ** -0.5), cosv, sinv).astype(BF16)

    in_specs = _ret_in_specs(True, nc) + [
        pl.BlockSpec((CHUNK, RET_VB), lambda h, n: (rn(n), h)),
        pl.BlockSpec((CHUNK, RET_VB), lambda h, n: (rn(n), h)),
        pl.BlockSpec((RET_HB, 1, RET_DK, RET_DV), lambda h, n: (h, rn(n), 0, 0)),
    ]
    return pl.pallas_call(
        body, name="ret_bwd", grid=(RET_HEADS // RET_HB, nc),
        in_specs=in_specs,
        out_specs=[pl.BlockSpec((CHUNK, ret_cols), lambda h, n: (rn(n), 0)),
                   pl.BlockSpec((RET_HB, 1, RET_DV), lambda h, n: (h, 0, 0))],
        out_shape=[jax.ShapeDtypeStruct((rows, IN_AB), BF16), jax.ShapeDtypeStruct((RET_HEADS, 1, RET_DV), F32)],
        scratch_shapes=[pltpu.VMEM((RET_HB, RET_DK, RET_DV), F32)],
        compiler_params=pltpu.CompilerParams(dimension_semantics=("parallel", "arbitrary")),
    )(proj, proj, proj, proj, cosf, sinf, decay, kw, qw, gch, normw, o_ret, dmix, states)


def _s5_discretize(lam_re, lam_im, log_dt, b_re, b_im):
    dt = jnp.exp(log_dt)[:, None]
    mag = jnp.exp(lam_re * dt)
    ab_re, ab_im = mag * jnp.cos(lam_im * dt), mag * jnp.sin(lam_im * dt)
    den = lam_re * lam_re + lam_im * lam_im
    nr, ni = ab_re - 1.0, ab_im
    f_re = (nr * lam_re + ni * lam_im) / den
    f_im = (ni * lam_re - nr * lam_im) / den
    bb_re = f_re[..., None] * b_re - f_im[..., None] * b_im
    bb_im = f_re[..., None] * b_im + f_im[..., None] * b_re
    return ab_re, ab_im, bb_re, bb_im


def _bdiag_in(bb):
    t = bb.reshape(S5_NT, S5_TG, S5_P, S5_GH).transpose(0, 1, 3, 2)
    eye = jnp.eye(S5_TG, dtype=bb.dtype)
    full = t[:, :, :, None, :] * eye[None, :, None, :, None]
    return full.reshape(S5_NT, S5_TU, S5_TS)


def _bdiag_in_extract(dense):
    t = dense.reshape(S5_NT, S5_TG, S5_GH, S5_TG, S5_P)
    diag = jnp.stack([t[:, g, :, g, :] for g in range(S5_TG)], axis=1)
    return diag.transpose(0, 1, 3, 2).reshape(S5_G, S5_P, S5_GH)


def _bdiag_out(c):
    t = c.reshape(S5_NT, S5_TG, S5_GH, S5_P).transpose(0, 1, 3, 2)
    eye = jnp.eye(S5_TG, dtype=c.dtype)
    full = t[:, :, :, None, :] * eye[None, :, None, :, None]
    return full.reshape(S5_NT, S5_TS, S5_TU)


def _bdiag_out_extract(dense):
    t = dense.reshape(S5_NT, S5_TG, S5_P, S5_TG, S5_GH)
    diag = jnp.stack([t[:, g, :, g, :] for g in range(S5_TG)], axis=1)
    return diag.transpose(0, 1, 3, 2).reshape(S5_G, S5_GH, S5_P)


def _cmul(ar, ai, br, bi):
    return ar * br - ai * bi, ar * bi + ai * br


S5_SEG = 8
S5_STEPS = CHUNK // S5_SEG


def _seg_perm(x):
    c = x.shape[1]
    return jnp.swapaxes(x.reshape(S5_SEG, S5_STEPS, c), 0, 1).reshape(CHUNK, c)


def _seg_unperm(x):
    c = x.shape[1]
    return jnp.swapaxes(x.reshape(S5_STEPS, S5_SEG, c), 0, 1).reshape(CHUNK, c)


def _rows(x, p):
    return x[p * S5_SEG:(p + 1) * S5_SEG]


def _s5_tables(ar, ai, tr_scr, ti_scr, wfr_scr, wfi_scr, wbr_scr, wbi_scr):
    row = lax.broadcasted_iota(jnp.int32, (S5_SEG, 1), 0)
    a8r = jnp.broadcast_to(ar, (S5_SEG, S5_TS))
    a8i = jnp.broadcast_to(ai, (S5_SEG, S5_TS))
    pr, pi = a8r, a8i
    for p in range(S5_STEPS):
        tr_scr[p * S5_SEG:(p + 1) * S5_SEG, :] = pr
        ti_scr[p * S5_SEG:(p + 1) * S5_SEG, :] = pi
        if p < S5_STEPS - 1:
            pr, pi = _cmul(pr, pi, a8r, a8i)
    wr, wi = pr, pi
    sh = 1
    while sh < S5_SEG:
        keep = row >= sh
        sr = jnp.where(keep, pltpu.roll(wr, sh, 0), 1.0)
        si = jnp.where(keep, pltpu.roll(wi, sh, 0), 0.0)
        wr, wi = _cmul(wr, wi, sr, si)
        sh *= 2
    wfr_scr[...] = wr
    wfi_scr[...] = wi
    wr, wi = pr, -pi
    sh = 1
    while sh < S5_SEG:
        keep = row < S5_SEG - sh
        sr = jnp.where(keep, pltpu.roll(wr, S5_SEG - sh, 0), 1.0)
        si = jnp.where(keep, pltpu.roll(wi, S5_SEG - sh, 0), 0.0)
        wr, wi = _cmul(wr, wi, sr, si)
        sh *= 2
    wbr_scr[...] = wr
    wbi_scr[...] = wi


def _seg_scan(vr, vi, ar, ai, tr_scr, ti_scr, wr_scr, wi_scr, c0r, c0i, down):
    row = lax.broadcasted_iota(jnp.int32, (S5_SEG, 1), 0)
    sgn = 1.0 if down else -1.0
    order = list(range(S5_STEPS)) if down else list(range(S5_STEPS - 1, -1, -1))
    xr, xi = _rows(vr, order[0]), _rows(vi, order[0])
    loc = {order[0]: (xr, xi)}
    for p in order[1:]:
        mr, mi = _cmul(ar, sgn * ai, xr, xi)
        xr, xi = mr + _rows(vr, p), mi + _rows(vi, p)
        loc[p] = (xr, xi)
    last = S5_STEPS - 1
    mr, mi = tr_scr[last * S5_SEG:(last + 1) * S5_SEG, :], sgn * ti_scr[last * S5_SEG:(last + 1) * S5_SEG, :]
    er, ei = xr, xi
    sh = 1
    while sh < S5_SEG:
        if down:
            keep = row >= sh
            sr, si = pltpu.roll(er, sh, 0), pltpu.roll(ei, sh, 0)
        else:
            keep = row < S5_SEG - sh
            sr, si = pltpu.roll(er, S5_SEG - sh, 0), pltpu.roll(ei, S5_SEG - sh, 0)
        pr, pi = _cmul(mr, mi, jnp.where(keep, sr, 0.0), jnp.where(keep, si, 0.0))
        er, ei = er + pr, ei + pi
        mr, mi = _cmul(mr, mi, mr, mi)
        sh *= 2
    pr, pi = _cmul(wr_scr[...], wi_scr[...], c0r, c0i)
    er, ei = er + pr, ei + pi
    if down:
        nr = jnp.where(row == 0, c0r, pltpu.roll(er, 1, 0))
        ni = jnp.where(row == 0, c0i, pltpu.roll(ei, 1, 0))
    else:
        nr = jnp.where(row == S5_SEG - 1, c0r, pltpu.roll(er, S5_SEG - 1, 0))
        ni = jnp.where(row == S5_SEG - 1, c0i, pltpu.roll(ei, S5_SEG - 1, 0))
    out_r, out_i = [], []
    for p in range(S5_STEPS):
        q = p if down else S5_STEPS - 1 - p
        pr, pi = _cmul(tr_scr[q * S5_SEG:(q + 1) * S5_SEG, :], sgn * ti_scr[q * S5_SEG:(q + 1) * S5_SEG, :], nr, ni)
        out_r.append(loc[p][0] + pr)
        out_i.append(loc[p][1] + pi)
    return jnp.concatenate(out_r, axis=0), jnp.concatenate(out_i, axis=0), (nr, ni), (er, ei)


def _gelu(y):
    c = math.sqrt(2.0 / math.pi)
    return 0.5 * y * (1.0 + jnp.tanh(c * (y + 0.044715 * y * y * y)))


def _gelu_grad(y):
    c = math.sqrt(2.0 / math.pi)
    th = jnp.tanh(c * (y + 0.044715 * y * y * y))
    return 0.5 * (1.0 + th) + 0.5 * y * (1.0 - th * th) * c * (1.0 + 3.0 * 0.044715 * y * y)


def _s5_fwd(proj, ab, bd_b, bd_c, dvec):
    rows = proj.shape[0]
    nc = rows // CHUNK
    tps = S5_FWD_TILES
    ubw = tps * S5_TU
    ub = (2 * RET_QK + 2 * RET_W) // ubw
    ab_re, ab_im = ab
    bre, bim = bd_b
    cre, cim = bd_c

    def body(u_ref, ar_ref, ai_ref, bre_ref, bim_ref, cre_ref, cim_ref, d_ref,
             y_ref, g_ref, er_ref, ei_ref, tr_scr, ti_scr, wfr_scr, wfi_scr, wbr_scr, wbi_scr,
             cr_scr, ci_scr, er_scr, ei_scr):
        n = pl.program_id(1)
        for tt in range(tps):
            cols = slice(tt * S5_TU, (tt + 1) * S5_TU)
            ar, ai = ar_ref[tt], ai_ref[tt]
            trs, tis, wfr, wfi = tr_scr.at[tt], ti_scr.at[tt], wfr_scr.at[tt], wfi_scr.at[tt]

            @pl.when(n == 0)
            def _(tt=tt, ar=ar, ai=ai, trs=trs, tis=tis, wfr=wfr, wfi=wfi):
                _s5_tables(ar, ai, trs, tis, wfr, wfi, wbr_scr.at[tt], wbi_scr.at[tt])
                cr_scr[tt] = jnp.zeros((S5_SEG, S5_TS), F32)
                ci_scr[tt] = jnp.zeros((S5_SEG, S5_TS), F32)

            u = _seg_perm(u_ref[:, cols])
            c0r, c0i = cr_scr[tt], ci_scr[tt]
            er_ref[tt, 0] = c0r
            ei_ref[tt, 0] = c0i
            xr, xi, _, (er, ei) = _seg_scan(_dot(u, bre_ref[tt]), _dot(u, bim_ref[tt]), ar, ai, trs, tis,
                                            wfr, wfi, c0r, c0i, True)
            er_scr[tt] = er
            ei_scr[tt] = ei
            cr_scr[tt] = jnp.broadcast_to(er_scr[tt, S5_SEG - 1:S5_SEG, :], (S5_SEG, S5_TS))
            ci_scr[tt] = jnp.broadcast_to(ei_scr[tt, S5_SEG - 1:S5_SEG, :], (S5_SEG, S5_TS))
            y = _seg_unperm(_dot(xr, cre_ref[tt]) - _dot(xi, cim_ref[tt]) + d_ref[:, cols] * u)
            y_ref[:, cols] = y
            g_ref[:, cols] = _gelu(y).astype(BF16)

    vec = pl.BlockSpec((tps, 1, S5_TS), lambda t, n: (t, 0, 0))
    return pl.pallas_call(
        body, name="s5_fwd", grid=(S5_NT // tps, nc),
        in_specs=[pl.BlockSpec((CHUNK, ubw), lambda t, n: (n, ub + t)), vec, vec,
                  pl.BlockSpec((tps, S5_TU, S5_TS), lambda t, n: (t, 0, 0)),
                  pl.BlockSpec((tps, S5_TU, S5_TS), lambda t, n: (t, 0, 0)),
                  pl.BlockSpec((tps, S5_TS, S5_TU), lambda t, n: (t, 0, 0)),
                  pl.BlockSpec((tps, S5_TS, S5_TU), lambda t, n: (t, 0, 0)),
                  pl.BlockSpec((1, ubw), lambda t, n: (0, t))],
        out_specs=[pl.BlockSpec((CHUNK, ubw), lambda t, n: (n, t)),
                   pl.BlockSpec((CHUNK, ubw), lambda t, n: (n, t)),
                   pl.BlockSpec((tps, 1, 8, S5_TS), lambda t, n: (t, n, 0, 0)),
                   pl.BlockSpec((tps, 1, 8, S5_TS), lambda t, n: (t, n, 0, 0))],
        out_shape=[jax.ShapeDtypeStruct((rows, S5_W), F32), jax.ShapeDtypeStruct((rows, S5_W), BF16),
                   jax.ShapeDtypeStruct((S5_NT, nc, 8, S5_TS), F32),
                   jax.ShapeDtypeStruct((S5_NT, nc, 8, S5_TS), F32)],
        scratch_shapes=[pltpu.VMEM((tps, CHUNK, S5_TS), F32) for _ in range(2)]
        + [pltpu.VMEM((tps, S5_SEG, S5_TS), F32) for _ in range(8)],
        compiler_params=pltpu.CompilerParams(dimension_semantics=("parallel", "arbitrary")),
    )(proj, ab_re.reshape(S5_NT, 1, S5_TS), ab_im.reshape(S5_NT, 1, S5_TS), bre, bim, cre, cim, dvec)


def _s5_bwd(proj, dy, ab, bd_b, bd_c, dvec, entry, dproj):
    rows = proj.shape[0]
    nc = rows // CHUNK
    tps = S5_BWD_TILES
    ubw = tps * S5_TU
    ub = (2 * RET_QK + 2 * RET_W) // ubw
    ab_re, ab_im = ab
    bre, bim = bd_b
    cre, cim = bd_c
    er, ei = entry

    def rn(n):
        return nc - 1 - n

    def body(u_ref, dy_ref, ar_ref, ai_ref, bre_ref, bim_ref, cre_ref, cim_ref, d_ref, er_ref, ei_ref, dp_ref,
             du_ref, dbr_ref, dbi_ref, dcr_ref, dci_ref, dar_ref, dai_ref, dd_ref,
             tr_scr, ti_scr, wfr_scr, wfi_scr, wbr_scr, wbi_scr, gr_scr, gi_scr, er_scr, ei_scr):
        n = pl.program_id(1)

        @pl.when(n == 0)
        def _():
            gr_scr[...] = jnp.zeros_like(gr_scr)
            gi_scr[...] = jnp.zeros_like(gi_scr)
            for r in (dbr_ref, dbi_ref, dcr_ref, dci_ref, dar_ref, dai_ref, dd_ref):
                r[...] = jnp.zeros_like(r)

        for tt in range(tps):
            cols = slice(tt * S5_TU, (tt + 1) * S5_TU)
            ar, ai = ar_ref[tt], ai_ref[tt]
            trs, tis = tr_scr.at[tt], ti_scr.at[tt]

            @pl.when(n == 0)
            def _(tt=tt, ar=ar, ai=ai, trs=trs, tis=tis):
                _s5_tables(ar, ai, trs, tis, wfr_scr.at[tt], wfi_scr.at[tt], wbr_scr.at[tt], wbi_scr.at[tt])

            u = _seg_perm(u_ref[:, cols])
            dy = _seg_perm(dy_ref[:, cols])
            xr, xi, (pr, pi), _ = _seg_scan(_dot(u, bre_ref[tt]), _dot(u, bim_ref[tt]), ar, ai, trs, tis,
                                            wfr_scr.at[tt], wfi_scr.at[tt], er_ref[tt, 0], ei_ref[tt, 0], True)
            dcr_ref[tt] += _dot(xr, dy, TN)
            dci_ref[tt] -= _dot(xi, dy, TN)
            gr, gi, _, (er, ei) = _seg_scan(_dot(dy, cre_ref[tt], NT), -_dot(dy, cim_ref[tt], NT), ar, ai, trs, tis,
                                            wbr_scr.at[tt], wbi_scr.at[tt], gr_scr[tt], gi_scr[tt], False)
            er_scr[tt] = er
            ei_scr[tt] = ei
            gr_scr[tt] = jnp.broadcast_to(er_scr[tt, 0:1, :], (S5_SEG, S5_TS))
            gi_scr[tt] = jnp.broadcast_to(ei_scr[tt, 0:1, :], (S5_SEG, S5_TS))
            xpr = jnp.concatenate([pr, xr[:CHUNK - S5_SEG]], axis=0)
            xpi = jnp.concatenate([pi, xi[:CHUNK - S5_SEG]], axis=0)
            dar_ref[tt] += jnp.sum((xpr * gr + xpi * gi).reshape(S5_STEPS, S5_SEG, S5_TS), axis=0)
            dai_ref[tt] += jnp.sum((xpr * gi - xpi * gr).reshape(S5_STEPS, S5_SEG, S5_TS), axis=0)
            dbr_ref[tt] += _dot(u, gr, TN)
            dbi_ref[tt] += _dot(u, gi, TN)
            dd_ref[tt] += jnp.sum((dy * u).reshape(S5_STEPS, S5_SEG, S5_TU), axis=0)
            du = dy * d_ref[:, cols] + _dot(gr, bre_ref[tt], NT) + _dot(gi, bim_ref[tt], NT)
            du_ref[:, cols] = _seg_unperm(du).astype(BF16)

    vec = pl.BlockSpec((tps, 1, S5_TS), lambda t, n: (t, 0, 0))
    acc_b = pl.BlockSpec((tps, S5_TU, S5_TS), lambda t, n: (t, 0, 0))
    acc_c = pl.BlockSpec((tps, S5_TS, S5_TU), lambda t, n: (t, 0, 0))
    acc_a = pl.BlockSpec((tps, 8, S5_TS), lambda t, n: (t, 0, 0))
    ent = pl.BlockSpec((tps, 1, 8, S5_TS), lambda t, n: (t, rn(n), 0, 0))
    return pl.pallas_call(
        body, name="s5_bwd", grid=(S5_NT // tps, nc),
        in_specs=[pl.BlockSpec((CHUNK, ubw), lambda t, n: (rn(n), ub + t)),
                  pl.BlockSpec((CHUNK, ubw), lambda t, n: (rn(n), t)), vec, vec,
                  acc_b, acc_b, acc_c, acc_c, pl.BlockSpec((1, ubw), lambda t, n: (0, t)), ent, ent, ANY],
        out_specs=[pl.BlockSpec((CHUNK, ubw), lambda t, n: (rn(n), ub + t)), acc_b, acc_b, acc_c, acc_c, acc_a, acc_a,
                   pl.BlockSpec((tps, 8, S5_TU), lambda t, n: (t, 0, 0))],
        input_output_aliases={11: 0},
        out_shape=[jax.ShapeDtypeStruct(dproj.shape, BF16),
                   jax.ShapeDtypeStruct((S5_NT, S5_TU, S5_TS), F32), jax.ShapeDtypeStruct((S5_NT, S5_TU, S5_TS), F32),
                   jax.ShapeDtypeStruct((S5_NT, S5_TS, S5_TU), F32), jax.ShapeDtypeStruct((S5_NT, S5_TS, S5_TU), F32),
                   jax.ShapeDtypeStruct((S5_NT, 8, S5_TS), F32), jax.ShapeDtypeStruct((S5_NT, 8, S5_TS), F32),
                   jax.ShapeDtypeStruct((S5_NT, 8, S5_TU), F32)],
        scratch_shapes=[pltpu.VMEM((tps, CHUNK, S5_TS), F32) for _ in range(2)]
        + [pltpu.VMEM((tps, S5_SEG, S5_TS), F32) for _ in range(8)],
        compiler_params=pltpu.CompilerParams(dimension_semantics=("parallel", "arbitrary")),
    )(proj, dy, ab_re.reshape(S5_NT, 1, S5_TS), ab_im.reshape(S5_NT, 1, S5_TS), bre, bim, cre, cim, dvec, er, ei,
      dproj)


def _s5_gate_bwd(dmix, g, t, proj, dproj):
    rows = g.shape[0]
    tm = _row_tile(rows, 384)
    ob = RET_W // S5_W
    zb = (2 * RET_QK + 2 * RET_W + S5_W) // S5_W

    def body(do_ref, g_ref, t_ref, z_ref, dp_ref, dz_ref, dt_ref, dg_ref):
        do = do_ref[...]
        gv = g_ref[...].astype(F32)
        z = z_ref[...]
        st = _sigmoid(t_ref[...])
        sg = _sigmoid(z)
        os5 = gv * st
        dz_ref[...] = (do * os5 * sg * (1.0 + z * (1.0 - sg))).astype(BF16)
        dos = do * z * sg
        dt_ref[...] = (dos * gv * st * (1.0 - st)).astype(BF16)
        dg_ref[...] = dos * st

    blk = pl.BlockSpec((tm, S5_W), lambda i: (i, 0))
    return pl.pallas_call(
        body, name="s5_gate_bwd", grid=(rows // tm,),
        in_specs=[pl.BlockSpec((tm, S5_W), lambda i: (i, ob)), blk, blk,
                  pl.BlockSpec((tm, S5_W), lambda i: (i, zb)), ANY],
        out_specs=[pl.BlockSpec((tm, S5_W), lambda i: (i, zb)), blk, blk],
        out_shape=[jax.ShapeDtypeStruct(dproj.shape, BF16), jax.ShapeDtypeStruct((rows, S5_W), BF16),
                   jax.ShapeDtypeStruct((rows, S5_W), F32)],
        input_output_aliases={4: 0},
    )(dmix, g, t, proj, dproj)


def _split3(x):
    hi = x.astype(BF16)
    r = x - hi.astype(F32)
    mid = r.astype(BF16)
    lo = (r - mid.astype(F32)).astype(BF16)
    return hi, mid, lo


def _tri_sum(x, upper):
    i = lax.broadcasted_iota(jnp.int32, (CHUNK, CHUNK), 0)
    j = lax.broadcasted_iota(jnp.int32, (CHUNK, CHUNK), 1)
    tri = jnp.where((j >= i) if upper else (j <= i), 1.0, 0.0).astype(BF16)
    hi, mid, lo = _split3(x)
    return _dot(tri, lo) + _dot(tri, mid) + _dot(tri, hi)


def _gla_log_decay(gl, wg, bg, n):
    logit = _dot(gl, wg) + bg
    la = (jnp.minimum(logit, 0.0) - jnp.log(1.0 + jnp.exp(-jnp.abs(logit)))) * (1.0 / GLA_TAU)
    row = lax.broadcasted_iota(jnp.int32, (CHUNK, 1), 0)
    live = jnp.logical_or(n > 0, row >= PAD)
    return logit, jnp.where(live, la, 0.0), live


def _gla_in_specs(rev, nc):
    def cn(n):
        return (nc - 1 - n) if rev else n
    kb = GLA_QK // GLA_DK
    vb = 2 * GLA_QK // GLA_DV
    zb = (2 * GLA_QK + GLA_W) // GLA_DV
    gb = (2 * GLA_QK + 2 * GLA_W) // 128
    return [
        pl.BlockSpec((CHUNK, GLA_DK), lambda h, n: (cn(n), h)),
        pl.BlockSpec((CHUNK, GLA_DK), lambda h, n: (cn(n), kb + h)),
        pl.BlockSpec((CHUNK, GLA_DV), lambda h, n: (cn(n), vb + h)),
        pl.BlockSpec((CHUNK, GLA_DV), lambda h, n: (cn(n), zb + h)),
        pl.BlockSpec((CHUNK, 128), lambda h, n: (cn(n), gb)),
        pl.BlockSpec((128, GLA_DK), lambda h, n: (0, h)),
        pl.BlockSpec((1, GLA_DK), lambda h, n: (0, h)),
        pl.BlockSpec((1, GLA_DV), lambda h, n: (0, h)),
    ]


def _gla_fwd(proj, wgate, bgate, normw):
    rows = proj.shape[0]
    nc = rows // CHUNK

    def body(q_ref, k_ref, v_ref, z_ref, gl_ref, wg_ref, bg_ref, w_ref, o_ref, oc_ref, st_ref, s_scr, b_scr):
        n = pl.program_id(1)

        @pl.when(n == 0)
        def _():
            s_scr[...] = jnp.zeros_like(s_scr)

        q = q_ref[...] * (GLA_DK ** -0.5)
        k = k_ref[...]
        v = v_ref[...]
        vb = v.astype(BF16)
        _, la, _ = _gla_log_decay(gl_ref[...], wg_ref[...], bg_ref[...], n)
        b = _tri_sum(la, False)
        b_scr[...] = b
        b_last = b_scr[CHUNK - 1:CHUNK, :]
        st = s_scr[...]
        st_ref[0, 0] = st
        s_scr[...] = st * jnp.exp(b_last) + _dot(v, k * jnp.exp(b_last - b), TN)
        rowc = lax.broadcasted_iota(jnp.int32, (CHUNK, 1), 0)
        rows16 = lax.broadcasted_iota(jnp.int32, (SUB, 1), 0)
        a_tot = jnp.zeros((CHUNK, CHUNK), F32)
        for s in range(1, NSUB):
            lo = s * SUB
            bref = b_scr[lo - 1:lo, :]
            in_s = jnp.logical_and(rowc >= lo, rowc < lo + SUB)
            qh = q * jnp.exp(jnp.where(in_s, b - bref, -1e30))
            kh = k * jnp.exp(jnp.where(rowc < lo, bref - b, -1e30))
            a_tot = a_tot + _dot(qh, kh, NT)
        lane = lax.broadcasted_iota(jnp.int32, (SUB, CHUNK), 1)
        diag = []
        for s in range(NSUB):
            lo = s * SUB
            qs, bs = q[lo:lo + SUB], b[lo:lo + SUB]
            s_blk = jnp.zeros((SUB, CHUNK), F32)
            for j in range(SUB):
                r = lo + j
                e = jnp.exp(jnp.where(rows16 >= j, bs - b_scr[r:r + 1, :], -1e30))
                col = jnp.sum(qs * k_ref[r:r + 1, :] * e, axis=1, keepdims=True)
                s_blk = jnp.where(lane == r, col, s_blk)
            diag.append(s_blk)
        o = _dot(q * jnp.exp(b), st, NT) + _dot(a_tot + jnp.concatenate(diag, axis=0), vb)
        o_ref[...] = o
        oc_ref[...] = _gate_fwd(o, z_ref[...], w_ref[...]).astype(BF16)

    return pl.pallas_call(
        body, name="gla_fwd", grid=(GLA_HEADS, nc),
        in_specs=_gla_in_specs(False, nc),
        out_specs=[pl.BlockSpec((CHUNK, GLA_DV), lambda h, n: (n, h)),
                   pl.BlockSpec((CHUNK, GLA_DV), lambda h, n: (n, h)),
                   pl.BlockSpec((1, 1, GLA_DV, GLA_DK), lambda h, n: (h, n, 0, 0))],
        out_shape=[jax.ShapeDtypeStruct((rows, GLA_W), F32), jax.ShapeDtypeStruct((rows, GLA_W), BF16),
                   jax.ShapeDtypeStruct((GLA_HEADS, nc, GLA_DV, GLA_DK), F32)],
        scratch_shapes=[pltpu.VMEM((GLA_DV, GLA_DK), F32), pltpu.VMEM((CHUNK, GLA_DK), F32)],
        compiler_params=pltpu.CompilerParams(dimension_semantics=("parallel", "arbitrary")),
    )(proj, proj, proj, proj, proj, wgate, bgate, normw)


def _gla_bwd(proj, wgate, bgate, normw, o_gla, d_oc, states):
    rows = proj.shape[0]
    nc = rows // CHUNK

    def rn(n):
        return nc - 1 - n

    def body(q_ref, k_ref, v_ref, z_ref, gl_ref, wg_ref, bg_ref, w_ref, o_ref, do_ref, st_ref,
             dq_ref, dk_ref, dv_ref, dz_ref, dl_ref, dw_ref, dbg_ref,
             ds_scr, dq_scr, dk_scr, dv_scr, db_scr, b_scr, q_scr):
        n = pl.program_id(1)
        cn = rn(n)

        @pl.when(n == 0)
        def _():
            ds_scr[...] = jnp.zeros_like(ds_scr)
            dw_ref[...] = jnp.zeros_like(dw_ref)
            dbg_ref[...] = jnp.zeros_like(dbg_ref)

        q = q_ref[...] * (GLA_DK ** -0.5)
        k = k_ref[...]
        v = v_ref[...]
        vb = v.astype(BF16)
        do, dz, dw = _gate_bwd(do_ref[...], o_ref[...], z_ref[...], w_ref[...])
        dz_ref[...] = dz.astype(BF16)
        dw_ref[0] += dw
        logit, la, live = _gla_log_decay(gl_ref[...], wg_ref[...], bg_ref[...], cn)
        b = _tri_sum(la, False)
        b_scr[...] = b
        b_last = b_scr[CHUNK - 1:CHUNK, :]
        e_last = jnp.exp(b_last)
        st = st_ref[0, 0]
        g1 = ds_scr[...]
        eb = jnp.exp(b)
        qe = q * eb
        dqe = _dot(do, st)
        dq_scr[...] = dqe * eb
        db_scr[...] = dqe * qe
        ekb = jnp.exp(b_last - b)
        kdec = k * ekb
        dkdec = _dot(v, g1)
        dv_scr[...] = _dot(kdec, g1, NT)
        dk_scr[...] = dkdec * ekb
        wk = dkdec * kdec
        db_scr[...] -= wk
        dbl = jnp.sum(wk, axis=0, keepdims=True) + jnp.sum(g1 * st, axis=0, keepdims=True) * e_last
        ds_scr[...] = g1 * e_last + _dot(do, qe, TN)
        rowc = lax.broadcasted_iota(jnp.int32, (CHUNK, 1), 0)
        rows16 = lax.broadcasted_iota(jnp.int32, (SUB, 1), 0)
        da_full = _dot(do, vb, NT)
        a_tot = jnp.zeros((CHUNK, CHUNK), F32)
        for s in range(1, NSUB):
            lo = s * SUB
            bref = b_scr[lo - 1:lo, :]
            in_s = jnp.logical_and(rowc >= lo, rowc < lo + SUB)
            eq = jnp.exp(jnp.where(in_s, b - bref, -1e30))
            ek = jnp.exp(jnp.where(rowc < lo, bref - b, -1e30))
            qh = q * eq
            kh = k * ek
            a_tot = a_tot + _dot(qh, kh, NT)
            da = jnp.where(in_s, da_full, 0.0)
            dqh = _dot(da, kh)
            dkh = _dot(da, qh, TN)
            tq = dqh * qh
            tk = dkh * kh
            dq_scr[...] += dqh * eq
            dk_scr[...] += dkh * ek
            db_scr[...] += tq - tk
            db_scr[lo - 1:lo, :] += jnp.sum(tk, axis=0, keepdims=True) - jnp.sum(tq, axis=0, keepdims=True)
        dat_full = _dot(vb, do, NT)
        q_scr[...] = q
        lane = lax.broadcasted_iota(jnp.int32, (SUB, CHUNK), 1)
        diag = []
        for s in range(NSUB):
            lo = s * SUB
            qs, ks, bs = q[lo:lo + SUB], k[lo:lo + SUB], b[lo:lo + SUB]
            da_blk, dat_blk = da_full[lo:lo + SUB], dat_full[lo:lo + SUB]
            dqs = jnp.zeros((SUB, GLA_DK), F32)
            dks = jnp.zeros((SUB, GLA_DK), F32)
            dbs = jnp.zeros((SUB, GLA_DK), F32)
            s_blk = jnp.zeros((SUB, CHUNK), F32)
            for j in range(SUB):
                r = lo + j
                kj = k_ref[r:r + 1, :]
                e = jnp.exp(jnp.where(rows16 >= j, bs - b_scr[r:r + 1, :], -1e30))
                p = qs * e * kj
                s_blk = jnp.where(lane == r, jnp.sum(p, axis=1, keepdims=True), s_blk)
                dcol = jnp.sum(jnp.where(lane == r, da_blk, 0.0), axis=1, keepdims=True)
                dqs = dqs + (dcol * e) * kj
                dbs = dbs + dcol * p
            for i in range(SUB):
                r = lo + i
                e = jnp.exp(jnp.where(rows16 <= i, b_scr[r:r + 1, :] - bs, -1e30))
                drow = jnp.sum(jnp.where(lane == r, dat_blk, 0.0), axis=1, keepdims=True)
                nq = (drow * e) * q_scr[r:r + 1, :]
                dks = dks + nq
                dbs = dbs - nq * ks
            dq_scr[lo:lo + SUB, :] += dqs
            dk_scr[lo:lo + SUB, :] += dks
            db_scr[lo:lo + SUB, :] += dbs
            diag.append(s_blk)
        dv_scr[...] += _dot(a_tot + jnp.concatenate(diag, axis=0), do, TN)
        db_scr[CHUNK - 1:CHUNK, :] += dbl
        dla = _tri_sum(db_scr[...], True)
        dlogit = jnp.where(live, dla * (1.0 / GLA_TAU) * _sigmoid(-logit), 0.0)
        dl_ref[...] = dlogit
        dbg_ref[0] += jnp.sum(dlogit, axis=0, keepdims=True)
        dq_ref[...] = (dq_scr[...] * (GLA_DK ** -0.5)).astype(BF16)
        dk_ref[...] = dk_scr[...].astype(BF16)
        dv_ref[...] = dv_scr[...].astype(BF16)

    in_specs = _gla_in_specs(True, nc) + [
        pl.BlockSpec((CHUNK, GLA_DV), lambda h, n: (rn(n), h)),
        pl.BlockSpec((CHUNK, GLA_DV), lambda h, n: (rn(n), h)),
        pl.BlockSpec((1, 1, GLA_DV, GLA_DK), lambda h, n: (h, rn(n), 0, 0)),
    ]
    return pl.pallas_call(
        body, name="gla_bwd", grid=(GLA_HEADS, nc),
        in_specs=in_specs,
        out_specs=[pl.BlockSpec((CHUNK, GLA_DK), lambda h, n: (rn(n), h)),
                   pl.BlockSpec((CHUNK, GLA_DK), lambda h, n: (rn(n), h)),
                   pl.BlockSpec((CHUNK, GLA_DV), lambda h, n: (rn(n), h)),
                   pl.BlockSpec((CHUNK, GLA_DV), lambda h, n: (rn(n), h)),
                   pl.BlockSpec((CHUNK, GLA_DK), lambda h, n: (rn(n), h)),
                   pl.BlockSpec((1, 1, GLA_DV), lambda h, n: (h, 0, 0)),
                   pl.BlockSpec((1, 1, GLA_DK), lambda h, n: (h, 0, 0))],
        out_shape=[jax.ShapeDtypeStruct((rows, GLA_QK), BF16), jax.ShapeDtypeStruct((rows, GLA_QK), BF16),
                   jax.ShapeDtypeStruct((rows, GLA_W), BF16), jax.ShapeDtypeStruct((rows, GLA_W), BF16),
                   jax.ShapeDtypeStruct((rows, GLA_QK), F32),
                   jax.ShapeDtypeStruct((GLA_HEADS, 1, GLA_DV), F32),
                   jax.ShapeDtypeStruct((GLA_HEADS, 1, GLA_DK), F32)],
        scratch_shapes=[pltpu.VMEM((GLA_DV, GLA_DK), F32), pltpu.VMEM((CHUNK, GLA_DK), F32),
                        pltpu.VMEM((CHUNK, GLA_DK), F32), pltpu.VMEM((CHUNK, GLA_DV), F32),
                        pltpu.VMEM((CHUNK, GLA_DK), F32), pltpu.VMEM((CHUNK, GLA_DK), F32),
                        pltpu.VMEM((CHUNK, GLA_DK), F32)],
        compiler_params=pltpu.CompilerParams(dimension_semantics=("parallel", "arbitrary")),
    )(proj, proj, proj, proj, proj, wgate, bgate, normw, o_gla, d_oc, states)


def _adamw(name, w, g, m, v):
    rows, cols = w.shape
    tm = 8
    for cand in range(8, rows + 1, 8):
        if rows % cand == 0 and cand * cols * 4 <= 2 ** 21:
            tm = cand
    c1 = 1.0 - ADAM_B1 ** ADAM_STEP
    c2 = 1.0 - ADAM_B2 ** ADAM_STEP

    def body(w_ref, g_ref, m_ref, v_ref, d_ref, nm_ref, nv_ref):
        gv = g_ref[...]
        nm = ADAM_B1 * m_ref[...] + (1.0 - ADAM_B1) * gv
        nv = ADAM_B2 * v_ref[...] + (1.0 - ADAM_B2) * (gv * gv)
        nm_ref[...] = nm
        nv_ref[...] = nv
        d_ref[...] = -ADAM_LR * ((nm / c1) / (jnp.sqrt(nv / c2) + ADAM_EPS) + ADAM_WD * w_ref[...])

    blk = pl.BlockSpec((tm, cols), lambda i: (i, 0))
    return pl.pallas_call(
        body, name=name, grid=(rows // tm,),
        in_specs=[blk] * 4, out_specs=[blk] * 3,
        out_shape=[jax.ShapeDtypeStruct((rows, cols), F32)] * 3,
    )(w, g, m, v)


def _place():
    x, y, c = lax.axis_index("x"), lax.axis_index("y"), lax.axis_index("c")
    chips = [(1 - x, y), (x, 1 - y), (1 - x, 1 - y)]
    return x, y, c, chips


ANY = pl.BlockSpec(memory_space=pl.ANY)


def _gathered_struct(shape, dtype, kind):
    r, cc = shape
    if kind == "row":
        return jax.ShapeDtypeStruct((N_SHARD * r, cc), dtype)
    if kind == "col":
        return jax.ShapeDtypeStruct((r, N_SHARD * cc), dtype)
    return jax.ShapeDtypeStruct((N_SHARD, r, cc), dtype)


def _cast_place(name, w, kind, mine_arr, dtype):
    r, cc = w.shape
    tr = r
    for cand in (256, 128, 64, 32, 16):
        if r % cand == 0:
            tr = cand
            break
    nb = r // tr
    if kind == "row":
        o_spec = pl.BlockSpec((tr, cc), lambda i, m: (m[0] * nb + i, 0))
    elif kind == "col":
        o_spec = pl.BlockSpec((tr, cc), lambda i, m: (i, m[0]))
    else:
        o_spec = pl.BlockSpec((None, tr, cc), lambda i, m: (m[0], i, 0))
    w_spec = pl.BlockSpec((tr, cc), lambda i, m: (i, 0))

    def body(m_ref, w_ref, o_ref):
        o_ref[...] = w_ref[...].astype(o_ref.dtype)

    return pl.pallas_call(
        body, name=name,
        grid_spec=pltpu.PrefetchScalarGridSpec(
            num_scalar_prefetch=1, grid=(nb,), in_specs=[w_spec], out_specs=o_spec),
        out_shape=_gathered_struct((r, cc), dtype, kind),
    )(mine_arr, w)


def _allreduce_small(buf):
    rows, cols = buf.shape

    def body(in_ref, out_ref, sib_ref, pair_ref, far_ref, send_sems, recv_sems):
        x, y, c, chips = _place()
        sibling = (x, y, 1 - c)
        to_sib = pltpu.make_async_remote_copy(
            src_ref=in_ref, dst_ref=sib_ref, send_sem=send_sems.at[0], recv_sem=recv_sems.at[0],
            device_id=sibling, device_id_type=MESH)
        to_sib.start()
        to_sib.wait()
        pair_ref[...] = in_ref[...] + sib_ref[...]
        far = [pltpu.make_async_remote_copy(
            src_ref=pair_ref, dst_ref=far_ref.at[j], send_sem=send_sems.at[1 + j], recv_sem=recv_sems.at[1 + j],
            device_id=(*chip, c), device_id_type=MESH) for j, chip in enumerate(chips)]
        for cp in far:
            cp.start()
        for cp in far:
            cp.wait()
        out_ref[...] = (pair_ref[...] + far_ref[1]) + (far_ref[0] + far_ref[2])

    vm = pl.BlockSpec(memory_space=pltpu.VMEM)
    return pl.pallas_call(
        body, name="allreduce_small",
        in_specs=[vm], out_specs=vm,
        out_shape=jax.ShapeDtypeStruct((rows, cols), F32),
        scratch_shapes=[pltpu.VMEM((rows, cols), F32), pltpu.VMEM((rows, cols), F32),
                        pltpu.VMEM((3, rows, cols), F32),
                        pltpu.SemaphoreType.DMA((4,)), pltpu.SemaphoreType.DMA((4,))],
        compiler_params=pltpu.CompilerParams(has_side_effects=True),
    )(buf)


def _shard_window(ref, kind, shard_shape, shard, half):
    r, cc = shard_shape
    hr = r // 2
    if kind == "row":
        return ref.at[pl.ds(_mo(shard * r + half * hr, 8), hr), :]
    if kind == "col":
        return ref.at[pl.ds(_mo(half * hr, 8), hr), pl.ds(_mo(shard * cc, 128), cc)]
    if kind == "colw":
        return ref.at[pl.ds(_mo(half * hr, 8), hr), pl.ds(_mo(shard * (cc - 128), 128), cc)]
    return ref.at[shard, pl.ds(_mo(half * hr, 8), hr), :]


HBM = pl.BlockSpec(memory_space=pltpu.HBM)
SEM = pl.BlockSpec(memory_space=pltpu.SEMAPHORE)
DATAFLOW = pltpu.SideEffectType.DATAFLOW_SIDE_EFFECTING


def _in_hbm(a):
    return pltpu.with_memory_space_constraint(a, pltpu.HBM)


def _empty_hbm(shape, dtype):
    return _in_hbm(lax.empty(shape, dtype))


def _copies_start(name, bufs, n_copies, plan, carry):
    nb = len(bufs)

    def body(*refs):
        send_sems, recv_sems = refs[nb + 1], refs[nb + 2]
        for k, (src, dst, to) in enumerate(plan(refs[:nb])):
            pltpu.make_async_remote_copy(src_ref=src, dst_ref=dst, send_sem=send_sems.at[k], recv_sem=recv_sems.at[k],
                                         device_id=to, device_id_type=MESH).start()

    passed = list(bufs) + [carry]
    out = pl.pallas_call(
        body, name=name,
        in_specs=[HBM] * (nb + 1), out_specs=[SEM, SEM] + [HBM] * (nb + 1),
        out_shape=[pltpu.SemaphoreType.DMA((n_copies,)), pltpu.SemaphoreType.DMA((n_copies,))]
        + [pltpu.HBM(a.shape, a.dtype) for a in passed],
        input_output_aliases={i: 2 + i for i in range(nb + 1)},
        compiler_params=pltpu.CompilerParams(has_side_effects=DATAFLOW),
    )(*[_in_hbm(a) for a in passed])
    return out[0], out[1], list(out[2:2 + nb]), out[2 + nb]


def _copies_wait(name, send_sems, recv_sems, bufs, plan, after):
    nb = len(bufs)
    after = list(after) if isinstance(after, (list, tuple)) else [after]

    def body(*refs):
        send, recv = refs[nb], refs[nb + 1]
        for k, (src, dst, to) in enumerate(plan(refs[:nb])):
            cp = pltpu.make_async_remote_copy(src_ref=src, dst_ref=dst, send_sem=send.at[k], recv_sem=recv.at[k],
                                              device_id=to, device_id_type=MESH)
            cp.wait_send()
            cp.wait_recv()

    out = pl.pallas_call(
        body, name=name,
        in_specs=[HBM] * nb + [SEM, SEM] + [ANY] * len(after), out_specs=[HBM] * nb,
        out_shape=[pltpu.HBM(a.shape, a.dtype) for a in bufs],
        input_output_aliases={i: i for i in range(nb)},
        compiler_params=pltpu.CompilerParams(has_side_effects=DATAFLOW),
    )(*bufs, send_sems, recv_sems, *after)
    return list(out)


def _gather_ici_plan(shard_shapes, kinds):
    n_arr = len(kinds)

    def plan(refs):
        x, y, c, chips = _place()
        out = []
        for i in range(n_arr):
            w = _shard_window(refs[i], kinds[i], shard_shapes[i], 2 * x + y, c)
            out += [(w, w, (*chip, c)) for chip in chips]
        return out

    return plan


def _gather_d2d_plan(shard_shapes, kinds):
    n_arr = len(kinds)

    def plan(refs):
        x, y, c, chips = _place()
        out = []
        for i in range(n_arr):
            for chip in chips:
                w = _shard_window(refs[i], kinds[i], shard_shapes[i], 2 * chip[0] + chip[1], c)
                out.append((w, w, (x, y, 1 - c)))
        return out

    return plan


def _rs_pair_plan(kinds, shard_shapes):
    n_arr = len(kinds)

    def plan(refs):
        x, y, c, _ = _place()
        out = []
        for i in range(n_arr):
            for s in range(N_SHARD):
                out.append((_shard_window(refs[i], kinds[i], shard_shapes[i], s, 1 - c), refs[n_arr + i].at[s],
                            (x, y, 1 - c)))
        return out

    return plan


def _rs_chip_plan(n_arr):
    def plan(refs):
        x, y, c, chips = _place()
        out = []
        for i in range(n_arr):
            for j, chip in enumerate(chips):
                out.append((refs[i].at[2 * chip[0] + chip[1]], refs[n_arr + i].at[j], (*chip, c)))
        return out

    return plan


def _rs_pair_add(name, grad, got, kind, shard_shape, c):
    r, cc = shard_shape
    hr = r // 2
    tr = hr
    for cand in (256, 128, 64, 32, 16):
        if hr % cand == 0:
            tr = cand
            break
    nb = hr // tr

    def body(c_ref, g_ref, t_ref, p_ref, pb_ref):
        p = g_ref[...] + t_ref[...]
        p_ref[...] = p
        pb_ref[...] = p.astype(BF16)

    out_shape = [jax.ShapeDtypeStruct((N_SHARD, hr, cc), F32), jax.ShapeDtypeStruct((N_SHARD, hr, cc), BF16)]
    if kind == "colw":
        tiles = cc // 128
        tr = hr
        g_spec = pl.BlockSpec((tr, 128), lambda s, t, cr: (cr[0], s * (tiles - 1) + t))
        t_spec = pl.BlockSpec((None, tr, 128), lambda s, t, cr: (s, 0, t))
        return pl.pallas_call(
            body, name=name,
            grid_spec=pltpu.PrefetchScalarGridSpec(
                num_scalar_prefetch=1, grid=(N_SHARD, tiles), in_specs=[g_spec, t_spec], out_specs=[t_spec, t_spec]),
            out_shape=out_shape,
        )(c, grad, got)
    if kind == "row":
        g_spec = pl.BlockSpec((tr, cc), lambda s, i, cr: (s * 2 * nb + cr[0] * nb + i, 0))
    elif kind == "col":
        g_spec = pl.BlockSpec((tr, cc), lambda s, i, cr: (cr[0] * nb + i, s))
    else:
        g_spec = pl.BlockSpec((None, tr, cc), lambda s, i, cr: (s, cr[0] * nb + i, 0))
    t_spec = pl.BlockSpec((None, tr, cc), lambda s, i, cr: (s, i, 0))
    return pl.pallas_call(
        body, name=name,
        grid_spec=pltpu.PrefetchScalarGridSpec(
            num_scalar_prefetch=1, grid=(N_SHARD, nb),
            in_specs=[g_spec, t_spec], out_specs=[t_spec, t_spec]),
        out_shape=out_shape,
    )(c, grad, got)


def _rs_chip_add(name, pair_f32, got, shard_shape, mine_c):
    r, cc = shard_shape
    hr = r // 2
    tr = hr
    for cand in (256, 128, 64, 32, 16):
        if hr % cand == 0:
            tr = cand
            break
    nb = hr // tr

    def body(mc_ref, p_ref, t0_ref, t1_ref, t2_ref, o_ref):
        o_ref[...] = (p_ref[...] + t1_ref[...].astype(F32)) + (t0_ref[...].astype(F32) + t2_ref[...].astype(F32))

    def far(j):
        return pl.BlockSpec((None, tr, cc), lambda i, mc: (j, i, 0))

    return pl.pallas_call(
        body, name=name,
        grid_spec=pltpu.PrefetchScalarGridSpec(
            num_scalar_prefetch=1, grid=(nb,),
            in_specs=[pl.BlockSpec((None, tr, cc), lambda i, mc: (mc[0], i, 0)), far(0), far(1), far(2)],
            out_specs=pl.BlockSpec((tr, cc), lambda i, mc: (mc[1] * nb + i, 0))),
        out_shape=jax.ShapeDtypeStruct((r, cc), F32),
    )(mine_c, pair_f32, got, got, got)


def _rs_pair_share(name, halves, shard_shapes):
    n_arr = len(halves)

    def body(*refs):
        ins = refs[:n_arr]
        outs = refs[n_arr:2 * n_arr]
        send_sems, recv_sems = refs[2 * n_arr:]
        x, y, c, _ = _place()
        sibling = (x, y, 1 - c)
        cps = []
        for i in range(n_arr):
            hr = shard_shapes[i][0] // 2
            rows = pl.ds(_mo(c * hr, 8), hr)
            cp = pltpu.make_async_remote_copy(
                src_ref=outs[i].at[rows, :], dst_ref=outs[i].at[rows, :],
                send_sem=send_sems.at[i], recv_sem=recv_sems.at[i],
                device_id=sibling, device_id_type=MESH)
            cp.start()
            cps.append(cp)
        for cp in cps:
            cp.wait()

    return pl.pallas_call(
        body, name=name,
        in_specs=[ANY] * n_arr, out_specs=[ANY] * n_arr,
        out_shape=[jax.ShapeDtypeStruct(s, F32) for s in shard_shapes],
        input_output_aliases={i: i for i in range(n_arr)},
        scratch_shapes=[pltpu.SemaphoreType.DMA((n_arr,)), pltpu.SemaphoreType.DMA((n_arr,))],
        compiler_params=pltpu.CompilerParams(has_side_effects=True),
    )(*halves)


def _pack(arrays):
    flat = []
    for a in arrays:
        v = a.reshape(-1).astype(F32)
        flat.append(jnp.pad(v, (0, (-v.shape[0]) % SMALL_COLS)))
    buf = jnp.concatenate(flat).reshape(-1, SMALL_COLS)
    return jnp.pad(buf, ((0, (-buf.shape[0]) % 16), (0, 0)))


def _unpack(buf, shapes):
    out = []
    row = 0
    for s in shapes:
        size = math.prod(s)
        nrow = -(-size // SMALL_COLS)
        out.append(buf[row:row + nrow].reshape(-1)[:size].reshape(s))
        row += nrow
    return out


def kernel(x, meta, norm_ab_w, w_in_ab, ret_norm_w, s5_lam_re, s5_lam_im, s5_log_dt, s5_b_re, s5_b_im, s5_c_re, s5_c_im, s5_d, s5_w_glu, w_out_ab, norm_c_w, w_in_c, gla_w_gate, gla_b_gate, gla_norm_w, w_out_c, final_norm_w, loss_target, m_meta, m_norm_ab_w, m_w_in_ab, m_ret_norm_w, m_s5_lam_re, m_s5_lam_im, m_s5_log_dt, m_s5_b_re, m_s5_b_im, m_s5_c_re, m_s5_c_im, m_s5_d, m_s5_w_glu, m_w_out_ab, m_norm_c_w, m_w_in_c, m_gla_w_gate, m_gla_b_gate, m_gla_norm_w, m_w_out_c, m_final_norm_w, v_meta, v_norm_ab_w, v_w_in_ab, v_ret_norm_w, v_s5_lam_re, v_s5_lam_im, v_s5_log_dt, v_s5_b_re, v_s5_b_im, v_s5_c_re, v_s5_c_im, v_s5_d, v_s5_w_glu, v_w_out_ab, v_norm_c_w, v_w_in_c, v_gla_w_gate, v_gla_b_gate, v_gla_norm_w, v_w_out_c, v_final_norm_w):
    seq = x.shape[1]
    rows = seq + CHUNK
    xi, yi, ci = lax.axis_index("x"), lax.axis_index("y"), lax.axis_index("c")
    mine = 2 * xi + yi
    c_arr = jnp.reshape(ci, (1,)).astype(jnp.int32)
    mine_c = jnp.stack([mine, ci]).astype(jnp.int32)

    mine_arr = jnp.reshape(mine, (1,)).astype(jnp.int32)
    small_shard = _pack([meta, norm_c_w, gla_norm_w, gla_b_gate, gla_w_gate[0]])
    first_kinds = ["col", "stack"]
    first_shapes = [w_in_ab.shape[1:], small_shard.shape]
    first_ici = _gather_ici_plan(first_shapes, first_kinds)
    first_d2d = _gather_d2d_plan(first_shapes, first_kinds)
    f_send, f_recv, f_bufs, small_shard = _copies_start(
        "gather_first_ici_start",
        [_cast_place("place_w_in_ab", w_in_ab[0], "col", mine_arr, BF16),
         _cast_place("place_small", small_shard, "stack", mine_arr, F32)], 6, first_ici, small_shard)
    late = [("w_out_ab", w_out_ab[0]), ("w_in_c", w_in_c[0]), ("w_out_c", w_out_c[0]), ("w_glu", s5_w_glu[0])]
    late_kinds = ["row", "stack", "row", "row"]
    late_shapes = [a.shape for _, a in late]
    ici_plan = _gather_ici_plan(late_shapes, late_kinds)
    d2d_plan = _gather_d2d_plan(late_shapes, late_kinds)
    n_late = 3 * len(late)
    g_bufs = [_cast_place("place_" + nm, a, kd, mine_arr, BF16) for (nm, a), kd in zip(late, late_kinds)]
    cosf, sinf = _rope_tables(rows)
    rtab = _ret_tables()
    ab_re, ab_im, bb_re, bb_im = _s5_discretize(s5_lam_re[0], s5_lam_im[0], s5_log_dt[0], s5_b_re[0], s5_b_im[0])
    ab = (ab_re, ab_im)
    bd_b = (_bdiag_in(bb_re), _bdiag_in(bb_im))
    bd_c = (_bdiag_out(s5_c_re[0]), _bdiag_out(s5_c_im[0]))
    f_bufs = _copies_wait("gather_first_ici_wait", f_send, f_recv, f_bufs, first_ici,
                          [cosf, sinf, bd_b[0], bd_b[1], bd_c[0], bd_c[1]] + g_bufs + list(rtab))
    f_send, f_recv, f_bufs, cosf = _copies_start("gather_first_d2d_start", f_bufs, 6, first_d2d, cosf)
    wab, small_all = _copies_wait("gather_first_d2d_wait", f_send, f_recv, f_bufs, first_d2d, cosf)
    g_send, g_recv, g_bufs, wab = _copies_start("gather_late_ici_start", g_bufs, n_late, ici_plan, wab)
    q4 = D_MODEL // N_SHARD
    g4 = GLA_QK // N_SHARD
    parts = [_unpack(small_all[j], [(N_META, q4), (1, q4), (1, q4), (1, g4), (GLA_RANK, g4)]) for j in range(N_SHARD)]
    meta_f, norm_c_f, gla_norm_f, bgate_f, wgate_f = [jnp.concatenate([p[i] for p in parts], axis=1) for i in range(5)]
    wgate_pad = jnp.pad(wgate_f, ((0, 128 - GLA_RANK), (0, 0)))

    h0, hn0 = _embed_norm(x[0], meta_f, norm_ab_w)

    tm = _row_tile(rows, 1408)
    tmk = _row_tile(rows, 1408)
    proj0 = _matmul("in_proj_ab", hn0, wab, NN, rows, IN_AB, D_MODEL, tm=tm, tn=512, tk=D_MODEL)
    o_ret, o_a, ret_states = _ret_fwd(proj0, cosf, sinf, rtab, ret_norm_w)
    g_bufs = _copies_wait("gather_late_ici_wait", g_send, g_recv, g_bufs, ici_plan, o_a)
    g_send, g_recv, g_bufs, proj0 = _copies_start("gather_late_d2d_start", g_bufs, n_late, d2d_plan, proj0)
    y_s5, g_s5, s5_er, s5_ei = _s5_fwd(proj0, ab, bd_b, bd_c, s5_d)
    wout_ab, wc_st, wout_c, wglu = _copies_wait("gather_late_d2d_wait", g_send, g_recv, g_bufs, d2d_plan, g_s5)
    wc = jnp.concatenate([wc_st[j] for j in range(N_SHARD)] + [jnp.zeros((D_MODEL, IN_C_PAD - IN_C), BF16)], axis=1)
    zb_blk = (2 * RET_QK + 2 * RET_W + S5_W) // 512

    def glu_out(acc, gv, z):
        return gv.astype(F32) * _sigmoid(acc) * (z * _sigmoid(z))

    t_glu = _matmul("glu", g_s5, wglu, NN, rows, S5_W, S5_W, tm=tm, tn=512, tk=S5_W)
    o_b = _matmul("glu_out", g_s5, wglu, NN, rows, S5_W, S5_W, tm=tm, tn=512, tk=S5_W, out_dtype=BF16,
                  extras=[(g_s5, (tm, 512), lambda i, j, kk: (i, j)),
                          (proj0, (tm, 512), lambda i, j, kk: (i, zb_blk + j))],
                  epilogue=glu_out)
    h1 = _matmul("out_proj_ab", None, None, NN, rows, D_MODEL, OUT_AB, tm=tm, tn=512, tk=1024,
                 segs=[(o_a, (0, 0), wout_ab, (0, 0), RET_W, 1024),
                       (o_b, (0, 0), wout_ab, (RET_W // 1024, 0), S5_W, 1024)],
                 extras=[(h0, (tm, 512), lambda i, j, kk: (i, j))], epilogue=lambda acc, r: acc + r)

    hn1 = _rms_fwd("norm_c", h1, norm_c_f)
    proj1 = _matmul("in_proj_c", hn1, wc, NN, rows, IN_C_PAD, D_MODEL, tm=tm, tn=896, tk=D_MODEL)
    o_gla, o_c, gla_states = _gla_fwd(proj1, wgate_pad, bgate_f, gla_norm_f)
    h2 = _matmul("out_proj_c", o_c, wout_c, NN, rows, D_MODEL, GLA_W, tm=tm, tn=512, tk=GLA_W,
                 extras=[(h1, (tm, 512), lambda i, j, kk: (i, j))], epilogue=lambda acc, r: acc + r)
    loss_dev, dh2, d_final = _final_loss(h2, final_norm_w.reshape(1, D_MODEL), loss_target[0])

    g_wout_c = _matmul("d_w_out_c", o_c, dh2, TN, GLA_W, D_MODEL, rows, tm=1024, tn=1024, tk=tmk)
    d_oc = _matmul("d_o_c", dh2, wout_c, NT, rows, GLA_W, D_MODEL, tm=tm, tn=512, tk=1024)
    dq1, dk1, dv1, dz1, dlogit, d_gla_norm, d_bgate = _gla_bwd(proj1, wgate_pad, bgate_f, gla_norm_f, o_gla, d_oc, gla_states)
    gl_blk = (2 * GLA_QK + 2 * GLA_W) // 128
    dgl = _matmul("d_g_low", dlogit, wgate_pad, NT, rows, 128, GLA_QK, tm=tm, tn=128, tk=GLA_QK, out_dtype=BF16)
    g_wgate = _matmul("d_w_gate", proj1, dlogit, TN, 128, GLA_QK, rows, tm=128, tn=GLA_QK, tk=tmk, a_off=(0, gl_blk))
    dproj1 = jnp.concatenate([dq1, dk1, dv1, dz1, dgl], axis=1)
    g_wc = _matmul("d_w_in_c", hn1, dproj1, TN, D_MODEL, IN_C_PAD, rows, tm=1024, tn=896, tk=tmk)
    dhn1 = _matmul("d_hn1", dproj1, wc, NT, rows, D_MODEL, IN_C_PAD, tm=tm, tn=512, tk=896)
    dh1, d_norm_c = _rms_bwd("norm_c_bwd", dhn1, h1, norm_c_f, dh2)

    g_wout_ab = _matmul("d_w_out_ab_a", o_a, dh1, TN, RET_W, D_MODEL, rows, tm=1024, tn=1024, tk=tmk,
                        out_shape=jax.ShapeDtypeStruct((OUT_AB, D_MODEL), F32))
    g_wout_ab = _matmul("d_w_out_ab_b", o_b, dh1, TN, S5_W, D_MODEL, rows, tm=1024, tn=1024, tk=tmk,
                        into=(g_wout_ab, RET_W // 1024, 0))
    dmix = _matmul("d_mix", dh1, wout_ab, NT, rows, OUT_AB, D_MODEL, tm=tm, tn=512, tk=1024)
    dproj0, d_ret_norm = _ret_bwd(proj0, cosf, sinf, rtab, ret_norm_w, o_ret, dmix, ret_states)
    dproj0, dt_glu, dg_direct = _s5_gate_bwd(dmix, g_s5, t_glu, proj0, dproj0)
    g_wglu = _matmul("d_w_glu", g_s5, dt_glu, TN, S5_W, S5_W, rows, tm=1024, tn=1024, tk=tmk)
    dy_s5 = _matmul("d_y_s5", dt_glu, wglu, NT, rows, S5_W, S5_W, tm=tm, tn=512, tk=S5_W,
                    extras=[(dg_direct, (tm, 512), lambda i, j, kk: (i, j)),
                            (y_s5, (tm, 512), lambda i, j, kk: (i, j))],
                    epilogue=lambda acc, dg, yv: (acc + dg) * _gelu_grad(yv))
    wc_cols = IN_C // N_SHARD
    wc_win = (wc_cols // 128 + 1) * 128
    rs1_names = ["w_out_ab", "w_in_c", "w_out_c", "w_glu"]
    rs1_kinds = ["row", "colw", "row", "row"]
    rs1_shapes = [w_out_ab.shape[1:], (D_MODEL, wc_win), w_out_c.shape[1:], s5_w_glu.shape[1:]]
    rs1_plan = _rs_pair_plan(rs1_kinds, rs1_shapes)
    rs1_land = [_empty_hbm((N_SHARD, r // 2, cc), F32) for (r, cc) in rs1_shapes]
    p_send, p_recv, p_bufs, dy_s5 = _copies_start("rs1_pair_start", [g_wout_ab, g_wc, g_wout_c, g_wglu] + rs1_land,
                                                  N_SHARD * 4, rs1_plan, dy_s5)
    dproj0, dbr_d, dbi_d, dcr_d, dci_d, dar_p, dai_p, dd_p = _s5_bwd(proj0, dy_s5, ab, bd_b, bd_c, s5_d,
                                                                     (s5_er, s5_ei), dproj0)
    p_bufs = _copies_wait("rs1_pair_wait", p_send, p_recv, p_bufs, rs1_plan, dproj0)
    rs1_pairs = [_rs_pair_add("rs_pair_add_" + nm, g, t, kd, ss, c_arr)
                 for nm, g, t, kd, ss in zip(rs1_names, p_bufs[:4], p_bufs[4:], rs1_kinds, rs1_shapes)]
    rs1_chip_plan = _rs_chip_plan(4)
    rs1_land2 = [_empty_hbm((3, r // 2, cc), BF16) for (r, cc) in rs1_shapes]
    c_send, c_recv, c_bufs, dproj0 = _copies_start("rs1_chip_start", [p[1] for p in rs1_pairs] + rs1_land2, 12,
                                                   rs1_chip_plan, dproj0)
    g_wab = _matmul("d_w_in_ab", hn0, dproj0, TN, D_MODEL, IN_AB, rows, tm=1024, tn=1024, tk=tmk)
    rs2_shapes = [w_in_ab.shape[1:]]
    rs2_plan = _rs_pair_plan(["col"], rs2_shapes)
    rs2_land = [_empty_hbm((N_SHARD, rs2_shapes[0][0] // 2, rs2_shapes[0][1]), F32)]
    q_send, q_recv, q_bufs, dproj0 = _copies_start("rs2_pair_start", [g_wab] + rs2_land, N_SHARD, rs2_plan, dproj0)
    dhn0 = _matmul("d_hn0_a", dproj0, wab, NT, tm, D_MODEL, IN_AB, tm=tm, tn=512, tk=2048,
                   out_shape=jax.ShapeDtypeStruct((rows, D_MODEL), F32))
    q_bufs = _copies_wait("rs2_pair_wait", q_send, q_recv, q_bufs, rs2_plan, dhn0)
    rs2_pair = _rs_pair_add("rs_pair_add_w_in_ab", q_bufs[0], q_bufs[1], "col", rs2_shapes[0], c_arr)
    rs2_chip_plan = _rs_chip_plan(1)
    rs2_land2 = [_empty_hbm((3, rs2_shapes[0][0] // 2, rs2_shapes[0][1]), BF16)]
    r_send, r_recv, r_bufs, dhn0 = _copies_start("rs2_chip_start", [rs2_pair[1]] + rs2_land2, 3, rs2_chip_plan, dhn0)
    if rows > tm:
        dhn0 = _matmul("d_hn0_b", dproj0, wab, NT, rows - tm, D_MODEL, IN_AB, tm=tm, tn=512, tk=2048, a_off=(1, 0),
                       into=(dhn0, 1, 0))
    grad_x, d_meta, d_norm_ab = _rms_bwd_embed(dhn0, h0, norm_ab_w, dh1)
    c_bufs = _copies_wait("rs1_chip_wait", c_send, c_recv, c_bufs, rs1_chip_plan, grad_x)
    grad_x = grad_x[None]
    rs1_halves = [_rs_chip_add("rs_chip_add_" + nm, p[0], t, ss, mine_c)
                  for nm, p, t, ss in zip(rs1_names, rs1_pairs, c_bufs[4:], rs1_shapes)]
    g_w_out_ab, g_w_in_c, g_w_out_c, g_w_glu = _rs_pair_share("rs1_pair_share", rs1_halves, rs1_shapes)
    g_w_in_c = lax.dynamic_slice(g_w_in_c, (0, (wc_cols % 128) * mine), (D_MODEL, wc_cols))

    d_ab_re = jnp.sum(dar_p, axis=1).reshape(S5_G, S5_P)
    d_ab_im = jnp.sum(dai_p, axis=1).reshape(S5_G, S5_P)
    small_local = [loss_dev, d_meta, d_norm_ab, d_ret_norm.reshape(1, RET_W), d_ab_re, d_ab_im,
                   _bdiag_in_extract(dbr_d), _bdiag_in_extract(dbi_d),
                   _bdiag_out_extract(dcr_d), _bdiag_out_extract(dci_d),
                   jnp.sum(dd_p, axis=1).reshape(1, S5_W), d_norm_c, g_wgate[:GLA_RANK],
                   d_bgate.reshape(1, GLA_QK), d_gla_norm.reshape(1, GLA_W), d_final]
    small_shapes = [a.shape for a in small_local]
    summed = _unpack(_allreduce_small(_pack(small_local)), small_shapes)
    (loss, g_meta_f, g_norm_ab, g_ret_norm, g_ab_re, g_ab_im, g_bb_re, g_bb_im, g_c_re, g_c_im, g_d,
     g_norm_c_f, g_wgate_f, g_bgate_f, g_gla_norm_f, g_final) = summed
    _, s5_vjp = jax.vjp(_s5_discretize, s5_lam_re[0], s5_lam_im[0], s5_log_dt[0], s5_b_re[0], s5_b_im[0])
    g_lam_re, g_lam_im, g_log_dt, g_b_re, g_b_im = s5_vjp((g_ab_re, g_ab_im, g_bb_re, g_bb_im))

    def take(a, width):
        return lax.dynamic_slice_in_dim(a, mine * width, width, axis=1)

    grads = {
        "meta": take(g_meta_f, q4), "norm_ab_w": g_norm_ab, "ret_norm_w": g_ret_norm,
        "s5_lam_re": g_lam_re[None], "s5_lam_im": g_lam_im[None], "s5_log_dt": g_log_dt[None],
        "s5_b_re": g_b_re[None], "s5_b_im": g_b_im[None], "s5_c_re": g_c_re[None], "s5_c_im": g_c_im[None],
        "s5_d": g_d, "s5_w_glu": g_w_glu[None], "w_out_ab": g_w_out_ab[None], "norm_c_w": take(g_norm_c_f, q4),
        "w_in_c": g_w_in_c[None], "gla_w_gate": take(g_wgate_f, g4)[None], "gla_b_gate": take(g_bgate_f, g4),
        "gla_norm_w": take(g_gla_norm_f, q4), "w_out_c": g_w_out_c[None], "final_norm_w": g_final.reshape(D_MODEL),
    }
    weights = dict(meta=meta, norm_ab_w=norm_ab_w, w_in_ab=w_in_ab, ret_norm_w=ret_norm_w, s5_lam_re=s5_lam_re,
                   s5_lam_im=s5_lam_im, s5_log_dt=s5_log_dt, s5_b_re=s5_b_re, s5_b_im=s5_b_im, s5_c_re=s5_c_re,
                   s5_c_im=s5_c_im, s5_d=s5_d, s5_w_glu=s5_w_glu, w_out_ab=w_out_ab, norm_c_w=norm_c_w,
                   w_in_c=w_in_c, gla_w_gate=gla_w_gate, gla_b_gate=gla_b_gate, gla_norm_w=gla_norm_w,
                   w_out_c=w_out_c, final_norm_w=final_norm_w)
    m_in = dict(meta=m_meta, norm_ab_w=m_norm_ab_w, w_in_ab=m_w_in_ab, ret_norm_w=m_ret_norm_w,
                s5_lam_re=m_s5_lam_re, s5_lam_im=m_s5_lam_im, s5_log_dt=m_s5_log_dt, s5_b_re=m_s5_b_re,
                s5_b_im=m_s5_b_im, s5_c_re=m_s5_c_re, s5_c_im=m_s5_c_im, s5_d=m_s5_d, s5_w_glu=m_s5_w_glu,
                w_out_ab=m_w_out_ab, norm_c_w=m_norm_c_w, w_in_c=m_w_in_c, gla_w_gate=m_gla_w_gate,
                gla_b_gate=m_gla_b_gate, gla_norm_w=m_gla_norm_w, w_out_c=m_w_out_c, final_norm_w=m_final_norm_w)
    v_in = dict(meta=v_meta, norm_ab_w=v_norm_ab_w, w_in_ab=v_w_in_ab, ret_norm_w=v_ret_norm_w,
                s5_lam_re=v_s5_lam_re, s5_lam_im=v_s5_lam_im, s5_log_dt=v_s5_log_dt, s5_b_re=v_s5_b_re,
                s5_b_im=v_s5_b_im, s5_c_re=v_s5_c_re, s5_c_im=v_s5_c_im, s5_d=v_s5_d, s5_w_glu=v_s5_w_glu,
                w_out_ab=v_w_out_ab, norm_c_w=v_norm_c_w, w_in_c=v_w_in_c, gla_w_gate=v_gla_w_gate,
                gla_b_gate=v_gla_b_gate, gla_norm_w=v_gla_norm_w, w_out_c=v_w_out_c, final_norm_w=v_final_norm_w)
    order = list(weights)
    big_names = ["s5_w_glu", "w_out_ab", "w_in_c", "w_out_c", "w_in_ab"]
    small_names = [nm for nm in order if nm not in big_names]
    delta, new_m, new_v = {}, {}, {}

    def big_update(nm):
        shp = weights[nm].shape
        d2, m2, v2 = _adamw("adamw_" + nm, weights[nm][0], grads[nm][0], m_in[nm][0], v_in[nm][0])
        delta[nm], new_m[nm], new_v[nm] = d2.reshape(shp), m2.reshape(shp), v2.reshape(shp)

    for nm in big_names[:-1]:
        big_update(nm)
    sshapes = [weights[nm].shape for nm in small_names]
    d2, m2, v2 = _adamw("adamw_small", _pack([weights[nm] for nm in small_names]),
                        _pack([grads[nm] for nm in small_names]), _pack([m_in[nm] for nm in small_names]),
                        _pack([v_in[nm] for nm in small_names]))
    for nm, dd, mm, vv in zip(small_names, _unpack(d2, sshapes), _unpack(m2, sshapes), _unpack(v2, sshapes)):
        delta[nm], new_m[nm], new_v[nm] = dd, mm, vv
    r_bufs = _copies_wait("rs2_chip_wait", r_send, r_recv, r_bufs, rs2_chip_plan,
                          [v2] + [new_v[nm] for nm in big_names[:-1]])
    rs2_half = _rs_chip_add("rs_chip_add_w_in_ab", rs2_pair[0], r_bufs[1], rs2_shapes[0], mine_c)
    grads["w_in_ab"] = _rs_pair_share("rs2_pair_share", [rs2_half], rs2_shapes)[0][None]
    big_update("w_in_ab")
    grads = {nm: grads[nm].reshape(weights[nm].shape) for nm in order}
    return (loss.reshape(()), grad_x, *[grads[nm] for nm in order], *[delta[nm] for nm in order],
            *[new_m[nm] for nm in order], *[new_v[nm] for nm in order])
```

```python
import functools
import math

import jax
import jax.numpy as jnp
from jax import lax
from jax.experimental import pallas as pl
from jax.experimental.pallas import tpu as pltpu

F32 = jnp.float32
BF16 = jnp.bfloat16
MESH = pl.DeviceIdType.MESH

D_MODEL = 2048
N_META = 16
CHUNK = 128
SUB = 16
NSUB = CHUNK // SUB
PAD = CHUNK - N_META
EPS = 1e-6

RET_HEADS = 8
RET_DK = 128
RET_DV = 256
RET_QK = RET_HEADS * RET_DK
RET_W = RET_HEADS * RET_DV
ROPE_BASE = 10000.0

S5_W = 1024
S5_GH = 16
S5_G = S5_W // S5_GH
S5_P = 64
S5_TG = 8
S5_NT = S5_G // S5_TG
S5_TU = S5_TG * S5_GH
S5_TS = S5_TG * S5_P
S5_FWD_TILES = 2
S5_BWD_TILES = 1

GLA_HEADS = 4
GLA_DK = 256
GLA_DV = 512
GLA_QK = GLA_HEADS * GLA_DK
GLA_W = GLA_HEADS * GLA_DV
GLA_RANK = 16
GLA_TAU = 16.0

IN_AB = 2 * RET_QK + 2 * RET_W + 2 * S5_W
OUT_AB = RET_W + S5_W
IN_C = 2 * GLA_QK + 2 * GLA_W + GLA_RANK
IN_C_PAD = 2 * GLA_QK + 2 * GLA_W + 128

ADAM_LR = 0.001
ADAM_B1 = 0.9
ADAM_B2 = 0.999
ADAM_EPS = 1e-08
ADAM_WD = 0.01
ADAM_STEP = 10

N_SHARD = 4
SMALL_COLS = 512

NN = (((1,), (0,)), ((), ()))
NT = (((1,), (1,)), ((), ()))
TN = (((0,), (0,)), ((), ()))


def _dot(a, b, dims=NN):
    return lax.dot_general(a.astype(BF16), b.astype(BF16), dims, preferred_element_type=F32)


def _mo(v, m):
    return v if isinstance(v, int) else pl.multiple_of(v, m)


def _sigmoid(x):
    return 1.0 / (1.0 + jnp.exp(-x))


def _row_tile(rows, cap):
    n = rows // CHUNK
    best = 1
    for d in range(1, n + 1):
        if n % d == 0 and d * CHUNK <= cap:
            best = d
    return best * CHUNK


def _col_tile(cols, cap):
    n = cols // 128
    best = 1
    for d in range(1, n + 1):
        if n % d == 0 and d * 128 <= cap:
            best = d
    return best * 128


def _matmul(name, a, b, dims, m, n, k, *, tm, tn, tk, out_dtype=F32, a_off=(0, 0), b_off=(0, 0),
            extras=(), epilogue=None, out_shape=None, out_spec=None, segs=None, into=None):
    if segs is None:
        segs = [(a, a_off, b, b_off, k, tk)]
    assert m % tm == 0 and n % tn == 0, (name, m, n, tm, tn)
    starts, counts = [], []
    nk = 0
    for (_, _, _, _, ks, tks) in segs:
        assert ks % tks == 0, (name, ks, tks)
        starts.append(nk)
        counts.append(ks // tks)
        nk += ks // tks
    in_specs, operands = [], []
    for s, (sa, (ar, ac), sb, (br, bc), _, tks) in enumerate(segs):
        def kpos(kk, st=starts[s], cnt=counts[s]):
            return jnp.clip(kk - st, 0, cnt - 1) if len(segs) > 1 else kk

        if dims == NN:
            a_spec = pl.BlockSpec((tm, tks), lambda i, j, kk, p=kpos, r=ar, c=ac: (i + r, p(kk) + c))
            b_spec = pl.BlockSpec((tks, tn), lambda i, j, kk, p=kpos, r=br, c=bc: (p(kk) + r, j + c))
        elif dims == NT:
            a_spec = pl.BlockSpec((tm, tks), lambda i, j, kk, p=kpos, r=ar, c=ac: (i + r, p(kk) + c))
            b_spec = pl.BlockSpec((tn, tks), lambda i, j, kk, p=kpos, r=br, c=bc: (j + r, p(kk) + c))
        else:
            a_spec = pl.BlockSpec((tks, tm), lambda i, j, kk, p=kpos, r=ar, c=ac: (p(kk) + r, i + c))
            b_spec = pl.BlockSpec((tks, tn), lambda i, j, kk, p=kpos, r=br, c=bc: (p(kk) + r, j + c))
        in_specs += [a_spec, b_spec]
        operands += [sa, sb]
    n_seg = len(segs)
    n_extra = len(extras)
    if out_shape is None:
        out_shape = jax.ShapeDtypeStruct((m, n), out_dtype)

    def body(*refs):
        e_refs = refs[2 * n_seg:2 * n_seg + n_extra]
        n_in = 2 * n_seg + n_extra + (1 if into is not None else 0)
        o_ref = refs[n_in]
        if nk == 1:
            part = _dot(refs[0][...], refs[1][...], dims)
            if epilogue is not None:
                part = epilogue(part, *[e[...] for e in e_refs])
            o_ref[...] = part.astype(o_ref.dtype)
            return
        acc_ref = refs[n_in + 1]
        kk = pl.program_id(2)

        @pl.when(kk == 0)
        def _():
            acc_ref[...] = jnp.zeros_like(acc_ref)

        if n_seg == 1:
            acc_ref[...] += _dot(refs[0][...], refs[1][...], dims)
        else:
            for s in range(n_seg):
                @pl.when(jnp.logical_and(kk >= starts[s], kk < starts[s] + counts[s]))
                def _(s=s):
                    acc_ref[...] += _dot(refs[2 * s][...], refs[2 * s + 1][...], dims)

        @pl.when(kk == nk - 1)
        def _():
            acc = acc_ref[...]
            if epilogue is not None:
                acc = epilogue(acc, *[e[...] for e in e_refs])
            o_ref[...] = acc.astype(o_ref.dtype)

    if out_spec is None:
        out_spec = pl.BlockSpec((tm, tn), lambda i, j, kk: (i, j))
    in_specs += [pl.BlockSpec(bs, im) for (_, bs, im) in extras]
    operands += [e for (e, _, _) in extras]
    aliases = {}
    if into is not None:
        dest, ro, co = into
        out_shape = jax.ShapeDtypeStruct(dest.shape, dest.dtype)
        out_spec = pl.BlockSpec((tm, tn), lambda i, j, kk: (i + ro, j + co))
        aliases = {len(operands): 0}
        in_specs.append(ANY)
        operands.append(dest)
    return pl.pallas_call(
        body, name=name, grid=(m // tm, n // tn, nk),
        in_specs=in_specs, out_specs=out_spec, out_shape=out_shape, input_output_aliases=aliases,
        scratch_shapes=[] if nk == 1 else [pltpu.VMEM((tm, tn), F32)],
        compiler_params=pltpu.CompilerParams(dimension_semantics=("parallel", "parallel", "arbitrary")),
    )(*operands)


def _rms_fwd(name, h, w):
    rows, d = h.shape
    tm = _row_tile(rows, 512)

    def body(h_ref, w_ref, o_ref):
        x = h_ref[...]
        r = lax.rsqrt(jnp.mean(x * x, axis=-1, keepdims=True) + EPS)
        o_ref[...] = (x * r * w_ref[...]).astype(BF16)

    return pl.pallas_call(
        body, name=name, grid=(rows // tm,),
        in_specs=[pl.BlockSpec((tm, d), lambda i: (i, 0)), pl.BlockSpec((1, d), lambda i: (0, 0))],
        out_specs=pl.BlockSpec((tm, d), lambda i: (i, 0)),
        out_shape=jax.ShapeDtypeStruct((rows, d), BF16),
    )(h, w)


def _rms_bwd(name, dhn, h, w, dres):
    rows, d = h.shape
    tm = _row_tile(rows, 384)

    def body(g_ref, h_ref, w_ref, r_ref, dh_ref, dw_ref):
        i = pl.program_id(0)
        x = h_ref[...]
        r = lax.rsqrt(jnp.mean(x * x, axis=-1, keepdims=True) + EPS)
        xh = x * r
        g = g_ref[...]
        gw = g * w_ref[...]
        dh_ref[...] = r_ref[...] + r * (gw - xh * jnp.mean(gw * xh, axis=-1, keepdims=True))

        @pl.when(i == 0)
        def _():
            dw_ref[...] = jnp.zeros_like(dw_ref)

        dw_ref[...] += jnp.sum(g * xh, axis=0, keepdims=True)

    return pl.pallas_call(
        body, name=name, grid=(rows // tm,),
        in_specs=[pl.BlockSpec((tm, d), lambda i: (i, 0)), pl.BlockSpec((tm, d), lambda i: (i, 0)),
                  pl.BlockSpec((1, d), lambda i: (0, 0)), pl.BlockSpec((tm, d), lambda i: (i, 0))],
        out_specs=[pl.BlockSpec((tm, d), lambda i: (i, 0)), pl.BlockSpec((1, d), lambda i: (0, 0))],
        out_shape=[jax.ShapeDtypeStruct((rows, d), F32), jax.ShapeDtypeStruct((1, d), F32)],
    )(dhn, h, w, dres)


def _embed_norm(x, meta, w):
    seq, d = x.shape
    rows = seq + CHUNK

    def body(x_ref, m_ref, w_ref, h_ref, o_ref):
        i = pl.program_id(0)

        def emit(h):
            h_ref[...] = h
            r = lax.rsqrt(jnp.mean(h * h, axis=-1, keepdims=True) + EPS)
            o_ref[...] = (h * r * w_ref[...]).astype(BF16)

        @pl.when(i == 0)
        def _():
            emit(jnp.concatenate([jnp.zeros((PAD, d), F32), m_ref[...]], axis=0))

        @pl.when(i > 0)
        def _():
            emit(x_ref[...])

    blk = pl.BlockSpec((CHUNK, d), lambda i: (i, 0))
    return pl.pallas_call(
        body, name="embed_norm_ab", grid=(rows // CHUNK,),
        in_specs=[pl.BlockSpec((CHUNK, d), lambda i: (jnp.maximum(i - 1, 0), 0)),
                  pl.BlockSpec((N_META, d), lambda i: (0, 0)), pl.BlockSpec((1, d), lambda i: (0, 0))],
        out_specs=[blk, blk],
        out_shape=[jax.ShapeDtypeStruct((rows, d), F32), jax.ShapeDtypeStruct((rows, d), BF16)],
    )(x, meta, w)


def _rms_bwd_embed(dhn, h, w, dres):
    rows, d = h.shape
    seq = rows - CHUNK

    def body(g_ref, h_ref, w_ref, r_ref, gx_ref, gm_ref, dw_ref):
        i = pl.program_id(0)
        x = h_ref[...]
        r = lax.rsqrt(jnp.mean(x * x, axis=-1, keepdims=True) + EPS)
        xh = x * r
        g = g_ref[...]
        gw = g * w_ref[...]
        dh = r_ref[...] + r * (gw - xh * jnp.mean(gw * xh, axis=-1, keepdims=True))

        @pl.when(i == 0)
        def _():
            dw_ref[...] = jnp.zeros_like(dw_ref)
            gm_ref[...] = dh[PAD:]

        @pl.when(i > 0)
        def _():
            gx_ref[...] = dh

        dw_ref[...] += jnp.sum(g * xh, axis=0, keepdims=True)

    blk = pl.BlockSpec((CHUNK, d), lambda i: (i, 0))
    return pl.pallas_call(
        body, name="norm_ab_bwd", grid=(rows // CHUNK,),
        in_specs=[blk, blk, pl.BlockSpec((1, d), lambda i: (0, 0)), blk],
        out_specs=[pl.BlockSpec((CHUNK, d), lambda i: (jnp.maximum(i - 1, 0), 0)),
                   pl.BlockSpec((N_META, d), lambda i: (0, 0)), pl.BlockSpec((1, d), lambda i: (0, 0))],
        out_shape=[jax.ShapeDtypeStruct((seq, d), F32), jax.ShapeDtypeStruct((N_META, d), F32),
                   jax.ShapeDtypeStruct((1, d), F32)],
    )(dhn, h, w, dres)


def _final_loss(h2, w, target):
    rows, d = h2.shape

    def body(h_ref, w_ref, t_ref, loss_ref, dh_ref, dw_ref):
        i = pl.program_id(0)

        @pl.when(i == 0)
        def _():
            loss_ref[...] = jnp.zeros_like(loss_ref)
            dw_ref[...] = jnp.zeros_like(dw_ref)
            dh_ref[...] = jnp.zeros_like(dh_ref)

        @pl.when(i > 0)
        def _():
            x = h_ref[...]
            r = lax.rsqrt(jnp.mean(x * x, axis=-1, keepdims=True) + EPS)
            xh = x * r
            wv = w_ref[...]
            err = xh * wv - t_ref[...]
            loss_ref[...] += 0.5 * jnp.sum(jnp.mean(err * err, axis=-1, keepdims=True), axis=0, keepdims=True)
            g = err * (1.0 / d)
            gw = g * wv
            dh_ref[...] = r * (gw - xh * jnp.mean(gw * xh, axis=-1, keepdims=True))
            dw_ref[...] += jnp.sum(g * xh, axis=0, keepdims=True)

    return pl.pallas_call(
        body, name="final_loss", grid=(rows // CHUNK,),
        in_specs=[pl.BlockSpec((CHUNK, d), lambda i: (i, 0)), pl.BlockSpec((1, d), lambda i: (0, 0)),
                  pl.BlockSpec((CHUNK, d), lambda i: (jnp.maximum(i - 1, 0), 0))],
        out_specs=[pl.BlockSpec((1, 1), lambda i: (0, 0)), pl.BlockSpec((CHUNK, d), lambda i: (i, 0)),
                   pl.BlockSpec((1, d), lambda i: (0, 0))],
        out_shape=[jax.ShapeDtypeStruct((1, 1), F32), jax.ShapeDtypeStruct((rows, d), F32),
                   jax.ShapeDtypeStruct((1, d), F32)],
    )(h2, w, target)


def _gate_fwd(o, z, w):
    rs = lax.rsqrt(jnp.mean(o * o, axis=-1, keepdims=True) + EPS)
    return o * rs * w * (z * _sigmoid(z))


def _gate_bwd(dout, o, z, w):
    rs = lax.rsqrt(jnp.mean(o * o, axis=-1, keepdims=True) + EPS)
    yn = o * rs
    sg = _sigmoid(z)
    sil = z * sg
    dsil = sg * (1.0 + z * (1.0 - sg))
    dz = dout * yn * w * dsil
    dyn = dout * w * sil
    dw = jnp.sum(dout * yn * sil, axis=0, keepdims=True)
    do = rs * (dyn - yn * jnp.mean(dyn * yn, axis=-1, keepdims=True))
    return do, dz, dw


def _rope(t, cosf, sinf):
    return t * cosf + pltpu.roll(t, RET_DK // 2, 1) * sinf


def _rope_t(d, cosf, sinf):
    return d * cosf + pltpu.roll(d * sinf, RET_DK // 2, 1)


def _ret_tables():
    log_g = jnp.log1p(-jnp.exp2(-5.0 - jnp.arange(RET_HEADS, dtype=F32)))
    idx = jnp.arange(CHUNK, dtype=F32)
    diff = idx[:, None] - idx[None, :]
    decay = jnp.where(diff >= 0, jnp.exp(log_g[:, None, None] * jnp.maximum(diff, 0.0)), 0.0)
    kw = jnp.exp(log_g[:, None] * (CHUNK - 1 - idx))
    qw = jnp.exp(log_g[:, None] * (idx + 1.0))
    gch = jnp.exp(log_g * CHUNK)
    kw = jnp.broadcast_to(kw[:, :, None], (RET_HEADS, CHUNK, RET_DK))
    qw = jnp.broadcast_to(qw[:, :, None], (RET_HEADS, CHUNK, RET_DK))
    gch = jnp.broadcast_to(gch[:, None, None], (RET_HEADS, 1, RET_DV))
    return decay, kw, qw, gch


def _rope_tables(rows):
    pos = jnp.arange(rows, dtype=F32) - float(PAD)
    inv_freq = jnp.power(ROPE_BASE, -jnp.arange(0, RET_DK, 2, dtype=F32) / RET_DK)
    ang = pos[:, None] * inv_freq[None, :]
    cos, sin = jnp.cos(ang), jnp.sin(ang)
    return jnp.concatenate([cos, cos], axis=1), jnp.concatenate([-sin, sin], axis=1)


RET_HB = 8
RET_QB = RET_HB * RET_DK
RET_VB = RET_HB * RET_DV


def _ret_in_specs(rev, nc):
    def cn(n):
        return (nc - 1 - n) if rev else n
    kb = RET_QK // RET_QB
    vb = 2 * RET_QK // RET_VB
    zb = (2 * RET_QK + RET_W) // RET_VB
    return [
        pl.BlockSpec((CHUNK, RET_QB), lambda h, n: (cn(n), h)),
        pl.BlockSpec((CHUNK, RET_QB), lambda h, n: (cn(n), kb + h)),
        pl.BlockSpec((CHUNK, RET_VB), lambda h, n: (cn(n), vb + h)),
        pl.BlockSpec((CHUNK, RET_VB), lambda h, n: (cn(n), zb + h)),
        pl.BlockSpec((CHUNK, RET_DK), lambda h, n: (cn(n), 0)),
        pl.BlockSpec((CHUNK, RET_DK), lambda h, n: (cn(n), 0)),
        pl.BlockSpec((RET_HB, CHUNK, CHUNK), lambda h, n: (h, 0, 0)),
        pl.BlockSpec((RET_HB, CHUNK, RET_DK), lambda h, n: (h, 0, 0)),
        pl.BlockSpec((RET_HB, CHUNK, RET_DK), lambda h, n: (h, 0, 0)),
        pl.BlockSpec((RET_HB, 1, RET_DV), lambda h, n: (h, 0, 0)),
        pl.BlockSpec((1, RET_VB), lambda h, n: (0, h)),
    ]


def _ret_fwd(proj, cosf, sinf, tables, normw):
    rows = proj.shape[0]
    nc = rows // CHUNK
    decay, kw, qw, gch = tables

    def body(q_ref, k_ref, v_ref, z_ref, cos_ref, sin_ref, dm_ref, kw_ref, qw_ref, g_ref, w_ref,
             o_ref, oa_ref, st_ref, s_scr):
        n = pl.program_id(1)

        @pl.when(n == 0)
        def _():
            s_scr[...] = jnp.zeros_like(s_scr)

        cosv, sinv = cos_ref[...], sin_ref[...]
        for hh in range(RET_HB):
            qc = slice(hh * RET_DK, (hh + 1) * RET_DK)
            vc = slice(hh * RET_DV, (hh + 1) * RET_DV)
            q = _rope(q_ref[:, qc], cosv, sinv)
            k = _rope(k_ref[:, qc], cosv, sinv) * (RET_DK ** -0.5)
            v = v_ref[:, vc]
            s = s_scr[hh]
            st_ref[hh, 0] = s.astype(BF16)
            a = _dot(q, k, NT) * dm_ref[hh]
            o = _dot(a, v) + _dot(q * qw_ref[hh], s)
            s_scr[hh] = s * g_ref[hh] + _dot(k * kw_ref[hh], v, TN)
            o_ref[:, vc] = o
            oa_ref[:, vc] = _gate_fwd(o, z_ref[:, vc], w_ref[:, vc]).astype(BF16)

    return pl.pallas_call(
        body, name="ret_fwd", grid=(RET_HEADS // RET_HB, nc),
        in_specs=_ret_in_specs(False, nc),
        out_specs=[pl.BlockSpec((CHUNK, RET_VB), lambda h, n: (n, h)),
                   pl.BlockSpec((CHUNK, RET_VB), lambda h, n: (n, h)),
                   pl.BlockSpec((RET_HB, 1, RET_DK, RET_DV), lambda h, n: (h, n, 0, 0))],
        out_shape=[jax.ShapeDtypeStruct((rows, RET_W), F32), jax.ShapeDtypeStruct((rows, RET_W), BF16),
                   jax.ShapeDtypeStruct((RET_HEADS, nc, RET_DK, RET_DV), BF16)],
        scratch_shapes=[pltpu.VMEM((RET_HB, RET_DK, RET_DV), F32)],
        compiler_params=pltpu.CompilerParams(dimension_semantics=("parallel", "arbitrary")),
    )(proj, proj, proj, proj, cosf, sinf, decay, kw, qw, gch, normw)


def _ret_bwd(proj, cosf, sinf, tables, normw, o_ret, dmix, states):
    assert RET_HB == RET_HEADS
    rows = proj.shape[0]
    nc = rows // CHUNK
    decay, kw, qw, gch = tables
    ret_cols = 2 * RET_QK + 2 * RET_W

    def rn(n):
        return nc - 1 - n

    def body(q_ref, k_ref, v_ref, z_ref, cos_ref, sin_ref, dm_ref, kw_ref, qw_ref, g_ref, w_ref,
             o_ref, do_ref, st_ref, dp_ref, dw_ref, ds_scr):
        n = pl.program_id(1)
        dq_ref = dp_ref.at[:, 0:RET_QK]
        dk_ref = dp_ref.at[:, RET_QK:2 * RET_QK]
        dv_ref = dp_ref.at[:, 2 * RET_QK:2 * RET_QK + RET_W]
        dz_ref = dp_ref.at[:, 2 * RET_QK + RET_W:ret_cols]

        @pl.when(n == 0)
        def _():
            ds_scr[...] = jnp.zeros_like(ds_scr)
            dw_ref[...] = jnp.zeros_like(dw_ref)

        cosv, sinv = cos_ref[...], sin_ref[...]
        for hh in range(RET_HB):
            qc = slice(hh * RET_DK, (hh + 1) * RET_DK)
            vc = slice(hh * RET_DV, (hh + 1) * RET_DV)
            q = _rope(q_ref[:, qc], cosv, sinv)
            k = _rope(k_ref[:, qc], cosv, sinv) * (RET_DK ** -0.5)
            v = v_ref[:, vc]
            do, dz, dw = _gate_bwd(do_ref[:, vc], o_ref[:, vc], z_ref[:, vc], w_ref[:, vc])
            dz_ref[:, vc] = dz.astype(BF16)
            dw_ref[hh] += dw
            dm = dm_ref[hh]
            s = st_ref[hh, 0]
            g1 = ds_scr[hh]
            p = _dot(q, k, NT) * dm
            kwv = k * kw_ref[hh]
            qwv = q * qw_ref[hh]
            dp = _dot(do, v, NT)
            da = dp * dm
            dv = _dot(p, do, TN) + _dot(kwv, g1)
            dq = _dot(da, k) + _dot(do, s, NT) * qw_ref[hh]
            dk = _dot(da, q, TN) + _dot(v, g1, NT) * kw_ref[hh]
            ds_scr[hh] = g1 * g_ref[hh] + _dot(qwv, do, TN)
            dv_ref[:, vc] = dv.astype(BF16)
            dq_ref[:, qc] = _rope_t(dq, cosv, sinv).astype(BF16)
            dk_ref[:, qc] = _rope_t(dk * (RET_DK ** -0.5), cosv, sinv).astype(BF16)

    in_specs = _ret_in_specs(True, nc) + [
        pl.BlockSpec((CHUNK, RET_VB), lambda h, n: (rn(n), h)),
        pl.BlockSpec((CHUNK, RET_VB), lambda h, n: (rn(n), h)),
        pl.BlockSpec((RET_HB, 1, RET_DK, RET_DV), lambda h, n: (h, rn(n), 0, 0)),
    ]
    return pl.pallas_call(
        body, name="ret_bwd", grid=(RET_HEADS // RET_HB, nc),
        in_specs=in_specs,
        out_specs=[pl.BlockSpec((CHUNK, ret_cols), lambda h, n: (rn(n), 0)),
                   pl.BlockSpec((RET_HB, 1, RET_DV), lambda h, n: (h, 0, 0))],
        out_shape=[jax.ShapeDtypeStruct((rows, IN_AB), BF16), jax.ShapeDtypeStruct((RET_HEADS, 1, RET_DV), F32)],
        scratch_shapes=[pltpu.VMEM((RET_HB, RET_DK, RET_DV), F32)],
        compiler_params=pltpu.CompilerParams(dimension_semantics=("parallel", "arbitrary")),
    )(proj, proj, proj, proj, cosf, sinf, decay, kw, qw, gch, normw, o_ret, dmix, states)


def _s5_discretize(lam_re, lam_im, log_dt, b_re, b_im):
    dt = jnp.exp(log_dt)[:, None]
    mag = jnp.exp(lam_re * dt)
    ab_re, ab_im = mag * jnp.cos(lam_im * dt), mag * jnp.sin(lam_im * dt)
    den = lam_re * lam_re + lam_im * lam_im
    nr, ni = ab_re - 1.0, ab_im
    f_re = (nr * lam_re + ni * lam_im) / den
    f_im = (ni * lam_re - nr * lam_im) / den
    bb_re = f_re[..., None] * b_re - f_im[..., None] * b_im
    bb_im = f_re[..., None] * b_im + f_im[..., None] * b_re
    return ab_re, ab_im, bb_re, bb_im


def _bdiag_in(bb):
    t = bb.reshape(S5_NT, S5_TG, S5_P, S5_GH).transpose(0, 1, 3, 2)
    eye = jnp.eye(S5_TG, dtype=bb.dtype)
    full = t[:, :, :, None, :] * eye[None, :, None, :, None]
    return full.reshape(S5_NT, S5_TU, S5_TS)


def _bdiag_in_extract(dense):
    t = dense.reshape(S5_NT, S5_TG, S5_GH, S5_TG, S5_P)
    diag = jnp.stack([t[:, g, :, g, :] for g in range(S5_TG)], axis=1)
    return diag.transpose(0, 1, 3, 2).reshape(S5_G, S5_P, S5_GH)


def _bdiag_out(c):
    t = c.reshape(S5_NT, S5_TG, S5_GH, S5_P).transpose(0, 1, 3, 2)
    eye = jnp.eye(S5_TG, dtype=c.dtype)
    full = t[:, :, :, None, :] * eye[None, :, None, :, None]
    return full.reshape(S5_NT, S5_TS, S5_TU)


def _bdiag_out_extract(dense):
    t = dense.reshape(S5_NT, S5_TG, S5_P, S5_TG, S5_GH)
    diag = jnp.stack([t[:, g, :, g, :] for g in range(S5_TG)], axis=1)
    return diag.transpose(0, 1, 3, 2).reshape(S5_G, S5_GH, S5_P)


def _cmul(ar, ai, br, bi):
    return ar * br - ai * bi, ar * bi + ai * br


S5_SEG = 8
S5_STEPS = CHUNK // S5_SEG


def _seg_perm(x):
    c = x.shape[1]
    return jnp.swapaxes(x.reshape(S5_SEG, S5_STEPS, c), 0, 1).reshape(CHUNK, c)


def _seg_unperm(x):
    c = x.shape[1]
    return jnp.swapaxes(x.reshape(S5_STEPS, S5_SEG, c), 0, 1).reshape(CHUNK, c)


def _rows(x, p):
    return x[p * S5_SEG:(p + 1) * S5_SEG]


def _s5_tables(ar, ai, tr_scr, ti_scr, wfr_scr, wfi_scr, wbr_scr, wbi_scr):
    row = lax.broadcasted_iota(jnp.int32, (S5_SEG, 1), 0)
    a8r = jnp.broadcast_to(ar, (S5_SEG, S5_TS))
    a8i = jnp.broadcast_to(ai, (S5_SEG, S5_TS))
    pr, pi = a8r, a8i
    for p in range(S5_STEPS):
        tr_scr[p * S5_SEG:(p + 1) * S5_SEG, :] = pr
        ti_scr[p * S5_SEG:(p + 1) * S5_SEG, :] = pi
        if p < S5_STEPS - 1:
            pr, pi = _cmul(pr, pi, a8r, a8i)
    wr, wi = pr, pi
    sh = 1
    while sh < S5_SEG:
        keep = row >= sh
        sr = jnp.where(keep, pltpu.roll(wr, sh, 0), 1.0)
        si = jnp.where(keep, pltpu.roll(wi, sh, 0), 0.0)
        wr, wi = _cmul(wr, wi, sr, si)
        sh *= 2
    wfr_scr[...] = wr
    wfi_scr[...] = wi
    wr, wi = pr, -pi
    sh = 1
    while sh < S5_SEG:
        keep = row < S5_SEG - sh
        sr = jnp.where(keep, pltpu.roll(wr, S5_SEG - sh, 0), 1.0)
        si = jnp.where(keep, pltpu.roll(wi, S5_SEG - sh, 0), 0.0)
        wr, wi = _cmul(wr, wi, sr, si)
        sh *= 2
    wbr_scr[...] = wr
    wbi_scr[...] = wi


def _seg_scan(vr, vi, ar, ai, tr_scr, ti_scr, wr_scr, wi_scr, c0r, c0i, down):
    row = lax.broadcasted_iota(jnp.int32, (S5_SEG, 1), 0)
    sgn = 1.0 if down else -1.0
    order = list(range(S5_STEPS)) if down else list(range(S5_STEPS - 1, -1, -1))
    xr, xi = _rows(vr, order[0]), _rows(vi, order[0])
    loc = {order[0]: (xr, xi)}
    for p in order[1:]:
        mr, mi = _cmul(ar, sgn * ai, xr, xi)
        xr, xi = mr + _rows(vr, p), mi + _rows(vi, p)
        loc[p] = (xr, xi)
    last = S5_STEPS - 1
    mr, mi = tr_scr[last * S5_SEG:(last + 1) * S5_SEG, :], sgn * ti_scr[last * S5_SEG:(last + 1) * S5_SEG, :]
    er, ei = xr, xi
    sh = 1
    while sh < S5_SEG:
        if down:
            keep = row >= sh
            sr, si = pltpu.roll(er, sh, 0), pltpu.roll(ei, sh, 0)
        else:
            keep = row < S5_SEG - sh
            sr, si = pltpu.roll(er, S5_SEG - sh, 0), pltpu.roll(ei, S5_SEG - sh, 0)
        pr, pi = _cmul(mr, mi, jnp.where(keep, sr, 0.0), jnp.where(keep, si, 0.0))
        er, ei = er + pr, ei + pi
        mr, mi = _cmul(mr, mi, mr, mi)
        sh *= 2
    pr, pi = _cmul(wr_scr[...], wi_scr[...], c0r, c0i)
    er, ei = er + pr, ei + pi
    if down:
        nr = jnp.where(row == 0, c0r, pltpu.roll(er, 1, 0))
        ni = jnp.where(row == 0, c0i, pltpu.roll(ei, 1, 0))
    else:
        nr = jnp.where(row == S5_SEG - 1, c0r, pltpu.roll(er, S5_SEG - 1, 0))
        ni = jnp.where(row == S5_SEG - 1, c0i, pltpu.roll(ei, S5_SEG - 1, 0))
    out_r, out_i = [], []
    for p in range(S5_STEPS):
        q = p if down else S5_STEPS - 1 - p
        pr, pi = _cmul(tr_scr[q * S5_SEG:(q + 1) * S5_SEG, :], sgn * ti_scr[q * S5_SEG:(q + 1) * S5_SEG, :], nr, ni)
        out_r.append(loc[p][0] + pr)
        out_i.append(loc[p][1] + pi)
    return jnp.concatenate(out_r, axis=0), jnp.concatenate(out_i, axis=0), (nr, ni), (er, ei)


def _gelu(y):
    c = math.sqrt(2.0 / math.pi)
    return 0.5 * y * (1.0 + jnp.tanh(c * (y + 0.044715 * y * y * y)))


def _gelu_grad(y):
    c = math.sqrt(2.0 / math.pi)
    th = jnp.tanh(c * (y + 0.044715 * y * y * y))
    return 0.5 * (1.0 + th) + 0.5 * y * (1.0 - th * th) * c * (1.0 + 3.0 * 0.044715 * y * y)


def _s5_fwd(proj, ab, bd_b, bd_c, dvec):
    rows = proj.shape[0]
    nc = rows // CHUNK
    tps = S5_FWD_TILES
    ubw = tps * S5_TU
    ub = (2 * RET_QK + 2 * RET_W) // ubw
    ab_re, ab_im = ab
    bre, bim = bd_b
    cre, cim = bd_c

    def body(u_ref, ar_ref, ai_ref, bre_ref, bim_ref, cre_ref, cim_ref, d_ref,
             y_ref, g_ref, er_ref, ei_ref, tr_scr, ti_scr, wfr_scr, wfi_scr, wbr_scr, wbi_scr,
             cr_scr, ci_scr, er_scr, ei_scr):
        n = pl.program_id(1)
        for tt in range(tps):
            cols = slice(tt * S5_TU, (tt + 1) * S5_TU)
            ar, ai = ar_ref[tt], ai_ref[tt]
            trs, tis, wfr, wfi = tr_scr.at[tt], ti_scr.at[tt], wfr_scr.at[tt], wfi_scr.at[tt]

            @pl.when(n == 0)
            def _(tt=tt, ar=ar, ai=ai, trs=trs, tis=tis, wfr=wfr, wfi=wfi):
                _s5_tables(ar, ai, trs, tis, wfr, wfi, wbr_scr.at[tt], wbi_scr.at[tt])
                cr_scr[tt] = jnp.zeros((S5_SEG, S5_TS), F32)
                ci_scr[tt] = jnp.zeros((S5_SEG, S5_TS), F32)

            u = _seg_perm(u_ref[:, cols])
            c0r, c0i = cr_scr[tt], ci_scr[tt]
            er_ref[tt, 0] = c0r
            ei_ref[tt, 0] = c0i
            xr, xi, _, (er, ei) = _seg_scan(_dot(u, bre_ref[tt]), _dot(u, bim_ref[tt]), ar, ai, trs, tis,
                                            wfr, wfi, c0r, c0i, True)
            er_scr[tt] = er
            ei_scr[tt] = ei
            cr_scr[tt] = jnp.broadcast_to(er_scr[tt, S5_SEG - 1:S5_SEG, :], (S5_SEG, S5_TS))
            ci_scr[tt] = jnp.broadcast_to(ei_scr[tt, S5_SEG - 1:S5_SEG, :], (S5_SEG, S5_TS))
            y = _seg_unperm(_dot(xr, cre_ref[tt]) - _dot(xi, cim_ref[tt]) + d_ref[:, cols] * u)
            y_ref[:, cols] = y
            g_ref[:, cols] = _gelu(y).astype(BF16)

    vec = pl.BlockSpec((tps, 1, S5_TS), lambda t, n: (t, 0, 0))
    return pl.pallas_call(
        body, name="s5_fwd", grid=(S5_NT // tps, nc),
        in_specs=[pl.BlockSpec((CHUNK, ubw), lambda t, n: (n, ub + t)), vec, vec,
                  pl.BlockSpec((tps, S5_TU, S5_TS), lambda t, n: (t, 0, 0)),
                  pl.BlockSpec((tps, S5_TU, S5_TS), lambda t, n: (t, 0, 0)),
                  pl.BlockSpec((tps, S5_TS, S5_TU), lambda t, n: (t, 0, 0)),
                  pl.BlockSpec((tps, S5_TS, S5_TU), lambda t, n: (t, 0, 0)),
                  pl.BlockSpec((1, ubw), lambda t, n: (0, t))],
        out_specs=[pl.BlockSpec((CHUNK, ubw), lambda t, n: (n, t)),
                   pl.BlockSpec((CHUNK, ubw), lambda t, n: (n, t)),
                   pl.BlockSpec((tps, 1, 8, S5_TS), lambda t, n: (t, n, 0, 0)),
                   pl.BlockSpec((tps, 1, 8, S5_TS), lambda t, n: (t, n, 0, 0))],
        out_shape=[jax.ShapeDtypeStruct((rows, S5_W), F32), jax.ShapeDtypeStruct((rows, S5_W), BF16),
                   jax.ShapeDtypeStruct((S5_NT, nc, 8, S5_TS), F32),
                   jax.ShapeDtypeStruct((S5_NT, nc, 8, S5_TS), F32)],
        scratch_shapes=[pltpu.VMEM((tps, CHUNK, S5_TS), F32) for _ in range(2)]
        + [pltpu.VMEM((tps, S5_SEG, S5_TS), F32) for _ in range(8)],
        compiler_params=pltpu.CompilerParams(dimension_semantics=("parallel", "arbitrary")),
    )(proj, ab_re.reshape(S5_NT, 1, S5_TS), ab_im.reshape(S5_NT, 1, S5_TS), bre, bim, cre, cim, dvec)


def _s5_bwd(proj, dy, ab, bd_b, bd_c, dvec, entry, dproj):
    rows = proj.shape[0]
    nc = rows // CHUNK
    tps = S5_BWD_TILES
    ubw = tps * S5_TU
    ub = (2 * RET_QK + 2 * RET_W) // ubw
    ab_re, ab_im = ab
    bre, bim = bd_b
    cre, cim = bd_c
    er, ei = entry

    def rn(n):
        return nc - 1 - n

    def body(u_ref, dy_ref, ar_ref, ai_ref, bre_ref, bim_ref, cre_ref, cim_ref, d_ref, er_ref, ei_ref, dp_ref,
             du_ref, dbr_ref, dbi_ref, dcr_ref, dci_ref, dar_ref, dai_ref, dd_ref,
             tr_scr, ti_scr, wfr_scr, wfi_scr, wbr_scr, wbi_scr, gr_scr, gi_scr, er_scr, ei_scr):
        n = pl.program_id(1)

        @pl.when(n == 0)
        def _():
            gr_scr[...] = jnp.zeros_like(gr_scr)
            gi_scr[...] = jnp.zeros_like(gi_scr)
            for r in (dbr_ref, dbi_ref, dcr_ref, dci_ref, dar_ref, dai_ref, dd_ref):
                r[...] = jnp.zeros_like(r)

        for tt in range(tps):
            cols = slice(tt * S5_TU, (tt + 1) * S5_TU)
            ar, ai = ar_ref[tt], ai_ref[tt]
            trs, tis = tr_scr.at[tt], ti_scr.at[tt]

            @pl.when(n == 0)
            def _(tt=tt, ar=ar, ai=ai, trs=trs, tis=tis):
                _s5_tables(ar, ai, trs, tis, wfr_scr.at[tt], wfi_scr.at[tt], wbr_scr.at[tt], wbi_scr.at[tt])

            u = _seg_perm(u_ref[:, cols])
            dy = _seg_perm(dy_ref[:, cols])
            xr, xi, (pr, pi), _ = _seg_scan(_dot(u, bre_ref[tt]), _dot(u, bim_ref[tt]), ar, ai, trs, tis,
                                            wfr_scr.at[tt], wfi_scr.at[tt], er_ref[tt, 0], ei_ref[tt, 0], True)
            dcr_ref[tt] += _dot(xr, dy, TN)
            dci_ref[tt] -= _dot(xi, dy, TN)
            gr, gi, _, (er, ei) = _seg_scan(_dot(dy, cre_ref[tt], NT), -_dot(dy, cim_ref[tt], NT), ar, ai, trs, tis,
                                            wbr_scr.at[tt], wbi_scr.at[tt], gr_scr[tt], gi_scr[tt], False)
            er_scr[tt] = er
            ei_scr[tt] = ei
            gr_scr[tt] = jnp.broadcast_to(er_scr[tt, 0:1, :], (S5_SEG, S5_TS))
            gi_scr[tt] = jnp.broadcast_to(ei_scr[tt, 0:1, :], (S5_SEG, S5_TS))
            xpr = jnp.concatenate([pr, xr[:CHUNK - S5_SEG]], axis=0)
            xpi = jnp.concatenate([pi, xi[:CHUNK - S5_SEG]], axis=0)
            dar_ref[tt] += jnp.sum((xpr * gr + xpi * gi).reshape(S5_STEPS, S5_SEG, S5_TS), axis=0)
            dai_ref[tt] += jnp.sum((xpr * gi - xpi * gr).reshape(S5_STEPS, S5_SEG, S5_TS), axis=0)
            dbr_ref[tt] += _dot(u, gr, TN)
            dbi_ref[tt] += _dot(u, gi, TN)
            dd_ref[tt] += jnp.sum((dy * u).reshape(S5_STEPS, S5_SEG, S5_TU), axis=0)
            du = dy * d_ref[:, cols] + _dot(gr, bre_ref[tt], NT) + _dot(gi, bim_ref[tt], NT)
            du_ref[:, cols] = _seg_unperm(du).astype(BF16)

    vec = pl.BlockSpec((tps, 1, S5_TS), lambda t, n: (t, 0, 0))
    acc_b = pl.BlockSpec((tps, S5_TU, S5_TS), lambda t, n: (t, 0, 0))
    acc_c = pl.BlockSpec((tps, S5_TS, S5_TU), lambda t, n: (t, 0, 0))
    acc_a = pl.BlockSpec((tps, 8, S5_TS), lambda t, n: (t, 0, 0))
    ent = pl.BlockSpec((tps, 1, 8, S5_TS), lambda t, n: (t, rn(n), 0, 0))
    return pl.pallas_call(
        body, name="s5_bwd", grid=(S5_NT // tps, nc),
        in_specs=[pl.BlockSpec((CHUNK, ubw), lambda t, n: (rn(n), ub + t)),
                  pl.BlockSpec((CHUNK, ubw), lambda t, n: (rn(n), t)), vec, vec,
                  acc_b, acc_b, acc_c, acc_c, pl.BlockSpec((1, ubw), lambda t, n: (0, t)), ent, ent, ANY],
        out_specs=[pl.BlockSpec((CHUNK, ubw), lambda t, n: (rn(n), ub + t)), acc_b, acc_b, acc_c, acc_c, acc_a, acc_a,
                   pl.BlockSpec((tps, 8, S5_TU), lambda t, n: (t, 0, 0))],
        input_output_aliases={11: 0},
        out_shape=[jax.ShapeDtypeStruct(dproj.shape, BF16),
                   jax.ShapeDtypeStruct((S5_NT, S5_TU, S5_TS), F32), jax.ShapeDtypeStruct((S5_NT, S5_TU, S5_TS), F32),
                   jax.ShapeDtypeStruct((S5_NT, S5_TS, S5_TU), F32), jax.ShapeDtypeStruct((S5_NT, S5_TS, S5_TU), F32),
                   jax.ShapeDtypeStruct((S5_NT, 8, S5_TS), F32), jax.ShapeDtypeStruct((S5_NT, 8, S5_TS), F32),
                   jax.ShapeDtypeStruct((S5_NT, 8, S5_TU), F32)],
        scratch_shapes=[pltpu.VMEM((tps, CHUNK, S5_TS), F32) for _ in range(2)]
        + [pltpu.VMEM((tps, S5_SEG, S5_TS), F32) for _ in range(8)],
        compiler_params=pltpu.CompilerParams(dimension_semantics=("parallel", "arbitrary")),
    )(proj, dy, ab_re.reshape(S5_NT, 1, S5_TS), ab_im.reshape(S5_NT, 1, S5_TS), bre, bim, cre, cim, dvec, er, ei,
      dproj)


def _s5_gate_bwd(dmix, g, t, proj, dproj):
    rows = g.shape[0]
    tm = _row_tile(rows, 384)
    ob = RET_W // S5_W
    zb = (2 * RET_QK + 2 * RET_W + S5_W) // S5_W

    def body(do_ref, g_ref, t_ref, z_ref, dp_ref, dz_ref, dt_ref, dg_ref):
        do = do_ref[...]
        gv = g_ref[...].astype(F32)
        z = z_ref[...]
        st = _sigmoid(t_ref[...])
        sg = _sigmoid(z)
        os5 = gv * st
        dz_ref[...] = (do * os5 * sg * (1.0 + z * (1.0 - sg))).astype(BF16)
        dos = do * z * sg
        dt_ref[...] = (dos * gv * st * (1.0 - st)).astype(BF16)
        dg_ref[...] = dos * st

    blk = pl.BlockSpec((tm, S5_W), lambda i: (i, 0))
    return pl.pallas_call(
        body, name="s5_gate_bwd", grid=(rows // tm,),
        in_specs=[pl.BlockSpec((tm, S5_W), lambda i: (i, ob)), blk, blk,
                  pl.BlockSpec((tm, S5_W), lambda i: (i, zb)), ANY],
        out_specs=[pl.BlockSpec((tm, S5_W), lambda i: (i, zb)), blk, blk],
        out_shape=[jax.ShapeDtypeStruct(dproj.shape, BF16), jax.ShapeDtypeStruct((rows, S5_W), BF16),
                   jax.ShapeDtypeStruct((rows, S5_W), F32)],
        input_output_aliases={4: 0},
    )(dmix, g, t, proj, dproj)


def _split3(x):
    hi = x.astype(BF16)
    r = x - hi.astype(F32)
    mid = r.astype(BF16)
    lo = (r - mid.astype(F32)).astype(BF16)
    return hi, mid, lo


def _tri_sum(x, upper):
    i = lax.broadcasted_iota(jnp.int32, (CHUNK, CHUNK), 0)
    j = lax.broadcasted_iota(jnp.int32, (CHUNK, CHUNK), 1)
    tri = jnp.where((j >= i) if upper else (j <= i), 1.0, 0.0).astype(BF16)
    hi, mid, lo = _split3(x)
    return _dot(tri, lo) + _dot(tri, mid) + _dot(tri, hi)


def _gla_log_decay(gl, wg, bg, n):
    logit = _dot(gl, wg) + bg
    la = (jnp.minimum(logit, 0.0) - jnp.log(1.0 + jnp.exp(-jnp.abs(logit)))) * (1.0 / GLA_TAU)
    row = lax.broadcasted_iota(jnp.int32, (CHUNK, 1), 0)
    live = jnp.logical_or(n > 0, row >= PAD)
    return logit, jnp.where(live, la, 0.0), live


def _gla_in_specs(rev, nc):
    def cn(n):
        return (nc - 1 - n) if rev else n
    kb = GLA_QK // GLA_DK
    vb = 2 * GLA_QK // GLA_DV
    zb = (2 * GLA_QK + GLA_W) // GLA_DV
    gb = (2 * GLA_QK + 2 * GLA_W) // 128
    return [
        pl.BlockSpec((CHUNK, GLA_DK), lambda h, n: (cn(n), h)),
        pl.BlockSpec((CHUNK, GLA_DK), lambda h, n: (cn(n), kb + h)),
        pl.BlockSpec((CHUNK, GLA_DV), lambda h, n: (cn(n), vb + h)),
        pl.BlockSpec((CHUNK, GLA_DV), lambda h, n: (cn(n), zb + h)),
        pl.BlockSpec((CHUNK, 128), lambda h, n: (cn(n), gb)),
        pl.BlockSpec((128, GLA_DK), lambda h, n: (0, h)),
        pl.BlockSpec((1, GLA_DK), lambda h, n: (0, h)),
        pl.BlockSpec((1, GLA_DV), lambda h, n: (0, h)),
    ]


def _gla_fwd(proj, wgate, bgate, normw):
    rows = proj.shape[0]
    nc = rows // CHUNK

    def body(q_ref, k_ref, v_ref, z_ref, gl_ref, wg_ref, bg_ref, w_ref, o_ref, oc_ref, st_ref, s_scr, b_scr):
        n = pl.program_id(1)

        @pl.when(n == 0)
        def _():
            s_scr[...] = jnp.zeros_like(s_scr)

        q = q_ref[...] * (GLA_DK ** -0.5)
        k = k_ref[...]
        v = v_ref[...]
        vb = v.astype(BF16)
        _, la, _ = _gla_log_decay(gl_ref[...], wg_ref[...], bg_ref[...], n)
        b = _tri_sum(la, False)
        b_scr[...] = b
        b_last = b_scr[CHUNK - 1:CHUNK, :]
        st = s_scr[...]
        st_ref[0, 0] = st
        s_scr[...] = st * jnp.exp(b_last) + _dot(v, k * jnp.exp(b_last - b), TN)
        rowc = lax.broadcasted_iota(jnp.int32, (CHUNK, 1), 0)
        rows16 = lax.broadcasted_iota(jnp.int32, (SUB, 1), 0)
        a_tot = jnp.zeros((CHUNK, CHUNK), F32)
        for s in range(1, NSUB):
            lo = s * SUB
            bref = b_scr[lo - 1:lo, :]
            in_s = jnp.logical_and(rowc >= lo, rowc < lo + SUB)
            qh = q * jnp.exp(jnp.where(in_s, b - bref, -1e30))
            kh = k * jnp.exp(jnp.where(rowc < lo, bref - b, -1e30))
            a_tot = a_tot + _dot(qh, kh, NT)
        lane = lax.broadcasted_iota(jnp.int32, (SUB, CHUNK), 1)
        diag = []
        for s in range(NSUB):
            lo = s * SUB
            qs, bs = q[lo:lo + SUB], b[lo:lo + SUB]
            s_blk = jnp.zeros((SUB, CHUNK), F32)
            for j in range(SUB):
                r = lo + j
                e = jnp.exp(jnp.where(rows16 >= j, bs - b_scr[r:r + 1, :], -1e30))
                col = jnp.sum(qs * k_ref[r:r + 1, :] * e, axis=1, keepdims=True)
                s_blk = jnp.where(lane == r, col, s_blk)
            diag.append(s_blk)
        o = _dot(q * jnp.exp(b), st, NT) + _dot(a_tot + jnp.concatenate(diag, axis=0), vb)
        o_ref[...] = o
        oc_ref[...] = _gate_fwd(o, z_ref[...], w_ref[...]).astype(BF16)

    return pl.pallas_call(
        body, name="gla_fwd", grid=(GLA_HEADS, nc),
        in_specs=_gla_in_specs(False, nc),
        out_specs=[pl.BlockSpec((CHUNK, GLA_DV), lambda h, n: (n, h)),
                   pl.BlockSpec((CHUNK, GLA_DV), lambda h, n: (n, h)),
                   pl.BlockSpec((1, 1, GLA_DV, GLA_DK), lambda h, n: (h, n, 0, 0))],
        out_shape=[jax.ShapeDtypeStruct((rows, GLA_W), F32), jax.ShapeDtypeStruct((rows, GLA_W), BF16),
                   jax.ShapeDtypeStruct((GLA_HEADS, nc, GLA_DV, GLA_DK), F32)],
        scratch_shapes=[pltpu.VMEM((GLA_DV, GLA_DK), F32), pltpu.VMEM((CHUNK, GLA_DK), F32)],
        compiler_params=pltpu.CompilerParams(dimension_semantics=("parallel", "arbitrary")),
    )(proj, proj, proj, proj, proj, wgate, bgate, normw)


def _gla_bwd(proj, wgate, bgate, normw, o_gla, d_oc, states):
    rows = proj.shape[0]
    nc = rows // CHUNK

    def rn(n):
        return nc - 1 - n

    def body(q_ref, k_ref, v_ref, z_ref, gl_ref, wg_ref, bg_ref, w_ref, o_ref, do_ref, st_ref,
             dq_ref, dk_ref, dv_ref, dz_ref, dl_ref, dw_ref, dbg_ref,
             ds_scr, dq_scr, dk_scr, dv_scr, db_scr, b_scr, q_scr):
        n = pl.program_id(1)
        cn = rn(n)

        @pl.when(n == 0)
        def _():
            ds_scr[...] = jnp.zeros_like(ds_scr)
            dw_ref[...] = jnp.zeros_like(dw_ref)
            dbg_ref[...] = jnp.zeros_like(dbg_ref)

        q = q_ref[...] * (GLA_DK ** -0.5)
        k = k_ref[...]
        v = v_ref[...]
        vb = v.astype(BF16)
        do, dz, dw = _gate_bwd(do_ref[...], o_ref[...], z_ref[...], w_ref[...])
        dz_ref[...] = dz.astype(BF16)
        dw_ref[0] += dw
        logit, la, live = _gla_log_decay(gl_ref[...], wg_ref[...], bg_ref[...], cn)
        b = _tri_sum(la, False)
        b_scr[...] = b
        b_last = b_scr[CHUNK - 1:CHUNK, :]
        e_last = jnp.exp(b_last)
        st = st_ref[0, 0]
        g1 = ds_scr[...]
        eb = jnp.exp(b)
        qe = q * eb
        dqe = _dot(do, st)
        dq_scr[...] = dqe * eb
        db_scr[...] = dqe * qe
        ekb = jnp.exp(b_last - b)
        kdec = k * ekb
        dkdec = _dot(v, g1)
        dv_scr[...] = _dot(kdec, g1, NT)
        dk_scr[...] = dkdec * ekb
        wk = dkdec * kdec
        db_scr[...] -= wk
        dbl = jnp.sum(wk, axis=0, keepdims=True) + jnp.sum(g1 * st, axis=0, keepdims=True) * e_last
        ds_scr[...] = g1 * e_last + _dot(do, qe, TN)
        rowc = lax.broadcasted_iota(jnp.int32, (CHUNK, 1), 0)
        rows16 = lax.broadcasted_iota(jnp.int32, (SUB, 1), 0)
        da_full = _dot(do, vb, NT)
        a_tot = jnp.zeros((CHUNK, CHUNK), F32)
        for s in range(1, NSUB):
            lo = s * SUB
            bref = b_scr[lo - 1:lo, :]
            in_s = jnp.logical_and(rowc >= lo, rowc < lo + SUB)
            eq = jnp.exp(jnp.where(in_s, b - bref, -1e30))
            ek = jnp.exp(jnp.where(rowc < lo, bref - b, -1e30))
            qh = q * eq
            kh = k * ek
            a_tot = a_tot + _dot(qh, kh, NT)
            da = jnp.where(in_s, da_full, 0.0)
            dqh = _dot(da, kh)
            dkh = _dot(da, qh, TN)
            tq = dqh * qh
            tk = dkh * kh
            dq_scr[...] += dqh * eq
            dk_scr[...] += dkh * ek
            db_scr[...] += tq - tk
            db_scr[lo - 1:lo, :] += jnp.sum(tk, axis=0, keepdims=True) - jnp.sum(tq, axis=0, keepdims=True)
        dat_full = _dot(vb, do, NT)
        q_scr[...] = q
        lane = lax.broadcasted_iota(jnp.int32, (SUB, CHUNK), 1)
        diag = []
        for s in range(NSUB):
            lo = s * SUB
            qs, ks, bs = q[lo:lo + SUB], k[lo:lo + SUB], b[lo:lo + SUB]
            da_blk, dat_blk = da_full[lo:lo + SUB], dat_full[lo:lo + SUB]
            dqs = jnp.zeros((SUB, GLA_DK), F32)
            dks = jnp.zeros((SUB, GLA_DK), F32)
            dbs = jnp.zeros((SUB, GLA_DK), F32)
            s_blk = jnp.zeros((SUB, CHUNK), F32)
            for j in range(SUB):
                r = lo + j
                kj = k_ref[r:r + 1, :]
                e = jnp.exp(jnp.where(rows16 >= j, bs - b_scr[r:r + 1, :], -1e30))
                p = qs * e * kj
                s_blk = jnp.where(lane == r, jnp.sum(p, axis=1, keepdims=True), s_blk)
                dcol = jnp.sum(jnp.where(lane == r, da_blk, 0.0), axis=1, keepdims=True)
                dqs = dqs + (dcol * e) * kj
                dbs = dbs + dcol * p
            for i in range(SUB):
                r = lo + i
                e = jnp.exp(jnp.where(rows16 <= i, b_scr[r:r + 1, :] - bs, -1e30))
                drow = jnp.sum(jnp.where(lane == r, dat_blk, 0.0), axis=1, keepdims=True)
                nq = (drow * e) * q_scr[r:r + 1, :]
                dks = dks + nq
                dbs = dbs - nq * ks
            dq_scr[lo:lo + SUB, :] += dqs
            dk_scr[lo:lo + SUB, :] += dks
            db_scr[lo:lo + SUB, :] += dbs
            diag.append(s_blk)
        dv_scr[...] += _dot(a_tot + jnp.concatenate(diag, axis=0), do, TN)
        db_scr[CHUNK - 1:CHUNK, :] += dbl
        dla = _tri_sum(db_scr[...], True)
        dlogit = jnp.where(live, dla * (1.0 / GLA_TAU) * _sigmoid(-logit), 0.0)
        dl_ref[...] = dlogit
        dbg_ref[0] += jnp.sum(dlogit, axis=0, keepdims=True)
        dq_ref[...] = (dq_scr[...] * (GLA_DK ** -0.5)).astype(BF16)
        dk_ref[...] = dk_scr[...].astype(BF16)
        dv_ref[...] = dv_scr[...].astype(BF16)

    in_specs = _gla_in_specs(True, nc) + [
        pl.BlockSpec((CHUNK, GLA_DV), lambda h, n: (rn(n), h)),
        pl.BlockSpec((CHUNK, GLA_DV), lambda h, n: (rn(n), h)),
        pl.BlockSpec((1, 1, GLA_DV, GLA_DK), lambda h, n: (h, rn(n), 0, 0)),
    ]
    return pl.pallas_call(
        body, name="gla_bwd", grid=(GLA_HEADS, nc),
        in_specs=in_specs,
        out_specs=[pl.BlockSpec((CHUNK, GLA_DK), lambda h, n: (rn(n), h)),
                   pl.BlockSpec((CHUNK, GLA_DK), lambda h, n: (rn(n), h)),
                   pl.BlockSpec((CHUNK, GLA_DV), lambda h, n: (rn(n), h)),
                   pl.BlockSpec((CHUNK, GLA_DV), lambda h, n: (rn(n), h)),
                   pl.BlockSpec((CHUNK, GLA_DK), lambda h, n: (rn(n), h)),
                   pl.BlockSpec((1, 1, GLA_DV), lambda h, n: (h, 0, 0)),
                   pl.BlockSpec((1, 1, GLA_DK), lambda h, n: (h, 0, 0))],
        out_shape=[jax.ShapeDtypeStruct((rows, GLA_QK), BF16), jax.ShapeDtypeStruct((rows, GLA_QK), BF16),
                   jax.ShapeDtypeStruct((rows, GLA_W), BF16), jax.ShapeDtypeStruct((rows, GLA_W), BF16),
                   jax.ShapeDtypeStruct((rows, GLA_QK), F32),
                   jax.ShapeDtypeStruct((GLA_HEADS, 1, GLA_DV), F32),
                   jax.ShapeDtypeStruct((GLA_HEADS, 1, GLA_DK), F32)],
        scratch_shapes=[pltpu.VMEM((GLA_DV, GLA_DK), F32), pltpu.VMEM((CHUNK, GLA_DK), F32),
                        pltpu.VMEM((CHUNK, GLA_DK), F32), pltpu.VMEM((CHUNK, GLA_DV), F32),
                        pltpu.VMEM((CHUNK, GLA_DK), F32), pltpu.VMEM((CHUNK, GLA_DK), F32),
                        pltpu.VMEM((CHUNK, GLA_DK), F32)],
        compiler_params=pltpu.CompilerParams(dimension_semantics=("parallel", "arbitrary")),
    )(proj, proj, proj, proj, proj, wgate, bgate, normw, o_gla, d_oc, states)


def _adamw(name, w, g, m, v):
    rows, cols = w.shape
    tm = 8
    for cand in range(8, rows + 1, 8):
        if rows % cand == 0 and cand * cols * 4 <= 2 ** 21:
            tm = cand
    c1 = 1.0 - ADAM_B1 ** ADAM_STEP
    c2 = 1.0 - ADAM_B2 ** ADAM_STEP

    def body(w_ref, g_ref, m_ref, v_ref, d_ref, nm_ref, nv_ref):
        gv = g_ref[...]
        nm = ADAM_B1 * m_ref[...] + (1.0 - ADAM_B1) * gv
        nv = ADAM_B2 * v_ref[...] + (1.0 - ADAM_B2) * (gv * gv)
        nm_ref[...] = nm
        nv_ref[...] = nv
        d_ref[...] = -ADAM_LR * ((nm / c1) / (jnp.sqrt(nv / c2) + ADAM_EPS) + ADAM_WD * w_ref[...])

    blk = pl.BlockSpec((tm, cols), lambda i: (i, 0))
    return pl.pallas_call(
        body, name=name, grid=(rows // tm,),
        in_specs=[blk] * 4, out_specs=[blk] * 3,
        out_shape=[jax.ShapeDtypeStruct((rows, cols), F32)] * 3,
    )(w, g, m, v)


def _place():
    x, y, c = lax.axis_index("x"), lax.axis_index("y"), lax.axis_index("c")
    chips = [(1 - x, y), (x, 1 - y), (1 - x, 1 - y)]
    return x, y, c, chips


ANY = pl.BlockSpec(memory_space=pl.ANY)


def _gathered_struct(shape, dtype, kind):
    r, cc = shape
    if kind == "row":
        return jax.ShapeDtypeStruct((N_SHARD * r, cc), dtype)
    if kind == "col":
        return jax.ShapeDtypeStruct((r, N_SHARD * cc), dtype)
    return jax.ShapeDtypeStruct((N_SHARD, r, cc), dtype)


def _cast_place(name, w, kind, mine_arr, dtype, also_own=False):
    r, cc = w.shape
    tr = r
    for cand in (256, 128, 64, 32, 16):
        if r % cand == 0:
            tr = cand
            break
    nb = r // tr
    if kind == "row":
        o_spec = pl.BlockSpec((tr, cc), lambda i, m: (m[0] * nb + i, 0))
    elif kind == "col":
        o_spec = pl.BlockSpec((tr, cc), lambda i, m: (i, m[0]))
    else:
        o_spec = pl.BlockSpec((None, tr, cc), lambda i, m: (m[0], i, 0))
    w_spec = pl.BlockSpec((tr, cc), lambda i, m: (i, 0))

    def body(m_ref, w_ref, o_ref, *own_ref):
        o_ref[...] = w_ref[...].astype(o_ref.dtype)
        for ref in own_ref:
            ref[...] = w_ref[...].astype(ref.dtype)

    out_specs, out_shape = [o_spec], [_gathered_struct((r, cc), dtype, kind)]
    if also_own:
        out_specs.append(w_spec)
        out_shape.append(jax.ShapeDtypeStruct((r, cc), dtype))
    out = pl.pallas_call(
        body, name=name,
        grid_spec=pltpu.PrefetchScalarGridSpec(
            num_scalar_prefetch=1, grid=(nb,), in_specs=[w_spec], out_specs=out_specs),
        out_shape=out_shape,
    )(mine_arr, w)
    return out if also_own else out[0]


def _gather_small(shard):
    rows, cols = shard.shape

    def body(in_ref, out_ref, send_sems, recv_sems):
        x, y, c, chips = _place()
        mine = 2 * x + y
        out_ref[mine] = in_ref[...]
        cps = []
        for j, chip in enumerate(chips):
            cp = pltpu.make_async_remote_copy(
                src_ref=in_ref, dst_ref=out_ref.at[mine], send_sem=send_sems.at[j], recv_sem=recv_sems.at[j],
                device_id=(*chip, c), device_id_type=MESH)
            cp.start()
            cps.append(cp)
        for cp in cps:
            cp.wait()

    vm = pl.BlockSpec(memory_space=pltpu.VMEM)
    return pl.pallas_call(
        body, name="gather_small",
        in_specs=[vm], out_specs=vm,
        out_shape=jax.ShapeDtypeStruct((N_SHARD, rows, cols), F32),
        scratch_shapes=[pltpu.SemaphoreType.DMA((3,)), pltpu.SemaphoreType.DMA((3,))],
        compiler_params=pltpu.CompilerParams(has_side_effects=True),
    )(shard)


def _in_proj_shifted(name, a, b, n, shifts, tm, tn, out_cols, into=None):
    m, k = a.shape
    nb_b = b.shape[1] // tn
    nb_o = out_cols // tn

    def body(s_ref, a_ref, b_ref, *rest):
        rest[-1][...] = _dot(a_ref[...], b_ref[...])

    in_specs = [pl.BlockSpec((tm, k), lambda i, j, s: (i, 0)),
                pl.BlockSpec((k, tn), lambda i, j, s: (0, (s[0] + j) % nb_b))]
    operands = [shifts, a, b]
    aliases = {}
    if into is not None:
        in_specs.append(ANY)
        operands.append(into)
        aliases = {3: 0}
    return pl.pallas_call(
        body, name=name,
        grid_spec=pltpu.PrefetchScalarGridSpec(
            num_scalar_prefetch=1, grid=(m // tm, n // tn), in_specs=in_specs,
            out_specs=pl.BlockSpec((tm, tn), lambda i, j, s: (i, (s[1] + j) % nb_o))),
        out_shape=jax.ShapeDtypeStruct((m, out_cols), F32), input_output_aliases=aliases,
    )(*operands)


def _allreduce_small(buf):
    rows, cols = buf.shape

    def body(in_ref, out_ref, sib_ref, pair_ref, far_ref, send_sems, recv_sems):
        x, y, c, chips = _place()
        sibling = (x, y, 1 - c)
        to_sib = pltpu.make_async_remote_copy(
            src_ref=in_ref, dst_ref=sib_ref, send_sem=send_sems.at[0], recv_sem=recv_sems.at[0],
            device_id=sibling, device_id_type=MESH)
        to_sib.start()
        to_sib.wait()
        pair_ref[...] = in_ref[...] + sib_ref[...]
        far = [pltpu.make_async_remote_copy(
            src_ref=pair_ref, dst_ref=far_ref.at[j], send_sem=send_sems.at[1 + j], recv_sem=recv_sems.at[1 + j],
            device_id=(*chip, c), device_id_type=MESH) for j, chip in enumerate(chips)]
        for cp in far:
            cp.start()
        for cp in far:
            cp.wait()
        out_ref[...] = (pair_ref[...] + far_ref[1]) + (far_ref[0] + far_ref[2])

    vm = pl.BlockSpec(memory_space=pltpu.VMEM)
    return pl.pallas_call(
        body, name="allreduce_small",
        in_specs=[vm], out_specs=vm,
        out_shape=jax.ShapeDtypeStruct((rows, cols), F32),
        scratch_shapes=[pltpu.VMEM((rows, cols), F32), pltpu.VMEM((rows, cols), F32),
                        pltpu.VMEM((3, rows, cols), F32),
                        pltpu.SemaphoreType.DMA((4,)), pltpu.SemaphoreType.DMA((4,))],
        compiler_params=pltpu.CompilerParams(has_side_effects=True),
    )(buf)


def _shard_window(ref, kind, shard_shape, shard, half):
    r, cc = shard_shape
    hr = r // 2
    if kind == "row":
        return ref.at[pl.ds(_mo(shard * r + half * hr, 8), hr), :]
    if kind == "col":
        return ref.at[pl.ds(_mo(half * hr, 8), hr), pl.ds(_mo(shard * cc, 128), cc)]
    if kind == "colw":
        return ref.at[pl.ds(_mo(half * hr, 8), hr), pl.ds(_mo(shard * (cc - 128), 128), cc)]
    return ref.at[shard, pl.ds(_mo(half * hr, 8), hr), :]


HBM = pl.BlockSpec(memory_space=pltpu.HBM)
SEM = pl.BlockSpec(memory_space=pltpu.SEMAPHORE)
DATAFLOW = pltpu.SideEffectType.DATAFLOW_SIDE_EFFECTING


def _in_hbm(a):
    return pltpu.with_memory_space_constraint(a, pltpu.HBM)


def _empty_hbm(shape, dtype):
    return _in_hbm(lax.empty(shape, dtype))


def _copies_start(name, bufs, n_copies, plan, carry):
    nb = len(bufs)

    def body(*refs):
        send_sems, recv_sems = refs[nb + 1], refs[nb + 2]
        for k, (src, dst, to) in enumerate(plan(refs[:nb])):
            pltpu.make_async_remote_copy(src_ref=src, dst_ref=dst, send_sem=send_sems.at[k], recv_sem=recv_sems.at[k],
                                         device_id=to, device_id_type=MESH).start()

    passed = list(bufs) + [carry]
    out = pl.pallas_call(
        body, name=name,
        in_specs=[HBM] * (nb + 1), out_specs=[SEM, SEM] + [HBM] * (nb + 1),
        out_shape=[pltpu.SemaphoreType.DMA((n_copies,)), pltpu.SemaphoreType.DMA((n_copies,))]
        + [pltpu.HBM(a.shape, a.dtype) for a in passed],
        input_output_aliases={i: 2 + i for i in range(nb + 1)},
        compiler_params=pltpu.CompilerParams(has_side_effects=DATAFLOW),
    )(*[_in_hbm(a) for a in passed])
    return out[0], out[1], list(out[2:2 + nb]), out[2 + nb]


def _copies_wait(name, send_sems, recv_sems, bufs, plan, after):
    nb = len(bufs)
    after = list(after) if isinstance(after, (list, tuple)) else [after]

    def body(*refs):
        send, recv = refs[nb], refs[nb + 1]
        for k, (src, dst, to) in enumerate(plan(refs[:nb])):
            cp = pltpu.make_async_remote_copy(src_ref=src, dst_ref=dst, send_sem=send.at[k], recv_sem=recv.at[k],
                                              device_id=to, device_id_type=MESH)
            cp.wait_send()
            cp.wait_recv()

    out = pl.pallas_call(
        body, name=name,
        in_specs=[HBM] * nb + [SEM, SEM] + [ANY] * len(after), out_specs=[HBM] * nb,
        out_shape=[pltpu.HBM(a.shape, a.dtype) for a in bufs],
        input_output_aliases={i: i for i in range(nb)},
        compiler_params=pltpu.CompilerParams(has_side_effects=DATAFLOW),
    )(*bufs, send_sems, recv_sems, *after)
    return list(out)


def _gather_ici_plan(shard_shapes, kinds):
    n_arr = len(kinds)

    def plan(refs):
        x, y, c, chips = _place()
        out = []
        for i in range(n_arr):
            w = _shard_window(refs[i], kinds[i], shard_shapes[i], 2 * x + y, c)
            out += [(w, w, (*chip, c)) for chip in chips]
        return out

    return plan


def _gather_d2d_plan(shard_shapes, kinds):
    n_arr = len(kinds)

    def plan(refs):
        x, y, c, chips = _place()
        out = []
        for i in range(n_arr):
            for chip in chips:
                w = _shard_window(refs[i], kinds[i], shard_shapes[i], 2 * chip[0] + chip[1], c)
                out.append((w, w, (x, y, 1 - c)))
        return out

    return plan


def _rs_pair_plan(kinds, shard_shapes):
    n_arr = len(kinds)

    def plan(refs):
        x, y, c, _ = _place()
        out = []
        for i in range(n_arr):
            for s in range(N_SHARD):
                out.append((_shard_window(refs[i], kinds[i], shard_shapes[i], s, 1 - c), refs[n_arr + i].at[s],
                            (x, y, 1 - c)))
        return out

    return plan


def _rs_chip_plan(n_arr):
    def plan(refs):
        x, y, c, chips = _place()
        out = []
        for i in range(n_arr):
            for j, chip in enumerate(chips):
                out.append((refs[i].at[2 * chip[0] + chip[1]], refs[n_arr + i].at[j], (*chip, c)))
        return out

    return plan


def _rs_pair_add(name, grad, got, kind, shard_shape, c):
    r, cc = shard_shape
    hr = r // 2
    tr = hr
    for cand in (256, 128, 64, 32, 16):
        if hr % cand == 0:
            tr = cand
            break
    nb = hr // tr

    def body(c_ref, g_ref, t_ref, p_ref, pb_ref):
        p = g_ref[...] + t_ref[...]
        p_ref[...] = p
        pb_ref[...] = p.astype(BF16)

    out_shape = [jax.ShapeDtypeStruct((N_SHARD, hr, cc), F32), jax.ShapeDtypeStruct((N_SHARD, hr, cc), BF16)]
    if kind == "colw":
        tiles = cc // 128
        tr = hr
        g_spec = pl.BlockSpec((tr, 128), lambda s, t, cr: (cr[0], s * (tiles - 1) + t))
        t_spec = pl.BlockSpec((None, tr, 128), lambda s, t, cr: (s, 0, t))
        return pl.pallas_call(
            body, name=name,
            grid_spec=pltpu.PrefetchScalarGridSpec(
                num_scalar_prefetch=1, grid=(N_SHARD, tiles), in_specs=[g_spec, t_spec], out_specs=[t_spec, t_spec]),
            out_shape=out_shape,
        )(c, grad, got)
    if kind == "row":
        g_spec = pl.BlockSpec((tr, cc), lambda s, i, cr: (s * 2 * nb + cr[0] * nb + i, 0))
    elif kind == "col":
        g_spec = pl.BlockSpec((tr, cc), lambda s, i, cr: (cr[0] * nb + i, s))
    else:
        g_spec = pl.BlockSpec((None, tr, cc), lambda s, i, cr: (s, cr[0] * nb + i, 0))
    t_spec = pl.BlockSpec((None, tr, cc), lambda s, i, cr: (s, i, 0))
    return pl.pallas_call(
        body, name=name,
        grid_spec=pltpu.PrefetchScalarGridSpec(
            num_scalar_prefetch=1, grid=(N_SHARD, nb),
            in_specs=[g_spec, t_spec], out_specs=[t_spec, t_spec]),
        out_shape=out_shape,
    )(c, grad, got)


def _rs_chip_add(name, pair_f32, got, shard_shape, mine_c):
    r, cc = shard_shape
    hr = r // 2
    tr = hr
    for cand in (256, 128, 64, 32, 16):
        if hr % cand == 0:
            tr = cand
            break
    nb = hr // tr

    def body(mc_ref, p_ref, t0_ref, t1_ref, t2_ref, o_ref):
        o_ref[...] = (p_ref[...] + t1_ref[...].astype(F32)) + (t0_ref[...].astype(F32) + t2_ref[...].astype(F32))

    def far(j):
        return pl.BlockSpec((None, tr, cc), lambda i, mc: (j, i, 0))

    return pl.pallas_call(
        body, name=name,
        grid_spec=pltpu.PrefetchScalarGridSpec(
            num_scalar_prefetch=1, grid=(nb,),
            in_specs=[pl.BlockSpec((None, tr, cc), lambda i, mc: (mc[0], i, 0)), far(0), far(1), far(2)],
            out_specs=pl.BlockSpec((tr, cc), lambda i, mc: (mc[1] * nb + i, 0))),
        out_shape=jax.ShapeDtypeStruct((r, cc), F32),
    )(mine_c, pair_f32, got, got, got)


def _rs_pair_share(name, halves, shard_shapes):
    n_arr = len(halves)

    def body(*refs):
        ins = refs[:n_arr]
        outs = refs[n_arr:2 * n_arr]
        send_sems, recv_sems = refs[2 * n_arr:]
        x, y, c, _ = _place()
        sibling = (x, y, 1 - c)
        cps = []
        for i in range(n_arr):
            hr = shard_shapes[i][0] // 2
            rows = pl.ds(_mo(c * hr, 8), hr)
            cp = pltpu.make_async_remote_copy(
                src_ref=outs[i].at[rows, :], dst_ref=outs[i].at[rows, :],
                send_sem=send_sems.at[i], recv_sem=recv_sems.at[i],
                device_id=sibling, device_id_type=MESH)
            cp.start()
            cps.append(cp)
        for cp in cps:
            cp.wait()

    return pl.pallas_call(
        body, name=name,
        in_specs=[ANY] * n_arr, out_specs=[ANY] * n_arr,
        out_shape=[jax.ShapeDtypeStruct(s, F32) for s in shard_shapes],
        input_output_aliases={i: i for i in range(n_arr)},
        scratch_shapes=[pltpu.SemaphoreType.DMA((n_arr,)), pltpu.SemaphoreType.DMA((n_arr,))],
        compiler_params=pltpu.CompilerParams(has_side_effects=True),
    )(*halves)


def _pack(arrays):
    flat = []
    for a in arrays:
        v = a.reshape(-1).astype(F32)
        flat.append(jnp.pad(v, (0, (-v.shape[0]) % SMALL_COLS)))
    buf = jnp.concatenate(flat).reshape(-1, SMALL_COLS)
    return jnp.pad(buf, ((0, (-buf.shape[0]) % 16), (0, 0)))


def _unpack(buf, shapes):
    out = []
    row = 0
    for s in shapes:
        size = math.prod(s)
        nrow = -(-size // SMALL_COLS)
        out.append(buf[row:row + nrow].reshape(-1)[:size].reshape(s))
        row += nrow
    return out


def kernel(x, meta, norm_ab_w, w_in_ab, ret_norm_w, s5_lam_re, s5_lam_im, s5_log_dt, s5_b_re, s5_b_im, s5_c_re, s5_c_im, s5_d, s5_w_glu, w_out_ab, norm_c_w, w_in_c, gla_w_gate, gla_b_gate, gla_norm_w, w_out_c, final_norm_w, loss_target, m_meta, m_norm_ab_w, m_w_in_ab, m_ret_norm_w, m_s5_lam_re, m_s5_lam_im, m_s5_log_dt, m_s5_b_re, m_s5_b_im, m_s5_c_re, m_s5_c_im, m_s5_d, m_s5_w_glu, m_w_out_ab, m_norm_c_w, m_w_in_c, m_gla_w_gate, m_gla_b_gate, m_gla_norm_w, m_w_out_c, m_final_norm_w, v_meta, v_norm_ab_w, v_w_in_ab, v_ret_norm_w, v_s5_lam_re, v_s5_lam_im, v_s5_log_dt, v_s5_b_re, v_s5_b_im, v_s5_c_re, v_s5_c_im, v_s5_d, v_s5_w_glu, v_w_out_ab, v_norm_c_w, v_w_in_c, v_gla_w_gate, v_gla_b_gate, v_gla_norm_w, v_w_out_c, v_final_norm_w):
    seq = x.shape[1]
    rows = seq + CHUNK
    xi, yi, ci = lax.axis_index("x"), lax.axis_index("y"), lax.axis_index("c")
    mine = 2 * xi + yi
    c_arr = jnp.reshape(ci, (1,)).astype(jnp.int32)
    mine_c = jnp.stack([mine, ci]).astype(jnp.int32)

    mine_arr = jnp.reshape(mine, (1,)).astype(jnp.int32)
    small_shard = _pack([meta, norm_c_w, gla_norm_w, gla_b_gate, gla_w_gate[0]])
    small_all = _gather_small(small_shard)
    first_kinds = ["col"]
    first_shapes = [w_in_ab.shape[1:]]
    first_ici = _gather_ici_plan(first_shapes, first_kinds)
    first_d2d = _gather_d2d_plan(first_shapes, first_kinds)
    wab_buf, wab_own = _cast_place("place_w_in_ab", w_in_ab[0], "col", mine_arr, BF16, also_own=True)
    f_send, f_recv, f_bufs, small_all = _copies_start("gather_first_ici_start", [wab_buf], 3, first_ici, small_all)
    late =[("w_out_ab", w_out_ab[0]), ("w_in_c", w_in_c[0]), ("w_out_c", w_out_c[0]), ("w_glu", s5_w_glu[0])]
    late_kinds = ["row", "stack", "row", "row"]
    late_shapes = [a.shape for _, a in late]
    ici_plan = _gather_ici_plan(late_shapes, late_kinds)
    d2d_plan = _gather_d2d_plan(late_shapes, late_kinds)
    n_late = 3 * len(late)
    g_bufs = [_cast_place("place_" + nm, a, kd, mine_arr, BF16) for (nm, a), kd in zip(late, late_kinds)]
    cosf, sinf = _rope_tables(rows)
    rtab = _ret_tables()
    ab_re, ab_im, bb_re, bb_im = _s5_discretize(s5_lam_re[0], s5_lam_im[0], s5_log_dt[0], s5_b_re[0], s5_b_im[0])
    ab = (ab_re, ab_im)
    bd_b = (_bdiag_in(bb_re), _bdiag_in(bb_im))
    bd_c = (_bdiag_out(s5_c_re[0]), _bdiag_out(s5_c_im[0]))
    q4 = D_MODEL // N_SHARD
    g4 = GLA_QK // N_SHARD
    parts = [_unpack(small_all[j], [(N_META, q4), (1, q4), (1, q4), (1, g4), (GLA_RANK, g4)]) for j in range(N_SHARD)]
    meta_f, norm_c_f, gla_norm_f, bgate_f, wgate_f = [jnp.concatenate([p[i] for p in parts], axis=1) for i in range(5)]
    wgate_pad = jnp.pad(wgate_f, ((0, 128 - GLA_RANK), (0, 0)))

    h0, hn0 = _embed_norm(x[0], meta_f, norm_ab_w)

    tm = _row_tile(rows, 1408)
    tmk = _row_tile(rows, 1408)
    own_blocks = (IN_AB // N_SHARD) // 512
    shift_own = jnp.stack([jnp.zeros((), jnp.int32), mine.astype(jnp.int32) * own_blocks])
    shift_rest = jnp.stack([(mine.astype(jnp.int32) + 1) * own_blocks, (mine.astype(jnp.int32) + 1) * own_blocks])
    proj0 = _in_proj_shifted("in_proj_ab_own", hn0, wab_own, IN_AB // N_SHARD, shift_own, tm, 512, IN_AB)
    f_bufs = _copies_wait("gather_first_ici_wait", f_send, f_recv, f_bufs, first_ici,
                          [proj0, cosf, sinf, bd_b[0], bd_b[1], bd_c[0], bd_c[1]] + g_bufs + list(rtab))
    f_send, f_recv, f_bufs, cosf = _copies_start("gather_first_d2d_start", f_bufs, 3, first_d2d, cosf)
    wab, = _copies_wait("gather_first_d2d_wait", f_send, f_recv, f_bufs, first_d2d, cosf)
    g_send, g_recv, g_bufs, wab = _copies_start("gather_late_ici_start", g_bufs, n_late, ici_plan, wab)
    proj0 = _in_proj_shifted("in_proj_ab_rest", hn0, wab, IN_AB - IN_AB // N_SHARD, shift_rest, tm, 512, IN_AB,
                             into=proj0)
    o_ret, o_a, ret_states = _ret_fwd(proj0, cosf, sinf, rtab, ret_norm_w)
    g_bufs = _copies_wait("gather_late_ici_wait", g_send, g_recv, g_bufs, ici_plan, o_a)
    g_send, g_recv, g_bufs, proj0 = _copies_start("gather_late_d2d_start", g_bufs, n_late, d2d_plan, proj0)
    y_s5, g_s5, s5_er, s5_ei = _s5_fwd(proj0, ab, bd_b, bd_c, s5_d)
    wout_ab, wc_st, wout_c, wglu = _copies_wait("gather_late_d2d_wait", g_send, g_recv, g_bufs, d2d_plan, g_s5)
    wc = jnp.concatenate([wc_st[j] for j in range(N_SHARD)] + [jnp.zeros((D_MODEL, IN_C_PAD - IN_C), BF16)], axis=1)
    zb_blk = (2 * RET_QK + 2 * RET_W + S5_W) // 512

    def glu_out(acc, gv, z):
        return gv.astype(F32) * _sigmoid(acc) * (z * _sigmoid(z))

    t_glu = _matmul("glu", g_s5, wglu, NN, rows, S5_W, S5_W, tm=tm, tn=512, tk=S5_W)
    o_b = _matmul("glu_out", g_s5, wglu, NN, rows, S5_W, S5_W, tm=tm, tn=512, tk=S5_W, out_dtype=BF16,
                  extras=[(g_s5, (tm, 512), lambda i, j, kk: (i, j)),
                          (proj0, (tm, 512), lambda i, j, kk: (i, zb_blk + j))],
                  epilogue=glu_out)
    h1 = _matmul("out_proj_ab", None, None, NN, rows, D_MODEL, OUT_AB, tm=tm, tn=512, tk=1024,
                 segs=[(o_a, (0, 0), wout_ab, (0, 0), RET_W, 1024),
                       (o_b, (0, 0), wout_ab, (RET_W // 1024, 0), S5_W, 1024)],
                 extras=[(h0, (tm, 512), lambda i, j, kk: (i, j))], epilogue=lambda acc, r: acc + r)

    hn1 = _rms_fwd("norm_c", h1, norm_c_f)
    proj1 = _matmul("in_proj_c", hn1, wc, NN, rows, IN_C_PAD, D_MODEL, tm=tm, tn=896, tk=D_MODEL)
    o_gla, o_c, gla_states = _gla_fwd(proj1, wgate_pad, bgate_f, gla_norm_f)
    h2 = _matmul("out_proj_c", o_c, wout_c, NN, rows, D_MODEL, GLA_W, tm=tm, tn=512, tk=GLA_W,
                 extras=[(h1, (tm, 512), lambda i, j, kk: (i, j))], epilogue=lambda acc, r: acc + r)
    loss_dev, dh2, d_final = _final_loss(h2, final_norm_w.reshape(1, D_MODEL), loss_target[0])

    g_wout_c = _matmul("d_w_out_c", o_c, dh2, TN, GLA_W, D_MODEL, rows, tm=1024, tn=1024, tk=tmk)
    d_oc = _matmul("d_o_c", dh2, wout_c, NT, rows, GLA_W, D_MODEL, tm=tm, tn=512, tk=1024)
    dq1, dk1, dv1, dz1, dlogit, d_gla_norm, d_bgate = _gla_bwd(proj1, wgate_pad, bgate_f, gla_norm_f, o_gla, d_oc, gla_states)
    gl_blk = (2 * GLA_QK + 2 * GLA_W) // 128
    dgl = _matmul("d_g_low", dlogit, wgate_pad, NT, rows, 128, GLA_QK, tm=tm, tn=128, tk=GLA_QK, out_dtype=BF16)
    g_wgate = _matmul("d_w_gate", proj1, dlogit, TN, 128, GLA_QK, rows, tm=128, tn=GLA_QK, tk=tmk, a_off=(0, gl_blk))
    dproj1 = jnp.concatenate([dq1, dk1, dv1, dz1, dgl], axis=1)
    g_wc = _matmul("d_w_in_c", hn1, dproj1, TN, D_MODEL, IN_C_PAD, rows, tm=1024, tn=896, tk=tmk)
    dhn1 = _matmul("d_hn1", dproj1, wc, NT, rows, D_MODEL, IN_C_PAD, tm=tm, tn=512, tk=896)
    dh1, d_norm_c = _rms_bwd("norm_c_bwd", dhn1, h1, norm_c_f, dh2)

    g_wout_ab = _matmul("d_w_out_ab_a", o_a, dh1, TN, RET_W, D_MODEL, rows, tm=1024, tn=1024, tk=tmk,
                        out_shape=jax.ShapeDtypeStruct((OUT_AB, D_MODEL), F32))
    g_wout_ab = _matmul("d_w_out_ab_b", o_b, dh1, TN, S5_W, D_MODEL, rows, tm=1024, tn=1024, tk=tmk,
                        into=(g_wout_ab, RET_W // 1024, 0))
    dmix = _matmul("d_mix", dh1, wout_ab, NT, rows, OUT_AB, D_MODEL, tm=tm, tn=512, tk=1024)
    dproj0, d_ret_norm = _ret_bwd(proj0, cosf, sinf, rtab, ret_norm_w, o_ret, dmix, ret_states)
    dproj0, dt_glu, dg_direct = _s5_gate_bwd(dmix, g_s5, t_glu, proj0, dproj0)
    g_wglu = _matmul("d_w_glu", g_s5, dt_glu, TN, S5_W, S5_W, rows, tm=1024, tn=1024, tk=tmk)
    dy_s5 = _matmul("d_y_s5", dt_glu, wglu, NT, rows, S5_W, S5_W, tm=tm, tn=512, tk=S5_W,
                    extras=[(dg_direct, (tm, 512), lambda i, j, kk: (i, j)),
                            (y_s5, (tm, 512), lambda i, j, kk: (i, j))],
                    epilogue=lambda acc, dg, yv: (acc + dg) * _gelu_grad(yv))
    wc_cols = IN_C // N_SHARD
    wc_win = (wc_cols // 128 + 1) * 128
    rs1_names = ["w_out_ab", "w_in_c", "w_out_c", "w_glu"]
    rs1_kinds = ["row", "colw", "row", "row"]
    rs1_shapes = [w_out_ab.shape[1:], (D_MODEL, wc_win), w_out_c.shape[1:], s5_w_glu.shape[1:]]
    rs1_plan = _rs_pair_plan(rs1_kinds, rs1_shapes)
    rs1_land = [_empty_hbm((N_SHARD, r // 2, cc), F32) for (r, cc) in rs1_shapes]
    p_send, p_recv, p_bufs, dy_s5 = _copies_start("rs1_pair_start", [g_wout_ab, g_wc, g_wout_c, g_wglu] + rs1_land,
                                                  N_SHARD * 4, rs1_plan, dy_s5)
    dproj0, dbr_d, dbi_d, dcr_d, dci_d, dar_p, dai_p, dd_p = _s5_bwd(proj0, dy_s5, ab, bd_b, bd_c, s5_d,
                                                                     (s5_er, s5_ei), dproj0)
    p_bufs = _copies_wait("rs1_pair_wait", p_send, p_recv, p_bufs, rs1_plan, dproj0)
    rs1_pairs = [_rs_pair_add("rs_pair_add_" + nm, g, t, kd, ss, c_arr)
                 for nm, g, t, kd, ss in zip(rs1_names, p_bufs[:4], p_bufs[4:], rs1_kinds, rs1_shapes)]
    rs1_chip_plan = _rs_chip_plan(4)
    rs1_land2 = [_empty_hbm((3, r // 2, cc), BF16) for (r, cc) in rs1_shapes]
    c_send, c_recv, c_bufs, dproj0 = _copies_start("rs1_chip_start", [p[1] for p in rs1_pairs] + rs1_land2, 12,
                                                   rs1_chip_plan, dproj0)
    g_wab = _matmul("d_w_in_ab", hn0, dproj0, TN, D_MODEL, IN_AB, rows, tm=1024, tn=1024, tk=tmk)
    rs2_shapes = [w_in_ab.shape[1:]]
    rs2_plan = _rs_pair_plan(["col"], rs2_shapes)
    rs2_land = [_empty_hbm((N_SHARD, rs2_shapes[0][0] // 2, rs2_shapes[0][1]), F32)]
    q_send, q_recv, q_bufs, dproj0 = _copies_start("rs2_pair_start", [g_wab] + rs2_land, N_SHARD, rs2_plan, dproj0)
    dhn0 = _matmul("d_hn0_a", dproj0, wab, NT, tm, D_MODEL, IN_AB, tm=tm, tn=512, tk=2048,
                   out_shape=jax.ShapeDtypeStruct((rows, D_MODEL), F32))
    q_bufs = _copies_wait("rs2_pair_wait", q_send, q_recv, q_bufs, rs2_plan, dhn0)
    rs2_pair = _rs_pair_add("rs_pair_add_w_in_ab", q_bufs[0], q_bufs[1], "col", rs2_shapes[0], c_arr)
    rs2_chip_plan = _rs_chip_plan(1)
    rs2_land2 = [_empty_hbm((3, rs2_shapes[0][0] // 2, rs2_shapes[0][1]), BF16)]
    r_send, r_recv, r_bufs, dhn0 = _copies_start("rs2_chip_start", [rs2_pair[1]] + rs2_land2, 3, rs2_chip_plan, dhn0)
    if rows > tm:
        dhn0 = _matmul("d_hn0_b", dproj0, wab, NT, rows - tm, D_MODEL, IN_AB, tm=tm, tn=512, tk=2048, a_off=(1, 0),
                       into=(dhn0, 1, 0))
    grad_x, d_meta, d_norm_ab = _rms_bwd_embed(dhn0, h0, norm_ab_w, dh1)
    c_bufs = _copies_wait("rs1_chip_wait", c_send, c_recv, c_bufs, rs1_chip_plan, grad_x)
    grad_x = grad_x[None]
    rs1_halves = [_rs_chip_add("rs_chip_add_" + nm, p[0], t, ss, mine_c)
                  for nm, p, t, ss in zip(rs1_names, rs1_pairs, c_bufs[4:], rs1_shapes)]
    g_w_out_ab, g_w_in_c, g_w_out_c, g_w_glu = _rs_pair_share("rs1_pair_share", rs1_halves, rs1_shapes)
    g_w_in_c = lax.dynamic_slice(g_w_in_c, (0, (wc_cols % 128) * mine), (D_MODEL, wc_cols))

    d_ab_re = jnp.sum(dar_p, axis=1).reshape(S5_G, S5_P)
    d_ab_im = jnp.sum(dai_p, axis=1).reshape(S5_G, S5_P)
    small_local = [loss_dev, d_meta, d_norm_ab, d_ret_norm.reshape(1, RET_W), d_ab_re, d_ab_im,
                   _bdiag_in_extract(dbr_d), _bdiag_in_extract(dbi_d),
                   _bdiag_out_extract(dcr_d), _bdiag_out_extract(dci_d),
                   jnp.sum(dd_p, axis=1).reshape(1, S5_W), d_norm_c, g_wgate[:GLA_RANK],
                   d_bgate.reshape(1, GLA_QK), d_gla_norm.reshape(1, GLA_W), d_final]
    small_shapes = [a.shape for a in small_local]
    summed = _unpack(_allreduce_small(_pack(small_local)), small_shapes)
    (loss, g_meta_f, g_norm_ab, g_ret_norm, g_ab_re, g_ab_im, g_bb_re, g_bb_im, g_c_re, g_c_im, g_d,
     g_norm_c_f, g_wgate_f, g_bgate_f, g_gla_norm_f, g_final) = summed
    _, s5_vjp = jax.vjp(_s5_discretize, s5_lam_re[0], s5_lam_im[0], s5_log_dt[0], s5_b_re[0], s5_b_im[0])
    g_lam_re, g_lam_im, g_log_dt, g_b_re, g_b_im = s5_vjp((g_ab_re, g_ab_im, g_bb_re, g_bb_im))

    def take(a, width):
        return lax.dynamic_slice_in_dim(a, mine * width, width, axis=1)

    grads = {
        "meta": take(g_meta_f, q4), "norm_ab_w": g_norm_ab, "ret_norm_w": g_ret_norm,
        "s5_lam_re": g_lam_re[None], "s5_lam_im": g_lam_im[None], "s5_log_dt": g_log_dt[None],
        "s5_b_re": g_b_re[None], "s5_b_im": g_b_im[None], "s5_c_re": g_c_re[None], "s5_c_im": g_c_im[None],
        "s5_d": g_d, "s5_w_glu": g_w_glu[None], "w_out_ab": g_w_out_ab[None], "norm_c_w": take(g_norm_c_f, q4),
        "w_in_c": g_w_in_c[None], "gla_w_gate": take(g_wgate_f, g4)[None], "gla_b_gate": take(g_bgate_f, g4),
        "gla_norm_w": take(g_gla_norm_f, q4), "w_out_c": g_w_out_c[None], "final_norm_w": g_final.reshape(D_MODEL),
    }
    weights = dict(meta=meta, norm_ab_w=norm_ab_w, w_in_ab=w_in_ab, ret_norm_w=ret_norm_w, s5_lam_re=s5_lam_re,
                   s5_lam_im=s5_lam_im, s5_log_dt=s5_log_dt, s5_b_re=s5_b_re, s5_b_im=s5_b_im, s5_c_re=s5_c_re,
                   s5_c_im=s5_c_im, s5_d=s5_d, s5_w_glu=s5_w_glu, w_out_ab=w_out_ab, norm_c_w=norm_c_w,
                   w_in_c=w_in_c, gla_w_gate=gla_w_gate, gla_b_gate=gla_b_gate, gla_norm_w=gla_norm_w,
                   w_out_c=w_out_c, final_norm_w=final_norm_w)
    m_in = dict(meta=m_meta, norm_ab_w=m_norm_ab_w, w_in_ab=m_w_in_ab, ret_norm_w=m_ret_norm_w,
                s5_lam_re=m_s5_lam_re, s5_lam_im=m_s5_lam_im, s5_log_dt=m_s5_log_dt, s5_b_re=m_s5_b_re,
                s5_b_im=m_s5_b_im, s5_c_re=m_s5_c_re, s5_c_im=m_s5_c_im, s5_d=m_s5_d, s5_w_glu=m_s5_w_glu,
                w_out_ab=m_w_out_ab, norm_c_w=m_norm_c_w, w_in_c=m_w_in_c, gla_w_gate=m_gla_w_gate,
                gla_b_gate=m_gla_b_gate, gla_norm_w=m_gla_norm_w, w_out_c=m_w_out_c, final_norm_w=m_final_norm_w)
    v_in = dict(meta=v_meta, norm_ab_w=v_norm_ab_w, w_in_ab=v_w_in_ab, ret_norm_w=v_ret_norm_w,
                s5_lam_re=v_s5_lam_re, s5_lam_im=v_s5_lam_im, s5_log_dt=v_s5_log_dt, s5_b_re=v_s5_b_re,
                s5_b_im=v_s5_b_im, s5_c_re=v_s5_c_re, s5_c_im=v_s5_c_im, s5_d=v_s5_d, s5_w_glu=v_s5_w_glu,
                w_out_ab=v_w_out_ab, norm_c_w=v_norm_c_w, w_in_c=v_w_in_c, gla_w_gate=v_gla_w_gate,
                gla_b_gate=v_gla_b_gate, gla_norm_w=v_gla_norm_w, w_out_c=v_w_out_c, final_norm_w=v_final_norm_w)
    order = list(weights)
    big_names = ["s5_w_glu", "w_out_ab", "w_in_c", "w_out_c", "w_in_ab"]
    small_names = [nm for nm in order if nm not in big_names]
    delta, new_m, new_v = {}, {}, {}

    def big_update(nm):
        shp = weights[nm].shape
        d2, m2, v2 = _adamw("adamw_" + nm, weights[nm][0], grads[nm][0], m_in[nm][0], v_in[nm][0])
        delta[nm], new_m[nm], new_v[nm] = d2.reshape(shp), m2.reshape(shp), v2.reshape(shp)

    for nm in big_names[:-1]:
        big_update(nm)
    sshapes = [weights[nm].shape for nm in small_names]
    d2, m2, v2 = _adamw("adamw_small", _pack([weights[nm] for nm in small_names]),
                        _pack([grads[nm] for nm in small_names]), _pack([m_in[nm] for nm in small_names]),
                        _pack([v_in[nm] for nm in small_names]))
    for nm, dd, mm, vv in zip(small_names, _unpack(d2, sshapes), _unpack(m2, sshapes), _unpack(v2, sshapes)):
        delta[nm], new_m[nm], new_v[nm] = dd, mm, vv
    r_bufs = _copies_wait("rs2_chip_wait", r_send, r_recv, r_bufs, rs2_chip_plan,
                          [v2] + [new_v[nm] for nm in big_names[:-1]])
    rs2_half = _rs_chip_add("rs_chip_add_w_in_ab", rs2_pair[0], r_bufs[1], rs2_shapes[0], mine_c)
    grads["w_in_ab"] = _rs_pair_share("rs2_pair_share", [rs2_half], rs2_shapes)[0][None]
    big_update("w_in_ab")
    grads = {nm: grads[nm].reshape(weights[nm].shape) for nm in order}
    return (loss.reshape(()), grad_x, *[grads[nm] for nm in order], *[delta[nm] for nm in order],
            *[new_m[nm] for nm in order], *[new_v[nm] for nm in order])
```

```python
import functools
import math

import jax
import jax.numpy as jnp
from jax import lax
from jax.experimental import pallas as pl
from jax.experimental.pallas import tpu as pltpu

F32 = jnp.float32
BF16 = jnp.bfloat16
MESH = pl.DeviceIdType.MESH

D_MODEL = 2048
N_META = 16
CHUNK = 128
SUB = 16
NSUB = CHUNK // SUB
PAD = CHUNK - N_META
EPS = 1e-6

RET_HEADS = 8
RET_DK = 128
RET_DV = 256
RET_QK = RET_HEADS * RET_DK
RET_W = RET_HEADS * RET_DV
ROPE_BASE = 10000.0

S5_W = 1024
S5_GH = 16
S5_G = S5_W // S5_GH
S5_P = 64
S5_TG = 8
S5_NT = S5_G // S5_TG
S5_TU = S5_TG * S5_GH
S5_TS = S5_TG * S5_P
S5_FWD_TILES = 2
S5_BWD_TILES = 1

GLA_HEADS = 4
GLA_DK = 256
GLA_DV = 512
GLA_QK = GLA_HEADS * GLA_DK
GLA_W = GLA_HEADS * GLA_DV
GLA_RANK = 16
GLA_TAU = 16.0

IN_AB = 2 * RET_QK + 2 * RET_W + 2 * S5_W
OUT_AB = RET_W + S5_W
IN_C = 2 * GLA_QK + 2 * GLA_W + GLA_RANK
IN_C_PAD = 2 * GLA_QK + 2 * GLA_W + 128

ADAM_LR = 0.001
ADAM_B1 = 0.9
ADAM_B2 = 0.999
ADAM_EPS = 1e-08
ADAM_WD = 0.01
ADAM_STEP = 10

N_SHARD = 4
SMALL_COLS = 512

NN = (((1,), (0,)), ((), ()))
NT = (((1,), (1,)), ((), ()))
TN = (((0,), (0,)), ((), ()))


def _dot(a, b, dims=NN):
    return lax.dot_general(a.astype(BF16), b.astype(BF16), dims, preferred_element_type=F32)


def _mo(v, m):
    return v if isinstance(v, int) else pl.multiple_of(v, m)


def _sigmoid(x):
    return 1.0 / (1.0 + jnp.exp(-x))


def _row_tile(rows, cap):
    n = rows // CHUNK
    best = 1
    for d in range(1, n + 1):
        if n % d == 0 and d * CHUNK <= cap:
            best = d
    return best * CHUNK


def _col_tile(cols, cap):
    n = cols // 128
    best = 1
    for d in range(1, n + 1):
        if n % d == 0 and d * 128 <= cap:
            best = d
    return best * 128


def _matmul(name, a, b, dims, m, n, k, *, tm, tn, tk, out_dtype=F32, a_off=(0, 0), b_off=(0, 0),
            extras=(), epilogue=None, out_shape=None, out_spec=None, segs=None, into=None):
    if segs is None:
        segs = [(a, a_off, b, b_off, k, tk)]
    assert m % tm == 0 and n % tn == 0, (name, m, n, tm, tn)
    starts, counts = [], []
    nk = 0
    for (_, _, _, _, ks, tks) in segs:
        assert ks % tks == 0, (name, ks, tks)
        starts.append(nk)
        counts.append(ks // tks)
        nk += ks // tks
    in_specs, operands = [], []
    for s, (sa, (ar, ac), sb, (br, bc), _, tks) in enumerate(segs):
        def kpos(kk, st=starts[s], cnt=counts[s]):
            return jnp.clip(kk - st, 0, cnt - 1) if len(segs) > 1 else kk

        if dims == NN:
            a_spec = pl.BlockSpec((tm, tks), lambda i, j, kk, p=kpos, r=ar, c=ac: (i + r, p(kk) + c))
            b_spec = pl.BlockSpec((tks, tn), lambda i, j, kk, p=kpos, r=br, c=bc: (p(kk) + r, j + c))
        elif dims == NT:
            a_spec = pl.BlockSpec((tm, tks), lambda i, j, kk, p=kpos, r=ar, c=ac: (i + r, p(kk) + c))
            b_spec = pl.BlockSpec((tn, tks), lambda i, j, kk, p=kpos, r=br, c=bc: (j + r, p(kk) + c))
        else:
            a_spec = pl.BlockSpec((tks, tm), lambda i, j, kk, p=kpos, r=ar, c=ac: (p(kk) + r, i + c))
            b_spec = pl.BlockSpec((tks, tn), lambda i, j, kk, p=kpos, r=br, c=bc: (p(kk) + r, j + c))
        in_specs += [a_spec, b_spec]
        operands += [sa, sb]
    n_seg = len(segs)
    n_extra = len(extras)
    if out_shape is None:
        out_shape = jax.ShapeDtypeStruct((m, n), out_dtype)

    def body(*refs):
        e_refs = refs[2 * n_seg:2 * n_seg + n_extra]
        n_in = 2 * n_seg + n_extra + (1 if into is not None else 0)
        o_ref = refs[n_in]
        if nk == 1:
            part = _dot(refs[0][...], refs[1][...], dims)
            if epilogue is not None:
                part = epilogue(part, *[e[...] for e in e_refs])
            o_ref[...] = part.astype(o_ref.dtype)
            return
        acc_ref = refs[n_in + 1]
        kk = pl.program_id(2)

        @pl.when(kk == 0)
        def _():
            acc_ref[...] = jnp.zeros_like(acc_ref)

        if n_seg == 1:
            acc_ref[...] += _dot(refs[0][...], refs[1][...], dims)
        else:
            for s in range(n_seg):
                @pl.when(jnp.logical_and(kk >= starts[s], kk < starts[s] + counts[s]))
                def _(s=s):
                    acc_ref[...] += _dot(refs[2 * s][...], refs[2 * s + 1][...], dims)

        @pl.when(kk == nk - 1)
        def _():
            acc = acc_ref[...]
            if epilogue is not None:
                acc = epilogue(acc, *[e[...] for e in e_refs])
            o_ref[...] = acc.astype(o_ref.dtype)

    if out_spec is None:
        out_spec = pl.BlockSpec((tm, tn), lambda i, j, kk: (i, j))
    in_specs += [pl.BlockSpec(bs, im) for (_, bs, im) in extras]
    operands += [e for (e, _, _) in extras]
    aliases = {}
    if into is not None:
        dest, ro, co = into
        out_shape = jax.ShapeDtypeStruct(dest.shape, dest.dtype)
        out_spec = pl.BlockSpec((tm, tn), lambda i, j, kk: (i + ro, j + co))
        aliases = {len(operands): 0}
        in_specs.append(ANY)
        operands.append(dest)
    return pl.pallas_call(
        body, name=name, grid=(m // tm, n // tn, nk),
        in_specs=in_specs, out_specs=out_spec, out_shape=out_shape, input_output_aliases=aliases,
        scratch_shapes=[] if nk == 1 else [pltpu.VMEM((tm, tn), F32)],
        compiler_params=pltpu.CompilerParams(dimension_semantics=("parallel", "parallel", "arbitrary")),
    )(*operands)


def _rms_fwd(name, h, w):
    rows, d = h.shape
    tm = _row_tile(rows, 512)

    def body(h_ref, w_ref, o_ref):
        x = h_ref[...]
        r = lax.rsqrt(jnp.mean(x * x, axis=-1, keepdims=True) + EPS)
        o_ref[...] = (x * r * w_ref[...]).astype(BF16)

    return pl.pallas_call(
        body, name=name, grid=(rows // tm,),
        in_specs=[pl.BlockSpec((tm, d), lambda i: (i, 0)), pl.BlockSpec((1, d), lambda i: (0, 0))],
        out_specs=pl.BlockSpec((tm, d), lambda i: (i, 0)),
        out_shape=jax.ShapeDtypeStruct((rows, d), BF16),
    )(h, w)


def _rms_bwd(name, dhn, h, w, dres):
    rows, d = h.shape
    tm = _row_tile(rows, 384)

    def body(g_ref, h_ref, w_ref, r_ref, dh_ref, dw_ref):
        i = pl.program_id(0)
        x = h_ref[...]
        r = lax.rsqrt(jnp.mean(x * x, axis=-1, keepdims=True) + EPS)
        xh = x * r
        g = g_ref[...]
        gw = g * w_ref[...]
        dh_ref[...] = r_ref[...] + r * (gw - xh * jnp.mean(gw * xh, axis=-1, keepdims=True))

        @pl.when(i == 0)
        def _():
            dw_ref[...] = jnp.zeros_like(dw_ref)

        dw_ref[...] += jnp.sum(g * xh, axis=0, keepdims=True)

    return pl.pallas_call(
        body, name=name, grid=(rows // tm,),
        in_specs=[pl.BlockSpec((tm, d), lambda i: (i, 0)), pl.BlockSpec((tm, d), lambda i: (i, 0)),
                  pl.BlockSpec((1, d), lambda i: (0, 0)), pl.BlockSpec((tm, d), lambda i: (i, 0))],
        out_specs=[pl.BlockSpec((tm, d), lambda i: (i, 0)), pl.BlockSpec((1, d), lambda i: (0, 0))],
        out_shape=[jax.ShapeDtypeStruct((rows, d), F32), jax.ShapeDtypeStruct((1, d), F32)],
    )(dhn, h, w, dres)


def _embed_norm(x, meta, w):
    seq, d = x.shape
    rows = seq + CHUNK

    def body(x_ref, m_ref, w_ref, h_ref, o_ref):
        i = pl.program_id(0)

        def emit(h):
            h_ref[...] = h
            r = lax.rsqrt(jnp.mean(h * h, axis=-1, keepdims=True) + EPS)
            o_ref[...] = (h * r * w_ref[...]).astype(BF16)

        @pl.when(i == 0)
        def _():
            emit(jnp.concatenate([jnp.zeros((PAD, d), F32), m_ref[...]], axis=0))

        @pl.when(i > 0)
        def _():
            emit(x_ref[...])

    blk = pl.BlockSpec((CHUNK, d), lambda i: (i, 0))
    return pl.pallas_call(
        body, name="embed_norm_ab", grid=(rows // CHUNK,),
        in_specs=[pl.BlockSpec((CHUNK, d), lambda i: (jnp.maximum(i - 1, 0), 0)),
                  pl.BlockSpec((N_META, d), lambda i: (0, 0)), pl.BlockSpec((1, d), lambda i: (0, 0))],
        out_specs=[blk, blk],
        out_shape=[jax.ShapeDtypeStruct((rows, d), F32), jax.ShapeDtypeStruct((rows, d), BF16)],
    )(x, meta, w)


def _rms_bwd_embed(dhn, h, w, dres):
    rows, d = h.shape
    seq = rows - CHUNK

    def body(g_ref, h_ref, w_ref, r_ref, gx_ref, gm_ref, dw_ref):
        i = pl.program_id(0)
        x = h_ref[...]
        r = lax.rsqrt(jnp.mean(x * x, axis=-1, keepdims=True) + EPS)
        xh = x * r
        g = g_ref[...]
        gw = g * w_ref[...]
        dh = r_ref[...] + r * (gw - xh * jnp.mean(gw * xh, axis=-1, keepdims=True))

        @pl.when(i == 0)
        def _():
            dw_ref[...] = jnp.zeros_like(dw_ref)
            gm_ref[...] = dh[PAD:]

        @pl.when(i > 0)
        def _():
            gx_ref[...] = dh

        dw_ref[...] += jnp.sum(g * xh, axis=0, keepdims=True)

    blk = pl.BlockSpec((CHUNK, d), lambda i: (i, 0))
    return pl.pallas_call(
        body, name="norm_ab_bwd", grid=(rows // CHUNK,),
        in_specs=[blk, blk, pl.BlockSpec((1, d), lambda i: (0, 0)), blk],
        out_specs=[pl.BlockSpec((CHUNK, d), lambda i: (jnp.maximum(i - 1, 0), 0)),
                   pl.BlockSpec((N_META, d), lambda i: (0, 0)), pl.BlockSpec((1, d), lambda i: (0, 0))],
        out_shape=[jax.ShapeDtypeStruct((seq, d), F32), jax.ShapeDtypeStruct((N_META, d), F32),
                   jax.ShapeDtypeStruct((1, d), F32)],
    )(dhn, h, w, dres)


def _final_loss(h2, w, target):
    rows, d = h2.shape

    def body(h_ref, w_ref, t_ref, loss_ref, dh_ref, dw_ref):
        i = pl.program_id(0)

        @pl.when(i == 0)
        def _():
            loss_ref[...] = jnp.zeros_like(loss_ref)
            dw_ref[...] = jnp.zeros_like(dw_ref)
            dh_ref[...] = jnp.zeros_like(dh_ref)

        @pl.when(i > 0)
        def _():
            x = h_ref[...]
            r = lax.rsqrt(jnp.mean(x * x, axis=-1, keepdims=True) + EPS)
            xh = x * r
            wv = w_ref[...]
            err = xh * wv - t_ref[...]
            loss_ref[...] += 0.5 * jnp.sum(jnp.mean(err * err, axis=-1, keepdims=True), axis=0, keepdims=True)
            g = err * (1.0 / d)
            gw = g * wv
            dh_ref[...] = r * (gw - xh * jnp.mean(gw * xh, axis=-1, keepdims=True))
            dw_ref[...] += jnp.sum(g * xh, axis=0, keepdims=True)

    return pl.pallas_call(
        body, name="final_loss", grid=(rows // CHUNK,),
        in_specs=[pl.BlockSpec((CHUNK, d), lambda i: (i, 0)), pl.BlockSpec((1, d), lambda i: (0, 0)),
                  pl.BlockSpec((CHUNK, d), lambda i: (jnp.maximum(i - 1, 0), 0))],
        out_specs=[pl.BlockSpec((1, 1), lambda i: (0, 0)), pl.BlockSpec((CHUNK, d), lambda i: (i, 0)),
                   pl.BlockSpec((1, d), lambda i: (0, 0))],
        out_shape=[jax.ShapeDtypeStruct((1, 1), F32), jax.ShapeDtypeStruct((rows, d), F32),
                   jax.ShapeDtypeStruct((1, d), F32)],
    )(h2, w, target)


def _gate_fwd(o, z, w):
    rs = lax.rsqrt(jnp.mean(o * o, axis=-1, keepdims=True) + EPS)
    return o * rs * w * (z * _sigmoid(z))


def _gate_bwd(dout, o, z, w):
    rs = lax.rsqrt(jnp.mean(o * o, axis=-1, keepdims=True) + EPS)
    yn = o * rs
    sg = _sigmoid(z)
    sil = z * sg
    dsil = sg * (1.0 + z * (1.0 - sg))
    dz = dout * yn * w * dsil
    dyn = dout * w * sil
    dw = jnp.sum(dout * yn * sil, axis=0, keepdims=True)
    do = rs * (dyn - yn * jnp.mean(dyn * yn, axis=-1, keepdims=True))
    return do, dz, dw


def _rope(t, cosf, sinf):
    return t * cosf + pltpu.roll(t, RET_DK // 2, 1) * sinf


def _rope_t(d, cosf, sinf):
    return d * cosf + pltpu.roll(d * sinf, RET_DK // 2, 1)


def _ret_tables():
    log_g = jnp.log1p(-jnp.exp2(-5.0 - jnp.arange(RET_HEADS, dtype=F32)))
    idx = jnp.arange(CHUNK, dtype=F32)
    diff = idx[:, None] - idx[None, :]
    decay = jnp.where(diff >= 0, jnp.exp(log_g[:, None, None] * jnp.maximum(diff, 0.0)), 0.0)
    kw = jnp.exp(log_g[:, None] * (CHUNK - 1 - idx))
    qw = jnp.exp(log_g[:, None] * (idx + 1.0))
    gch = jnp.exp(log_g * CHUNK)
    kw = jnp.broadcast_to(kw[:, :, None], (RET_HEADS, CHUNK, RET_DK))
    qw = jnp.broadcast_to(qw[:, :, None], (RET_HEADS, CHUNK, RET_DK))
    gch = jnp.broadcast_to(gch[:, None, None], (RET_HEADS, 1, RET_DV))
    return decay, kw, qw, gch


def _rope_tables(rows):
    pos = jnp.arange(rows, dtype=F32) - float(PAD)
    inv_freq = jnp.power(ROPE_BASE, -jnp.arange(0, RET_DK, 2, dtype=F32) / RET_DK)
    ang = pos[:, None] * inv_freq[None, :]
    cos, sin = jnp.cos(ang), jnp.sin(ang)
    return jnp.concatenate([cos, cos], axis=1), jnp.concatenate([-sin, sin], axis=1)


RET_HB = 8
RET_QB = RET_HB * RET_DK
RET_VB = RET_HB * RET_DV


def _ret_in_specs(rev, nc):
    def cn(n):
        return (nc - 1 - n) if rev else n
    kb = RET_QK // RET_QB
    vb = 2 * RET_QK // RET_VB
    zb = (2 * RET_QK + RET_W) // RET_VB
    return [
        pl.BlockSpec((CHUNK, RET_QB), lambda h, n: (cn(n), h)),
        pl.BlockSpec((CHUNK, RET_QB), lambda h, n: (cn(n), kb + h)),
        pl.BlockSpec((CHUNK, RET_VB), lambda h, n: (cn(n), vb + h)),
        pl.BlockSpec((CHUNK, RET_VB), lambda h, n: (cn(n), zb + h)),
        pl.BlockSpec((CHUNK, RET_DK), lambda h, n: (cn(n), 0)),
        pl.BlockSpec((CHUNK, RET_DK), lambda h, n: (cn(n), 0)),
        pl.BlockSpec((RET_HB, CHUNK, CHUNK), lambda h, n: (h, 0, 0)),
        pl.BlockSpec((RET_HB, CHUNK, RET_DK), lambda h, n: (h, 0, 0)),
        pl.BlockSpec((RET_HB, CHUNK, RET_DK), lambda h, n: (h, 0, 0)),
        pl.BlockSpec((RET_HB, 1, RET_DV), lambda h, n: (h, 0, 0)),
        pl.BlockSpec((1, RET_VB), lambda h, n: (0, h)),
    ]


def _ret_fwd(proj, cosf, sinf, tables, normw):
    rows = proj.shape[0]
    nc = rows // CHUNK
    decay, kw, qw, gch = tables

    def body(q_ref, k_ref, v_ref, z_ref, cos_ref, sin_ref, dm_ref, kw_ref, qw_ref, g_ref, w_ref,
             o_ref, oa_ref, st_ref, s_scr):
        n = pl.program_id(1)

        @pl.when(n == 0)
        def _():
            s_scr[...] = jnp.zeros_like(s_scr)

        cosv, sinv = cos_ref[...], sin_ref[...]
        for hh in range(RET_HB):
            qc = slice(hh * RET_DK, (hh + 1) * RET_DK)
            vc = slice(hh * RET_DV, (hh + 1) * RET_DV)
            q = _rope(q_ref[:, qc], cosv, sinv)
            k = _rope(k_ref[:, qc], cosv, sinv) * (RET_DK ** -0.5)
            v = v_ref[:, vc]
            s = s_scr[hh]
            st_ref[hh, 0] = s.astype(BF16)
            a = _dot(q, k, NT) * dm_ref[hh]
            o = _dot(a, v) + _dot(q * qw_ref[hh], s)
            s_scr[hh] = s * g_ref[hh] + _dot(k * kw_ref[hh], v, TN)
            o_ref[:, vc] = o
            oa_ref[:, vc] = _gate_fwd(o, z_ref[:, vc], w_ref[:, vc]).astype(BF16)

    return pl.pallas_call(
        body, name="ret_fwd", grid=(RET_HEADS // RET_HB, nc),
        in_specs=_ret_in_specs(False, nc),
        out_specs=[pl.BlockSpec((CHUNK, RET_VB), lambda h, n: (n, h)),
                   pl.BlockSpec((CHUNK, RET_VB), lambda h, n: (n, h)),
                   pl.BlockSpec((RET_HB, 1, RET_DK, RET_DV), lambda h, n: (h, n, 0, 0))],
        out_shape=[jax.ShapeDtypeStruct((rows, RET_W), F32), jax.ShapeDtypeStruct((rows, RET_W), BF16),
                   jax.ShapeDtypeStruct((RET_HEADS, nc, RET_DK, RET_DV), BF16)],
        scratch_shapes=[pltpu.VMEM((RET_HB, RET_DK, RET_DV), F32)],
        compiler_params=pltpu.CompilerParams(dimension_semantics=("parallel", "arbitrary")),
    )(proj, proj, proj, proj, cosf, sinf, decay, kw, qw, gch, normw)


def _ret_bwd(proj, cosf, sinf, tables, normw, o_ret, dmix, states):
    assert RET_HB == RET_HEADS
    rows = proj.shape[0]
    nc = rows // CHUNK
    decay, kw, qw, gch = tables
    ret_cols = 2 * RET_QK + 2 * RET_W

    def rn(n):
        return nc - 1 - n

    def body(q_ref, k_ref, v_ref, z_ref, cos_ref, sin_ref, dm_ref, kw_ref, qw_ref, g_ref, w_ref,
             o_ref, do_ref, st_ref, dp_ref, dw_ref, ds_scr):
        n = pl.program_id(1)
        dq_ref = dp_ref.at[:, 0:RET_QK]
        dk_ref = dp_ref.at[:, RET_QK:2 * RET_QK]
        dv_ref = dp_ref.at[:, 2 * RET_QK:2 * RET_QK + RET_W]
        dz_ref = dp_ref.at[:, 2 * RET_QK + RET_W:ret_cols]

        @pl.when(n == 0)
        def _():
            ds_scr[...] = jnp.zeros_like(ds_scr)
            dw_ref[...] = jnp.zeros_like(dw_ref)

        cosv, sinv = cos_ref[...], sin_ref[...]
        for hh in range(RET_HB):
            qc = slice(hh * RET_DK, (hh + 1) * RET_DK)
            vc = slice(hh * RET_DV, (hh + 1) * RET_DV)
            q = _rope(q_ref[:, qc], cosv, sinv)
            k = _rope(k_ref[:, qc], cosv, sinv) * (RET_DK ** -0.5)
            v = v_ref[:, vc]
            do, dz, dw = _gate_bwd(do_ref[:, vc], o_ref[:, vc], z_ref[:, vc], w_ref[:, vc])
            dz_ref[:, vc] = dz.astype(BF16)
            dw_ref[hh] += dw
            dm = dm_ref[hh]
            s = st_ref[hh, 0]
            g1 = ds_scr[hh]
            p = _dot(q, k, NT) * dm
            kwv = k * kw_ref[hh]
            qwv = q * qw_ref[hh]
            dp = _dot(do, v, NT)
            da = dp * dm
            dv = _dot(p, do, TN) + _dot(kwv, g1)
            dq = _dot(da, k) + _dot(do, s, NT) * qw_ref[hh]
            dk = _dot(da, q, TN) + _dot(v, g1, NT) * kw_ref[hh]
            ds_scr[hh] = g1 * g_ref[hh] + _dot(qwv, do, TN)
            dv_ref[:, vc] = dv.astype(BF16)
            dq_ref[:, qc] = _rope_t(dq, cosv, sinv).astype(BF16)
            dk_ref[:, qc] = _rope_t(dk * (RET_DK ** -0.5), cosv, sinv).astype(BF16)

    in_specs = _ret_in_specs(True, nc) + [
        pl.BlockSpec((CHUNK, RET_VB), lambda h, n: (rn(n), h)),
        pl.BlockSpec((CHUNK, RET_VB), lambda h, n: (rn(n), h)),
        pl.BlockSpec((RET_HB, 1, RET_DK, RET_DV), lambda h, n: (h, rn(n), 0, 0)),
    ]
    return pl.pallas_call(
        body, name="ret_bwd", grid=(RET_HEADS // RET_HB, nc),
        in_specs=in_specs,
        out_specs=[pl.BlockSpec((CHUNK, ret_cols), lambda h, n: (rn(n), 0)),
                   pl.BlockSpec((RET_HB, 1, RET_DV), lambda h, n: (h, 0, 0))],
        out_shape=[jax.ShapeDtypeStruct((rows, IN_AB), BF16), jax.ShapeDtypeStruct((RET_HEADS, 1, RET_DV), F32)],
        scratch_shapes=[pltpu.VMEM((RET_HB, RET_DK, RET_DV), F32)],
        compiler_params=pltpu.CompilerParams(dimension_semantics=("parallel", "arbitrary")),
    )(proj, proj, proj, proj, cosf, sinf, decay, kw, qw, gch, normw, o_ret, dmix, states)


def _s5_discretize(lam_re, lam_im, log_dt, b_re, b_im):
    dt = jnp.exp(log_dt)[:, None]
    mag = jnp.exp(lam_re * dt)
    ab_re, ab_im = mag * jnp.cos(lam_im * dt), mag * jnp.sin(lam_im * dt)
    den = lam_re * lam_re + lam_im * lam_im
    nr, ni = ab_re - 1.0, ab_im
    f_re = (nr * lam_re + ni * lam_im) / den
    f_im = (ni * lam_re - nr * lam_im) / den
    bb_re = f_re[..., None] * b_re - f_im[..., None] * b_im
    bb_im = f_re[..., None] * b_im + f_im[..., None] * b_re
    return ab_re, ab_im, bb_re, bb_im


def _bdiag_in(bb):
    t = bb.reshape(S5_NT, S5_TG, S5_P, S5_GH).transpose(0, 1, 3, 2)
    eye = jnp.eye(S5_TG, dtype=bb.dtype)
    full = t[:, :, :, None, :] * eye[None, :, None, :, None]
    return full.reshape(S5_NT, S5_TU, S5_TS)


def _bdiag_in_extract(dense):
    t = dense.reshape(S5_NT, S5_TG, S5_GH, S5_TG, S5_P)
    diag = jnp.stack([t[:, g, :, g, :] for g in range(S5_TG)], axis=1)
    return diag.transpose(0, 1, 3, 2).reshape(S5_G, S5_P, S5_GH)


def _bdiag_out(c):
    t = c.reshape(S5_NT, S5_TG, S5_GH, S5_P).transpose(0, 1, 3, 2)
    eye = jnp.eye(S5_TG, dtype=c.dtype)
    full = t[:, :, :, None, :] * eye[None, :, None, :, None]
    return full.reshape(S5_NT, S5_TS, S5_TU)


def _bdiag_out_extract(dense):
    t = dense.reshape(S5_NT, S5_TG, S5_P, S5_TG, S5_GH)
    diag = jnp.stack([t[:, g, :, g, :] for g in range(S5_TG)], axis=1)
    return diag.transpose(0, 1, 3, 2).reshape(S5_G, S5_GH, S5_P)


def _cmul(ar, ai, br, bi):
    return ar * br - ai * bi, ar * bi + ai * br


S5_SEG = 8
S5_STEPS = CHUNK // S5_SEG


def _seg_perm(x):
    c = x.shape[1]
    return jnp.swapaxes(x.reshape(S5_SEG, S5_STEPS, c), 0, 1).reshape(CHUNK, c)


def _seg_unperm(x):
    c = x.shape[1]
    return jnp.swapaxes(x.reshape(S5_STEPS, S5_SEG, c), 0, 1).reshape(CHUNK, c)


def _rows(x, p):
    return x[p * S5_SEG:(p + 1) * S5_SEG]


def _s5_tables(ar, ai, tr_scr, ti_scr, wfr_scr, wfi_scr, wbr_scr, wbi_scr):
    row = lax.broadcasted_iota(jnp.int32, (S5_SEG, 1), 0)
    a8r = jnp.broadcast_to(ar, (S5_SEG, S5_TS))
    a8i = jnp.broadcast_to(ai, (S5_SEG, S5_TS))
    pr, pi = a8r, a8i
    for p in range(S5_STEPS):
        tr_scr[p * S5_SEG:(p + 1) * S5_SEG, :] = pr
        ti_scr[p * S5_SEG:(p + 1) * S5_SEG, :] = pi
        if p < S5_STEPS - 1:
            pr, pi = _cmul(pr, pi, a8r, a8i)
    wr, wi = pr, pi
    sh = 1
    while sh < S5_SEG:
        keep = row >= sh
        sr = jnp.where(keep, pltpu.roll(wr, sh, 0), 1.0)
        si = jnp.where(keep, pltpu.roll(wi, sh, 0), 0.0)
        wr, wi = _cmul(wr, wi, sr, si)
        sh *= 2
    wfr_scr[...] = wr
    wfi_scr[...] = wi
    wr, wi = pr, -pi
    sh = 1
    while sh < S5_SEG:
        keep = row < S5_SEG - sh
        sr = jnp.where(keep, pltpu.roll(wr, S5_SEG - sh, 0), 1.0)
        si = jnp.where(keep, pltpu.roll(wi, S5_SEG - sh, 0), 0.0)
        wr, wi = _cmul(wr, wi, sr, si)
        sh *= 2
    wbr_scr[...] = wr
    wbi_scr[...] = wi


def _seg_scan(vr, vi, ar, ai, tr_scr, ti_scr, wr_scr, wi_scr, c0r, c0i, down):
    row = lax.broadcasted_iota(jnp.int32, (S5_SEG, 1), 0)
    sgn = 1.0 if down else -1.0
    order = list(range(S5_STEPS)) if down else list(range(S5_STEPS - 1, -1, -1))
    xr, xi = _rows(vr, order[0]), _rows(vi, order[0])
    loc = {order[0]: (xr, xi)}
    for p in order[1:]:
        mr, mi = _cmul(ar, sgn * ai, xr, xi)
        xr, xi = mr + _rows(vr, p), mi + _rows(vi, p)
        loc[p] = (xr, xi)
    last = S5_STEPS - 1
    mr, mi = tr_scr[last * S5_SEG:(last + 1) * S5_SEG, :], sgn * ti_scr[last * S5_SEG:(last + 1) * S5_SEG, :]
    er, ei = xr, xi
    sh = 1
    while sh < S5_SEG:
        if down:
            keep = row >= sh
            sr, si = pltpu.roll(er, sh, 0), pltpu.roll(ei, sh, 0)
        else:
            keep = row < S5_SEG - sh
            sr, si = pltpu.roll(er, S5_SEG - sh, 0), pltpu.roll(ei, S5_SEG - sh, 0)
        pr, pi = _cmul(mr, mi, jnp.where(keep, sr, 0.0), jnp.where(keep, si, 0.0))
        er, ei = er + pr, ei + pi
        mr, mi = _cmul(mr, mi, mr, mi)
        sh *= 2
    pr, pi = _cmul(wr_scr[...], wi_scr[...], c0r, c0i)
    er, ei = er + pr, ei + pi
    if down:
        nr = jnp.where(row == 0, c0r, pltpu.roll(er, 1, 0))
        ni = jnp.where(row == 0, c0i, pltpu.roll(ei, 1, 0))
    else:
        nr = jnp.where(row == S5_SEG - 1, c0r, pltpu.roll(er, S5_SEG - 1, 0))
        ni = jnp.where(row == S5_SEG - 1, c0i, pltpu.roll(ei, S5_SEG - 1, 0))
    out_r, out_i = [], []
    for p in range(S5_STEPS):
        q = p if down else S5_STEPS - 1 - p
        pr, pi = _cmul(tr_scr[q * S5_SEG:(q + 1) * S5_SEG, :], sgn * ti_scr[q * S5_SEG:(q + 1) * S5_SEG, :], nr, ni)
        out_r.append(loc[p][0] + pr)
        out_i.append(loc[p][1] + pi)
    return jnp.concatenate(out_r, axis=0), jnp.concatenate(out_i, axis=0), (nr, ni), (er, ei)


def _gelu(y):
    c = math.sqrt(2.0 / math.pi)
    return 0.5 * y * (1.0 + jnp.tanh(c * (y + 0.044715 * y * y * y)))


def _gelu_grad(y):
    c = math.sqrt(2.0 / math.pi)
    th = jnp.tanh(c * (y + 0.044715 * y * y * y))
    return 0.5 * (1.0 + th) + 0.5 * y * (1.0 - th * th) * c * (1.0 + 3.0 * 0.044715 * y * y)


def _s5_fwd(proj, ab, bd_b, bd_c, dvec):
    rows = proj.shape[0]
    nc = rows // CHUNK
    tps = S5_FWD_TILES
    ubw = tps * S5_TU
    ub = (2 * RET_QK + 2 * RET_W) // ubw
    ab_re, ab_im = ab
    bre, bim = bd_b
    cre, cim = bd_c

    def body(u_ref, ar_ref, ai_ref, bre_ref, bim_ref, cre_ref, cim_ref, d_ref,
             y_ref, g_ref, er_ref, ei_ref, tr_scr, ti_scr, wfr_scr, wfi_scr, wbr_scr, wbi_scr,
             cr_scr, ci_scr, er_scr, ei_scr):
        n = pl.program_id(1)
        for tt in range(tps):
            cols = slice(tt * S5_TU, (tt + 1) * S5_TU)
            ar, ai = ar_ref[tt], ai_ref[tt]
            trs, tis, wfr, wfi = tr_scr.at[tt], ti_scr.at[tt], wfr_scr.at[tt], wfi_scr.at[tt]

            @pl.when(n == 0)
            def _(tt=tt, ar=ar, ai=ai, trs=trs, tis=tis, wfr=wfr, wfi=wfi):
                _s5_tables(ar, ai, trs, tis, wfr, wfi, wbr_scr.at[tt], wbi_scr.at[tt])
                cr_scr[tt] = jnp.zeros((S5_SEG, S5_TS), F32)
                ci_scr[tt] = jnp.zeros((S5_SEG, S5_TS), F32)

            u = _seg_perm(u_ref[:, cols])
            c0r, c0i = cr_scr[tt], ci_scr[tt]
            er_ref[tt, 0] = c0r
            ei_ref[tt, 0] = c0i
            xr, xi, _, (er, ei) = _seg_scan(_dot(u, bre_ref[tt]), _dot(u, bim_ref[tt]), ar, ai, trs, tis,
                                            wfr, wfi, c0r, c0i, True)
            er_scr[tt] = er
            ei_scr[tt] = ei
            cr_scr[tt] = jnp.broadcast_to(er_scr[tt, S5_SEG - 1:S5_SEG, :], (S5_SEG, S5_TS))
            ci_scr[tt] = jnp.broadcast_to(ei_scr[tt, S5_SEG - 1:S5_SEG, :], (S5_SEG, S5_TS))
            y = _seg_unperm(_dot(xr, cre_ref[tt]) - _dot(xi, cim_ref[tt]) + d_ref[:, cols] * u)
            y_ref[:, cols] = y
            g_ref[:, cols] = _gelu(y).astype(BF16)

    vec = pl.BlockSpec((tps, 1, S5_TS), lambda t, n: (t, 0, 0))
    return pl.pallas_call(
        body, name="s5_fwd", grid=(S5_NT // tps, nc),
        in_specs=[pl.BlockSpec((CHUNK, ubw), lambda t, n: (n, ub + t)), vec, vec,
                  pl.BlockSpec((tps, S5_TU, S5_TS), lambda t, n: (t, 0, 0)),
                  pl.BlockSpec((tps, S5_TU, S5_TS), lambda t, n: (t, 0, 0)),
                  pl.BlockSpec((tps, S5_TS, S5_TU), lambda t, n: (t, 0, 0)),
                  pl.BlockSpec((tps, S5_TS, S5_TU), lambda t, n: (t, 0, 0)),
                  pl.BlockSpec((1, ubw), lambda t, n: (0, t))],
        out_specs=[pl.BlockSpec((CHUNK, ubw), lambda t, n: (n, t)),
                   pl.BlockSpec((CHUNK, ubw), lambda t, n: (n, t)),
                   pl.BlockSpec((tps, 1, 8, S5_TS), lambda t, n: (t, n, 0, 0)),
                   pl.BlockSpec((tps, 1, 8, S5_TS), lambda t, n: (t, n, 0, 0))],
        out_shape=[jax.ShapeDtypeStruct((rows, S5_W), F32), jax.ShapeDtypeStruct((rows, S5_W), BF16),
                   jax.ShapeDtypeStruct((S5_NT, nc, 8, S5_TS), F32),
                   jax.ShapeDtypeStruct((S5_NT, nc, 8, S5_TS), F32)],
        scratch_shapes=[pltpu.VMEM((tps, CHUNK, S5_TS), F32) for _ in range(2)]
        + [pltpu.VMEM((tps, S5_SEG, S5_TS), F32) for _ in range(8)],
        compiler_params=pltpu.CompilerParams(dimension_semantics=("parallel", "arbitrary")),
    )(proj, ab_re.reshape(S5_NT, 1, S5_TS), ab_im.reshape(S5_NT, 1, S5_TS), bre, bim, cre, cim, dvec)


def _s5_bwd(proj, dy, ab, bd_b, bd_c, dvec, entry, dproj):
    rows = proj.shape[0]
    nc = rows // CHUNK
    tps = S5_BWD_TILES
    ubw = tps * S5_TU
    ub = (2 * RET_QK + 2 * RET_W) // ubw
    ab_re, ab_im = ab
    bre, bim = bd_b
    cre, cim = bd_c
    er, ei = entry

    def rn(n):
        return nc - 1 - n

    def body(u_ref, dy_ref, ar_ref, ai_ref, bre_ref, bim_ref, cre_ref, cim_ref, d_ref, er_ref, ei_ref, dp_ref,
             du_ref, dbr_ref, dbi_ref, dcr_ref, dci_ref, dar_ref, dai_ref, dd_ref,
             tr_scr, ti_scr, wfr_scr, wfi_scr, wbr_scr, wbi_scr, gr_scr, gi_scr, er_scr, ei_scr):
        n = pl.program_id(1)

        @pl.when(n == 0)
        def _():
            gr_scr[...] = jnp.zeros_like(gr_scr)
            gi_scr[...] = jnp.zeros_like(gi_scr)
            for r in (dbr_ref, dbi_ref, dcr_ref, dci_ref, dar_ref, dai_ref, dd_ref):
                r[...] = jnp.zeros_like(r)

        for tt in range(tps):
            cols = slice(tt * S5_TU, (tt + 1) * S5_TU)
            ar, ai = ar_ref[tt], ai_ref[tt]
            trs, tis = tr_scr.at[tt], ti_scr.at[tt]

            @pl.when(n == 0)
            def _(tt=tt, ar=ar, ai=ai, trs=trs, tis=tis):
                _s5_tables(ar, ai, trs, tis, wfr_scr.at[tt], wfi_scr.at[tt], wbr_scr.at[tt], wbi_scr.at[tt])

            u = _seg_perm(u_ref[:, cols])
            dy = _seg_perm(dy_ref[:, cols])
            xr, xi, (pr, pi), _ = _seg_scan(_dot(u, bre_ref[tt]), _dot(u, bim_ref[tt]), ar, ai, trs, tis,
                                            wfr_scr.at[tt], wfi_scr.at[tt], er_ref[tt, 0], ei_ref[tt, 0], True)
            dcr_ref[tt] += _dot(xr, dy, TN)
            dci_ref[tt] -= _dot(xi, dy, TN)
            gr, gi, _, (er, ei) = _seg_scan(_dot(dy, cre_ref[tt], NT), -_dot(dy, cim_ref[tt], NT), ar, ai, trs, tis,
                                            wbr_scr.at[tt], wbi_scr.at[tt], gr_scr[tt], gi_scr[tt], False)
            er_scr[tt] = er
            ei_scr[tt] = ei
            gr_scr[tt] = jnp.broadcast_to(er_scr[tt, 0:1, :], (S5_SEG, S5_TS))
            gi_scr[tt] = jnp.broadcast_to(ei_scr[tt, 0:1, :], (S5_SEG, S5_TS))
            xpr = jnp.concatenate([pr, xr[:CHUNK - S5_SEG]], axis=0)
            xpi = jnp.concatenate([pi, xi[:CHUNK - S5_SEG]], axis=0)
            dar_ref[tt] += jnp.sum((xpr * gr + xpi * gi).reshape(S5_STEPS, S5_SEG, S5_TS), axis=0)
            dai_ref[tt] += jnp.sum((xpr * gi - xpi * gr).reshape(S5_STEPS, S5_SEG, S5_TS), axis=0)
            dbr_ref[tt] += _dot(u, gr, TN)
            dbi_ref[tt] += _dot(u, gi, TN)
            dd_ref[tt] += jnp.sum((dy * u).reshape(S5_STEPS, S5_SEG, S5_TU), axis=0)
            du = dy * d_ref[:, cols] + _dot(gr, bre_ref[tt], NT) + _dot(gi, bim_ref[tt], NT)
            du_ref[:, cols] = _seg_unperm(du).astype(BF16)

    vec = pl.BlockSpec((tps, 1, S5_TS), lambda t, n: (t, 0, 0))
    acc_b = pl.BlockSpec((tps, S5_TU, S5_TS), lambda t, n: (t, 0, 0))
    acc_c = pl.BlockSpec((tps, S5_TS, S5_TU), lambda t, n: (t, 0, 0))
    acc_a = pl.BlockSpec((tps, 8, S5_TS), lambda t, n: (t, 0, 0))
    ent = pl.BlockSpec((tps, 1, 8, S5_TS), lambda t, n: (t, rn(n), 0, 0))
    return pl.pallas_call(
        body, name="s5_bwd", grid=(S5_NT // tps, nc),
        in_specs=[pl.BlockSpec((CHUNK, ubw), lambda t, n: (rn(n), ub + t)),
                  pl.BlockSpec((CHUNK, ubw), lambda t, n: (rn(n), t)), vec, vec,
                  acc_b, acc_b, acc_c, acc_c, pl.BlockSpec((1, ubw), lambda t, n: (0, t)), ent, ent, ANY],
        out_specs=[pl.BlockSpec((CHUNK, ubw), lambda t, n: (rn(n), ub + t)), acc_b, acc_b, acc_c, acc_c, acc_a, acc_a,
                   pl.BlockSpec((tps, 8, S5_TU), lambda t, n: (t, 0, 0))],
        input_output_aliases={11: 0},
        out_shape=[jax.ShapeDtypeStruct(dproj.shape, BF16),
                   jax.ShapeDtypeStruct((S5_NT, S5_TU, S5_TS), F32), jax.ShapeDtypeStruct((S5_NT, S5_TU, S5_TS), F32),
                   jax.ShapeDtypeStruct((S5_NT, S5_TS, S5_TU), F32), jax.ShapeDtypeStruct((S5_NT, S5_TS, S5_TU), F32),
                   jax.ShapeDtypeStruct((S5_NT, 8, S5_TS), F32), jax.ShapeDtypeStruct((S5_NT, 8, S5_TS), F32),
                   jax.ShapeDtypeStruct((S5_NT, 8, S5_TU), F32)],
        scratch_shapes=[pltpu.VMEM((tps, CHUNK, S5_TS), F32) for _ in range(2)]
        + [pltpu.VMEM((tps, S5_SEG, S5_TS), F32) for _ in range(8)],
        compiler_params=pltpu.CompilerParams(dimension_semantics=("parallel", "arbitrary")),
    )(proj, dy, ab_re.reshape(S5_NT, 1, S5_TS), ab_im.reshape(S5_NT, 1, S5_TS), bre, bim, cre, cim, dvec, er, ei,
      dproj)


def _s5_gate_bwd(dmix, g, t, proj, dproj):
    rows = g.shape[0]
    tm = _row_tile(rows, 384)
    ob = RET_W // S5_W
    zb = (2 * RET_QK + 2 * RET_W + S5_W) // S5_W

    def body(do_ref, g_ref, t_ref, z_ref, dp_ref, dz_ref, dt_ref, dg_ref):
        do = do_ref[...]
        gv = g_ref[...].astype(F32)
        z = z_ref[...]
        st = _sigmoid(t_ref[...])
        sg = _sigmoid(z)
        os5 = gv * st
        dz_ref[...] = (do * os5 * sg * (1.0 + z * (1.0 - sg))).astype(BF16)
        dos = do * z * sg
        dt_ref[...] = (dos * gv * st * (1.0 - st)).astype(BF16)
        dg_ref[...] = dos * st

    blk = pl.BlockSpec((tm, S5_W), lambda i: (i, 0))
    return pl.pallas_call(
        body, name="s5_gate_bwd", grid=(rows // tm,),
        in_specs=[pl.BlockSpec((tm, S5_W), lambda i: (i, ob)), blk, blk,
                  pl.BlockSpec((tm, S5_W), lambda i: (i, zb)), ANY],
        out_specs=[pl.BlockSpec((tm, S5_W), lambda i: (i, zb)), blk, blk],
        out_shape=[jax.ShapeDtypeStruct(dproj.shape, BF16), jax.ShapeDtypeStruct((rows, S5_W), BF16),
                   jax.ShapeDtypeStruct((rows, S5_W), F32)],
        input_output_aliases={4: 0},
    )(dmix, g, t, proj, dproj)


def _split3(x):
    hi = x.astype(BF16)
    r = x - hi.astype(F32)
    mid = r.astype(BF16)
    lo = (r - mid.astype(F32)).astype(BF16)
    return hi, mid, lo


def _tri_sum(x, upper):
    i = lax.broadcasted_iota(jnp.int32, (CHUNK, CHUNK), 0)
    j = lax.broadcasted_iota(jnp.int32, (CHUNK, CHUNK), 1)
    tri = jnp.where((j >= i) if upper else (j <= i), 1.0, 0.0).astype(BF16)
    hi, mid, lo = _split3(x)
    return _dot(tri, lo) + _dot(tri, mid) + _dot(tri, hi)


def _gla_log_decay(gl, wg, bg, n):
    logit = _dot(gl, wg) + bg
    la = (jnp.minimum(logit, 0.0) - jnp.log(1.0 + jnp.exp(-jnp.abs(logit)))) * (1.0 / GLA_TAU)
    row = lax.broadcasted_iota(jnp.int32, (CHUNK, 1), 0)
    live = jnp.logical_or(n > 0, row >= PAD)
    return logit, jnp.where(live, la, 0.0), live


def _gla_in_specs(rev, nc):
    def cn(n):
        return (nc - 1 - n) if rev else n
    kb = GLA_QK // GLA_DK
    vb = 2 * GLA_QK // GLA_DV
    zb = (2 * GLA_QK + GLA_W) // GLA_DV
    gb = (2 * GLA_QK + 2 * GLA_W) // 128
    return [
        pl.BlockSpec((CHUNK, GLA_DK), lambda h, n: (cn(n), h)),
        pl.BlockSpec((CHUNK, GLA_DK), lambda h, n: (cn(n), kb + h)),
        pl.BlockSpec((CHUNK, GLA_DV), lambda h, n: (cn(n), vb + h)),
        pl.BlockSpec((CHUNK, GLA_DV), lambda h, n: (cn(n), zb + h)),
        pl.BlockSpec((CHUNK, 128), lambda h, n: (cn(n), gb)),
        pl.BlockSpec((128, GLA_DK), lambda h, n: (0, h)),
        pl.BlockSpec((1, GLA_DK), lambda h, n: (0, h)),
        pl.BlockSpec((1, GLA_DV), lambda h, n: (0, h)),
    ]


def _gla_fwd(proj, wgate, bgate, normw):
    rows = proj.shape[0]
    nc = rows // CHUNK

    def body(q_ref, k_ref, v_ref, z_ref, gl_ref, wg_ref, bg_ref, w_ref, o_ref, oc_ref, st_ref, s_scr, b_scr):
        n = pl.program_id(1)

        @pl.when(n == 0)
        def _():
            s_scr[...] = jnp.zeros_like(s_scr)

        q = q_ref[...] * (GLA_DK ** -0.5)
        k = k_ref[...]
        v = v_ref[...]
        vb = v.astype(BF16)
        _, la, _ = _gla_log_decay(gl_ref[...], wg_ref[...], bg_ref[...], n)
        b = _tri_sum(la, False)
        b_scr[...] = b
        b_last = b_scr[CHUNK - 1:CHUNK, :]
        st = s_scr[...]
        st_ref[0, 0] = st
        s_scr[...] = st * jnp.exp(b_last) + _dot(v, k * jnp.exp(b_last - b), TN)
        rowc = lax.broadcasted_iota(jnp.int32, (CHUNK, 1), 0)
        rows16 = lax.broadcasted_iota(jnp.int32, (SUB, 1), 0)
        a_tot = jnp.zeros((CHUNK, CHUNK), F32)
        for s in range(1, NSUB):
            lo = s * SUB
            bref = b_scr[lo - 1:lo, :]
            in_s = jnp.logical_and(rowc >= lo, rowc < lo + SUB)
            qh = q * jnp.exp(jnp.where(in_s, b - bref, -1e30))
            kh = k * jnp.exp(jnp.where(rowc < lo, bref - b, -1e30))
            a_tot = a_tot + _dot(qh, kh, NT)
        lane = lax.broadcasted_iota(jnp.int32, (SUB, CHUNK), 1)
        diag = []
        for s in range(NSUB):
            lo = s * SUB
            qs, bs = q[lo:lo + SUB], b[lo:lo + SUB]
            s_blk = jnp.zeros((SUB, CHUNK), F32)
            for j in range(SUB):
                r = lo + j
                e = jnp.exp(jnp.where(rows16 >= j, bs - b_scr[r:r + 1, :], -1e30))
                col = jnp.sum(qs * k_ref[r:r + 1, :] * e, axis=1, keepdims=True)
                s_blk = jnp.where(lane == r, col, s_blk)
            diag.append(s_blk)
        o = _dot(q * jnp.exp(b), st, NT) + _dot(a_tot + jnp.concatenate(diag, axis=0), vb)
        o_ref[...] = o
        oc_ref[...] = _gate_fwd(o, z_ref[...], w_ref[...]).astype(BF16)

    return pl.pallas_call(
        body, name="gla_fwd", grid=(GLA_HEADS, nc),
        in_specs=_gla_in_specs(False, nc),
        out_specs=[pl.BlockSpec((CHUNK, GLA_DV), lambda h, n: (n, h)),
                   pl.BlockSpec((CHUNK, GLA_DV), lambda h, n: (n, h)),
                   pl.BlockSpec((1, 1, GLA_DV, GLA_DK), lambda h, n: (h, n, 0, 0))],
        out_shape=[jax.ShapeDtypeStruct((rows, GLA_W), F32), jax.ShapeDtypeStruct((rows, GLA_W), BF16),
                   jax.ShapeDtypeStruct((GLA_HEADS, nc, GLA_DV, GLA_DK), F32)],
        scratch_shapes=[pltpu.VMEM((GLA_DV, GLA_DK), F32), pltpu.VMEM((CHUNK, GLA_DK), F32)],
        compiler_params=pltpu.CompilerParams(dimension_semantics=("parallel", "arbitrary")),
    )(proj, proj, proj, proj, proj, wgate, bgate, normw)


def _gla_bwd(proj, wgate, bgate, normw, o_gla, d_oc, states):
    rows = proj.shape[0]
    nc = rows // CHUNK

    def rn(n):
        return nc - 1 - n

    def body(q_ref, k_ref, v_ref, z_ref, gl_ref, wg_ref, bg_ref, w_ref, o_ref, do_ref, st_ref,
             dq_ref, dk_ref, dv_ref, dz_ref, dl_ref, dw_ref, dbg_ref,
             ds_scr, dq_scr, dk_scr, dv_scr, db_scr, b_scr, q_scr):
        n = pl.program_id(1)
        cn = rn(n)

        @pl.when(n == 0)
        def _():
            ds_scr[...] = jnp.zeros_like(ds_scr)
            dw_ref[...] = jnp.zeros_like(dw_ref)
            dbg_ref[...] = jnp.zeros_like(dbg_ref)

        q = q_ref[...] * (GLA_DK ** -0.5)
        k = k_ref[...]
        v = v_ref[...]
        vb = v.astype(BF16)
        do, dz, dw = _gate_bwd(do_ref[...], o_ref[...], z_ref[...], w_ref[...])
        dz_ref[...] = dz.astype(BF16)
        dw_ref[0] += dw
        logit, la, live = _gla_log_decay(gl_ref[...], wg_ref[...], bg_ref[...], cn)
        b = _tri_sum(la, False)
        b_scr[...] = b
        b_last = b_scr[CHUNK - 1:CHUNK, :]
        e_last = jnp.exp(b_last)
        st = st_ref[0, 0]
        g1 = ds_scr[...]
        eb = jnp.exp(b)
        qe = q * eb
        dqe = _dot(do, st)
        dq_scr[...] = dqe * eb
        db_scr[...] = dqe * qe
        ekb = jnp.exp(b_last - b)
        kdec = k * ekb
        dkdec = _dot(v, g1)
        dv_scr[...] = _dot(kdec, g1, NT)
        dk_scr[...] = dkdec * ekb
        wk = dkdec * kdec
        db_scr[...] -= wk
        dbl = jnp.sum(wk, axis=0, keepdims=True) + jnp.sum(g1 * st, axis=0, keepdims=True) * e_last
        ds_scr[...] = g1 * e_last + _dot(do, qe, TN)
        rowc = lax.broadcasted_iota(jnp.int32, (CHUNK, 1), 0)
        rows16 = lax.broadcasted_iota(jnp.int32, (SUB, 1), 0)
        da_full = _dot(do, vb, NT)
        a_tot = jnp.zeros((CHUNK, CHUNK), F32)
        for s in range(1, NSUB):
            lo = s * SUB
            bref = b_scr[lo - 1:lo, :]
            in_s = jnp.logical_and(rowc >= lo, rowc < lo + SUB)
            eq = jnp.exp(jnp.where(in_s, b - bref, -1e30))
            ek = jnp.exp(jnp.where(rowc < lo, bref - b, -1e30))
            qh = q * eq
            kh = k * ek
            a_tot = a_tot + _dot(qh, kh, NT)
            da = jnp.where(in_s, da_full, 0.0)
            dqh = _dot(da, kh)
            dkh = _dot(da, qh, TN)
            tq = dqh * qh
            tk = dkh * kh
            dq_scr[...] += dqh * eq
            dk_scr[...] += dkh * ek
            db_scr[...] += tq - tk
            db_scr[lo - 1:lo, :] += jnp.sum(tk, axis=0, keepdims=True) - jnp.sum(tq, axis=0, keepdims=True)
        dat_full = _dot(vb, do, NT)
        q_scr[...] = q
        lane = lax.broadcasted_iota(jnp.int32, (SUB, CHUNK), 1)
        diag = []
        for s in range(NSUB):
            lo = s * SUB
            qs, ks, bs = q[lo:lo + SUB], k[lo:lo + SUB], b[lo:lo + SUB]
            da_blk, dat_blk = da_full[lo:lo + SUB], dat_full[lo:lo + SUB]
            dqs = jnp.zeros((SUB, GLA_DK), F32)
            dks = jnp.zeros((SUB, GLA_DK), F32)
            dbs = jnp.zeros((SUB, GLA_DK), F32)
            s_blk = jnp.zeros((SUB, CHUNK), F32)
            for j in range(SUB):
                r = lo + j
                kj = k_ref[r:r + 1, :]
                e = jnp.exp(jnp.where(rows16 >= j, bs - b_scr[r:r + 1, :], -1e30))
                p = qs * e * kj
                s_blk = jnp.where(lane == r, jnp.sum(p, axis=1, keepdims=True), s_blk)
                dcol = jnp.sum(jnp.where(lane == r, da_blk, 0.0), axis=1, keepdims=True)
                dqs = dqs + (dcol * e) * kj
                dbs = dbs + dcol * p
            for i in range(SUB):
                r = lo + i
                e = jnp.exp(jnp.where(rows16 <= i, b_scr[r:r + 1, :] - bs, -1e30))
                drow = jnp.sum(jnp.where(lane == r, dat_blk, 0.0), axis=1, keepdims=True)
                nq = (drow * e) * q_scr[r:r + 1, :]
                dks = dks + nq
                dbs = dbs - nq * ks
            dq_scr[lo:lo + SUB, :] += dqs
            dk_scr[lo:lo + SUB, :] += dks
            db_scr[lo:lo + SUB, :] += dbs
            diag.append(s_blk)
        dv_scr[...] += _dot(a_tot + jnp.concatenate(diag, axis=0), do, TN)
        db_scr[CHUNK - 1:CHUNK, :] += dbl
        dla = _tri_sum(db_scr[...], True)
        dlogit = jnp.where(live, dla * (1.0 / GLA_TAU) * _sigmoid(-logit), 0.0)
        dl_ref[...] = dlogit
        dbg_ref[0] += jnp.sum(dlogit, axis=0, keepdims=True)
        dq_ref[...] = (dq_scr[...] * (GLA_DK ** -0.5)).astype(BF16)
        dk_ref[...] = dk_scr[...].astype(BF16)
        dv_ref[...] = dv_scr[...].astype(BF16)

    in_specs = _gla_in_specs(True, nc) + [
        pl.BlockSpec((CHUNK, GLA_DV), lambda h, n: (rn(n), h)),
        pl.BlockSpec((CHUNK, GLA_DV), lambda h, n: (rn(n), h)),
        pl.BlockSpec((1, 1, GLA_DV, GLA_DK), lambda h, n: (h, rn(n), 0, 0)),
    ]
    return pl.pallas_call(
        body, name="gla_bwd", grid=(GLA_HEADS, nc),
        in_specs=in_specs,
        out_specs=[pl.BlockSpec((CHUNK, GLA_DK), lambda h, n: (rn(n), h)),
                   pl.BlockSpec((CHUNK, GLA_DK), lambda h, n: (rn(n), h)),
                   pl.BlockSpec((CHUNK, GLA_DV), lambda h, n: (rn(n), h)),
                   pl.BlockSpec((CHUNK, GLA_DV), lambda h, n: (rn(n), h)),
                   pl.BlockSpec((CHUNK, GLA_DK), lambda h, n: (rn(n), h)),
                   pl.BlockSpec((1, 1, GLA_DV), lambda h, n: (h, 0, 0)),
                   pl.BlockSpec((1, 1, GLA_DK), lambda h, n: (h, 0, 0))],
        out_shape=[jax.ShapeDtypeStruct((rows, GLA_QK), BF16), jax.ShapeDtypeStruct((rows, GLA_QK), BF16),
                   jax.ShapeDtypeStruct((rows, GLA_W), BF16), jax.ShapeDtypeStruct((rows, GLA_W), BF16),
                   jax.ShapeDtypeStruct((rows, GLA_QK), F32),
                   jax.ShapeDtypeStruct((GLA_HEADS, 1, GLA_DV), F32),
                   jax.ShapeDtypeStruct((GLA_HEADS, 1, GLA_DK), F32)],
        scratch_shapes=[pltpu.VMEM((GLA_DV, GLA_DK), F32), pltpu.VMEM((CHUNK, GLA_DK), F32),
                        pltpu.VMEM((CHUNK, GLA_DK), F32), pltpu.VMEM((CHUNK, GLA_DV), F32),
                        pltpu.VMEM((CHUNK, GLA_DK), F32), pltpu.VMEM((CHUNK, GLA_DK), F32),
                        pltpu.VMEM((CHUNK, GLA_DK), F32)],
        compiler_params=pltpu.CompilerParams(dimension_semantics=("parallel", "arbitrary")),
    )(proj, proj, proj, proj, proj, wgate, bgate, normw, o_gla, d_oc, states)


def _adamw(name, w, g, m, v):
    rows, cols = w.shape
    tm = 8
    for cand in range(8, rows + 1, 8):
        if rows % cand == 0 and cand * cols * 4 <= 2 ** 21:
            tm = cand
    c1 = 1.0 - ADAM_B1 ** ADAM_STEP
    c2 = 1.0 - ADAM_B2 ** ADAM_STEP

    def body(w_ref, g_ref, m_ref, v_ref, d_ref, nm_ref, nv_ref):
        gv = g_ref[...]
        nm = ADAM_B1 * m_ref[...] + (1.0 - ADAM_B1) * gv
        nv = ADAM_B2 * v_ref[...] + (1.0 - ADAM_B2) * (gv * gv)
        nm_ref[...] = nm
        nv_ref[...] = nv
        d_ref[...] = -ADAM_LR * ((nm / c1) / (jnp.sqrt(nv / c2) + ADAM_EPS) + ADAM_WD * w_ref[...])

    blk = pl.BlockSpec((tm, cols), lambda i: (i, 0))
    return pl.pallas_call(
        body, name=name, grid=(rows // tm,),
        in_specs=[blk] * 4, out_specs=[blk] * 3,
        out_shape=[jax.ShapeDtypeStruct((rows, cols), F32)] * 3,
    )(w, g, m, v)


def _place():
    x, y, c = lax.axis_index("x"), lax.axis_index("y"), lax.axis_index("c")
    chips = [(1 - x, y), (x, 1 - y), (1 - x, 1 - y)]
    return x, y, c, chips


ANY = pl.BlockSpec(memory_space=pl.ANY)


def _gathered_struct(shape, dtype, kind):
    r, cc = shape
    if kind == "row":
        return jax.ShapeDtypeStruct((N_SHARD * r, cc), dtype)
    if kind == "col":
        return jax.ShapeDtypeStruct((r, N_SHARD * cc), dtype)
    return jax.ShapeDtypeStruct((N_SHARD, r, cc), dtype)


def _cast_place(name, w, kind, mine_arr, dtype, also_own=False):
    r, cc = w.shape
    tr = r
    for cand in (256, 128, 64, 32, 16):
        if r % cand == 0:
            tr = cand
            break
    nb = r // tr
    if kind == "row":
        o_spec = pl.BlockSpec((tr, cc), lambda i, m: (m[0] * nb + i, 0))
    elif kind == "col":
        o_spec = pl.BlockSpec((tr, cc), lambda i, m: (i, m[0]))
    else:
        o_spec = pl.BlockSpec((None, tr, cc), lambda i, m: (m[0], i, 0))
    w_spec = pl.BlockSpec((tr, cc), lambda i, m: (i, 0))

    def body(m_ref, w_ref, o_ref, *own_ref):
        o_ref[...] = w_ref[...].astype(o_ref.dtype)
        for ref in own_ref:
            ref[...] = w_ref[...].astype(ref.dtype)

    out_specs, out_shape = [o_spec], [_gathered_struct((r, cc), dtype, kind)]
    if also_own:
        out_specs.append(w_spec)
        out_shape.append(jax.ShapeDtypeStruct((r, cc), dtype))
    out = pl.pallas_call(
        body, name=name,
        grid_spec=pltpu.PrefetchScalarGridSpec(
            num_scalar_prefetch=1, grid=(nb,), in_specs=[w_spec], out_specs=out_specs),
        out_shape=out_shape,
    )(mine_arr, w)
    return out if also_own else out[0]


def _gather_small(shard):
    rows, cols = shard.shape

    def body(in_ref, out_ref, send_sems, recv_sems):
        x, y, c, chips = _place()
        mine = 2 * x + y
        out_ref[mine] = in_ref[...]
        cps = []
        for j, chip in enumerate(chips):
            cp = pltpu.make_async_remote_copy(
                src_ref=in_ref, dst_ref=out_ref.at[mine], send_sem=send_sems.at[j], recv_sem=recv_sems.at[j],
                device_id=(*chip, c), device_id_type=MESH)
            cp.start()
            cps.append(cp)
        for cp in cps:
            cp.wait()

    vm = pl.BlockSpec(memory_space=pltpu.VMEM)
    return pl.pallas_call(
        body, name="gather_small",
        in_specs=[vm], out_specs=vm,
        out_shape=jax.ShapeDtypeStruct((N_SHARD, rows, cols), F32),
        scratch_shapes=[pltpu.SemaphoreType.DMA((3,)), pltpu.SemaphoreType.DMA((3,))],
        compiler_params=pltpu.CompilerParams(has_side_effects=True),
    )(shard)


def _in_proj_shifted(name, a, b, n, shifts, tm, tn, out_cols, into=None):
    m, k = a.shape
    nb_b = b.shape[1] // tn
    nb_o = out_cols // tn

    def body(s_ref, a_ref, b_ref, *rest):
        rest[-1][...] = _dot(a_ref[...], b_ref[...])

    in_specs = [pl.BlockSpec((tm, k), lambda i, j, s: (i, 0)),
                pl.BlockSpec((k, tn), lambda i, j, s: (0, (s[0] + j) % nb_b))]
    operands = [shifts, a, b]
    aliases = {}
    if into is not None:
        in_specs.append(ANY)
        operands.append(into)
        aliases = {3: 0}
    return pl.pallas_call(
        body, name=name,
        grid_spec=pltpu.PrefetchScalarGridSpec(
            num_scalar_prefetch=1, grid=(m // tm, n // tn), in_specs=in_specs,
            out_specs=pl.BlockSpec((tm, tn), lambda i, j, s: (i, (s[1] + j) % nb_o))),
        out_shape=jax.ShapeDtypeStruct((m, out_cols), F32), input_output_aliases=aliases,
    )(*operands)


def _allreduce_small(buf):
    rows, cols = buf.shape

    def body(in_ref, out_ref, sib_ref, pair_ref, far_ref, send_sems, recv_sems):
        x, y, c, chips = _place()
        sibling = (x, y, 1 - c)
        to_sib = pltpu.make_async_remote_copy(
            src_ref=in_ref, dst_ref=sib_ref, send_sem=send_sems.at[0], recv_sem=recv_sems.at[0],
            device_id=sibling, device_id_type=MESH)
        to_sib.start()
        to_sib.wait()
        pair_ref[...] = in_ref[...] + sib_ref[...]
        far = [pltpu.make_async_remote_copy(
            src_ref=pair_ref, dst_ref=far_ref.at[j], send_sem=send_sems.at[1 + j], recv_sem=recv_sems.at[1 + j],
            device_id=(*chip, c), device_id_type=MESH) for j, chip in enumerate(chips)]
        for cp in far:
            cp.start()
        for cp in far:
            cp.wait()
        out_ref[...] = (pair_ref[...] + far_ref[1]) + (far_ref[0] + far_ref[2])

    vm = pl.BlockSpec(memory_space=pltpu.VMEM)
    return pl.pallas_call(
        body, name="allreduce_small",
        in_specs=[vm], out_specs=vm,
        out_shape=jax.ShapeDtypeStruct((rows, cols), F32),
        scratch_shapes=[pltpu.VMEM((rows, cols), F32), pltpu.VMEM((rows, cols), F32),
                        pltpu.VMEM((3, rows, cols), F32),
                        pltpu.SemaphoreType.DMA((4,)), pltpu.SemaphoreType.DMA((4,))],
        compiler_params=pltpu.CompilerParams(has_side_effects=True),
    )(buf)


def _shard_window(ref, kind, shard_shape, shard, half):
    r, cc = shard_shape
    hr = r // 2
    if kind == "row":
        return ref.at[pl.ds(_mo(shard * r + half * hr, 8), hr), :]
    if kind == "col":
        return ref.at[pl.ds(_mo(half * hr, 8), hr), pl.ds(_mo(shard * cc, 128), cc)]
    if kind == "colw":
        return ref.at[pl.ds(_mo(half * hr, 8), hr), pl.ds(_mo(shard * (cc - 128), 128), cc)]
    return ref.at[shard, pl.ds(_mo(half * hr, 8), hr), :]


HBM = pl.BlockSpec(memory_space=pltpu.HBM)
SEM = pl.BlockSpec(memory_space=pltpu.SEMAPHORE)
DATAFLOW = pltpu.SideEffectType.DATAFLOW_SIDE_EFFECTING


def _in_hbm(a):
    return pltpu.with_memory_space_constraint(a, pltpu.HBM)


def _empty_hbm(shape, dtype):
    return _in_hbm(lax.empty(shape, dtype))


def _copies_start(name, bufs, n_copies, plan, carry):
    nb = len(bufs)

    def body(*refs):
        send_sems, recv_sems = refs[nb + 1], refs[nb + 2]
        for k, (src, dst, to) in enumerate(plan(refs[:nb])):
            pltpu.make_async_remote_copy(src_ref=src, dst_ref=dst, send_sem=send_sems.at[k], recv_sem=recv_sems.at[k],
                                         device_id=to, device_id_type=MESH).start()

    passed = list(bufs) + [carry]
    out = pl.pallas_call(
        body, name=name,
        in_specs=[HBM] * (nb + 1), out_specs=[SEM, SEM] + [HBM] * (nb + 1),
        out_shape=[pltpu.SemaphoreType.DMA((n_copies,)), pltpu.SemaphoreType.DMA((n_copies,))]
        + [pltpu.HBM(a.shape, a.dtype) for a in passed],
        input_output_aliases={i: 2 + i for i in range(nb + 1)},
        compiler_params=pltpu.CompilerParams(has_side_effects=DATAFLOW),
    )(*[_in_hbm(a) for a in passed])
    return out[0], out[1], list(out[2:2 + nb]), out[2 + nb]


def _copies_wait(name, send_sems, recv_sems, bufs, plan, after):
    nb = len(bufs)
    after = list(after) if isinstance(after, (list, tuple)) else [after]

    def body(*refs):
        send, recv = refs[nb], refs[nb + 1]
        for k, (src, dst, to) in enumerate(plan(refs[:nb])):
            cp = pltpu.make_async_remote_copy(src_ref=src, dst_ref=dst, send_sem=send.at[k], recv_sem=recv.at[k],
                                              device_id=to, device_id_type=MESH)
            cp.wait_send()
            cp.wait_recv()

    out = pl.pallas_call(
        body, name=name,
        in_specs=[HBM] * nb + [SEM, SEM] + [ANY] * len(after), out_specs=[HBM] * nb,
        out_shape=[pltpu.HBM(a.shape, a.dtype) for a in bufs],
        input_output_aliases={i: i for i in range(nb)},
        compiler_params=pltpu.CompilerParams(has_side_effects=DATAFLOW),
    )(*bufs, send_sems, recv_sems, *after)
    return list(out)


def _gather_ici_plan(shard_shapes, kinds):
    n_arr = len(kinds)

    def plan(refs):
        x, y, c, chips = _place()
        out = []
        for i in range(n_arr):
            w = _shard_window(refs[i], kinds[i], shard_shapes[i], 2 * x + y, c)
            out += [(w, w, (*chip, c)) for chip in chips]
        return out

    return plan


def _gather_d2d_plan(shard_shapes, kinds):
    n_arr = len(kinds)

    def plan(refs):
        x, y, c, chips = _place()
        out = []
        for i in range(n_arr):
            for chip in chips:
                w = _shard_window(refs[i], kinds[i], shard_shapes[i], 2 * chip[0] + chip[1], c)
                out.append((w, w, (x, y, 1 - c)))
        return out

    return plan


def _rs_pair_plan(kinds, shard_shapes):
    n_arr = len(kinds)

    def plan(refs):
        x, y, c, _ = _place()
        out = []
        for i in range(n_arr):
            for s in range(N_SHARD):
                out.append((_shard_window(refs[i], kinds[i], shard_shapes[i], s, 1 - c), refs[n_arr + i].at[s],
                            (x, y, 1 - c)))
        return out

    return plan


def _rs_chip_plan(n_arr):
    def plan(refs):
        x, y, c, chips = _place()
        out = []
        for i in range(n_arr):
            for j, chip in enumerate(chips):
                out.append((refs[i].at[2 * chip[0] + chip[1]], refs[n_arr + i].at[j], (*chip, c)))
        return out

    return plan


def _rs_pair_add(name, grad, got, kind, shard_shape, c):
    r, cc = shard_shape
    hr = r // 2
    tr = hr
    for cand in (256, 128, 64, 32, 16):
        if hr % cand == 0:
            tr = cand
            break
    nb = hr // tr

    def body(c_ref, g_ref, t_ref, p_ref, pb_ref):
        p = g_ref[...] + t_ref[...]
        p_ref[...] = p
        pb_ref[...] = p.astype(BF16)

    out_shape = [jax.ShapeDtypeStruct((N_SHARD, hr, cc), F32), jax.ShapeDtypeStruct((N_SHARD, hr, cc), BF16)]
    if kind == "colw":
        tiles = cc // 128
        tr = hr
        g_spec = pl.BlockSpec((tr, 128), lambda s, t, cr: (cr[0], s * (tiles - 1) + t))
        t_spec = pl.BlockSpec((None, tr, 128), lambda s, t, cr: (s, 0, t))
        return pl.pallas_call(
            body, name=name,
            grid_spec=pltpu.PrefetchScalarGridSpec(
                num_scalar_prefetch=1, grid=(N_SHARD, tiles), in_specs=[g_spec, t_spec], out_specs=[t_spec, t_spec]),
            out_shape=out_shape,
        )(c, grad, got)
    if kind == "row":
        g_spec = pl.BlockSpec((tr, cc), lambda s, i, cr: (s * 2 * nb + cr[0] * nb + i, 0))
    elif kind == "col":
        g_spec = pl.BlockSpec((tr, cc), lambda s, i, cr: (cr[0] * nb + i, s))
    else:
        g_spec = pl.BlockSpec((None, tr, cc), lambda s, i, cr: (s, cr[0] * nb + i, 0))
    t_spec = pl.BlockSpec((None, tr, cc), lambda s, i, cr: (s, i, 0))
    return pl.pallas_call(
        body, name=name,
        grid_spec=pltpu.PrefetchScalarGridSpec(
            num_scalar_prefetch=1, grid=(N_SHARD, nb),
            in_specs=[g_spec, t_spec], out_specs=[t_spec, t_spec]),
        out_shape=out_shape,
    )(c, grad, got)


def _rs_chip_add(name, pair_f32, got, shard_shape, mine_c):
    r, cc = shard_shape
    hr = r // 2
    tr = hr
    for cand in (256, 128, 64, 32, 16):
        if hr % cand == 0:
            tr = cand
            break
    nb = hr // tr

    def body(mc_ref, p_ref, t0_ref, t1_ref, t2_ref, o_ref):
        o_ref[...] = (p_ref[...] + t1_ref[...].astype(F32)) + (t0_ref[...].astype(F32) + t2_ref[...].astype(F32))

    def far(j):
        return pl.BlockSpec((None, tr, cc), lambda i, mc: (j, i, 0))

    return pl.pallas_call(
        body, name=name,
        grid_spec=pltpu.PrefetchScalarGridSpec(
            num_scalar_prefetch=1, grid=(nb,),
            in_specs=[pl.BlockSpec((None, tr, cc), lambda i, mc: (mc[0], i, 0)), far(0), far(1), far(2)],
            out_specs=pl.BlockSpec((tr, cc), lambda i, mc: (mc[1] * nb + i, 0))),
        out_shape=jax.ShapeDtypeStruct((r, cc), F32),
    )(mine_c, pair_f32, got, got, got)


def _rs_pair_share(name, halves, shard_shapes):
    n_arr = len(halves)

    def body(*refs):
        ins = refs[:n_arr]
        outs = refs[n_arr:2 * n_arr]
        send_sems, recv_sems = refs[2 * n_arr:]
        x, y, c, _ = _place()
        sibling = (x, y, 1 - c)
        cps = []
        for i in range(n_arr):
            hr = shard_shapes[i][0] // 2
            rows = pl.ds(_mo(c * hr, 8), hr)
            cp = pltpu.make_async_remote_copy(
                src_ref=outs[i].at[rows, :], dst_ref=outs[i].at[rows, :],
                send_sem=send_sems.at[i], recv_sem=recv_sems.at[i],
                device_id=sibling, device_id_type=MESH)
            cp.start()
            cps.append(cp)
        for cp in cps:
            cp.wait()

    return pl.pallas_call(
        body, name=name,
        in_specs=[ANY] * n_arr, out_specs=[ANY] * n_arr,
        out_shape=[jax.ShapeDtypeStruct(s, F32) for s in shard_shapes],
        input_output_aliases={i: i for i in range(n_arr)},
        scratch_shapes=[pltpu.SemaphoreType.DMA((n_arr,)), pltpu.SemaphoreType.DMA((n_arr,))],
        compiler_params=pltpu.CompilerParams(has_side_effects=True),
    )(*halves)


def _pack(arrays):
    flat = []
    for a in arrays:
        v = a.reshape(-1).astype(F32)
        flat.append(jnp.pad(v, (0, (-v.shape[0]) % SMALL_COLS)))
    buf = jnp.concatenate(flat).reshape(-1, SMALL_COLS)
    return jnp.pad(buf, ((0, (-buf.shape[0]) % 16), (0, 0)))


def _unpack(buf, shapes):
    out = []
    row = 0
    for s in shapes:
        size = math.prod(s)
        nrow = -(-size // SMALL_COLS)
        out.append(buf[row:row + nrow].reshape(-1)[:size].reshape(s))
        row += nrow
    return out


def kernel(x, meta, norm_ab_w, w_in_ab, ret_norm_w, s5_lam_re, s5_lam_im, s5_log_dt, s5_b_re, s5_b_im, s5_c_re, s5_c_im, s5_d, s5_w_glu, w_out_ab, norm_c_w, w_in_c, gla_w_gate, gla_b_gate, gla_norm_w, w_out_c, final_norm_w, loss_target, m_meta, m_norm_ab_w, m_w_in_ab, m_ret_norm_w, m_s5_lam_re, m_s5_lam_im, m_s5_log_dt, m_s5_b_re, m_s5_b_im, m_s5_c_re, m_s5_c_im, m_s5_d, m_s5_w_glu, m_w_out_ab, m_norm_c_w, m_w_in_c, m_gla_w_gate, m_gla_b_gate, m_gla_norm_w, m_w_out_c, m_final_norm_w, v_meta, v_norm_ab_w, v_w_in_ab, v_ret_norm_w, v_s5_lam_re, v_s5_lam_im, v_s5_log_dt, v_s5_b_re, v_s5_b_im, v_s5_c_re, v_s5_c_im, v_s5_d, v_s5_w_glu, v_w_out_ab, v_norm_c_w, v_w_in_c, v_gla_w_gate, v_gla_b_gate, v_gla_norm_w, v_w_out_c, v_final_norm_w):
    seq = x.shape[1]
    rows = seq + CHUNK
    xi, yi, ci = lax.axis_index("x"), lax.axis_index("y"), lax.axis_index("c")
    mine = 2 * xi + yi
    c_arr = jnp.reshape(ci, (1,)).astype(jnp.int32)
    mine_c = jnp.stack([mine, ci]).astype(jnp.int32)

    mine_arr = jnp.reshape(mine, (1,)).astype(jnp.int32)
    small_shard = _pack([meta, norm_c_w, gla_norm_w, gla_b_gate, gla_w_gate[0]])
    small_all = _gather_small(small_shard)
    first_kinds = ["col"]
    first_shapes = [w_in_ab.shape[1:]]
    first_ici = _gather_ici_plan(first_shapes, first_kinds)
    first_d2d = _gather_d2d_plan(first_shapes, first_kinds)
    wab_buf, wab_own = _cast_place("place_w_in_ab", w_in_ab[0], "col", mine_arr, BF16, also_own=True)
    f_send, f_recv, f_bufs, small_all = _copies_start("gather_first_ici_start", [wab_buf], 3, first_ici, small_all)
    def late_group(items, kinds):
        shapes = [a.shape for _, a in items]
        bufs = [_cast_place("place_" + nm, a, kd, mine_arr, BF16) for (nm, a), kd in zip(items, kinds)]
        return bufs, _gather_ici_plan(shapes, kinds), _gather_d2d_plan(shapes, kinds), 3 * len(items)

    a_bufs, a_ici, a_d2d, n_a = late_group([("w_out_ab", w_out_ab[0]), ("w_glu", s5_w_glu[0])], ["row", "row"])
    b_bufs, b_ici, b_d2d, n_b = late_group([("w_in_c", w_in_c[0]), ("w_out_c", w_out_c[0])], ["stack", "row"])
    g_bufs = a_bufs + b_bufs
    cosf, sinf = _rope_tables(rows)
    rtab = _ret_tables()
    ab_re, ab_im, bb_re, bb_im = _s5_discretize(s5_lam_re[0], s5_lam_im[0], s5_log_dt[0], s5_b_re[0], s5_b_im[0])
    ab = (ab_re, ab_im)
    bd_b = (_bdiag_in(bb_re), _bdiag_in(bb_im))
    bd_c = (_bdiag_out(s5_c_re[0]), _bdiag_out(s5_c_im[0]))
    q4 = D_MODEL // N_SHARD
    g4 = GLA_QK // N_SHARD
    parts = [_unpack(small_all[j], [(N_META, q4), (1, q4), (1, q4), (1, g4), (GLA_RANK, g4)]) for j in range(N_SHARD)]
    meta_f, norm_c_f, gla_norm_f, bgate_f, wgate_f = [jnp.concatenate([p[i] for p in parts], axis=1) for i in range(5)]
    wgate_pad = jnp.pad(wgate_f, ((0, 128 - GLA_RANK), (0, 0)))

    h0, hn0 = _embed_norm(x[0], meta_f, norm_ab_w)

    tm = _row_tile(rows, 1408)
    tmk = _row_tile(rows, 1408)
    own_blocks = (IN_AB // N_SHARD) // 512
    shift_own = jnp.stack([jnp.zeros((), jnp.int32), mine.astype(jnp.int32) * own_blocks])
    shift_rest = jnp.stack([(mine.astype(jnp.int32) + 1) * own_blocks, (mine.astype(jnp.int32) + 1) * own_blocks])
    proj0 = _in_proj_shifted("in_proj_ab_own", hn0, wab_own, IN_AB // N_SHARD, shift_own, tm, 512, IN_AB)
    f_bufs = _copies_wait("gather_first_ici_wait", f_send, f_recv, f_bufs, first_ici,
                          [proj0, cosf, sinf, bd_b[0], bd_b[1], bd_c[0], bd_c[1]] + g_bufs + list(rtab))
    f_send, f_recv, f_bufs, cosf = _copies_start("gather_first_d2d_start", f_bufs, 3, first_d2d, cosf)
    wab, = _copies_wait("gather_first_d2d_wait", f_send, f_recv, f_bufs, first_d2d, cosf)
    a_send, a_recv, a_bufs, wab = _copies_start("gather_a_ici_start", a_bufs, n_a, a_ici, wab)
    b_send, b_recv, b_bufs, wab = _copies_start("gather_b_ici_start", b_bufs, n_b, b_ici, wab)
    proj0 = _in_proj_shifted("in_proj_ab_rest", hn0, wab, IN_AB - IN_AB // N_SHARD, shift_rest, tm, 512, IN_AB,
                             into=proj0)
    o_ret, o_a, ret_states = _ret_fwd(proj0, cosf, sinf, rtab, ret_norm_w)
    a_bufs = _copies_wait("gather_a_ici_wait", a_send, a_recv, a_bufs, a_ici, o_a)
    a_send, a_recv, a_bufs, proj0 = _copies_start("gather_a_d2d_start", a_bufs, n_a, a_d2d, proj0)
    y_s5, g_s5, s5_er, s5_ei = _s5_fwd(proj0, ab, bd_b, bd_c, s5_d)
    wout_ab, wglu = _copies_wait("gather_a_d2d_wait", a_send, a_recv, a_bufs, a_d2d, g_s5)
    zb_blk = (2 * RET_QK + 2 * RET_W + S5_W) // 512

    def glu_out(acc, gv, z):
        return gv.astype(F32) * _sigmoid(acc) * (z * _sigmoid(z))

    t_glu = _matmul("glu", g_s5, wglu, NN, rows, S5_W, S5_W, tm=tm, tn=512, tk=S5_W)
    o_b = _matmul("glu_out", g_s5, wglu, NN, rows, S5_W, S5_W, tm=tm, tn=512, tk=S5_W, out_dtype=BF16,
                  extras=[(g_s5, (tm, 512), lambda i, j, kk: (i, j)),
                          (proj0, (tm, 512), lambda i, j, kk: (i, zb_blk + j))],
                  epilogue=glu_out)
    b_bufs = _copies_wait("gather_b_ici_wait", b_send, b_recv, b_bufs, b_ici, o_b)
    b_send, b_recv, b_bufs, o_b = _copies_start("gather_b_d2d_start", b_bufs, n_b, b_d2d, o_b)
    h1 = _matmul("out_proj_ab", None, None, NN, rows, D_MODEL, OUT_AB, tm=tm, tn=512, tk=1024,
                 segs=[(o_a, (0, 0), wout_ab, (0, 0), RET_W, 1024),
                       (o_b, (0, 0), wout_ab, (RET_W // 1024, 0), S5_W, 1024)],
                 extras=[(h0, (tm, 512), lambda i, j, kk: (i, j))], epilogue=lambda acc, r: acc + r)
    wc_st, wout_c = _copies_wait("gather_b_d2d_wait", b_send, b_recv, b_bufs, b_d2d, h1)
    wc = jnp.concatenate([wc_st[j] for j in range(N_SHARD)] + [jnp.zeros((D_MODEL, IN_C_PAD - IN_C), BF16)], axis=1)

    hn1 = _rms_fwd("norm_c", h1, norm_c_f)
    proj1 = _matmul("in_proj_c", hn1, wc, NN, rows, IN_C_PAD, D_MODEL, tm=tm, tn=896, tk=D_MODEL)
    o_gla, o_c, gla_states = _gla_fwd(proj1, wgate_pad, bgate_f, gla_norm_f)
    h2 = _matmul("out_proj_c", o_c, wout_c, NN, rows, D_MODEL, GLA_W, tm=tm, tn=512, tk=GLA_W,
                 extras=[(h1, (tm, 512), lambda i, j, kk: (i, j))], epilogue=lambda acc, r: acc + r)
    loss_dev, dh2, d_final = _final_loss(h2, final_norm_w.reshape(1, D_MODEL), loss_target[0])

    g_wout_c = _matmul("d_w_out_c", o_c, dh2, TN, GLA_W, D_MODEL, rows, tm=1024, tn=1024, tk=tmk)
    d_oc = _matmul("d_o_c", dh2, wout_c, NT, rows, GLA_W, D_MODEL, tm=tm, tn=512, tk=1024)
    dq1, dk1, dv1, dz1, dlogit, d_gla_norm, d_bgate = _gla_bwd(proj1, wgate_pad, bgate_f, gla_norm_f, o_gla, d_oc, gla_states)
    gl_blk = (2 * GLA_QK + 2 * GLA_W) // 128
    dgl = _matmul("d_g_low", dlogit, wgate_pad, NT, rows, 128, GLA_QK, tm=tm, tn=128, tk=GLA_QK, out_dtype=BF16)
    g_wgate = _matmul("d_w_gate", proj1, dlogit, TN, 128, GLA_QK, rows, tm=128, tn=GLA_QK, tk=tmk, a_off=(0, gl_blk))
    dproj1 = jnp.concatenate([dq1, dk1, dv1, dz1, dgl], axis=1)
    g_wc = _matmul("d_w_in_c", hn1, dproj1, TN, D_MODEL, IN_C_PAD, rows, tm=1024, tn=896, tk=tmk)
    dhn1 = _matmul("d_hn1", dproj1, wc, NT, rows, D_MODEL, IN_C_PAD, tm=tm, tn=512, tk=896)
    dh1, d_norm_c = _rms_bwd("norm_c_bwd", dhn1, h1, norm_c_f, dh2)

    g_wout_ab = _matmul("d_w_out_ab_a", o_a, dh1, TN, RET_W, D_MODEL, rows, tm=1024, tn=1024, tk=tmk,
                        out_shape=jax.ShapeDtypeStruct((OUT_AB, D_MODEL), F32))
    g_wout_ab = _matmul("d_w_out_ab_b", o_b, dh1, TN, S5_W, D_MODEL, rows, tm=1024, tn=1024, tk=tmk,
                        into=(g_wout_ab, RET_W // 1024, 0))
    dmix = _matmul("d_mix", dh1, wout_ab, NT, rows, OUT_AB, D_MODEL, tm=tm, tn=512, tk=1024)
    dproj0, d_ret_norm = _ret_bwd(proj0, cosf, sinf, rtab, ret_norm_w, o_ret, dmix, ret_states)
    dproj0, dt_glu, dg_direct = _s5_gate_bwd(dmix, g_s5, t_glu, proj0, dproj0)
    g_wglu = _matmul("d_w_glu", g_s5, dt_glu, TN, S5_W, S5_W, rows, tm=1024, tn=1024, tk=tmk)
    dy_s5 = _matmul("d_y_s5", dt_glu, wglu, NT, rows, S5_W, S5_W, tm=tm, tn=512, tk=S5_W,
                    extras=[(dg_direct, (tm, 512), lambda i, j, kk: (i, j)),
                            (y_s5, (tm, 512), lambda i, j, kk: (i, j))],
                    epilogue=lambda acc, dg, yv: (acc + dg) * _gelu_grad(yv))
    wc_cols = IN_C // N_SHARD
    wc_win = (wc_cols // 128 + 1) * 128
    rs1_names = ["w_out_ab", "w_in_c", "w_out_c", "w_glu"]
    rs1_kinds = ["row", "colw", "row", "row"]
    rs1_shapes = [w_out_ab.shape[1:], (D_MODEL, wc_win), w_out_c.shape[1:], s5_w_glu.shape[1:]]
    rs1_plan = _rs_pair_plan(rs1_kinds, rs1_shapes)
    rs1_land = [_empty_hbm((N_SHARD, r // 2, cc), F32) for (r, cc) in rs1_shapes]
    p_send, p_recv, p_bufs, dy_s5 = _copies_start("rs1_pair_start", [g_wout_ab, g_wc, g_wout_c, g_wglu] + rs1_land,
                                                  N_SHARD * 4, rs1_plan, dy_s5)
    dproj0, dbr_d, dbi_d, dcr_d, dci_d, dar_p, dai_p, dd_p = _s5_bwd(proj0, dy_s5, ab, bd_b, bd_c, s5_d,
                                                                     (s5_er, s5_ei), dproj0)
    p_bufs = _copies_wait("rs1_pair_wait", p_send, p_recv, p_bufs, rs1_plan, dproj0)
    rs1_pairs = [_rs_pair_add("rs_pair_add_" + nm, g, t, kd, ss, c_arr)
                 for nm, g, t, kd, ss in zip(rs1_names, p_bufs[:4], p_bufs[4:], rs1_kinds, rs1_shapes)]
    rs1_chip_plan = _rs_chip_plan(4)
    rs1_land2 = [_empty_hbm((3, r // 2, cc), BF16) for (r, cc) in rs1_shapes]
    c_send, c_recv, c_bufs, dproj0 = _copies_start("rs1_chip_start", [p[1] for p in rs1_pairs] + rs1_land2, 12,
                                                   rs1_chip_plan, dproj0)
    g_wab = _matmul("d_w_in_ab", hn0, dproj0, TN, D_MODEL, IN_AB, rows, tm=1024, tn=1024, tk=tmk)
    rs2_shapes = [w_in_ab.shape[1:]]
    rs2_plan = _rs_pair_plan(["col"], rs2_shapes)
    rs2_land = [_empty_hbm((N_SHARD, rs2_shapes[0][0] // 2, rs2_shapes[0][1]), F32)]
    q_send, q_recv, q_bufs, dproj0 = _copies_start("rs2_pair_start", [g_wab] + rs2_land, N_SHARD, rs2_plan, dproj0)
    dhn0 = _matmul("d_hn0_a", dproj0, wab, NT, tm, D_MODEL, IN_AB, tm=tm, tn=512, tk=2048,
                   out_shape=jax.ShapeDtypeStruct((rows, D_MODEL), F32))
    q_bufs = _copies_wait("rs2_pair_wait", q_send, q_recv, q_bufs, rs2_plan, dhn0)
    rs2_pair = _rs_pair_add("rs_pair_add_w_in_ab", q_bufs[0], q_bufs[1], "col", rs2_shapes[0], c_arr)
    rs2_chip_plan = _rs_chip_plan(1)
    rs2_land2 = [_empty_hbm((3, rs2_shapes[0][0] // 2, rs2_shapes[0][1]), BF16)]
    r_send, r_recv, r_bufs, dhn0 = _copies_start("rs2_chip_start", [rs2_pair[1]] + rs2_land2, 3, rs2_chip_plan, dhn0)
    if rows > tm:
        dhn0 = _matmul("d_hn0_b", dproj0, wab, NT, rows - tm, D_MODEL, IN_AB, tm=tm, tn=512, tk=2048, a_off=(1, 0),
                       into=(dhn0, 1, 0))
    grad_x, d_meta, d_norm_ab = _rms_bwd_embed(dhn0, h0, norm_ab_w, dh1)
    c_bufs = _copies_wait("rs1_chip_wait", c_send, c_recv, c_bufs, rs1_chip_plan, grad_x)
    grad_x = grad_x[None]
    rs1_halves = [_rs_chip_add("rs_chip_add_" + nm, p[0], t, ss, mine_c)
                  for nm, p, t, ss in zip(rs1_names, rs1_pairs, c_bufs[4:], rs1_shapes)]
    g_w_out_ab, g_w_in_c, g_w_out_c, g_w_glu = _rs_pair_share("rs1_pair_share", rs1_halves, rs1_shapes)
    g_w_in_c = lax.dynamic_slice(g_w_in_c, (0, (wc_cols % 128) * mine), (D_MODEL, wc_cols))

    d_ab_re = jnp.sum(dar_p, axis=1).reshape(S5_G, S5_P)
    d_ab_im = jnp.sum(dai_p, axis=1).reshape(S5_G, S5_P)
    small_local = [loss_dev, d_meta, d_norm_ab, d_ret_norm.reshape(1, RET_W), d_ab_re, d_ab_im,
                   _bdiag_in_extract(dbr_d), _bdiag_in_extract(dbi_d),
                   _bdiag_out_extract(dcr_d), _bdiag_out_extract(dci_d),
                   jnp.sum(dd_p, axis=1).reshape(1, S5_W), d_norm_c, g_wgate[:GLA_RANK],
                   d_bgate.reshape(1, GLA_QK), d_gla_norm.reshape(1, GLA_W), d_final]
    small_shapes = [a.shape for a in small_local]
    summed = _unpack(_allreduce_small(_pack(small_local)), small_shapes)
    (loss, g_meta_f, g_norm_ab, g_ret_norm, g_ab_re, g_ab_im, g_bb_re, g_bb_im, g_c_re, g_c_im, g_d,
     g_norm_c_f, g_wgate_f, g_bgate_f, g_gla_norm_f, g_final) = summed
    _, s5_vjp = jax.vjp(_s5_discretize, s5_lam_re[0], s5_lam_im[0], s5_log_dt[0], s5_b_re[0], s5_b_im[0])
    g_lam_re, g_lam_im, g_log_dt, g_b_re, g_b_im = s5_vjp((g_ab_re, g_ab_im, g_bb_re, g_bb_im))

    def take(a, width):
        return lax.dynamic_slice_in_dim(a, mine * width, width, axis=1)

    grads = {
        "meta": take(g_meta_f, q4), "norm_ab_w": g_norm_ab, "ret_norm_w": g_ret_norm,
        "s5_lam_re": g_lam_re[None], "s5_lam_im": g_lam_im[None], "s5_log_dt": g_log_dt[None],
        "s5_b_re": g_b_re[None], "s5_b_im": g_b_im[None], "s5_c_re": g_c_re[None], "s5_c_im": g_c_im[None],
        "s5_d": g_d, "s5_w_glu": g_w_glu[None], "w_out_ab": g_w_out_ab[None], "norm_c_w": take(g_norm_c_f, q4),
        "w_in_c": g_w_in_c[None], "gla_w_gate": take(g_wgate_f, g4)[None], "gla_b_gate": take(g_bgate_f, g4),
        "gla_norm_w": take(g_gla_norm_f, q4), "w_out_c": g_w_out_c[None], "final_norm_w": g_final.reshape(D_MODEL),
    }
    weights = dict(meta=meta, norm_ab_w=norm_ab_w, w_in_ab=w_in_ab, ret_norm_w=ret_norm_w, s5_lam_re=s5_lam_re,
                   s5_lam_im=s5_lam_im, s5_log_dt=s5_log_dt, s5_b_re=s5_b_re, s5_b_im=s5_b_im, s5_c_re=s5_c_re,
                   s5_c_im=s5_c_im, s5_d=s5_d, s5_w_glu=s5_w_glu, w_out_ab=w_out_ab, norm_c_w=norm_c_w,
                   w_in_c=w_in_c, gla_w_gate=gla_w_gate, gla_b_gate=gla_b_gate, gla_norm_w=gla_norm_w,
                   w_out_c=w_out_c, final_norm_w=final_norm_w)
    m_in = dict(meta=m_meta, norm_ab_w=m_norm_ab_w, w_in_ab=m_w_in_ab, ret_norm_w=m_ret_norm_w,
                s5_lam_re=m_s5_lam_re, s5_lam_im=m_s5_lam_im, s5_log_dt=m_s5_log_dt, s5_b_re=m_s5_b_re,
                s5_b_im=m_s5_b_im, s5_c_re=m_s5_c_re, s5_c_im=m_s5_c_im, s5_d=m_s5_d, s5_w_glu=m_s5_w_glu,
                w_out_ab=m_w_out_ab, norm_c_w=m_norm_c_w, w_in_c=m_w_in_c, gla_w_gate=m_gla_w_gate,
                gla_b_gate=m_gla_b_gate, gla_norm_w=m_gla_norm_w, w_out_c=m_w_out_c, final_norm_w=m_final_norm_w)
    v_in = dict(meta=v_meta, norm_ab_w=v_norm_ab_w, w_in_ab=v_w_in_ab, ret_norm_w=v_ret_norm_w,
                s5_lam_re=v_s5_lam_re, s5_lam_im=v_s5_lam_im, s5_log_dt=v_s5_log_dt, s5_b_re=v_s5_b_re,
                s5_b_im=v_s5_b_im, s5_c_re=v_s5_c_re, s5_c_im=v_s5_c_im, s5_d=v_s5_d, s5_w_glu=v_s5_w_glu,
                w_out_ab=v_w_out_ab, norm_c_w=v_norm_c_w, w_in_c=v_w_in_c, gla_w_gate=v_gla_w_gate,
                gla_b_gate=v_gla_b_gate, gla_norm_w=v_gla_norm_w, w_out_c=v_w_out_c, final_norm_w=v_final_norm_w)
    order = list(weights)
    big_names = ["s5_w_glu", "w_out_ab", "w_in_c", "w_out_c", "w_in_ab"]
    small_names = [nm for nm in order if nm not in big_names]
    delta, new_m, new_v = {}, {}, {}

    def big_update(nm):
        shp = weights[nm].shape
        d2, m2, v2 = _adamw("adamw_" + nm, weights[nm][0], grads[nm][0], m_in[nm][0], v_in[nm][0])
        delta[nm], new_m[nm], new_v[nm] = d2.reshape(shp), m2.reshape(shp), v2.reshape(shp)

    for nm in big_names[:-1]:
        big_update(nm)
    sshapes = [weights[nm].shape for nm in small_names]
    d2, m2, v2 = _adamw("adamw_small", _pack([weights[nm] for nm in small_names]),
                        _pack([grads[nm] for nm in small_names]), _pack([m_in[nm] for nm in small_names]),
                        _pack([v_in[nm] for nm in small_names]))
    for nm, dd, mm, vv in zip(small_names, _unpack(d2, sshapes), _unpack(m2, sshapes), _unpack(v2, sshapes)):
        delta[nm], new_m[nm], new_v[nm] = dd, mm, vv
    r_bufs = _copies_wait("rs2_chip_wait", r_send, r_recv, r_bufs, rs2_chip_plan,
                          [v2] + [new_v[nm] for nm in big_names[:-1]])
    rs2_half = _rs_chip_add("rs_chip_add_w_in_ab", rs2_pair[0], r_bufs[1], rs2_shapes[0], mine_c)
    grads["w_in_ab"] = _rs_pair_share("rs2_pair_share", [rs2_half], rs2_shapes)[0][None]
    big_update("w_in_ab")
    grads = {nm: grads[nm].reshape(weights[nm].shape) for nm in order}
    return (loss.reshape(()), grad_x, *[grads[nm] for nm in order], *[delta[nm] for nm in order],
            *[new_m[nm] for nm in order], *[new_v[nm] for nm in order])
```

```python
import functools
import math

import jax
import jax.numpy as jnp
from jax import lax
from jax.experimental import pallas as pl
from jax.experimental.pallas import tpu as pltpu

F32 = jnp.float32
BF16 = jnp.bfloat16
MESH = pl.DeviceIdType.MESH

D_MODEL = 2048
N_META = 16
CHUNK = 128
SUB = 16
NSUB = CHUNK // SUB
PAD = CHUNK - N_META
EPS = 1e-6

RET_HEADS = 8
RET_DK = 128
RET_DV = 256
RET_QK = RET_HEADS * RET_DK
RET_W = RET_HEADS * RET_DV
ROPE_BASE = 10000.0

S5_W = 1024
S5_GH = 16
S5_G = S5_W // S5_GH
S5_P = 64
S5_TG = 8
S5_NT = S5_G // S5_TG
S5_TU = S5_TG * S5_GH
S5_TS = S5_TG * S5_P
S5_FWD_TILES = 2
S5_BWD_TILES = 1

GLA_HEADS = 4
GLA_DK = 256
GLA_DV = 512
GLA_QK = GLA_HEADS * GLA_DK
GLA_W = GLA_HEADS * GLA_DV
GLA_RANK = 16
GLA_TAU = 16.0

IN_AB = 2 * RET_QK + 2 * RET_W + 2 * S5_W
OUT_AB = RET_W + S5_W
IN_C = 2 * GLA_QK + 2 * GLA_W + GLA_RANK
IN_C_PAD = 2 * GLA_QK + 2 * GLA_W + 128

ADAM_LR = 0.001
ADAM_B1 = 0.9
ADAM_B2 = 0.999
ADAM_EPS = 1e-08
ADAM_WD = 0.01
ADAM_STEP = 10

N_SHARD = 4
SMALL_COLS = 512

NN = (((1,), (0,)), ((), ()))
NT = (((1,), (1,)), ((), ()))
TN = (((0,), (0,)), ((), ()))


def _dot(a, b, dims=NN):
    return lax.dot_general(a.astype(BF16), b.astype(BF16), dims, preferred_element_type=F32)


def _mo(v, m):
    return v if isinstance(v, int) else pl.multiple_of(v, m)


def _sigmoid(x):
    return 1.0 / (1.0 + jnp.exp(-x))


def _row_tile(rows, cap):
    n = rows // CHUNK
    best = 1
    for d in range(1, n + 1):
        if n % d == 0 and d * CHUNK <= cap:
            best = d
    return best * CHUNK


def _col_tile(cols, cap):
    n = cols // 128
    best = 1
    for d in range(1, n + 1):
        if n % d == 0 and d * 128 <= cap:
            best = d
    return best * 128


def _matmul(name, a, b, dims, m, n, k, *, tm, tn, tk, out_dtype=F32, a_off=(0, 0), b_off=(0, 0),
            extras=(), epilogue=None, out_shape=None, out_spec=None, segs=None, into=None):
    if segs is None:
        segs = [(a, a_off, b, b_off, k, tk)]
    assert m % tm == 0 and n % tn == 0, (name, m, n, tm, tn)
    starts, counts = [], []
    nk = 0
    for (_, _, _, _, ks, tks) in segs:
        assert ks % tks == 0, (name, ks, tks)
        starts.append(nk)
        counts.append(ks // tks)
        nk += ks // tks
    in_specs, operands = [], []
    for s, (sa, (ar, ac), sb, (br, bc), _, tks) in enumerate(segs):
        def kpos(kk, st=starts[s], cnt=counts[s]):
            return jnp.clip(kk - st, 0, cnt - 1) if len(segs) > 1 else kk

        if dims == NN:
            a_spec = pl.BlockSpec((tm, tks), lambda i, j, kk, p=kpos, r=ar, c=ac: (i + r, p(kk) + c))
            b_spec = pl.BlockSpec((tks, tn), lambda i, j, kk, p=kpos, r=br, c=bc: (p(kk) + r, j + c))
        elif dims == NT:
            a_spec = pl.BlockSpec((tm, tks), lambda i, j, kk, p=kpos, r=ar, c=ac: (i + r, p(kk) + c))
            b_spec = pl.BlockSpec((tn, tks), lambda i, j, kk, p=kpos, r=br, c=bc: (j + r, p(kk) + c))
        else:
            a_spec = pl.BlockSpec((tks, tm), lambda i, j, kk, p=kpos, r=ar, c=ac: (p(kk) + r, i + c))
            b_spec = pl.BlockSpec((tks, tn), lambda i, j, kk, p=kpos, r=br, c=bc: (p(kk) + r, j + c))
        in_specs += [a_spec, b_spec]
        operands += [sa, sb]
    n_seg = len(segs)
    n_extra = len(extras)
    if out_shape is None:
        out_shape = jax.ShapeDtypeStruct((m, n), out_dtype)

    def body(*refs):
        e_refs = refs[2 * n_seg:2 * n_seg + n_extra]
        n_in = 2 * n_seg + n_extra + (1 if into is not None else 0)
        o_ref = refs[n_in]
        if nk == 1:
            part = _dot(refs[0][...], refs[1][...], dims)
            if epilogue is not None:
                part = epilogue(part, *[e[...] for e in e_refs])
            o_ref[...] = part.astype(o_ref.dtype)
            return
        acc_ref = refs[n_in + 1]
        kk = pl.program_id(2)

        @pl.when(kk == 0)
        def _():
            acc_ref[...] = jnp.zeros_like(acc_ref)

        if n_seg == 1:
            acc_ref[...] += _dot(refs[0][...], refs[1][...], dims)
        else:
            for s in range(n_seg):
                @pl.when(jnp.logical_and(kk >= starts[s], kk < starts[s] + counts[s]))
                def _(s=s):
                    acc_ref[...] += _dot(refs[2 * s][...], refs[2 * s + 1][...], dims)

        @pl.when(kk == nk - 1)
        def _():
            acc = acc_ref[...]
            if epilogue is not None:
                acc = epilogue(acc, *[e[...] for e in e_refs])
            o_ref[...] = acc.astype(o_ref.dtype)

    if out_spec is None:
        out_spec = pl.BlockSpec((tm, tn), lambda i, j, kk: (i, j))
    in_specs += [pl.BlockSpec(bs, im) for (_, bs, im) in extras]
    operands += [e for (e, _, _) in extras]
    aliases = {}
    if into is not None:
        dest, ro, co = into
        out_shape = jax.ShapeDtypeStruct(dest.shape, dest.dtype)
        out_spec = pl.BlockSpec((tm, tn), lambda i, j, kk: (i + ro, j + co))
        aliases = {len(operands): 0}
        in_specs.append(ANY)
        operands.append(dest)
    return pl.pallas_call(
        body, name=name, grid=(m // tm, n // tn, nk),
        in_specs=in_specs, out_specs=out_spec, out_shape=out_shape, input_output_aliases=aliases,
        scratch_shapes=[] if nk == 1 else [pltpu.VMEM((tm, tn), F32)],
        compiler_params=pltpu.CompilerParams(dimension_semantics=("parallel", "parallel", "arbitrary")),
    )(*operands)


def _rms_fwd(name, h, w):
    rows, d = h.shape
    tm = _row_tile(rows, 512)

    def body(h_ref, w_ref, o_ref):
        x = h_ref[...]
        r = lax.rsqrt(jnp.mean(x * x, axis=-1, keepdims=True) + EPS)
        o_ref[...] = (x * r * w_ref[...]).astype(BF16)

    return pl.pallas_call(
        body, name=name, grid=(rows // tm,),
        in_specs=[pl.BlockSpec((tm, d), lambda i: (i, 0)), pl.BlockSpec((1, d), lambda i: (0, 0))],
        out_specs=pl.BlockSpec((tm, d), lambda i: (i, 0)),
        out_shape=jax.ShapeDtypeStruct((rows, d), BF16),
    )(h, w)


def _rms_bwd(name, dhn, h, w, dres):
    rows, d = h.shape
    tm = _row_tile(rows, 384)

    def body(g_ref, h_ref, w_ref, r_ref, dh_ref, dw_ref):
        i = pl.program_id(0)
        x = h_ref[...]
        r = lax.rsqrt(jnp.mean(x * x, axis=-1, keepdims=True) + EPS)
        xh = x * r
        g = g_ref[...]
        gw = g * w_ref[...]
        dh_ref[...] = r_ref[...] + r * (gw - xh * jnp.mean(gw * xh, axis=-1, keepdims=True))

        @pl.when(i == 0)
        def _():
            dw_ref[...] = jnp.zeros_like(dw_ref)

        dw_ref[...] += jnp.sum(g * xh, axis=0, keepdims=True)

    return pl.pallas_call(
        body, name=name, grid=(rows // tm,),
        in_specs=[pl.BlockSpec((tm, d), lambda i: (i, 0)), pl.BlockSpec((tm, d), lambda i: (i, 0)),
                  pl.BlockSpec((1, d), lambda i: (0, 0)), pl.BlockSpec((tm, d), lambda i: (i, 0))],
        out_specs=[pl.BlockSpec((tm, d), lambda i: (i, 0)), pl.BlockSpec((1, d), lambda i: (0, 0))],
        out_shape=[jax.ShapeDtypeStruct((rows, d), F32), jax.ShapeDtypeStruct((1, d), F32)],
    )(dhn, h, w, dres)


def _embed_norm(x, meta, w):
    seq, d = x.shape
    rows = seq + CHUNK

    def body(x_ref, m_ref, w_ref, h_ref, o_ref):
        i = pl.program_id(0)

        def emit(h):
            h_ref[...] = h
            r = lax.rsqrt(jnp.mean(h * h, axis=-1, keepdims=True) + EPS)
            o_ref[...] = (h * r * w_ref[...]).astype(BF16)

        @pl.when(i == 0)
        def _():
            emit(jnp.concatenate([jnp.zeros((PAD, d), F32), m_ref[...]], axis=0))

        @pl.when(i > 0)
        def _():
            emit(x_ref[...])

    blk = pl.BlockSpec((CHUNK, d), lambda i: (i, 0))
    return pl.pallas_call(
        body, name="embed_norm_ab", grid=(rows // CHUNK,),
        in_specs=[pl.BlockSpec((CHUNK, d), lambda i: (jnp.maximum(i - 1, 0), 0)),
                  pl.BlockSpec((N_META, d), lambda i: (0, 0)), pl.BlockSpec((1, d), lambda i: (0, 0))],
        out_specs=[blk, blk],
        out_shape=[jax.ShapeDtypeStruct((rows, d), F32), jax.ShapeDtypeStruct((rows, d), BF16)],
    )(x, meta, w)


def _rms_bwd_embed(dhn, h, w, dres):
    rows, d = h.shape
    seq = rows - CHUNK

    def body(g_ref, h_ref, w_ref, r_ref, gx_ref, gm_ref, dw_ref):
        i = pl.program_id(0)
        x = h_ref[...]
        r = lax.rsqrt(jnp.mean(x * x, axis=-1, keepdims=True) + EPS)
        xh = x * r
        g = g_ref[...]
        gw = g * w_ref[...]
        dh = r_ref[...] + r * (gw - xh * jnp.mean(gw * xh, axis=-1, keepdims=True))

        @pl.when(i == 0)
        def _():
            dw_ref[...] = jnp.zeros_like(dw_ref)
            gm_ref[...] = dh[PAD:]

        @pl.when(i > 0)
        def _():
            gx_ref[...] = dh

        dw_ref[...] += jnp.sum(g * xh, axis=0, keepdims=True)

    blk = pl.BlockSpec((CHUNK, d), lambda i: (i, 0))
    return pl.pallas_call(
        body, name="norm_ab_bwd", grid=(rows // CHUNK,),
        in_specs=[blk, blk, pl.BlockSpec((1, d), lambda i: (0, 0)), blk],
        out_specs=[pl.BlockSpec((CHUNK, d), lambda i: (jnp.maximum(i - 1, 0), 0)),
                   pl.BlockSpec((N_META, d), lambda i: (0, 0)), pl.BlockSpec((1, d), lambda i: (0, 0))],
        out_shape=[jax.ShapeDtypeStruct((seq, d), F32), jax.ShapeDtypeStruct((N_META, d), F32),
                   jax.ShapeDtypeStruct((1, d), F32)],
    )(dhn, h, w, dres)


def _final_loss(h2, w, target):
    rows, d = h2.shape

    def body(h_ref, w_ref, t_ref, loss_ref, dh_ref, dw_ref):
        i = pl.program_id(0)

        @pl.when(i == 0)
        def _():
            loss_ref[...] = jnp.zeros_like(loss_ref)
            dw_ref[...] = jnp.zeros_like(dw_ref)
            dh_ref[...] = jnp.zeros_like(dh_ref)

        @pl.when(i > 0)
        def _():
            x = h_ref[...]
            r = lax.rsqrt(jnp.mean(x * x, axis=-1, keepdims=True) + EPS)
            xh = x * r
            wv = w_ref[...]
            err = xh * wv - t_ref[...]
            loss_ref[...] += 0.5 * jnp.sum(jnp.mean(err * err, axis=-1, keepdims=True), axis=0, keepdims=True)
            g = err * (1.0 / d)
            gw = g * wv
            dh_ref[...] = r * (gw - xh * jnp.mean(gw * xh, axis=-1, keepdims=True))
            dw_ref[...] += jnp.sum(g * xh, axis=0, keepdims=True)

    return pl.pallas_call(
        body, name="final_loss", grid=(rows // CHUNK,),
        in_specs=[pl.BlockSpec((CHUNK, d), lambda i: (i, 0)), pl.BlockSpec((1, d), lambda i: (0, 0)),
                  pl.BlockSpec((CHUNK, d), lambda i: (jnp.maximum(i - 1, 0), 0))],
        out_specs=[pl.BlockSpec((1, 1), lambda i: (0, 0)), pl.BlockSpec((CHUNK, d), lambda i: (i, 0)),
                   pl.BlockSpec((1, d), lambda i: (0, 0))],
        out_shape=[jax.ShapeDtypeStruct((1, 1), F32), jax.ShapeDtypeStruct((rows, d), F32),
                   jax.ShapeDtypeStruct((1, d), F32)],
    )(h2, w, target)


def _gate_fwd(o, z, w):
    rs = lax.rsqrt(jnp.mean(o * o, axis=-1, keepdims=True) + EPS)
    return o * rs * w * (z * _sigmoid(z))


def _gate_bwd(dout, o, z, w):
    rs = lax.rsqrt(jnp.mean(o * o, axis=-1, keepdims=True) + EPS)
    yn = o * rs
    sg = _sigmoid(z)
    sil = z * sg
    dsil = sg * (1.0 + z * (1.0 - sg))
    dz = dout * yn * w * dsil
    dyn = dout * w * sil
    dw = jnp.sum(dout * yn * sil, axis=0, keepdims=True)
    do = rs * (dyn - yn * jnp.mean(dyn * yn, axis=-1, keepdims=True))
    return do, dz, dw


def _rope(t, cosf, sinf):
    return t * cosf + pltpu.roll(t, RET_DK // 2, 1) * sinf


def _rope_t(d, cosf, sinf):
    return d * cosf + pltpu.roll(d * sinf, RET_DK // 2, 1)


def _ret_tables():
    log_g = jnp.log1p(-jnp.exp2(-5.0 - jnp.arange(RET_HEADS, dtype=F32)))
    idx = jnp.arange(CHUNK, dtype=F32)
    diff = idx[:, None] - idx[None, :]
    decay = jnp.where(diff >= 0, jnp.exp(log_g[:, None, None] * jnp.maximum(diff, 0.0)), 0.0)
    kw = jnp.exp(log_g[:, None] * (CHUNK - 1 - idx))
    qw = jnp.exp(log_g[:, None] * (idx + 1.0))
    gch = jnp.exp(log_g * CHUNK)
    kw = jnp.broadcast_to(kw[:, :, None], (RET_HEADS, CHUNK, RET_DK))
    qw = jnp.broadcast_to(qw[:, :, None], (RET_HEADS, CHUNK, RET_DK))
    gch = jnp.broadcast_to(gch[:, None, None], (RET_HEADS, 1, RET_DV))
    return decay, kw, qw, gch


def _rope_tables(rows):
    pos = jnp.arange(rows, dtype=F32) - float(PAD)
    inv_freq = jnp.power(ROPE_BASE, -jnp.arange(0, RET_DK, 2, dtype=F32) / RET_DK)
    ang = pos[:, None] * inv_freq[None, :]
    cos, sin = jnp.cos(ang), jnp.sin(ang)
    return jnp.concatenate([cos, cos], axis=1), jnp.concatenate([-sin, sin], axis=1)


RET_HB = 8
RET_QB = RET_HB * RET_DK
RET_VB = RET_HB * RET_DV


def _ret_in_specs(rev, nc):
    def cn(n):
        return (nc - 1 - n) if rev else n
    kb = RET_QK // RET_QB
    vb = 2 * RET_QK // RET_VB
    zb = (2 * RET_QK + RET_W) // RET_VB
    return [
        pl.BlockSpec((CHUNK, RET_QB), lambda h, n: (cn(n), h)),
        pl.BlockSpec((CHUNK, RET_QB), lambda h, n: (cn(n), kb + h)),
        pl.BlockSpec((CHUNK, RET_VB), lambda h, n: (cn(n), vb + h)),
        pl.BlockSpec((CHUNK, RET_VB), lambda h, n: (cn(n), zb + h)),
        pl.BlockSpec((CHUNK, RET_DK), lambda h, n: (cn(n), 0)),
        pl.BlockSpec((CHUNK, RET_DK), lambda h, n: (cn(n), 0)),
        pl.BlockSpec((RET_HB, CHUNK, CHUNK), lambda h, n: (h, 0, 0)),
        pl.BlockSpec((RET_HB, CHUNK, RET_DK), lambda h, n: (h, 0, 0)),
        pl.BlockSpec((RET_HB, CHUNK, RET_DK), lambda h, n: (h, 0, 0)),
        pl.BlockSpec((RET_HB, 1, RET_DV), lambda h, n: (h, 0, 0)),
        pl.BlockSpec((1, RET_VB), lambda h, n: (0, h)),
    ]


def _ret_fwd(proj, cosf, sinf, tables, normw):
    rows = proj.shape[0]
    nc = rows // CHUNK
    decay, kw, qw, gch = tables

    def body(q_ref, k_ref, v_ref, z_ref, cos_ref, sin_ref, dm_ref, kw_ref, qw_ref, g_ref, w_ref,
             o_ref, oa_ref, st_ref, s_scr):
        n = pl.program_id(1)

        @pl.when(n == 0)
        def _():
            s_scr[...] = jnp.zeros_like(s_scr)

        cosv, sinv = cos_ref[...], sin_ref[...]
        for hh in range(RET_HB):
            qc = slice(hh * RET_DK, (hh + 1) * RET_DK)
            vc = slice(hh * RET_DV, (hh + 1) * RET_DV)
            q = _rope(q_ref[:, qc], cosv, sinv)
            k = _rope(k_ref[:, qc], cosv, sinv) * (RET_DK ** -0.5)
            v = v_ref[:, vc]
            s = s_scr[hh]
            st_ref[hh, 0] = s.astype(BF16)
            a = _dot(q, k, NT) * dm_ref[hh]
            o = _dot(a, v) + _dot(q * qw_ref[hh], s)
            s_scr[hh] = s * g_ref[hh] + _dot(k * kw_ref[hh], v, TN)
            o_ref[:, vc] = o
            oa_ref[:, vc] = _gate_fwd(o, z_ref[:, vc], w_ref[:, vc]).astype(BF16)

    return pl.pallas_call(
        body, name="ret_fwd", grid=(RET_HEADS // RET_HB, nc),
        in_specs=_ret_in_specs(False, nc),
        out_specs=[pl.BlockSpec((CHUNK, RET_VB), lambda h, n: (n, h)),
                   pl.BlockSpec((CHUNK, RET_VB), lambda h, n: (n, h)),
                   pl.BlockSpec((RET_HB, 1, RET_DK, RET_DV), lambda h, n: (h, n, 0, 0))],
        out_shape=[jax.ShapeDtypeStruct((rows, RET_W), F32), jax.ShapeDtypeStruct((rows, RET_W), BF16),
                   jax.ShapeDtypeStruct((RET_HEADS, nc, RET_DK, RET_DV), BF16)],
        scratch_shapes=[pltpu.VMEM((RET_HB, RET_DK, RET_DV), F32)],
        compiler_params=pltpu.CompilerParams(dimension_semantics=("parallel", "arbitrary")),
    )(proj, proj, proj, proj, cosf, sinf, decay, kw, qw, gch, normw)


def _ret_bwd(proj, cosf, sinf, tables, normw, o_ret, dmix, states):
    assert RET_HB == RET_HEADS
    rows = proj.shape[0]
    nc = rows // CHUNK
    decay, kw, qw, gch = tables
    ret_cols = 2 * RET_QK + 2 * RET_W

    def rn(n):
        return nc - 1 - n

    def body(q_ref, k_ref, v_ref, z_ref, cos_ref, sin_ref, dm_ref, kw_ref, qw_ref, g_ref, w_ref,
             o_ref, do_ref, st_ref, dp_ref, dw_ref, ds_scr):
        n = pl.program_id(1)
        dq_ref = dp_ref.at[:, 0:RET_QK]
        dk_ref = dp_ref.at[:, RET_QK:2 * RET_QK]
        dv_ref = dp_ref.at[:, 2 * RET_QK:2 * RET_QK + RET_W]
        dz_ref = dp_ref.at[:, 2 * RET_QK + RET_W:ret_cols]

        @pl.when(n == 0)
        def _():
            ds_scr[...] = jnp.zeros_like(ds_scr)
            dw_ref[...] = jnp.zeros_like(dw_ref)

        cosv, sinv = cos_ref[...], sin_ref[...]
        for hh in range(RET_HB):
            qc = slice(hh * RET_DK, (hh + 1) * RET_DK)
            vc = slice(hh * RET_DV, (hh + 1) * RET_DV)
            q = _rope(q_ref[:, qc], cosv, sinv)
            k = _rope(k_ref[:, qc], cosv, sinv) * (RET_DK ** -0.5)
            v = v_ref[:, vc]
            do, dz, dw = _gate_bwd(do_ref[:, vc], o_ref[:, vc], z_ref[:, vc], w_ref[:, vc])
            dz_ref[:, vc] = dz.astype(BF16)
            dw_ref[hh] += dw
            dm = dm_ref[hh]
            s = st_ref[hh, 0]
            g1 = ds_scr[hh]
            p = _dot(q, k, NT) * dm
            kwv = k * kw_ref[hh]
            qwv = q * qw_ref[hh]
            dp = _dot(do, v, NT)
            da = dp * dm
            dv = _dot(p, do, TN) + _dot(kwv, g1)
            dq = _dot(da, k) + _dot(do, s, NT) * qw_ref[hh]
            dk = _dot(da, q, TN) + _dot(v, g1, NT) * kw_ref[hh]
            ds_scr[hh] = g1 * g_ref[hh] + _dot(qwv, do, TN)
            dv_ref[:, vc] = dv.astype(BF16)
            dq_ref[:, qc] = _rope_t(dq, cosv, sinv).astype(BF16)
            dk_ref[:, qc] = _rope_t(dk * (RET_DK ** -0.5), cosv, sinv).astype(BF16)

    in_specs = _ret_in_specs(True, nc) + [
        pl.BlockSpec((CHUNK, RET_VB), lambda h, n: (rn(n), h)),
        pl.BlockSpec((CHUNK, RET_VB), lambda h, n: (rn(n), h)),
        pl.BlockSpec((RET_HB, 1, RET_DK, RET_DV), lambda h, n: (h, rn(n), 0, 0)),
    ]
    return pl.pallas_call(
        body, name="ret_bwd", grid=(RET_HEADS // RET_HB, nc),
        in_specs=in_specs,
        out_specs=[pl.BlockSpec((CHUNK, ret_cols), lambda h, n: (rn(n), 0)),
                   pl.BlockSpec((RET_HB, 1, RET_DV), lambda h, n: (h, 0, 0))],
        out_shape=[jax.ShapeDtypeStruct((rows, IN_AB), BF16), jax.ShapeDtypeStruct((RET_HEADS, 1, RET_DV), F32)],
        scratch_shapes=[pltpu.VMEM((RET_HB, RET_DK, RET_DV), F32)],
        compiler_params=pltpu.CompilerParams(dimension_semantics=("parallel", "arbitrary")),
    )(proj, proj, proj, proj, cosf, sinf, decay, kw, qw, gch, normw, o_ret, dmix, states)


def _s5_discretize(lam_re, lam_im, log_dt, b_re, b_im):
    dt = jnp.exp(log_dt)[:, None]
    mag = jnp.exp(lam_re * dt)
    ab_re, ab_im = mag * jnp.cos(lam_im * dt), mag * jnp.sin(lam_im * dt)
    den = lam_re * lam_re + lam_im * lam_im
    nr, ni = ab_re - 1.0, ab_im
    f_re = (nr * lam_re + ni * lam_im) / den
    f_im = (ni * lam_re - nr * lam_im) / den
    bb_re = f_re[..., None] * b_re - f_im[..., None] * b_im
    bb_im = f_re[..., None] * b_im + f_im[..., None] * b_re
    return ab_re, ab_im, bb_re, bb_im


def _bdiag_in(bb):
    t = bb.reshape(S5_NT, S5_TG, S5_P, S5_GH).transpose(0, 1, 3, 2)
    eye = jnp.eye(S5_TG, dtype=bb.dtype)
    full = t[:, :, :, None, :] * eye[None, :, None, :, None]
    return full.reshape(S5_NT, S5_TU, S5_TS)


def _bdiag_in_extract(dense):
    t = dense.reshape(S5_NT, S5_TG, S5_GH, S5_TG, S5_P)
    diag = jnp.stack([t[:, g, :, g, :] for g in range(S5_TG)], axis=1)
    return diag.transpose(0, 1, 3, 2).reshape(S5_G, S5_P, S5_GH)


def _bdiag_out(c):
    t = c.reshape(S5_NT, S5_TG, S5_GH, S5_P).transpose(0, 1, 3, 2)
    eye = jnp.eye(S5_TG, dtype=c.dtype)
    full = t[:, :, :, None, :] * eye[None, :, None, :, None]
    return full.reshape(S5_NT, S5_TS, S5_TU)


def _bdiag_out_extract(dense):
    t = dense.reshape(S5_NT, S5_TG, S5_P, S5_TG, S5_GH)
    diag = jnp.stack([t[:, g, :, g, :] for g in range(S5_TG)], axis=1)
    return diag.transpose(0, 1, 3, 2).reshape(S5_G, S5_GH, S5_P)


def _cmul(ar, ai, br, bi):
    return ar * br - ai * bi, ar * bi + ai * br


S5_SEG = 8
S5_STEPS = CHUNK // S5_SEG


def _seg_perm(x):
    c = x.shape[1]
    return jnp.swapaxes(x.reshape(S5_SEG, S5_STEPS, c), 0, 1).reshape(CHUNK, c)


def _seg_unperm(x):
    c = x.shape[1]
    return jnp.swapaxes(x.reshape(S5_STEPS, S5_SEG, c), 0, 1).reshape(CHUNK, c)


def _rows(x, p):
    return x[p * S5_SEG:(p + 1) * S5_SEG]


def _s5_tables(ar, ai, tr_scr, ti_scr, wfr_scr, wfi_scr, wbr_scr, wbi_scr):
    row = lax.broadcasted_iota(jnp.int32, (S5_SEG, 1), 0)
    a8r = jnp.broadcast_to(ar, (S5_SEG, S5_TS))
    a8i = jnp.broadcast_to(ai, (S5_SEG, S5_TS))
    pr, pi = a8r, a8i
    for p in range(S5_STEPS):
        tr_scr[p * S5_SEG:(p + 1) * S5_SEG, :] = pr
        ti_scr[p * S5_SEG:(p + 1) * S5_SEG, :] = pi
        if p < S5_STEPS - 1:
            pr, pi = _cmul(pr, pi, a8r, a8i)
    wr, wi = pr, pi
    sh = 1
    while sh < S5_SEG:
        keep = row >= sh
        sr = jnp.where(keep, pltpu.roll(wr, sh, 0), 1.0)
        si = jnp.where(keep, pltpu.roll(wi, sh, 0), 0.0)
        wr, wi = _cmul(wr, wi, sr, si)
        sh *= 2
    wfr_scr[...] = wr
    wfi_scr[...] = wi
    wr, wi = pr, -pi
    sh = 1
    while sh < S5_SEG:
        keep = row < S5_SEG - sh
        sr = jnp.where(keep, pltpu.roll(wr, S5_SEG - sh, 0), 1.0)
        si = jnp.where(keep, pltpu.roll(wi, S5_SEG - sh, 0), 0.0)
        wr, wi = _cmul(wr, wi, sr, si)
        sh *= 2
    wbr_scr[...] = wr
    wbi_scr[...] = wi


def _seg_scan(vr, vi, ar, ai, tr_scr, ti_scr, wr_scr, wi_scr, c0r, c0i, down):
    row = lax.broadcasted_iota(jnp.int32, (S5_SEG, 1), 0)
    sgn = 1.0 if down else -1.0
    order = list(range(S5_STEPS)) if down else list(range(S5_STEPS - 1, -1, -1))
    xr, xi = _rows(vr, order[0]), _rows(vi, order[0])
    loc = {order[0]: (xr, xi)}
    for p in order[1:]:
        mr, mi = _cmul(ar, sgn * ai, xr, xi)
        xr, xi = mr + _rows(vr, p), mi + _rows(vi, p)
        loc[p] = (xr, xi)
    last = S5_STEPS - 1
    mr, mi = tr_scr[last * S5_SEG:(last + 1) * S5_SEG, :], sgn * ti_scr[last * S5_SEG:(last + 1) * S5_SEG, :]
    er, ei = xr, xi
    sh = 1
    while sh < S5_SEG:
        if down:
            keep = row >= sh
            sr, si = pltpu.roll(er, sh, 0), pltpu.roll(ei, sh, 0)
        else:
            keep = row < S5_SEG - sh
            sr, si = pltpu.roll(er, S5_SEG - sh, 0), pltpu.roll(ei, S5_SEG - sh, 0)
        pr, pi = _cmul(mr, mi, jnp.where(keep, sr, 0.0), jnp.where(keep, si, 0.0))
        er, ei = er + pr, ei + pi
        mr, mi = _cmul(mr, mi, mr, mi)
        sh *= 2
    pr, pi = _cmul(wr_scr[...], wi_scr[...], c0r, c0i)
    er, ei = er + pr, ei + pi
    if down:
        nr = jnp.where(row == 0, c0r, pltpu.roll(er, 1, 0))
        ni = jnp.where(row == 0, c0i, pltpu.roll(ei, 1, 0))
    else:
        nr = jnp.where(row == S5_SEG - 1, c0r, pltpu.roll(er, S5_SEG - 1, 0))
        ni = jnp.where(row == S5_SEG - 1, c0i, pltpu.roll(ei, S5_SEG - 1, 0))
    out_r, out_i = [], []
    for p in range(S5_STEPS):
        q = p if down else S5_STEPS - 1 - p
        pr, pi = _cmul(tr_scr[q * S5_SEG:(q + 1) * S5_SEG, :], sgn * ti_scr[q * S5_SEG:(q + 1) * S5_SEG, :], nr, ni)
        out_r.append(loc[p][0] + pr)
        out_i.append(loc[p][1] + pi)
    return jnp.concatenate(out_r, axis=0), jnp.concatenate(out_i, axis=0), (nr, ni), (er, ei)


def _gelu(y):
    c = math.sqrt(2.0 / math.pi)
    return 0.5 * y * (1.0 + jnp.tanh(c * (y + 0.044715 * y * y * y)))


def _gelu_grad(y):
    c = math.sqrt(2.0 / math.pi)
    th = jnp.tanh(c * (y + 0.044715 * y * y * y))
    return 0.5 * (1.0 + th) + 0.5 * y * (1.0 - th * th) * c * (1.0 + 3.0 * 0.044715 * y * y)


def _s5_fwd(proj, ab, bd_b, bd_c, dvec):
    rows = proj.shape[0]
    nc = rows // CHUNK
    tps = S5_FWD_TILES
    ubw = tps * S5_TU
    ub = (2 * RET_QK + 2 * RET_W) // ubw
    ab_re, ab_im = ab
    bre, bim = bd_b
    cre, cim = bd_c

    def body(u_ref, ar_ref, ai_ref, bre_ref, bim_ref, cre_ref, cim_ref, d_ref,
             y_ref, g_ref, er_ref, ei_ref, tr_scr, ti_scr, wfr_scr, wfi_scr, wbr_scr, wbi_scr,
             cr_scr, ci_scr, er_scr, ei_scr):
        n = pl.program_id(1)
        for tt in range(tps):
            cols = slice(tt * S5_TU, (tt + 1) * S5_TU)
            ar, ai = ar_ref[tt], ai_ref[tt]
            trs, tis, wfr, wfi = tr_scr.at[tt], ti_scr.at[tt], wfr_scr.at[tt], wfi_scr.at[tt]

            @pl.when(n == 0)
            def _(tt=tt, ar=ar, ai=ai, trs=trs, tis=tis, wfr=wfr, wfi=wfi):
                _s5_tables(ar, ai, trs, tis, wfr, wfi, wbr_scr.at[tt], wbi_scr.at[tt])
                cr_scr[tt] = jnp.zeros((S5_SEG, S5_TS), F32)
                ci_scr[tt] = jnp.zeros((S5_SEG, S5_TS), F32)

            u = _seg_perm(u_ref[:, cols])
            c0r, c0i = cr_scr[tt], ci_scr[tt]
            er_ref[tt, 0] = c0r
            ei_ref[tt, 0] = c0i
            xr, xi, _, (er, ei) = _seg_scan(_dot(u, bre_ref[tt]), _dot(u, bim_ref[tt]), ar, ai, trs, tis,
                                            wfr, wfi, c0r, c0i, True)
            er_scr[tt] = er
            ei_scr[tt] = ei
            cr_scr[tt] = jnp.broadcast_to(er_scr[tt, S5_SEG - 1:S5_SEG, :], (S5_SEG, S5_TS))
            ci_scr[tt] = jnp.broadcast_to(ei_scr[tt, S5_SEG - 1:S5_SEG, :], (S5_SEG, S5_TS))
            y = _seg_unperm(_dot(xr, cre_ref[tt]) - _dot(xi, cim_ref[tt]) + d_ref[:, cols] * u)
            y_ref[:, cols] = y
            g_ref[:, cols] = _gelu(y).astype(BF16)

    vec = pl.BlockSpec((tps, 1, S5_TS), lambda t, n: (t, 0, 0))
    return pl.pallas_call(
        body, name="s5_fwd", grid=(S5_NT // tps, nc),
        in_specs=[pl.BlockSpec((CHUNK, ubw), lambda t, n: (n, ub + t)), vec, vec,
                  pl.BlockSpec((tps, S5_TU, S5_TS), lambda t, n: (t, 0, 0)),
                  pl.BlockSpec((tps, S5_TU, S5_TS), lambda t, n: (t, 0, 0)),
                  pl.BlockSpec((tps, S5_TS, S5_TU), lambda t, n: (t, 0, 0)),
                  pl.BlockSpec((tps, S5_TS, S5_TU), lambda t, n: (t, 0, 0)),
                  pl.BlockSpec((1, ubw), lambda t, n: (0, t))],
        out_specs=[pl.BlockSpec((CHUNK, ubw), lambda t, n: (n, t)),
                   pl.BlockSpec((CHUNK, ubw), lambda t, n: (n, t)),
                   pl.BlockSpec((tps, 1, 8, S5_TS), lambda t, n: (t, n, 0, 0)),
                   pl.BlockSpec((tps, 1, 8, S5_TS), lambda t, n: (t, n, 0, 0))],
        out_shape=[jax.ShapeDtypeStruct((rows, S5_W), F32), jax.ShapeDtypeStruct((rows, S5_W), BF16),
                   jax.ShapeDtypeStruct((S5_NT, nc, 8, S5_TS), F32),
                   jax.ShapeDtypeStruct((S5_NT, nc, 8, S5_TS), F32)],
        scratch_shapes=[pltpu.VMEM((tps, CHUNK, S5_TS), F32) for _ in range(2)]
        + [pltpu.VMEM((tps, S5_SEG, S5_TS), F32) for _ in range(8)],
        compiler_params=pltpu.CompilerParams(dimension_semantics=("parallel", "arbitrary")),
    )(proj, ab_re.reshape(S5_NT, 1, S5_TS), ab_im.reshape(S5_NT, 1, S5_TS), bre, bim, cre, cim, dvec)


def _s5_bwd(proj, dy, ab, bd_b, bd_c, dvec, entry, dproj):
    rows = proj.shape[0]
    nc = rows // CHUNK
    tps = S5_BWD_TILES
    ubw = tps * S5_TU
    ub = (2 * RET_QK + 2 * RET_W) // ubw
    ab_re, ab_im = ab
    bre, bim = bd_b
    cre, cim = bd_c
    er, ei = entry

    def rn(n):
        return nc - 1 - n

    def body(u_ref, dy_ref, ar_ref, ai_ref, bre_ref, bim_ref, cre_ref, cim_ref, d_ref, er_ref, ei_ref, dp_ref,
             du_ref, dbr_ref, dbi_ref, dcr_ref, dci_ref, dar_ref, dai_ref, dd_ref,
             tr_scr, ti_scr, wfr_scr, wfi_scr, wbr_scr, wbi_scr, gr_scr, gi_scr, er_scr, ei_scr):
        n = pl.program_id(1)

        @pl.when(n == 0)
        def _():
            gr_scr[...] = jnp.zeros_like(gr_scr)
            gi_scr[...] = jnp.zeros_like(gi_scr)
            for r in (dbr_ref, dbi_ref, dcr_ref, dci_ref, dar_ref, dai_ref, dd_ref):
                r[...] = jnp.zeros_like(r)

        for tt in range(tps):
            cols = slice(tt * S5_TU, (tt + 1) * S5_TU)
            ar, ai = ar_ref[tt], ai_ref[tt]
            trs, tis = tr_scr.at[tt], ti_scr.at[tt]

            @pl.when(n == 0)
            def _(tt=tt, ar=ar, ai=ai, trs=trs, tis=tis):
                _s5_tables(ar, ai, trs, tis, wfr_scr.at[tt], wfi_scr.at[tt], wbr_scr.at[tt], wbi_scr.at[tt])

            u = _seg_perm(u_ref[:, cols])
            dy = _seg_perm(dy_ref[:, cols])
            xr, xi, (pr, pi), _ = _seg_scan(_dot(u, bre_ref[tt]), _dot(u, bim_ref[tt]), ar, ai, trs, tis,
                                            wfr_scr.at[tt], wfi_scr.at[tt], er_ref[tt, 0], ei_ref[tt, 0], True)
            dcr_ref[tt] += _dot(xr, dy, TN)
            dci_ref[tt] -= _dot(xi, dy, TN)
            gr, gi, _, (er, ei) = _seg_scan(_dot(dy, cre_ref[tt], NT), -_dot(dy, cim_ref[tt], NT), ar, ai, trs, tis,
                                            wbr_scr.at[tt], wbi_scr.at[tt], gr_scr[tt], gi_scr[tt], False)
            er_scr[tt] = er
            ei_scr[tt] = ei
            gr_scr[tt] = jnp.broadcast_to(er_scr[tt, 0:1, :], (S5_SEG, S5_TS))
            gi_scr[tt] = jnp.broadcast_to(ei_scr[tt, 0:1, :], (S5_SEG, S5_TS))
            xpr = jnp.concatenate([pr, xr[:CHUNK - S5_SEG]], axis=0)
            xpi = jnp.concatenate([pi, xi[:CHUNK - S5_SEG]], axis=0)
            dar_ref[tt] += jnp.sum((xpr * gr + xpi * gi).reshape(S5_STEPS, S5_SEG, S5_TS), axis=0)
            dai_ref[tt] += jnp.sum((xpr * gi - xpi * gr).reshape(S5_STEPS, S5_SEG, S5_TS), axis=0)
            dbr_ref[tt] += _dot(u, gr, TN)
            dbi_ref[tt] += _dot(u, gi, TN)
            dd_ref[tt] += jnp.sum((dy * u).reshape(S5_STEPS, S5_SEG, S5_TU), axis=0)
            du = dy * d_ref[:, cols] + _dot(gr, bre_ref[tt], NT) + _dot(gi, bim_ref[tt], NT)
            du_ref[:, cols] = _seg_unperm(du).astype(BF16)

    vec = pl.BlockSpec((tps, 1, S5_TS), lambda t, n: (t, 0, 0))
    acc_b = pl.BlockSpec((tps, S5_TU, S5_TS), lambda t, n: (t, 0, 0))
    acc_c = pl.BlockSpec((tps, S5_TS, S5_TU), lambda t, n: (t, 0, 0))
    acc_a = pl.BlockSpec((tps, 8, S5_TS), lambda t, n: (t, 0, 0))
    ent = pl.BlockSpec((tps, 1, 8, S5_TS), lambda t, n: (t, rn(n), 0, 0))
    return pl.pallas_call(
        body, name="s5_bwd", grid=(S5_NT // tps, nc),
        in_specs=[pl.BlockSpec((CHUNK, ubw), lambda t, n: (rn(n), ub + t)),
                  pl.BlockSpec((CHUNK, ubw), lambda t, n: (rn(n), t)), vec, vec,
                  acc_b, acc_b, acc_c, acc_c, pl.BlockSpec((1, ubw), lambda t, n: (0, t)), ent, ent, ANY],
        out_specs=[pl.BlockSpec((CHUNK, ubw), lambda t, n: (rn(n), ub + t)), acc_b, acc_b, acc_c, acc_c, acc_a, acc_a,
                   pl.BlockSpec((tps, 8, S5_TU), lambda t, n: (t, 0, 0))],
        input_output_aliases={11: 0},
        out_shape=[jax.ShapeDtypeStruct(dproj.shape, BF16),
                   jax.ShapeDtypeStruct((S5_NT, S5_TU, S5_TS), F32), jax.ShapeDtypeStruct((S5_NT, S5_TU, S5_TS), F32),
                   jax.ShapeDtypeStruct((S5_NT, S5_TS, S5_TU), F32), jax.ShapeDtypeStruct((S5_NT, S5_TS, S5_TU), F32),
                   jax.ShapeDtypeStruct((S5_NT, 8, S5_TS), F32), jax.ShapeDtypeStruct((S5_NT, 8, S5_TS), F32),
                   jax.ShapeDtypeStruct((S5_NT, 8, S5_TU), F32)],
        scratch_shapes=[pltpu.VMEM((tps, CHUNK, S5_TS), F32) for _ in range(2)]
        + [pltpu.VMEM((tps, S5_SEG, S5_TS), F32) for _ in range(8)],
        compiler_params=pltpu.CompilerParams(dimension_semantics=("parallel", "arbitrary")),
    )(proj, dy, ab_re.reshape(S5_NT, 1, S5_TS), ab_im.reshape(S5_NT, 1, S5_TS), bre, bim, cre, cim, dvec, er, ei,
      dproj)


def _s5_gate_bwd(dmix, g, t, proj, dproj):
    rows = g.shape[0]
    tm = _row_tile(rows, 384)
    ob = RET_W // S5_W
    zb = (2 * RET_QK + 2 * RET_W + S5_W) // S5_W

    def body(do_ref, g_ref, t_ref, z_ref, dp_ref, dz_ref, dt_ref, dg_ref):
        do = do_ref[...]
        gv = g_ref[...].astype(F32)
        z = z_ref[...]
        st = _sigmoid(t_ref[...])
        sg = _sigmoid(z)
        os5 = gv * st
        dz_ref[...] = (do * os5 * sg * (1.0 + z * (1.0 - sg))).astype(BF16)
        dos = do * z * sg
        dt_ref[...] = (dos * gv * st * (1.0 - st)).astype(BF16)
        dg_ref[...] = dos * st

    blk = pl.BlockSpec((tm, S5_W), lambda i: (i, 0))
    return pl.pallas_call(
        body, name="s5_gate_bwd", grid=(rows // tm,),
        in_specs=[pl.BlockSpec((tm, S5_W), lambda i: (i, ob)), blk, blk,
                  pl.BlockSpec((tm, S5_W), lambda i: (i, zb)), ANY],
        out_specs=[pl.BlockSpec((tm, S5_W), lambda i: (i, zb)), blk, blk],
        out_shape=[jax.ShapeDtypeStruct(dproj.shape, BF16), jax.ShapeDtypeStruct((rows, S5_W), BF16),
                   jax.ShapeDtypeStruct((rows, S5_W), F32)],
        input_output_aliases={4: 0},
    )(dmix, g, t, proj, dproj)


def _split3(x):
    hi = x.astype(BF16)
    r = x - hi.astype(F32)
    mid = r.astype(BF16)
    lo = (r - mid.astype(F32)).astype(BF16)
    return hi, mid, lo


def _tri_sum(x, upper):
    i = lax.broadcasted_iota(jnp.int32, (CHUNK, CHUNK), 0)
    j = lax.broadcasted_iota(jnp.int32, (CHUNK, CHUNK), 1)
    tri = jnp.where((j >= i) if upper else (j <= i), 1.0, 0.0).astype(BF16)
    hi, mid, lo = _split3(x)
    return _dot(tri, lo) + _dot(tri, mid) + _dot(tri, hi)


def _gla_log_decay(gl, wg, bg, n):
    logit = _dot(gl, wg) + bg
    la = (jnp.minimum(logit, 0.0) - jnp.log(1.0 + jnp.exp(-jnp.abs(logit)))) * (1.0 / GLA_TAU)
    row = lax.broadcasted_iota(jnp.int32, (CHUNK, 1), 0)
    live = jnp.logical_or(n > 0, row >= PAD)
    return logit, jnp.where(live, la, 0.0), live


def _gla_in_specs(rev, nc):
    def cn(n):
        return (nc - 1 - n) if rev else n
    kb = GLA_QK // GLA_DK
    vb = 2 * GLA_QK // GLA_DV
    zb = (2 * GLA_QK + GLA_W) // GLA_DV
    gb = (2 * GLA_QK + 2 * GLA_W) // 128
    return [
        pl.BlockSpec((CHUNK, GLA_DK), lambda h, n: (cn(n), h)),
        pl.BlockSpec((CHUNK, GLA_DK), lambda h, n: (cn(n), kb + h)),
        pl.BlockSpec((CHUNK, GLA_DV), lambda h, n: (cn(n), vb + h)),
        pl.BlockSpec((CHUNK, GLA_DV), lambda h, n: (cn(n), zb + h)),
        pl.BlockSpec((CHUNK, 128), lambda h, n: (cn(n), gb)),
        pl.BlockSpec((128, GLA_DK), lambda h, n: (0, h)),
        pl.BlockSpec((1, GLA_DK), lambda h, n: (0, h)),
        pl.BlockSpec((1, GLA_DV), lambda h, n: (0, h)),
    ]


def _gla_fwd(proj, wgate, bgate, normw):
    rows = proj.shape[0]
    nc = rows // CHUNK

    def body(q_ref, k_ref, v_ref, z_ref, gl_ref, wg_ref, bg_ref, w_ref, o_ref, oc_ref, st_ref, s_scr, b_scr):
        n = pl.program_id(1)

        @pl.when(n == 0)
        def _():
            s_scr[...] = jnp.zeros_like(s_scr)

        q = q_ref[...] * (GLA_DK ** -0.5)
        k = k_ref[...]
        v = v_ref[...]
        vb = v.astype(BF16)
        _, la, _ = _gla_log_decay(gl_ref[...], wg_ref[...], bg_ref[...], n)
        b = _tri_sum(la, False)
        b_scr[...] = b
        b_last = b_scr[CHUNK - 1:CHUNK, :]
        st = s_scr[...]
        st_ref[0, 0] = st
        s_scr[...] = st * jnp.exp(b_last) + _dot(v, k * jnp.exp(b_last - b), TN)
        rowc = lax.broadcasted_iota(jnp.int32, (CHUNK, 1), 0)
        rows16 = lax.broadcasted_iota(jnp.int32, (SUB, 1), 0)
        a_tot = jnp.zeros((CHUNK, CHUNK), F32)
        for s in range(1, NSUB):
            lo = s * SUB
            bref = b_scr[lo - 1:lo, :]
            in_s = jnp.logical_and(rowc >= lo, rowc < lo + SUB)
            qh = q * jnp.exp(jnp.where(in_s, b - bref, -1e30))
            kh = k * jnp.exp(jnp.where(rowc < lo, bref - b, -1e30))
            a_tot = a_tot + _dot(qh, kh, NT)
        lane = lax.broadcasted_iota(jnp.int32, (SUB, CHUNK), 1)
        diag = []
        for s in range(NSUB):
            lo = s * SUB
            qs, bs = q[lo:lo + SUB], b[lo:lo + SUB]
            s_blk = jnp.zeros((SUB, CHUNK), F32)
            for j in range(SUB):
                r = lo + j
                e = jnp.exp(jnp.where(rows16 >= j, bs - b_scr[r:r + 1, :], -1e30))
                col = jnp.sum(qs * k_ref[r:r + 1, :] * e, axis=1, keepdims=True)
                s_blk = jnp.where(lane == r, col, s_blk)
            diag.append(s_blk)
        o = _dot(q * jnp.exp(b), st, NT) + _dot(a_tot + jnp.concatenate(diag, axis=0), vb)
        o_ref[...] = o
        oc_ref[...] = _gate_fwd(o, z_ref[...], w_ref[...]).astype(BF16)

    return pl.pallas_call(
        body, name="gla_fwd", grid=(GLA_HEADS, nc),
        in_specs=_gla_in_specs(False, nc),
        out_specs=[pl.BlockSpec((CHUNK, GLA_DV), lambda h, n: (n, h)),
                   pl.BlockSpec((CHUNK, GLA_DV), lambda h, n: (n, h)),
                   pl.BlockSpec((1, 1, GLA_DV, GLA_DK), lambda h, n: (h, n, 0, 0))],
        out_shape=[jax.ShapeDtypeStruct((rows, GLA_W), F32), jax.ShapeDtypeStruct((rows, GLA_W), BF16),
                   jax.ShapeDtypeStruct((GLA_HEADS, nc, GLA_DV, GLA_DK), F32)],
        scratch_shapes=[pltpu.VMEM((GLA_DV, GLA_DK), F32), pltpu.VMEM((CHUNK, GLA_DK), F32)],
        compiler_params=pltpu.CompilerParams(dimension_semantics=("parallel", "arbitrary")),
    )(proj, proj, proj, proj, proj, wgate, bgate, normw)


def _gla_bwd(proj, wgate, bgate, normw, o_gla, d_oc, states):
    rows = proj.shape[0]
    nc = rows // CHUNK

    def rn(n):
        return nc - 1 - n

    def body(q_ref, k_ref, v_ref, z_ref, gl_ref, wg_ref, bg_ref, w_ref, o_ref, do_ref, st_ref,
             dq_ref, dk_ref, dv_ref, dz_ref, dl_ref, dw_ref, dbg_ref,
             ds_scr, dq_scr, dk_scr, dv_scr, db_scr, b_scr, q_scr):
        n = pl.program_id(1)
        cn = rn(n)

        @pl.when(n == 0)
        def _():
            ds_scr[...] = jnp.zeros_like(ds_scr)
            dw_ref[...] = jnp.zeros_like(dw_ref)
            dbg_ref[...] = jnp.zeros_like(dbg_ref)

        q = q_ref[...] * (GLA_DK ** -0.5)
        k = k_ref[...]
        v = v_ref[...]
        vb = v.astype(BF16)
        do, dz, dw = _gate_bwd(do_ref[...], o_ref[...], z_ref[...], w_ref[...])
        dz_ref[...] = dz.astype(BF16)
        dw_ref[0] += dw
        logit, la, live = _gla_log_decay(gl_ref[...], wg_ref[...], bg_ref[...], cn)
        b = _tri_sum(la, False)
        b_scr[...] = b
        b_last = b_scr[CHUNK - 1:CHUNK, :]
        e_last = jnp.exp(b_last)
        st = st_ref[0, 0]
        g1 = ds_scr[...]
        eb = jnp.exp(b)
        qe = q * eb
        dqe = _dot(do, st)
        dq_scr[...] = dqe * eb
        db_scr[...] = dqe * qe
        ekb = jnp.exp(b_last - b)
        kdec = k * ekb
        dkdec = _dot(v, g1)
        dv_scr[...] = _dot(kdec, g1, NT)
        dk_scr[...] = dkdec * ekb
        wk = dkdec * kdec
        db_scr[...] -= wk
        dbl = jnp.sum(wk, axis=0, keepdims=True) + jnp.sum(g1 * st, axis=0, keepdims=True) * e_last
        ds_scr[...] = g1 * e_last + _dot(do, qe, TN)
        rowc = lax.broadcasted_iota(jnp.int32, (CHUNK, 1), 0)
        rows16 = lax.broadcasted_iota(jnp.int32, (SUB, 1), 0)
        da_full = _dot(do, vb, NT)
        a_tot = jnp.zeros((CHUNK, CHUNK), F32)
        for s in range(1, NSUB):
            lo = s * SUB
            bref = b_scr[lo - 1:lo, :]
            in_s = jnp.logical_and(rowc >= lo, rowc < lo + SUB)
            eq = jnp.exp(jnp.where(in_s, b - bref, -1e30))
            ek = jnp.exp(jnp.where(rowc < lo, bref - b, -1e30))
            qh = q * eq
            kh = k * ek
            a_tot = a_tot + _dot(qh, kh, NT)
            da = jnp.where(in_s, da_full, 0.0)
            dqh = _dot(da, kh)
            dkh = _dot(da, qh, TN)
            tq = dqh * qh
            tk = dkh * kh
            dq_scr[...] += dqh * eq
            dk_scr[...] += dkh * ek
            db_scr[...] += tq - tk
            db_scr[lo - 1:lo, :] += jnp.sum(tk, axis=0, keepdims=True) - jnp.sum(tq, axis=0, keepdims=True)
        dat_full = _dot(vb, do, NT)
        q_scr[...] = q
        lane = lax.broadcasted_iota(jnp.int32, (SUB, CHUNK), 1)
        diag = []
        for s in range(NSUB):
            lo = s * SUB
            qs, ks, bs = q[lo:lo + SUB], k[lo:lo + SUB], b[lo:lo + SUB]
            da_blk, dat_blk = da_full[lo:lo + SUB], dat_full[lo:lo + SUB]
            dqs = jnp.zeros((SUB, GLA_DK), F32)
            dks = jnp.zeros((SUB, GLA_DK), F32)
            dbs = jnp.zeros((SUB, GLA_DK), F32)
            s_blk = jnp.zeros((SUB, CHUNK), F32)
            for j in range(SUB):
                r = lo + j
                kj = k_ref[r:r + 1, :]
                e = jnp.exp(jnp.where(rows16 >= j, bs - b_scr[r:r + 1, :], -1e30))
                p = qs * e * kj
                s_blk = jnp.where(lane == r, jnp.sum(p, axis=1, keepdims=True), s_blk)
                dcol = jnp.sum(jnp.where(lane == r, da_blk, 0.0), axis=1, keepdims=True)
                dqs = dqs + (dcol * e) * kj
                dbs = dbs + dcol * p
            for i in range(SUB):
                r = lo + i
                e = jnp.exp(jnp.where(rows16 <= i, b_scr[r:r + 1, :] - bs, -1e30))
                drow = jnp.sum(jnp.where(lane == r, dat_blk, 0.0), axis=1, keepdims=True)
                nq = (drow * e) * q_scr[r:r + 1, :]
                dks = dks + nq
                dbs = dbs - nq * ks
            dq_scr[lo:lo + SUB, :] += dqs
            dk_scr[lo:lo + SUB, :] += dks
            db_scr[lo:lo + SUB, :] += dbs
            diag.append(s_blk)
        dv_scr[...] += _dot(a_tot + jnp.concatenate(diag, axis=0), do, TN)
        db_scr[CHUNK - 1:CHUNK, :] += dbl
        dla = _tri_sum(db_scr[...], True)
        dlogit = jnp.where(live, dla * (1.0 / GLA_TAU) * _sigmoid(-logit), 0.0)
        dl_ref[...] = dlogit
        dbg_ref[0] += jnp.sum(dlogit, axis=0, keepdims=True)
        dq_ref[...] = (dq_scr[...] * (GLA_DK ** -0.5)).astype(BF16)
        dk_ref[...] = dk_scr[...].astype(BF16)
        dv_ref[...] = dv_scr[...].astype(BF16)

    in_specs = _gla_in_specs(True, nc) + [
        pl.BlockSpec((CHUNK, GLA_DV), lambda h, n: (rn(n), h)),
        pl.BlockSpec((CHUNK, GLA_DV), lambda h, n: (rn(n), h)),
        pl.BlockSpec((1, 1, GLA_DV, GLA_DK), lambda h, n: (h, rn(n), 0, 0)),
    ]
    return pl.pallas_call(
        body, name="gla_bwd", grid=(GLA_HEADS, nc),
        in_specs=in_specs,
        out_specs=[pl.BlockSpec((CHUNK, GLA_DK), lambda h, n: (rn(n), h)),
                   pl.BlockSpec((CHUNK, GLA_DK), lambda h, n: (rn(n), h)),
                   pl.BlockSpec((CHUNK, GLA_DV), lambda h, n: (rn(n), h)),
                   pl.BlockSpec((CHUNK, GLA_DV), lambda h, n: (rn(n), h)),
                   pl.BlockSpec((CHUNK, GLA_DK), lambda h, n: (rn(n), h)),
                   pl.BlockSpec((1, 1, GLA_DV), lambda h, n: (h, 0, 0)),
                   pl.BlockSpec((1, 1, GLA_DK), lambda h, n: (h, 0, 0))],
        out_shape=[jax.ShapeDtypeStruct((rows, GLA_QK), BF16), jax.ShapeDtypeStruct((rows, GLA_QK), BF16),
                   jax.ShapeDtypeStruct((rows, GLA_W), BF16), jax.ShapeDtypeStruct((rows, GLA_W), BF16),
                   jax.ShapeDtypeStruct((rows, GLA_QK), F32),
                   jax.ShapeDtypeStruct((GLA_HEADS, 1, GLA_DV), F32),
                   jax.ShapeDtypeStruct((GLA_HEADS, 1, GLA_DK), F32)],
        scratch_shapes=[pltpu.VMEM((GLA_DV, GLA_DK), F32), pltpu.VMEM((CHUNK, GLA_DK), F32),
                        pltpu.VMEM((CHUNK, GLA_DK), F32), pltpu.VMEM((CHUNK, GLA_DV), F32),
                        pltpu.VMEM((CHUNK, GLA_DK), F32), pltpu.VMEM((CHUNK, GLA_DK), F32),
                        pltpu.VMEM((CHUNK, GLA_DK), F32)],
        compiler_params=pltpu.CompilerParams(dimension_semantics=("parallel", "arbitrary")),
    )(proj, proj, proj, proj, proj, wgate, bgate, normw, o_gla, d_oc, states)


def _adamw(name, w, g, m, v):
    rows, cols = w.shape
    tm = 8
    for cand in range(8, rows + 1, 8):
        if rows % cand == 0 and cand * cols * 4 <= 2 ** 21:
            tm = cand
    c1 = 1.0 - ADAM_B1 ** ADAM_STEP
    c2 = 1.0 - ADAM_B2 ** ADAM_STEP

    def body(w_ref, g_ref, m_ref, v_ref, d_ref, nm_ref, nv_ref):
        gv = g_ref[...]
        nm = ADAM_B1 * m_ref[...] + (1.0 - ADAM_B1) * gv
        nv = ADAM_B2 * v_ref[...] + (1.0 - ADAM_B2) * (gv * gv)
        nm_ref[...] = nm
        nv_ref[...] = nv
        d_ref[...] = -ADAM_LR * ((nm / c1) / (jnp.sqrt(nv / c2) + ADAM_EPS) + ADAM_WD * w_ref[...])

    blk = pl.BlockSpec((tm, cols), lambda i: (i, 0))
    return pl.pallas_call(
        body, name=name, grid=(rows // tm,),
        in_specs=[blk] * 4, out_specs=[blk] * 3,
        out_shape=[jax.ShapeDtypeStruct((rows, cols), F32)] * 3,
    )(w, g, m, v)


def _place():
    x, y, c = lax.axis_index("x"), lax.axis_index("y"), lax.axis_index("c")
    chips = [(1 - x, y), (x, 1 - y), (1 - x, 1 - y)]
    return x, y, c, chips


ANY = pl.BlockSpec(memory_space=pl.ANY)


def _gathered_struct(shape, dtype, kind):
    r, cc = shape
    if kind == "row":
        return jax.ShapeDtypeStruct((N_SHARD * r, cc), dtype)
    if kind == "col":
        return jax.ShapeDtypeStruct((r, N_SHARD * cc), dtype)
    return jax.ShapeDtypeStruct((N_SHARD, r, cc), dtype)


def _cast_place(name, w, kind, mine_arr, dtype, also_own=False):
    r, cc = w.shape
    tr = r
    for cand in (256, 128, 64, 32, 16):
        if r % cand == 0:
            tr = cand
            break
    nb = r // tr
    if kind == "row":
        o_spec = pl.BlockSpec((tr, cc), lambda i, m: (m[0] * nb + i, 0))
    elif kind == "col":
        o_spec = pl.BlockSpec((tr, cc), lambda i, m: (i, m[0]))
    else:
        o_spec = pl.BlockSpec((None, tr, cc), lambda i, m: (m[0], i, 0))
    w_spec = pl.BlockSpec((tr, cc), lambda i, m: (i, 0))

    def body(m_ref, w_ref, o_ref, *own_ref):
        o_ref[...] = w_ref[...].astype(o_ref.dtype)
        for ref in own_ref:
            ref[...] = w_ref[...].astype(ref.dtype)

    out_specs, out_shape = [o_spec], [_gathered_struct((r, cc), dtype, kind)]
    if also_own:
        out_specs.append(w_spec)
        out_shape.append(jax.ShapeDtypeStruct((r, cc), dtype))
    out = pl.pallas_call(
        body, name=name,
        grid_spec=pltpu.PrefetchScalarGridSpec(
            num_scalar_prefetch=1, grid=(nb,), in_specs=[w_spec], out_specs=out_specs),
        out_shape=out_shape,
    )(mine_arr, w)
    return out if also_own else out[0]


def _gather_small(shard):
    rows, cols = shard.shape

    def body(in_ref, out_ref, send_sems, recv_sems):
        x, y, c, chips = _place()
        mine = 2 * x + y
        out_ref[mine] = in_ref[...]
        cps = []
        for j, chip in enumerate(chips):
            cp = pltpu.make_async_remote_copy(
                src_ref=in_ref, dst_ref=out_ref.at[mine], send_sem=send_sems.at[j], recv_sem=recv_sems.at[j],
                device_id=(*chip, c), device_id_type=MESH)
            cp.start()
            cps.append(cp)
        for cp in cps:
            cp.wait()

    vm = pl.BlockSpec(memory_space=pltpu.VMEM)
    return pl.pallas_call(
        body, name="gather_small",
        in_specs=[vm], out_specs=vm,
        out_shape=jax.ShapeDtypeStruct((N_SHARD, rows, cols), F32),
        scratch_shapes=[pltpu.SemaphoreType.DMA((3,)), pltpu.SemaphoreType.DMA((3,))],
        compiler_params=pltpu.CompilerParams(has_side_effects=True),
    )(shard)


def _in_proj_shifted(name, a, b, n, shifts, tm, tn, out_cols, into=None):
    m, k = a.shape
    nb_b = b.shape[1] // tn
    nb_o = out_cols // tn

    def body(s_ref, a_ref, b_ref, *rest):
        rest[-1][...] = _dot(a_ref[...], b_ref[...])

    in_specs = [pl.BlockSpec((tm, k), lambda i, j, s: (i, 0)),
                pl.BlockSpec((k, tn), lambda i, j, s: (0, (s[0] + j) % nb_b))]
    operands = [shifts, a, b]
    aliases = {}
    if into is not None:
        in_specs.append(ANY)
        operands.append(into)
        aliases = {3: 0}
    return pl.pallas_call(
        body, name=name,
        grid_spec=pltpu.PrefetchScalarGridSpec(
            num_scalar_prefetch=1, grid=(m // tm, n // tn), in_specs=in_specs,
            out_specs=pl.BlockSpec((tm, tn), lambda i, j, s: (i, (s[1] + j) % nb_o))),
        out_shape=jax.ShapeDtypeStruct((m, out_cols), F32), input_output_aliases=aliases,
    )(*operands)


def _allreduce_small(buf):
    rows, cols = buf.shape
    hr = rows // 2

    def body(in_ref, out_ref, sib_ref, pair_ref, far_ref, send_sems, recv_sems):
        x, y, c, chips = _place()
        sibling = (x, y, 1 - c)
        mine = pl.ds(pl.multiple_of(c * hr, 8), hr)
        theirs = pl.ds(pl.multiple_of((1 - c) * hr, 8), hr)
        to_sib = pltpu.make_async_remote_copy(
            src_ref=in_ref.at[theirs, :], dst_ref=sib_ref, send_sem=send_sems.at[0], recv_sem=recv_sems.at[0],
            device_id=sibling, device_id_type=MESH)
        to_sib.start()
        to_sib.wait()
        pair_ref[...] = in_ref[mine, :] + sib_ref[...]
        far = [pltpu.make_async_remote_copy(
            src_ref=pair_ref, dst_ref=far_ref.at[j], send_sem=send_sems.at[1 + j], recv_sem=recv_sems.at[1 + j],
            device_id=(*chip, c), device_id_type=MESH) for j, chip in enumerate(chips)]
        for cp in far:
            cp.start()
        for cp in far:
            cp.wait()
        out_ref[mine, :] = (pair_ref[...] + far_ref[1]) + (far_ref[0] + far_ref[2])
        swap = pltpu.make_async_remote_copy(
            src_ref=out_ref.at[mine, :], dst_ref=out_ref.at[mine, :], send_sem=send_sems.at[4],
            recv_sem=recv_sems.at[4], device_id=sibling, device_id_type=MESH)
        swap.start()
        swap.wait()

    vm = pl.BlockSpec(memory_space=pltpu.VMEM)
    return pl.pallas_call(
        body, name="allreduce_small",
        in_specs=[vm], out_specs=vm,
        out_shape=jax.ShapeDtypeStruct((rows, cols), F32),
        scratch_shapes=[pltpu.VMEM((hr, cols), F32), pltpu.VMEM((hr, cols), F32),
                        pltpu.VMEM((3, hr, cols), F32),
                        pltpu.SemaphoreType.DMA((5,)), pltpu.SemaphoreType.DMA((5,))],
        compiler_params=pltpu.CompilerParams(has_side_effects=True),
    )(buf)


def _shard_window(ref, kind, shard_shape, shard, half):
    r, cc = shard_shape
    hr = r // 2
    if kind == "row":
        return ref.at[pl.ds(_mo(shard * r + half * hr, 8), hr), :]
    if kind == "col":
        return ref.at[pl.ds(_mo(half * hr, 8), hr), pl.ds(_mo(shard * cc, 128), cc)]
    if kind == "colw":
        return ref.at[pl.ds(_mo(half * hr, 8), hr), pl.ds(_mo(shard * (cc - 128), 128), cc)]
    return ref.at[shard, pl.ds(_mo(half * hr, 8), hr), :]


HBM = pl.BlockSpec(memory_space=pltpu.HBM)
SEM = pl.BlockSpec(memory_space=pltpu.SEMAPHORE)
DATAFLOW = pltpu.SideEffectType.DATAFLOW_SIDE_EFFECTING


def _in_hbm(a):
    return pltpu.with_memory_space_constraint(a, pltpu.HBM)


def _empty_hbm(shape, dtype):
    return _in_hbm(lax.empty(shape, dtype))


def _copies_start(name, bufs, n_copies, plan, carry):
    nb = len(bufs)

    def body(*refs):
        send_sems, recv_sems = refs[nb + 1], refs[nb + 2]
        for k, (src, dst, to) in enumerate(plan(refs[:nb])):
            pltpu.make_async_remote_copy(src_ref=src, dst_ref=dst, send_sem=send_sems.at[k], recv_sem=recv_sems.at[k],
                                         device_id=to, device_id_type=MESH).start()

    passed = list(bufs) + [carry]
    out = pl.pallas_call(
        body, name=name,
        in_specs=[HBM] * (nb + 1), out_specs=[SEM, SEM] + [HBM] * (nb + 1),
        out_shape=[pltpu.SemaphoreType.DMA((n_copies,)), pltpu.SemaphoreType.DMA((n_copies,))]
        + [pltpu.HBM(a.shape, a.dtype) for a in passed],
        input_output_aliases={i: 2 + i for i in range(nb + 1)},
        compiler_params=pltpu.CompilerParams(has_side_effects=DATAFLOW),
    )(*[_in_hbm(a) for a in passed])
    return out[0], out[1], list(out[2:2 + nb]), out[2 + nb]


def _copies_wait(name, send_sems, recv_sems, bufs, plan, after):
    nb = len(bufs)
    after = list(after) if isinstance(after, (list, tuple)) else [after]

    def body(*refs):
        send, recv = refs[nb], refs[nb + 1]
        for k, (src, dst, to) in enumerate(plan(refs[:nb])):
            cp = pltpu.make_async_remote_copy(src_ref=src, dst_ref=dst, send_sem=send.at[k], recv_sem=recv.at[k],
                                              device_id=to, device_id_type=MESH)
            cp.wait_send()
            cp.wait_recv()

    out = pl.pallas_call(
        body, name=name,
        in_specs=[HBM] * nb + [SEM, SEM] + [ANY] * len(after), out_specs=[HBM] * nb,
        out_shape=[pltpu.HBM(a.shape, a.dtype) for a in bufs],
        input_output_aliases={i: i for i in range(nb)},
        compiler_params=pltpu.CompilerParams(has_side_effects=DATAFLOW),
    )(*bufs, send_sems, recv_sems, *after)
    return list(out)


def _gather_ici_plan(shard_shapes, kinds):
    n_arr = len(kinds)

    def plan(refs):
        x, y, c, chips = _place()
        out = []
        for i in range(n_arr):
            w = _shard_window(refs[i], kinds[i], shard_shapes[i], 2 * x + y, c)
            out += [(w, w, (*chip, c)) for chip in chips]
        return out

    return plan


def _gather_d2d_plan(shard_shapes, kinds):
    n_arr = len(kinds)

    def plan(refs):
        x, y, c, chips = _place()
        out = []
        for i in range(n_arr):
            for chip in chips:
                w = _shard_window(refs[i], kinds[i], shard_shapes[i], 2 * chip[0] + chip[1], c)
                out.append((w, w, (x, y, 1 - c)))
        return out

    return plan


def _rs_pair_plan(kinds, shard_shapes):
    n_arr = len(kinds)

    def plan(refs):
        x, y, c, _ = _place()
        out = []
        for i in range(n_arr):
            for s in range(N_SHARD):
                out.append((_shard_window(refs[i], kinds[i], shard_shapes[i], s, 1 - c), refs[n_arr + i].at[s],
                            (x, y, 1 - c)))
        return out

    return plan


def _rs_chip_plan(n_arr):
    def plan(refs):
        x, y, c, chips = _place()
        out = []
        for i in range(n_arr):
            for j, chip in enumerate(chips):
                out.append((refs[i].at[2 * chip[0] + chip[1]], refs[n_arr + i].at[j], (*chip, c)))
        return out

    return plan


def _rs_pair_add(name, grad, got, kind, shard_shape, c):
    r, cc = shard_shape
    hr = r // 2
    tr = hr
    for cand in (256, 128, 64, 32, 16):
        if hr % cand == 0:
            tr = cand
            break
    nb = hr // tr

    def body(c_ref, g_ref, t_ref, p_ref, pb_ref):
        p = g_ref[...] + t_ref[...]
        p_ref[...] = p
        pb_ref[...] = p.astype(BF16)

    out_shape = [jax.ShapeDtypeStruct((N_SHARD, hr, cc), F32), jax.ShapeDtypeStruct((N_SHARD, hr, cc), BF16)]
    if kind == "colw":
        tiles = cc // 128
        tr = hr
        g_spec = pl.BlockSpec((tr, 128), lambda s, t, cr: (cr[0], s * (tiles - 1) + t))
        t_spec = pl.BlockSpec((None, tr, 128), lambda s, t, cr: (s, 0, t))
        return pl.pallas_call(
            body, name=name,
            grid_spec=pltpu.PrefetchScalarGridSpec(
                num_scalar_prefetch=1, grid=(N_SHARD, tiles), in_specs=[g_spec, t_spec], out_specs=[t_spec, t_spec]),
            out_shape=out_shape,
        )(c, grad, got)
    if kind == "row":
        g_spec = pl.BlockSpec((tr, cc), lambda s, i, cr: (s * 2 * nb + cr[0] * nb + i, 0))
    elif kind == "col":
        g_spec = pl.BlockSpec((tr, cc), lambda s, i, cr: (cr[0] * nb + i, s))
    else:
        g_spec = pl.BlockSpec((None, tr, cc), lambda s, i, cr: (s, cr[0] * nb + i, 0))
    t_spec = pl.BlockSpec((None, tr, cc), lambda s, i, cr: (s, i, 0))
    return pl.pallas_call(
        body, name=name,
        grid_spec=pltpu.PrefetchScalarGridSpec(
            num_scalar_prefetch=1, grid=(N_SHARD, nb),
            in_specs=[g_spec, t_spec], out_specs=[t_spec, t_spec]),
        out_shape=out_shape,
    )(c, grad, got)


def _rs_chip_add(name, pair_f32, got, shard_shape, mine_c):
    r, cc = shard_shape
    hr = r // 2
    tr = hr
    for cand in (256, 128, 64, 32, 16):
        if hr % cand == 0:
            tr = cand
            break
    nb = hr // tr

    def body(mc_ref, p_ref, t0_ref, t1_ref, t2_ref, o_ref):
        o_ref[...] = (p_ref[...] + t1_ref[...].astype(F32)) + (t0_ref[...].astype(F32) + t2_ref[...].astype(F32))

    def far(j):
        return pl.BlockSpec((None, tr, cc), lambda i, mc: (j, i, 0))

    return pl.pallas_call(
        body, name=name,
        grid_spec=pltpu.PrefetchScalarGridSpec(
            num_scalar_prefetch=1, grid=(nb,),
            in_specs=[pl.BlockSpec((None, tr, cc), lambda i, mc: (mc[0], i, 0)), far(0), far(1), far(2)],
            out_specs=pl.BlockSpec((tr, cc), lambda i, mc: (mc[1] * nb + i, 0))),
        out_shape=jax.ShapeDtypeStruct((r, cc), F32),
    )(mine_c, pair_f32, got, got, got)


def _rs_pair_share(name, halves, shard_shapes):
    n_arr = len(halves)

    def body(*refs):
        ins = refs[:n_arr]
        outs = refs[n_arr:2 * n_arr]
        send_sems, recv_sems = refs[2 * n_arr:]
        x, y, c, _ = _place()
        sibling = (x, y, 1 - c)
        cps = []
        for i in range(n_arr):
            hr = shard_shapes[i][0] // 2
            rows = pl.ds(_mo(c * hr, 8), hr)
            cp = pltpu.make_async_remote_copy(
                src_ref=outs[i].at[rows, :], dst_ref=outs[i].at[rows, :],
                send_sem=send_sems.at[i], recv_sem=recv_sems.at[i],
                device_id=sibling, device_id_type=MESH)
            cp.start()
            cps.append(cp)
        for cp in cps:
            cp.wait()

    return pl.pallas_call(
        body, name=name,
        in_specs=[ANY] * n_arr, out_specs=[ANY] * n_arr,
        out_shape=[jax.ShapeDtypeStruct(s, F32) for s in shard_shapes],
        input_output_aliases={i: i for i in range(n_arr)},
        scratch_shapes=[pltpu.SemaphoreType.DMA((n_arr,)), pltpu.SemaphoreType.DMA((n_arr,))],
        compiler_params=pltpu.CompilerParams(has_side_effects=True),
    )(*halves)


def _pack(arrays):
    flat = []
    for a in arrays:
        v = a.reshape(-1).astype(F32)
        flat.append(jnp.pad(v, (0, (-v.shape[0]) % SMALL_COLS)))
    buf = jnp.concatenate(flat).reshape(-1, SMALL_COLS)
    return jnp.pad(buf, ((0, (-buf.shape[0]) % 16), (0, 0)))


def _unpack(buf, shapes):
    out = []
    row = 0
    for s in shapes:
        size = math.prod(s)
        nrow = -(-size // SMALL_COLS)
        out.append(buf[row:row + nrow].reshape(-1)[:size].reshape(s))
        row += nrow
    return out


def kernel(x, meta, norm_ab_w, w_in_ab, ret_norm_w, s5_lam_re, s5_lam_im, s5_log_dt, s5_b_re, s5_b_im, s5_c_re, s5_c_im, s5_d, s5_w_glu, w_out_ab, norm_c_w, w_in_c, gla_w_gate, gla_b_gate, gla_norm_w, w_out_c, final_norm_w, loss_target, m_meta, m_norm_ab_w, m_w_in_ab, m_ret_norm_w, m_s5_lam_re, m_s5_lam_im, m_s5_log_dt, m_s5_b_re, m_s5_b_im, m_s5_c_re, m_s5_c_im, m_s5_d, m_s5_w_glu, m_w_out_ab, m_norm_c_w, m_w_in_c, m_gla_w_gate, m_gla_b_gate, m_gla_norm_w, m_w_out_c, m_final_norm_w, v_meta, v_norm_ab_w, v_w_in_ab, v_ret_norm_w, v_s5_lam_re, v_s5_lam_im, v_s5_log_dt, v_s5_b_re, v_s5_b_im, v_s5_c_re, v_s5_c_im, v_s5_d, v_s5_w_glu, v_w_out_ab, v_norm_c_w, v_w_in_c, v_gla_w_gate, v_gla_b_gate, v_gla_norm_w, v_w_out_c, v_final_norm_w):
    seq = x.shape[1]
    rows = seq + CHUNK
    xi, yi, ci = lax.axis_index("x"), lax.axis_index("y"), lax.axis_index("c")
    mine = 2 * xi + yi
    c_arr = jnp.reshape(ci, (1,)).astype(jnp.int32)
    mine_c = jnp.stack([mine, ci]).astype(jnp.int32)

    mine_arr = jnp.reshape(mine, (1,)).astype(jnp.int32)
    small_shard = _pack([meta, norm_c_w, gla_norm_w, gla_b_gate, gla_w_gate[0]])
    small_all = _gather_small(small_shard)
    first_kinds = ["col"]
    first_shapes = [w_in_ab.shape[1:]]
    first_ici = _gather_ici_plan(first_shapes, first_kinds)
    first_d2d = _gather_d2d_plan(first_shapes, first_kinds)
    wab_buf, wab_own = _cast_place("place_w_in_ab", w_in_ab[0], "col", mine_arr, BF16, also_own=True)
    f_send, f_recv, f_bufs, small_all = _copies_start("gather_first_ici_start", [wab_buf], 3, first_ici, small_all)
    def late_group(items, kinds):
        shapes = [a.shape for _, a in items]
        bufs = [_cast_place("place_" + nm, a, kd, mine_arr, BF16) for (nm, a), kd in zip(items, kinds)]
        return bufs, _gather_ici_plan(shapes, kinds), _gather_d2d_plan(shapes, kinds), 3 * len(items)

    a_bufs, a_ici, a_d2d, n_a = late_group([("w_out_ab", w_out_ab[0]), ("w_glu", s5_w_glu[0])], ["row", "row"])
    b_bufs, b_ici, b_d2d, n_b = late_group([("w_in_c", w_in_c[0]), ("w_out_c", w_out_c[0])], ["stack", "row"])
    g_bufs = a_bufs + b_bufs
    cosf, sinf = _rope_tables(rows)
    rtab = _ret_tables()
    ab_re, ab_im, bb_re, bb_im = _s5_discretize(s5_lam_re[0], s5_lam_im[0], s5_log_dt[0], s5_b_re[0], s5_b_im[0])
    ab = (ab_re, ab_im)
    bd_b = (_bdiag_in(bb_re), _bdiag_in(bb_im))
    bd_c = (_bdiag_out(s5_c_re[0]), _bdiag_out(s5_c_im[0]))
    q4 = D_MODEL // N_SHARD
    g4 = GLA_QK // N_SHARD
    parts = [_unpack(small_all[j], [(N_META, q4), (1, q4), (1, q4), (1, g4), (GLA_RANK, g4)]) for j in range(N_SHARD)]
    meta_f, norm_c_f, gla_norm_f, bgate_f, wgate_f = [jnp.concatenate([p[i] for p in parts], axis=1) for i in range(5)]
    wgate_pad = jnp.pad(wgate_f, ((0, 128 - GLA_RANK), (0, 0)))

    h0, hn0 = _embed_norm(x[0], meta_f, norm_ab_w)

    tm = _row_tile(rows, 1408)
    tmk = _row_tile(rows, 1408)
    own_blocks = (IN_AB // N_SHARD) // 512
    shift_own = jnp.stack([jnp.zeros((), jnp.int32), mine.astype(jnp.int32) * own_blocks])
    shift_rest = jnp.stack([(mine.astype(jnp.int32) + 1) * own_blocks, (mine.astype(jnp.int32) + 1) * own_blocks])
    proj0 = _in_proj_shifted("in_proj_ab_own", hn0, wab_own, IN_AB // N_SHARD, shift_own, tm, 512, IN_AB)
    f_bufs = _copies_wait("gather_first_ici_wait", f_send, f_recv, f_bufs, first_ici,
                          [proj0, cosf, sinf, bd_b[0], bd_b[1], bd_c[0], bd_c[1]] + g_bufs + list(rtab))
    f_send, f_recv, f_bufs, cosf = _copies_start("gather_first_d2d_start", f_bufs, 3, first_d2d, cosf)
    wab, = _copies_wait("gather_first_d2d_wait", f_send, f_recv, f_bufs, first_d2d, cosf)
    a_send, a_recv, a_bufs, wab = _copies_start("gather_a_ici_start", a_bufs, n_a, a_ici, wab)
    b_send, b_recv, b_bufs, wab = _copies_start("gather_b_ici_start", b_bufs, n_b, b_ici, wab)
    proj0 = _in_proj_shifted("in_proj_ab_rest", hn0, wab, IN_AB - IN_AB // N_SHARD, shift_rest, tm, 512, IN_AB,
                             into=proj0)
    o_ret, o_a, ret_states = _ret_fwd(proj0, cosf, sinf, rtab, ret_norm_w)
    a_bufs = _copies_wait("gather_a_ici_wait", a_send, a_recv, a_bufs, a_ici, o_a)
    a_send, a_recv, a_bufs, proj0 = _copies_start("gather_a_d2d_start", a_bufs, n_a, a_d2d, proj0)
    y_s5, g_s5, s5_er, s5_ei = _s5_fwd(proj0, ab, bd_b, bd_c, s5_d)
    wout_ab, wglu = _copies_wait("gather_a_d2d_wait", a_send, a_recv, a_bufs, a_d2d, g_s5)
    zb_blk = (2 * RET_QK + 2 * RET_W + S5_W) // 512

    def glu_out(acc, gv, z):
        return gv.astype(F32) * _sigmoid(acc) * (z * _sigmoid(z))

    t_glu = _matmul("glu", g_s5, wglu, NN, rows, S5_W, S5_W, tm=tm, tn=512, tk=S5_W)
    o_b = _matmul("glu_out", g_s5, wglu, NN, rows, S5_W, S5_W, tm=tm, tn=512, tk=S5_W, out_dtype=BF16,
                  extras=[(g_s5, (tm, 512), lambda i, j, kk: (i, j)),
                          (proj0, (tm, 512), lambda i, j, kk: (i, zb_blk + j))],
                  epilogue=glu_out)
    b_bufs = _copies_wait("gather_b_ici_wait", b_send, b_recv, b_bufs, b_ici, o_b)
    b_send, b_recv, b_bufs, o_b = _copies_start("gather_b_d2d_start", b_bufs, n_b, b_d2d, o_b)
    h1 = _matmul("out_proj_ab", None, None, NN, rows, D_MODEL, OUT_AB, tm=tm, tn=512, tk=1024,
                 segs=[(o_a, (0, 0), wout_ab, (0, 0), RET_W, 1024),
                       (o_b, (0, 0), wout_ab, (RET_W // 1024, 0), S5_W, 1024)],
                 extras=[(h0, (tm, 512), lambda i, j, kk: (i, j))], epilogue=lambda acc, r: acc + r)
    wc_st, wout_c = _copies_wait("gather_b_d2d_wait", b_send, b_recv, b_bufs, b_d2d, h1)
    wc = jnp.concatenate([wc_st[j] for j in range(N_SHARD)] + [jnp.zeros((D_MODEL, IN_C_PAD - IN_C), BF16)], axis=1)

    hn1 = _rms_fwd("norm_c", h1, norm_c_f)
    proj1 = _matmul("in_proj_c", hn1, wc, NN, rows, IN_C_PAD, D_MODEL, tm=tm, tn=896, tk=D_MODEL)
    o_gla, o_c, gla_states = _gla_fwd(proj1, wgate_pad, bgate_f, gla_norm_f)
    h2 = _matmul("out_proj_c", o_c, wout_c, NN, rows, D_MODEL, GLA_W, tm=tm, tn=512, tk=GLA_W,
                 extras=[(h1, (tm, 512), lambda i, j, kk: (i, j))], epilogue=lambda acc, r: acc + r)
    loss_dev, dh2, d_final = _final_loss(h2, final_norm_w.reshape(1, D_MODEL), loss_target[0])

    g_wout_c = _matmul("d_w_out_c", o_c, dh2, TN, GLA_W, D_MODEL, rows, tm=1024, tn=1024, tk=tmk)
    d_oc = _matmul("d_o_c", dh2, wout_c, NT, rows, GLA_W, D_MODEL, tm=tm, tn=512, tk=1024)
    dq1, dk1, dv1, dz1, dlogit, d_gla_norm, d_bgate = _gla_bwd(proj1, wgate_pad, bgate_f, gla_norm_f, o_gla, d_oc, gla_states)
    gl_blk = (2 * GLA_QK + 2 * GLA_W) // 128
    dgl = _matmul("d_g_low", dlogit, wgate_pad, NT, rows, 128, GLA_QK, tm=tm, tn=128, tk=GLA_QK, out_dtype=BF16)
    g_wgate = _matmul("d_w_gate", proj1, dlogit, TN, 128, GLA_QK, rows, tm=128, tn=GLA_QK, tk=tmk, a_off=(0, gl_blk))
    dproj1 = jnp.concatenate([dq1, dk1, dv1, dz1, dgl], axis=1)
    g_wc = _matmul("d_w_in_c", hn1, dproj1, TN, D_MODEL, IN_C_PAD, rows, tm=1024, tn=896, tk=tmk)
    dhn1 = _matmul("d_hn1", dproj1, wc, NT, rows, D_MODEL, IN_C_PAD, tm=tm, tn=512, tk=896)
    dh1, d_norm_c = _rms_bwd("norm_c_bwd", dhn1, h1, norm_c_f, dh2)

    g_wout_ab = _matmul("d_w_out_ab_a", o_a, dh1, TN, RET_W, D_MODEL, rows, tm=1024, tn=1024, tk=tmk,
                        out_shape=jax.ShapeDtypeStruct((OUT_AB, D_MODEL), F32))
    g_wout_ab = _matmul("d_w_out_ab_b", o_b, dh1, TN, S5_W, D_MODEL, rows, tm=1024, tn=1024, tk=tmk,
                        into=(g_wout_ab, RET_W // 1024, 0))
    dmix = _matmul("d_mix", dh1, wout_ab, NT, rows, OUT_AB, D_MODEL, tm=tm, tn=512, tk=1024)
    dproj0, d_ret_norm = _ret_bwd(proj0, cosf, sinf, rtab, ret_norm_w, o_ret, dmix, ret_states)
    dproj0, dt_glu, dg_direct = _s5_gate_bwd(dmix, g_s5, t_glu, proj0, dproj0)
    g_wglu = _matmul("d_w_glu", g_s5, dt_glu, TN, S5_W, S5_W, rows, tm=1024, tn=1024, tk=tmk)
    dy_s5 = _matmul("d_y_s5", dt_glu, wglu, NT, rows, S5_W, S5_W, tm=tm, tn=512, tk=S5_W,
                    extras=[(dg_direct, (tm, 512), lambda i, j, kk: (i, j)),
                            (y_s5, (tm, 512), lambda i, j, kk: (i, j))],
                    epilogue=lambda acc, dg, yv: (acc + dg) * _gelu_grad(yv))
    wc_cols = IN_C // N_SHARD
    wc_win = (wc_cols // 128 + 1) * 128
    rs1_names = ["w_out_ab", "w_in_c", "w_out_c", "w_glu"]
    rs1_kinds = ["row", "colw", "row", "row"]
    rs1_shapes = [w_out_ab.shape[1:], (D_MODEL, wc_win), w_out_c.shape[1:], s5_w_glu.shape[1:]]
    rs1_plan = _rs_pair_plan(rs1_kinds, rs1_shapes)
    rs1_land = [_empty_hbm((N_SHARD, r // 2, cc), F32) for (r, cc) in rs1_shapes]
    p_send, p_recv, p_bufs, dy_s5 = _copies_start("rs1_pair_start", [g_wout_ab, g_wc, g_wout_c, g_wglu] + rs1_land,
                                                  N_SHARD * 4, rs1_plan, dy_s5)
    dproj0, dbr_d, dbi_d, dcr_d, dci_d, dar_p, dai_p, dd_p = _s5_bwd(proj0, dy_s5, ab, bd_b, bd_c, s5_d,
                                                                     (s5_er, s5_ei), dproj0)
    p_bufs = _copies_wait("rs1_pair_wait", p_send, p_recv, p_bufs, rs1_plan, dproj0)
    rs1_pairs = [_rs_pair_add("rs_pair_add_" + nm, g, t, kd, ss, c_arr)
                 for nm, g, t, kd, ss in zip(rs1_names, p_bufs[:4], p_bufs[4:], rs1_kinds, rs1_shapes)]
    rs1_chip_plan = _rs_chip_plan(4)
    rs1_land2 = [_empty_hbm((3, r // 2, cc), BF16) for (r, cc) in rs1_shapes]
    c_send, c_recv, c_bufs, dproj0 = _copies_start("rs1_chip_start", [p[1] for p in rs1_pairs] + rs1_land2, 12,
                                                   rs1_chip_plan, dproj0)
    g_wab = _matmul("d_w_in_ab", hn0, dproj0, TN, D_MODEL, IN_AB, rows, tm=1024, tn=1024, tk=tmk)
    rs2_shapes = [w_in_ab.shape[1:]]
    rs2_plan = _rs_pair_plan(["col"], rs2_shapes)
    rs2_land = [_empty_hbm((N_SHARD, rs2_shapes[0][0] // 2, rs2_shapes[0][1]), F32)]
    q_send, q_recv, q_bufs, dproj0 = _copies_start("rs2_pair_start", [g_wab] + rs2_land, N_SHARD, rs2_plan, dproj0)
    dhn0 = _matmul("d_hn0_a", dproj0, wab, NT, tm, D_MODEL, IN_AB, tm=tm, tn=512, tk=2048,
                   out_shape=jax.ShapeDtypeStruct((rows, D_MODEL), F32))
    q_bufs = _copies_wait("rs2_pair_wait", q_send, q_recv, q_bufs, rs2_plan, dhn0)
    rs2_pair = _rs_pair_add("rs_pair_add_w_in_ab", q_bufs[0], q_bufs[1], "col", rs2_shapes[0], c_arr)
    rs2_chip_plan = _rs_chip_plan(1)
    rs2_land2 = [_empty_hbm((3, rs2_shapes[0][0] // 2, rs2_shapes[0][1]), BF16)]
    r_send, r_recv, r_bufs, dhn0 = _copies_start("rs2_chip_start", [rs2_pair[1]] + rs2_land2, 3, rs2_chip_plan, dhn0)
    if rows > tm:
        dhn0 = _matmul("d_hn0_b", dproj0, wab, NT, rows - tm, D_MODEL, IN_AB, tm=tm, tn=512, tk=2048, a_off=(1, 0),
                       into=(dhn0, 1, 0))
    grad_x, d_meta, d_norm_ab = _rms_bwd_embed(dhn0, h0, norm_ab_w, dh1)
    c_bufs = _copies_wait("rs1_chip_wait", c_send, c_recv, c_bufs, rs1_chip_plan, grad_x)
    grad_x = grad_x[None]
    rs1_halves = [_rs_chip_add("rs_chip_add_" + nm, p[0], t, ss, mine_c)
                  for nm, p, t, ss in zip(rs1_names, rs1_pairs, c_bufs[4:], rs1_shapes)]
    g_w_out_ab, g_w_in_c, g_w_out_c, g_w_glu = _rs_pair_share("rs1_pair_share", rs1_halves, rs1_shapes)
    g_w_in_c = lax.dynamic_slice(g_w_in_c, (0, (wc_cols % 128) * mine), (D_MODEL, wc_cols))

    d_ab_re = jnp.sum(dar_p, axis=1).reshape(S5_G, S5_P)
    d_ab_im = jnp.sum(dai_p, axis=1).reshape(S5_G, S5_P)
    small_local = [loss_dev, d_meta, d_norm_ab, d_ret_norm.reshape(1, RET_W), d_ab_re, d_ab_im,
                   _bdiag_in_extract(dbr_d), _bdiag_in_extract(dbi_d),
                   _bdiag_out_extract(dcr_d), _bdiag_out_extract(dci_d),
                   jnp.sum(dd_p, axis=1).reshape(1, S5_W), d_norm_c, g_wgate[:GLA_RANK],
                   d_bgate.reshape(1, GLA_QK), d_gla_norm.reshape(1, GLA_W), d_final]
    small_shapes = [a.shape for a in small_local]
    summed = _unpack(_allreduce_small(_pack(small_local)), small_shapes)
    (loss, g_meta_f, g_norm_ab, g_ret_norm, g_ab_re, g_ab_im, g_bb_re, g_bb_im, g_c_re, g_c_im, g_d,
     g_norm_c_f, g_wgate_f, g_bgate_f, g_gla_norm_f, g_final) = summed
    _, s5_vjp = jax.vjp(_s5_discretize, s5_lam_re[0], s5_lam_im[0], s5_log_dt[0], s5_b_re[0], s5_b_im[0])
    g_lam_re, g_lam_im, g_log_dt, g_b_re, g_b_im = s5_vjp((g_ab_re, g_ab_im, g_bb_re, g_bb_im))

    def take(a, width):
        return lax.dynamic_slice_in_dim(a, mine * width, width, axis=1)

    grads = {
        "meta": take(g_meta_f, q4), "norm_ab_w": g_norm_ab, "ret_norm_w": g_ret_norm,
        "s5_lam_re": g_lam_re[None], "s5_lam_im": g_lam_im[None], "s5_log_dt": g_log_dt[None],
        "s5_b_re": g_b_re[None], "s5_b_im": g_b_im[None], "s5_c_re": g_c_re[None], "s5_c_im": g_c_im[None],
        "s5_d": g_d, "s5_w_glu": g_w_glu[None], "w_out_ab": g_w_out_ab[None], "norm_c_w": take(g_norm_c_f, q4),
        "w_in_c": g_w_in_c[None], "gla_w_gate": take(g_wgate_f, g4)[None], "gla_b_gate": take(g_bgate_f, g4),
        "gla_norm_w": take(g_gla_norm_f, q4), "w_out_c": g_w_out_c[None], "final_norm_w": g_final.reshape(D_MODEL),
    }
    weights = dict(meta=meta, norm_ab_w=norm_ab_w, w_in_ab=w_in_ab, ret_norm_w=ret_norm_w, s5_lam_re=s5_lam_re,
                   s5_lam_im=s5_lam_im, s5_log_dt=s5_log_dt, s5_b_re=s5_b_re, s5_b_im=s5_b_im, s5_c_re=s5_c_re,
                   s5_c_im=s5_c_im, s5_d=s5_d, s5_w_glu=s5_w_glu, w_out_ab=w_out_ab, norm_c_w=norm_c_w,
                   w_in_c=w_in_c, gla_w_gate=gla_w_gate, gla_b_gate=gla_b_gate, gla_norm_w=gla_norm_w,
                   w_out_c=w_out_c, final_norm_w=final_norm_w)
    m_in = dict(meta=m_meta, norm_ab_w=m_norm_ab_w, w_in_ab=m_w_in_ab, ret_norm_w=m_ret_norm_w,
                s5_lam_re=m_s5_lam_re, s5_lam_im=m_s5_lam_im, s5_log_dt=m_s5_log_dt, s5_b_re=m_s5_b_re,
                s5_b_im=m_s5_b_im, s5_c_re=m_s5_c_re, s5_c_im=m_s5_c_im, s5_d=m_s5_d, s5_w_glu=m_s5_w_glu,
                w_out_ab=m_w_out_ab, norm_c_w=m_norm_c_w, w_in_c=m_w_in_c, gla_w_gate=m_gla_w_gate,
                gla_b_gate=m_gla_b_gate, gla_norm_w=m_gla_norm_w, w_out_c=m_w_out_c, final_norm_w=m_final_norm_w)
    v_in = dict(meta=v_meta, norm_ab_w=v_norm_ab_w, w_in_ab=v_w_in_ab, ret_norm_w=v_ret_norm_w,
                s5_lam_re=v_s5_lam_re, s5_lam_im=v_s5_lam_im, s5_log_dt=v_s5_log_dt, s5_b_re=v_s5_b_re,
                s5_b_im=v_s5_b_im, s5_c_re=v_s5_c_re, s5_c_im=v_s5_c_im, s5_d=v_s5_d, s5_w_glu=v_s5_w_glu,
                w_out_ab=v_w_out_ab, norm_c_w=v_norm_c_w, w_in_c=v_w_in_c, gla_w_gate=v_gla_w_gate,
                gla_b_gate=v_gla_b_gate, gla_norm_w=v_gla_norm_w, w_out_c=v_w_out_c, final_norm_w=v_final_norm_w)
    order = list(weights)
    big_names = ["s5_w_glu", "w_out_ab", "w_in_c", "w_out_c", "w_in_ab"]
    small_names = [nm for nm in order if nm not in big_names]
    delta, new_m, new_v = {}, {}, {}

    def big_update(nm):
        shp = weights[nm].shape
        d2, m2, v2 = _adamw("adamw_" + nm, weights[nm][0], grads[nm][0], m_in[nm][0], v_in[nm][0])
        delta[nm], new_m[nm], new_v[nm] = d2.reshape(shp), m2.reshape(shp), v2.reshape(shp)

    for nm in big_names[:-1]:
        big_update(nm)
    sshapes = [weights[nm].shape for nm in small_names]
    d2, m2, v2 = _adamw("adamw_small", _pack([weights[nm] for nm in small_names]),
                        _pack([grads[nm] for nm in small_names]), _pack([m_in[nm] for nm in small_names]),
                        _pack([v_in[nm] for nm in small_names]))
    for nm, dd, mm, vv in zip(small_names, _unpack(d2, sshapes), _unpack(m2, sshapes), _unpack(v2, sshapes)):
        delta[nm], new_m[nm], new_v[nm] = dd, mm, vv
    r_bufs = _copies_wait("rs2_chip_wait", r_send, r_recv, r_bufs, rs2_chip_plan,
                          [v2] + [new_v[nm] for nm in big_names[:-1]])
    rs2_half = _rs_chip_add("rs_chip_add_w_in_ab", rs2_pair[0], r_bufs[1], rs2_shapes[0], mine_c)
    grads["w_in_ab"] = _rs_pair_share("rs2_pair_share", [rs2_half], rs2_shapes)[0][None]
    big_update("w_in_ab")
    grads = {nm: grads[nm].reshape(weights[nm].shape) for nm in order}
    return (loss.reshape(()), grad_x, *[grads[nm] for nm in order], *[delta[nm] for nm in order],
            *[new_m[nm] for nm in order], *[new_v[nm] for nm in order])
```

```python
import math

import jax
import jax.numpy as jnp
from jax import lax
from jax.experimental import pallas as pl
from jax.experimental.pallas import tpu as pltpu

F32 = jnp.float32
BF16 = jnp.bfloat16
MESH = pl.DeviceIdType.MESH

D_MODEL = 2048
N_META = 16
CHUNK = 128
SUB = 16
NSUB = CHUNK // SUB
PAD = CHUNK - N_META
EPS = 1e-6

RET_HEADS = 8
RET_DK = 128
RET_DV = 256
RET_QK = RET_HEADS * RET_DK
RET_W = RET_HEADS * RET_DV
ROPE_BASE = 10000.0

S5_W = 1024
S5_GH = 16
S5_G = S5_W // S5_GH
S5_P = 64
S5_TG = 8
S5_NT = S5_G // S5_TG
S5_TU = S5_TG * S5_GH
S5_TS = S5_TG * S5_P
S5_FWD_TILES = 2
S5_BWD_TILES = 1

GLA_HEADS = 4
GLA_DK = 256
GLA_DV = 512
GLA_QK = GLA_HEADS * GLA_DK
GLA_W = GLA_HEADS * GLA_DV
GLA_RANK = 16
GLA_TAU = 16.0

IN_AB = 2 * RET_QK + 2 * RET_W + 2 * S5_W
OUT_AB = RET_W + S5_W
IN_C = 2 * GLA_QK + 2 * GLA_W + GLA_RANK
IN_C_PAD = 2 * GLA_QK + 2 * GLA_W + 128

ADAM_LR = 0.001
ADAM_B1 = 0.9
ADAM_B2 = 0.999
ADAM_EPS = 1e-08
ADAM_WD = 0.01
ADAM_STEP = 10

N_SHARD = 4
SMALL_COLS = 512

NN = (((1,), (0,)), ((), ()))
NT = (((1,), (1,)), ((), ()))
TN = (((0,), (0,)), ((), ()))


def _dot(a, b, dims=NN):
    return lax.dot_general(a.astype(BF16), b.astype(BF16), dims, preferred_element_type=F32)


def _mo(v, m):
    return v if isinstance(v, int) else pl.multiple_of(v, m)


def _sigmoid(x):
    return 1.0 / (1.0 + jnp.exp(-x))


def _row_tile(rows, cap):
    n = rows // CHUNK
    best = 1
    for d in range(1, n + 1):
        if n % d == 0 and d * CHUNK <= cap:
            best = d
    return best * CHUNK


def _matmul(name, a, b, dims, m, n, k, *, tm, tn, tk, out_dtype=F32, a_off=(0, 0), b_off=(0, 0),
            extras=(), epilogue=None, out_shape=None, out_spec=None, segs=None, into=None):
    if segs is None:
        segs = [(a, a_off, b, b_off, k, tk)]
    assert m % tm == 0 and n % tn == 0, (name, m, n, tm, tn)
    starts, counts = [], []
    nk = 0
    for (_, _, _, _, ks, tks) in segs:
        assert ks % tks == 0, (name, ks, tks)
        starts.append(nk)
        counts.append(ks // tks)
        nk += ks // tks
    in_specs, operands = [], []
    for s, (sa, (ar, ac), sb, (br, bc), _, tks) in enumerate(segs):
        def kpos(kk, st=starts[s], cnt=counts[s]):
            return jnp.clip(kk - st, 0, cnt - 1) if len(segs) > 1 else kk

        if dims == NN:
            a_spec = pl.BlockSpec((tm, tks), lambda i, j, kk, p=kpos, r=ar, c=ac: (i + r, p(kk) + c))
            b_spec = pl.BlockSpec((tks, tn), lambda i, j, kk, p=kpos, r=br, c=bc: (p(kk) + r, j + c))
        elif dims == NT:
            a_spec = pl.BlockSpec((tm, tks), lambda i, j, kk, p=kpos, r=ar, c=ac: (i + r, p(kk) + c))
            b_spec = pl.BlockSpec((tn, tks), lambda i, j, kk, p=kpos, r=br, c=bc: (j + r, p(kk) + c))
        else:
            a_spec = pl.BlockSpec((tks, tm), lambda i, j, kk, p=kpos, r=ar, c=ac: (p(kk) + r, i + c))
            b_spec = pl.BlockSpec((tks, tn), lambda i, j, kk, p=kpos, r=br, c=bc: (p(kk) + r, j + c))
        in_specs += [a_spec, b_spec]
        operands += [sa, sb]
    n_seg = len(segs)
    n_extra = len(extras)
    if out_shape is None:
        out_shape = jax.ShapeDtypeStruct((m, n), out_dtype)

    def body(*refs):
        e_refs = refs[2 * n_seg:2 * n_seg + n_extra]
        n_in = 2 * n_seg + n_extra + (1 if into is not None else 0)
        o_ref = refs[n_in]
        if nk == 1:
            part = _dot(refs[0][...], refs[1][...], dims)
            if epilogue is not None:
                part = epilogue(part, *[e[...] for e in e_refs])
            o_ref[...] = part.astype(o_ref.dtype)
            return
        acc_ref = refs[n_in + 1]
        kk = pl.program_id(2)

        @pl.when(kk == 0)
        def _():
            acc_ref[...] = jnp.zeros_like(acc_ref)

        if n_seg == 1:
            acc_ref[...] += _dot(refs[0][...], refs[1][...], dims)
        else:
            for s in range(n_seg):
                @pl.when(jnp.logical_and(kk >= starts[s], kk < starts[s] + counts[s]))
                def _(s=s):
                    acc_ref[...] += _dot(refs[2 * s][...], refs[2 * s + 1][...], dims)

        @pl.when(kk == nk - 1)
        def _():
            acc = acc_ref[...]
            if epilogue is not None:
                acc = epilogue(acc, *[e[...] for e in e_refs])
            o_ref[...] = acc.astype(o_ref.dtype)

    if out_spec is None:
        out_spec = pl.BlockSpec((tm, tn), lambda i, j, kk: (i, j))
    in_specs += [pl.BlockSpec(bs, im) for (_, bs, im) in extras]
    operands += [e for (e, _, _) in extras]
    aliases = {}
    if into is not None:
        dest, ro, co = into
        out_shape = jax.ShapeDtypeStruct(dest.shape, dest.dtype)
        out_spec = pl.BlockSpec((tm, tn), lambda i, j, kk: (i + ro, j + co))
        aliases = {len(operands): 0}
        in_specs.append(ANY)
        operands.append(dest)
    return pl.pallas_call(
        body, name=name, grid=(m // tm, n // tn, nk),
        in_specs=in_specs, out_specs=out_spec, out_shape=out_shape, input_output_aliases=aliases,
        scratch_shapes=[] if nk == 1 else [pltpu.VMEM((tm, tn), F32)],
        compiler_params=pltpu.CompilerParams(dimension_semantics=("parallel", "parallel", "arbitrary")),
    )(*operands)


def _rms_fwd(name, h, w):
    rows, d = h.shape
    tm = _row_tile(rows, 512)

    def body(h_ref, w_ref, o_ref):
        x = h_ref[...]
        r = lax.rsqrt(jnp.mean(x * x, axis=-1, keepdims=True) + EPS)
        o_ref[...] = (x * r * w_ref[...]).astype(BF16)

    return pl.pallas_call(
        body, name=name, grid=(rows // tm,),
        in_specs=[pl.BlockSpec((tm, d), lambda i: (i, 0)), pl.BlockSpec((1, d), lambda i: (0, 0))],
        out_specs=pl.BlockSpec((tm, d), lambda i: (i, 0)),
        out_shape=jax.ShapeDtypeStruct((rows, d), BF16),
    )(h, w)


def _rms_bwd(name, dhn, h, w, dres):
    rows, d = h.shape
    tm = _row_tile(rows, 384)

    def body(g_ref, h_ref, w_ref, r_ref, dh_ref, dw_ref):
        i = pl.program_id(0)
        x = h_ref[...]
        r = lax.rsqrt(jnp.mean(x * x, axis=-1, keepdims=True) + EPS)
        xh = x * r
        g = g_ref[...]
        gw = g * w_ref[...]
        dh_ref[...] = r_ref[...] + r * (gw - xh * jnp.mean(gw * xh, axis=-1, keepdims=True))

        @pl.when(i == 0)
        def _():
            dw_ref[...] = jnp.zeros_like(dw_ref)

        dw_ref[...] += jnp.sum(g * xh, axis=0, keepdims=True)

    return pl.pallas_call(
        body, name=name, grid=(rows // tm,),
        in_specs=[pl.BlockSpec((tm, d), lambda i: (i, 0)), pl.BlockSpec((tm, d), lambda i: (i, 0)),
                  pl.BlockSpec((1, d), lambda i: (0, 0)), pl.BlockSpec((tm, d), lambda i: (i, 0))],
        out_specs=[pl.BlockSpec((tm, d), lambda i: (i, 0)), pl.BlockSpec((1, d), lambda i: (0, 0))],
        out_shape=[jax.ShapeDtypeStruct((rows, d), F32), jax.ShapeDtypeStruct((1, d), F32)],
    )(dhn, h, w, dres)


def _embed_norm(x, meta, w):
    seq, d = x.shape
    rows = seq + CHUNK

    def body(x_ref, m_ref, w_ref, h_ref, o_ref):
        i = pl.program_id(0)

        def emit(h):
            h_ref[...] = h
            r = lax.rsqrt(jnp.mean(h * h, axis=-1, keepdims=True) + EPS)
            o_ref[...] = (h * r * w_ref[...]).astype(BF16)

        @pl.when(i == 0)
        def _():
            emit(jnp.concatenate([jnp.zeros((PAD, d), F32), m_ref[...]], axis=0))

        @pl.when(i > 0)
        def _():
            emit(x_ref[...])

    blk = pl.BlockSpec((CHUNK, d), lambda i: (i, 0))
    return pl.pallas_call(
        body, name="embed_norm_ab", grid=(rows // CHUNK,),
        in_specs=[pl.BlockSpec((CHUNK, d), lambda i: (jnp.maximum(i - 1, 0), 0)),
                  pl.BlockSpec((N_META, d), lambda i: (0, 0)), pl.BlockSpec((1, d), lambda i: (0, 0))],
        out_specs=[blk, blk],
        out_shape=[jax.ShapeDtypeStruct((rows, d), F32), jax.ShapeDtypeStruct((rows, d), BF16)],
    )(x, meta, w)


def _rms_bwd_embed(dhn, h, w, dres):
    rows, d = h.shape
    seq = rows - CHUNK

    def body(g_ref, h_ref, w_ref, r_ref, gx_ref, gm_ref, dw_ref):
        i = pl.program_id(0)
        x = h_ref[...]
        r = lax.rsqrt(jnp.mean(x * x, axis=-1, keepdims=True) + EPS)
        xh = x * r
        g = g_ref[...]
        gw = g * w_ref[...]
        dh = r_ref[...] + r * (gw - xh * jnp.mean(gw * xh, axis=-1, keepdims=True))

        @pl.when(i == 0)
        def _():
            dw_ref[...] = jnp.zeros_like(dw_ref)
            gm_ref[...] = dh[PAD:]

        @pl.when(i > 0)
        def _():
            gx_ref[...] = dh

        dw_ref[...] += jnp.sum(g * xh, axis=0, keepdims=True)

    blk = pl.BlockSpec((CHUNK, d), lambda i: (i, 0))
    return pl.pallas_call(
        body, name="norm_ab_bwd", grid=(rows // CHUNK,),
        in_specs=[blk, blk, pl.BlockSpec((1, d), lambda i: (0, 0)), blk],
        out_specs=[pl.BlockSpec((CHUNK, d), lambda i: (jnp.maximum(i - 1, 0), 0)),
                   pl.BlockSpec((N_META, d), lambda i: (0, 0)), pl.BlockSpec((1, d), lambda i: (0, 0))],
        out_shape=[jax.ShapeDtypeStruct((seq, d), F32), jax.ShapeDtypeStruct((N_META, d), F32),
                   jax.ShapeDtypeStruct((1, d), F32)],
    )(dhn, h, w, dres)


def _final_loss(h2, w, target):
    rows, d = h2.shape

    def body(h_ref, w_ref, t_ref, loss_ref, dh_ref, dw_ref):
        i = pl.program_id(0)

        @pl.when(i == 0)
        def _():
            loss_ref[...] = jnp.zeros_like(loss_ref)
            dw_ref[...] = jnp.zeros_like(dw_ref)
            dh_ref[...] = jnp.zeros_like(dh_ref)

        @pl.when(i > 0)
        def _():
            x = h_ref[...]
            r = lax.rsqrt(jnp.mean(x * x, axis=-1, keepdims=True) + EPS)
            xh = x * r
            wv = w_ref[...]
            err = xh * wv - t_ref[...]
            loss_ref[...] += 0.5 * jnp.sum(jnp.mean(err * err, axis=-1, keepdims=True), axis=0, keepdims=True)
            g = err * (1.0 / d)
            gw = g * wv
            dh_ref[...] = r * (gw - xh * jnp.mean(gw * xh, axis=-1, keepdims=True))
            dw_ref[...] += jnp.sum(g * xh, axis=0, keepdims=True)

    return pl.pallas_call(
        body, name="final_loss", grid=(rows // CHUNK,),
        in_specs=[pl.BlockSpec((CHUNK, d), lambda i: (i, 0)), pl.BlockSpec((1, d), lambda i: (0, 0)),
                  pl.BlockSpec((CHUNK, d), lambda i: (jnp.maximum(i - 1, 0), 0))],
        out_specs=[pl.BlockSpec((1, 1), lambda i: (0, 0)), pl.BlockSpec((CHUNK, d), lambda i: (i, 0)),
                   pl.BlockSpec((1, d), lambda i: (0, 0))],
        out_shape=[jax.ShapeDtypeStruct((1, 1), F32), jax.ShapeDtypeStruct((rows, d), F32),
                   jax.ShapeDtypeStruct((1, d), F32)],
    )(h2, w, target)


def _gate_fwd(o, z, w):
    rs = lax.rsqrt(jnp.mean(o * o, axis=-1, keepdims=True) + EPS)
    return o * rs * w * (z * _sigmoid(z))


def _gate_bwd(dout, o, z, w):
    rs = lax.rsqrt(jnp.mean(o * o, axis=-1, keepdims=True) + EPS)
    yn = o * rs
    sg = _sigmoid(z)
    sil = z * sg
    dsil = sg * (1.0 + z * (1.0 - sg))
    dz = dout * yn * w * dsil
    dyn = dout * w * sil
    dw = jnp.sum(dout * yn * sil, axis=0, keepdims=True)
    do = rs * (dyn - yn * jnp.mean(dyn * yn, axis=-1, keepdims=True))
    return do, dz, dw


def _rope(t, cosf, sinf):
    return t * cosf + pltpu.roll(t, RET_DK // 2, 1) * sinf


def _rope_t(d, cosf, sinf):
    return d * cosf + pltpu.roll(d * sinf, RET_DK // 2, 1)


def _ret_tables():
    log_g = jnp.log1p(-jnp.exp2(-5.0 - jnp.arange(RET_HEADS, dtype=F32)))
    idx = jnp.arange(CHUNK, dtype=F32)
    diff = idx[:, None] - idx[None, :]
    decay = jnp.where(diff >= 0, jnp.exp(log_g[:, None, None] * jnp.maximum(diff, 0.0)), 0.0)
    kw = jnp.exp(log_g[:, None] * (CHUNK - 1 - idx))
    qw = jnp.exp(log_g[:, None] * (idx + 1.0))
    gch = jnp.exp(log_g * CHUNK)
    kw = jnp.broadcast_to(kw[:, :, None], (RET_HEADS, CHUNK, RET_DK))
    qw = jnp.broadcast_to(qw[:, :, None], (RET_HEADS, CHUNK, RET_DK))
    gch = jnp.broadcast_to(gch[:, None, None], (RET_HEADS, 1, RET_DV))
    return decay, kw, qw, gch


def _rope_tables(rows):
    pos = jnp.arange(rows, dtype=F32) - float(PAD)
    inv_freq = jnp.power(ROPE_BASE, -jnp.arange(0, RET_DK, 2, dtype=F32) / RET_DK)
    ang = pos[:, None] * inv_freq[None, :]
    cos, sin = jnp.cos(ang), jnp.sin(ang)
    return jnp.concatenate([cos, cos], axis=1), jnp.concatenate([-sin, sin], axis=1)


RET_HB = 8
RET_QB = RET_HB * RET_DK
RET_VB = RET_HB * RET_DV


def _ret_in_specs(rev, nc):
    def cn(n):
        return (nc - 1 - n) if rev else n
    kb = RET_QK // RET_QB
    vb = 2 * RET_QK // RET_VB
    zb = (2 * RET_QK + RET_W) // RET_VB
    return [
        pl.BlockSpec((CHUNK, RET_QB), lambda h, n: (cn(n), h)),
        pl.BlockSpec((CHUNK, RET_QB), lambda h, n: (cn(n), kb + h)),
        pl.BlockSpec((CHUNK, RET_VB), lambda h, n: (cn(n), vb + h)),
        pl.BlockSpec((CHUNK, RET_VB), lambda h, n: (cn(n), zb + h)),
        pl.BlockSpec((CHUNK, RET_DK), lambda h, n: (cn(n), 0)),
        pl.BlockSpec((CHUNK, RET_DK), lambda h, n: (cn(n), 0)),
        pl.BlockSpec((RET_HB, CHUNK, CHUNK), lambda h, n: (h, 0, 0)),
        pl.BlockSpec((RET_HB, CHUNK, RET_DK), lambda h, n: (h, 0, 0)),
        pl.BlockSpec((RET_HB, CHUNK, RET_DK), lambda h, n: (h, 0, 0)),
        pl.BlockSpec((RET_HB, 1, RET_DV), lambda h, n: (h, 0, 0)),
        pl.BlockSpec((1, RET_VB), lambda h, n: (0, h)),
    ]


def _ret_fwd(proj, cosf, sinf, tables, normw):
    rows = proj.shape[0]
    nc = rows // CHUNK
    decay, kw, qw, gch = tables

    def body(q_ref, k_ref, v_ref, z_ref, cos_ref, sin_ref, dm_ref, kw_ref, qw_ref, g_ref, w_ref,
             o_ref, oa_ref, st_ref, s_scr):
        n = pl.program_id(1)

        @pl.when(n == 0)
        def _():
            s_scr[...] = jnp.zeros_like(s_scr)

        cosv, sinv = cos_ref[...], sin_ref[...]
        for hh in range(RET_HB):
            qc = slice(hh * RET_DK, (hh + 1) * RET_DK)
            vc = slice(hh * RET_DV, (hh + 1) * RET_DV)
            q = _rope(q_ref[:, qc], cosv, sinv)
            k = _rope(k_ref[:, qc], cosv, sinv) * (RET_DK ** -0.5)
            v = v_ref[:, vc]
            s = s_scr[hh]
            st_ref[hh, 0] = s.astype(BF16)
            a = _dot(q, k, NT) * dm_ref[hh]
            o = _dot(a, v) + _dot(q * qw_ref[hh], s)
            s_scr[hh] = s * g_ref[hh] + _dot(k * kw_ref[hh], v, TN)
            o_ref[:, vc] = o
            oa_ref[:, vc] = _gate_fwd(o, z_ref[:, vc], w_ref[:, vc]).astype(BF16)

    return pl.pallas_call(
        body, name="ret_fwd", grid=(RET_HEADS // RET_HB, nc),
        in_specs=_ret_in_specs(False, nc),
        out_specs=[pl.BlockSpec((CHUNK, RET_VB), lambda h, n: (n, h)),
                   pl.BlockSpec((CHUNK, RET_VB), lambda h, n: (n, h)),
                   pl.BlockSpec((RET_HB, 1, RET_DK, RET_DV), lambda h, n: (h, n, 0, 0))],
        out_shape=[jax.ShapeDtypeStruct((rows, RET_W), F32), jax.ShapeDtypeStruct((rows, RET_W), BF16),
                   jax.ShapeDtypeStruct((RET_HEADS, nc, RET_DK, RET_DV), BF16)],
        scratch_shapes=[pltpu.VMEM((RET_HB, RET_DK, RET_DV), F32)],
        compiler_params=pltpu.CompilerParams(dimension_semantics=("parallel", "arbitrary")),
    )(proj, proj, proj, proj, cosf, sinf, decay, kw, qw, gch, normw)


def _ret_bwd(proj, cosf, sinf, tables, normw, o_ret, dmix, states):
    assert RET_HB == RET_HEADS
    rows = proj.shape[0]
    nc = rows // CHUNK
    decay, kw, qw, gch = tables
    ret_cols = 2 * RET_QK + 2 * RET_W

    def rn(n):
        return nc - 1 - n

    def body(q_ref, k_ref, v_ref, z_ref, cos_ref, sin_ref, dm_ref, kw_ref, qw_ref, g_ref, w_ref,
             o_ref, do_ref, st_ref, dp_ref, dw_ref, ds_scr):
        n = pl.program_id(1)
        dq_ref = dp_ref.at[:, 0:RET_QK]
        dk_ref = dp_ref.at[:, RET_QK:2 * RET_QK]
        dv_ref = dp_ref.at[:, 2 * RET_QK:2 * RET_QK + RET_W]
        dz_ref = dp_ref.at[:, 2 * RET_QK + RET_W:ret_cols]

        @pl.when(n == 0)
        def _():
            ds_scr[...] = jnp.zeros_like(ds_scr)
            dw_ref[...] = jnp.zeros_like(dw_ref)

        cosv, sinv = cos_ref[...], sin_ref[...]
        for hh in range(RET_HB):
            qc = slice(hh * RET_DK, (hh + 1) * RET_DK)
            vc = slice(hh * RET_DV, (hh + 1) * RET_DV)
            q = _rope(q_ref[:, qc], cosv, sinv)
            k = _rope(k_ref[:, qc], cosv, sinv) * (RET_DK ** -0.5)
            v = v_ref[:, vc]
            do, dz, dw = _gate_bwd(do_ref[:, vc], o_ref[:, vc], z_ref[:, vc], w_ref[:, vc])
            dz_ref[:, vc] = dz.astype(BF16)
            dw_ref[hh] += dw
            dm = dm_ref[hh]
            s = st_ref[hh, 0]
            g1 = ds_scr[hh]
            p = _dot(q, k, NT) * dm
            kwv = k * kw_ref[hh]
            qwv = q * qw_ref[hh]
            dp = _dot(do, v, NT)
            da = dp * dm
            dv = _dot(p, do, TN) + _dot(kwv, g1)
            dq = _dot(da, k) + _dot(do, s, NT) * qw_ref[hh]
            dk = _dot(da, q, TN) + _dot(v, g1, NT) * kw_ref[hh]
            ds_scr[hh] = g1 * g_ref[hh] + _dot(qwv, do, TN)
            dv_ref[:, vc] = dv.astype(BF16)
            dq_ref[:, qc] = _rope_t(dq, cosv, sinv).astype(BF16)
            dk_ref[:, qc] = _rope_t(dk * (RET_DK ** -0.5), cosv, sinv).astype(BF16)

    in_specs = _ret_in_specs(True, nc) + [
        pl.BlockSpec((CHUNK, RET_VB), lambda h, n: (rn(n), h)),
        pl.BlockSpec((CHUNK, RET_VB), lambda h, n: (rn(n), h)),
        pl.BlockSpec((RET_HB, 1, RET_DK, RET_DV), lambda h, n: (h, rn(n), 0, 0)),
    ]
    return pl.pallas_call(
        body, name="ret_bwd", grid=(RET_HEADS // RET_HB, nc),
        in_specs=in_specs,
        out_specs=[pl.BlockSpec((CHUNK, ret_cols), lambda h, n: (rn(n), 0)),
                   pl.BlockSpec((RET_HB, 1, RET_DV), lambda h, n: (h, 0, 0))],
        out_shape=[jax.ShapeDtypeStruct((rows, IN_AB), BF16), jax.ShapeDtypeStruct((RET_HEADS, 1, RET_DV), F32)],
        scratch_shapes=[pltpu.VMEM((RET_HB, RET_DK, RET_DV), F32)],
        compiler_params=pltpu.CompilerParams(dimension_semantics=("parallel", "arbitrary")),
    )(proj, proj, proj, proj, cosf, sinf, decay, kw, qw, gch, normw, o_ret, dmix, states)


def _s5_discretize(lam_re, lam_im, log_dt, b_re, b_im):
    dt = jnp.exp(log_dt)[:, None]
    mag = jnp.exp(lam_re * dt)
    ab_re, ab_im = mag * jnp.cos(lam_im * dt), mag * jnp.sin(lam_im * dt)
    den = lam_re * lam_re + lam_im * lam_im
    nr, ni = ab_re - 1.0, ab_im
    f_re = (nr * lam_re + ni * lam_im) / den
    f_im = (ni * lam_re - nr * lam_im) / den
    bb_re = f_re[..., None] * b_re - f_im[..., None] * b_im
    bb_im = f_re[..., None] * b_im + f_im[..., None] * b_re
    return ab_re, ab_im, bb_re, bb_im


def _bdiag_in(bb):
    t = bb.reshape(S5_NT, S5_TG, S5_P, S5_GH).transpose(0, 1, 3, 2)
    eye = jnp.eye(S5_TG, dtype=bb.dtype)
    full = t[:, :, :, None, :] * eye[None, :, None, :, None]
    return full.reshape(S5_NT, S5_TU, S5_TS)


def _bdiag_in_extract(dense):
    t = dense.reshape(S5_NT, S5_TG, S5_GH, S5_TG, S5_P)
    diag = jnp.stack([t[:, g, :, g, :] for g in range(S5_TG)], axis=1)
    return diag.transpose(0, 1, 3, 2).reshape(S5_G, S5_P, S5_GH)


def _bdiag_out(c):
    t = c.reshape(S5_NT, S5_TG, S5_GH, S5_P).transpose(0, 1, 3, 2)
    eye = jnp.eye(S5_TG, dtype=c.dtype)
    full = t[:, :, :, None, :] * eye[None, :, None, :, None]
    return full.reshape(S5_NT, S5_TS, S5_TU)


def _bdiag_out_extract(dense):
    t = dense.reshape(S5_NT, S5_TG, S5_P, S5_TG, S5_GH)
    diag = jnp.stack([t[:, g, :, g, :] for g in range(S5_TG)], axis=1)
    return diag.transpose(0, 1, 3, 2).reshape(S5_G, S5_GH, S5_P)


def _cmul(ar, ai, br, bi):
    return ar * br - ai * bi, ar * bi + ai * br


S5_SEG = 8
S5_STEPS = CHUNK // S5_SEG


def _seg_perm(x):
    c = x.shape[1]
    return jnp.swapaxes(x.reshape(S5_SEG, S5_STEPS, c), 0, 1).reshape(CHUNK, c)


def _seg_unperm(x):
    c = x.shape[1]
    return jnp.swapaxes(x.reshape(S5_STEPS, S5_SEG, c), 0, 1).reshape(CHUNK, c)


def _rows(x, p):
    return x[p * S5_SEG:(p + 1) * S5_SEG]


def _s5_tables(ar, ai, tr_scr, ti_scr, wfr_scr, wfi_scr, wbr_scr, wbi_scr):
    row = lax.broadcasted_iota(jnp.int32, (S5_SEG, 1), 0)
    a8r = jnp.broadcast_to(ar, (S5_SEG, S5_TS))
    a8i = jnp.broadcast_to(ai, (S5_SEG, S5_TS))
    pr, pi = a8r, a8i
    for p in range(S5_STEPS):
        tr_scr[p * S5_SEG:(p + 1) * S5_SEG, :] = pr
        ti_scr[p * S5_SEG:(p + 1) * S5_SEG, :] = pi
        if p < S5_STEPS - 1:
            pr, pi = _cmul(pr, pi, a8r, a8i)
    wr, wi = pr, pi
    sh = 1
    while sh < S5_SEG:
        keep = row >= sh
        sr = jnp.where(keep, pltpu.roll(wr, sh, 0), 1.0)
        si = jnp.where(keep, pltpu.roll(wi, sh, 0), 0.0)
        wr, wi = _cmul(wr, wi, sr, si)
        sh *= 2
    wfr_scr[...] = wr
    wfi_scr[...] = wi
    wr, wi = pr, -pi
    sh = 1
    while sh < S5_SEG:
        keep = row < S5_SEG - sh
        sr = jnp.where(keep, pltpu.roll(wr, S5_SEG - sh, 0), 1.0)
        si = jnp.where(keep, pltpu.roll(wi, S5_SEG - sh, 0), 0.0)
        wr, wi = _cmul(wr, wi, sr, si)
        sh *= 2
    wbr_scr[...] = wr
    wbi_scr[...] = wi


def _seg_scan(vr, vi, ar, ai, tr_scr, ti_scr, wr_scr, wi_scr, c0r, c0i, down):
    row = lax.broadcasted_iota(jnp.int32, (S5_SEG, 1), 0)
    sgn = 1.0 if down else -1.0
    order = list(range(S5_STEPS)) if down else list(range(S5_STEPS - 1, -1, -1))
    xr, xi = _rows(vr, order[0]), _rows(vi, order[0])
    loc = {order[0]: (xr, xi)}
    for p in order[1:]:
        mr, mi = _cmul(ar, sgn * ai, xr, xi)
        xr, xi = mr + _rows(vr, p), mi + _rows(vi, p)
        loc[p] = (xr, xi)
    last = S5_STEPS - 1
    mr, mi = tr_scr[last * S5_SEG:(last + 1) * S5_SEG, :], sgn * ti_scr[last * S5_SEG:(last + 1) * S5_SEG, :]
    er, ei = xr, xi
    sh = 1
    while sh < S5_SEG:
        if down:
            keep = row >= sh
            sr, si = pltpu.roll(er, sh, 0), pltpu.roll(ei, sh, 0)
        else:
            keep = row < S5_SEG - sh
            sr, si = pltpu.roll(er, S5_SEG - sh, 0), pltpu.roll(ei, S5_SEG - sh, 0)
        pr, pi = _cmul(mr, mi, jnp.where(keep, sr, 0.0), jnp.where(keep, si, 0.0))
        er, ei = er + pr, ei + pi
        mr, mi = _cmul(mr, mi, mr, mi)
        sh *= 2
    pr, pi = _cmul(wr_scr[...], wi_scr[...], c0r, c0i)
    er, ei = er + pr, ei + pi
    if down:
        nr = jnp.where(row == 0, c0r, pltpu.roll(er, 1, 0))
        ni = jnp.where(row == 0, c0i, pltpu.roll(ei, 1, 0))
    else:
        nr = jnp.where(row == S5_SEG - 1, c0r, pltpu.roll(er, S5_SEG - 1, 0))
        ni = jnp.where(row == S5_SEG - 1, c0i, pltpu.roll(ei, S5_SEG - 1, 0))
    out_r, out_i = [], []
    for p in range(S5_STEPS):
        q = p if down else S5_STEPS - 1 - p
        pr, pi = _cmul(tr_scr[q * S5_SEG:(q + 1) * S5_SEG, :], sgn * ti_scr[q * S5_SEG:(q + 1) * S5_SEG, :], nr, ni)
        out_r.append(loc[p][0] + pr)
        out_i.append(loc[p][1] + pi)
    return jnp.concatenate(out_r, axis=0), jnp.concatenate(out_i, axis=0), (nr, ni), (er, ei)


def _gelu(y):
    c = math.sqrt(2.0 / math.pi)
    return 0.5 * y * (1.0 + jnp.tanh(c * (y + 0.044715 * y * y * y)))


def _gelu_grad(y):
    c = math.sqrt(2.0 / math.pi)
    th = jnp.tanh(c * (y + 0.044715 * y * y * y))
    return 0.5 * (1.0 + th) + 0.5 * y * (1.0 - th * th) * c * (1.0 + 3.0 * 0.044715 * y * y)


def _s5_fwd(proj, ab, bd_b, bd_c, dvec):
    rows = proj.shape[0]
    nc = rows // CHUNK
    tps = S5_FWD_TILES
    ubw = tps * S5_TU
    ub = (2 * RET_QK + 2 * RET_W) // ubw
    ab_re, ab_im = ab
    bre, bim = bd_b
    cre, cim = bd_c

    def body(u_ref, ar_ref, ai_ref, bre_ref, bim_ref, cre_ref, cim_ref, d_ref,
             y_ref, g_ref, er_ref, ei_ref, tr_scr, ti_scr, wfr_scr, wfi_scr, wbr_scr, wbi_scr,
             cr_scr, ci_scr, er_scr, ei_scr):
        n = pl.program_id(1)
        for tt in range(tps):
            cols = slice(tt * S5_TU, (tt + 1) * S5_TU)
            ar, ai = ar_ref[tt], ai_ref[tt]
            trs, tis, wfr, wfi = tr_scr.at[tt], ti_scr.at[tt], wfr_scr.at[tt], wfi_scr.at[tt]

            @pl.when(n == 0)
            def _(tt=tt, ar=ar, ai=ai, trs=trs, tis=tis, wfr=wfr, wfi=wfi):
                _s5_tables(ar, ai, trs, tis, wfr, wfi, wbr_scr.at[tt], wbi_scr.at[tt])
                cr_scr[tt] = jnp.zeros((S5_SEG, S5_TS), F32)
                ci_scr[tt] = jnp.zeros((S5_SEG, S5_TS), F32)

            u = _seg_perm(u_ref[:, cols])
            c0r, c0i = cr_scr[tt], ci_scr[tt]
            er_ref[tt, 0] = c0r
            ei_ref[tt, 0] = c0i
            xr, xi, _, (er, ei) = _seg_scan(_dot(u, bre_ref[tt]), _dot(u, bim_ref[tt]), ar, ai, trs, tis,
                                            wfr, wfi, c0r, c0i, True)
            er_scr[tt] = er
            ei_scr[tt] = ei
            cr_scr[tt] = jnp.broadcast_to(er_scr[tt, S5_SEG - 1:S5_SEG, :], (S5_SEG, S5_TS))
            ci_scr[tt] = jnp.broadcast_to(ei_scr[tt, S5_SEG - 1:S5_SEG, :], (S5_SEG, S5_TS))
            y = _seg_unperm(_dot(xr, cre_ref[tt]) - _dot(xi, cim_ref[tt]) + d_ref[:, cols] * u)
            y_ref[:, cols] = y
            g_ref[:, cols] = _gelu(y).astype(BF16)

    vec = pl.BlockSpec((tps, 1, S5_TS), lambda t, n: (t, 0, 0))
    return pl.pallas_call(
        body, name="s5_fwd", grid=(S5_NT // tps, nc),
        in_specs=[pl.BlockSpec((CHUNK, ubw), lambda t, n: (n, ub + t)), vec, vec,
                  pl.BlockSpec((tps, S5_TU, S5_TS), lambda t, n: (t, 0, 0)),
                  pl.BlockSpec((tps, S5_TU, S5_TS), lambda t, n: (t, 0, 0)),
                  pl.BlockSpec((tps, S5_TS, S5_TU), lambda t, n: (t, 0, 0)),
                  pl.BlockSpec((tps, S5_TS, S5_TU), lambda t, n: (t, 0, 0)),
                  pl.BlockSpec((1, ubw), lambda t, n: (0, t))],
        out_specs=[pl.BlockSpec((CHUNK, ubw), lambda t, n: (n, t)),
                   pl.BlockSpec((CHUNK, ubw), lambda t, n: (n, t)),
                   pl.BlockSpec((tps, 1, 8, S5_TS), lambda t, n: (t, n, 0, 0)),
                   pl.BlockSpec((tps, 1, 8, S5_TS), lambda t, n: (t, n, 0, 0))],
        out_shape=[jax.ShapeDtypeStruct((rows, S5_W), F32), jax.ShapeDtypeStruct((rows, S5_W), BF16),
                   jax.ShapeDtypeStruct((S5_NT, nc, 8, S5_TS), F32),
                   jax.ShapeDtypeStruct((S5_NT, nc, 8, S5_TS), F32)],
        scratch_shapes=[pltpu.VMEM((tps, CHUNK, S5_TS), F32) for _ in range(2)]
        + [pltpu.VMEM((tps, S5_SEG, S5_TS), F32) for _ in range(8)],
        compiler_params=pltpu.CompilerParams(dimension_semantics=("parallel", "arbitrary")),
    )(proj, ab_re.reshape(S5_NT, 1, S5_TS), ab_im.reshape(S5_NT, 1, S5_TS), bre, bim, cre, cim, dvec)


def _s5_bwd(proj, dy, ab, bd_b, bd_c, dvec, entry, dproj):
    rows = proj.shape[0]
    nc = rows // CHUNK
    tps = S5_BWD_TILES
    ubw = tps * S5_TU
    ub = (2 * RET_QK + 2 * RET_W) // ubw
    ab_re, ab_im = ab
    bre, bim = bd_b
    cre, cim = bd_c
    er, ei = entry

    def rn(n):
        return nc - 1 - n

    def body(u_ref, dy_ref, ar_ref, ai_ref, bre_ref, bim_ref, cre_ref, cim_ref, d_ref, er_ref, ei_ref, dp_ref,
             du_ref, dbr_ref, dbi_ref, dcr_ref, dci_ref, dar_ref, dai_ref, dd_ref,
             tr_scr, ti_scr, wfr_scr, wfi_scr, wbr_scr, wbi_scr, gr_scr, gi_scr, er_scr, ei_scr):
        n = pl.program_id(1)

        @pl.when(n == 0)
        def _():
            gr_scr[...] = jnp.zeros_like(gr_scr)
            gi_scr[...] = jnp.zeros_like(gi_scr)
            for r in (dbr_ref, dbi_ref, dcr_ref, dci_ref, dar_ref, dai_ref, dd_ref):
                r[...] = jnp.zeros_like(r)

        for tt in range(tps):
            cols = slice(tt * S5_TU, (tt + 1) * S5_TU)
            ar, ai = ar_ref[tt], ai_ref[tt]
            trs, tis = tr_scr.at[tt], ti_scr.at[tt]

            @pl.when(n == 0)
            def _(tt=tt, ar=ar, ai=ai, trs=trs, tis=tis):
                _s5_tables(ar, ai, trs, tis, wfr_scr.at[tt], wfi_scr.at[tt], wbr_scr.at[tt], wbi_scr.at[tt])

            u = _seg_perm(u_ref[:, cols])
            dy = _seg_perm(dy_ref[:, cols])
            xr, xi, (pr, pi), _ = _seg_scan(_dot(u, bre_ref[tt]), _dot(u, bim_ref[tt]), ar, ai, trs, tis,
                                            wfr_scr.at[tt], wfi_scr.at[tt], er_ref[tt, 0], ei_ref[tt, 0], True)
            dcr_ref[tt] += _dot(xr, dy, TN)
            dci_ref[tt] -= _dot(xi, dy, TN)
            gr, gi, _, (er, ei) = _seg_scan(_dot(dy, cre_ref[tt], NT), -_dot(dy, cim_ref[tt], NT), ar, ai, trs, tis,
                                            wbr_scr.at[tt], wbi_scr.at[tt], gr_scr[tt], gi_scr[tt], False)
            er_scr[tt] = er
            ei_scr[tt] = ei
            gr_scr[tt] = jnp.broadcast_to(er_scr[tt, 0:1, :], (S5_SEG, S5_TS))
            gi_scr[tt] = jnp.broadcast_to(ei_scr[tt, 0:1, :], (S5_SEG, S5_TS))
            xpr = jnp.concatenate([pr, xr[:CHUNK - S5_SEG]], axis=0)
            xpi = jnp.concatenate([pi, xi[:CHUNK - S5_SEG]], axis=0)
            dar_ref[tt] += jnp.sum((xpr * gr + xpi * gi).reshape(S5_STEPS, S5_SEG, S5_TS), axis=0)
            dai_ref[tt] += jnp.sum((xpr * gi - xpi * gr).reshape(S5_STEPS, S5_SEG, S5_TS), axis=0)
            dbr_ref[tt] += _dot(u, gr, TN)
            dbi_ref[tt] += _dot(u, gi, TN)
            dd_ref[tt] += jnp.sum((dy * u).reshape(S5_STEPS, S5_SEG, S5_TU), axis=0)
            du = dy * d_ref[:, cols] + _dot(gr, bre_ref[tt], NT) + _dot(gi, bim_ref[tt], NT)
            du_ref[:, cols] = _seg_unperm(du).astype(BF16)

    vec = pl.BlockSpec((tps, 1, S5_TS), lambda t, n: (t, 0, 0))
    acc_b = pl.BlockSpec((tps, S5_TU, S5_TS), lambda t, n: (t, 0, 0))
    acc_c = pl.BlockSpec((tps, S5_TS, S5_TU), lambda t, n: (t, 0, 0))
    acc_a = pl.BlockSpec((tps, 8, S5_TS), lambda t, n: (t, 0, 0))
    ent = pl.BlockSpec((tps, 1, 8, S5_TS), lambda t, n: (t, rn(n), 0, 0))
    return pl.pallas_call(
        body, name="s5_bwd", grid=(S5_NT // tps, nc),
        in_specs=[pl.BlockSpec((CHUNK, ubw), lambda t, n: (rn(n), ub + t)),
                  pl.BlockSpec((CHUNK, ubw), lambda t, n: (rn(n), t)), vec, vec,
                  acc_b, acc_b, acc_c, acc_c, pl.BlockSpec((1, ubw), lambda t, n: (0, t)), ent, ent, ANY],
        out_specs=[pl.BlockSpec((CHUNK, ubw), lambda t, n: (rn(n), ub + t)), acc_b, acc_b, acc_c, acc_c, acc_a, acc_a,
                   pl.BlockSpec((tps, 8, S5_TU), lambda t, n: (t, 0, 0))],
        input_output_aliases={11: 0},
        out_shape=[jax.ShapeDtypeStruct(dproj.shape, BF16),
                   jax.ShapeDtypeStruct((S5_NT, S5_TU, S5_TS), F32), jax.ShapeDtypeStruct((S5_NT, S5_TU, S5_TS), F32),
                   jax.ShapeDtypeStruct((S5_NT, S5_TS, S5_TU), F32), jax.ShapeDtypeStruct((S5_NT, S5_TS, S5_TU), F32),
                   jax.ShapeDtypeStruct((S5_NT, 8, S5_TS), F32), jax.ShapeDtypeStruct((S5_NT, 8, S5_TS), F32),
                   jax.ShapeDtypeStruct((S5_NT, 8, S5_TU), F32)],
        scratch_shapes=[pltpu.VMEM((tps, CHUNK, S5_TS), F32) for _ in range(2)]
        + [pltpu.VMEM((tps, S5_SEG, S5_TS), F32) for _ in range(8)],
        compiler_params=pltpu.CompilerParams(dimension_semantics=("parallel", "arbitrary")),
    )(proj, dy, ab_re.reshape(S5_NT, 1, S5_TS), ab_im.reshape(S5_NT, 1, S5_TS), bre, bim, cre, cim, dvec, er, ei,
      dproj)


def _s5_gate_bwd(dmix, g, t, proj, dproj):
    rows = g.shape[0]
    tm = _row_tile(rows, 384)
    ob = RET_W // S5_W
    zb = (2 * RET_QK + 2 * RET_W + S5_W) // S5_W

    def body(do_ref, g_ref, t_ref, z_ref, dp_ref, dz_ref, dt_ref, dg_ref):
        do = do_ref[...]
        gv = g_ref[...].astype(F32)
        z = z_ref[...]
        st = _sigmoid(t_ref[...])
        sg = _sigmoid(z)
        os5 = gv * st
        dz_ref[...] = (do * os5 * sg * (1.0 + z * (1.0 - sg))).astype(BF16)
        dos = do * z * sg
        dt_ref[...] = (dos * gv * st * (1.0 - st)).astype(BF16)
        dg_ref[...] = dos * st

    blk = pl.BlockSpec((tm, S5_W), lambda i: (i, 0))
    return pl.pallas_call(
        body, name="s5_gate_bwd", grid=(rows // tm,),
        in_specs=[pl.BlockSpec((tm, S5_W), lambda i: (i, ob)), blk, blk,
                  pl.BlockSpec((tm, S5_W), lambda i: (i, zb)), ANY],
        out_specs=[pl.BlockSpec((tm, S5_W), lambda i: (i, zb)), blk, blk],
        out_shape=[jax.ShapeDtypeStruct(dproj.shape, BF16), jax.ShapeDtypeStruct((rows, S5_W), BF16),
                   jax.ShapeDtypeStruct((rows, S5_W), F32)],
        input_output_aliases={4: 0},
    )(dmix, g, t, proj, dproj)


def _split3(x):
    hi = x.astype(BF16)
    r = x - hi.astype(F32)
    mid = r.astype(BF16)
    lo = (r - mid.astype(F32)).astype(BF16)
    return hi, mid, lo


def _tri_sum(x, upper):
    i = lax.broadcasted_iota(jnp.int32, (CHUNK, CHUNK), 0)
    j = lax.broadcasted_iota(jnp.int32, (CHUNK, CHUNK), 1)
    tri = jnp.where((j >= i) if upper else (j <= i), 1.0, 0.0).astype(BF16)
    hi, mid, lo = _split3(x)
    return _dot(tri, lo) + _dot(tri, mid) + _dot(tri, hi)


def _gla_log_decay(gl, wg, bg, n):
    logit = _dot(gl, wg) + bg
    la = (jnp.minimum(logit, 0.0) - jnp.log(1.0 + jnp.exp(-jnp.abs(logit)))) * (1.0 / GLA_TAU)
    row = lax.broadcasted_iota(jnp.int32, (CHUNK, 1), 0)
    live = jnp.logical_or(n > 0, row >= PAD)
    return logit, jnp.where(live, la, 0.0), live


def _gla_in_specs(rev, nc):
    def cn(n):
        return (nc - 1 - n) if rev else n
    kb = GLA_QK // GLA_DK
    vb = 2 * GLA_QK // GLA_DV
    zb = (2 * GLA_QK + GLA_W) // GLA_DV
    gb = (2 * GLA_QK + 2 * GLA_W) // 128
    return [
        pl.BlockSpec((CHUNK, GLA_DK), lambda h, n: (cn(n), h)),
        pl.BlockSpec((CHUNK, GLA_DK), lambda h, n: (cn(n), kb + h)),
        pl.BlockSpec((CHUNK, GLA_DV), lambda h, n: (cn(n), vb + h)),
        pl.BlockSpec((CHUNK, GLA_DV), lambda h, n: (cn(n), zb + h)),
        pl.BlockSpec((CHUNK, 128), lambda h, n: (cn(n), gb)),
        pl.BlockSpec((128, GLA_DK), lambda h, n: (0, h)),
        pl.BlockSpec((1, GLA_DK), lambda h, n: (0, h)),
        pl.BlockSpec((1, GLA_DV), lambda h, n: (0, h)),
    ]


def _gla_fwd(proj, wgate, bgate, normw):
    rows = proj.shape[0]
    nc = rows // CHUNK

    def body(q_ref, k_ref, v_ref, z_ref, gl_ref, wg_ref, bg_ref, w_ref, o_ref, oc_ref, st_ref, s_scr, b_scr):
        n = pl.program_id(1)

        @pl.when(n == 0)
        def _():
            s_scr[...] = jnp.zeros_like(s_scr)

        q = q_ref[...] * (GLA_DK ** -0.5)
        k = k_ref[...]
        v = v_ref[...]
        vb = v.astype(BF16)
        _, la, _ = _gla_log_decay(gl_ref[...], wg_ref[...], bg_ref[...], n)
        b = _tri_sum(la, False)
        b_scr[...] = b
        b_last = b_scr[CHUNK - 1:CHUNK, :]
        st = s_scr[...]
        st_ref[0, 0] = st
        s_scr[...] = st * jnp.exp(b_last) + _dot(v, k * jnp.exp(b_last - b), TN)
        rowc = lax.broadcasted_iota(jnp.int32, (CHUNK, 1), 0)
        rows16 = lax.broadcasted_iota(jnp.int32, (SUB, 1), 0)
        a_tot = jnp.zeros((CHUNK, CHUNK), F32)
        for s in range(1, NSUB):
            lo = s * SUB
            bref = b_scr[lo - 1:lo, :]
            in_s = jnp.logical_and(rowc >= lo, rowc < lo + SUB)
            qh = q * jnp.exp(jnp.where(in_s, b - bref, -1e30))
            kh = k * jnp.exp(jnp.where(rowc < lo, bref - b, -1e30))
            a_tot = a_tot + _dot(qh, kh, NT)
        lane = lax.broadcasted_iota(jnp.int32, (SUB, CHUNK), 1)
        diag = []
        for s in range(NSUB):
            lo = s * SUB
            qs, bs = q[lo:lo + SUB], b[lo:lo + SUB]
            s_blk = jnp.zeros((SUB, CHUNK), F32)
            for j in range(SUB):
                r = lo + j
                e = jnp.exp(jnp.where(rows16 >= j, bs - b_scr[r:r + 1, :], -1e30))
                col = jnp.sum(qs * k_ref[r:r + 1, :] * e, axis=1, keepdims=True)
                s_blk = jnp.where(lane == r, col, s_blk)
            diag.append(s_blk)
        o = _dot(q * jnp.exp(b), st, NT) + _dot(a_tot + jnp.concatenate(diag, axis=0), vb)
        o_ref[...] = o
        oc_ref[...] = _gate_fwd(o, z_ref[...], w_ref[...]).astype(BF16)

    return pl.pallas_call(
        body, name="gla_fwd", grid=(GLA_HEADS, nc),
        in_specs=_gla_in_specs(False, nc),
        out_specs=[pl.BlockSpec((CHUNK, GLA_DV), lambda h, n: (n, h)),
                   pl.BlockSpec((CHUNK, GLA_DV), lambda h, n: (n, h)),
                   pl.BlockSpec((1, 1, GLA_DV, GLA_DK), lambda h, n: (h, n, 0, 0))],
        out_shape=[jax.ShapeDtypeStruct((rows, GLA_W), F32), jax.ShapeDtypeStruct((rows, GLA_W), BF16),
                   jax.ShapeDtypeStruct((GLA_HEADS, nc, GLA_DV, GLA_DK), F32)],
        scratch_shapes=[pltpu.VMEM((GLA_DV, GLA_DK), F32), pltpu.VMEM((CHUNK, GLA_DK), F32)],
        compiler_params=pltpu.CompilerParams(dimension_semantics=("parallel", "arbitrary")),
    )(proj, proj, proj, proj, proj, wgate, bgate, normw)


def _gla_bwd(proj, wgate, bgate, normw, o_gla, d_oc, states):
    rows = proj.shape[0]
    nc = rows // CHUNK

    def rn(n):
        return nc - 1 - n

    def body(q_ref, k_ref, v_ref, z_ref, gl_ref, wg_ref, bg_ref, w_ref, o_ref, do_ref, st_ref,
             dq_ref, dk_ref, dv_ref, dz_ref, dl_ref, dw_ref, dbg_ref,
             ds_scr, dq_scr, dk_scr, dv_scr, db_scr, b_scr, q_scr):
        n = pl.program_id(1)
        cn = rn(n)

        @pl.when(n == 0)
        def _():
            ds_scr[...] = jnp.zeros_like(ds_scr)
            dw_ref[...] = jnp.zeros_like(dw_ref)
            dbg_ref[...] = jnp.zeros_like(dbg_ref)

        q = q_ref[...] * (GLA_DK ** -0.5)
        k = k_ref[...]
        v = v_ref[...]
        vb = v.astype(BF16)
        do, dz, dw = _gate_bwd(do_ref[...], o_ref[...], z_ref[...], w_ref[...])
        dz_ref[...] = dz.astype(BF16)
        dw_ref[0] += dw
        logit, la, live = _gla_log_decay(gl_ref[...], wg_ref[...], bg_ref[...], cn)
        b = _tri_sum(la, False)
        b_scr[...] = b
        b_last = b_scr[CHUNK - 1:CHUNK, :]
        e_last = jnp.exp(b_last)
        st = st_ref[0, 0]
        g1 = ds_scr[...]
        eb = jnp.exp(b)
        qe = q * eb
        dqe = _dot(do, st)
        dq_scr[...] = dqe * eb
        db_scr[...] = dqe * qe
        ekb = jnp.exp(b_last - b)
        kdec = k * ekb
        dkdec = _dot(v, g1)
        dv_scr[...] = _dot(kdec, g1, NT)
        dk_scr[...] = dkdec * ekb
        wk = dkdec * kdec
        db_scr[...] -= wk
        dbl = jnp.sum(wk, axis=0, keepdims=True) + jnp.sum(g1 * st, axis=0, keepdims=True) * e_last
        ds_scr[...] = g1 * e_last + _dot(do, qe, TN)
        rowc = lax.broadcasted_iota(jnp.int32, (CHUNK, 1), 0)
        rows16 = lax.broadcasted_iota(jnp.int32, (SUB, 1), 0)
        da_full = _dot(do, vb, NT)
        a_tot = jnp.zeros((CHUNK, CHUNK), F32)
        for s in range(1, NSUB):
            lo = s * SUB
            bref = b_scr[lo - 1:lo, :]
            in_s = jnp.logical_and(rowc >= lo, rowc < lo + SUB)
            eq = jnp.exp(jnp.where(in_s, b - bref, -1e30))
            ek = jnp.exp(jnp.where(rowc < lo, bref - b, -1e30))
            qh = q * eq
            kh = k * ek
            a_tot = a_tot + _dot(qh, kh, NT)
            da = jnp.where(in_s, da_full, 0.0)
            dqh = _dot(da, kh)
            dkh = _dot(da, qh, TN)
            tq = dqh * qh
            tk = dkh * kh
            dq_scr[...] += dqh * eq
            dk_scr[...] += dkh * ek
            db_scr[...] += tq - tk
            db_scr[lo - 1:lo, :] += jnp.sum(tk, axis=0, keepdims=True) - jnp.sum(tq, axis=0, keepdims=True)
        dat_full = _dot(vb, do, NT)
        q_scr[...] = q
        lane = lax.broadcasted_iota(jnp.int32, (SUB, CHUNK), 1)
        diag = []
        for s in range(NSUB):
            lo = s * SUB
            qs, ks, bs = q[lo:lo + SUB], k[lo:lo + SUB], b[lo:lo + SUB]
            da_blk, dat_blk = da_full[lo:lo + SUB], dat_full[lo:lo + SUB]
            dqs = jnp.zeros((SUB, GLA_DK), F32)
            dks = jnp.zeros((SUB, GLA_DK), F32)
            dbs = jnp.zeros((SUB, GLA_DK), F32)
            s_blk = jnp.zeros((SUB, CHUNK), F32)
            for j in range(SUB):
                r = lo + j
                kj = k_ref[r:r + 1, :]
                e = jnp.exp(jnp.where(rows16 >= j, bs - b_scr[r:r + 1, :], -1e30))
                p = qs * e * kj
                s_blk = jnp.where(lane == r, jnp.sum(p, axis=1, keepdims=True), s_blk)
                dcol = jnp.sum(jnp.where(lane == r, da_blk, 0.0), axis=1, keepdims=True)
                dqs = dqs + (dcol * e) * kj
                dbs = dbs + dcol * p
            for i in range(SUB):
                r = lo + i
                e = jnp.exp(jnp.where(rows16 <= i, b_scr[r:r + 1, :] - bs, -1e30))
                drow = jnp.sum(jnp.where(lane == r, dat_blk, 0.0), axis=1, keepdims=True)
                nq = (drow * e) * q_scr[r:r + 1, :]
                dks = dks + nq
                dbs = dbs - nq * ks
            dq_scr[lo:lo + SUB, :] += dqs
            dk_scr[lo:lo + SUB, :] += dks
            db_scr[lo:lo + SUB, :] += dbs
            diag.append(s_blk)
        dv_scr[...] += _dot(a_tot + jnp.concatenate(diag, axis=0), do, TN)
        db_scr[CHUNK - 1:CHUNK, :] += dbl
        dla = _tri_sum(db_scr[...], True)
        dlogit = jnp.where(live, dla * (1.0 / GLA_TAU) * _sigmoid(-logit), 0.0)
        dl_ref[...] = dlogit
        dbg_ref[0] += jnp.sum(dlogit, axis=0, keepdims=True)
        dq_ref[...] = (dq_scr[...] * (GLA_DK ** -0.5)).astype(BF16)
        dk_ref[...] = dk_scr[...].astype(BF16)
        dv_ref[...] = dv_scr[...].astype(BF16)

    in_specs = _gla_in_specs(True, nc) + [
        pl.BlockSpec((CHUNK, GLA_DV), lambda h, n: (rn(n), h)),
        pl.BlockSpec((CHUNK, GLA_DV), lambda h, n: (rn(n), h)),
        pl.BlockSpec((1, 1, GLA_DV, GLA_DK), lambda h, n: (h, rn(n), 0, 0)),
    ]
    return pl.pallas_call(
        body, name="gla_bwd", grid=(GLA_HEADS, nc),
        in_specs=in_specs,
        out_specs=[pl.BlockSpec((CHUNK, GLA_DK), lambda h, n: (rn(n), h)),
                   pl.BlockSpec((CHUNK, GLA_DK), lambda h, n: (rn(n), h)),
                   pl.BlockSpec((CHUNK, GLA_DV), lambda h, n: (rn(n), h)),
                   pl.BlockSpec((CHUNK, GLA_DV), lambda h, n: (rn(n), h)),
                   pl.BlockSpec((CHUNK, GLA_DK), lambda h, n: (rn(n), h)),
                   pl.BlockSpec((1, 1, GLA_DV), lambda h, n: (h, 0, 0)),
                   pl.BlockSpec((1, 1, GLA_DK), lambda h, n: (h, 0, 0))],
        out_shape=[jax.ShapeDtypeStruct((rows, GLA_QK), BF16), jax.ShapeDtypeStruct((rows, GLA_QK), BF16),
                   jax.ShapeDtypeStruct((rows, GLA_W), BF16), jax.ShapeDtypeStruct((rows, GLA_W), BF16),
                   jax.ShapeDtypeStruct((rows, GLA_QK), F32),
                   jax.ShapeDtypeStruct((GLA_HEADS, 1, GLA_DV), F32),
                   jax.ShapeDtypeStruct((GLA_HEADS, 1, GLA_DK), F32)],
        scratch_shapes=[pltpu.VMEM((GLA_DV, GLA_DK), F32), pltpu.VMEM((CHUNK, GLA_DK), F32),
                        pltpu.VMEM((CHUNK, GLA_DK), F32), pltpu.VMEM((CHUNK, GLA_DV), F32),
                        pltpu.VMEM((CHUNK, GLA_DK), F32), pltpu.VMEM((CHUNK, GLA_DK), F32),
                        pltpu.VMEM((CHUNK, GLA_DK), F32)],
        compiler_params=pltpu.CompilerParams(dimension_semantics=("parallel", "arbitrary")),
    )(proj, proj, proj, proj, proj, wgate, bgate, normw, o_gla, d_oc, states)


def _adamw(name, w, g, m, v):
    rows, cols = w.shape
    tm = 8
    for cand in range(8, rows + 1, 8):
        if rows % cand == 0 and cand * cols * 4 <= 2 ** 21:
            tm = cand
    c1 = 1.0 - ADAM_B1 ** ADAM_STEP
    c2 = 1.0 - ADAM_B2 ** ADAM_STEP

    def body(w_ref, g_ref, m_ref, v_ref, d_ref, nm_ref, nv_ref):
        gv = g_ref[...]
        nm = ADAM_B1 * m_ref[...] + (1.0 - ADAM_B1) * gv
        nv = ADAM_B2 * v_ref[...] + (1.0 - ADAM_B2) * (gv * gv)
        nm_ref[...] = nm
        nv_ref[...] = nv
        d_ref[...] = -ADAM_LR * ((nm / c1) / (jnp.sqrt(nv / c2) + ADAM_EPS) + ADAM_WD * w_ref[...])

    blk = pl.BlockSpec((tm, cols), lambda i: (i, 0))
    return pl.pallas_call(
        body, name=name, grid=(rows // tm,),
        in_specs=[blk] * 4, out_specs=[blk] * 3,
        out_shape=[jax.ShapeDtypeStruct((rows, cols), F32)] * 3,
    )(w, g, m, v)


def _place():
    x, y, c = lax.axis_index("x"), lax.axis_index("y"), lax.axis_index("c")
    chips = [(1 - x, y), (x, 1 - y), (1 - x, 1 - y)]
    return x, y, c, chips


ANY = pl.BlockSpec(memory_space=pl.ANY)


def _gathered_struct(shape, dtype, kind):
    r, cc = shape
    if kind == "row":
        return jax.ShapeDtypeStruct((N_SHARD * r, cc), dtype)
    if kind == "col":
        return jax.ShapeDtypeStruct((r, N_SHARD * cc), dtype)
    return jax.ShapeDtypeStruct((N_SHARD, r, cc), dtype)


def _cast_place(name, w, kind, mine_arr, dtype, also_own=False):
    r, cc = w.shape
    tr = r
    for cand in (256, 128, 64, 32, 16):
        if r % cand == 0:
            tr = cand
            break
    nb = r // tr
    if kind == "row":
        o_spec = pl.BlockSpec((tr, cc), lambda i, m: (m[0] * nb + i, 0))
    elif kind == "col":
        o_spec = pl.BlockSpec((tr, cc), lambda i, m: (i, m[0]))
    else:
        o_spec = pl.BlockSpec((None, tr, cc), lambda i, m: (m[0], i, 0))
    w_spec = pl.BlockSpec((tr, cc), lambda i, m: (i, 0))

    def body(m_ref, w_ref, o_ref, *own_ref):
        o_ref[...] = w_ref[...].astype(o_ref.dtype)
        for ref in own_ref:
            ref[...] = w_ref[...].astype(ref.dtype)

    out_specs, out_shape = [o_spec], [_gathered_struct((r, cc), dtype, kind)]
    if also_own:
        out_specs.append(w_spec)
        out_shape.append(jax.ShapeDtypeStruct((r, cc), dtype))
    out = pl.pallas_call(
        body, name=name,
        grid_spec=pltpu.PrefetchScalarGridSpec(
            num_scalar_prefetch=1, grid=(nb,), in_specs=[w_spec], out_specs=out_specs),
        out_shape=out_shape,
    )(mine_arr, w)
    return out if also_own else out[0]


def _gather_small(shard):
    rows, cols = shard.shape

    def body(in_ref, out_ref, send_sems, recv_sems):
        x, y, c, chips = _place()
        mine = 2 * x + y
        out_ref[mine] = in_ref[...]
        cps = []
        for j, chip in enumerate(chips):
            cp = pltpu.make_async_remote_copy(
                src_ref=in_ref, dst_ref=out_ref.at[mine], send_sem=send_sems.at[j], recv_sem=recv_sems.at[j],
                device_id=(*chip, c), device_id_type=MESH)
            cp.start()
            cps.append(cp)
        for cp in cps:
            cp.wait()

    vm = pl.BlockSpec(memory_space=pltpu.VMEM)
    return pl.pallas_call(
        body, name="gather_small",
        in_specs=[vm], out_specs=vm,
        out_shape=jax.ShapeDtypeStruct((N_SHARD, rows, cols), F32),
        scratch_shapes=[pltpu.SemaphoreType.DMA((3,)), pltpu.SemaphoreType.DMA((3,))],
        compiler_params=pltpu.CompilerParams(has_side_effects=True),
    )(shard)


def _in_proj_shifted(name, a, b, n, shifts, tm, tn, out_cols, into=None):
    m, k = a.shape
    nb_b = b.shape[1] // tn
    nb_o = out_cols // tn

    def body(s_ref, a_ref, b_ref, *rest):
        rest[-1][...] = _dot(a_ref[...], b_ref[...])

    in_specs = [pl.BlockSpec((tm, k), lambda i, j, s: (i, 0)),
                pl.BlockSpec((k, tn), lambda i, j, s: (0, (s[0] + j) % nb_b))]
    operands = [shifts, a, b]
    aliases = {}
    if into is not None:
        in_specs.append(ANY)
        operands.append(into)
        aliases = {3: 0}
    return pl.pallas_call(
        body, name=name,
        grid_spec=pltpu.PrefetchScalarGridSpec(
            num_scalar_prefetch=1, grid=(m // tm, n // tn), in_specs=in_specs,
            out_specs=pl.BlockSpec((tm, tn), lambda i, j, s: (i, (s[1] + j) % nb_o))),
        out_shape=jax.ShapeDtypeStruct((m, out_cols), F32), input_output_aliases=aliases,
    )(*operands)


def _allreduce_small(buf):
    rows, cols = buf.shape
    hr = rows // 2

    def body(in_ref, out_ref, sib_ref, pair_ref, far_ref, send_sems, recv_sems):
        x, y, c, chips = _place()
        sibling = (x, y, 1 - c)
        mine = pl.ds(pl.multiple_of(c * hr, 8), hr)
        theirs = pl.ds(pl.multiple_of((1 - c) * hr, 8), hr)
        to_sib = pltpu.make_async_remote_copy(
            src_ref=in_ref.at[theirs, :], dst_ref=sib_ref, send_sem=send_sems.at[0], recv_sem=recv_sems.at[0],
            device_id=sibling, device_id_type=MESH)
        to_sib.start()
        to_sib.wait()
        pair_ref[...] = in_ref[mine, :] + sib_ref[...]
        far = [pltpu.make_async_remote_copy(
            src_ref=pair_ref, dst_ref=far_ref.at[j], send_sem=send_sems.at[1 + j], recv_sem=recv_sems.at[1 + j],
            device_id=(*chip, c), device_id_type=MESH) for j, chip in enumerate(chips)]
        for cp in far:
            cp.start()
        for cp in far:
            cp.wait()
        out_ref[mine, :] = (pair_ref[...] + far_ref[1]) + (far_ref[0] + far_ref[2])
        swap = pltpu.make_async_remote_copy(
            src_ref=out_ref.at[mine, :], dst_ref=out_ref.at[mine, :], send_sem=send_sems.at[4],
            recv_sem=recv_sems.at[4], device_id=sibling, device_id_type=MESH)
        swap.start()
        swap.wait()

    vm = pl.BlockSpec(memory_space=pltpu.VMEM)
    return pl.pallas_call(
        body, name="allreduce_small",
        in_specs=[vm], out_specs=vm,
        out_shape=jax.ShapeDtypeStruct((rows, cols), F32),
        scratch_shapes=[pltpu.VMEM((hr, cols), F32), pltpu.VMEM((hr, cols), F32),
                        pltpu.VMEM((3, hr, cols), F32),
                        pltpu.SemaphoreType.DMA((5,)), pltpu.SemaphoreType.DMA((5,))],
        compiler_params=pltpu.CompilerParams(has_side_effects=True),
    )(buf)


def _shard_window(ref, kind, shard_shape, shard, half):
    r, cc = shard_shape
    hr = r // 2
    if kind == "row":
        return ref.at[pl.ds(_mo(shard * r + half * hr, 8), hr), :]
    if kind == "col":
        return ref.at[pl.ds(_mo(half * hr, 8), hr), pl.ds(_mo(shard * cc, 128), cc)]
    if kind == "colw":
        return ref.at[pl.ds(_mo(half * hr, 8), hr), pl.ds(_mo(shard * (cc - 128), 128), cc)]
    return ref.at[shard, pl.ds(_mo(half * hr, 8), hr), :]


HBM = pl.BlockSpec(memory_space=pltpu.HBM)
SEM = pl.BlockSpec(memory_space=pltpu.SEMAPHORE)
DATAFLOW = pltpu.SideEffectType.DATAFLOW_SIDE_EFFECTING


def _in_hbm(a):
    return pltpu.with_memory_space_constraint(a, pltpu.HBM)


def _empty_hbm(shape, dtype):
    return _in_hbm(lax.empty(shape, dtype))


def _copies_start(name, bufs, n_copies, plan, carry):
    nb = len(bufs)

    def body(*refs):
        send_sems, recv_sems = refs[nb + 1], refs[nb + 2]
        for k, (src, dst, to) in enumerate(plan(refs[:nb])):
            pltpu.make_async_remote_copy(src_ref=src, dst_ref=dst, send_sem=send_sems.at[k], recv_sem=recv_sems.at[k],
                                         device_id=to, device_id_type=MESH).start()

    passed = list(bufs) + [carry]
    out = pl.pallas_call(
        body, name=name,
        in_specs=[HBM] * (nb + 1), out_specs=[SEM, SEM] + [HBM] * (nb + 1),
        out_shape=[pltpu.SemaphoreType.DMA((n_copies,)), pltpu.SemaphoreType.DMA((n_copies,))]
        + [pltpu.HBM(a.shape, a.dtype) for a in passed],
        input_output_aliases={i: 2 + i for i in range(nb + 1)},
        compiler_params=pltpu.CompilerParams(has_side_effects=DATAFLOW),
    )(*[_in_hbm(a) for a in passed])
    return out[0], out[1], list(out[2:2 + nb]), out[2 + nb]


def _copies_wait(name, send_sems, recv_sems, bufs, plan, after):
    nb = len(bufs)
    after = list(after) if isinstance(after, (list, tuple)) else [after]

    def body(*refs):
        send, recv = refs[nb], refs[nb + 1]
        for k, (src, dst, to) in enumerate(plan(refs[:nb])):
            cp = pltpu.make_async_remote_copy(src_ref=src, dst_ref=dst, send_sem=send.at[k], recv_sem=recv.at[k],
                                              device_id=to, device_id_type=MESH)
            cp.wait_send()
            cp.wait_recv()

    out = pl.pallas_call(
        body, name=name,
        in_specs=[HBM] * nb + [SEM, SEM] + [ANY] * len(after), out_specs=[HBM] * nb,
        out_shape=[pltpu.HBM(a.shape, a.dtype) for a in bufs],
        input_output_aliases={i: i for i in range(nb)},
        compiler_params=pltpu.CompilerParams(has_side_effects=DATAFLOW),
    )(*bufs, send_sems, recv_sems, *after)
    return list(out)


def _gather_ici_plan(shard_shapes, kinds):
    n_arr = len(kinds)

    def plan(refs):
        x, y, c, chips = _place()
        out = []
        for i in range(n_arr):
            w = _shard_window(refs[i], kinds[i], shard_shapes[i], 2 * x + y, c)
            out += [(w, w, (*chip, c)) for chip in chips]
        return out

    return plan


def _gather_d2d_plan(shard_shapes, kinds):
    n_arr = len(kinds)

    def plan(refs):
        x, y, c, chips = _place()
        out = []
        for i in range(n_arr):
            for chip in chips:
                w = _shard_window(refs[i], kinds[i], shard_shapes[i], 2 * chip[0] + chip[1], c)
                out.append((w, w, (x, y, 1 - c)))
        return out

    return plan


def _rs_pair_plan(kinds, shard_shapes):
    n_arr = len(kinds)

    def plan(refs):
        x, y, c, _ = _place()
        out = []
        for i in range(n_arr):
            for s in range(N_SHARD):
                out.append((_shard_window(refs[i], kinds[i], shard_shapes[i], s, 1 - c), refs[n_arr + i].at[s],
                            (x, y, 1 - c)))
        return out

    return plan


def _rs_share_plan(shard_shapes):
    def plan(refs):
        x, y, c, _ = _place()
        out = []
        for ref, (r, _) in zip(refs, shard_shapes):
            w = ref.at[pl.ds(_mo(c * (r // 2), 8), r // 2), :]
            out.append((w, w, (x, y, 1 - c)))
        return out

    return plan


def _rs_chip_plan(n_arr):
    def plan(refs):
        x, y, c, chips = _place()
        out = []
        for i in range(n_arr):
            for j, chip in enumerate(chips):
                out.append((refs[i].at[2 * chip[0] + chip[1]], refs[n_arr + i].at[j], (*chip, c)))
        return out

    return plan


def _rs_pair_add(name, grad, got, kind, shard_shape, c):
    r, cc = shard_shape
    hr = r // 2
    tr = hr
    for cand in (256, 128, 64, 32, 16):
        if hr % cand == 0:
            tr = cand
            break
    nb = hr // tr

    def body(c_ref, g_ref, t_ref, p_ref, pb_ref):
        p = g_ref[...] + t_ref[...]
        p_ref[...] = p
        pb_ref[...] = p.astype(BF16)

    out_shape = [jax.ShapeDtypeStruct((N_SHARD, hr, cc), F32), jax.ShapeDtypeStruct((N_SHARD, hr, cc), BF16)]
    if kind == "colw":
        tiles = cc // 128
        tr = hr
        g_spec = pl.BlockSpec((tr, 128), lambda s, t, cr: (cr[0], s * (tiles - 1) + t))
        t_spec = pl.BlockSpec((None, tr, 128), lambda s, t, cr: (s, 0, t))
        return pl.pallas_call(
            body, name=name,
            grid_spec=pltpu.PrefetchScalarGridSpec(
                num_scalar_prefetch=1, grid=(N_SHARD, tiles), in_specs=[g_spec, t_spec], out_specs=[t_spec, t_spec]),
            out_shape=out_shape,
        )(c, grad, got)
    if kind == "row":
        g_spec = pl.BlockSpec((tr, cc), lambda s, i, cr: (s * 2 * nb + cr[0] * nb + i, 0))
    elif kind == "col":
        g_spec = pl.BlockSpec((tr, cc), lambda s, i, cr: (cr[0] * nb + i, s))
    else:
        g_spec = pl.BlockSpec((None, tr, cc), lambda s, i, cr: (s, cr[0] * nb + i, 0))
    t_spec = pl.BlockSpec((None, tr, cc), lambda s, i, cr: (s, i, 0))
    return pl.pallas_call(
        body, name=name,
        grid_spec=pltpu.PrefetchScalarGridSpec(
            num_scalar_prefetch=1, grid=(N_SHARD, nb),
            in_specs=[g_spec, t_spec], out_specs=[t_spec, t_spec]),
        out_shape=out_shape,
    )(c, grad, got)


def _rs_chip_add(name, pair_f32, got, shard_shape, mine_c):
    r, cc = shard_shape
    hr = r // 2
    tr = hr
    for cand in (256, 128, 64, 32, 16):
        if hr % cand == 0:
            tr = cand
            break
    nb = hr // tr

    def body(mc_ref, p_ref, t0_ref, t1_ref, t2_ref, o_ref):
        o_ref[...] = (p_ref[...] + t1_ref[...].astype(F32)) + (t0_ref[...].astype(F32) + t2_ref[...].astype(F32))

    def far(j):
        return pl.BlockSpec((None, tr, cc), lambda i, mc: (j, i, 0))

    return pl.pallas_call(
        body, name=name,
        grid_spec=pltpu.PrefetchScalarGridSpec(
            num_scalar_prefetch=1, grid=(nb,),
            in_specs=[pl.BlockSpec((None, tr, cc), lambda i, mc: (mc[0], i, 0)), far(0), far(1), far(2)],
            out_specs=pl.BlockSpec((tr, cc), lambda i, mc: (mc[1] * nb + i, 0))),
        out_shape=jax.ShapeDtypeStruct((r, cc), F32),
    )(mine_c, pair_f32, got, got, got)


def _rs_pair_share(name, halves, shard_shapes):
    n_arr = len(halves)

    def body(*refs):
        ins = refs[:n_arr]
        outs = refs[n_arr:2 * n_arr]
        send_sems, recv_sems = refs[2 * n_arr:]
        x, y, c, _ = _place()
        sibling = (x, y, 1 - c)
        cps = []
        for i in range(n_arr):
            hr = shard_shapes[i][0] // 2
            rows = pl.ds(_mo(c * hr, 8), hr)
            cp = pltpu.make_async_remote_copy(
                src_ref=outs[i].at[rows, :], dst_ref=outs[i].at[rows, :],
                send_sem=send_sems.at[i], recv_sem=recv_sems.at[i],
                device_id=sibling, device_id_type=MESH)
            cp.start()
            cps.append(cp)
        for cp in cps:
            cp.wait()

    return pl.pallas_call(
        body, name=name,
        in_specs=[ANY] * n_arr, out_specs=[ANY] * n_arr,
        out_shape=[jax.ShapeDtypeStruct(s, F32) for s in shard_shapes],
        input_output_aliases={i: i for i in range(n_arr)},
        scratch_shapes=[pltpu.SemaphoreType.DMA((n_arr,)), pltpu.SemaphoreType.DMA((n_arr,))],
        compiler_params=pltpu.CompilerParams(has_side_effects=True),
    )(*halves)


def _pack(arrays):
    flat = []
    for a in arrays:
        v = a.reshape(-1).astype(F32)
        flat.append(jnp.pad(v, (0, (-v.shape[0]) % SMALL_COLS)))
    buf = jnp.concatenate(flat).reshape(-1, SMALL_COLS)
    return jnp.pad(buf, ((0, (-buf.shape[0]) % 16), (0, 0)))


def _unpack(buf, shapes):
    out = []
    row = 0
    for s in shapes:
        size = math.prod(s)
        nrow = -(-size // SMALL_COLS)
        out.append(buf[row:row + nrow].reshape(-1)[:size].reshape(s))
        row += nrow
    return out


def kernel(x, meta, norm_ab_w, w_in_ab, ret_norm_w, s5_lam_re, s5_lam_im, s5_log_dt, s5_b_re, s5_b_im, s5_c_re, s5_c_im, s5_d, s5_w_glu, w_out_ab, norm_c_w, w_in_c, gla_w_gate, gla_b_gate, gla_norm_w, w_out_c, final_norm_w, loss_target, m_meta, m_norm_ab_w, m_w_in_ab, m_ret_norm_w, m_s5_lam_re, m_s5_lam_im, m_s5_log_dt, m_s5_b_re, m_s5_b_im, m_s5_c_re, m_s5_c_im, m_s5_d, m_s5_w_glu, m_w_out_ab, m_norm_c_w, m_w_in_c, m_gla_w_gate, m_gla_b_gate, m_gla_norm_w, m_w_out_c, m_final_norm_w, v_meta, v_norm_ab_w, v_w_in_ab, v_ret_norm_w, v_s5_lam_re, v_s5_lam_im, v_s5_log_dt, v_s5_b_re, v_s5_b_im, v_s5_c_re, v_s5_c_im, v_s5_d, v_s5_w_glu, v_w_out_ab, v_norm_c_w, v_w_in_c, v_gla_w_gate, v_gla_b_gate, v_gla_norm_w, v_w_out_c, v_final_norm_w):
    seq = x.shape[1]
    rows = seq + CHUNK
    xi, yi, ci = lax.axis_index("x"), lax.axis_index("y"), lax.axis_index("c")
    mine = 2 * xi + yi
    c_arr = jnp.reshape(ci, (1,)).astype(jnp.int32)
    mine_c = jnp.stack([mine, ci]).astype(jnp.int32)

    mine_arr = jnp.reshape(mine, (1,)).astype(jnp.int32)
    small_shard = _pack([meta, norm_c_w, gla_norm_w, gla_b_gate, gla_w_gate[0]])
    small_all = _gather_small(small_shard)
    first_kinds = ["col"]
    first_shapes = [w_in_ab.shape[1:]]
    first_ici = _gather_ici_plan(first_shapes, first_kinds)
    first_d2d = _gather_d2d_plan(first_shapes, first_kinds)
    wab_buf, wab_own = _cast_place("place_w_in_ab", w_in_ab[0], "col", mine_arr, BF16, also_own=True)
    f_send, f_recv, f_bufs, small_all = _copies_start("gather_first_ici_start", [wab_buf], 3, first_ici, small_all)
    def late_group(items, kinds):
        shapes = [a.shape for _, a in items]
        bufs = [_cast_place("place_" + nm, a, kd, mine_arr, BF16) for (nm, a), kd in zip(items, kinds)]
        return bufs, _gather_ici_plan(shapes, kinds), _gather_d2d_plan(shapes, kinds), 3 * len(items)

    a_bufs, a_ici, a_d2d, n_a = late_group([("w_out_ab", w_out_ab[0]), ("w_glu", s5_w_glu[0])], ["row", "row"])
    b_bufs, b_ici, b_d2d, n_b = late_group([("w_in_c", w_in_c[0]), ("w_out_c", w_out_c[0])], ["stack", "row"])
    g_bufs = a_bufs + b_bufs
    cosf, sinf = _rope_tables(rows)
    rtab = _ret_tables()
    ab_re, ab_im, bb_re, bb_im = _s5_discretize(s5_lam_re[0], s5_lam_im[0], s5_log_dt[0], s5_b_re[0], s5_b_im[0])
    ab = (ab_re, ab_im)
    bd_b = (_bdiag_in(bb_re), _bdiag_in(bb_im))
    bd_c = (_bdiag_out(s5_c_re[0]), _bdiag_out(s5_c_im[0]))
    q4 = D_MODEL // N_SHARD
    g4 = GLA_QK // N_SHARD
    parts = [_unpack(small_all[j], [(N_META, q4), (1, q4), (1, q4), (1, g4), (GLA_RANK, g4)]) for j in range(N_SHARD)]
    meta_f, norm_c_f, gla_norm_f, bgate_f, wgate_f = [jnp.concatenate([p[i] for p in parts], axis=1) for i in range(5)]
    wgate_pad = jnp.pad(wgate_f, ((0, 128 - GLA_RANK), (0, 0)))

    h0, hn0 = _embed_norm(x[0], meta_f, norm_ab_w)

    tm = _row_tile(rows, 1408)
    tmk = _row_tile(rows, 1408)
    own_blocks = (IN_AB // N_SHARD) // 512
    shift_own = jnp.stack([jnp.zeros((), jnp.int32), mine.astype(jnp.int32) * own_blocks])
    shift_rest = jnp.stack([(mine.astype(jnp.int32) + 1) * own_blocks, (mine.astype(jnp.int32) + 1) * own_blocks])
    proj0 = _in_proj_shifted("in_proj_ab_own", hn0, wab_own, IN_AB // N_SHARD, shift_own, tm, 512, IN_AB)
    f_bufs = _copies_wait("gather_first_ici_wait", f_send, f_recv, f_bufs, first_ici,
                          [proj0, cosf, sinf, bd_b[0], bd_b[1], bd_c[0], bd_c[1]] + g_bufs + list(rtab))
    f_send, f_recv, f_bufs, cosf = _copies_start("gather_first_d2d_start", f_bufs, 3, first_d2d, cosf)
    wab, = _copies_wait("gather_first_d2d_wait", f_send, f_recv, f_bufs, first_d2d, cosf)
    a_send, a_recv, a_bufs, wab = _copies_start("gather_a_ici_start", a_bufs, n_a, a_ici, wab)
    b_send, b_recv, b_bufs, wab = _copies_start("gather_b_ici_start", b_bufs, n_b, b_ici, wab)
    proj0 = _in_proj_shifted("in_proj_ab_rest", hn0, wab, IN_AB - IN_AB // N_SHARD, shift_rest, tm, 512, IN_AB,
                             into=proj0)
    o_ret, o_a, ret_states = _ret_fwd(proj0, cosf, sinf, rtab, ret_norm_w)
    a_bufs = _copies_wait("gather_a_ici_wait", a_send, a_recv, a_bufs, a_ici, o_a)
    a_send, a_recv, a_bufs, proj0 = _copies_start("gather_a_d2d_start", a_bufs, n_a, a_d2d, proj0)
    y_s5, g_s5, s5_er, s5_ei = _s5_fwd(proj0, ab, bd_b, bd_c, s5_d)
    wout_ab, wglu = _copies_wait("gather_a_d2d_wait", a_send, a_recv, a_bufs, a_d2d, g_s5)
    zb_blk = (2 * RET_QK + 2 * RET_W + S5_W) // 512

    def glu_out(acc, gv, z):
        return gv.astype(F32) * _sigmoid(acc) * (z * _sigmoid(z))

    t_glu = _matmul("glu", g_s5, wglu, NN, rows, S5_W, S5_W, tm=tm, tn=512, tk=S5_W)
    o_b = _matmul("glu_out", g_s5, wglu, NN, rows, S5_W, S5_W, tm=tm, tn=512, tk=S5_W, out_dtype=BF16,
                  extras=[(g_s5, (tm, 512), lambda i, j, kk: (i, j)),
                          (proj0, (tm, 512), lambda i, j, kk: (i, zb_blk + j))],
                  epilogue=glu_out)
    b_bufs = _copies_wait("gather_b_ici_wait", b_send, b_recv, b_bufs, b_ici, o_b)
    b_send, b_recv, b_bufs, o_b = _copies_start("gather_b_d2d_start", b_bufs, n_b, b_d2d, o_b)
    h1 = _matmul("out_proj_ab", None, None, NN, rows, D_MODEL, OUT_AB, tm=tm, tn=512, tk=1024,
                 segs=[(o_a, (0, 0), wout_ab, (0, 0), RET_W, 1024),
                       (o_b, (0, 0), wout_ab, (RET_W // 1024, 0), S5_W, 1024)],
                 extras=[(h0, (tm, 512), lambda i, j, kk: (i, j))], epilogue=lambda acc, r: acc + r)
    wc_st, wout_c = _copies_wait("gather_b_d2d_wait", b_send, b_recv, b_bufs, b_d2d, h1)
    wc = jnp.concatenate([wc_st[j] for j in range(N_SHARD)] + [jnp.zeros((D_MODEL, IN_C_PAD - IN_C), BF16)], axis=1)

    hn1 = _rms_fwd("norm_c", h1, norm_c_f)
    proj1 = _matmul("in_proj_c", hn1, wc, NN, rows, IN_C_PAD, D_MODEL, tm=tm, tn=896, tk=D_MODEL)
    o_gla, o_c, gla_states = _gla_fwd(proj1, wgate_pad, bgate_f, gla_norm_f)
    h2 = _matmul("out_proj_c", o_c, wout_c, NN, rows, D_MODEL, GLA_W, tm=tm, tn=512, tk=GLA_W,
                 extras=[(h1, (tm, 512), lambda i, j, kk: (i, j))], epilogue=lambda acc, r: acc + r)
    loss_dev, dh2, d_final = _final_loss(h2, final_norm_w.reshape(1, D_MODEL), loss_target[0])

    g_wout_c = _matmul("d_w_out_c", o_c, dh2, TN, GLA_W, D_MODEL, rows, tm=1024, tn=1024, tk=tmk)
    d_oc = _matmul("d_o_c", dh2, wout_c, NT, rows, GLA_W, D_MODEL, tm=tm, tn=512, tk=1024)
    dq1, dk1, dv1, dz1, dlogit, d_gla_norm, d_bgate = _gla_bwd(proj1, wgate_pad, bgate_f, gla_norm_f, o_gla, d_oc, gla_states)
    gl_blk = (2 * GLA_QK + 2 * GLA_W) // 128
    dgl = _matmul("d_g_low", dlogit, wgate_pad, NT, rows, 128, GLA_QK, tm=tm, tn=128, tk=GLA_QK, out_dtype=BF16)
    g_wgate = _matmul("d_w_gate", proj1, dlogit, TN, 128, GLA_QK, rows, tm=128, tn=GLA_QK, tk=tmk, a_off=(0, gl_blk))
    dproj1 = jnp.concatenate([dq1, dk1, dv1, dz1, dgl], axis=1)
    g_wc = _matmul("d_w_in_c", hn1, dproj1, TN, D_MODEL, IN_C_PAD, rows, tm=1024, tn=896, tk=tmk)
    dhn1 = _matmul("d_hn1", dproj1, wc, NT, rows, D_MODEL, IN_C_PAD, tm=tm, tn=512, tk=896)
    dh1, d_norm_c = _rms_bwd("norm_c_bwd", dhn1, h1, norm_c_f, dh2)

    g_wout_ab = _matmul("d_w_out_ab_a", o_a, dh1, TN, RET_W, D_MODEL, rows, tm=1024, tn=1024, tk=tmk,
                        out_shape=jax.ShapeDtypeStruct((OUT_AB, D_MODEL), F32))
    g_wout_ab = _matmul("d_w_out_ab_b", o_b, dh1, TN, S5_W, D_MODEL, rows, tm=1024, tn=1024, tk=tmk,
                        into=(g_wout_ab, RET_W // 1024, 0))
    dmix = _matmul("d_mix", dh1, wout_ab, NT, rows, OUT_AB, D_MODEL, tm=tm, tn=512, tk=1024)
    dproj0, d_ret_norm = _ret_bwd(proj0, cosf, sinf, rtab, ret_norm_w, o_ret, dmix, ret_states)
    dproj0, dt_glu, dg_direct = _s5_gate_bwd(dmix, g_s5, t_glu, proj0, dproj0)
    g_wglu = _matmul("d_w_glu", g_s5, dt_glu, TN, S5_W, S5_W, rows, tm=1024, tn=1024, tk=tmk)
    dy_s5 = _matmul("d_y_s5", dt_glu, wglu, NT, rows, S5_W, S5_W, tm=tm, tn=512, tk=S5_W,
                    extras=[(dg_direct, (tm, 512), lambda i, j, kk: (i, j)),
                            (y_s5, (tm, 512), lambda i, j, kk: (i, j))],
                    epilogue=lambda acc, dg, yv: (acc + dg) * _gelu_grad(yv))
    wc_cols = IN_C // N_SHARD
    wc_win = (wc_cols // 128 + 1) * 128
    rs1_names = ["w_out_ab", "w_in_c", "w_out_c", "w_glu"]
    rs1_kinds = ["row", "colw", "row", "row"]
    rs1_shapes = [w_out_ab.shape[1:], (D_MODEL, wc_win), w_out_c.shape[1:], s5_w_glu.shape[1:]]
    rs1_plan = _rs_pair_plan(rs1_kinds, rs1_shapes)
    rs1_land = [_empty_hbm((N_SHARD, r // 2, cc), F32) for (r, cc) in rs1_shapes]
    p_send, p_recv, p_bufs, dy_s5 = _copies_start("rs1_pair_start", [g_wout_ab, g_wc, g_wout_c, g_wglu] + rs1_land,
                                                  N_SHARD * 4, rs1_plan, dy_s5)
    dproj0, dbr_d, dbi_d, dcr_d, dci_d, dar_p, dai_p, dd_p = _s5_bwd(proj0, dy_s5, ab, bd_b, bd_c, s5_d,
                                                                     (s5_er, s5_ei), dproj0)
    p_bufs = _copies_wait("rs1_pair_wait", p_send, p_recv, p_bufs, rs1_plan, dproj0)
    rs1_pairs = [_rs_pair_add("rs_pair_add_" + nm, g, t, kd, ss, c_arr)
                 for nm, g, t, kd, ss in zip(rs1_names, p_bufs[:4], p_bufs[4:], rs1_kinds, rs1_shapes)]
    rs1_chip_plan = _rs_chip_plan(4)
    rs1_land2 = [_empty_hbm((3, r // 2, cc), BF16) for (r, cc) in rs1_shapes]
    c_send, c_recv, c_bufs, dproj0 = _copies_start("rs1_chip_start", [p[1] for p in rs1_pairs] + rs1_land2, 12,
                                                   rs1_chip_plan, dproj0)
    g_wab = _matmul("d_w_in_ab", hn0, dproj0, TN, D_MODEL, IN_AB, rows, tm=1024, tn=1024, tk=tmk)
    rs2_shapes = [w_in_ab.shape[1:]]
    rs2_plan = _rs_pair_plan(["col"], rs2_shapes)
    rs2_land = [_empty_hbm((N_SHARD, rs2_shapes[0][0] // 2, rs2_shapes[0][1]), F32)]
    q_send, q_recv, q_bufs, dproj0 = _copies_start("rs2_pair_start", [g_wab] + rs2_land, N_SHARD, rs2_plan, dproj0)
    dhn0 = _matmul("d_hn0_a", dproj0, wab, NT, tm, D_MODEL, IN_AB, tm=tm, tn=512, tk=2048,
                   out_shape=jax.ShapeDtypeStruct((rows, D_MODEL), F32))
    q_bufs = _copies_wait("rs2_pair_wait", q_send, q_recv, q_bufs, rs2_plan, dhn0)
    rs2_pair = _rs_pair_add("rs_pair_add_w_in_ab", q_bufs[0], q_bufs[1], "col", rs2_shapes[0], c_arr)
    rs2_chip_plan = _rs_chip_plan(1)
    rs2_land2 = [_empty_hbm((3, rs2_shapes[0][0] // 2, rs2_shapes[0][1]), BF16)]
    r_send, r_recv, r_bufs, dhn0 = _copies_start("rs2_chip_start", [rs2_pair[1]] + rs2_land2, 3, rs2_chip_plan, dhn0)
    if rows > tm:
        dhn0 = _matmul("d_hn0_b", dproj0, wab, NT, rows - tm, D_MODEL, IN_AB, tm=tm, tn=512, tk=2048, a_off=(1, 0),
                       into=(dhn0, 1, 0))
    grad_x, d_meta, d_norm_ab = _rms_bwd_embed(dhn0, h0, norm_ab_w, dh1)
    c_bufs = _copies_wait("rs1_chip_wait", c_send, c_recv, c_bufs, rs1_chip_plan, grad_x)
    grad_x = grad_x[None]
    rs1_halves = [_rs_chip_add("rs_chip_add_" + nm, p[0], t, ss, mine_c)
                  for nm, p, t, ss in zip(rs1_names, rs1_pairs, c_bufs[4:], rs1_shapes)]
    share_plan = _rs_share_plan(rs1_shapes)
    s_send, s_recv, s_bufs, d_meta = _copies_start("rs1_share_start", rs1_halves, len(rs1_halves), share_plan, d_meta)

    d_ab_re = jnp.sum(dar_p, axis=1).reshape(S5_G, S5_P)
    d_ab_im = jnp.sum(dai_p, axis=1).reshape(S5_G, S5_P)
    small_local = [loss_dev, d_meta, d_norm_ab, d_ret_norm.reshape(1, RET_W), d_ab_re, d_ab_im,
                   _bdiag_in_extract(dbr_d), _bdiag_in_extract(dbi_d),
                   _bdiag_out_extract(dcr_d), _bdiag_out_extract(dci_d),
                   jnp.sum(dd_p, axis=1).reshape(1, S5_W), d_norm_c, g_wgate[:GLA_RANK],
                   d_bgate.reshape(1, GLA_QK), d_gla_norm.reshape(1, GLA_W), d_final]
    small_shapes = [a.shape for a in small_local]
    summed_buf = _allreduce_small(_pack(small_local))
    summed = _unpack(summed_buf, small_shapes)
    g_w_out_ab, g_w_in_c, g_w_out_c, g_w_glu = _copies_wait("rs1_share_wait", s_send, s_recv, s_bufs, share_plan,
                                                             summed_buf)
    g_w_in_c = lax.dynamic_slice(g_w_in_c, (0, (wc_cols % 128) * mine), (D_MODEL, wc_cols))
    (loss, g_meta_f, g_norm_ab, g_ret_norm, g_ab_re, g_ab_im, g_bb_re, g_bb_im, g_c_re, g_c_im, g_d,
     g_norm_c_f, g_wgate_f, g_bgate_f, g_gla_norm_f, g_final) = summed
    _, s5_vjp = jax.vjp(_s5_discretize, s5_lam_re[0], s5_lam_im[0], s5_log_dt[0], s5_b_re[0], s5_b_im[0])
    g_lam_re, g_lam_im, g_log_dt, g_b_re, g_b_im = s5_vjp((g_ab_re, g_ab_im, g_bb_re, g_bb_im))

    def take(a, width):
        return lax.dynamic_slice_in_dim(a, mine * width, width, axis=1)

    grads = {
        "meta": take(g_meta_f, q4), "norm_ab_w": g_norm_ab, "ret_norm_w": g_ret_norm,
        "s5_lam_re": g_lam_re[None], "s5_lam_im": g_lam_im[None], "s5_log_dt": g_log_dt[None],
        "s5_b_re": g_b_re[None], "s5_b_im": g_b_im[None], "s5_c_re": g_c_re[None], "s5_c_im": g_c_im[None],
        "s5_d": g_d, "s5_w_glu": g_w_glu[None], "w_out_ab": g_w_out_ab[None], "norm_c_w": take(g_norm_c_f, q4),
        "w_in_c": g_w_in_c[None], "gla_w_gate": take(g_wgate_f, g4)[None], "gla_b_gate": take(g_bgate_f, g4),
        "gla_norm_w": take(g_gla_norm_f, q4), "w_out_c": g_w_out_c[None], "final_norm_w": g_final.reshape(D_MODEL),
    }
    weights = dict(meta=meta, norm_ab_w=norm_ab_w, w_in_ab=w_in_ab, ret_norm_w=ret_norm_w, s5_lam_re=s5_lam_re,
                   s5_lam_im=s5_lam_im, s5_log_dt=s5_log_dt, s5_b_re=s5_b_re, s5_b_im=s5_b_im, s5_c_re=s5_c_re,
                   s5_c_im=s5_c_im, s5_d=s5_d, s5_w_glu=s5_w_glu, w_out_ab=w_out_ab, norm_c_w=norm_c_w,
                   w_in_c=w_in_c, gla_w_gate=gla_w_gate, gla_b_gate=gla_b_gate, gla_norm_w=gla_norm_w,
                   w_out_c=w_out_c, final_norm_w=final_norm_w)
    m_in = dict(meta=m_meta, norm_ab_w=m_norm_ab_w, w_in_ab=m_w_in_ab, ret_norm_w=m_ret_norm_w,
                s5_lam_re=m_s5_lam_re, s5_lam_im=m_s5_lam_im, s5_log_dt=m_s5_log_dt, s5_b_re=m_s5_b_re,
                s5_b_im=m_s5_b_im, s5_c_re=m_s5_c_re, s5_c_im=m_s5_c_im, s5_d=m_s5_d, s5_w_glu=m_s5_w_glu,
                w_out_ab=m_w_out_ab, norm_c_w=m_norm_c_w, w_in_c=m_w_in_c, gla_w_gate=m_gla_w_gate,
                gla_b_gate=m_gla_b_gate, gla_norm_w=m_gla_norm_w, w_out_c=m_w_out_c, final_norm_w=m_final_norm_w)
    v_in = dict(meta=v_meta, norm_ab_w=v_norm_ab_w, w_in_ab=v_w_in_ab, ret_norm_w=v_ret_norm_w,
                s5_lam_re=v_s5_lam_re, s5_lam_im=v_s5_lam_im, s5_log_dt=v_s5_log_dt, s5_b_re=v_s5_b_re,
                s5_b_im=v_s5_b_im, s5_c_re=v_s5_c_re, s5_c_im=v_s5_c_im, s5_d=v_s5_d, s5_w_glu=v_s5_w_glu,
                w_out_ab=v_w_out_ab, norm_c_w=v_norm_c_w, w_in_c=v_w_in_c, gla_w_gate=v_gla_w_gate,
                gla_b_gate=v_gla_b_gate, gla_norm_w=v_gla_norm_w, w_out_c=v_w_out_c, final_norm_w=v_final_norm_w)
    order = list(weights)
    big_names = ["s5_w_glu", "w_out_ab", "w_in_c", "w_out_c", "w_in_ab"]
    small_names = [nm for nm in order if nm not in big_names]
    delta, new_m, new_v = {}, {}, {}

    def big_update(nm):
        shp = weights[nm].shape
        d2, m2, v2 = _adamw("adamw_" + nm, weights[nm][0], grads[nm][0], m_in[nm][0], v_in[nm][0])
        delta[nm], new_m[nm], new_v[nm] = d2.reshape(shp), m2.reshape(shp), v2.reshape(shp)

    for nm in big_names[:-1]:
        big_update(nm)
    sshapes = [weights[nm].shape for nm in small_names]
    d2, m2, v2 = _adamw("adamw_small", _pack([weights[nm] for nm in small_names]),
                        _pack([grads[nm] for nm in small_names]), _pack([m_in[nm] for nm in small_names]),
                        _pack([v_in[nm] for nm in small_names]))
    for nm, dd, mm, vv in zip(small_names, _unpack(d2, sshapes), _unpack(m2, sshapes), _unpack(v2, sshapes)):
        delta[nm], new_m[nm], new_v[nm] = dd, mm, vv
    r_bufs = _copies_wait("rs2_chip_wait", r_send, r_recv, r_bufs, rs2_chip_plan,
                          [v2] + [new_v[nm] for nm in big_names[:-1]])
    rs2_half = _rs_chip_add("rs_chip_add_w_in_ab", rs2_pair[0], r_bufs[1], rs2_shapes[0], mine_c)
    grads["w_in_ab"] = _rs_pair_share("rs2_pair_share", [rs2_half], rs2_shapes)[0][None]
    big_update("w_in_ab")
    grads = {nm: grads[nm].reshape(weights[nm].shape) for nm in order}
    return (loss.reshape(()), grad_x, *[grads[nm] for nm in order], *[delta[nm] for nm in order],
            *[new_m[nm] for nm in order], *[new_v[nm] for nm in order])
```

```python
import math

import jax
import jax.numpy as jnp
from jax import lax
from jax.experimental import pallas as pl
from jax.experimental.pallas import tpu as pltpu

F32 = jnp.float32
BF16 = jnp.bfloat16
MESH = pl.DeviceIdType.MESH

D_MODEL = 2048
N_META = 16
CHUNK = 128
SUB = 16
NSUB = CHUNK // SUB
PAD = CHUNK - N_META
EPS = 1e-6

RET_HEADS = 8
RET_DK = 128
RET_DV = 256
RET_QK = RET_HEADS * RET_DK
RET_W = RET_HEADS * RET_DV
ROPE_BASE = 10000.0

S5_W = 1024
S5_GH = 16
S5_G = S5_W // S5_GH
S5_P = 64
S5_TG = 8
S5_NT = S5_G // S5_TG
S5_TU = S5_TG * S5_GH
S5_TS = S5_TG * S5_P
S5_FWD_TILES = 2
S5_BWD_TILES = 1

GLA_HEADS = 4
GLA_DK = 256
GLA_DV = 512
GLA_QK = GLA_HEADS * GLA_DK
GLA_W = GLA_HEADS * GLA_DV
GLA_RANK = 16
GLA_TAU = 16.0

IN_AB = 2 * RET_QK + 2 * RET_W + 2 * S5_W
OUT_AB = RET_W + S5_W
IN_C = 2 * GLA_QK + 2 * GLA_W + GLA_RANK
IN_C_PAD = 2 * GLA_QK + 2 * GLA_W + 128

ADAM_LR = 0.001
ADAM_B1 = 0.9
ADAM_B2 = 0.999
ADAM_EPS = 1e-08
ADAM_WD = 0.01
ADAM_STEP = 10

N_SHARD = 4
SMALL_COLS = 512

NN = (((1,), (0,)), ((), ()))
NT = (((1,), (1,)), ((), ()))
TN = (((0,), (0,)), ((), ()))


def _dot(a, b, dims=NN):
    return lax.dot_general(a.astype(BF16), b.astype(BF16), dims, preferred_element_type=F32)


def _mo(v, m):
    return v if isinstance(v, int) else pl.multiple_of(v, m)


def _sigmoid(x):
    return 1.0 / (1.0 + jnp.exp(-x))


def _row_tile(rows, cap):
    n = rows // CHUNK
    best = 1
    for d in range(1, n + 1):
        if n % d == 0 and d * CHUNK <= cap:
            best = d
    return best * CHUNK


def _matmul(name, a, b, dims, m, n, k, *, tm, tn, tk, out_dtype=F32, a_off=(0, 0), b_off=(0, 0),
            extras=(), epilogue=None, out_shape=None, out_spec=None, segs=None, into=None):
    if segs is None:
        segs = [(a, a_off, b, b_off, k, tk)]
    assert m % tm == 0 and n % tn == 0, (name, m, n, tm, tn)
    starts, counts = [], []
    nk = 0
    for (_, _, _, _, ks, tks) in segs:
        assert ks % tks == 0, (name, ks, tks)
        starts.append(nk)
        counts.append(ks // tks)
        nk += ks // tks
    in_specs, operands = [], []
    for s, (sa, (ar, ac), sb, (br, bc), _, tks) in enumerate(segs):
        def kpos(kk, st=starts[s], cnt=counts[s]):
            return jnp.clip(kk - st, 0, cnt - 1) if len(segs) > 1 else kk

        if dims == NN:
            a_spec = pl.BlockSpec((tm, tks), lambda i, j, kk, p=kpos, r=ar, c=ac: (i + r, p(kk) + c))
            b_spec = pl.BlockSpec((tks, tn), lambda i, j, kk, p=kpos, r=br, c=bc: (p(kk) + r, j + c))
        elif dims == NT:
            a_spec = pl.BlockSpec((tm, tks), lambda i, j, kk, p=kpos, r=ar, c=ac: (i + r, p(kk) + c))
            b_spec = pl.BlockSpec((tn, tks), lambda i, j, kk, p=kpos, r=br, c=bc: (j + r, p(kk) + c))
        else:
            a_spec = pl.BlockSpec((tks, tm), lambda i, j, kk, p=kpos, r=ar, c=ac: (p(kk) + r, i + c))
            b_spec = pl.BlockSpec((tks, tn), lambda i, j, kk, p=kpos, r=br, c=bc: (p(kk) + r, j + c))
        in_specs += [a_spec, b_spec]
        operands += [sa, sb]
    n_seg = len(segs)
    n_extra = len(extras)
    if out_shape is None:
        out_shape = jax.ShapeDtypeStruct((m, n), out_dtype)

    def body(*refs):
        e_refs = refs[2 * n_seg:2 * n_seg + n_extra]
        n_in = 2 * n_seg + n_extra + (1 if into is not None else 0)
        o_ref = refs[n_in]
        if nk == 1:
            part = _dot(refs[0][...], refs[1][...], dims)
            if epilogue is not None:
                part = epilogue(part, *[e[...] for e in e_refs])
            o_ref[...] = part.astype(o_ref.dtype)
            return
        acc_ref = refs[n_in + 1]
        kk = pl.program_id(2)

        @pl.when(kk == 0)
        def _():
            acc_ref[...] = jnp.zeros_like(acc_ref)

        if n_seg == 1:
            acc_ref[...] += _dot(refs[0][...], refs[1][...], dims)
        else:
            for s in range(n_seg):
                @pl.when(jnp.logical_and(kk >= starts[s], kk < starts[s] + counts[s]))
                def _(s=s):
                    acc_ref[...] += _dot(refs[2 * s][...], refs[2 * s + 1][...], dims)

        @pl.when(kk == nk - 1)
        def _():
            acc = acc_ref[...]
            if epilogue is not None:
                acc = epilogue(acc, *[e[...] for e in e_refs])
            o_ref[...] = acc.astype(o_ref.dtype)

    if out_spec is None:
        out_spec = pl.BlockSpec((tm, tn), lambda i, j, kk: (i, j))
    in_specs += [pl.BlockSpec(bs, im) for (_, bs, im) in extras]
    operands += [e for (e, _, _) in extras]
    aliases = {}
    if into is not None:
        dest, ro, co = into
        out_shape = jax.ShapeDtypeStruct(dest.shape, dest.dtype)
        out_spec = pl.BlockSpec((tm, tn), lambda i, j, kk: (i + ro, j + co))
        aliases = {len(operands): 0}
        in_specs.append(ANY)
        operands.append(dest)
    return pl.pallas_call(
        body, name=name, grid=(m // tm, n // tn, nk),
        in_specs=in_specs, out_specs=out_spec, out_shape=out_shape, input_output_aliases=aliases,
        scratch_shapes=[] if nk == 1 else [pltpu.VMEM((tm, tn), F32)],
        compiler_params=pltpu.CompilerParams(dimension_semantics=("parallel", "parallel", "arbitrary")),
    )(*operands)


def _rms_fwd(name, h, w):
    rows, d = h.shape
    tm = _row_tile(rows, 512)

    def body(h_ref, w_ref, o_ref):
        x = h_ref[...]
        r = lax.rsqrt(jnp.mean(x * x, axis=-1, keepdims=True) + EPS)
        o_ref[...] = (x * r * w_ref[...]).astype(BF16)

    return pl.pallas_call(
        body, name=name, grid=(rows // tm,),
        in_specs=[pl.BlockSpec((tm, d), lambda i: (i, 0)), pl.BlockSpec((1, d), lambda i: (0, 0))],
        out_specs=pl.BlockSpec((tm, d), lambda i: (i, 0)),
        out_shape=jax.ShapeDtypeStruct((rows, d), BF16),
    )(h, w)


def _rms_bwd(name, dhn, h, w, dres):
    rows, d = h.shape
    tm = _row_tile(rows, 384)

    def body(g_ref, h_ref, w_ref, r_ref, dh_ref, dw_ref):
        i = pl.program_id(0)
        x = h_ref[...]
        r = lax.rsqrt(jnp.mean(x * x, axis=-1, keepdims=True) + EPS)
        xh = x * r
        g = g_ref[...]
        gw = g * w_ref[...]
        dh_ref[...] = r_ref[...] + r * (gw - xh * jnp.mean(gw * xh, axis=-1, keepdims=True))

        @pl.when(i == 0)
        def _():
            dw_ref[...] = jnp.zeros_like(dw_ref)

        dw_ref[...] += jnp.sum(g * xh, axis=0, keepdims=True)

    return pl.pallas_call(
        body, name=name, grid=(rows // tm,),
        in_specs=[pl.BlockSpec((tm, d), lambda i: (i, 0)), pl.BlockSpec((tm, d), lambda i: (i, 0)),
                  pl.BlockSpec((1, d), lambda i: (0, 0)), pl.BlockSpec((tm, d), lambda i: (i, 0))],
        out_specs=[pl.BlockSpec((tm, d), lambda i: (i, 0)), pl.BlockSpec((1, d), lambda i: (0, 0))],
        out_shape=[jax.ShapeDtypeStruct((rows, d), F32), jax.ShapeDtypeStruct((1, d), F32)],
    )(dhn, h, w, dres)


def _embed_norm(x, meta, w):
    seq, d = x.shape
    rows = seq + CHUNK

    def body(x_ref, m_ref, w_ref, h_ref, o_ref):
        i = pl.program_id(0)

        def emit(h):
            h_ref[...] = h
            r = lax.rsqrt(jnp.mean(h * h, axis=-1, keepdims=True) + EPS)
            o_ref[...] = (h * r * w_ref[...]).astype(BF16)

        @pl.when(i == 0)
        def _():
            emit(jnp.concatenate([jnp.zeros((PAD, d), F32), m_ref[...]], axis=0))

        @pl.when(i > 0)
        def _():
            emit(x_ref[...])

    blk = pl.BlockSpec((CHUNK, d), lambda i: (i, 0))
    return pl.pallas_call(
        body, name="embed_norm_ab", grid=(rows // CHUNK,),
        in_specs=[pl.BlockSpec((CHUNK, d), lambda i: (jnp.maximum(i - 1, 0), 0)),
                  pl.BlockSpec((N_META, d), lambda i: (0, 0)), pl.BlockSpec((1, d), lambda i: (0, 0))],
        out_specs=[blk, blk],
        out_shape=[jax.ShapeDtypeStruct((rows, d), F32), jax.ShapeDtypeStruct((rows, d), BF16)],
    )(x, meta, w)


def _rms_bwd_embed(dhn, h, w, dres):
    rows, d = h.shape
    seq = rows - CHUNK

    def body(g_ref, h_ref, w_ref, r_ref, gx_ref, gm_ref, dw_ref):
        i = pl.program_id(0)
        x = h_ref[...]
        r = lax.rsqrt(jnp.mean(x * x, axis=-1, keepdims=True) + EPS)
        xh = x * r
        g = g_ref[...]
        gw = g * w_ref[...]
        dh = r_ref[...] + r * (gw - xh * jnp.mean(gw * xh, axis=-1, keepdims=True))

        @pl.when(i == 0)
        def _():
            dw_ref[...] = jnp.zeros_like(dw_ref)
            gm_ref[...] = dh[PAD:]

        @pl.when(i > 0)
        def _():
            gx_ref[...] = dh

        dw_ref[...] += jnp.sum(g * xh, axis=0, keepdims=True)

    blk = pl.BlockSpec((CHUNK, d), lambda i: (i, 0))
    return pl.pallas_call(
        body, name="norm_ab_bwd", grid=(rows // CHUNK,),
        in_specs=[blk, blk, pl.BlockSpec((1, d), lambda i: (0, 0)), blk],
        out_specs=[pl.BlockSpec((CHUNK, d), lambda i: (jnp.maximum(i - 1, 0), 0)),
                   pl.BlockSpec((N_META, d), lambda i: (0, 0)), pl.BlockSpec((1, d), lambda i: (0, 0))],
        out_shape=[jax.ShapeDtypeStruct((seq, d), F32), jax.ShapeDtypeStruct((N_META, d), F32),
                   jax.ShapeDtypeStruct((1, d), F32)],
    )(dhn, h, w, dres)


def _final_loss(h2, w, target):
    rows, d = h2.shape

    def body(h_ref, w_ref, t_ref, loss_ref, dh_ref, dw_ref):
        i = pl.program_id(0)

        @pl.when(i == 0)
        def _():
            loss_ref[...] = jnp.zeros_like(loss_ref)
            dw_ref[...] = jnp.zeros_like(dw_ref)
            dh_ref[...] = jnp.zeros_like(dh_ref)

        @pl.when(i > 0)
        def _():
            x = h_ref[...]
            r = lax.rsqrt(jnp.mean(x * x, axis=-1, keepdims=True) + EPS)
            xh = x * r
            wv = w_ref[...]
            err = xh * wv - t_ref[...]
            loss_ref[...] += 0.5 * jnp.sum(jnp.mean(err * err, axis=-1, keepdims=True), axis=0, keepdims=True)
            g = err * (1.0 / d)
            gw = g * wv
            dh_ref[...] = r * (gw - xh * jnp.mean(gw * xh, axis=-1, keepdims=True))
            dw_ref[...] += jnp.sum(g * xh, axis=0, keepdims=True)

    return pl.pallas_call(
        body, name="final_loss", grid=(rows // CHUNK,),
        in_specs=[pl.BlockSpec((CHUNK, d), lambda i: (i, 0)), pl.BlockSpec((1, d), lambda i: (0, 0)),
                  pl.BlockSpec((CHUNK, d), lambda i: (jnp.maximum(i - 1, 0), 0))],
        out_specs=[pl.BlockSpec((1, 1), lambda i: (0, 0)), pl.BlockSpec((CHUNK, d), lambda i: (i, 0)),
                   pl.BlockSpec((1, d), lambda i: (0, 0))],
        out_shape=[jax.ShapeDtypeStruct((1, 1), F32), jax.ShapeDtypeStruct((rows, d), F32),
                   jax.ShapeDtypeStruct((1, d), F32)],
    )(h2, w, target)


def _gate_fwd(o, z, w):
    rs = lax.rsqrt(jnp.mean(o * o, axis=-1, keepdims=True) + EPS)
    return o * rs * w * (z * _sigmoid(z))


def _gate_bwd(dout, o, z, w):
    rs = lax.rsqrt(jnp.mean(o * o, axis=-1, keepdims=True) + EPS)
    yn = o * rs
    sg = _sigmoid(z)
    sil = z * sg
    dsil = sg * (1.0 + z * (1.0 - sg))
    dz = dout * yn * w * dsil
    dyn = dout * w * sil
    dw = jnp.sum(dout * yn * sil, axis=0, keepdims=True)
    do = rs * (dyn - yn * jnp.mean(dyn * yn, axis=-1, keepdims=True))
    return do, dz, dw


def _rope(t, cosf, sinf):
    return t * cosf + pltpu.roll(t, RET_DK // 2, 1) * sinf


def _rope_t(d, cosf, sinf):
    return d * cosf + pltpu.roll(d * sinf, RET_DK // 2, 1)


def _ret_tables():
    log_g = jnp.log1p(-jnp.exp2(-5.0 - jnp.arange(RET_HEADS, dtype=F32)))
    idx = jnp.arange(CHUNK, dtype=F32)
    diff = idx[:, None] - idx[None, :]
    decay = jnp.where(diff >= 0, jnp.exp(log_g[:, None, None] * jnp.maximum(diff, 0.0)), 0.0)
    kw = jnp.exp(log_g[:, None] * (CHUNK - 1 - idx))
    qw = jnp.exp(log_g[:, None] * (idx + 1.0))
    gch = jnp.exp(log_g * CHUNK)
    kw = jnp.broadcast_to(kw[:, :, None], (RET_HEADS, CHUNK, RET_DK))
    qw = jnp.broadcast_to(qw[:, :, None], (RET_HEADS, CHUNK, RET_DK))
    gch = jnp.broadcast_to(gch[:, None, None], (RET_HEADS, 1, RET_DV))
    return decay, kw, qw, gch


def _rope_tables(rows):
    pos = jnp.arange(rows, dtype=F32) - float(PAD)
    inv_freq = jnp.power(ROPE_BASE, -jnp.arange(0, RET_DK, 2, dtype=F32) / RET_DK)
    ang = pos[:, None] * inv_freq[None, :]
    cos, sin = jnp.cos(ang), jnp.sin(ang)
    return jnp.concatenate([cos, cos], axis=1), jnp.concatenate([-sin, sin], axis=1)


RET_HB = 8
RET_QB = RET_HB * RET_DK
RET_VB = RET_HB * RET_DV


def _ret_in_specs(rev, nc):
    def cn(n):
        return (nc - 1 - n) if rev else n
    kb = RET_QK // RET_QB
    vb = 2 * RET_QK // RET_VB
    zb = (2 * RET_QK + RET_W) // RET_VB
    return [
        pl.BlockSpec((CHUNK, RET_QB), lambda h, n: (cn(n), h)),
        pl.BlockSpec((CHUNK, RET_QB), lambda h, n: (cn(n), kb + h)),
        pl.BlockSpec((CHUNK, RET_VB), lambda h, n: (cn(n), vb + h)),
        pl.BlockSpec((CHUNK, RET_VB), lambda h, n: (cn(n), zb + h)),
        pl.BlockSpec((CHUNK, RET_DK), lambda h, n: (cn(n), 0)),
        pl.BlockSpec((CHUNK, RET_DK), lambda h, n: (cn(n), 0)),
        pl.BlockSpec((RET_HB, CHUNK, CHUNK), lambda h, n: (h, 0, 0)),
        pl.BlockSpec((RET_HB, CHUNK, RET_DK), lambda h, n: (h, 0, 0)),
        pl.BlockSpec((RET_HB, CHUNK, RET_DK), lambda h, n: (h, 0, 0)),
        pl.BlockSpec((RET_HB, 1, RET_DV), lambda h, n: (h, 0, 0)),
        pl.BlockSpec((1, RET_VB), lambda h, n: (0, h)),
    ]


def _ret_fwd(proj, cosf, sinf, tables, normw):
    rows = proj.shape[0]
    nc = rows // CHUNK
    decay, kw, qw, gch = tables

    def body(q_ref, k_ref, v_ref, z_ref, cos_ref, sin_ref, dm_ref, kw_ref, qw_ref, g_ref, w_ref,
             o_ref, oa_ref, st_ref, s_scr):
        n = pl.program_id(1)

        @pl.when(n == 0)
        def _():
            s_scr[...] = jnp.zeros_like(s_scr)

        cosv, sinv = cos_ref[...], sin_ref[...]
        for hh in range(RET_HB):
            qc = slice(hh * RET_DK, (hh + 1) * RET_DK)
            vc = slice(hh * RET_DV, (hh + 1) * RET_DV)
            q = _rope(q_ref[:, qc], cosv, sinv)
            k = _rope(k_ref[:, qc], cosv, sinv) * (RET_DK ** -0.5)
            v = v_ref[:, vc]
            s = s_scr[hh]
            st_ref[hh, 0] = s.astype(BF16)
            a = _dot(q, k, NT) * dm_ref[hh]
            o = _dot(a, v) + _dot(q * qw_ref[hh], s)
            s_scr[hh] = s * g_ref[hh] + _dot(k * kw_ref[hh], v, TN)
            o_ref[:, vc] = o
            oa_ref[:, vc] = _gate_fwd(o, z_ref[:, vc], w_ref[:, vc]).astype(BF16)

    return pl.pallas_call(
        body, name="ret_fwd", grid=(RET_HEADS // RET_HB, nc),
        in_specs=_ret_in_specs(False, nc),
        out_specs=[pl.BlockSpec((CHUNK, RET_VB), lambda h, n: (n, h)),
                   pl.BlockSpec((CHUNK, RET_VB), lambda h, n: (n, h)),
                   pl.BlockSpec((RET_HB, 1, RET_DK, RET_DV), lambda h, n: (h, n, 0, 0))],
        out_shape=[jax.ShapeDtypeStruct((rows, RET_W), F32), jax.ShapeDtypeStruct((rows, RET_W), BF16),
                   jax.ShapeDtypeStruct((RET_HEADS, nc, RET_DK, RET_DV), BF16)],
        scratch_shapes=[pltpu.VMEM((RET_HB, RET_DK, RET_DV), F32)],
        compiler_params=pltpu.CompilerParams(dimension_semantics=("parallel", "arbitrary")),
    )(proj, proj, proj, proj, cosf, sinf, decay, kw, qw, gch, normw)


def _ret_bwd(proj, cosf, sinf, tables, normw, o_ret, dmix, states):
    assert RET_HB == RET_HEADS
    rows = proj.shape[0]
    nc = rows // CHUNK
    decay, kw, qw, gch = tables
    ret_cols = 2 * RET_QK + 2 * RET_W

    def rn(n):
        return nc - 1 - n

    def body(q_ref, k_ref, v_ref, z_ref, cos_ref, sin_ref, dm_ref, kw_ref, qw_ref, g_ref, w_ref,
             o_ref, do_ref, st_ref, dp_ref, dw_ref, ds_scr):
        n = pl.program_id(1)
        dq_ref = dp_ref.at[:, 0:RET_QK]
        dk_ref = dp_ref.at[:, RET_QK:2 * RET_QK]
        dv_ref = dp_ref.at[:, 2 * RET_QK:2 * RET_QK + RET_W]
        dz_ref = dp_ref.at[:, 2 * RET_QK + RET_W:ret_cols]

        @pl.when(n == 0)
        def _():
            ds_scr[...] = jnp.zeros_like(ds_scr)
            dw_ref[...] = jnp.zeros_like(dw_ref)

        cosv, sinv = cos_ref[...], sin_ref[...]
        for hh in range(RET_HB):
            qc = slice(hh * RET_DK, (hh + 1) * RET_DK)
            vc = slice(hh * RET_DV, (hh + 1) * RET_DV)
            q = _rope(q_ref[:, qc], cosv, sinv)
            k = _rope(k_ref[:, qc], cosv, sinv) * (RET_DK ** -0.5)
            v = v_ref[:, vc]
            do, dz, dw = _gate_bwd(do_ref[:, vc], o_ref[:, vc], z_ref[:, vc], w_ref[:, vc])
            dz_ref[:, vc] = dz.astype(BF16)
            dw_ref[hh] += dw
            dm = dm_ref[hh]
            s = st_ref[hh, 0]
            g1 = ds_scr[hh]
            p = _dot(q, k, NT) * dm
            kwv = k * kw_ref[hh]
            qwv = q * qw_ref[hh]
            dp = _dot(do, v, NT)
            da = dp * dm
            dv = _dot(p, do, TN) + _dot(kwv, g1)
            dq = _dot(da, k) + _dot(do, s, NT) * qw_ref[hh]
            dk = _dot(da, q, TN) + _dot(v, g1, NT) * kw_ref[hh]
            ds_scr[hh] = g1 * g_ref[hh] + _dot(qwv, do, TN)
            dv_ref[:, vc] = dv.astype(BF16)
            dq_ref[:, qc] = _rope_t(dq, cosv, sinv).astype(BF16)
            dk_ref[:, qc] = _rope_t(dk * (RET_DK ** -0.5), cosv, sinv).astype(BF16)

    in_specs = _ret_in_specs(True, nc) + [
        pl.BlockSpec((CHUNK, RET_VB), lambda h, n: (rn(n), h)),
        pl.BlockSpec((CHUNK, RET_VB), lambda h, n: (rn(n), h)),
        pl.BlockSpec((RET_HB, 1, RET_DK, RET_DV), lambda h, n: (h, rn(n), 0, 0)),
    ]
    return pl.pallas_call(
        body, name="ret_bwd", grid=(RET_HEADS // RET_HB, nc),
        in_specs=in_specs,
        out_specs=[pl.BlockSpec((CHUNK, ret_cols), lambda h, n: (rn(n), 0)),
                   pl.BlockSpec((RET_HB, 1, RET_DV), lambda h, n: (h, 0, 0))],
        out_shape=[jax.ShapeDtypeStruct((rows, IN_AB), BF16), jax.ShapeDtypeStruct((RET_HEADS, 1, RET_DV), F32)],
        scratch_shapes=[pltpu.VMEM((RET_HB, RET_DK, RET_DV), F32)],
        compiler_params=pltpu.CompilerParams(dimension_semantics=("parallel", "arbitrary")),
    )(proj, proj, proj, proj, cosf, sinf, decay, kw, qw, gch, normw, o_ret, dmix, states)


def _s5_discretize(lam_re, lam_im, log_dt, b_re, b_im):
    dt = jnp.exp(log_dt)[:, None]
    mag = jnp.exp(lam_re * dt)
    ab_re, ab_im = mag * jnp.cos(lam_im * dt), mag * jnp.sin(lam_im * dt)
    den = lam_re * lam_re + lam_im * lam_im
    nr, ni = ab_re - 1.0, ab_im
    f_re = (nr * lam_re + ni * lam_im) / den
    f_im = (ni * lam_re - nr * lam_im) / den
    bb_re = f_re[..., None] * b_re - f_im[..., None] * b_im
    bb_im = f_re[..., None] * b_im + f_im[..., None] * b_re
    return ab_re, ab_im, bb_re, bb_im


def _bdiag_in(bb):
    t = bb.reshape(S5_NT, S5_TG, S5_P, S5_GH).transpose(0, 1, 3, 2)
    eye = jnp.eye(S5_TG, dtype=bb.dtype)
    full = t[:, :, :, None, :] * eye[None, :, None, :, None]
    return full.reshape(S5_NT, S5_TU, S5_TS)


def _bdiag_in_extract(dense):
    t = dense.reshape(S5_NT, S5_TG, S5_GH, S5_TG, S5_P)
    diag = jnp.stack([t[:, g, :, g, :] for g in range(S5_TG)], axis=1)
    return diag.transpose(0, 1, 3, 2).reshape(S5_G, S5_P, S5_GH)


def _bdiag_out(c):
    t = c.reshape(S5_NT, S5_TG, S5_GH, S5_P).transpose(0, 1, 3, 2)
    eye = jnp.eye(S5_TG, dtype=c.dtype)
    full = t[:, :, :, None, :] * eye[None, :, None, :, None]
    return full.reshape(S5_NT, S5_TS, S5_TU)


def _bdiag_out_extract(dense):
    t = dense.reshape(S5_NT, S5_TG, S5_P, S5_TG, S5_GH)
    diag = jnp.stack([t[:, g, :, g, :] for g in range(S5_TG)], axis=1)
    return diag.transpose(0, 1, 3, 2).reshape(S5_G, S5_GH, S5_P)


def _cmul(ar, ai, br, bi):
    return ar * br - ai * bi, ar * bi + ai * br


S5_SEG = 8
S5_STEPS = CHUNK // S5_SEG


def _seg_perm(x):
    c = x.shape[1]
    return jnp.swapaxes(x.reshape(S5_SEG, S5_STEPS, c), 0, 1).reshape(CHUNK, c)


def _seg_unperm(x):
    c = x.shape[1]
    return jnp.swapaxes(x.reshape(S5_STEPS, S5_SEG, c), 0, 1).reshape(CHUNK, c)


def _rows(x, p):
    return x[p * S5_SEG:(p + 1) * S5_SEG]


def _s5_tables(ar, ai, tr_scr, ti_scr, wfr_scr, wfi_scr, wbr_scr, wbi_scr):
    row = lax.broadcasted_iota(jnp.int32, (S5_SEG, 1), 0)
    a8r = jnp.broadcast_to(ar, (S5_SEG, S5_TS))
    a8i = jnp.broadcast_to(ai, (S5_SEG, S5_TS))
    pr, pi = a8r, a8i
    for p in range(S5_STEPS):
        tr_scr[p * S5_SEG:(p + 1) * S5_SEG, :] = pr
        ti_scr[p * S5_SEG:(p + 1) * S5_SEG, :] = pi
        if p < S5_STEPS - 1:
            pr, pi = _cmul(pr, pi, a8r, a8i)
    wr, wi = pr, pi
    sh = 1
    while sh < S5_SEG:
        keep = row >= sh
        sr = jnp.where(keep, pltpu.roll(wr, sh, 0), 1.0)
        si = jnp.where(keep, pltpu.roll(wi, sh, 0), 0.0)
        wr, wi = _cmul(wr, wi, sr, si)
        sh *= 2
    wfr_scr[...] = wr
    wfi_scr[...] = wi
    wr, wi = pr, -pi
    sh = 1
    while sh < S5_SEG:
        keep = row < S5_SEG - sh
        sr = jnp.where(keep, pltpu.roll(wr, S5_SEG - sh, 0), 1.0)
        si = jnp.where(keep, pltpu.roll(wi, S5_SEG - sh, 0), 0.0)
        wr, wi = _cmul(wr, wi, sr, si)
        sh *= 2
    wbr_scr[...] = wr
    wbi_scr[...] = wi


def _seg_scan(vr, vi, ar, ai, tr_scr, ti_scr, wr_scr, wi_scr, c0r, c0i, down):
    row = lax.broadcasted_iota(jnp.int32, (S5_SEG, 1), 0)
    sgn = 1.0 if down else -1.0
    order = list(range(S5_STEPS)) if down else list(range(S5_STEPS - 1, -1, -1))
    xr, xi = _rows(vr, order[0]), _rows(vi, order[0])
    loc = {order[0]: (xr, xi)}
    for p in order[1:]:
        mr, mi = _cmul(ar, sgn * ai, xr, xi)
        xr, xi = mr + _rows(vr, p), mi + _rows(vi, p)
        loc[p] = (xr, xi)
    last = S5_STEPS - 1
    mr, mi = tr_scr[last * S5_SEG:(last + 1) * S5_SEG, :], sgn * ti_scr[last * S5_SEG:(last + 1) * S5_SEG, :]
    er, ei = xr, xi
    sh = 1
    while sh < S5_SEG:
        if down:
            keep = row >= sh
            sr, si = pltpu.roll(er, sh, 0), pltpu.roll(ei, sh, 0)
        else:
            keep = row < S5_SEG - sh
            sr, si = pltpu.roll(er, S5_SEG - sh, 0), pltpu.roll(ei, S5_SEG - sh, 0)
        pr, pi = _cmul(mr, mi, jnp.where(keep, sr, 0.0), jnp.where(keep, si, 0.0))
        er, ei = er + pr, ei + pi
        mr, mi = _cmul(mr, mi, mr, mi)
        sh *= 2
    pr, pi = _cmul(wr_scr[...], wi_scr[...], c0r, c0i)
    er, ei = er + pr, ei + pi
    if down:
        nr = jnp.where(row == 0, c0r, pltpu.roll(er, 1, 0))
        ni = jnp.where(row == 0, c0i, pltpu.roll(ei, 1, 0))
    else:
        nr = jnp.where(row == S5_SEG - 1, c0r, pltpu.roll(er, S5_SEG - 1, 0))
        ni = jnp.where(row == S5_SEG - 1, c0i, pltpu.roll(ei, S5_SEG - 1, 0))
    out_r, out_i = [], []
    for p in range(S5_STEPS):
        q = p if down else S5_STEPS - 1 - p
        pr, pi = _cmul(tr_scr[q * S5_SEG:(q + 1) * S5_SEG, :], sgn * ti_scr[q * S5_SEG:(q + 1) * S5_SEG, :], nr, ni)
        out_r.append(loc[p][0] + pr)
        out_i.append(loc[p][1] + pi)
    return jnp.concatenate(out_r, axis=0), jnp.concatenate(out_i, axis=0), (nr, ni), (er, ei)


def _gelu(y):
    c = math.sqrt(2.0 / math.pi)
    return 0.5 * y * (1.0 + jnp.tanh(c * (y + 0.044715 * y * y * y)))


def _gelu_grad(y):
    c = math.sqrt(2.0 / math.pi)
    th = jnp.tanh(c * (y + 0.044715 * y * y * y))
    return 0.5 * (1.0 + th) + 0.5 * y * (1.0 - th * th) * c * (1.0 + 3.0 * 0.044715 * y * y)


def _s5_fwd(proj, ab, bd_b, bd_c, dvec):
    rows = proj.shape[0]
    nc = rows // CHUNK
    tps = S5_FWD_TILES
    ubw = tps * S5_TU
    ub = (2 * RET_QK + 2 * RET_W) // ubw
    ab_re, ab_im = ab
    bre, bim = bd_b
    cre, cim = bd_c

    def body(u_ref, ar_ref, ai_ref, bre_ref, bim_ref, cre_ref, cim_ref, d_ref,
             y_ref, g_ref, er_ref, ei_ref, tr_scr, ti_scr, wfr_scr, wfi_scr, wbr_scr, wbi_scr,
             cr_scr, ci_scr, er_scr, ei_scr):
        n = pl.program_id(1)
        for tt in range(tps):
            cols = slice(tt * S5_TU, (tt + 1) * S5_TU)
            ar, ai = ar_ref[tt], ai_ref[tt]
            trs, tis, wfr, wfi = tr_scr.at[tt], ti_scr.at[tt], wfr_scr.at[tt], wfi_scr.at[tt]

            @pl.when(n == 0)
            def _(tt=tt, ar=ar, ai=ai, trs=trs, tis=tis, wfr=wfr, wfi=wfi):
                _s5_tables(ar, ai, trs, tis, wfr, wfi, wbr_scr.at[tt], wbi_scr.at[tt])
                cr_scr[tt] = jnp.zeros((S5_SEG, S5_TS), F32)
                ci_scr[tt] = jnp.zeros((S5_SEG, S5_TS), F32)

            u = _seg_perm(u_ref[:, cols])
            c0r, c0i = cr_scr[tt], ci_scr[tt]
            er_ref[tt, 0] = c0r
            ei_ref[tt, 0] = c0i
            xr, xi, _, (er, ei) = _seg_scan(_dot(u, bre_ref[tt]), _dot(u, bim_ref[tt]), ar, ai, trs, tis,
                                            wfr, wfi, c0r, c0i, True)
            er_scr[tt] = er
            ei_scr[tt] = ei
            cr_scr[tt] = jnp.broadcast_to(er_scr[tt, S5_SEG - 1:S5_SEG, :], (S5_SEG, S5_TS))
            ci_scr[tt] = jnp.broadcast_to(ei_scr[tt, S5_SEG - 1:S5_SEG, :], (S5_SEG, S5_TS))
            y = _seg_unperm(_dot(xr, cre_ref[tt]) - _dot(xi, cim_ref[tt]) + d_ref[:, cols] * u)
            y_ref[:, cols] = y
            g_ref[:, cols] = _gelu(y).astype(BF16)

    vec = pl.BlockSpec((tps, 1, S5_TS), lambda t, n: (t, 0, 0))
    return pl.pallas_call(
        body, name="s5_fwd", grid=(S5_NT // tps, nc),
        in_specs=[pl.BlockSpec((CHUNK, ubw), lambda t, n: (n, ub + t)), vec, vec,
                  pl.BlockSpec((tps, S5_TU, S5_TS), lambda t, n: (t, 0, 0)),
                  pl.BlockSpec((tps, S5_TU, S5_TS), lambda t, n: (t, 0, 0)),
                  pl.BlockSpec((tps, S5_TS, S5_TU), lambda t, n: (t, 0, 0)),
                  pl.BlockSpec((tps, S5_TS, S5_TU), lambda t, n: (t, 0, 0)),
                  pl.BlockSpec((1, ubw), lambda t, n: (0, t))],
        out_specs=[pl.BlockSpec((CHUNK, ubw), lambda t, n: (n, t)),
                   pl.BlockSpec((CHUNK, ubw), lambda t, n: (n, t)),
                   pl.BlockSpec((tps, 1, 8, S5_TS), lambda t, n: (t, n, 0, 0)),
                   pl.BlockSpec((tps, 1, 8, S5_TS), lambda t, n: (t, n, 0, 0))],
        out_shape=[jax.ShapeDtypeStruct((rows, S5_W), F32), jax.ShapeDtypeStruct((rows, S5_W), BF16),
                   jax.ShapeDtypeStruct((S5_NT, nc, 8, S5_TS), F32),
                   jax.ShapeDtypeStruct((S5_NT, nc, 8, S5_TS), F32)],
        scratch_shapes=[pltpu.VMEM((tps, CHUNK, S5_TS), F32) for _ in range(2)]
        + [pltpu.VMEM((tps, S5_SEG, S5_TS), F32) for _ in range(8)],
        compiler_params=pltpu.CompilerParams(dimension_semantics=("parallel", "arbitrary")),
    )(proj, ab_re.reshape(S5_NT, 1, S5_TS), ab_im.reshape(S5_NT, 1, S5_TS), bre, bim, cre, cim, dvec)


def _s5_bwd(proj, dy, ab, bd_b, bd_c, dvec, entry, dproj):
    rows = proj.shape[0]
    nc = rows // CHUNK
    tps = S5_BWD_TILES
    ubw = tps * S5_TU
    ub = (2 * RET_QK + 2 * RET_W) // ubw
    ab_re, ab_im = ab
    bre, bim = bd_b
    cre, cim = bd_c
    er, ei = entry

    def rn(n):
        return nc - 1 - n

    def body(u_ref, dy_ref, ar_ref, ai_ref, bre_ref, bim_ref, cre_ref, cim_ref, d_ref, er_ref, ei_ref, dp_ref,
             du_ref, dbr_ref, dbi_ref, dcr_ref, dci_ref, dar_ref, dai_ref, dd_ref,
             tr_scr, ti_scr, wfr_scr, wfi_scr, wbr_scr, wbi_scr, gr_scr, gi_scr, er_scr, ei_scr):
        n = pl.program_id(1)

        @pl.when(n == 0)
        def _():
            gr_scr[...] = jnp.zeros_like(gr_scr)
            gi_scr[...] = jnp.zeros_like(gi_scr)
            for r in (dbr_ref, dbi_ref, dcr_ref, dci_ref, dar_ref, dai_ref, dd_ref):
                r[...] = jnp.zeros_like(r)

        for tt in range(tps):
            cols = slice(tt * S5_TU, (tt + 1) * S5_TU)
            ar, ai = ar_ref[tt], ai_ref[tt]
            trs, tis = tr_scr.at[tt], ti_scr.at[tt]

            @pl.when(n == 0)
            def _(tt=tt, ar=ar, ai=ai, trs=trs, tis=tis):
                _s5_tables(ar, ai, trs, tis, wfr_scr.at[tt], wfi_scr.at[tt], wbr_scr.at[tt], wbi_scr.at[tt])

            u = _seg_perm(u_ref[:, cols])
            dy = _seg_perm(dy_ref[:, cols])
            xr, xi, (pr, pi), _ = _seg_scan(_dot(u, bre_ref[tt]), _dot(u, bim_ref[tt]), ar, ai, trs, tis,
                                            wfr_scr.at[tt], wfi_scr.at[tt], er_ref[tt, 0], ei_ref[tt, 0], True)
            dcr_ref[tt] += _dot(xr, dy, TN)
            dci_ref[tt] -= _dot(xi, dy, TN)
            gr, gi, _, (er, ei) = _seg_scan(_dot(dy, cre_ref[tt], NT), -_dot(dy, cim_ref[tt], NT), ar, ai, trs, tis,
                                            wbr_scr.at[tt], wbi_scr.at[tt], gr_scr[tt], gi_scr[tt], False)
            er_scr[tt] = er
            ei_scr[tt] = ei
            gr_scr[tt] = jnp.broadcast_to(er_scr[tt, 0:1, :], (S5_SEG, S5_TS))
            gi_scr[tt] = jnp.broadcast_to(ei_scr[tt, 0:1, :], (S5_SEG, S5_TS))
            xpr = jnp.concatenate([pr, xr[:CHUNK - S5_SEG]], axis=0)
            xpi = jnp.concatenate([pi, xi[:CHUNK - S5_SEG]], axis=0)
            dar_ref[tt] += jnp.sum((xpr * gr + xpi * gi).reshape(S5_STEPS, S5_SEG, S5_TS), axis=0)
            dai_ref[tt] += jnp.sum((xpr * gi - xpi * gr).reshape(S5_STEPS, S5_SEG, S5_TS), axis=0)
            dbr_ref[tt] += _dot(u, gr, TN)
            dbi_ref[tt] += _dot(u, gi, TN)
            dd_ref[tt] += jnp.sum((dy * u).reshape(S5_STEPS, S5_SEG, S5_TU), axis=0)
            du = dy * d_ref[:, cols] + _dot(gr, bre_ref[tt], NT) + _dot(gi, bim_ref[tt], NT)
            du_ref[:, cols] = _seg_unperm(du).astype(BF16)

    vec = pl.BlockSpec((tps, 1, S5_TS), lambda t, n: (t, 0, 0))
    acc_b = pl.BlockSpec((tps, S5_TU, S5_TS), lambda t, n: (t, 0, 0))
    acc_c = pl.BlockSpec((tps, S5_TS, S5_TU), lambda t, n: (t, 0, 0))
    acc_a = pl.BlockSpec((tps, 8, S5_TS), lambda t, n: (t, 0, 0))
    ent = pl.BlockSpec((tps, 1, 8, S5_TS), lambda t, n: (t, rn(n), 0, 0))
    return pl.pallas_call(
        body, name="s5_bwd", grid=(S5_NT // tps, nc),
        in_specs=[pl.BlockSpec((CHUNK, ubw), lambda t, n: (rn(n), ub + t)),
                  pl.BlockSpec((CHUNK, ubw), lambda t, n: (rn(n), t)), vec, vec,
                  acc_b, acc_b, acc_c, acc_c, pl.BlockSpec((1, ubw), lambda t, n: (0, t)), ent, ent, ANY],
        out_specs=[pl.BlockSpec((CHUNK, ubw), lambda t, n: (rn(n), ub + t)), acc_b, acc_b, acc_c, acc_c, acc_a, acc_a,
                   pl.BlockSpec((tps, 8, S5_TU), lambda t, n: (t, 0, 0))],
        input_output_aliases={11: 0},
        out_shape=[jax.ShapeDtypeStruct(dproj.shape, BF16),
                   jax.ShapeDtypeStruct((S5_NT, S5_TU, S5_TS), F32), jax.ShapeDtypeStruct((S5_NT, S5_TU, S5_TS), F32),
                   jax.ShapeDtypeStruct((S5_NT, S5_TS, S5_TU), F32), jax.ShapeDtypeStruct((S5_NT, S5_TS, S5_TU), F32),
                   jax.ShapeDtypeStruct((S5_NT, 8, S5_TS), F32), jax.ShapeDtypeStruct((S5_NT, 8, S5_TS), F32),
                   jax.ShapeDtypeStruct((S5_NT, 8, S5_TU), F32)],
        scratch_shapes=[pltpu.VMEM((tps, CHUNK, S5_TS), F32) for _ in range(2)]
        + [pltpu.VMEM((tps, S5_SEG, S5_TS), F32) for _ in range(8)],
        compiler_params=pltpu.CompilerParams(dimension_semantics=("parallel", "arbitrary")),
    )(proj, dy, ab_re.reshape(S5_NT, 1, S5_TS), ab_im.reshape(S5_NT, 1, S5_TS), bre, bim, cre, cim, dvec, er, ei,
      dproj)


def _s5_gate_bwd(dmix, g, t, proj, dproj):
    rows = g.shape[0]
    tm = _row_tile(rows, 384)
    ob = RET_W // S5_W
    zb = (2 * RET_QK + 2 * RET_W + S5_W) // S5_W

    def body(do_ref, g_ref, t_ref, z_ref, dp_ref, dz_ref, dt_ref, dg_ref):
        do = do_ref[...]
        gv = g_ref[...].astype(F32)
        z = z_ref[...]
        st = _sigmoid(t_ref[...])
        sg = _sigmoid(z)
        os5 = gv * st
        dz_ref[...] = (do * os5 * sg * (1.0 + z * (1.0 - sg))).astype(BF16)
        dos = do * z * sg
        dt_ref[...] = (dos * gv * st * (1.0 - st)).astype(BF16)
        dg_ref[...] = dos * st

    blk = pl.BlockSpec((tm, S5_W), lambda i: (i, 0))
    return pl.pallas_call(
        body, name="s5_gate_bwd", grid=(rows // tm,),
        in_specs=[pl.BlockSpec((tm, S5_W), lambda i: (i, ob)), blk, blk,
                  pl.BlockSpec((tm, S5_W), lambda i: (i, zb)), ANY],
        out_specs=[pl.BlockSpec((tm, S5_W), lambda i: (i, zb)), blk, blk],
        out_shape=[jax.ShapeDtypeStruct(dproj.shape, BF16), jax.ShapeDtypeStruct((rows, S5_W), BF16),
                   jax.ShapeDtypeStruct((rows, S5_W), F32)],
        input_output_aliases={4: 0},
    )(dmix, g, t, proj, dproj)


def _split3(x):
    hi = x.astype(BF16)
    r = x - hi.astype(F32)
    mid = r.astype(BF16)
    lo = (r - mid.astype(F32)).astype(BF16)
    return hi, mid, lo


def _tri_sum(x, upper):
    i = lax.broadcasted_iota(jnp.int32, (CHUNK, CHUNK), 0)
    j = lax.broadcasted_iota(jnp.int32, (CHUNK, CHUNK), 1)
    tri = jnp.where((j >= i) if upper else (j <= i), 1.0, 0.0).astype(BF16)
    hi, mid, lo = _split3(x)
    return _dot(tri, lo) + _dot(tri, mid) + _dot(tri, hi)


def _gla_log_decay(gl, wg, bg, n):
    logit = _dot(gl, wg) + bg
    la = (jnp.minimum(logit, 0.0) - jnp.log(1.0 + jnp.exp(-jnp.abs(logit)))) * (1.0 / GLA_TAU)
    row = lax.broadcasted_iota(jnp.int32, (CHUNK, 1), 0)
    live = jnp.logical_or(n > 0, row >= PAD)
    return logit, jnp.where(live, la, 0.0), live


def _gla_in_specs(rev, nc):
    def cn(n):
        return (nc - 1 - n) if rev else n
    kb = GLA_QK // GLA_DK
    vb = 2 * GLA_QK // GLA_DV
    zb = (2 * GLA_QK + GLA_W) // GLA_DV
    gb = (2 * GLA_QK + 2 * GLA_W) // 128
    return [
        pl.BlockSpec((CHUNK, GLA_DK), lambda h, n: (cn(n), h)),
        pl.BlockSpec((CHUNK, GLA_DK), lambda h, n: (cn(n), kb + h)),
        pl.BlockSpec((CHUNK, GLA_DV), lambda h, n: (cn(n), vb + h)),
        pl.BlockSpec((CHUNK, GLA_DV), lambda h, n: (cn(n), zb + h)),
        pl.BlockSpec((CHUNK, 128), lambda h, n: (cn(n), gb)),
        pl.BlockSpec((128, GLA_DK), lambda h, n: (0, h)),
        pl.BlockSpec((1, GLA_DK), lambda h, n: (0, h)),
        pl.BlockSpec((1, GLA_DV), lambda h, n: (0, h)),
    ]


def _gla_fwd(proj, wgate, bgate, normw):
    rows = proj.shape[0]
    nc = rows // CHUNK

    def body(q_ref, k_ref, v_ref, z_ref, gl_ref, wg_ref, bg_ref, w_ref, o_ref, oc_ref, st_ref, s_scr, b_scr):
        n = pl.program_id(1)

        @pl.when(n == 0)
        def _():
            s_scr[...] = jnp.zeros_like(s_scr)

        q = q_ref[...] * (GLA_DK ** -0.5)
        k = k_ref[...]
        v = v_ref[...]
        vb = v.astype(BF16)
        _, la, _ = _gla_log_decay(gl_ref[...], wg_ref[...], bg_ref[...], n)
        b = _tri_sum(la, False)
        b_scr[...] = b
        b_last = b_scr[CHUNK - 1:CHUNK, :]
        st = s_scr[...]
        st_ref[0, 0] = st
        s_scr[...] = st * jnp.exp(b_last) + _dot(v, k * jnp.exp(b_last - b), TN)
        rowc = lax.broadcasted_iota(jnp.int32, (CHUNK, 1), 0)
        rows16 = lax.broadcasted_iota(jnp.int32, (SUB, 1), 0)
        a_tot = jnp.zeros((CHUNK, CHUNK), F32)
        for s in range(1, NSUB):
            lo = s * SUB
            bref = b_scr[lo - 1:lo, :]
            in_s = jnp.logical_and(rowc >= lo, rowc < lo + SUB)
            qh = q * jnp.exp(jnp.where(in_s, b - bref, -1e30))
            kh = k * jnp.exp(jnp.where(rowc < lo, bref - b, -1e30))
            a_tot = a_tot + _dot(qh, kh, NT)
        lane = lax.broadcasted_iota(jnp.int32, (SUB, CHUNK), 1)
        diag = []
        for s in range(NSUB):
            lo = s * SUB
            qs, bs = q[lo:lo + SUB], b[lo:lo + SUB]
            s_blk = jnp.zeros((SUB, CHUNK), F32)
            for j in range(SUB):
                r = lo + j
                e = jnp.exp(jnp.where(rows16 >= j, bs - b_scr[r:r + 1, :], -1e30))
                col = jnp.sum(qs * k_ref[r:r + 1, :] * e, axis=1, keepdims=True)
                s_blk = jnp.where(lane == r, col, s_blk)
            diag.append(s_blk)
        o = _dot(q * jnp.exp(b), st, NT) + _dot(a_tot + jnp.concatenate(diag, axis=0), vb)
        o_ref[...] = o
        oc_ref[...] = _gate_fwd(o, z_ref[...], w_ref[...]).astype(BF16)

    return pl.pallas_call(
        body, name="gla_fwd", grid=(GLA_HEADS, nc),
        in_specs=_gla_in_specs(False, nc),
        out_specs=[pl.BlockSpec((CHUNK, GLA_DV), lambda h, n: (n, h)),
                   pl.BlockSpec((CHUNK, GLA_DV), lambda h, n: (n, h)),
                   pl.BlockSpec((1, 1, GLA_DV, GLA_DK), lambda h, n: (h, n, 0, 0))],
        out_shape=[jax.ShapeDtypeStruct((rows, GLA_W), F32), jax.ShapeDtypeStruct((rows, GLA_W), BF16),
                   jax.ShapeDtypeStruct((GLA_HEADS, nc, GLA_DV, GLA_DK), F32)],
        scratch_shapes=[pltpu.VMEM((GLA_DV, GLA_DK), F32), pltpu.VMEM((CHUNK, GLA_DK), F32)],
        compiler_params=pltpu.CompilerParams(dimension_semantics=("parallel", "arbitrary")),
    )(proj, proj, proj, proj, proj, wgate, bgate, normw)


def _gla_bwd(proj, wgate, bgate, normw, o_gla, d_oc, states):
    rows = proj.shape[0]
    nc = rows // CHUNK

    def rn(n):
        return nc - 1 - n

    def body(q_ref, k_ref, v_ref, z_ref, gl_ref, wg_ref, bg_ref, w_ref, o_ref, do_ref, st_ref,
             dq_ref, dk_ref, dv_ref, dz_ref, dl_ref, dw_ref, dbg_ref,
             ds_scr, dq_scr, dk_scr, dv_scr, db_scr, b_scr, q_scr):
        n = pl.program_id(1)
        cn = rn(n)

        @pl.when(n == 0)
        def _():
            ds_scr[...] = jnp.zeros_like(ds_scr)
            dw_ref[...] = jnp.zeros_like(dw_ref)
            dbg_ref[...] = jnp.zeros_like(dbg_ref)

        q = q_ref[...] * (GLA_DK ** -0.5)
        k = k_ref[...]
        v = v_ref[...]
        vb = v.astype(BF16)
        do, dz, dw = _gate_bwd(do_ref[...], o_ref[...], z_ref[...], w_ref[...])
        dz_ref[...] = dz.astype(BF16)
        dw_ref[0] += dw
        logit, la, live = _gla_log_decay(gl_ref[...], wg_ref[...], bg_ref[...], cn)
        b = _tri_sum(la, False)
        b_scr[...] = b
        b_last = b_scr[CHUNK - 1:CHUNK, :]
        e_last = jnp.exp(b_last)
        st = st_ref[0, 0]
        g1 = ds_scr[...]
        eb = jnp.exp(b)
        qe = q * eb
        dqe = _dot(do, st)
        dq_scr[...] = dqe * eb
        db_scr[...] = dqe * qe
        ekb = jnp.exp(b_last - b)
        kdec = k * ekb
        dkdec = _dot(v, g1)
        dv_scr[...] = _dot(kdec, g1, NT)
        dk_scr[...] = dkdec * ekb
        wk = dkdec * kdec
        db_scr[...] -= wk
        dbl = jnp.sum(wk, axis=0, keepdims=True) + jnp.sum(g1 * st, axis=0, keepdims=True) * e_last
        ds_scr[...] = g1 * e_last + _dot(do, qe, TN)
        rowc = lax.broadcasted_iota(jnp.int32, (CHUNK, 1), 0)
        rows16 = lax.broadcasted_iota(jnp.int32, (SUB, 1), 0)
        da_full = _dot(do, vb, NT)
        a_tot = jnp.zeros((CHUNK, CHUNK), F32)
        for s in range(1, NSUB):
            lo = s * SUB
            bref = b_scr[lo - 1:lo, :]
            in_s = jnp.logical_and(rowc >= lo, rowc < lo + SUB)
            eq = jnp.exp(jnp.where(in_s, b - bref, -1e30))
            ek = jnp.exp(jnp.where(rowc < lo, bref - b, -1e30))
            qh = q * eq
            kh = k * ek
            a_tot = a_tot + _dot(qh, kh, NT)
            da = jnp.where(in_s, da_full, 0.0)
            dqh = _dot(da, kh)
            dkh = _dot(da, qh, TN)
            tq = dqh * qh
            tk = dkh * kh
            dq_scr[...] += dqh * eq
            dk_scr[...] += dkh * ek
            db_scr[...] += tq - tk
            db_scr[lo - 1:lo, :] += jnp.sum(tk, axis=0, keepdims=True) - jnp.sum(tq, axis=0, keepdims=True)
        dat_full = _dot(vb, do, NT)
        q_scr[...] = q
        lane = lax.broadcasted_iota(jnp.int32, (SUB, CHUNK), 1)
        diag = []
        for s in range(NSUB):
            lo = s * SUB
            qs, ks, bs = q[lo:lo + SUB], k[lo:lo + SUB], b[lo:lo + SUB]
            da_blk, dat_blk = da_full[lo:lo + SUB], dat_full[lo:lo + SUB]
            dqs = jnp.zeros((SUB, GLA_DK), F32)
            dks = jnp.zeros((SUB, GLA_DK), F32)
            dbs = jnp.zeros((SUB, GLA_DK), F32)
            s_blk = jnp.zeros((SUB, CHUNK), F32)
            for j in range(SUB):
                r = lo + j
                kj = k_ref[r:r + 1, :]
                e = jnp.exp(jnp.where(rows16 >= j, bs - b_scr[r:r + 1, :], -1e30))
                p = qs * e * kj
                s_blk = jnp.where(lane == r, jnp.sum(p, axis=1, keepdims=True), s_blk)
                dcol = jnp.sum(jnp.where(lane == r, da_blk, 0.0), axis=1, keepdims=True)
                dqs = dqs + (dcol * e) * kj
                dbs = dbs + dcol * p
            for i in range(SUB):
                r = lo + i
                e = jnp.exp(jnp.where(rows16 <= i, b_scr[r:r + 1, :] - bs, -1e30))
                drow = jnp.sum(jnp.where(lane == r, dat_blk, 0.0), axis=1, keepdims=True)
                nq = (drow * e) * q_scr[r:r + 1, :]
                dks = dks + nq
                dbs = dbs - nq * ks
            dq_scr[lo:lo + SUB, :] += dqs
            dk_scr[lo:lo + SUB, :] += dks
            db_scr[lo:lo + SUB, :] += dbs
            diag.append(s_blk)
        dv_scr[...] += _dot(a_tot + jnp.concatenate(diag, axis=0), do, TN)
        db_scr[CHUNK - 1:CHUNK, :] += dbl
        dla = _tri_sum(db_scr[...], True)
        dlogit = jnp.where(live, dla * (1.0 / GLA_TAU) * _sigmoid(-logit), 0.0)
        dl_ref[...] = dlogit
        dbg_ref[0] += jnp.sum(dlogit, axis=0, keepdims=True)
        dq_ref[...] = (dq_scr[...] * (GLA_DK ** -0.5)).astype(BF16)
        dk_ref[...] = dk_scr[...].astype(BF16)
        dv_ref[...] = dv_scr[...].astype(BF16)

    in_specs = _gla_in_specs(True, nc) + [
        pl.BlockSpec((CHUNK, GLA_DV), lambda h, n: (rn(n), h)),
        pl.BlockSpec((CHUNK, GLA_DV), lambda h, n: (rn(n), h)),
        pl.BlockSpec((1, 1, GLA_DV, GLA_DK), lambda h, n: (h, rn(n), 0, 0)),
    ]
    return pl.pallas_call(
        body, name="gla_bwd", grid=(GLA_HEADS, nc),
        in_specs=in_specs,
        out_specs=[pl.BlockSpec((CHUNK, GLA_DK), lambda h, n: (rn(n), h)),
                   pl.BlockSpec((CHUNK, GLA_DK), lambda h, n: (rn(n), h)),
                   pl.BlockSpec((CHUNK, GLA_DV), lambda h, n: (rn(n), h)),
                   pl.BlockSpec((CHUNK, GLA_DV), lambda h, n: (rn(n), h)),
                   pl.BlockSpec((CHUNK, GLA_DK), lambda h, n: (rn(n), h)),
                   pl.BlockSpec((1, 1, GLA_DV), lambda h, n: (h, 0, 0)),
                   pl.BlockSpec((1, 1, GLA_DK), lambda h, n: (h, 0, 0))],
        out_shape=[jax.ShapeDtypeStruct((rows, GLA_QK), BF16), jax.ShapeDtypeStruct((rows, GLA_QK), BF16),
                   jax.ShapeDtypeStruct((rows, GLA_W), BF16), jax.ShapeDtypeStruct((rows, GLA_W), BF16),
                   jax.ShapeDtypeStruct((rows, GLA_QK), F32),
                   jax.ShapeDtypeStruct((GLA_HEADS, 1, GLA_DV), F32),
                   jax.ShapeDtypeStruct((GLA_HEADS, 1, GLA_DK), F32)],
        scratch_shapes=[pltpu.VMEM((GLA_DV, GLA_DK), F32), pltpu.VMEM((CHUNK, GLA_DK), F32),
                        pltpu.VMEM((CHUNK, GLA_DK), F32), pltpu.VMEM((CHUNK, GLA_DV), F32),
                        pltpu.VMEM((CHUNK, GLA_DK), F32), pltpu.VMEM((CHUNK, GLA_DK), F32),
                        pltpu.VMEM((CHUNK, GLA_DK), F32)],
        compiler_params=pltpu.CompilerParams(dimension_semantics=("parallel", "arbitrary")),
    )(proj, proj, proj, proj, proj, wgate, bgate, normw, o_gla, d_oc, states)


def _adamw(name, w, g, m, v):
    rows, cols = w.shape
    tm = 8
    for cand in range(8, rows + 1, 8):
        if rows % cand == 0 and cand * cols * 4 <= 2 ** 21:
            tm = cand
    c1 = 1.0 - ADAM_B1 ** ADAM_STEP
    c2 = 1.0 - ADAM_B2 ** ADAM_STEP

    def body(w_ref, g_ref, m_ref, v_ref, d_ref, nm_ref, nv_ref):
        gv = g_ref[...]
        nm = ADAM_B1 * m_ref[...] + (1.0 - ADAM_B1) * gv
        nv = ADAM_B2 * v_ref[...] + (1.0 - ADAM_B2) * (gv * gv)
        nm_ref[...] = nm
        nv_ref[...] = nv
        d_ref[...] = -ADAM_LR * ((nm / c1) / (jnp.sqrt(nv / c2) + ADAM_EPS) + ADAM_WD * w_ref[...])

    blk = pl.BlockSpec((tm, cols), lambda i: (i, 0))
    return pl.pallas_call(
        body, name=name, grid=(rows // tm,),
        in_specs=[blk] * 4, out_specs=[blk] * 3,
        out_shape=[jax.ShapeDtypeStruct((rows, cols), F32)] * 3,
    )(w, g, m, v)


def _place():
    x, y, c = lax.axis_index("x"), lax.axis_index("y"), lax.axis_index("c")
    chips = [(1 - x, y), (x, 1 - y), (1 - x, 1 - y)]
    return x, y, c, chips


ANY = pl.BlockSpec(memory_space=pl.ANY)


def _gathered_struct(shape, dtype, kind):
    r, cc = shape
    if kind == "row":
        return jax.ShapeDtypeStruct((N_SHARD * r, cc), dtype)
    if kind == "col":
        return jax.ShapeDtypeStruct((r, N_SHARD * cc), dtype)
    return jax.ShapeDtypeStruct((N_SHARD, r, cc), dtype)


def _cast_place(name, w, kind, mine_arr, dtype, also_own=False):
    r, cc = w.shape
    tr = r
    for cand in (256, 128, 64, 32, 16):
        if r % cand == 0:
            tr = cand
            break
    nb = r // tr
    if kind == "row":
        o_spec = pl.BlockSpec((tr, cc), lambda i, m: (m[0] * nb + i, 0))
    elif kind == "col":
        o_spec = pl.BlockSpec((tr, cc), lambda i, m: (i, m[0]))
    else:
        o_spec = pl.BlockSpec((None, tr, cc), lambda i, m: (m[0], i, 0))
    w_spec = pl.BlockSpec((tr, cc), lambda i, m: (i, 0))

    def body(m_ref, w_ref, o_ref, *own_ref):
        o_ref[...] = w_ref[...].astype(o_ref.dtype)
        for ref in own_ref:
            ref[...] = w_ref[...].astype(ref.dtype)

    out_specs, out_shape = [o_spec], [_gathered_struct((r, cc), dtype, kind)]
    if also_own:
        out_specs.append(w_spec)
        out_shape.append(jax.ShapeDtypeStruct((r, cc), dtype))
    out = pl.pallas_call(
        body, name=name,
        grid_spec=pltpu.PrefetchScalarGridSpec(
            num_scalar_prefetch=1, grid=(nb,), in_specs=[w_spec], out_specs=out_specs),
        out_shape=out_shape,
    )(mine_arr, w)
    return out if also_own else out[0]


def _gather_small(shard):
    rows, cols = shard.shape

    def body(in_ref, out_ref, send_sems, recv_sems):
        x, y, c, chips = _place()
        mine = 2 * x + y
        out_ref[mine] = in_ref[...]
        cps = []
        for j, chip in enumerate(chips):
            cp = pltpu.make_async_remote_copy(
                src_ref=in_ref, dst_ref=out_ref.at[mine], send_sem=send_sems.at[j], recv_sem=recv_sems.at[j],
                device_id=(*chip, c), device_id_type=MESH)
            cp.start()
            cps.append(cp)
        for cp in cps:
            cp.wait()

    vm = pl.BlockSpec(memory_space=pltpu.VMEM)
    return pl.pallas_call(
        body, name="gather_small",
        in_specs=[vm], out_specs=vm,
        out_shape=jax.ShapeDtypeStruct((N_SHARD, rows, cols), F32),
        scratch_shapes=[pltpu.SemaphoreType.DMA((3,)), pltpu.SemaphoreType.DMA((3,))],
        compiler_params=pltpu.CompilerParams(has_side_effects=True),
    )(shard)


def _in_proj_shifted(name, a, b, n, shifts, tm, tn, out_cols, into=None):
    m, k = a.shape
    nb_b = b.shape[1] // tn
    nb_o = out_cols // tn

    def body(s_ref, a_ref, b_ref, *rest):
        rest[-1][...] = _dot(a_ref[...], b_ref[...])

    in_specs = [pl.BlockSpec((tm, k), lambda i, j, s: (i, 0)),
                pl.BlockSpec((k, tn), lambda i, j, s: (0, (s[0] + j) % nb_b))]
    operands = [shifts, a, b]
    aliases = {}
    if into is not None:
        in_specs.append(ANY)
        operands.append(into)
        aliases = {3: 0}
    return pl.pallas_call(
        body, name=name,
        grid_spec=pltpu.PrefetchScalarGridSpec(
            num_scalar_prefetch=1, grid=(m // tm, n // tn), in_specs=in_specs,
            out_specs=pl.BlockSpec((tm, tn), lambda i, j, s: (i, (s[1] + j) % nb_o))),
        out_shape=jax.ShapeDtypeStruct((m, out_cols), F32), input_output_aliases=aliases,
    )(*operands)


def _allreduce_small(buf):
    rows, cols = buf.shape
    hr = rows // 2

    def body(in_ref, out_ref, sib_ref, pair_ref, far_ref, send_sems, recv_sems):
        x, y, c, chips = _place()
        sibling = (x, y, 1 - c)
        mine = pl.ds(pl.multiple_of(c * hr, 8), hr)
        theirs = pl.ds(pl.multiple_of((1 - c) * hr, 8), hr)
        to_sib = pltpu.make_async_remote_copy(
            src_ref=in_ref.at[theirs, :], dst_ref=sib_ref, send_sem=send_sems.at[0], recv_sem=recv_sems.at[0],
            device_id=sibling, device_id_type=MESH)
        to_sib.start()
        to_sib.wait()
        pair_ref[...] = in_ref[mine, :] + sib_ref[...]
        far = [pltpu.make_async_remote_copy(
            src_ref=pair_ref, dst_ref=far_ref.at[j], send_sem=send_sems.at[1 + j], recv_sem=recv_sems.at[1 + j],
            device_id=(*chip, c), device_id_type=MESH) for j, chip in enumerate(chips)]
        for cp in far:
            cp.start()
        for cp in far:
            cp.wait()
        out_ref[mine, :] = (pair_ref[...] + far_ref[1]) + (far_ref[0] + far_ref[2])
        swap = pltpu.make_async_remote_copy(
            src_ref=out_ref.at[mine, :], dst_ref=out_ref.at[mine, :], send_sem=send_sems.at[4],
            recv_sem=recv_sems.at[4], device_id=sibling, device_id_type=MESH)
        swap.start()
        swap.wait()

    vm = pl.BlockSpec(memory_space=pltpu.VMEM)
    return pl.pallas_call(
        body, name="allreduce_small",
        in_specs=[vm], out_specs=vm,
        out_shape=jax.ShapeDtypeStruct((rows, cols), F32),
        scratch_shapes=[pltpu.VMEM((hr, cols), F32), pltpu.VMEM((hr, cols), F32),
                        pltpu.VMEM((3, hr, cols), F32),
                        pltpu.SemaphoreType.DMA((5,)), pltpu.SemaphoreType.DMA((5,))],
        compiler_params=pltpu.CompilerParams(has_side_effects=True),
    )(buf)


def _shard_window(ref, kind, shard_shape, shard, half):
    r, cc = shard_shape
    hr = r // 2
    if kind == "row":
        return ref.at[pl.ds(_mo(shard * r + half * hr, 8), hr), :]
    if kind == "col":
        return ref.at[pl.ds(_mo(half * hr, 8), hr), pl.ds(_mo(shard * cc, 128), cc)]
    if kind == "colw":
        return ref.at[pl.ds(_mo(half * hr, 8), hr), pl.ds(_mo(shard * (cc - 128), 128), cc)]
    return ref.at[shard, pl.ds(_mo(half * hr, 8), hr), :]


HBM = pl.BlockSpec(memory_space=pltpu.HBM)
SEM = pl.BlockSpec(memory_space=pltpu.SEMAPHORE)
DATAFLOW = pltpu.SideEffectType.DATAFLOW_SIDE_EFFECTING


def _in_hbm(a):
    return pltpu.with_memory_space_constraint(a, pltpu.HBM)


def _empty_hbm(shape, dtype):
    return _in_hbm(lax.empty(shape, dtype))


def _copies_start(name, bufs, n_copies, plan, carry):
    nb = len(bufs)

    def body(*refs):
        send_sems, recv_sems = refs[nb + 1], refs[nb + 2]
        for k, (src, dst, to) in enumerate(plan(refs[:nb])):
            pltpu.make_async_remote_copy(src_ref=src, dst_ref=dst, send_sem=send_sems.at[k], recv_sem=recv_sems.at[k],
                                         device_id=to, device_id_type=MESH).start()

    passed = list(bufs) + [carry]
    out = pl.pallas_call(
        body, name=name,
        in_specs=[HBM] * (nb + 1), out_specs=[SEM, SEM] + [HBM] * (nb + 1),
        out_shape=[pltpu.SemaphoreType.DMA((n_copies,)), pltpu.SemaphoreType.DMA((n_copies,))]
        + [pltpu.HBM(a.shape, a.dtype) for a in passed],
        input_output_aliases={i: 2 + i for i in range(nb + 1)},
        compiler_params=pltpu.CompilerParams(has_side_effects=DATAFLOW),
    )(*[_in_hbm(a) for a in passed])
    return out[0], out[1], list(out[2:2 + nb]), out[2 + nb]


def _copies_wait(name, send_sems, recv_sems, bufs, plan, after):
    nb = len(bufs)
    after = list(after) if isinstance(after, (list, tuple)) else [after]

    def body(*refs):
        send, recv = refs[nb], refs[nb + 1]
        for k, (src, dst, to) in enumerate(plan(refs[:nb])):
            cp = pltpu.make_async_remote_copy(src_ref=src, dst_ref=dst, send_sem=send.at[k], recv_sem=recv.at[k],
                                              device_id=to, device_id_type=MESH)
            cp.wait_send()
            cp.wait_recv()

    out = pl.pallas_call(
        body, name=name,
        in_specs=[HBM] * nb + [SEM, SEM] + [ANY] * len(after), out_specs=[HBM] * nb,
        out_shape=[pltpu.HBM(a.shape, a.dtype) for a in bufs],
        input_output_aliases={i: i for i in range(nb)},
        compiler_params=pltpu.CompilerParams(has_side_effects=DATAFLOW),
    )(*bufs, send_sems, recv_sems, *after)
    return list(out)


def _gather_ici_plan(shard_shapes, kinds):
    n_arr = len(kinds)

    def plan(refs):
        x, y, c, chips = _place()
        out = []
        for i in range(n_arr):
            w = _shard_window(refs[i], kinds[i], shard_shapes[i], 2 * x + y, c)
            out += [(w, w, (*chip, c)) for chip in chips]
        return out

    return plan


def _gather_d2d_plan(shard_shapes, kinds):
    n_arr = len(kinds)

    def plan(refs):
        x, y, c, chips = _place()
        out = []
        for i in range(n_arr):
            for chip in chips:
                w = _shard_window(refs[i], kinds[i], shard_shapes[i], 2 * chip[0] + chip[1], c)
                out.append((w, w, (x, y, 1 - c)))
        return out

    return plan


def _rs_pair_plan(kinds, shard_shapes):
    n_arr = len(kinds)

    def plan(refs):
        x, y, c, _ = _place()
        out = []
        for i in range(n_arr):
            for s in range(N_SHARD):
                out.append((_shard_window(refs[i], kinds[i], shard_shapes[i], s, 1 - c), refs[n_arr + i].at[s],
                            (x, y, 1 - c)))
        return out

    return plan


def _rs_share_plan(shard_shapes):
    def plan(refs):
        x, y, c, _ = _place()
        out = []
        for ref, (r, _) in zip(refs, shard_shapes):
            w = ref.at[pl.ds(_mo(c * (r // 2), 8), r // 2), :]
            out.append((w, w, (x, y, 1 - c)))
        return out

    return plan


def _rs_chip_plan(n_arr):
    def plan(refs):
        x, y, c, chips = _place()
        out = []
        for i in range(n_arr):
            for j, chip in enumerate(chips):
                out.append((refs[i].at[2 * chip[0] + chip[1]], refs[n_arr + i].at[j], (*chip, c)))
        return out

    return plan


def _rs_pair_add(name, grad, got, kind, shard_shape, c):
    r, cc = shard_shape
    hr = r // 2
    tr = hr
    for cand in (256, 128, 64, 32, 16):
        if hr % cand == 0:
            tr = cand
            break
    nb = hr // tr

    def body(c_ref, g_ref, t_ref, p_ref, pb_ref):
        p = g_ref[...] + t_ref[...]
        p_ref[...] = p
        pb_ref[...] = p.astype(BF16)

    out_shape = [jax.ShapeDtypeStruct((N_SHARD, hr, cc), F32), jax.ShapeDtypeStruct((N_SHARD, hr, cc), BF16)]
    if kind == "colw":
        tiles = cc // 128
        tr = hr
        g_spec = pl.BlockSpec((tr, 128), lambda s, t, cr: (cr[0], s * (tiles - 1) + t))
        t_spec = pl.BlockSpec((None, tr, 128), lambda s, t, cr: (s, 0, t))
        return pl.pallas_call(
            body, name=name,
            grid_spec=pltpu.PrefetchScalarGridSpec(
                num_scalar_prefetch=1, grid=(N_SHARD, tiles), in_specs=[g_spec, t_spec], out_specs=[t_spec, t_spec]),
            out_shape=out_shape,
        )(c, grad, got)
    if kind == "row":
        g_spec = pl.BlockSpec((tr, cc), lambda s, i, cr: (s * 2 * nb + cr[0] * nb + i, 0))
    elif kind == "col":
        g_spec = pl.BlockSpec((tr, cc), lambda s, i, cr: (cr[0] * nb + i, s))
    else:
        g_spec = pl.BlockSpec((None, tr, cc), lambda s, i, cr: (s, cr[0] * nb + i, 0))
    t_spec = pl.BlockSpec((None, tr, cc), lambda s, i, cr: (s, i, 0))
    return pl.pallas_call(
        body, name=name,
        grid_spec=pltpu.PrefetchScalarGridSpec(
            num_scalar_prefetch=1, grid=(N_SHARD, nb),
            in_specs=[g_spec, t_spec], out_specs=[t_spec, t_spec]),
        out_shape=out_shape,
    )(c, grad, got)


def _rs_chip_add(name, pair_f32, got, shard_shape, mine_c):
    r, cc = shard_shape
    hr = r // 2
    tr = hr
    for cand in (256, 128, 64, 32, 16):
        if hr % cand == 0:
            tr = cand
            break
    nb = hr // tr

    def body(mc_ref, p_ref, t0_ref, t1_ref, t2_ref, o_ref):
        o_ref[...] = (p_ref[...] + t1_ref[...].astype(F32)) + (t0_ref[...].astype(F32) + t2_ref[...].astype(F32))

    def far(j):
        return pl.BlockSpec((None, tr, cc), lambda i, mc: (j, i, 0))

    return pl.pallas_call(
        body, name=name,
        grid_spec=pltpu.PrefetchScalarGridSpec(
            num_scalar_prefetch=1, grid=(nb,),
            in_specs=[pl.BlockSpec((None, tr, cc), lambda i, mc: (mc[0], i, 0)), far(0), far(1), far(2)],
            out_specs=pl.BlockSpec((tr, cc), lambda i, mc: (mc[1] * nb + i, 0))),
        out_shape=jax.ShapeDtypeStruct((r, cc), F32),
    )(mine_c, pair_f32, got, got, got)


def _pack(arrays):
    flat = []
    for a in arrays:
        v = a.reshape(-1).astype(F32)
        flat.append(jnp.pad(v, (0, (-v.shape[0]) % SMALL_COLS)))
    buf = jnp.concatenate(flat).reshape(-1, SMALL_COLS)
    return jnp.pad(buf, ((0, (-buf.shape[0]) % 16), (0, 0)))


def _unpack(buf, shapes):
    out = []
    row = 0
    for s in shapes:
        size = math.prod(s)
        nrow = -(-size // SMALL_COLS)
        out.append(buf[row:row + nrow].reshape(-1)[:size].reshape(s))
        row += nrow
    return out


def kernel(x, meta, norm_ab_w, w_in_ab, ret_norm_w, s5_lam_re, s5_lam_im, s5_log_dt, s5_b_re, s5_b_im, s5_c_re, s5_c_im, s5_d, s5_w_glu, w_out_ab, norm_c_w, w_in_c, gla_w_gate, gla_b_gate, gla_norm_w, w_out_c, final_norm_w, loss_target, m_meta, m_norm_ab_w, m_w_in_ab, m_ret_norm_w, m_s5_lam_re, m_s5_lam_im, m_s5_log_dt, m_s5_b_re, m_s5_b_im, m_s5_c_re, m_s5_c_im, m_s5_d, m_s5_w_glu, m_w_out_ab, m_norm_c_w, m_w_in_c, m_gla_w_gate, m_gla_b_gate, m_gla_norm_w, m_w_out_c, m_final_norm_w, v_meta, v_norm_ab_w, v_w_in_ab, v_ret_norm_w, v_s5_lam_re, v_s5_lam_im, v_s5_log_dt, v_s5_b_re, v_s5_b_im, v_s5_c_re, v_s5_c_im, v_s5_d, v_s5_w_glu, v_w_out_ab, v_norm_c_w, v_w_in_c, v_gla_w_gate, v_gla_b_gate, v_gla_norm_w, v_w_out_c, v_final_norm_w):
    seq = x.shape[1]
    rows = seq + CHUNK
    xi, yi, ci = lax.axis_index("x"), lax.axis_index("y"), lax.axis_index("c")
    mine = 2 * xi + yi
    c_arr = jnp.reshape(ci, (1,)).astype(jnp.int32)
    mine_c = jnp.stack([mine, ci]).astype(jnp.int32)

    mine_arr = jnp.reshape(mine, (1,)).astype(jnp.int32)
    small_shard = _pack([meta, norm_c_w, gla_norm_w, gla_b_gate, gla_w_gate[0]])
    small_all = _gather_small(small_shard)
    first_kinds = ["col"]
    first_shapes = [w_in_ab.shape[1:]]
    first_ici = _gather_ici_plan(first_shapes, first_kinds)
    first_d2d = _gather_d2d_plan(first_shapes, first_kinds)
    wab_buf, wab_own = _cast_place("place_w_in_ab", w_in_ab[0], "col", mine_arr, BF16, also_own=True)
    f_send, f_recv, f_bufs, small_all = _copies_start("gather_first_ici_start", [wab_buf], 3, first_ici, small_all)
    def late_group(items, kinds):
        shapes = [a.shape for _, a in items]
        bufs = [_cast_place("place_" + nm, a, kd, mine_arr, BF16) for (nm, a), kd in zip(items, kinds)]
        return bufs, _gather_ici_plan(shapes, kinds), _gather_d2d_plan(shapes, kinds), 3 * len(items)

    a_bufs, a_ici, a_d2d, n_a = late_group([("w_out_ab", w_out_ab[0]), ("w_glu", s5_w_glu[0])], ["row", "row"])
    b_bufs, b_ici, b_d2d, n_b = late_group([("w_in_c", w_in_c[0]), ("w_out_c", w_out_c[0])], ["stack", "row"])
    g_bufs = a_bufs + b_bufs
    cosf, sinf = _rope_tables(rows)
    rtab = _ret_tables()
    ab_re, ab_im, bb_re, bb_im = _s5_discretize(s5_lam_re[0], s5_lam_im[0], s5_log_dt[0], s5_b_re[0], s5_b_im[0])
    ab = (ab_re, ab_im)
    bd_b = (_bdiag_in(bb_re), _bdiag_in(bb_im))
    bd_c = (_bdiag_out(s5_c_re[0]), _bdiag_out(s5_c_im[0]))
    q4 = D_MODEL // N_SHARD
    g4 = GLA_QK // N_SHARD
    parts = [_unpack(small_all[j], [(N_META, q4), (1, q4), (1, q4), (1, g4), (GLA_RANK, g4)]) for j in range(N_SHARD)]
    meta_f, norm_c_f, gla_norm_f, bgate_f, wgate_f = [jnp.concatenate([p[i] for p in parts], axis=1) for i in range(5)]
    wgate_pad = jnp.pad(wgate_f, ((0, 128 - GLA_RANK), (0, 0)))

    h0, hn0 = _embed_norm(x[0], meta_f, norm_ab_w)

    tm = _row_tile(rows, 1408)
    tmk = _row_tile(rows, 1408)
    own_blocks = (IN_AB // N_SHARD) // 512
    shift_own = jnp.stack([jnp.zeros((), jnp.int32), mine.astype(jnp.int32) * own_blocks])
    shift_rest = jnp.stack([(mine.astype(jnp.int32) + 1) * own_blocks, (mine.astype(jnp.int32) + 1) * own_blocks])
    proj0 = _in_proj_shifted("in_proj_ab_own", hn0, wab_own, IN_AB // N_SHARD, shift_own, tm, 512, IN_AB)
    f_bufs = _copies_wait("gather_first_ici_wait", f_send, f_recv, f_bufs, first_ici,
                          [proj0, cosf, sinf, bd_b[0], bd_b[1], bd_c[0], bd_c[1]] + g_bufs + list(rtab))
    f_send, f_recv, f_bufs, cosf = _copies_start("gather_first_d2d_start", f_bufs, 3, first_d2d, cosf)
    wab, = _copies_wait("gather_first_d2d_wait", f_send, f_recv, f_bufs, first_d2d, cosf)
    a_send, a_recv, a_bufs, wab = _copies_start("gather_a_ici_start", a_bufs, n_a, a_ici, wab)
    b_send, b_recv, b_bufs, wab = _copies_start("gather_b_ici_start", b_bufs, n_b, b_ici, wab)
    proj0 = _in_proj_shifted("in_proj_ab_rest", hn0, wab, IN_AB - IN_AB // N_SHARD, shift_rest, tm, 512, IN_AB,
                             into=proj0)
    o_ret, o_a, ret_states = _ret_fwd(proj0, cosf, sinf, rtab, ret_norm_w)
    a_bufs = _copies_wait("gather_a_ici_wait", a_send, a_recv, a_bufs, a_ici, o_a)
    a_send, a_recv, a_bufs, proj0 = _copies_start("gather_a_d2d_start", a_bufs, n_a, a_d2d, proj0)
    y_s5, g_s5, s5_er, s5_ei = _s5_fwd(proj0, ab, bd_b, bd_c, s5_d)
    wout_ab, wglu = _copies_wait("gather_a_d2d_wait", a_send, a_recv, a_bufs, a_d2d, g_s5)
    zb_blk = (2 * RET_QK + 2 * RET_W + S5_W) // 512

    def glu_out(acc, gv, z):
        return gv.astype(F32) * _sigmoid(acc) * (z * _sigmoid(z))

    t_glu = _matmul("glu", g_s5, wglu, NN, rows, S5_W, S5_W, tm=tm, tn=512, tk=S5_W)
    o_b = _matmul("glu_out", g_s5, wglu, NN, rows, S5_W, S5_W, tm=tm, tn=512, tk=S5_W, out_dtype=BF16,
                  extras=[(g_s5, (tm, 512), lambda i, j, kk: (i, j)),
                          (proj0, (tm, 512), lambda i, j, kk: (i, zb_blk + j))],
                  epilogue=glu_out)
    b_bufs = _copies_wait("gather_b_ici_wait", b_send, b_recv, b_bufs, b_ici, o_b)
    b_send, b_recv, b_bufs, o_b = _copies_start("gather_b_d2d_start", b_bufs, n_b, b_d2d, o_b)
    h1 = _matmul("out_proj_ab", None, None, NN, rows, D_MODEL, OUT_AB, tm=tm, tn=512, tk=1024,
                 segs=[(o_a, (0, 0), wout_ab, (0, 0), RET_W, 1024),
                       (o_b, (0, 0), wout_ab, (RET_W // 1024, 0), S5_W, 1024)],
                 extras=[(h0, (tm, 512), lambda i, j, kk: (i, j))], epilogue=lambda acc, r: acc + r)
    wc_st, wout_c = _copies_wait("gather_b_d2d_wait", b_send, b_recv, b_bufs, b_d2d, h1)
    wc = jnp.concatenate([wc_st[j] for j in range(N_SHARD)] + [jnp.zeros((D_MODEL, IN_C_PAD - IN_C), BF16)], axis=1)

    hn1 = _rms_fwd("norm_c", h1, norm_c_f)
    proj1 = _matmul("in_proj_c", hn1, wc, NN, rows, IN_C_PAD, D_MODEL, tm=tm, tn=896, tk=D_MODEL)
    o_gla, o_c, gla_states = _gla_fwd(proj1, wgate_pad, bgate_f, gla_norm_f)
    h2 = _matmul("out_proj_c", o_c, wout_c, NN, rows, D_MODEL, GLA_W, tm=tm, tn=512, tk=GLA_W,
                 extras=[(h1, (tm, 512), lambda i, j, kk: (i, j))], epilogue=lambda acc, r: acc + r)
    loss_dev, dh2, d_final = _final_loss(h2, final_norm_w.reshape(1, D_MODEL), loss_target[0])

    g_wout_c = _matmul("d_w_out_c", o_c, dh2, TN, GLA_W, D_MODEL, rows, tm=1024, tn=1024, tk=tmk)
    d_oc = _matmul("d_o_c", dh2, wout_c, NT, rows, GLA_W, D_MODEL, tm=tm, tn=512, tk=1024)
    dq1, dk1, dv1, dz1, dlogit, d_gla_norm, d_bgate = _gla_bwd(proj1, wgate_pad, bgate_f, gla_norm_f, o_gla, d_oc, gla_states)
    gl_blk = (2 * GLA_QK + 2 * GLA_W) // 128
    dgl = _matmul("d_g_low", dlogit, wgate_pad, NT, rows, 128, GLA_QK, tm=tm, tn=128, tk=GLA_QK, out_dtype=BF16)
    g_wgate = _matmul("d_w_gate", proj1, dlogit, TN, 128, GLA_QK, rows, tm=128, tn=GLA_QK, tk=tmk, a_off=(0, gl_blk))
    dproj1 = jnp.concatenate([dq1, dk1, dv1, dz1, dgl], axis=1)
    g_wc = _matmul("d_w_in_c", hn1, dproj1, TN, D_MODEL, IN_C_PAD, rows, tm=1024, tn=896, tk=tmk)
    dhn1 = _matmul("d_hn1", dproj1, wc, NT, rows, D_MODEL, IN_C_PAD, tm=tm, tn=512, tk=896)
    dh1, d_norm_c = _rms_bwd("norm_c_bwd", dhn1, h1, norm_c_f, dh2)

    g_wout_ab = _matmul("d_w_out_ab_a", o_a, dh1, TN, RET_W, D_MODEL, rows, tm=1024, tn=1024, tk=tmk,
                        out_shape=jax.ShapeDtypeStruct((OUT_AB, D_MODEL), F32))
    g_wout_ab = _matmul("d_w_out_ab_b", o_b, dh1, TN, S5_W, D_MODEL, rows, tm=1024, tn=1024, tk=tmk,
                        into=(g_wout_ab, RET_W // 1024, 0))
    dmix = _matmul("d_mix", dh1, wout_ab, NT, rows, OUT_AB, D_MODEL, tm=tm, tn=512, tk=1024)
    dproj0, d_ret_norm = _ret_bwd(proj0, cosf, sinf, rtab, ret_norm_w, o_ret, dmix, ret_states)
    dproj0, dt_glu, dg_direct = _s5_gate_bwd(dmix, g_s5, t_glu, proj0, dproj0)
    g_wglu = _matmul("d_w_glu", g_s5, dt_glu, TN, S5_W, S5_W, rows, tm=1024, tn=1024, tk=tmk)
    dy_s5 = _matmul("d_y_s5", dt_glu, wglu, NT, rows, S5_W, S5_W, tm=tm, tn=512, tk=S5_W,
                    extras=[(dg_direct, (tm, 512), lambda i, j, kk: (i, j)),
                            (y_s5, (tm, 512), lambda i, j, kk: (i, j))],
                    epilogue=lambda acc, dg, yv: (acc + dg) * _gelu_grad(yv))
    wc_cols = IN_C // N_SHARD
    wc_win = (wc_cols // 128 + 1) * 128
    rs1_names = ["w_out_ab", "w_in_c", "w_out_c", "w_glu"]
    rs1_kinds = ["row", "colw", "row", "row"]
    rs1_shapes = [w_out_ab.shape[1:], (D_MODEL, wc_win), w_out_c.shape[1:], s5_w_glu.shape[1:]]
    rs1_plan = _rs_pair_plan(rs1_kinds, rs1_shapes)
    rs1_land = [_empty_hbm((N_SHARD, r // 2, cc), F32) for (r, cc) in rs1_shapes]
    p_send, p_recv, p_bufs, dy_s5 = _copies_start("rs1_pair_start", [g_wout_ab, g_wc, g_wout_c, g_wglu] + rs1_land,
                                                  N_SHARD * 4, rs1_plan, dy_s5)
    dproj0, dbr_d, dbi_d, dcr_d, dci_d, dar_p, dai_p, dd_p = _s5_bwd(proj0, dy_s5, ab, bd_b, bd_c, s5_d,
                                                                     (s5_er, s5_ei), dproj0)
    p_bufs = _copies_wait("rs1_pair_wait", p_send, p_recv, p_bufs, rs1_plan, dproj0)
    rs1_pairs = [_rs_pair_add("rs_pair_add_" + nm, g, t, kd, ss, c_arr)
                 for nm, g, t, kd, ss in zip(rs1_names, p_bufs[:4], p_bufs[4:], rs1_kinds, rs1_shapes)]
    rs1_chip_plan = _rs_chip_plan(4)
    rs1_land2 = [_empty_hbm((3, r // 2, cc), BF16) for (r, cc) in rs1_shapes]
    c_send, c_recv, c_bufs, dproj0 = _copies_start("rs1_chip_start", [p[1] for p in rs1_pairs] + rs1_land2, 12,
                                                   rs1_chip_plan, dproj0)
    g_wab = _matmul("d_w_in_ab", hn0, dproj0, TN, D_MODEL, IN_AB, rows, tm=1024, tn=1024, tk=tmk)
    rs2_shapes = [w_in_ab.shape[1:]]
    rs2_plan = _rs_pair_plan(["col"], rs2_shapes)
    rs2_land = [_empty_hbm((N_SHARD, rs2_shapes[0][0] // 2, rs2_shapes[0][1]), F32)]
    q_send, q_recv, q_bufs, dproj0 = _copies_start("rs2_pair_start", [g_wab] + rs2_land, N_SHARD, rs2_plan, dproj0)
    dhn0 = _matmul("d_hn0_a", dproj0, wab, NT, tm, D_MODEL, IN_AB, tm=tm, tn=512, tk=2048,
                   out_shape=jax.ShapeDtypeStruct((rows, D_MODEL), F32))
    q_bufs = _copies_wait("rs2_pair_wait", q_send, q_recv, q_bufs, rs2_plan, dhn0)
    rs2_pair = _rs_pair_add("rs_pair_add_w_in_ab", q_bufs[0], q_bufs[1], "col", rs2_shapes[0], c_arr)
    rs2_chip_plan = _rs_chip_plan(1)
    rs2_land2 = [_empty_hbm((3, rs2_shapes[0][0] // 2, rs2_shapes[0][1]), BF16)]
    r_send, r_recv, r_bufs, dhn0 = _copies_start("rs2_chip_start", [rs2_pair[1]] + rs2_land2, 3, rs2_chip_plan, dhn0)
    if rows > tm:
        dhn0 = _matmul("d_hn0_b", dproj0, wab, NT, rows - tm, D_MODEL, IN_AB, tm=tm, tn=512, tk=2048, a_off=(1, 0),
                       into=(dhn0, 1, 0))
    grad_x, d_meta, d_norm_ab = _rms_bwd_embed(dhn0, h0, norm_ab_w, dh1)
    c_bufs = _copies_wait("rs1_chip_wait", c_send, c_recv, c_bufs, rs1_chip_plan, grad_x)
    grad_x = grad_x[None]
    rs1_halves = [_rs_chip_add("rs_chip_add_" + nm, p[0], t, ss, mine_c)
                  for nm, p, t, ss in zip(rs1_names, rs1_pairs, c_bufs[4:], rs1_shapes)]
    share_plan = _rs_share_plan(rs1_shapes)
    s_send, s_recv, s_bufs, d_meta = _copies_start("rs1_share_start", rs1_halves, len(rs1_halves), share_plan, d_meta)

    d_ab_re = jnp.sum(dar_p, axis=1).reshape(S5_G, S5_P)
    d_ab_im = jnp.sum(dai_p, axis=1).reshape(S5_G, S5_P)
    small_local = [loss_dev, d_meta, d_norm_ab, d_ret_norm.reshape(1, RET_W), d_ab_re, d_ab_im,
                   _bdiag_in_extract(dbr_d), _bdiag_in_extract(dbi_d),
                   _bdiag_out_extract(dcr_d), _bdiag_out_extract(dci_d),
                   jnp.sum(dd_p, axis=1).reshape(1, S5_W), d_norm_c, g_wgate[:GLA_RANK],
                   d_bgate.reshape(1, GLA_QK), d_gla_norm.reshape(1, GLA_W), d_final]
    small_shapes = [a.shape for a in small_local]
    summed_buf = _allreduce_small(_pack(small_local))
    summed = _unpack(summed_buf, small_shapes)
    g_w_out_ab, g_w_in_c, g_w_out_c, g_w_glu = _copies_wait("rs1_share_wait", s_send, s_recv, s_bufs, share_plan,
                                                             summed_buf)
    g_w_in_c = lax.dynamic_slice(g_w_in_c, (0, (wc_cols % 128) * mine), (D_MODEL, wc_cols))
    (loss, g_meta_f, g_norm_ab, g_ret_norm, g_ab_re, g_ab_im, g_bb_re, g_bb_im, g_c_re, g_c_im, g_d,
     g_norm_c_f, g_wgate_f, g_bgate_f, g_gla_norm_f, g_final) = summed
    _, s5_vjp = jax.vjp(_s5_discretize, s5_lam_re[0], s5_lam_im[0], s5_log_dt[0], s5_b_re[0], s5_b_im[0])
    g_lam_re, g_lam_im, g_log_dt, g_b_re, g_b_im = s5_vjp((g_ab_re, g_ab_im, g_bb_re, g_bb_im))

    def take(a, width):
        return lax.dynamic_slice_in_dim(a, mine * width, width, axis=1)

    grads = {
        "meta": take(g_meta_f, q4), "norm_ab_w": g_norm_ab, "ret_norm_w": g_ret_norm,
        "s5_lam_re": g_lam_re[None], "s5_lam_im": g_lam_im[None], "s5_log_dt": g_log_dt[None],
        "s5_b_re": g_b_re[None], "s5_b_im": g_b_im[None], "s5_c_re": g_c_re[None], "s5_c_im": g_c_im[None],
        "s5_d": g_d, "s5_w_glu": g_w_glu[None], "w_out_ab": g_w_out_ab[None], "norm_c_w": take(g_norm_c_f, q4),
        "w_in_c": g_w_in_c[None], "gla_w_gate": take(g_wgate_f, g4)[None], "gla_b_gate": take(g_bgate_f, g4),
        "gla_norm_w": take(g_gla_norm_f, q4), "w_out_c": g_w_out_c[None], "final_norm_w": g_final.reshape(D_MODEL),
    }
    weights = dict(meta=meta, norm_ab_w=norm_ab_w, w_in_ab=w_in_ab, ret_norm_w=ret_norm_w, s5_lam_re=s5_lam_re,
                   s5_lam_im=s5_lam_im, s5_log_dt=s5_log_dt, s5_b_re=s5_b_re, s5_b_im=s5_b_im, s5_c_re=s5_c_re,
                   s5_c_im=s5_c_im, s5_d=s5_d, s5_w_glu=s5_w_glu, w_out_ab=w_out_ab, norm_c_w=norm_c_w,
                   w_in_c=w_in_c, gla_w_gate=gla_w_gate, gla_b_gate=gla_b_gate, gla_norm_w=gla_norm_w,
                   w_out_c=w_out_c, final_norm_w=final_norm_w)
    m_in = dict(meta=m_meta, norm_ab_w=m_norm_ab_w, w_in_ab=m_w_in_ab, ret_norm_w=m_ret_norm_w,
                s5_lam_re=m_s5_lam_re, s5_lam_im=m_s5_lam_im, s5_log_dt=m_s5_log_dt, s5_b_re=m_s5_b_re,
                s5_b_im=m_s5_b_im, s5_c_re=m_s5_c_re, s5_c_im=m_s5_c_im, s5_d=m_s5_d, s5_w_glu=m_s5_w_glu,
                w_out_ab=m_w_out_ab, norm_c_w=m_norm_c_w, w_in_c=m_w_in_c, gla_w_gate=m_gla_w_gate,
                gla_b_gate=m_gla_b_gate, gla_norm_w=m_gla_norm_w, w_out_c=m_w_out_c, final_norm_w=m_final_norm_w)
    v_in = dict(meta=v_meta, norm_ab_w=v_norm_ab_w, w_in_ab=v_w_in_ab, ret_norm_w=v_ret_norm_w,
                s5_lam_re=v_s5_lam_re, s5_lam_im=v_s5_lam_im, s5_log_dt=v_s5_log_dt, s5_b_re=v_s5_b_re,
                s5_b_im=v_s5_b_im, s5_c_re=v_s5_c_re, s5_c_im=v_s5_c_im, s5_d=v_s5_d, s5_w_glu=v_s5_w_glu,
                w_out_ab=v_w_out_ab, norm_c_w=v_norm_c_w, w_in_c=v_w_in_c, gla_w_gate=v_gla_w_gate,
                gla_b_gate=v_gla_b_gate, gla_norm_w=v_gla_norm_w, w_out_c=v_w_out_c, final_norm_w=v_final_norm_w)
    order = list(weights)
    big_names = ["s5_w_glu", "w_out_ab", "w_in_c", "w_out_c", "w_in_ab"]
    small_names = [nm for nm in order if nm not in big_names]
    delta, new_m, new_v = {}, {}, {}

    def big_update(nm):
        shp = weights[nm].shape
        d2, m2, v2 = _adamw("adamw_" + nm, weights[nm][0], grads[nm][0], m_in[nm][0], v_in[nm][0])
        delta[nm], new_m[nm], new_v[nm] = d2.reshape(shp), m2.reshape(shp), v2.reshape(shp)

    for nm in big_names[:-1]:
        big_update(nm)
    sshapes = [weights[nm].shape for nm in small_names]
    d2, m2, v2 = _adamw("adamw_small", _pack([weights[nm] for nm in small_names]),
                        _pack([grads[nm] for nm in small_names]), _pack([m_in[nm] for nm in small_names]),
                        _pack([v_in[nm] for nm in small_names]))
    for nm, dd, mm, vv in zip(small_names, _unpack(d2, sshapes), _unpack(m2, sshapes), _unpack(v2, sshapes)):
        delta[nm], new_m[nm], new_v[nm] = dd, mm, vv
    r_bufs = _copies_wait("rs2_chip_wait", r_send, r_recv, r_bufs, rs2_chip_plan,
                          [v2] + [new_v[nm] for nm in big_names[:-1]])
    rs2_half = _rs_chip_add("rs_chip_add_w_in_ab", rs2_pair[0], r_bufs[1], rs2_shapes[0], mine_c)
    share2_plan = _rs_share_plan(rs2_shapes)
    t_send, t_recv, t_bufs, v2 = _copies_start("rs2_share_start", [rs2_half], 1, share2_plan, v2)
    grads["w_in_ab"] = _copies_wait("rs2_share_wait", t_send, t_recv, t_bufs, share2_plan, v2)[0][None]
    big_update("w_in_ab")
    grads = {nm: grads[nm].reshape(weights[nm].shape) for nm in order}
    return (loss.reshape(()), grad_x, *[grads[nm] for nm in order], *[delta[nm] for nm in order],
            *[new_m[nm] for nm in order], *[new_v[nm] for nm in order])
```

```python
import math

import jax
import jax.numpy as jnp
from jax import lax
from jax.experimental import pallas as pl
from jax.experimental.pallas import tpu as pltpu

F32 = jnp.float32
BF16 = jnp.bfloat16
MESH = pl.DeviceIdType.MESH

D_MODEL = 2048
N_META = 16
CHUNK = 128
SUB = 16
NSUB = CHUNK // SUB
PAD = CHUNK - N_META
EPS = 1e-6

RET_HEADS = 8
RET_DK = 128
RET_DV = 256
RET_QK = RET_HEADS * RET_DK
RET_W = RET_HEADS * RET_DV
ROPE_BASE = 10000.0

S5_W = 1024
S5_GH = 16
S5_G = S5_W // S5_GH
S5_P = 64
S5_TG = 8
S5_NT = S5_G // S5_TG
S5_TU = S5_TG * S5_GH
S5_TS = S5_TG * S5_P
S5_FWD_TILES = 4
S5_BWD_TILES = 1

GLA_HEADS = 4
GLA_DK = 256
GLA_DV = 512
GLA_QK = GLA_HEADS * GLA_DK
GLA_W = GLA_HEADS * GLA_DV
GLA_RANK = 16
GLA_TAU = 16.0

IN_AB = 2 * RET_QK + 2 * RET_W + 2 * S5_W
OUT_AB = RET_W + S5_W
IN_C = 2 * GLA_QK + 2 * GLA_W + GLA_RANK
IN_C_PAD = 2 * GLA_QK + 2 * GLA_W + 128

ADAM_LR = 0.001
ADAM_B1 = 0.9
ADAM_B2 = 0.999
ADAM_EPS = 1e-08
ADAM_WD = 0.01
ADAM_STEP = 10

N_SHARD = 4
SMALL_COLS = 512

NN = (((1,), (0,)), ((), ()))
NT = (((1,), (1,)), ((), ()))
TN = (((0,), (0,)), ((), ()))


def _dot(a, b, dims=NN):
    return lax.dot_general(a.astype(BF16), b.astype(BF16), dims, preferred_element_type=F32)


def _mo(v, m):
    return v if isinstance(v, int) else pl.multiple_of(v, m)


def _sigmoid(x):
    return 1.0 / (1.0 + jnp.exp(-x))


def _row_tile(rows, cap):
    n = rows // CHUNK
    best = 1
    for d in range(1, n + 1):
        if n % d == 0 and d * CHUNK <= cap:
            best = d
    return best * CHUNK


def _matmul(name, a, b, dims, m, n, k, *, tm, tn, tk, out_dtype=F32, a_off=(0, 0), b_off=(0, 0),
            extras=(), epilogue=None, out_shape=None, out_spec=None, segs=None, into=None):
    if segs is None:
        segs = [(a, a_off, b, b_off, k, tk)]
    assert m % tm == 0 and n % tn == 0, (name, m, n, tm, tn)
    starts, counts = [], []
    nk = 0
    for (_, _, _, _, ks, tks) in segs:
        assert ks % tks == 0, (name, ks, tks)
        starts.append(nk)
        counts.append(ks // tks)
        nk += ks // tks
    in_specs, operands = [], []
    for s, (sa, (ar, ac), sb, (br, bc), _, tks) in enumerate(segs):
        def kpos(kk, st=starts[s], cnt=counts[s]):
            return jnp.clip(kk - st, 0, cnt - 1) if len(segs) > 1 else kk

        if dims == NN:
            a_spec = pl.BlockSpec((tm, tks), lambda i, j, kk, p=kpos, r=ar, c=ac: (i + r, p(kk) + c))
            b_spec = pl.BlockSpec((tks, tn), lambda i, j, kk, p=kpos, r=br, c=bc: (p(kk) + r, j + c))
        elif dims == NT:
            a_spec = pl.BlockSpec((tm, tks), lambda i, j, kk, p=kpos, r=ar, c=ac: (i + r, p(kk) + c))
            b_spec = pl.BlockSpec((tn, tks), lambda i, j, kk, p=kpos, r=br, c=bc: (j + r, p(kk) + c))
        else:
            a_spec = pl.BlockSpec((tks, tm), lambda i, j, kk, p=kpos, r=ar, c=ac: (p(kk) + r, i + c))
            b_spec = pl.BlockSpec((tks, tn), lambda i, j, kk, p=kpos, r=br, c=bc: (p(kk) + r, j + c))
        in_specs += [a_spec, b_spec]
        operands += [sa, sb]
    n_seg = len(segs)
    n_extra = len(extras)
    if out_shape is None:
        out_shape = jax.ShapeDtypeStruct((m, n), out_dtype)

    def body(*refs):
        e_refs = refs[2 * n_seg:2 * n_seg + n_extra]
        n_in = 2 * n_seg + n_extra + (1 if into is not None else 0)
        o_ref = refs[n_in]
        if nk == 1:
            part = _dot(refs[0][...], refs[1][...], dims)
            if epilogue is not None:
                part = epilogue(part, *[e[...] for e in e_refs])
            o_ref[...] = part.astype(o_ref.dtype)
            return
        acc_ref = refs[n_in + 1]
        kk = pl.program_id(2)

        @pl.when(kk == 0)
        def _():
            acc_ref[...] = jnp.zeros_like(acc_ref)

        if n_seg == 1:
            acc_ref[...] += _dot(refs[0][...], refs[1][...], dims)
        else:
            for s in range(n_seg):
                @pl.when(jnp.logical_and(kk >= starts[s], kk < starts[s] + counts[s]))
                def _(s=s):
                    acc_ref[...] += _dot(refs[2 * s][...], refs[2 * s + 1][...], dims)

        @pl.when(kk == nk - 1)
        def _():
            acc = acc_ref[...]
            if epilogue is not None:
                acc = epilogue(acc, *[e[...] for e in e_refs])
            o_ref[...] = acc.astype(o_ref.dtype)

    if out_spec is None:
        out_spec = pl.BlockSpec((tm, tn), lambda i, j, kk: (i, j))
    in_specs += [pl.BlockSpec(bs, im) for (_, bs, im) in extras]
    operands += [e for (e, _, _) in extras]
    aliases = {}
    if into is not None:
        dest, ro, co = into
        out_shape = jax.ShapeDtypeStruct(dest.shape, dest.dtype)
        out_spec = pl.BlockSpec((tm, tn), lambda i, j, kk: (i + ro, j + co))
        aliases = {len(operands): 0}
        in_specs.append(ANY)
        operands.append(dest)
    return pl.pallas_call(
        body, name=name, grid=(m // tm, n // tn, nk),
        in_specs=in_specs, out_specs=out_spec, out_shape=out_shape, input_output_aliases=aliases,
        scratch_shapes=[] if nk == 1 else [pltpu.VMEM((tm, tn), F32)],
        compiler_params=pltpu.CompilerParams(dimension_semantics=("parallel", "parallel", "arbitrary")),
    )(*operands)


def _rms_fwd(name, h, w):
    rows, d = h.shape
    tm = _row_tile(rows, 512)

    def body(h_ref, w_ref, o_ref):
        x = h_ref[...]
        r = lax.rsqrt(jnp.mean(x * x, axis=-1, keepdims=True) + EPS)
        o_ref[...] = (x * r * w_ref[...]).astype(BF16)

    return pl.pallas_call(
        body, name=name, grid=(rows // tm,),
        in_specs=[pl.BlockSpec((tm, d), lambda i: (i, 0)), pl.BlockSpec((1, d), lambda i: (0, 0))],
        out_specs=pl.BlockSpec((tm, d), lambda i: (i, 0)),
        out_shape=jax.ShapeDtypeStruct((rows, d), BF16),
    )(h, w)


def _rms_bwd(name, dhn, h, w, dres):
    rows, d = h.shape
    tm = _row_tile(rows, 384)

    def body(g_ref, h_ref, w_ref, r_ref, dh_ref, dw_ref):
        i = pl.program_id(0)
        x = h_ref[...]
        r = lax.rsqrt(jnp.mean(x * x, axis=-1, keepdims=True) + EPS)
        xh = x * r
        g = g_ref[...]
        gw = g * w_ref[...]
        dh_ref[...] = r_ref[...] + r * (gw - xh * jnp.mean(gw * xh, axis=-1, keepdims=True))

        @pl.when(i == 0)
        def _():
            dw_ref[...] = jnp.zeros_like(dw_ref)

        dw_ref[...] += jnp.sum(g * xh, axis=0, keepdims=True)

    return pl.pallas_call(
        body, name=name, grid=(rows // tm,),
        in_specs=[pl.BlockSpec((tm, d), lambda i: (i, 0)), pl.BlockSpec((tm, d), lambda i: (i, 0)),
                  pl.BlockSpec((1, d), lambda i: (0, 0)), pl.BlockSpec((tm, d), lambda i: (i, 0))],
        out_specs=[pl.BlockSpec((tm, d), lambda i: (i, 0)), pl.BlockSpec((1, d), lambda i: (0, 0))],
        out_shape=[jax.ShapeDtypeStruct((rows, d), F32), jax.ShapeDtypeStruct((1, d), F32)],
    )(dhn, h, w, dres)


def _embed_norm(x, meta, w):
    seq, d = x.shape
    rows = seq + CHUNK

    def body(x_ref, m_ref, w_ref, h_ref, o_ref):
        i = pl.program_id(0)

        def emit(h):
            h_ref[...] = h
            r = lax.rsqrt(jnp.mean(h * h, axis=-1, keepdims=True) + EPS)
            o_ref[...] = (h * r * w_ref[...]).astype(BF16)

        @pl.when(i == 0)
        def _():
            emit(jnp.concatenate([jnp.zeros((PAD, d), F32), m_ref[...]], axis=0))

        @pl.when(i > 0)
        def _():
            emit(x_ref[...])

    blk = pl.BlockSpec((CHUNK, d), lambda i: (i, 0))
    return pl.pallas_call(
        body, name="embed_norm_ab", grid=(rows // CHUNK,),
        in_specs=[pl.BlockSpec((CHUNK, d), lambda i: (jnp.maximum(i - 1, 0), 0)),
                  pl.BlockSpec((N_META, d), lambda i: (0, 0)), pl.BlockSpec((1, d), lambda i: (0, 0))],
        out_specs=[blk, blk],
        out_shape=[jax.ShapeDtypeStruct((rows, d), F32), jax.ShapeDtypeStruct((rows, d), BF16)],
    )(x, meta, w)


def _rms_bwd_embed(dhn, h, w, dres):
    rows, d = h.shape
    seq = rows - CHUNK

    def body(g_ref, h_ref, w_ref, r_ref, gx_ref, gm_ref, dw_ref):
        i = pl.program_id(0)
        x = h_ref[...]
        r = lax.rsqrt(jnp.mean(x * x, axis=-1, keepdims=True) + EPS)
        xh = x * r
        g = g_ref[...]
        gw = g * w_ref[...]
        dh = r_ref[...] + r * (gw - xh * jnp.mean(gw * xh, axis=-1, keepdims=True))

        @pl.when(i == 0)
        def _():
            dw_ref[...] = jnp.zeros_like(dw_ref)
            gm_ref[...] = dh[PAD:]

        @pl.when(i > 0)
        def _():
            gx_ref[...] = dh

        dw_ref[...] += jnp.sum(g * xh, axis=0, keepdims=True)

    blk = pl.BlockSpec((CHUNK, d), lambda i: (i, 0))
    return pl.pallas_call(
        body, name="norm_ab_bwd", grid=(rows // CHUNK,),
        in_specs=[blk, blk, pl.BlockSpec((1, d), lambda i: (0, 0)), blk],
        out_specs=[pl.BlockSpec((CHUNK, d), lambda i: (jnp.maximum(i - 1, 0), 0)),
                   pl.BlockSpec((N_META, d), lambda i: (0, 0)), pl.BlockSpec((1, d), lambda i: (0, 0))],
        out_shape=[jax.ShapeDtypeStruct((seq, d), F32), jax.ShapeDtypeStruct((N_META, d), F32),
                   jax.ShapeDtypeStruct((1, d), F32)],
    )(dhn, h, w, dres)


def _final_loss(h2, w, target):
    rows, d = h2.shape

    def body(h_ref, w_ref, t_ref, loss_ref, dh_ref, dw_ref):
        i = pl.program_id(0)

        @pl.when(i == 0)
        def _():
            loss_ref[...] = jnp.zeros_like(loss_ref)
            dw_ref[...] = jnp.zeros_like(dw_ref)
            dh_ref[...] = jnp.zeros_like(dh_ref)

        @pl.when(i > 0)
        def _():
            x = h_ref[...]
            r = lax.rsqrt(jnp.mean(x * x, axis=-1, keepdims=True) + EPS)
            xh = x * r
            wv = w_ref[...]
            err = xh * wv - t_ref[...]
            loss_ref[...] += 0.5 * jnp.sum(jnp.mean(err * err, axis=-1, keepdims=True), axis=0, keepdims=True)
            g = err * (1.0 / d)
            gw = g * wv
            dh_ref[...] = r * (gw - xh * jnp.mean(gw * xh, axis=-1, keepdims=True))
            dw_ref[...] += jnp.sum(g * xh, axis=0, keepdims=True)

    return pl.pallas_call(
        body, name="final_loss", grid=(rows // CHUNK,),
        in_specs=[pl.BlockSpec((CHUNK, d), lambda i: (i, 0)), pl.BlockSpec((1, d), lambda i: (0, 0)),
                  pl.BlockSpec((CHUNK, d), lambda i: (jnp.maximum(i - 1, 0), 0))],
        out_specs=[pl.BlockSpec((1, 1), lambda i: (0, 0)), pl.BlockSpec((CHUNK, d), lambda i: (i, 0)),
                   pl.BlockSpec((1, d), lambda i: (0, 0))],
        out_shape=[jax.ShapeDtypeStruct((1, 1), F32), jax.ShapeDtypeStruct((rows, d), F32),
                   jax.ShapeDtypeStruct((1, d), F32)],
    )(h2, w, target)


def _gate_fwd(o, z, w):
    rs = lax.rsqrt(jnp.mean(o * o, axis=-1, keepdims=True) + EPS)
    return o * rs * w * (z * _sigmoid(z))


def _gate_bwd(dout, o, z, w):
    rs = lax.rsqrt(jnp.mean(o * o, axis=-1, keepdims=True) + EPS)
    yn = o * rs
    sg = _sigmoid(z)
    sil = z * sg
    dsil = sg * (1.0 + z * (1.0 - sg))
    dz = dout * yn * w * dsil
    dyn = dout * w * sil
    dw = jnp.sum(dout * yn * sil, axis=0, keepdims=True)
    do = rs * (dyn - yn * jnp.mean(dyn * yn, axis=-1, keepdims=True))
    return do, dz, dw


def _rope(t, cosf, sinf):
    return t * cosf + pltpu.roll(t, RET_DK // 2, 1) * sinf


def _rope_t(d, cosf, sinf):
    return d * cosf + pltpu.roll(d * sinf, RET_DK // 2, 1)


def _ret_tables():
    log_g = jnp.log1p(-jnp.exp2(-5.0 - jnp.arange(RET_HEADS, dtype=F32)))
    idx = jnp.arange(CHUNK, dtype=F32)
    diff = idx[:, None] - idx[None, :]
    decay = jnp.where(diff >= 0, jnp.exp(log_g[:, None, None] * jnp.maximum(diff, 0.0)), 0.0)
    kw = jnp.exp(log_g[:, None] * (CHUNK - 1 - idx))
    qw = jnp.exp(log_g[:, None] * (idx + 1.0))
    gch = jnp.exp(log_g * CHUNK)
    kw = jnp.broadcast_to(kw[:, :, None], (RET_HEADS, CHUNK, RET_DK))
    qw = jnp.broadcast_to(qw[:, :, None], (RET_HEADS, CHUNK, RET_DK))
    gch = jnp.broadcast_to(gch[:, None, None], (RET_HEADS, 1, RET_DV))
    return decay, kw, qw, gch


def _rope_tables(rows):
    pos = jnp.arange(rows, dtype=F32) - float(PAD)
    inv_freq = jnp.power(ROPE_BASE, -jnp.arange(0, RET_DK, 2, dtype=F32) / RET_DK)
    ang = pos[:, None] * inv_freq[None, :]
    cos, sin = jnp.cos(ang), jnp.sin(ang)
    return jnp.concatenate([cos, cos], axis=1), jnp.concatenate([-sin, sin], axis=1)


RET_HB = 8
RET_QB = RET_HB * RET_DK
RET_VB = RET_HB * RET_DV


def _ret_in_specs(rev, nc):
    def cn(n):
        return (nc - 1 - n) if rev else n
    kb = RET_QK // RET_QB
    vb = 2 * RET_QK // RET_VB
    zb = (2 * RET_QK + RET_W) // RET_VB
    return [
        pl.BlockSpec((CHUNK, RET_QB), lambda h, n: (cn(n), h)),
        pl.BlockSpec((CHUNK, RET_QB), lambda h, n: (cn(n), kb + h)),
        pl.BlockSpec((CHUNK, RET_VB), lambda h, n: (cn(n), vb + h)),
        pl.BlockSpec((CHUNK, RET_VB), lambda h, n: (cn(n), zb + h)),
        pl.BlockSpec((CHUNK, RET_DK), lambda h, n: (cn(n), 0)),
        pl.BlockSpec((CHUNK, RET_DK), lambda h, n: (cn(n), 0)),
        pl.BlockSpec((RET_HB, CHUNK, CHUNK), lambda h, n: (h, 0, 0)),
        pl.BlockSpec((RET_HB, CHUNK, RET_DK), lambda h, n: (h, 0, 0)),
        pl.BlockSpec((RET_HB, CHUNK, RET_DK), lambda h, n: (h, 0, 0)),
        pl.BlockSpec((RET_HB, 1, RET_DV), lambda h, n: (h, 0, 0)),
        pl.BlockSpec((1, RET_VB), lambda h, n: (0, h)),
    ]


def _ret_fwd(proj, cosf, sinf, tables, normw):
    rows = proj.shape[0]
    nc = rows // CHUNK
    decay, kw, qw, gch = tables

    def body(q_ref, k_ref, v_ref, z_ref, cos_ref, sin_ref, dm_ref, kw_ref, qw_ref, g_ref, w_ref,
             o_ref, oa_ref, st_ref, s_scr):
        n = pl.program_id(1)

        @pl.when(n == 0)
        def _():
            s_scr[...] = jnp.zeros_like(s_scr)

        cosv, sinv = cos_ref[...], sin_ref[...]
        for hh in range(RET_HB):
            qc = slice(hh * RET_DK, (hh + 1) * RET_DK)
            vc = slice(hh * RET_DV, (hh + 1) * RET_DV)
            q = _rope(q_ref[:, qc], cosv, sinv)
            k = _rope(k_ref[:, qc], cosv, sinv) * (RET_DK ** -0.5)
            v = v_ref[:, vc]
            s = s_scr[hh]
            st_ref[hh, 0] = s.astype(BF16)
            a = _dot(q, k, NT) * dm_ref[hh]
            o = _dot(a, v) + _dot(q * qw_ref[hh], s)
            s_scr[hh] = s * g_ref[hh] + _dot(k * kw_ref[hh], v, TN)
            o_ref[:, vc] = o
            oa_ref[:, vc] = _gate_fwd(o, z_ref[:, vc], w_ref[:, vc]).astype(BF16)

    return pl.pallas_call(
        body, name="ret_fwd", grid=(RET_HEADS // RET_HB, nc),
        in_specs=_ret_in_specs(False, nc),
        out_specs=[pl.BlockSpec((CHUNK, RET_VB), lambda h, n: (n, h)),
                   pl.BlockSpec((CHUNK, RET_VB), lambda h, n: (n, h)),
                   pl.BlockSpec((RET_HB, 1, RET_DK, RET_DV), lambda h, n: (h, n, 0, 0))],
        out_shape=[jax.ShapeDtypeStruct((rows, RET_W), F32), jax.ShapeDtypeStruct((rows, RET_W), BF16),
                   jax.ShapeDtypeStruct((RET_HEADS, nc, RET_DK, RET_DV), BF16)],
        scratch_shapes=[pltpu.VMEM((RET_HB, RET_DK, RET_DV), F32)],
        compiler_params=pltpu.CompilerParams(dimension_semantics=("parallel", "arbitrary")),
    )(proj, proj, proj, proj, cosf, sinf, decay, kw, qw, gch, normw)


def _ret_bwd(proj, cosf, sinf, tables, normw, o_ret, dmix, states):
    assert RET_HB == RET_HEADS
    rows = proj.shape[0]
    nc = rows // CHUNK
    decay, kw, qw, gch = tables
    ret_cols = 2 * RET_QK + 2 * RET_W

    def rn(n):
        return nc - 1 - n

    def body(q_ref, k_ref, v_ref, z_ref, cos_ref, sin_ref, dm_ref, kw_ref, qw_ref, g_ref, w_ref,
             o_ref, do_ref, st_ref, dp_ref, dw_ref, ds_scr):
        n = pl.program_id(1)
        dq_ref = dp_ref.at[:, 0:RET_QK]
        dk_ref = dp_ref.at[:, RET_QK:2 * RET_QK]
        dv_ref = dp_ref.at[:, 2 * RET_QK:2 * RET_QK + RET_W]
        dz_ref = dp_ref.at[:, 2 * RET_QK + RET_W:ret_cols]

        @pl.when(n == 0)
        def _():
            ds_scr[...] = jnp.zeros_like(ds_scr)
            dw_ref[...] = jnp.zeros_like(dw_ref)

        cosv, sinv = cos_ref[...], sin_ref[...]
        for hh in range(RET_HB):
            qc = slice(hh * RET_DK, (hh + 1) * RET_DK)
            vc = slice(hh * RET_DV, (hh + 1) * RET_DV)
            q = _rope(q_ref[:, qc], cosv, sinv)
            k = _rope(k_ref[:, qc], cosv, sinv) * (RET_DK ** -0.5)
            v = v_ref[:, vc]
            do, dz, dw = _gate_bwd(do_ref[:, vc], o_ref[:, vc], z_ref[:, vc], w_ref[:, vc])
            dz_ref[:, vc] = dz.astype(BF16)
            dw_ref[hh] += dw
            dm = dm_ref[hh]
            s = st_ref[hh, 0]
            g1 = ds_scr[hh]
            p = _dot(q, k, NT) * dm
            kwv = k * kw_ref[hh]
            qwv = q * qw_ref[hh]
            dp = _dot(do, v, NT)
            da = dp * dm
            dv = _dot(p, do, TN) + _dot(kwv, g1)
            dq = _dot(da, k) + _dot(do, s, NT) * qw_ref[hh]
            dk = _dot(da, q, TN) + _dot(v, g1, NT) * kw_ref[hh]
            ds_scr[hh] = g1 * g_ref[hh] + _dot(qwv, do, TN)
            dv_ref[:, vc] = dv.astype(BF16)
            dq_ref[:, qc] = _rope_t(dq, cosv, sinv).astype(BF16)
            dk_ref[:, qc] = _rope_t(dk * (RET_DK ** -0.5), cosv, sinv).astype(BF16)

    in_specs = _ret_in_specs(True, nc) + [
        pl.BlockSpec((CHUNK, RET_VB), lambda h, n: (rn(n), h)),
        pl.BlockSpec((CHUNK, RET_VB), lambda h, n: (rn(n), h)),
        pl.BlockSpec((RET_HB, 1, RET_DK, RET_DV), lambda h, n: (h, rn(n), 0, 0)),
    ]
    return pl.pallas_call(
        body, name="ret_bwd", grid=(RET_HEADS // RET_HB, nc),
        in_specs=in_specs,
        out_specs=[pl.BlockSpec((CHUNK, ret_cols), lambda h, n: (rn(n), 0)),
                   pl.BlockSpec((RET_HB, 1, RET_DV), lambda h, n: (h, 0, 0))],
        out_shape=[jax.ShapeDtypeStruct((rows, IN_AB), BF16), jax.ShapeDtypeStruct((RET_HEADS, 1, RET_DV), F32)],
        scratch_shapes=[pltpu.VMEM((RET_HB, RET_DK, RET_DV), F32)],
        compiler_params=pltpu.CompilerParams(dimension_semantics=("parallel", "arbitrary")),
    )(proj, proj, proj, proj, cosf, sinf, decay, kw, qw, gch, normw, o_ret, dmix, states)


def _s5_discretize(lam_re, lam_im, log_dt, b_re, b_im):
    dt = jnp.exp(log_dt)[:, None]
    mag = jnp.exp(lam_re * dt)
    ab_re, ab_im = mag * jnp.cos(lam_im * dt), mag * jnp.sin(lam_im * dt)
    den = lam_re * lam_re + lam_im * lam_im
    nr, ni = ab_re - 1.0, ab_im
    f_re = (nr * lam_re + ni * lam_im) / den
    f_im = (ni * lam_re - nr * lam_im) / den
    bb_re = f_re[..., None] * b_re - f_im[..., None] * b_im
    bb_im = f_re[..., None] * b_im + f_im[..., None] * b_re
    return ab_re, ab_im, bb_re, bb_im


def _bdiag_in(bb):
    t = bb.reshape(S5_NT, S5_TG, S5_P, S5_GH).transpose(0, 1, 3, 2)
    eye = jnp.eye(S5_TG, dtype=bb.dtype)
    full = t[:, :, :, None, :] * eye[None, :, None, :, None]
    return full.reshape(S5_NT, S5_TU, S5_TS)


def _bdiag_in_extract(dense):
    t = dense.reshape(S5_NT, S5_TG, S5_GH, S5_TG, S5_P)
    diag = jnp.stack([t[:, g, :, g, :] for g in range(S5_TG)], axis=1)
    return diag.transpose(0, 1, 3, 2).reshape(S5_G, S5_P, S5_GH)


def _bdiag_out(c):
    t = c.reshape(S5_NT, S5_TG, S5_GH, S5_P).transpose(0, 1, 3, 2)
    eye = jnp.eye(S5_TG, dtype=c.dtype)
    full = t[:, :, :, None, :] * eye[None, :, None, :, None]
    return full.reshape(S5_NT, S5_TS, S5_TU)


def _bdiag_out_extract(dense):
    t = dense.reshape(S5_NT, S5_TG, S5_P, S5_TG, S5_GH)
    diag = jnp.stack([t[:, g, :, g, :] for g in range(S5_TG)], axis=1)
    return diag.transpose(0, 1, 3, 2).reshape(S5_G, S5_GH, S5_P)


def _cmul(ar, ai, br, bi):
    return ar * br - ai * bi, ar * bi + ai * br


S5_SEG = 8
S5_STEPS = CHUNK // S5_SEG


def _seg_perm(x):
    c = x.shape[1]
    return jnp.swapaxes(x.reshape(S5_SEG, S5_STEPS, c), 0, 1).reshape(CHUNK, c)


def _seg_unperm(x):
    c = x.shape[1]
    return jnp.swapaxes(x.reshape(S5_STEPS, S5_SEG, c), 0, 1).reshape(CHUNK, c)


def _rows(x, p):
    return x[p * S5_SEG:(p + 1) * S5_SEG]


def _s5_tables(ar, ai, tr_scr, ti_scr, wfr_scr, wfi_scr, wbr_scr, wbi_scr):
    row = lax.broadcasted_iota(jnp.int32, (S5_SEG, 1), 0)
    a8r = jnp.broadcast_to(ar, (S5_SEG, S5_TS))
    a8i = jnp.broadcast_to(ai, (S5_SEG, S5_TS))
    pr, pi = a8r, a8i
    for p in range(S5_STEPS):
        tr_scr[p * S5_SEG:(p + 1) * S5_SEG, :] = pr
        ti_scr[p * S5_SEG:(p + 1) * S5_SEG, :] = pi
        if p < S5_STEPS - 1:
            pr, pi = _cmul(pr, pi, a8r, a8i)
    wr, wi = pr, pi
    sh = 1
    while sh < S5_SEG:
        keep = row >= sh
        sr = jnp.where(keep, pltpu.roll(wr, sh, 0), 1.0)
        si = jnp.where(keep, pltpu.roll(wi, sh, 0), 0.0)
        wr, wi = _cmul(wr, wi, sr, si)
        sh *= 2
    wfr_scr[...] = wr
    wfi_scr[...] = wi
    wr, wi = pr, -pi
    sh = 1
    while sh < S5_SEG:
        keep = row < S5_SEG - sh
        sr = jnp.where(keep, pltpu.roll(wr, S5_SEG - sh, 0), 1.0)
        si = jnp.where(keep, pltpu.roll(wi, S5_SEG - sh, 0), 0.0)
        wr, wi = _cmul(wr, wi, sr, si)
        sh *= 2
    wbr_scr[...] = wr
    wbi_scr[...] = wi


def _seg_scan(vr, vi, ar, ai, tr_scr, ti_scr, wr_scr, wi_scr, c0r, c0i, down):
    row = lax.broadcasted_iota(jnp.int32, (S5_SEG, 1), 0)
    sgn = 1.0 if down else -1.0
    order = list(range(S5_STEPS)) if down else list(range(S5_STEPS - 1, -1, -1))
    xr, xi = _rows(vr, order[0]), _rows(vi, order[0])
    loc = {order[0]: (xr, xi)}
    for p in order[1:]:
        mr, mi = _cmul(ar, sgn * ai, xr, xi)
        xr, xi = mr + _rows(vr, p), mi + _rows(vi, p)
        loc[p] = (xr, xi)
    last = S5_STEPS - 1
    mr, mi = tr_scr[last * S5_SEG:(last + 1) * S5_SEG, :], sgn * ti_scr[last * S5_SEG:(last + 1) * S5_SEG, :]
    er, ei = xr, xi
    sh = 1
    while sh < S5_SEG:
        if down:
            keep = row >= sh
            sr, si = pltpu.roll(er, sh, 0), pltpu.roll(ei, sh, 0)
        else:
            keep = row < S5_SEG - sh
            sr, si = pltpu.roll(er, S5_SEG - sh, 0), pltpu.roll(ei, S5_SEG - sh, 0)
        pr, pi = _cmul(mr, mi, jnp.where(keep, sr, 0.0), jnp.where(keep, si, 0.0))
        er, ei = er + pr, ei + pi
        mr, mi = _cmul(mr, mi, mr, mi)
        sh *= 2
    pr, pi = _cmul(wr_scr[...], wi_scr[...], c0r, c0i)
    er, ei = er + pr, ei + pi
    if down:
        nr = jnp.where(row == 0, c0r, pltpu.roll(er, 1, 0))
        ni = jnp.where(row == 0, c0i, pltpu.roll(ei, 1, 0))
    else:
        nr = jnp.where(row == S5_SEG - 1, c0r, pltpu.roll(er, S5_SEG - 1, 0))
        ni = jnp.where(row == S5_SEG - 1, c0i, pltpu.roll(ei, S5_SEG - 1, 0))
    out_r, out_i = [], []
    for p in range(S5_STEPS):
        q = p if down else S5_STEPS - 1 - p
        pr, pi = _cmul(tr_scr[q * S5_SEG:(q + 1) * S5_SEG, :], sgn * ti_scr[q * S5_SEG:(q + 1) * S5_SEG, :], nr, ni)
        out_r.append(loc[p][0] + pr)
        out_i.append(loc[p][1] + pi)
    return jnp.concatenate(out_r, axis=0), jnp.concatenate(out_i, axis=0), (nr, ni), (er, ei)


def _gelu(y):
    c = math.sqrt(2.0 / math.pi)
    return 0.5 * y * (1.0 + jnp.tanh(c * (y + 0.044715 * y * y * y)))


def _gelu_grad(y):
    c = math.sqrt(2.0 / math.pi)
    th = jnp.tanh(c * (y + 0.044715 * y * y * y))
    return 0.5 * (1.0 + th) + 0.5 * y * (1.0 - th * th) * c * (1.0 + 3.0 * 0.044715 * y * y)


def _s5_fwd(proj, ab, bd_b, bd_c, dvec):
    rows = proj.shape[0]
    nc = rows // CHUNK
    tps = S5_FWD_TILES
    ubw = tps * S5_TU
    ub = (2 * RET_QK + 2 * RET_W) // ubw
    ab_re, ab_im = ab
    bre, bim = bd_b
    cre, cim = bd_c

    def body(u_ref, ar_ref, ai_ref, bre_ref, bim_ref, cre_ref, cim_ref, d_ref,
             y_ref, g_ref, er_ref, ei_ref, tr_scr, ti_scr, wfr_scr, wfi_scr, wbr_scr, wbi_scr,
             cr_scr, ci_scr, er_scr, ei_scr):
        n = pl.program_id(1)
        for tt in range(tps):
            cols = slice(tt * S5_TU, (tt + 1) * S5_TU)
            ar, ai = ar_ref[tt], ai_ref[tt]
            trs, tis, wfr, wfi = tr_scr.at[tt], ti_scr.at[tt], wfr_scr.at[tt], wfi_scr.at[tt]

            @pl.when(n == 0)
            def _(tt=tt, ar=ar, ai=ai, trs=trs, tis=tis, wfr=wfr, wfi=wfi):
                _s5_tables(ar, ai, trs, tis, wfr, wfi, wbr_scr.at[tt], wbi_scr.at[tt])
                cr_scr[tt] = jnp.zeros((S5_SEG, S5_TS), F32)
                ci_scr[tt] = jnp.zeros((S5_SEG, S5_TS), F32)

            u = _seg_perm(u_ref[:, cols])
            c0r, c0i = cr_scr[tt], ci_scr[tt]
            er_ref[tt, 0] = c0r
            ei_ref[tt, 0] = c0i
            xr, xi, _, (er, ei) = _seg_scan(_dot(u, bre_ref[tt]), _dot(u, bim_ref[tt]), ar, ai, trs, tis,
                                            wfr, wfi, c0r, c0i, True)
            er_scr[tt] = er
            ei_scr[tt] = ei
            cr_scr[tt] = jnp.broadcast_to(er_scr[tt, S5_SEG - 1:S5_SEG, :], (S5_SEG, S5_TS))
            ci_scr[tt] = jnp.broadcast_to(ei_scr[tt, S5_SEG - 1:S5_SEG, :], (S5_SEG, S5_TS))
            y = _seg_unperm(_dot(xr, cre_ref[tt]) - _dot(xi, cim_ref[tt]) + d_ref[:, cols] * u)
            y_ref[:, cols] = y
            g_ref[:, cols] = _gelu(y).astype(BF16)

    vec = pl.BlockSpec((tps, 1, S5_TS), lambda t, n: (t, 0, 0))
    return pl.pallas_call(
        body, name="s5_fwd", grid=(S5_NT // tps, nc),
        in_specs=[pl.BlockSpec((CHUNK, ubw), lambda t, n: (n, ub + t)), vec, vec,
                  pl.BlockSpec((tps, S5_TU, S5_TS), lambda t, n: (t, 0, 0)),
                  pl.BlockSpec((tps, S5_TU, S5_TS), lambda t, n: (t, 0, 0)),
                  pl.BlockSpec((tps, S5_TS, S5_TU), lambda t, n: (t, 0, 0)),
                  pl.BlockSpec((tps, S5_TS, S5_TU), lambda t, n: (t, 0, 0)),
                  pl.BlockSpec((1, ubw), lambda t, n: (0, t))],
        out_specs=[pl.BlockSpec((CHUNK, ubw), lambda t, n: (n, t)),
                   pl.BlockSpec((CHUNK, ubw), lambda t, n: (n, t)),
                   pl.BlockSpec((tps, 1, 8, S5_TS), lambda t, n: (t, n, 0, 0)),
                   pl.BlockSpec((tps, 1, 8, S5_TS), lambda t, n: (t, n, 0, 0))],
        out_shape=[jax.ShapeDtypeStruct((rows, S5_W), F32), jax.ShapeDtypeStruct((rows, S5_W), BF16),
                   jax.ShapeDtypeStruct((S5_NT, nc, 8, S5_TS), F32),
                   jax.ShapeDtypeStruct((S5_NT, nc, 8, S5_TS), F32)],
        scratch_shapes=[pltpu.VMEM((tps, CHUNK, S5_TS), F32) for _ in range(2)]
        + [pltpu.VMEM((tps, S5_SEG, S5_TS), F32) for _ in range(8)],
        compiler_params=pltpu.CompilerParams(dimension_semantics=("parallel", "arbitrary")),
    )(proj, ab_re.reshape(S5_NT, 1, S5_TS), ab_im.reshape(S5_NT, 1, S5_TS), bre, bim, cre, cim, dvec)


def _s5_bwd(proj, dy, ab, bd_b, bd_c, dvec, entry, dproj):
    rows = proj.shape[0]
    nc = rows // CHUNK
    tps = S5_BWD_TILES
    ubw = tps * S5_TU
    ub = (2 * RET_QK + 2 * RET_W) // ubw
    ab_re, ab_im = ab
    bre, bim = bd_b
    cre, cim = bd_c
    er, ei = entry

    def rn(n):
        return nc - 1 - n

    def body(u_ref, dy_ref, ar_ref, ai_ref, bre_ref, bim_ref, cre_ref, cim_ref, d_ref, er_ref, ei_ref, dp_ref,
             du_ref, dbr_ref, dbi_ref, dcr_ref, dci_ref, dar_ref, dai_ref, dd_ref,
             tr_scr, ti_scr, wfr_scr, wfi_scr, wbr_scr, wbi_scr, gr_scr, gi_scr, er_scr, ei_scr):
        n = pl.program_id(1)

        @pl.when(n == 0)
        def _():
            gr_scr[...] = jnp.zeros_like(gr_scr)
            gi_scr[...] = jnp.zeros_like(gi_scr)
            for r in (dbr_ref, dbi_ref, dcr_ref, dci_ref, dar_ref, dai_ref, dd_ref):
                r[...] = jnp.zeros_like(r)

        for tt in range(tps):
            cols = slice(tt * S5_TU, (tt + 1) * S5_TU)
            ar, ai = ar_ref[tt], ai_ref[tt]
            trs, tis = tr_scr.at[tt], ti_scr.at[tt]

            @pl.when(n == 0)
            def _(tt=tt, ar=ar, ai=ai, trs=trs, tis=tis):
                _s5_tables(ar, ai, trs, tis, wfr_scr.at[tt], wfi_scr.at[tt], wbr_scr.at[tt], wbi_scr.at[tt])

            u = _seg_perm(u_ref[:, cols])
            dy = _seg_perm(dy_ref[:, cols])
            xr, xi, (pr, pi), _ = _seg_scan(_dot(u, bre_ref[tt]), _dot(u, bim_ref[tt]), ar, ai, trs, tis,
                                            wfr_scr.at[tt], wfi_scr.at[tt], er_ref[tt, 0], ei_ref[tt, 0], True)
            dcr_ref[tt] += _dot(xr, dy, TN)
            dci_ref[tt] -= _dot(xi, dy, TN)
            gr, gi, _, (er, ei) = _seg_scan(_dot(dy, cre_ref[tt], NT), -_dot(dy, cim_ref[tt], NT), ar, ai, trs, tis,
                                            wbr_scr.at[tt], wbi_scr.at[tt], gr_scr[tt], gi_scr[tt], False)
            er_scr[tt] = er
            ei_scr[tt] = ei
            gr_scr[tt] = jnp.broadcast_to(er_scr[tt, 0:1, :], (S5_SEG, S5_TS))
            gi_scr[tt] = jnp.broadcast_to(ei_scr[tt, 0:1, :], (S5_SEG, S5_TS))
            xpr = jnp.concatenate([pr, xr[:CHUNK - S5_SEG]], axis=0)
            xpi = jnp.concatenate([pi, xi[:CHUNK - S5_SEG]], axis=0)
            dar_ref[tt] += jnp.sum((xpr * gr + xpi * gi).reshape(S5_STEPS, S5_SEG, S5_TS), axis=0)
            dai_ref[tt] += jnp.sum((xpr * gi - xpi * gr).reshape(S5_STEPS, S5_SEG, S5_TS), axis=0)
            dbr_ref[tt] += _dot(u, gr, TN)
            dbi_ref[tt] += _dot(u, gi, TN)
            dd_ref[tt] += jnp.sum((dy * u).reshape(S5_STEPS, S5_SEG, S5_TU), axis=0)
            du = dy * d_ref[:, cols] + _dot(gr, bre_ref[tt], NT) + _dot(gi, bim_ref[tt], NT)
            du_ref[:, cols] = _seg_unperm(du).astype(BF16)

    vec = pl.BlockSpec((tps, 1, S5_TS), lambda t, n: (t, 0, 0))
    acc_b = pl.BlockSpec((tps, S5_TU, S5_TS), lambda t, n: (t, 0, 0))
    acc_c = pl.BlockSpec((tps, S5_TS, S5_TU), lambda t, n: (t, 0, 0))
    acc_a = pl.BlockSpec((tps, 8, S5_TS), lambda t, n: (t, 0, 0))
    ent = pl.BlockSpec((tps, 1, 8, S5_TS), lambda t, n: (t, rn(n), 0, 0))
    return pl.pallas_call(
        body, name="s5_bwd", grid=(S5_NT // tps, nc),
        in_specs=[pl.BlockSpec((CHUNK, ubw), lambda t, n: (rn(n), ub + t)),
                  pl.BlockSpec((CHUNK, ubw), lambda t, n: (rn(n), t)), vec, vec,
                  acc_b, acc_b, acc_c, acc_c, pl.BlockSpec((1, ubw), lambda t, n: (0, t)), ent, ent, ANY],
        out_specs=[pl.BlockSpec((CHUNK, ubw), lambda t, n: (rn(n), ub + t)), acc_b, acc_b, acc_c, acc_c, acc_a, acc_a,
                   pl.BlockSpec((tps, 8, S5_TU), lambda t, n: (t, 0, 0))],
        input_output_aliases={11: 0},
        out_shape=[jax.ShapeDtypeStruct(dproj.shape, BF16),
                   jax.ShapeDtypeStruct((S5_NT, S5_TU, S5_TS), F32), jax.ShapeDtypeStruct((S5_NT, S5_TU, S5_TS), F32),
                   jax.ShapeDtypeStruct((S5_NT, S5_TS, S5_TU), F32), jax.ShapeDtypeStruct((S5_NT, S5_TS, S5_TU), F32),
                   jax.ShapeDtypeStruct((S5_NT, 8, S5_TS), F32), jax.ShapeDtypeStruct((S5_NT, 8, S5_TS), F32),
                   jax.ShapeDtypeStruct((S5_NT, 8, S5_TU), F32)],
        scratch_shapes=[pltpu.VMEM((tps, CHUNK, S5_TS), F32) for _ in range(2)]
        + [pltpu.VMEM((tps, S5_SEG, S5_TS), F32) for _ in range(8)],
        compiler_params=pltpu.CompilerParams(dimension_semantics=("parallel", "arbitrary")),
    )(proj, dy, ab_re.reshape(S5_NT, 1, S5_TS), ab_im.reshape(S5_NT, 1, S5_TS), bre, bim, cre, cim, dvec, er, ei,
      dproj)


def _s5_gate_bwd(dmix, g, t, proj, dproj):
    rows = g.shape[0]
    tm = _row_tile(rows, 384)
    ob = RET_W // S5_W
    zb = (2 * RET_QK + 2 * RET_W + S5_W) // S5_W

    def body(do_ref, g_ref, t_ref, z_ref, dp_ref, dz_ref, dt_ref, dg_ref):
        do = do_ref[...]
        gv = g_ref[...].astype(F32)
        z = z_ref[...]
        st = _sigmoid(t_ref[...])
        sg = _sigmoid(z)
        os5 = gv * st
        dz_ref[...] = (do * os5 * sg * (1.0 + z * (1.0 - sg))).astype(BF16)
        dos = do * z * sg
        dt_ref[...] = (dos * gv * st * (1.0 - st)).astype(BF16)
        dg_ref[...] = dos * st

    blk = pl.BlockSpec((tm, S5_W), lambda i: (i, 0))
    return pl.pallas_call(
        body, name="s5_gate_bwd", grid=(rows // tm,),
        in_specs=[pl.BlockSpec((tm, S5_W), lambda i: (i, ob)), blk, blk,
                  pl.BlockSpec((tm, S5_W), lambda i: (i, zb)), ANY],
        out_specs=[pl.BlockSpec((tm, S5_W), lambda i: (i, zb)), blk, blk],
        out_shape=[jax.ShapeDtypeStruct(dproj.shape, BF16), jax.ShapeDtypeStruct((rows, S5_W), BF16),
                   jax.ShapeDtypeStruct((rows, S5_W), F32)],
        input_output_aliases={4: 0},
    )(dmix, g, t, proj, dproj)


def _split3(x):
    hi = x.astype(BF16)
    r = x - hi.astype(F32)
    mid = r.astype(BF16)
    lo = (r - mid.astype(F32)).astype(BF16)
    return hi, mid, lo


def _tri_sum(x, upper):
    i = lax.broadcasted_iota(jnp.int32, (CHUNK, CHUNK), 0)
    j = lax.broadcasted_iota(jnp.int32, (CHUNK, CHUNK), 1)
    tri = jnp.where((j >= i) if upper else (j <= i), 1.0, 0.0).astype(BF16)
    hi, mid, lo = _split3(x)
    return _dot(tri, lo) + _dot(tri, mid) + _dot(tri, hi)


def _gla_log_decay(gl, wg, bg, n):
    logit = _dot(gl, wg) + bg
    la = (jnp.minimum(logit, 0.0) - jnp.log(1.0 + jnp.exp(-jnp.abs(logit)))) * (1.0 / GLA_TAU)
    row = lax.broadcasted_iota(jnp.int32, (CHUNK, 1), 0)
    live = jnp.logical_or(n > 0, row >= PAD)
    return logit, jnp.where(live, la, 0.0), live


def _gla_in_specs(rev, nc):
    def cn(n):
        return (nc - 1 - n) if rev else n
    kb = GLA_QK // GLA_DK
    vb = 2 * GLA_QK // GLA_DV
    zb = (2 * GLA_QK + GLA_W) // GLA_DV
    gb = (2 * GLA_QK + 2 * GLA_W) // 128
    return [
        pl.BlockSpec((CHUNK, GLA_DK), lambda h, n: (cn(n), h)),
        pl.BlockSpec((CHUNK, GLA_DK), lambda h, n: (cn(n), kb + h)),
        pl.BlockSpec((CHUNK, GLA_DV), lambda h, n: (cn(n), vb + h)),
        pl.BlockSpec((CHUNK, GLA_DV), lambda h, n: (cn(n), zb + h)),
        pl.BlockSpec((CHUNK, 128), lambda h, n: (cn(n), gb)),
        pl.BlockSpec((128, GLA_DK), lambda h, n: (0, h)),
        pl.BlockSpec((1, GLA_DK), lambda h, n: (0, h)),
        pl.BlockSpec((1, GLA_DV), lambda h, n: (0, h)),
    ]


def _gla_fwd(proj, wgate, bgate, normw):
    rows = proj.shape[0]
    nc = rows // CHUNK

    def body(q_ref, k_ref, v_ref, z_ref, gl_ref, wg_ref, bg_ref, w_ref, o_ref, oc_ref, st_ref, s_scr, b_scr):
        n = pl.program_id(1)

        @pl.when(n == 0)
        def _():
            s_scr[...] = jnp.zeros_like(s_scr)

        q = q_ref[...] * (GLA_DK ** -0.5)
        k = k_ref[...]
        v = v_ref[...]
        vb = v.astype(BF16)
        _, la, _ = _gla_log_decay(gl_ref[...], wg_ref[...], bg_ref[...], n)
        b = _tri_sum(la, False)
        b_scr[...] = b
        b_last = b_scr[CHUNK - 1:CHUNK, :]
        st = s_scr[...]
        st_ref[0, 0] = st
        s_scr[...] = st * jnp.exp(b_last) + _dot(v, k * jnp.exp(b_last - b), TN)
        rowc = lax.broadcasted_iota(jnp.int32, (CHUNK, 1), 0)
        rows16 = lax.broadcasted_iota(jnp.int32, (SUB, 1), 0)
        a_tot = jnp.zeros((CHUNK, CHUNK), F32)
        for s in range(1, NSUB):
            lo = s * SUB
            bref = b_scr[lo - 1:lo, :]
            in_s = jnp.logical_and(rowc >= lo, rowc < lo + SUB)
            qh = q * jnp.exp(jnp.where(in_s, b - bref, -1e30))
            kh = k * jnp.exp(jnp.where(rowc < lo, bref - b, -1e30))
            a_tot = a_tot + _dot(qh, kh, NT)
        lane = lax.broadcasted_iota(jnp.int32, (SUB, CHUNK), 1)
        diag = []
        for s in range(NSUB):
            lo = s * SUB
            qs, bs = q[lo:lo + SUB], b[lo:lo + SUB]
            s_blk = jnp.zeros((SUB, CHUNK), F32)
            for j in range(SUB):
                r = lo + j
                e = jnp.exp(jnp.where(rows16 >= j, bs - b_scr[r:r + 1, :], -1e30))
                col = jnp.sum(qs * k_ref[r:r + 1, :] * e, axis=1, keepdims=True)
                s_blk = jnp.where(lane == r, col, s_blk)
            diag.append(s_blk)
        o = _dot(q * jnp.exp(b), st, NT) + _dot(a_tot + jnp.concatenate(diag, axis=0), vb)
        o_ref[...] = o
        oc_ref[...] = _gate_fwd(o, z_ref[...], w_ref[...]).astype(BF16)

    return pl.pallas_call(
        body, name="gla_fwd", grid=(GLA_HEADS, nc),
        in_specs=_gla_in_specs(False, nc),
        out_specs=[pl.BlockSpec((CHUNK, GLA_DV), lambda h, n: (n, h)),
                   pl.BlockSpec((CHUNK, GLA_DV), lambda h, n: (n, h)),
                   pl.BlockSpec((1, 1, GLA_DV, GLA_DK), lambda h, n: (h, n, 0, 0))],
        out_shape=[jax.ShapeDtypeStruct((rows, GLA_W), F32), jax.ShapeDtypeStruct((rows, GLA_W), BF16),
                   jax.ShapeDtypeStruct((GLA_HEADS, nc, GLA_DV, GLA_DK), F32)],
        scratch_shapes=[pltpu.VMEM((GLA_DV, GLA_DK), F32), pltpu.VMEM((CHUNK, GLA_DK), F32)],
        compiler_params=pltpu.CompilerParams(dimension_semantics=("parallel", "arbitrary")),
    )(proj, proj, proj, proj, proj, wgate, bgate, normw)


def _gla_bwd(proj, wgate, bgate, normw, o_gla, d_oc, states):
    rows = proj.shape[0]
    nc = rows // CHUNK

    def rn(n):
        return nc - 1 - n

    def body(q_ref, k_ref, v_ref, z_ref, gl_ref, wg_ref, bg_ref, w_ref, o_ref, do_ref, st_ref,
             dq_ref, dk_ref, dv_ref, dz_ref, dl_ref, dw_ref, dbg_ref,
             ds_scr, dq_scr, dk_scr, dv_scr, db_scr, b_scr, q_scr):
        n = pl.program_id(1)
        cn = rn(n)

        @pl.when(n == 0)
        def _():
            ds_scr[...] = jnp.zeros_like(ds_scr)
            dw_ref[...] = jnp.zeros_like(dw_ref)
            dbg_ref[...] = jnp.zeros_like(dbg_ref)

        q = q_ref[...] * (GLA_DK ** -0.5)
        k = k_ref[...]
        v = v_ref[...]
        vb = v.astype(BF16)
        do, dz, dw = _gate_bwd(do_ref[...], o_ref[...], z_ref[...], w_ref[...])
        dz_ref[...] = dz.astype(BF16)
        dw_ref[0] += dw
        logit, la, live = _gla_log_decay(gl_ref[...], wg_ref[...], bg_ref[...], cn)
        b = _tri_sum(la, False)
        b_scr[...] = b
        b_last = b_scr[CHUNK - 1:CHUNK, :]
        e_last = jnp.exp(b_last)
        st = st_ref[0, 0]
        g1 = ds_scr[...]
        eb = jnp.exp(b)
        qe = q * eb
        dqe = _dot(do, st)
        dq_scr[...] = dqe * eb
        db_scr[...] = dqe * qe
        ekb = jnp.exp(b_last - b)
        kdec = k * ekb
        dkdec = _dot(v, g1)
        dv_scr[...] = _dot(kdec, g1, NT)
        dk_scr[...] = dkdec * ekb
        wk = dkdec * kdec
        db_scr[...] -= wk
        dbl = jnp.sum(wk, axis=0, keepdims=True) + jnp.sum(g1 * st, axis=0, keepdims=True) * e_last
        ds_scr[...] = g1 * e_last + _dot(do, qe, TN)
        rowc = lax.broadcasted_iota(jnp.int32, (CHUNK, 1), 0)
        rows16 = lax.broadcasted_iota(jnp.int32, (SUB, 1), 0)
        da_full = _dot(do, vb, NT)
        a_tot = jnp.zeros((CHUNK, CHUNK), F32)
        for s in range(1, NSUB):
            lo = s * SUB
            bref = b_scr[lo - 1:lo, :]
            in_s = jnp.logical_and(rowc >= lo, rowc < lo + SUB)
            eq = jnp.exp(jnp.where(in_s, b - bref, -1e30))
            ek = jnp.exp(jnp.where(rowc < lo, bref - b, -1e30))
            qh = q * eq
            kh = k * ek
            a_tot = a_tot + _dot(qh, kh, NT)
            da = jnp.where(in_s, da_full, 0.0)
            dqh = _dot(da, kh)
            dkh = _dot(da, qh, TN)
            tq = dqh * qh
            tk = dkh * kh
            dq_scr[...] += dqh * eq
            dk_scr[...] += dkh * ek
            db_scr[...] += tq - tk
            db_scr[lo - 1:lo, :] += jnp.sum(tk, axis=0, keepdims=True) - jnp.sum(tq, axis=0, keepdims=True)
        dat_full = _dot(vb, do, NT)
        q_scr[...] = q
        lane = lax.broadcasted_iota(jnp.int32, (SUB, CHUNK), 1)
        diag = []
        for s in range(NSUB):
            lo = s * SUB
            qs, ks, bs = q[lo:lo + SUB], k[lo:lo + SUB], b[lo:lo + SUB]
            da_blk, dat_blk = da_full[lo:lo + SUB], dat_full[lo:lo + SUB]
            dqs = jnp.zeros((SUB, GLA_DK), F32)
            dks = jnp.zeros((SUB, GLA_DK), F32)
            dbs = jnp.zeros((SUB, GLA_DK), F32)
            s_blk = jnp.zeros((SUB, CHUNK), F32)
            for j in range(SUB):
                r = lo + j
                kj = k_ref[r:r + 1, :]
                e = jnp.exp(jnp.where(rows16 >= j, bs - b_scr[r:r + 1, :], -1e30))
                p = qs * e * kj
                s_blk = jnp.where(lane == r, jnp.sum(p, axis=1, keepdims=True), s_blk)
                dcol = jnp.sum(jnp.where(lane == r, da_blk, 0.0), axis=1, keepdims=True)
                dqs = dqs + (dcol * e) * kj
                dbs = dbs + dcol * p
            for i in range(SUB):
                r = lo + i
                e = jnp.exp(jnp.where(rows16 <= i, b_scr[r:r + 1, :] - bs, -1e30))
                drow = jnp.sum(jnp.where(lane == r, dat_blk, 0.0), axis=1, keepdims=True)
                nq = (drow * e) * q_scr[r:r + 1, :]
                dks = dks + nq
                dbs = dbs - nq * ks
            dq_scr[lo:lo + SUB, :] += dqs
            dk_scr[lo:lo + SUB, :] += dks
            db_scr[lo:lo + SUB, :] += dbs
            diag.append(s_blk)
        dv_scr[...] += _dot(a_tot + jnp.concatenate(diag, axis=0), do, TN)
        db_scr[CHUNK - 1:CHUNK, :] += dbl
        dla = _tri_sum(db_scr[...], True)
        dlogit = jnp.where(live, dla * (1.0 / GLA_TAU) * _sigmoid(-logit), 0.0)
        dl_ref[...] = dlogit
        dbg_ref[0] += jnp.sum(dlogit, axis=0, keepdims=True)
        dq_ref[...] = (dq_scr[...] * (GLA_DK ** -0.5)).astype(BF16)
        dk_ref[...] = dk_scr[...].astype(BF16)
        dv_ref[...] = dv_scr[...].astype(BF16)

    in_specs = _gla_in_specs(True, nc) + [
        pl.BlockSpec((CHUNK, GLA_DV), lambda h, n: (rn(n), h)),
        pl.BlockSpec((CHUNK, GLA_DV), lambda h, n: (rn(n), h)),
        pl.BlockSpec((1, 1, GLA_DV, GLA_DK), lambda h, n: (h, rn(n), 0, 0)),
    ]
    return pl.pallas_call(
        body, name="gla_bwd", grid=(GLA_HEADS, nc),
        in_specs=in_specs,
        out_specs=[pl.BlockSpec((CHUNK, GLA_DK), lambda h, n: (rn(n), h)),
                   pl.BlockSpec((CHUNK, GLA_DK), lambda h, n: (rn(n), h)),
                   pl.BlockSpec((CHUNK, GLA_DV), lambda h, n: (rn(n), h)),
                   pl.BlockSpec((CHUNK, GLA_DV), lambda h, n: (rn(n), h)),
                   pl.BlockSpec((CHUNK, GLA_DK), lambda h, n: (rn(n), h)),
                   pl.BlockSpec((1, 1, GLA_DV), lambda h, n: (h, 0, 0)),
                   pl.BlockSpec((1, 1, GLA_DK), lambda h, n: (h, 0, 0))],
        out_shape=[jax.ShapeDtypeStruct((rows, GLA_QK), BF16), jax.ShapeDtypeStruct((rows, GLA_QK), BF16),
                   jax.ShapeDtypeStruct((rows, GLA_W), BF16), jax.ShapeDtypeStruct((rows, GLA_W), BF16),
                   jax.ShapeDtypeStruct((rows, GLA_QK), F32),
                   jax.ShapeDtypeStruct((GLA_HEADS, 1, GLA_DV), F32),
                   jax.ShapeDtypeStruct((GLA_HEADS, 1, GLA_DK), F32)],
        scratch_shapes=[pltpu.VMEM((GLA_DV, GLA_DK), F32), pltpu.VMEM((CHUNK, GLA_DK), F32),
                        pltpu.VMEM((CHUNK, GLA_DK), F32), pltpu.VMEM((CHUNK, GLA_DV), F32),
                        pltpu.VMEM((CHUNK, GLA_DK), F32), pltpu.VMEM((CHUNK, GLA_DK), F32),
                        pltpu.VMEM((CHUNK, GLA_DK), F32)],
        compiler_params=pltpu.CompilerParams(dimension_semantics=("parallel", "arbitrary")),
    )(proj, proj, proj, proj, proj, wgate, bgate, normw, o_gla, d_oc, states)


def _adamw(name, w, g, m, v):
    rows, cols = w.shape
    tm = 8
    for cand in range(8, rows + 1, 8):
        if rows % cand == 0 and cand * cols * 4 <= 2 ** 21:
            tm = cand
    c1 = 1.0 - ADAM_B1 ** ADAM_STEP
    c2 = 1.0 - ADAM_B2 ** ADAM_STEP

    def body(w_ref, g_ref, m_ref, v_ref, d_ref, nm_ref, nv_ref):
        gv = g_ref[...]
        nm = ADAM_B1 * m_ref[...] + (1.0 - ADAM_B1) * gv
        nv = ADAM_B2 * v_ref[...] + (1.0 - ADAM_B2) * (gv * gv)
        nm_ref[...] = nm
        nv_ref[...] = nv
        d_ref[...] = -ADAM_LR * ((nm / c1) / (jnp.sqrt(nv / c2) + ADAM_EPS) + ADAM_WD * w_ref[...])

    blk = pl.BlockSpec((tm, cols), lambda i: (i, 0))
    return pl.pallas_call(
        body, name=name, grid=(rows // tm,),
        in_specs=[blk] * 4, out_specs=[blk] * 3,
        out_shape=[jax.ShapeDtypeStruct((rows, cols), F32)] * 3,
    )(w, g, m, v)


def _place():
    x, y, c = lax.axis_index("x"), lax.axis_index("y"), lax.axis_index("c")
    chips = [(1 - x, y), (x, 1 - y), (1 - x, 1 - y)]
    return x, y, c, chips


ANY = pl.BlockSpec(memory_space=pl.ANY)


def _gathered_struct(shape, dtype, kind):
    r, cc = shape
    if kind == "row":
        return jax.ShapeDtypeStruct((N_SHARD * r, cc), dtype)
    if kind == "col":
        return jax.ShapeDtypeStruct((r, N_SHARD * cc), dtype)
    return jax.ShapeDtypeStruct((N_SHARD, r, cc), dtype)


def _cast_place(name, w, kind, mine_arr, dtype, also_own=False):
    r, cc = w.shape
    tr = r
    for cand in (256, 128, 64, 32, 16):
        if r % cand == 0:
            tr = cand
            break
    nb = r // tr
    if kind == "row":
        o_spec = pl.BlockSpec((tr, cc), lambda i, m: (m[0] * nb + i, 0))
    elif kind == "col":
        o_spec = pl.BlockSpec((tr, cc), lambda i, m: (i, m[0]))
    else:
        o_spec = pl.BlockSpec((None, tr, cc), lambda i, m: (m[0], i, 0))
    w_spec = pl.BlockSpec((tr, cc), lambda i, m: (i, 0))

    def body(m_ref, w_ref, o_ref, *own_ref):
        o_ref[...] = w_ref[...].astype(o_ref.dtype)
        for ref in own_ref:
            ref[...] = w_ref[...].astype(ref.dtype)

    out_specs, out_shape = [o_spec], [_gathered_struct((r, cc), dtype, kind)]
    if also_own:
        out_specs.append(w_spec)
        out_shape.append(jax.ShapeDtypeStruct((r, cc), dtype))
    out = pl.pallas_call(
        body, name=name,
        grid_spec=pltpu.PrefetchScalarGridSpec(
            num_scalar_prefetch=1, grid=(nb,), in_specs=[w_spec], out_specs=out_specs),
        out_shape=out_shape,
    )(mine_arr, w)
    return out if also_own else out[0]


def _gather_small(shard):
    rows, cols = shard.shape

    def body(in_ref, out_ref, send_sems, recv_sems):
        x, y, c, chips = _place()
        mine = 2 * x + y
        out_ref[mine] = in_ref[...]
        cps = []
        for j, chip in enumerate(chips):
            cp = pltpu.make_async_remote_copy(
                src_ref=in_ref, dst_ref=out_ref.at[mine], send_sem=send_sems.at[j], recv_sem=recv_sems.at[j],
                device_id=(*chip, c), device_id_type=MESH)
            cp.start()
            cps.append(cp)
        for cp in cps:
            cp.wait()

    vm = pl.BlockSpec(memory_space=pltpu.VMEM)
    return pl.pallas_call(
        body, name="gather_small",
        in_specs=[vm], out_specs=vm,
        out_shape=jax.ShapeDtypeStruct((N_SHARD, rows, cols), F32),
        scratch_shapes=[pltpu.SemaphoreType.DMA((3,)), pltpu.SemaphoreType.DMA((3,))],
        compiler_params=pltpu.CompilerParams(has_side_effects=True),
    )(shard)


def _in_proj_shifted(name, a, b, n, shifts, tm, tn, out_cols, into=None):
    m, k = a.shape
    nb_b = b.shape[1] // tn
    nb_o = out_cols // tn

    def body(s_ref, a_ref, b_ref, *rest):
        rest[-1][...] = _dot(a_ref[...], b_ref[...])

    in_specs = [pl.BlockSpec((tm, k), lambda i, j, s: (i, 0)),
                pl.BlockSpec((k, tn), lambda i, j, s: (0, (s[0] + j) % nb_b))]
    operands = [shifts, a, b]
    aliases = {}
    if into is not None:
        in_specs.append(ANY)
        operands.append(into)
        aliases = {3: 0}
    return pl.pallas_call(
        body, name=name,
        grid_spec=pltpu.PrefetchScalarGridSpec(
            num_scalar_prefetch=1, grid=(m // tm, n // tn), in_specs=in_specs,
            out_specs=pl.BlockSpec((tm, tn), lambda i, j, s: (i, (s[1] + j) % nb_o))),
        out_shape=jax.ShapeDtypeStruct((m, out_cols), F32), input_output_aliases=aliases,
    )(*operands)


def _allreduce_small(buf):
    rows, cols = buf.shape
    hr = rows // 2

    def body(in_ref, out_ref, sib_ref, pair_ref, far_ref, send_sems, recv_sems):
        x, y, c, chips = _place()
        sibling = (x, y, 1 - c)
        mine = pl.ds(pl.multiple_of(c * hr, 8), hr)
        theirs = pl.ds(pl.multiple_of((1 - c) * hr, 8), hr)
        to_sib = pltpu.make_async_remote_copy(
            src_ref=in_ref.at[theirs, :], dst_ref=sib_ref, send_sem=send_sems.at[0], recv_sem=recv_sems.at[0],
            device_id=sibling, device_id_type=MESH)
        to_sib.start()
        to_sib.wait()
        pair_ref[...] = in_ref[mine, :] + sib_ref[...]
        far = [pltpu.make_async_remote_copy(
            src_ref=pair_ref, dst_ref=far_ref.at[j], send_sem=send_sems.at[1 + j], recv_sem=recv_sems.at[1 + j],
            device_id=(*chip, c), device_id_type=MESH) for j, chip in enumerate(chips)]
        for cp in far:
            cp.start()
        for cp in far:
            cp.wait()
        out_ref[mine, :] = (pair_ref[...] + far_ref[1]) + (far_ref[0] + far_ref[2])
        swap = pltpu.make_async_remote_copy(
            src_ref=out_ref.at[mine, :], dst_ref=out_ref.at[mine, :], send_sem=send_sems.at[4],
            recv_sem=recv_sems.at[4], device_id=sibling, device_id_type=MESH)
        swap.start()
        swap.wait()

    vm = pl.BlockSpec(memory_space=pltpu.VMEM)
    return pl.pallas_call(
        body, name="allreduce_small",
        in_specs=[vm], out_specs=vm,
        out_shape=jax.ShapeDtypeStruct((rows, cols), F32),
        scratch_shapes=[pltpu.VMEM((hr, cols), F32), pltpu.VMEM((hr, cols), F32),
                        pltpu.VMEM((3, hr, cols), F32),
                        pltpu.SemaphoreType.DMA((5,)), pltpu.SemaphoreType.DMA((5,))],
        compiler_params=pltpu.CompilerParams(has_side_effects=True),
    )(buf)


def _shard_window(ref, kind, shard_shape, shard, half):
    r, cc = shard_shape
    hr = r // 2
    if kind == "row":
        return ref.at[pl.ds(_mo(shard * r + half * hr, 8), hr), :]
    if kind == "col":
        return ref.at[pl.ds(_mo(half * hr, 8), hr), pl.ds(_mo(shard * cc, 128), cc)]
    if kind == "colw":
        return ref.at[pl.ds(_mo(half * hr, 8), hr), pl.ds(_mo(shard * (cc - 128), 128), cc)]
    return ref.at[shard, pl.ds(_mo(half * hr, 8), hr), :]


HBM = pl.BlockSpec(memory_space=pltpu.HBM)
SEM = pl.BlockSpec(memory_space=pltpu.SEMAPHORE)
DATAFLOW = pltpu.SideEffectType.DATAFLOW_SIDE_EFFECTING


def _in_hbm(a):
    return pltpu.with_memory_space_constraint(a, pltpu.HBM)


def _empty_hbm(shape, dtype):
    return _in_hbm(lax.empty(shape, dtype))


def _copies_start(name, bufs, n_copies, plan, carry):
    nb = len(bufs)

    def body(*refs):
        send_sems, recv_sems = refs[nb + 1], refs[nb + 2]
        for k, (src, dst, to) in enumerate(plan(refs[:nb])):
            pltpu.make_async_remote_copy(src_ref=src, dst_ref=dst, send_sem=send_sems.at[k], recv_sem=recv_sems.at[k],
                                         device_id=to, device_id_type=MESH).start()

    passed = list(bufs) + [carry]
    out = pl.pallas_call(
        body, name=name,
        in_specs=[HBM] * (nb + 1), out_specs=[SEM, SEM] + [HBM] * (nb + 1),
        out_shape=[pltpu.SemaphoreType.DMA((n_copies,)), pltpu.SemaphoreType.DMA((n_copies,))]
        + [pltpu.HBM(a.shape, a.dtype) for a in passed],
        input_output_aliases={i: 2 + i for i in range(nb + 1)},
        compiler_params=pltpu.CompilerParams(has_side_effects=DATAFLOW),
    )(*[_in_hbm(a) for a in passed])
    return out[0], out[1], list(out[2:2 + nb]), out[2 + nb]


def _copies_wait(name, send_sems, recv_sems, bufs, plan, after):
    nb = len(bufs)
    after = list(after) if isinstance(after, (list, tuple)) else [after]

    def body(*refs):
        send, recv = refs[nb], refs[nb + 1]
        for k, (src, dst, to) in enumerate(plan(refs[:nb])):
            cp = pltpu.make_async_remote_copy(src_ref=src, dst_ref=dst, send_sem=send.at[k], recv_sem=recv.at[k],
                                              device_id=to, device_id_type=MESH)
            cp.wait_send()
            cp.wait_recv()

    out = pl.pallas_call(
        body, name=name,
        in_specs=[HBM] * nb + [SEM, SEM] + [ANY] * len(after), out_specs=[HBM] * nb,
        out_shape=[pltpu.HBM(a.shape, a.dtype) for a in bufs],
        input_output_aliases={i: i for i in range(nb)},
        compiler_params=pltpu.CompilerParams(has_side_effects=DATAFLOW),
    )(*bufs, send_sems, recv_sems, *after)
    return list(out)


def _gather_ici_plan(shard_shapes, kinds):
    n_arr = len(kinds)

    def plan(refs):
        x, y, c, chips = _place()
        out = []
        for i in range(n_arr):
            w = _shard_window(refs[i], kinds[i], shard_shapes[i], 2 * x + y, c)
            out += [(w, w, (*chip, c)) for chip in chips]
        return out

    return plan


def _gather_d2d_plan(shard_shapes, kinds):
    n_arr = len(kinds)

    def plan(refs):
        x, y, c, chips = _place()
        out = []
        for i in range(n_arr):
            for chip in chips:
                w = _shard_window(refs[i], kinds[i], shard_shapes[i], 2 * chip[0] + chip[1], c)
                out.append((w, w, (x, y, 1 - c)))
        return out

    return plan


def _rs_pair_plan(kinds, shard_shapes):
    n_arr = len(kinds)

    def plan(refs):
        x, y, c, _ = _place()
        out = []
        for i in range(n_arr):
            for s in range(N_SHARD):
                out.append((_shard_window(refs[i], kinds[i], shard_shapes[i], s, 1 - c), refs[n_arr + i].at[s],
                            (x, y, 1 - c)))
        return out

    return plan


def _rs_share_plan(shard_shapes):
    def plan(refs):
        x, y, c, _ = _place()
        out = []
        for ref, (r, _) in zip(refs, shard_shapes):
            w = ref.at[pl.ds(_mo(c * (r // 2), 8), r // 2), :]
            out.append((w, w, (x, y, 1 - c)))
        return out

    return plan


def _rs_chip_plan(n_arr):
    def plan(refs):
        x, y, c, chips = _place()
        out = []
        for i in range(n_arr):
            for j, chip in enumerate(chips):
                out.append((refs[i].at[2 * chip[0] + chip[1]], refs[n_arr + i].at[j], (*chip, c)))
        return out

    return plan


def _rs_pair_add(name, grad, got, kind, shard_shape, c):
    r, cc = shard_shape
    hr = r // 2
    tr = hr
    for cand in (256, 128, 64, 32, 16):
        if hr % cand == 0:
            tr = cand
            break
    nb = hr // tr

    def body(c_ref, g_ref, t_ref, p_ref, pb_ref):
        p = g_ref[...] + t_ref[...]
        p_ref[...] = p
        pb_ref[...] = p.astype(BF16)

    out_shape = [jax.ShapeDtypeStruct((N_SHARD, hr, cc), F32), jax.ShapeDtypeStruct((N_SHARD, hr, cc), BF16)]
    if kind == "colw":
        tiles = cc // 128
        tr = hr
        g_spec = pl.BlockSpec((tr, 128), lambda s, t, cr: (cr[0], s * (tiles - 1) + t))
        t_spec = pl.BlockSpec((None, tr, 128), lambda s, t, cr: (s, 0, t))
        return pl.pallas_call(
            body, name=name,
            grid_spec=pltpu.PrefetchScalarGridSpec(
                num_scalar_prefetch=1, grid=(N_SHARD, tiles), in_specs=[g_spec, t_spec], out_specs=[t_spec, t_spec]),
            out_shape=out_shape,
        )(c, grad, got)
    if kind == "row":
        g_spec = pl.BlockSpec((tr, cc), lambda s, i, cr: (s * 2 * nb + cr[0] * nb + i, 0))
    elif kind == "col":
        g_spec = pl.BlockSpec((tr, cc), lambda s, i, cr: (cr[0] * nb + i, s))
    else:
        g_spec = pl.BlockSpec((None, tr, cc), lambda s, i, cr: (s, cr[0] * nb + i, 0))
    t_spec = pl.BlockSpec((None, tr, cc), lambda s, i, cr: (s, i, 0))
    return pl.pallas_call(
        body, name=name,
        grid_spec=pltpu.PrefetchScalarGridSpec(
            num_scalar_prefetch=1, grid=(N_SHARD, nb),
            in_specs=[g_spec, t_spec], out_specs=[t_spec, t_spec]),
        out_shape=out_shape,
    )(c, grad, got)


def _rs_chip_add(name, pair_f32, got, shard_shape, mine_c):
    r, cc = shard_shape
    hr = r // 2
    tr = hr
    for cand in (256, 128, 64, 32, 16):
        if hr % cand == 0:
            tr = cand
            break
    nb = hr // tr

    def body(mc_ref, p_ref, t0_ref, t1_ref, t2_ref, o_ref):
        o_ref[...] = (p_ref[...] + t1_ref[...].astype(F32)) + (t0_ref[...].astype(F32) + t2_ref[...].astype(F32))

    def far(j):
        return pl.BlockSpec((None, tr, cc), lambda i, mc: (j, i, 0))

    return pl.pallas_call(
        body, name=name,
        grid_spec=pltpu.PrefetchScalarGridSpec(
            num_scalar_prefetch=1, grid=(nb,),
            in_specs=[pl.BlockSpec((None, tr, cc), lambda i, mc: (mc[0], i, 0)), far(0), far(1), far(2)],
            out_specs=pl.BlockSpec((tr, cc), lambda i, mc: (mc[1] * nb + i, 0))),
        out_shape=jax.ShapeDtypeStruct((r, cc), F32),
    )(mine_c, pair_f32, got, got, got)


def _rs_pair_share(name, halves, shard_shapes):
    n_arr = len(halves)

    def body(*refs):
        ins = refs[:n_arr]
        outs = refs[n_arr:2 * n_arr]
        send_sems, recv_sems = refs[2 * n_arr:]
        x, y, c, _ = _place()
        sibling = (x, y, 1 - c)
        cps = []
        for i in range(n_arr):
            hr = shard_shapes[i][0] // 2
            rows = pl.ds(_mo(c * hr, 8), hr)
            cp = pltpu.make_async_remote_copy(
                src_ref=outs[i].at[rows, :], dst_ref=outs[i].at[rows, :],
                send_sem=send_sems.at[i], recv_sem=recv_sems.at[i],
                device_id=sibling, device_id_type=MESH)
            cp.start()
            cps.append(cp)
        for cp in cps:
            cp.wait()

    return pl.pallas_call(
        body, name=name,
        in_specs=[ANY] * n_arr, out_specs=[ANY] * n_arr,
        out_shape=[jax.ShapeDtypeStruct(s, F32) for s in shard_shapes],
        input_output_aliases={i: i for i in range(n_arr)},
        scratch_shapes=[pltpu.SemaphoreType.DMA((n_arr,)), pltpu.SemaphoreType.DMA((n_arr,))],
        compiler_params=pltpu.CompilerParams(has_side_effects=True),
    )(*halves)


def _pack(arrays):
    flat = []
    for a in arrays:
        v = a.reshape(-1).astype(F32)
        flat.append(jnp.pad(v, (0, (-v.shape[0]) % SMALL_COLS)))
    buf = jnp.concatenate(flat).reshape(-1, SMALL_COLS)
    return jnp.pad(buf, ((0, (-buf.shape[0]) % 16), (0, 0)))


def _unpack(buf, shapes):
    out = []
    row = 0
    for s in shapes:
        size = math.prod(s)
        nrow = -(-size // SMALL_COLS)
        out.append(buf[row:row + nrow].reshape(-1)[:size].reshape(s))
        row += nrow
    return out


def kernel(x, meta, norm_ab_w, w_in_ab, ret_norm_w, s5_lam_re, s5_lam_im, s5_log_dt, s5_b_re, s5_b_im, s5_c_re, s5_c_im, s5_d, s5_w_glu, w_out_ab, norm_c_w, w_in_c, gla_w_gate, gla_b_gate, gla_norm_w, w_out_c, final_norm_w, loss_target, m_meta, m_norm_ab_w, m_w_in_ab, m_ret_norm_w, m_s5_lam_re, m_s5_lam_im, m_s5_log_dt, m_s5_b_re, m_s5_b_im, m_s5_c_re, m_s5_c_im, m_s5_d, m_s5_w_glu, m_w_out_ab, m_norm_c_w, m_w_in_c, m_gla_w_gate, m_gla_b_gate, m_gla_norm_w, m_w_out_c, m_final_norm_w, v_meta, v_norm_ab_w, v_w_in_ab, v_ret_norm_w, v_s5_lam_re, v_s5_lam_im, v_s5_log_dt, v_s5_b_re, v_s5_b_im, v_s5_c_re, v_s5_c_im, v_s5_d, v_s5_w_glu, v_w_out_ab, v_norm_c_w, v_w_in_c, v_gla_w_gate, v_gla_b_gate, v_gla_norm_w, v_w_out_c, v_final_norm_w):
    seq = x.shape[1]
    rows = seq + CHUNK
    xi, yi, ci = lax.axis_index("x"), lax.axis_index("y"), lax.axis_index("c")
    mine = 2 * xi + yi
    c_arr = jnp.reshape(ci, (1,)).astype(jnp.int32)
    mine_c = jnp.stack([mine, ci]).astype(jnp.int32)

    mine_arr = jnp.reshape(mine, (1,)).astype(jnp.int32)
    small_shard = _pack([meta, norm_c_w, gla_norm_w, gla_b_gate, gla_w_gate[0]])
    small_all = _gather_small(small_shard)
    first_kinds = ["col"]
    first_shapes = [w_in_ab.shape[1:]]
    first_ici = _gather_ici_plan(first_shapes, first_kinds)
    first_d2d = _gather_d2d_plan(first_shapes, first_kinds)
    wab_buf, wab_own = _cast_place("place_w_in_ab", w_in_ab[0], "col", mine_arr, BF16, also_own=True)
    f_send, f_recv, f_bufs, small_all = _copies_start("gather_first_ici_start", [wab_buf], 3, first_ici, small_all)
    def late_group(items, kinds):
        shapes = [a.shape for _, a in items]
        bufs = [_cast_place("place_" + nm, a, kd, mine_arr, BF16) for (nm, a), kd in zip(items, kinds)]
        return bufs, _gather_ici_plan(shapes, kinds), _gather_d2d_plan(shapes, kinds), 3 * len(items)

    a_bufs, a_ici, a_d2d, n_a = late_group([("w_out_ab", w_out_ab[0]), ("w_glu", s5_w_glu[0])], ["row", "row"])
    b_bufs, b_ici, b_d2d, n_b = late_group([("w_in_c", w_in_c[0]), ("w_out_c", w_out_c[0])], ["stack", "row"])
    g_bufs = a_bufs + b_bufs
    cosf, sinf = _rope_tables(rows)
    rtab = _ret_tables()
    ab_re, ab_im, bb_re, bb_im = _s5_discretize(s5_lam_re[0], s5_lam_im[0], s5_log_dt[0], s5_b_re[0], s5_b_im[0])
    ab = (ab_re, ab_im)
    bd_b = (_bdiag_in(bb_re), _bdiag_in(bb_im))
    bd_c = (_bdiag_out(s5_c_re[0]), _bdiag_out(s5_c_im[0]))
    q4 = D_MODEL // N_SHARD
    g4 = GLA_QK // N_SHARD
    parts = [_unpack(small_all[j], [(N_META, q4), (1, q4), (1, q4), (1, g4), (GLA_RANK, g4)]) for j in range(N_SHARD)]
    meta_f, norm_c_f, gla_norm_f, bgate_f, wgate_f = [jnp.concatenate([p[i] for p in parts], axis=1) for i in range(5)]
    wgate_pad = jnp.pad(wgate_f, ((0, 128 - GLA_RANK), (0, 0)))

    h0, hn0 = _embed_norm(x[0], meta_f, norm_ab_w)

    tm = _row_tile(rows, 1408)
    tmk = _row_tile(rows, 1408)
    own_blocks = (IN_AB // N_SHARD) // 512
    shift_own = jnp.stack([jnp.zeros((), jnp.int32), mine.astype(jnp.int32) * own_blocks])
    shift_rest = jnp.stack([(mine.astype(jnp.int32) + 1) * own_blocks, (mine.astype(jnp.int32) + 1) * own_blocks])
    proj0 = _in_proj_shifted("in_proj_ab_own", hn0, wab_own, IN_AB // N_SHARD, shift_own, tm, 512, IN_AB)
    f_bufs = _copies_wait("gather_first_ici_wait", f_send, f_recv, f_bufs, first_ici,
                          [proj0, cosf, sinf, bd_b[0], bd_b[1], bd_c[0], bd_c[1]] + g_bufs + list(rtab))
    f_send, f_recv, f_bufs, cosf = _copies_start("gather_first_d2d_start", f_bufs, 3, first_d2d, cosf)
    wab, = _copies_wait("gather_first_d2d_wait", f_send, f_recv, f_bufs, first_d2d, cosf)
    a_send, a_recv, a_bufs, wab = _copies_start("gather_a_ici_start", a_bufs, n_a, a_ici, wab)
    b_send, b_recv, b_bufs, wab = _copies_start("gather_b_ici_start", b_bufs, n_b, b_ici, wab)
    proj0 = _in_proj_shifted("in_proj_ab_rest", hn0, wab, IN_AB - IN_AB // N_SHARD, shift_rest, tm, 512, IN_AB,
                             into=proj0)
    o_ret, o_a, ret_states = _ret_fwd(proj0, cosf, sinf, rtab, ret_norm_w)
    a_bufs = _copies_wait("gather_a_ici_wait", a_send, a_recv, a_bufs, a_ici, o_a)
    a_send, a_recv, a_bufs, proj0 = _copies_start("gather_a_d2d_start", a_bufs, n_a, a_d2d, proj0)
    y_s5, g_s5, s5_er, s5_ei = _s5_fwd(proj0, ab, bd_b, bd_c, s5_d)
    wout_ab, wglu = _copies_wait("gather_a_d2d_wait", a_send, a_recv, a_bufs, a_d2d, g_s5)
    zb_blk = (2 * RET_QK + 2 * RET_W + S5_W) // 512

    def glu_out(acc, gv, z):
        return gv.astype(F32) * _sigmoid(acc) * (z * _sigmoid(z))

    t_glu = _matmul("glu", g_s5, wglu, NN, rows, S5_W, S5_W, tm=tm, tn=512, tk=S5_W)
    o_b = _matmul("glu_out", g_s5, wglu, NN, rows, S5_W, S5_W, tm=tm, tn=512, tk=S5_W, out_dtype=BF16,
                  extras=[(g_s5, (tm, 512), lambda i, j, kk: (i, j)),
                          (proj0, (tm, 512), lambda i, j, kk: (i, zb_blk + j))],
                  epilogue=glu_out)
    b_bufs = _copies_wait("gather_b_ici_wait", b_send, b_recv, b_bufs, b_ici, o_b)
    b_send, b_recv, b_bufs, o_b = _copies_start("gather_b_d2d_start", b_bufs, n_b, b_d2d, o_b)
    h1 = _matmul("out_proj_ab", None, None, NN, rows, D_MODEL, OUT_AB, tm=tm, tn=512, tk=1024,
                 segs=[(o_a, (0, 0), wout_ab, (0, 0), RET_W, 1024),
                       (o_b, (0, 0), wout_ab, (RET_W // 1024, 0), S5_W, 1024)],
                 extras=[(h0, (tm, 512), lambda i, j, kk: (i, j))], epilogue=lambda acc, r: acc + r)
    wc_st, wout_c = _copies_wait("gather_b_d2d_wait", b_send, b_recv, b_bufs, b_d2d, h1)
    wc = jnp.concatenate([wc_st[j] for j in range(N_SHARD)] + [jnp.zeros((D_MODEL, IN_C_PAD - IN_C), BF16)], axis=1)

    hn1 = _rms_fwd("norm_c", h1, norm_c_f)
    proj1 = _matmul("in_proj_c", hn1, wc, NN, rows, IN_C_PAD, D_MODEL, tm=tm, tn=896, tk=D_MODEL)
    o_gla, o_c, gla_states = _gla_fwd(proj1, wgate_pad, bgate_f, gla_norm_f)
    h2 = _matmul("out_proj_c", o_c, wout_c, NN, rows, D_MODEL, GLA_W, tm=tm, tn=512, tk=GLA_W,
                 extras=[(h1, (tm, 512), lambda i, j, kk: (i, j))], epilogue=lambda acc, r: acc + r)
    loss_dev, dh2, d_final = _final_loss(h2, final_norm_w.reshape(1, D_MODEL), loss_target[0])

    g_wout_c = _matmul("d_w_out_c", o_c, dh2, TN, GLA_W, D_MODEL, rows, tm=1024, tn=1024, tk=tmk)
    d_oc = _matmul("d_o_c", dh2, wout_c, NT, rows, GLA_W, D_MODEL, tm=tm, tn=512, tk=1024)
    dq1, dk1, dv1, dz1, dlogit, d_gla_norm, d_bgate = _gla_bwd(proj1, wgate_pad, bgate_f, gla_norm_f, o_gla, d_oc, gla_states)
    gl_blk = (2 * GLA_QK + 2 * GLA_W) // 128
    dgl = _matmul("d_g_low", dlogit, wgate_pad, NT, rows, 128, GLA_QK, tm=tm, tn=128, tk=GLA_QK, out_dtype=BF16)
    g_wgate = _matmul("d_w_gate", proj1, dlogit, TN, 128, GLA_QK, rows, tm=128, tn=GLA_QK, tk=tmk, a_off=(0, gl_blk))
    dproj1 = jnp.concatenate([dq1, dk1, dv1, dz1, dgl], axis=1)
    g_wc = _matmul("d_w_in_c", hn1, dproj1, TN, D_MODEL, IN_C_PAD, rows, tm=1024, tn=896, tk=tmk)
    dhn1 = _matmul("d_hn1", dproj1, wc, NT, rows, D_MODEL, IN_C_PAD, tm=tm, tn=512, tk=896)
    dh1, d_norm_c = _rms_bwd("norm_c_bwd", dhn1, h1, norm_c_f, dh2)

    g_wout_ab = _matmul("d_w_out_ab_a", o_a, dh1, TN, RET_W, D_MODEL, rows, tm=1024, tn=1024, tk=tmk,
                        out_shape=jax.ShapeDtypeStruct((OUT_AB, D_MODEL), F32))
    g_wout_ab = _matmul("d_w_out_ab_b", o_b, dh1, TN, S5_W, D_MODEL, rows, tm=1024, tn=1024, tk=tmk,
                        into=(g_wout_ab, RET_W // 1024, 0))
    dmix = _matmul("d_mix", dh1, wout_ab, NT, rows, OUT_AB, D_MODEL, tm=tm, tn=512, tk=1024)
    dproj0, d_ret_norm = _ret_bwd(proj0, cosf, sinf, rtab, ret_norm_w, o_ret, dmix, ret_states)
    dproj0, dt_glu, dg_direct = _s5_gate_bwd(dmix, g_s5, t_glu, proj0, dproj0)
    g_wglu = _matmul("d_w_glu", g_s5, dt_glu, TN, S5_W, S5_W, rows, tm=1024, tn=1024, tk=tmk)
    dy_s5 = _matmul("d_y_s5", dt_glu, wglu, NT, rows, S5_W, S5_W, tm=tm, tn=512, tk=S5_W,
                    extras=[(dg_direct, (tm, 512), lambda i, j, kk: (i, j)),
                            (y_s5, (tm, 512), lambda i, j, kk: (i, j))],
                    epilogue=lambda acc, dg, yv: (acc + dg) * _gelu_grad(yv))
    wc_cols = IN_C // N_SHARD
    wc_win = (wc_cols // 128 + 1) * 128
    rs1_names = ["w_out_ab", "w_in_c", "w_out_c", "w_glu"]
    rs1_kinds = ["row", "colw", "row", "row"]
    rs1_shapes = [w_out_ab.shape[1:], (D_MODEL, wc_win), w_out_c.shape[1:], s5_w_glu.shape[1:]]
    rs1_plan = _rs_pair_plan(rs1_kinds, rs1_shapes)
    rs1_land = [_empty_hbm((N_SHARD, r // 2, cc), F32) for (r, cc) in rs1_shapes]
    p_send, p_recv, p_bufs, dy_s5 = _copies_start("rs1_pair_start", [g_wout_ab, g_wc, g_wout_c, g_wglu] + rs1_land,
                                                  N_SHARD * 4, rs1_plan, dy_s5)
    dproj0, dbr_d, dbi_d, dcr_d, dci_d, dar_p, dai_p, dd_p = _s5_bwd(proj0, dy_s5, ab, bd_b, bd_c, s5_d,
                                                                     (s5_er, s5_ei), dproj0)
    p_bufs = _copies_wait("rs1_pair_wait", p_send, p_recv, p_bufs, rs1_plan, dproj0)
    rs1_pairs = [_rs_pair_add("rs_pair_add_" + nm, g, t, kd, ss, c_arr)
                 for nm, g, t, kd, ss in zip(rs1_names, p_bufs[:4], p_bufs[4:], rs1_kinds, rs1_shapes)]
    rs1_chip_plan = _rs_chip_plan(4)
    rs1_land2 = [_empty_hbm((3, r // 2, cc), BF16) for (r, cc) in rs1_shapes]
    c_send, c_recv, c_bufs, dproj0 = _copies_start("rs1_chip_start", [p[1] for p in rs1_pairs] + rs1_land2, 12,
                                                   rs1_chip_plan, dproj0)
    g_wab = _matmul("d_w_in_ab", hn0, dproj0, TN, D_MODEL, IN_AB, rows, tm=1024, tn=1024, tk=tmk)
    rs2_shapes = [w_in_ab.shape[1:]]
    rs2_plan = _rs_pair_plan(["col"], rs2_shapes)
    rs2_land = [_empty_hbm((N_SHARD, rs2_shapes[0][0] // 2, rs2_shapes[0][1]), F32)]
    q_send, q_recv, q_bufs, dproj0 = _copies_start("rs2_pair_start", [g_wab] + rs2_land, N_SHARD, rs2_plan, dproj0)
    dhn0 = _matmul("d_hn0_a", dproj0, wab, NT, tm, D_MODEL, IN_AB, tm=tm, tn=512, tk=2048,
                   out_shape=jax.ShapeDtypeStruct((rows, D_MODEL), F32))
    q_bufs = _copies_wait("rs2_pair_wait", q_send, q_recv, q_bufs, rs2_plan, dhn0)
    rs2_pair = _rs_pair_add("rs_pair_add_w_in_ab", q_bufs[0], q_bufs[1], "col", rs2_shapes[0], c_arr)
    rs2_chip_plan = _rs_chip_plan(1)
    rs2_land2 = [_empty_hbm((3, rs2_shapes[0][0] // 2, rs2_shapes[0][1]), BF16)]
    r_send, r_recv, r_bufs, dhn0 = _copies_start("rs2_chip_start", [rs2_pair[1]] + rs2_land2, 3, rs2_chip_plan, dhn0)
    if rows > tm:
        dhn0 = _matmul("d_hn0_b", dproj0, wab, NT, rows - tm, D_MODEL, IN_AB, tm=tm, tn=512, tk=2048, a_off=(1, 0),
                       into=(dhn0, 1, 0))
    grad_x, d_meta, d_norm_ab = _rms_bwd_embed(dhn0, h0, norm_ab_w, dh1)
    c_bufs = _copies_wait("rs1_chip_wait", c_send, c_recv, c_bufs, rs1_chip_plan, grad_x)
    grad_x = grad_x[None]
    rs1_halves = [_rs_chip_add("rs_chip_add_" + nm, p[0], t, ss, mine_c)
                  for nm, p, t, ss in zip(rs1_names, rs1_pairs, c_bufs[4:], rs1_shapes)]
    share_plan = _rs_share_plan(rs1_shapes)
    s_send, s_recv, s_bufs, d_meta = _copies_start("rs1_share_start", rs1_halves, len(rs1_halves), share_plan, d_meta)

    d_ab_re = jnp.sum(dar_p, axis=1).reshape(S5_G, S5_P)
    d_ab_im = jnp.sum(dai_p, axis=1).reshape(S5_G, S5_P)
    small_local = [loss_dev, d_meta, d_norm_ab, d_ret_norm.reshape(1, RET_W), d_ab_re, d_ab_im,
                   _bdiag_in_extract(dbr_d), _bdiag_in_extract(dbi_d),
                   _bdiag_out_extract(dcr_d), _bdiag_out_extract(dci_d),
                   jnp.sum(dd_p, axis=1).reshape(1, S5_W), d_norm_c, g_wgate[:GLA_RANK],
                   d_bgate.reshape(1, GLA_QK), d_gla_norm.reshape(1, GLA_W), d_final]
    small_shapes = [a.shape for a in small_local]
    summed_buf = _allreduce_small(_pack(small_local))
    summed = _unpack(summed_buf, small_shapes)
    g_w_out_ab, g_w_in_c, g_w_out_c, g_w_glu = _copies_wait("rs1_share_wait", s_send, s_recv, s_bufs, share_plan,
                                                             summed_buf)
    g_w_in_c = lax.dynamic_slice(g_w_in_c, (0, (wc_cols % 128) * mine), (D_MODEL, wc_cols))
    (loss, g_meta_f, g_norm_ab, g_ret_norm, g_ab_re, g_ab_im, g_bb_re, g_bb_im, g_c_re, g_c_im, g_d,
     g_norm_c_f, g_wgate_f, g_bgate_f, g_gla_norm_f, g_final) = summed
    _, s5_vjp = jax.vjp(_s5_discretize, s5_lam_re[0], s5_lam_im[0], s5_log_dt[0], s5_b_re[0], s5_b_im[0])
    g_lam_re, g_lam_im, g_log_dt, g_b_re, g_b_im = s5_vjp((g_ab_re, g_ab_im, g_bb_re, g_bb_im))

    def take(a, width):
        return lax.dynamic_slice_in_dim(a, mine * width, width, axis=1)

    grads = {
        "meta": take(g_meta_f, q4), "norm_ab_w": g_norm_ab, "ret_norm_w": g_ret_norm,
        "s5_lam_re": g_lam_re[None], "s5_lam_im": g_lam_im[None], "s5_log_dt": g_log_dt[None],
        "s5_b_re": g_b_re[None], "s5_b_im": g_b_im[None], "s5_c_re": g_c_re[None], "s5_c_im": g_c_im[None],
        "s5_d": g_d, "s5_w_glu": g_w_glu[None], "w_out_ab": g_w_out_ab[None], "norm_c_w": take(g_norm_c_f, q4),
        "w_in_c": g_w_in_c[None], "gla_w_gate": take(g_wgate_f, g4)[None], "gla_b_gate": take(g_bgate_f, g4),
        "gla_norm_w": take(g_gla_norm_f, q4), "w_out_c": g_w_out_c[None], "final_norm_w": g_final.reshape(D_MODEL),
    }
    weights = dict(meta=meta, norm_ab_w=norm_ab_w, w_in_ab=w_in_ab, ret_norm_w=ret_norm_w, s5_lam_re=s5_lam_re,
                   s5_lam_im=s5_lam_im, s5_log_dt=s5_log_dt, s5_b_re=s5_b_re, s5_b_im=s5_b_im, s5_c_re=s5_c_re,
                   s5_c_im=s5_c_im, s5_d=s5_d, s5_w_glu=s5_w_glu, w_out_ab=w_out_ab, norm_c_w=norm_c_w,
                   w_in_c=w_in_c, gla_w_gate=gla_w_gate, gla_b_gate=gla_b_gate, gla_norm_w=gla_norm_w,
                   w_out_c=w_out_c, final_norm_w=final_norm_w)
    m_in = dict(meta=m_meta, norm_ab_w=m_norm_ab_w, w_in_ab=m_w_in_ab, ret_norm_w=m_ret_norm_w,
                s5_lam_re=m_s5_lam_re, s5_lam_im=m_s5_lam_im, s5_log_dt=m_s5_log_dt, s5_b_re=m_s5_b_re,
                s5_b_im=m_s5_b_im, s5_c_re=m_s5_c_re, s5_c_im=m_s5_c_im, s5_d=m_s5_d, s5_w_glu=m_s5_w_glu,
                w_out_ab=m_w_out_ab, norm_c_w=m_norm_c_w, w_in_c=m_w_in_c, gla_w_gate=m_gla_w_gate,
                gla_b_gate=m_gla_b_gate, gla_norm_w=m_gla_norm_w, w_out_c=m_w_out_c, final_norm_w=m_final_norm_w)
    v_in = dict(meta=v_meta, norm_ab_w=v_norm_ab_w, w_in_ab=v_w_in_ab, ret_norm_w=v_ret_norm_w,
                s5_lam_re=v_s5_lam_re, s5_lam_im=v_s5_lam_im, s5_log_dt=v_s5_log_dt, s5_b_re=v_s5_b_re,
                s5_b_im=v_s5_b_im, s5_c_re=v_s5_c_re, s5_c_im=v_s5_c_im, s5_d=v_s5_d, s5_w_glu=v_s5_w_glu,
                w_out_ab=v_w_out_ab, norm_c_w=v_norm_c_w, w_in_c=v_w_in_c, gla_w_gate=v_gla_w_gate,
                gla_b_gate=v_gla_b_gate, gla_norm_w=v_gla_norm_w, w_out_c=v_w_out_c, final_norm_w=v_final_norm_w)
    order = list(weights)
    big_names = ["s5_w_glu", "w_out_ab", "w_in_c", "w_out_c", "w_in_ab"]
    small_names = [nm for nm in order if nm not in big_names]
    delta, new_m, new_v = {}, {}, {}

    def big_update(nm):
        shp = weights[nm].shape
        d2, m2, v2 = _adamw("adamw_" + nm, weights[nm][0], grads[nm][0], m_in[nm][0], v_in[nm][0])
        delta[nm], new_m[nm], new_v[nm] = d2.reshape(shp), m2.reshape(shp), v2.reshape(shp)

    for nm in big_names[:-1]:
        big_update(nm)
    sshapes = [weights[nm].shape for nm in small_names]
    d2, m2, v2 = _adamw("adamw_small", _pack([weights[nm] for nm in small_names]),
                        _pack([grads[nm] for nm in small_names]), _pack([m_in[nm] for nm in small_names]),
                        _pack([v_in[nm] for nm in small_names]))
    for nm, dd, mm, vv in zip(small_names, _unpack(d2, sshapes), _unpack(m2, sshapes), _unpack(v2, sshapes)):
        delta[nm], new_m[nm], new_v[nm] = dd, mm, vv
    r_bufs = _copies_wait("rs2_chip_wait", r_send, r_recv, r_bufs, rs2_chip_plan,
                          [v2] + [new_v[nm] for nm in big_names[:-1]])
    rs2_half = _rs_chip_add("rs_chip_add_w_in_ab", rs2_pair[0], r_bufs[1], rs2_shapes[0], mine_c)
    grads["w_in_ab"] = _rs_pair_share("rs2_pair_share", [rs2_half], rs2_shapes)[0][None]
    big_update("w_in_ab")
    grads = {nm: grads[nm].reshape(weights[nm].shape) for nm in order}
    return (loss.reshape(()), grad_x, *[grads[nm] for nm in order], *[delta[nm] for nm in order],
            *[new_m[nm] for nm in order], *[new_v[nm] for nm in order])
```
